```python
import jax, jax.numpy as jnp
from jax import lax
import numpy as np

D_MODEL = 1024
BATCH = 8
SEQ = 16384
DEPTH = 2

D_FF = 2816
N_SUB = 3
W_A = D_MODEL // 2
W_B = D_MODEL // 2
H_A = 8
DH_A = W_A // H_A
G_B = 8
DG_B = W_B // G_B
CONV_A = 4
CONV_B = 31
LRU_C = 8.0
W_C = D_MODEL
H_C = 8
DH_C = W_C // H_C
CHUNK = 128
EPS = 1e-6

kernel_name = "hybrid_rglru_conformer_gmlp_block"


def _rmsnorm(x, g):
    x32 = x.astype(jnp.float32)
    y = x32 * lax.rsqrt(jnp.mean(x32 * x32, axis=-1, keepdims=True) + EPS)
    return (y * g.astype(jnp.float32)).astype(x.dtype)


def _layernorm(x, g, b):
    x32 = x.astype(jnp.float32)
    mu = jnp.mean(x32, axis=-1, keepdims=True)
    var = jnp.mean(jnp.square(x32 - mu), axis=-1, keepdims=True)
    y = (x32 - mu) * lax.rsqrt(var + EPS)
    return (y * g.astype(jnp.float32) + b.astype(jnp.float32)).astype(x.dtype)


def _causal_dwconv(x, w, b):
    k = w.shape[0]
    y = lax.conv_general_dilated(
        x, w[:, None, :].astype(x.dtype), window_strides=(1,), padding=[(k - 1, 0)],
        dimension_numbers=("NWC", "WIO", "NWC"), feature_group_count=x.shape[-1])
    return y + b


def _swiglu(h, w13, w2):
    g, u = jnp.split(h @ w13, 2, axis=-1)
    return (jax.nn.silu(g) * u) @ w2


def _rg_lru(xr, gate_w, gate_b, lam):
    bsz, s, _ = xr.shape
    xh = xr.reshape(bsz, s, H_A, DH_A)
    gates = (jnp.einsum("bshd,hde->bshe", xh, gate_w) + gate_b).astype(jnp.float32)
    r, i = jnp.split(jax.nn.sigmoid(gates), 2, axis=-1)
    r = r.reshape(bsz, s, W_A)
    i = i.reshape(bsz, s, W_A)
    log_a = LRU_C * r * jax.nn.log_sigmoid(lam.astype(jnp.float32))
    a = jnp.exp(log_a)
    u = jnp.sqrt(-jnp.expm1(2.0 * log_a)) * (i * xr.astype(jnp.float32))

    def combine(left, right):
        a1, b1 = left
        a2, b2 = right
        return a1 * a2, a2 * b1 + b2

    _, h = lax.associative_scan(combine, (a, u), axis=1)
    return h.astype(xr.dtype)


def _mixer_ab(h, w_in, a_conv_w, a_conv_b, a_gate_w, a_gate_b, a_lam,
              b_conv_w, b_conv_b, b_norm_g, b_norm_b, w_out):
    z = h @ w_in
    a_gate, a_x, b_val, b_gate = jnp.split(z, [W_A, 2 * W_A, 2 * W_A + W_B], axis=-1)
    a_x = _causal_dwconv(a_x, a_conv_w, a_conv_b)
    y_a = _rg_lru(a_x, a_gate_w, a_gate_b, a_lam) * jax.nn.gelu(a_gate)
    v = b_val * jax.nn.sigmoid(b_gate)
    v = _causal_dwconv(v, b_conv_w, b_conv_b)
    bsz, s, _ = v.shape
    v = _layernorm(v.reshape(bsz, s, G_B, DG_B), b_norm_g.reshape(G_B, DG_B),
                   b_norm_b.reshape(G_B, DG_B)).reshape(bsz, s, W_B)
    y_b = jax.nn.silu(v)
    return jnp.concatenate([y_a, y_b], axis=-1) @ w_out


def _mixer_c(h, w_in, b_in, norm_g, norm_b, w_s, b_s, w_out):
    z = jax.nn.gelu(h @ w_in + b_in)
    u, v = jnp.split(z, 2, axis=-1)
    v = _layernorm(v, norm_g, norm_b)
    bsz, s, _ = v.shape
    vc = v.reshape(bsz, s // CHUNK, CHUNK, H_C, DH_C)
    mask = jnp.tril(jnp.ones((CHUNK, CHUNK), dtype=bool))
    ws = jnp.where(mask, w_s, jnp.zeros_like(w_s)).astype(v.dtype)
    mixed = jnp.einsum("hts,bnshd->bnthd", ws, vc) + jnp.transpose(b_s)[:, :, None]
    return (u * mixed.reshape(bsz, s, W_C)) @ w_out


def _sublayer(x, fn, pre_g, post_g, shift, scale, gate, res_w):
    h = _rmsnorm(x, pre_g) * (1.0 + scale[:, None, :]) + shift[:, None, :]
    y = _rmsnorm(fn(h), post_g)
    return x + res_w * (1.0 + gate[:, None, :]) * y


def _fwd_setup_inputs(seed: int = 0) -> dict:
    key = jax.random.key(seed)
    ks = jax.random.split(key, 32)
    ne = (DEPTH + 1) // 2
    no = DEPTH // 2
    f32 = jnp.float32

    def nrm(k, shape, scale):
        return jax.random.normal(k, shape, f32) * scale

    u_lam = jax.random.uniform(ks[14], (ne, W_A), f32, minval=0.9, maxval=0.999)
    sa = u_lam ** (1.0 / LRU_C)
    a_lam = jnp.log(sa) - jnp.log1p(-sa)
    return {
        "x": nrm(ks[0], (BATCH, SEQ, D_MODEL), 1.0),
        "c": nrm(ks[1], (BATCH, D_MODEL), 1.0),
        "ada_w": nrm(ks[2], (DEPTH, D_MODEL, N_SUB * 3 * D_MODEL), 0.1 * D_MODEL ** -0.5),
        "ada_b": nrm(ks[3], (DEPTH, N_SUB * 3 * D_MODEL), 0.01),
        "norm_pre": 1.0 + nrm(ks[4], (DEPTH, N_SUB, D_MODEL), 0.02),
        "norm_post": 1.0 + nrm(ks[5], (DEPTH, N_SUB, D_MODEL), 0.02),
        "ffn_w13": nrm(ks[6], (DEPTH, 2, D_MODEL, 2 * D_FF), D_MODEL ** -0.5),
        "ffn_w2": nrm(ks[7], (DEPTH, 2, D_FF, D_MODEL), D_FF ** -0.5),
        "ab_w_in": nrm(ks[8], (ne, D_MODEL, 2 * W_A + 2 * W_B), D_MODEL ** -0.5),
        "a_conv_w": nrm(ks[9], (ne, CONV_A, W_A), CONV_A ** -0.5),
        "a_conv_b": nrm(ks[10], (ne, W_A), 0.01),
        "a_gate_w": nrm(ks[11], (ne, H_A, DH_A, 2 * DH_A), DH_A ** -0.5),
        "a_gate_b": nrm(ks[12], (ne, H_A, 2 * DH_A), 0.01),
        "a_lam": a_lam,
        "b_conv_w": nrm(ks[15], (ne, CONV_B, W_B), CONV_B ** -0.5),
        "b_conv_b": nrm(ks[16], (ne, W_B), 0.01),
        "b_norm_g": 1.0 + nrm(ks[17], (ne, W_B), 0.02),
        "b_norm_b": nrm(ks[18], (ne, W_B), 0.01),
        "ab_w_out": nrm(ks[19], (ne, W_A + W_B, D_MODEL), (W_A + W_B) ** -0.5),
        "c_w_in": nrm(ks[20], (no, D_MODEL, 2 * W_C), D_MODEL ** -0.5),
        "c_b_in": nrm(ks[21], (no, 2 * W_C), 0.01),
        "c_norm_g": 1.0 + nrm(ks[22], (no, W_C), 0.02),
        "c_norm_b": nrm(ks[23], (no, W_C), 0.01),
        "c_w_s": nrm(ks[24], (no, H_C, CHUNK, CHUNK), 0.5 * CHUNK ** -0.5),
        "c_b_s": 1.0 + nrm(ks[25], (no, H_C, CHUNK), 0.01),
        "c_w_out": nrm(ks[26], (no, W_C, D_MODEL), W_C ** -0.5),
    }


def _fwd_reference(x, c, ada_w, ada_b, norm_pre, norm_post, ffn_w13, ffn_w2,
              ab_w_in, a_conv_w, a_conv_b, a_gate_w, a_gate_b, a_lam,
              b_conv_w, b_conv_b, b_norm_g, b_norm_b, ab_w_out,
              c_w_in, c_b_in, c_norm_g, c_norm_b, c_w_s, c_b_s, c_w_out):
    bsz = x.shape[0]
    c_act = jax.nn.silu(c)
    for l in range(DEPTH):
        mod = (c_act @ ada_w[l] + ada_b[l]).reshape(bsz, N_SUB, 3, D_MODEL)

        def ffn_pre(h, l=l):
            return _swiglu(h, ffn_w13[l, 0], ffn_w2[l, 0])

        def ffn_post(h, l=l):
            return _swiglu(h, ffn_w13[l, 1], ffn_w2[l, 1])

        if l % 2 == 0:
            k = l // 2

            def mixer(h, k=k):
                return _mixer_ab(h, ab_w_in[k], a_conv_w[k], a_conv_b[k], a_gate_w[k],
                                 a_gate_b[k], a_lam[k], b_conv_w[k], b_conv_b[k],
                                 b_norm_g[k], b_norm_b[k], ab_w_out[k])
        else:
            k = l // 2

            def mixer(h, k=k):
                return _mixer_c(h, c_w_in[k], c_b_in[k], c_norm_g[k], c_norm_b[k],
                                c_w_s[k], c_b_s[k], c_w_out[k])

        x = _sublayer(x, ffn_pre, norm_pre[l, 0], norm_post[l, 0],
                      mod[:, 0, 0], mod[:, 0, 1], mod[:, 0, 2], 0.5)
        x = _sublayer(x, mixer, norm_pre[l, 1], norm_post[l, 1],
                      mod[:, 1, 0], mod[:, 1, 1], mod[:, 1, 2], 1.0)
        x = _sublayer(x, ffn_post, norm_pre[l, 2], norm_post[l, 2],
                      mod[:, 2, 0], mod[:, 2, 1], mod[:, 2, 2], 0.5)
    return x


import jax as _jax
import jax.numpy as _jnp

TWIN_FORMAT = 'train_step'
FWD_PARAMS = ['x', 'c', 'ada_w', 'ada_b', 'norm_pre', 'norm_post', 'ffn_w13', 'ffn_w2', 'ab_w_in', 'a_conv_w', 'a_conv_b', 'a_gate_w', 'a_gate_b', 'a_lam', 'b_conv_w', 'b_conv_b', 'b_norm_g', 'b_norm_b', 'ab_w_out', 'c_w_in', 'c_b_in', 'c_norm_g', 'c_norm_b', 'c_w_s', 'c_b_s', 'c_w_out']
TWIN_WEIGHTS = ['ada_w', 'ada_b', 'norm_pre', 'norm_post', 'ffn_w13', 'ffn_w2', 'ab_w_in', 'a_conv_w', 'a_conv_b', 'a_gate_w', 'a_gate_b', 'a_lam', 'b_conv_w', 'b_conv_b', 'b_norm_g', 'b_norm_b', 'ab_w_out', 'c_w_in', 'c_b_in', 'c_norm_g', 'c_norm_b', 'c_w_s', 'c_b_s', 'c_w_out']
TWIN_DIFF_INPUT = 'x'
TWIN_INPUTS = ['x', 'c', 'ada_w', 'ada_b', 'norm_pre', 'norm_post', 'ffn_w13', 'ffn_w2', 'ab_w_in', 'a_conv_w', 'a_conv_b', 'a_gate_w', 'a_gate_b', 'a_lam', 'b_conv_w', 'b_conv_b', 'b_norm_g', 'b_norm_b', 'ab_w_out', 'c_w_in', 'c_b_in', 'c_norm_g', 'c_norm_b', 'c_w_s', 'c_b_s', 'c_w_out', 'loss_target', 'm_ada_w', 'm_ada_b', 'm_norm_pre', 'm_norm_post', 'm_ffn_w13', 'm_ffn_w2', 'm_ab_w_in', 'm_a_conv_w', 'm_a_conv_b', 'm_a_gate_w', 'm_a_gate_b', 'm_a_lam', 'm_b_conv_w', 'm_b_conv_b', 'm_b_norm_g', 'm_b_norm_b', 'm_ab_w_out', 'm_c_w_in', 'm_c_b_in', 'm_c_norm_g', 'm_c_norm_b', 'm_c_w_s', 'm_c_b_s', 'm_c_w_out', 'v_ada_w', 'v_ada_b', 'v_norm_pre', 'v_norm_post', 'v_ffn_w13', 'v_ffn_w2', 'v_ab_w_in', 'v_a_conv_w', 'v_a_conv_b', 'v_a_gate_w', 'v_a_gate_b', 'v_a_lam', 'v_b_conv_w', 'v_b_conv_b', 'v_b_norm_g', 'v_b_norm_b', 'v_ab_w_out', 'v_c_w_in', 'v_c_b_in', 'v_c_norm_g', 'v_c_norm_b', 'v_c_w_s', 'v_c_b_s', 'v_c_w_out']
TWIN_OUTPUTS = ['loss', 'grad_x', 'grad_ada_w', 'grad_ada_b', 'grad_norm_pre', 'grad_norm_post', 'grad_ffn_w13', 'grad_ffn_w2', 'grad_ab_w_in', 'grad_a_conv_w', 'grad_a_conv_b', 'grad_a_gate_w', 'grad_a_gate_b', 'grad_a_lam', 'grad_b_conv_w', 'grad_b_conv_b', 'grad_b_norm_g', 'grad_b_norm_b', 'grad_ab_w_out', 'grad_c_w_in', 'grad_c_b_in', 'grad_c_norm_g', 'grad_c_norm_b', 'grad_c_w_s', 'grad_c_b_s', 'grad_c_w_out', 'delta_ada_w', 'delta_ada_b', 'delta_norm_pre', 'delta_norm_post', 'delta_ffn_w13', 'delta_ffn_w2', 'delta_ab_w_in', 'delta_a_conv_w', 'delta_a_conv_b', 'delta_a_gate_w', 'delta_a_gate_b', 'delta_a_lam', 'delta_b_conv_w', 'delta_b_conv_b', 'delta_b_norm_g', 'delta_b_norm_b', 'delta_ab_w_out', 'delta_c_w_in', 'delta_c_b_in', 'delta_c_norm_g', 'delta_c_norm_b', 'delta_c_w_s', 'delta_c_b_s', 'delta_c_w_out', 'new_m_ada_w', 'new_m_ada_b', 'new_m_norm_pre', 'new_m_norm_post', 'new_m_ffn_w13', 'new_m_ffn_w2', 'new_m_ab_w_in', 'new_m_a_conv_w', 'new_m_a_conv_b', 'new_m_a_gate_w', 'new_m_a_gate_b', 'new_m_a_lam', 'new_m_b_conv_w', 'new_m_b_conv_b', 'new_m_b_norm_g', 'new_m_b_norm_b', 'new_m_ab_w_out', 'new_m_c_w_in', 'new_m_c_b_in', 'new_m_c_norm_g', 'new_m_c_norm_b', 'new_m_c_w_s', 'new_m_c_b_s', 'new_m_c_w_out', 'new_v_ada_w', 'new_v_ada_b', 'new_v_norm_pre', 'new_v_norm_post', 'new_v_ffn_w13', 'new_v_ffn_w2', 'new_v_ab_w_in', 'new_v_a_conv_w', 'new_v_a_conv_b', 'new_v_a_gate_w', 'new_v_a_gate_b', 'new_v_a_lam', 'new_v_b_conv_w', 'new_v_b_conv_b', 'new_v_b_norm_g', 'new_v_b_norm_b', 'new_v_ab_w_out', 'new_v_c_w_in', 'new_v_c_b_in', 'new_v_c_norm_g', 'new_v_c_norm_b', 'new_v_c_w_s', 'new_v_c_b_s', 'new_v_c_w_out']
TWIN_LEAF_KINDS = {'loss': 'loss', 'grad_x': 'grad_x', 'grad_ada_w': 'grad_w', 'grad_ada_b': 'grad_w', 'grad_norm_pre': 'grad_w', 'grad_norm_post': 'grad_w', 'grad_ffn_w13': 'grad_w', 'grad_ffn_w2': 'grad_w', 'grad_ab_w_in': 'grad_w', 'grad_a_conv_w': 'grad_w', 'grad_a_conv_b': 'grad_w', 'grad_a_gate_w': 'grad_w', 'grad_a_gate_b': 'grad_w', 'grad_a_lam': 'grad_w', 'grad_b_conv_w': 'grad_w', 'grad_b_conv_b': 'grad_w', 'grad_b_norm_g': 'grad_w', 'grad_b_norm_b': 'grad_w', 'grad_ab_w_out': 'grad_w', 'grad_c_w_in': 'grad_w', 'grad_c_b_in': 'grad_w', 'grad_c_norm_g': 'grad_w', 'grad_c_norm_b': 'grad_w', 'grad_c_w_s': 'grad_w', 'grad_c_b_s': 'grad_w', 'grad_c_w_out': 'grad_w', 'delta_ada_w': 'delta_w', 'delta_ada_b': 'delta_w', 'delta_norm_pre': 'delta_w', 'delta_norm_post': 'delta_w', 'delta_ffn_w13': 'delta_w', 'delta_ffn_w2': 'delta_w', 'delta_ab_w_in': 'delta_w', 'delta_a_conv_w': 'delta_w', 'delta_a_conv_b': 'delta_w', 'delta_a_gate_w': 'delta_w', 'delta_a_gate_b': 'delta_w', 'delta_a_lam': 'delta_w', 'delta_b_conv_w': 'delta_w', 'delta_b_conv_b': 'delta_w', 'delta_b_norm_g': 'delta_w', 'delta_b_norm_b': 'delta_w', 'delta_ab_w_out': 'delta_w', 'delta_c_w_in': 'delta_w', 'delta_c_b_in': 'delta_w', 'delta_c_norm_g': 'delta_w', 'delta_c_norm_b': 'delta_w', 'delta_c_w_s': 'delta_w', 'delta_c_b_s': 'delta_w', 'delta_c_w_out': 'delta_w', 'new_m_ada_w': 'new_m', 'new_m_ada_b': 'new_m', 'new_m_norm_pre': 'new_m', 'new_m_norm_post': 'new_m', 'new_m_ffn_w13': 'new_m', 'new_m_ffn_w2': 'new_m', 'new_m_ab_w_in': 'new_m', 'new_m_a_conv_w': 'new_m', 'new_m_a_conv_b': 'new_m', 'new_m_a_gate_w': 'new_m', 'new_m_a_gate_b': 'new_m', 'new_m_a_lam': 'new_m', 'new_m_b_conv_w': 'new_m', 'new_m_b_conv_b': 'new_m', 'new_m_b_norm_g': 'new_m', 'new_m_b_norm_b': 'new_m', 'new_m_ab_w_out': 'new_m', 'new_m_c_w_in': 'new_m', 'new_m_c_b_in': 'new_m', 'new_m_c_norm_g': 'new_m', 'new_m_c_norm_b': 'new_m', 'new_m_c_w_s': 'new_m', 'new_m_c_b_s': 'new_m', 'new_m_c_w_out': 'new_m', 'new_v_ada_w': 'new_v', 'new_v_ada_b': 'new_v', 'new_v_norm_pre': 'new_v', 'new_v_norm_post': 'new_v', 'new_v_ffn_w13': 'new_v', 'new_v_ffn_w2': 'new_v', 'new_v_ab_w_in': 'new_v', 'new_v_a_conv_w': 'new_v', 'new_v_a_conv_b': 'new_v', 'new_v_a_gate_w': 'new_v', 'new_v_a_gate_b': 'new_v', 'new_v_a_lam': 'new_v', 'new_v_b_conv_w': 'new_v', 'new_v_b_conv_b': 'new_v', 'new_v_b_norm_g': 'new_v', 'new_v_b_norm_b': 'new_v', 'new_v_ab_w_out': 'new_v', 'new_v_c_w_in': 'new_v', 'new_v_c_b_in': 'new_v', 'new_v_c_norm_g': 'new_v', 'new_v_c_norm_b': 'new_v', 'new_v_c_w_s': 'new_v', 'new_v_c_b_s': 'new_v', 'new_v_c_w_out': 'new_v'}


def _forward(args):
    return _fwd_reference(*[args[k] for k in FWD_PARAMS])


def _output_shape():
    def fwd():
        inp = _fwd_setup_inputs(0)
        return _fwd_reference(*[inp[k] for k in FWD_PARAMS])
    out = _jax.eval_shape(fwd)
    return out.shape, out.dtype

N_MICROBATCH = 1
ADAM_LR = 0.001
ADAM_B1 = 0.9
ADAM_B2 = 0.999
ADAM_EPS = 1e-08
ADAM_WD = 0.01
ADAM_STEP = 10
PER_EXAMPLE_BATCH_AXIS = {'x': 0, 'c': 0, 'loss_target': 0}
SHARED_INPUTS = []
_WEIGHT_DTYPES = {'ada_w': _jnp.float32, 'ada_b': _jnp.float32, 'norm_pre': _jnp.float32, 'norm_post': _jnp.float32, 'ffn_w13': _jnp.float32, 'ffn_w2': _jnp.float32, 'ab_w_in': _jnp.float32, 'a_conv_w': _jnp.float32, 'a_conv_b': _jnp.float32, 'a_gate_w': _jnp.float32, 'a_gate_b': _jnp.float32, 'a_lam': _jnp.float32, 'b_conv_w': _jnp.float32, 'b_conv_b': _jnp.float32, 'b_norm_g': _jnp.float32, 'b_norm_b': _jnp.float32, 'ab_w_out': _jnp.float32, 'c_w_in': _jnp.float32, 'c_b_in': _jnp.float32, 'c_norm_g': _jnp.float32, 'c_norm_b': _jnp.float32, 'c_w_s': _jnp.float32, 'c_b_s': _jnp.float32, 'c_w_out': _jnp.float32}
MOMENT_SCALE = {'ada_w': 1.681331e+01, 'ada_b': 5.445985e+01, 'norm_pre': 3.304098e+00, 'norm_post': 8.151304e+01, 'ffn_w13': 1.143030e+00, 'ffn_w2': 2.502663e+00, 'ab_w_in': 2.202030e+00, 'a_conv_w': 1.205478e+01, 'a_conv_b': 1.350339e+02, 'a_gate_w': 6.857567e+00, 'a_gate_b': 3.430954e+00, 'a_lam': 4.253011e+00, 'b_conv_w': 3.987806e+00, 'b_conv_b': 8.025111e+01, 'b_norm_g': 3.488772e+01, 'b_norm_b': 5.101250e+01, 'ab_w_out': 1.750600e+01, 'c_w_in': 4.458488e+00, 'c_b_in': 2.066211e+01, 'c_norm_g': 2.607636e-01, 'c_norm_b': 3.217099e-01, 'c_w_s': 5.040194e-01, 'c_b_s': 7.846662e-01, 'c_w_out': 1.802140e+01}


def _to_microbatches(a, axis):
    t = _jnp.moveaxis(a, axis, 0)
    t = t.reshape((N_MICROBATCH, t.shape[0] // N_MICROBATCH) + t.shape[1:])
    return _jnp.moveaxis(t, 1, axis + 1)


def setup_inputs(seed: int = 0) -> dict:
    inp = _fwd_setup_inputs(seed)
    key = _jax.random.fold_in(_jax.random.key(seed), 7919)
    shape, _ = _output_shape()
    out = dict(inp)
    out["loss_target"] = _jax.random.normal(_jax.random.fold_in(key, 0), shape, _jnp.float32)
    for i, name in enumerate(TWIN_WEIGHTS):
        w = inp[name].astype(_jnp.float32)
        if MOMENT_SCALE is None:
            s = _jnp.sqrt(_jnp.mean(_jnp.square(w)) + 1e-30)
        else:
            s = MOMENT_SCALE[name]
        km, kv = _jax.random.split(_jax.random.fold_in(key, i + 1))
        out[name] = w
        out["m_" + name] = s * _jax.random.normal(km, w.shape, _jnp.float32)
        out["v_" + name] = (s * s) * _jax.random.uniform(kv, w.shape, _jnp.float32, 0.5, 1.5)
    if N_MICROBATCH > 1:
        for name, axis in PER_EXAMPLE_BATCH_AXIS.items():
            out[name] = _to_microbatches(out[name], axis)
    return {'x': out['x'], 'c': out['c'], 'ada_w': out['ada_w'], 'ada_b': out['ada_b'], 'norm_pre': out['norm_pre'], 'norm_post': out['norm_post'], 'ffn_w13': out['ffn_w13'], 'ffn_w2': out['ffn_w2'], 'ab_w_in': out['ab_w_in'], 'a_conv_w': out['a_conv_w'], 'a_conv_b': out['a_conv_b'], 'a_gate_w': out['a_gate_w'], 'a_gate_b': out['a_gate_b'], 'a_lam': out['a_lam'], 'b_conv_w': out['b_conv_w'], 'b_conv_b': out['b_conv_b'], 'b_norm_g': out['b_norm_g'], 'b_norm_b': out['b_norm_b'], 'ab_w_out': out['ab_w_out'], 'c_w_in': out['c_w_in'], 'c_b_in': out['c_b_in'], 'c_norm_g': out['c_norm_g'], 'c_norm_b': out['c_norm_b'], 'c_w_s': out['c_w_s'], 'c_b_s': out['c_b_s'], 'c_w_out': out['c_w_out'], 'loss_target': out['loss_target'], 'm_ada_w': out['m_ada_w'], 'm_ada_b': out['m_ada_b'], 'm_norm_pre': out['m_norm_pre'], 'm_norm_post': out['m_norm_post'], 'm_ffn_w13': out['m_ffn_w13'], 'm_ffn_w2': out['m_ffn_w2'], 'm_ab_w_in': out['m_ab_w_in'], 'm_a_conv_w': out['m_a_conv_w'], 'm_a_conv_b': out['m_a_conv_b'], 'm_a_gate_w': out['m_a_gate_w'], 'm_a_gate_b': out['m_a_gate_b'], 'm_a_lam': out['m_a_lam'], 'm_b_conv_w': out['m_b_conv_w'], 'm_b_conv_b': out['m_b_conv_b'], 'm_b_norm_g': out['m_b_norm_g'], 'm_b_norm_b': out['m_b_norm_b'], 'm_ab_w_out': out['m_ab_w_out'], 'm_c_w_in': out['m_c_w_in'], 'm_c_b_in': out['m_c_b_in'], 'm_c_norm_g': out['m_c_norm_g'], 'm_c_norm_b': out['m_c_norm_b'], 'm_c_w_s': out['m_c_w_s'], 'm_c_b_s': out['m_c_b_s'], 'm_c_w_out': out['m_c_w_out'], 'v_ada_w': out['v_ada_w'], 'v_ada_b': out['v_ada_b'], 'v_norm_pre': out['v_norm_pre'], 'v_norm_post': out['v_norm_post'], 'v_ffn_w13': out['v_ffn_w13'], 'v_ffn_w2': out['v_ffn_w2'], 'v_ab_w_in': out['v_ab_w_in'], 'v_a_conv_w': out['v_a_conv_w'], 'v_a_conv_b': out['v_a_conv_b'], 'v_a_gate_w': out['v_a_gate_w'], 'v_a_gate_b': out['v_a_gate_b'], 'v_a_lam': out['v_a_lam'], 'v_b_conv_w': out['v_b_conv_w'], 'v_b_conv_b': out['v_b_conv_b'], 'v_b_norm_g': out['v_b_norm_g'], 'v_b_norm_b': out['v_b_norm_b'], 'v_ab_w_out': out['v_ab_w_out'], 'v_c_w_in': out['v_c_w_in'], 'v_c_b_in': out['v_c_b_in'], 'v_c_norm_g': out['v_c_norm_g'], 'v_c_norm_b': out['v_c_norm_b'], 'v_c_w_s': out['v_c_w_s'], 'v_c_b_s': out['v_c_b_s'], 'v_c_w_out': out['v_c_w_out']}


def _loss(weights, diff, rest, loss_target):
    with _jax.named_scope("forward"):
        args = {**rest, TWIN_DIFF_INPUT: diff, **{k: w.astype(_WEIGHT_DTYPES[k]) for k, w in weights.items()}}
        y = _forward(args)
    with _jax.named_scope("loss_head"):
        err = _jnp.square(y.astype(_jnp.float32) - loss_target)
        return 0.5 * _jnp.sum(_jnp.mean(err, axis=-1)) if err.ndim else 0.5 * err


def _adamw(w, g, m, v):
    m = ADAM_B1 * m + (1.0 - ADAM_B1) * g
    v = ADAM_B2 * v + (1.0 - ADAM_B2) * _jnp.square(g)
    m_hat = m / (1.0 - ADAM_B1 ** ADAM_STEP)
    v_hat = v / (1.0 - ADAM_B2 ** ADAM_STEP)
    delta = -ADAM_LR * (m_hat / (_jnp.sqrt(v_hat) + ADAM_EPS) + ADAM_WD * w)
    return delta, m, v


def reference(x, c, ada_w, ada_b, norm_pre, norm_post, ffn_w13, ffn_w2, ab_w_in, a_conv_w, a_conv_b, a_gate_w, a_gate_b, a_lam, b_conv_w, b_conv_b, b_norm_g, b_norm_b, ab_w_out, c_w_in, c_b_in, c_norm_g, c_norm_b, c_w_s, c_b_s, c_w_out, loss_target, m_ada_w, m_ada_b, m_norm_pre, m_norm_post, m_ffn_w13, m_ffn_w2, m_ab_w_in, m_a_conv_w, m_a_conv_b, m_a_gate_w, m_a_gate_b, m_a_lam, m_b_conv_w, m_b_conv_b, m_b_norm_g, m_b_norm_b, m_ab_w_out, m_c_w_in, m_c_b_in, m_c_norm_g, m_c_norm_b, m_c_w_s, m_c_b_s, m_c_w_out, v_ada_w, v_ada_b, v_norm_pre, v_norm_post, v_ffn_w13, v_ffn_w2, v_ab_w_in, v_a_conv_w, v_a_conv_b, v_a_gate_w, v_a_gate_b, v_a_lam, v_b_conv_w, v_b_conv_b, v_b_norm_g, v_b_norm_b, v_ab_w_out, v_c_w_in, v_c_b_in, v_c_norm_g, v_c_norm_b, v_c_w_s, v_c_b_s, v_c_w_out):
    given = dict(x=x, c=c, ada_w=ada_w, ada_b=ada_b, norm_pre=norm_pre, norm_post=norm_post, ffn_w13=ffn_w13, ffn_w2=ffn_w2, ab_w_in=ab_w_in, a_conv_w=a_conv_w, a_conv_b=a_conv_b, a_gate_w=a_gate_w, a_gate_b=a_gate_b, a_lam=a_lam, b_conv_w=b_conv_w, b_conv_b=b_conv_b, b_norm_g=b_norm_g, b_norm_b=b_norm_b, ab_w_out=ab_w_out, c_w_in=c_w_in, c_b_in=c_b_in, c_norm_g=c_norm_g, c_norm_b=c_norm_b, c_w_s=c_w_s, c_b_s=c_b_s, c_w_out=c_w_out, loss_target=loss_target, m_ada_w=m_ada_w, m_ada_b=m_ada_b, m_norm_pre=m_norm_pre, m_norm_post=m_norm_post, m_ffn_w13=m_ffn_w13, m_ffn_w2=m_ffn_w2, m_ab_w_in=m_ab_w_in, m_a_conv_w=m_a_conv_w, m_a_conv_b=m_a_conv_b, m_a_gate_w=m_a_gate_w, m_a_gate_b=m_a_gate_b, m_a_lam=m_a_lam, m_b_conv_w=m_b_conv_w, m_b_conv_b=m_b_conv_b, m_b_norm_g=m_b_norm_g, m_b_norm_b=m_b_norm_b, m_ab_w_out=m_ab_w_out, m_c_w_in=m_c_w_in, m_c_b_in=m_c_b_in, m_c_norm_g=m_c_norm_g, m_c_norm_b=m_c_norm_b, m_c_w_s=m_c_w_s, m_c_b_s=m_c_b_s, m_c_w_out=m_c_w_out, v_ada_w=v_ada_w, v_ada_b=v_ada_b, v_norm_pre=v_norm_pre, v_norm_post=v_norm_post, v_ffn_w13=v_ffn_w13, v_ffn_w2=v_ffn_w2, v_ab_w_in=v_ab_w_in, v_a_conv_w=v_a_conv_w, v_a_conv_b=v_a_conv_b, v_a_gate_w=v_a_gate_w, v_a_gate_b=v_a_gate_b, v_a_lam=v_a_lam, v_b_conv_w=v_b_conv_w, v_b_conv_b=v_b_conv_b, v_b_norm_g=v_b_norm_g, v_b_norm_b=v_b_norm_b, v_ab_w_out=v_ab_w_out, v_c_w_in=v_c_w_in, v_c_b_in=v_c_b_in, v_c_norm_g=v_c_norm_g, v_c_norm_b=v_c_norm_b, v_c_w_s=v_c_w_s, v_c_b_s=v_c_b_s, v_c_w_out=v_c_w_out)
    weights = {n: given[n] for n in TWIN_WEIGHTS}
    shared = {n: given[n] for n in SHARED_INPUTS}
    per_example = {n: given[n] for n in ['x', 'c']}
    grad_fn = _jax.value_and_grad(_loss, argnums=(0, 1))

    def one_microbatch(ex, loss_target):
        ex = dict(ex)
        diff = ex.pop(TWIN_DIFF_INPUT)
        return grad_fn(weights, diff, {**shared, **ex}, loss_target)

    if N_MICROBATCH == 1:
        loss, (grad_w, grad_x) = one_microbatch(per_example, given["loss_target"])
    else:
        def body(carry, xs):
            loss_sum, grad_sum = carry
            l_k, (gw_k, gx_k) = one_microbatch(xs[0], xs[1])
            with _jax.named_scope("update"):
                return (loss_sum + l_k, _jax.tree.map(_jnp.add, grad_sum, gw_k)), gx_k

        init = (_jnp.zeros((), _jnp.float32), _jax.tree.map(_jnp.zeros_like, weights))
        (loss, grad_w), grad_x = _jax.lax.scan(body, init, (per_example, given["loss_target"]))
    with _jax.named_scope("update"):
        delta_w, new_m, new_v = {}, {}, {}
        for n in TWIN_WEIGHTS:
            delta_w[n], new_m[n], new_v[n] = _adamw(weights[n], grad_w[n], given["m_" + n], given["v_" + n])
    return (loss, grad_x, *[grad_w[n] for n in TWIN_WEIGHTS], *[delta_w[n] for n in TWIN_WEIGHTS],
            *[new_m[n] for n in TWIN_WEIGHTS], *[new_v[n] for n in TWIN_WEIGHTS])
```

```python
import functools
import math

import jax
import jax.numpy as jnp
from jax import lax
from jax.experimental import pallas as pl
from jax.experimental.pallas import tpu as pltpu

F32 = jnp.float32
BF = jnp.bfloat16

N_DEV = 8
D = 1024
EPS = 1e-6
D_FF = 2816
FF_SHARD = 704
FF_PAD = 768
FF_CHUNKS = 4
W2_SHARD = 352
W_A = 512
W_B = 512
CONV_A = 4
CONV_B = 31
HALO_A = 8
HALO_B = 32
LRU_C = 8.0
CHUNK = 128
H_C = 8
ADAM_LR = 0.001
ADAM_B1 = 0.9
ADAM_B2 = 0.999
ADAM_EPS = 1e-08
ADAM_WD = 0.01
ADAM_STEP = 10
VMEM_LIMIT = 56 * 1024 * 1024
GELU_C = math.sqrt(2.0 / math.pi)

TM_FFN = 512
TM_FFN_BWD = 256
TM_MIX = 256
TK_WGRAD = 512


def _params(limit=VMEM_LIMIT):
    return pltpu.CompilerParams(dimension_semantics=("arbitrary",), vmem_limit_bytes=limit)


def _dot(a, b):
    return jnp.dot(a, b, preferred_element_type=F32)


def _dot_nt(a, b):
    return lax.dot_general(a, b, (((1,), (1,)), ((), ())), preferred_element_type=F32)


def _dot_tn(a, b):
    return lax.dot_general(a, b, (((0,), (0,)), ((), ())), preferred_element_type=F32)


def _sigmoid(x):
    return 1.0 / (1.0 + jnp.exp(-x))


def _gelu(x):
    t = jnp.tanh(GELU_C * (x + 0.044715 * x * x * x))
    return 0.5 * x * (1.0 + t), t


def _gelu_grad(x, t):
    return 0.5 * (1.0 + t) + 0.5 * x * (1.0 - t * t) * GELU_C * (1.0 + 3.0 * 0.044715 * x * x)


def _rms(x):
    r = lax.rsqrt(jnp.mean(x * x, axis=-1, keepdims=True) + EPS)
    return x * r, r


def _colsum(x):
    return jnp.sum(x, axis=0, keepdims=True)


def _pre_fwd(x, vec_ref):
    xn, r = _rms(x)
    n = xn * vec_ref[3:4, :]
    h = n * (1.0 + vec_ref[1:2, :]) + vec_ref[0:1, :]
    return h, xn, r, n


def _post_fwd(x, f, vec_ref, res_w):
    fn, _ = _rms(f)
    return x + (res_w * (1.0 + vec_ref[2:3, :])) * (fn * vec_ref[4:5, :])


def _post_bwd(dout, f, vec_ref, acc_ref, res_w):
    fn, r2 = _rms(f)
    post_g = vec_ref[4:5, :]
    dy = dout * (res_w * (1.0 + vec_ref[2:3, :]))
    acc_ref[2:3, :] += _colsum(dout * (res_w * (fn * post_g)))
    acc_ref[4:5, :] += _colsum(dy * fn)
    dfn = dy * post_g
    return r2 * (dfn - fn * jnp.mean(dfn * fn, axis=-1, keepdims=True))


def _pre_bwd(dout, dh, xn, r, n, vec_ref, acc_ref):
    acc_ref[0:1, :] += _colsum(dh)
    acc_ref[1:2, :] += _colsum(dh * n)
    dn = dh * (1.0 + vec_ref[1:2, :])
    acc_ref[3:4, :] += _colsum(dn * xn)
    dxn = dn * vec_ref[3:4, :]
    return dout + r * (dxn - xn * jnp.mean(dxn * xn, axis=-1, keepdims=True))


def _tile(tm, ncol):
    return pl.BlockSpec((tm, ncol), lambda i: (i, 0))


def _full(shape):
    return pl.BlockSpec(shape, lambda i: (0,) * len(shape))


def _any():
    return pl.BlockSpec(memory_space=pl.ANY)


def _load_ffn_weights(w13_hbm, w2_hbm, w13_v, w2_v, sems, f):
    w2_v[:, FF_SHARD:FF_PAD, :] = jnp.zeros((FF_CHUNKS, FF_PAD - FF_SHARD, D), BF)
    copies = [pltpu.make_async_copy(w13_hbm.at[:, f], w13_v, sems.at[0])]
    for j in range(N_DEV):
        copies.append(pltpu.make_async_copy(
            w2_hbm.at[j, f], w2_v.at[j // 2, pl.ds((j % 2) * W2_SHARD, W2_SHARD), :], sems.at[1 + j]))
    for cp in copies:
        cp.start()
    for cp in copies:
        cp.wait()


_FFN_SCRATCH = [pltpu.VMEM((N_DEV, D, FF_PAD), BF), pltpu.VMEM((FF_CHUNKS, FF_PAD, D), BF),
                pltpu.SemaphoreType.DMA((1 + N_DEV,))]


def _ffn_fwd(x, vec, w13g, w2g, f, res_w, name):
    t_len = x.shape[0]
    tm = min(TM_FFN, t_len)

    def body(x_ref, vec_ref, w13_hbm, w2_hbm, xo_ref, f_ref, gu_ref, w13_v, w2_v, sems):
        @pl.when(pl.program_id(0) == 0)
        def _():
            _load_ffn_weights(w13_hbm, w2_hbm, w13_v, w2_v, sems, f)

        x_t = x_ref[...]
        h, _, _, _ = _pre_fwd(x_t, vec_ref)
        hb = h.astype(BF)
        acc = jnp.zeros((tm, D), F32)
        for k in range(FF_CHUNKS):
            g = _dot(hb, w13_v[k])
            u = _dot(hb, w13_v[k + FF_CHUNKS])
            gu_ref[k] = g.astype(BF)
            gu_ref[k + FF_CHUNKS] = u.astype(BF)
            s = g * _sigmoid(g) * u
            acc = acc + _dot(s.astype(BF), w2_v[k])
        f_ref[...] = acc
        xo_ref[...] = _post_fwd(x_t, acc, vec_ref, res_w)

    return pl.pallas_call(
        body, grid=(t_len // tm,),
        in_specs=[_tile(tm, D), _full((8, D)), _any(), _any()],
        out_specs=[_tile(tm, D), _tile(tm, D), pl.BlockSpec((N_DEV, tm, FF_PAD), lambda i: (0, i, 0))],
        out_shape=[jax.ShapeDtypeStruct((t_len, D), F32), jax.ShapeDtypeStruct((t_len, D), F32),
                   jax.ShapeDtypeStruct((N_DEV, t_len, FF_PAD), BF)],
        scratch_shapes=_FFN_SCRATCH, compiler_params=_params(), name=name,
    )(x, vec, w13g, w2g)


def _ffn_bwd(dout, x, fpre, gu, vec, w13g, w2g, f, res_w, name):
    t_len = x.shape[0]
    tm = min(TM_FFN_BWD, t_len)

    def body(dout_ref, x_ref, f_ref, gu_ref, vec_ref, w13_hbm, w2_hbm,
             dx_ref, dgu_ref, s_ref, hb_ref, dfb_ref, acc_ref, w13_v, w2_v, sems):
        @pl.when(pl.program_id(0) == 0)
        def _():
            _load_ffn_weights(w13_hbm, w2_hbm, w13_v, w2_v, sems, f)
            acc_ref[...] = jnp.zeros((8, D), F32)

        dout_t = dout_ref[...]
        df = _post_bwd(dout_t, f_ref[...], vec_ref, acc_ref, res_w)
        dfb = df.astype(BF)
        dfb_ref[...] = dfb
        h, xn, r, n = _pre_fwd(x_ref[...], vec_ref)
        hb_ref[...] = h.astype(BF)
        dh = jnp.zeros((tm, D), F32)
        for k in range(FF_CHUNKS):
            g = gu_ref[k].astype(F32)
            u = gu_ref[k + FF_CHUNKS].astype(F32)
            sig = _sigmoid(g)
            sl = g * sig
            ds = _dot_nt(dfb, w2_v[k])
            dg = (ds * u * (sig * (1.0 + g * (1.0 - sig)))).astype(BF)
            du = (ds * sl).astype(BF)
            s_ref[k] = (sl * u).astype(BF)
            dgu_ref[k] = dg
            dgu_ref[k + FF_CHUNKS] = du
            dh = dh + _dot_nt(dg, w13_v[k]) + _dot_nt(du, w13_v[k + FF_CHUNKS])
        dx_ref[...] = _pre_bwd(dout_t, dh, xn, r, n, vec_ref, acc_ref)

    gu_spec = pl.BlockSpec((N_DEV, tm, FF_PAD), lambda i: (0, i, 0))
    return pl.pallas_call(
        body, grid=(t_len // tm,),
        in_specs=[_tile(tm, D), _tile(tm, D), _tile(tm, D), gu_spec, _full((8, D)), _any(), _any()],
        out_specs=[_tile(tm, D), gu_spec, pl.BlockSpec((FF_CHUNKS, tm, FF_PAD), lambda i: (0, i, 0)),
                   _tile(tm, D), _tile(tm, D), _full((8, D))],
        out_shape=[jax.ShapeDtypeStruct((t_len, D), F32), jax.ShapeDtypeStruct((N_DEV, t_len, FF_PAD), BF),
                   jax.ShapeDtypeStruct((FF_CHUNKS, t_len, FF_PAD), BF), jax.ShapeDtypeStruct((t_len, D), BF),
                   jax.ShapeDtypeStruct((t_len, D), BF), jax.ShapeDtypeStruct((8, D), F32)],
        scratch_shapes=_FFN_SCRATCH, compiler_params=_params(), name=name,
    )(dout, x, fpre, gu, vec, w13g, w2g)


def _wgrad(a, b, j_count, m, n, a_mode, b_mode, out_rows, out_dtype, name):
    t_len = a.shape[-2]
    tk = min(TK_WGRAD, t_len)
    nk = t_len // tk

    def spec(mode, width):
        if mode == "stack":
            return pl.BlockSpec((None, tk, width), lambda j, t: (j, t, 0))
        if mode == "share":
            return pl.BlockSpec((tk, width), lambda j, t: (t, 0))
        return pl.BlockSpec((tk, width), lambda j, t: (t, j))

    def body(a_ref, b_ref, o_ref, acc):
        t = pl.program_id(1)

        @pl.when(t == 0)
        def _():
            acc[...] = jnp.zeros((m, n), F32)

        acc[...] += _dot_tn(a_ref[...], b_ref[...])

        @pl.when(t == nk - 1)
        def _():
            o_ref[...] = acc[0:out_rows, :].astype(out_dtype)

    return pl.pallas_call(
        body, grid=(j_count, nk),
        in_specs=[spec(a_mode, m), spec(b_mode, n)],
        out_specs=pl.BlockSpec((None, out_rows, n), lambda j, t: (j, 0, 0)),
        out_shape=jax.ShapeDtypeStruct((j_count, out_rows, n), out_dtype),
        scratch_shapes=[pltpu.VMEM((m, n), F32)],
        compiler_params=pltpu.CompilerParams(dimension_semantics=("arbitrary", "arbitrary"), vmem_limit_bytes=VMEM_LIMIT),
        name=name,
    )(a, b)


def _c_mask_weights(ws_ref, wsm, wsmt):
    row = lax.broadcasted_iota(jnp.int32, (CHUNK, CHUNK), 0)
    col = lax.broadcasted_iota(jnp.int32, (CHUNK, CHUNK), 1)
    for hh in range(H_C):
        w = jnp.where(row >= col, ws_ref[hh], 0.0)
        wsm[hh] = w.astype(BF)
        if wsmt is not None:
            wsmt[hh] = w.T.astype(BF)


def _c_inner(pre, cvec_ref, wsm, bst_ref, mix_sc, tm):
    z, t = _gelu(pre)
    u = z[:, 0:D]
    v = z[:, D:2 * D]
    mu = jnp.mean(v, axis=-1, keepdims=True)
    vc = v - mu
    rstd = lax.rsqrt(jnp.mean(vc * vc, axis=-1, keepdims=True) + EPS)
    vhat = vc * rstd
    vnb = (vhat * cvec_ref[0:1, :] + cvec_ref[1:2, :]).astype(BF)
    for nn in range(tm // CHUNK):
        for hh in range(H_C):
            rows = slice(CHUNK * nn, CHUNK * (nn + 1))
            cols = slice(CHUNK * hh, CHUNK * (hh + 1))
            mix_sc[rows, cols] = _dot(wsm[hh], vnb[rows, cols]) + bst_ref[:, hh:hh + 1]
    return u, t, rstd, vhat, vnb


def _mixc_fwd(x, vec, w_in, b_in, cvec, ws, bst, w_out, name):
    t_len = x.shape[0]
    tm = min(TM_MIX, t_len)

    def body(x_ref, vec_ref, win_ref, bin_ref, cvec_ref, ws_ref, bst_ref, wout_ref,
             xo_ref, f_ref, pre_ref, wsm, mix_sc):
        @pl.when(pl.program_id(0) == 0)
        def _():
            _c_mask_weights(ws_ref, wsm, None)

        x_t = x_ref[...]
        h, _, _, _ = _pre_fwd(x_t, vec_ref)
        hb = h.astype(BF)
        for j in range(N_DEV):
            cols = slice(256 * j, 256 * (j + 1))
            pre_ref[:, cols] = _dot(hb, win_ref[j]) + bin_ref[:, cols]
        u, _, _, _, _ = _c_inner(pre_ref[...], cvec_ref, wsm, bst_ref, mix_sc, tm)
        fpre = _dot((u * mix_sc[...]).astype(BF), wout_ref[...])
        f_ref[...] = fpre
        xo_ref[...] = _post_fwd(x_t, fpre, vec_ref, 1.0)

    return pl.pallas_call(
        body, grid=(t_len // tm,),
        in_specs=[_tile(tm, D), _full((8, D)), _full((N_DEV, D, 256)), _full((1, 2 * D)), _full((8, D)),
                  _full((H_C, CHUNK, CHUNK)), _full((CHUNK, H_C)), _full((D, D))],
        out_specs=[_tile(tm, D), _tile(tm, D), _tile(tm, 2 * D)],
        out_shape=[jax.ShapeDtypeStruct((t_len, D), F32), jax.ShapeDtypeStruct((t_len, D), F32),
                   jax.ShapeDtypeStruct((t_len, 2 * D), F32)],
        scratch_shapes=[pltpu.VMEM((H_C, CHUNK, CHUNK), BF), pltpu.VMEM((tm, D), F32)],
        compiler_params=_params(), name=name,
    )(x, vec, w_in, b_in, cvec, ws, bst, w_out)


def _mixc_bwd(dout, x, fpre, pre, vec, w_in, cvec, ws, bst, w_out, name):
    t_len = x.shape[0]
    tm = min(TM_MIX, t_len)
    nt = t_len // tm

    def body(dout_ref, x_ref, f_ref, pre_ref, vec_ref, win_ref, cvec_ref, ws_ref, bst_ref, wout_ref,
             dx_ref, dpre_ref, p_ref, hb_ref, dfb_ref, acc_ref, dbin_ref, dws_ref, dbst_ref,
             wsm, wsmt, mix_sc, dvn_sc, dmsum):
        i = pl.program_id(0)

        @pl.when(i == 0)
        def _():
            _c_mask_weights(ws_ref, wsm, wsmt)
            acc_ref[...] = jnp.zeros((8, D), F32)
            dbin_ref[...] = jnp.zeros((8, 2 * D), F32)
            dws_ref[...] = jnp.zeros((H_C, CHUNK, CHUNK), F32)
            dmsum[...] = jnp.zeros((CHUNK, D), F32)

        dout_t = dout_ref[...]
        df = _post_bwd(dout_t, f_ref[...], vec_ref, acc_ref, 1.0)
        dfb = df.astype(BF)
        dfb_ref[...] = dfb
        h, xn, r, n = _pre_fwd(x_ref[...], vec_ref)
        hb_ref[...] = h.astype(BF)
        pre_t = pre_ref[...]
        u, t, rstd, vhat, vnb = _c_inner(pre_t, cvec_ref, wsm, bst_ref, mix_sc, tm)
        mix = mix_sc[...]
        p_ref[...] = (u * mix).astype(BF)
        dp = _dot_nt(dfb, wout_ref[...])
        du = dp * mix
        dmix = dp * u
        dmb = dmix.astype(BF)
        for nn in range(tm // CHUNK):
            rows = slice(CHUNK * nn, CHUNK * (nn + 1))
            dmsum[...] += dmix[rows, :]
            for hh in range(H_C):
                cols = slice(CHUNK * hh, CHUNK * (hh + 1))
                dvn_sc[rows, cols] = _dot(wsmt[hh], dmb[rows, cols])
                dws_ref[hh] += _dot_nt(dmb[rows, cols], vnb[rows, cols])
        dvn = dvn_sc[...]
        acc_ref[5:6, :] += _colsum(dvn * vhat)
        acc_ref[6:7, :] += _colsum(dvn)
        dvhat = dvn * cvec_ref[0:1, :]
        dv = rstd * (dvhat - jnp.mean(dvhat, axis=-1, keepdims=True)
                     - vhat * jnp.mean(dvhat * vhat, axis=-1, keepdims=True))
        gg = _gelu_grad(pre_t, t)
        dpre_u = du * gg[:, 0:D]
        dpre_v = dv * gg[:, D:2 * D]
        dbin_ref[0:1, 0:D] += _colsum(dpre_u)
        dbin_ref[0:1, D:2 * D] += _colsum(dpre_v)
        dpre_ref[:, 0:D] = dpre_u.astype(BF)
        dpre_ref[:, D:2 * D] = dpre_v.astype(BF)
        dh = jnp.zeros((tm, D), F32)
        for j in range(N_DEV):
            dh = dh + _dot_nt(dpre_ref[:, 256 * j:256 * (j + 1)], win_ref[j])
        dx_ref[...] = _pre_bwd(dout_t, dh, xn, r, n, vec_ref, acc_ref)

        @pl.when(i == nt - 1)
        def _():
            row = lax.broadcasted_iota(jnp.int32, (CHUNK, CHUNK), 0)
            col = lax.broadcasted_iota(jnp.int32, (CHUNK, CHUNK), 1)
            for hh in range(H_C):
                dws_ref[hh] = jnp.where(row >= col, dws_ref[hh], 0.0)
                dbst_ref[:, hh:hh + 1] = jnp.sum(dmsum[:, CHUNK * hh:CHUNK * (hh + 1)], axis=1, keepdims=True)

    return pl.pallas_call(
        body, grid=(nt,),
        in_specs=[_tile(tm, D), _tile(tm, D), _tile(tm, D), _tile(tm, 2 * D), _full((8, D)), _full((N_DEV, D, 256)),
                  _full((8, D)), _full((H_C, CHUNK, CHUNK)), _full((CHUNK, H_C)), _full((D, D))],
        out_specs=[_tile(tm, D), _tile(tm, 2 * D), _tile(tm, D), _tile(tm, D), _tile(tm, D), _full((8, D)),
                   _full((8, 2 * D)), _full((H_C, CHUNK, CHUNK)), _full((CHUNK, H_C))],
        out_shape=[jax.ShapeDtypeStruct((t_len, D), F32), jax.ShapeDtypeStruct((t_len, 2 * D), BF),
                   jax.ShapeDtypeStruct((t_len, D), BF), jax.ShapeDtypeStruct((t_len, D), BF),
                   jax.ShapeDtypeStruct((t_len, D), BF), jax.ShapeDtypeStruct((8, D), F32),
                   jax.ShapeDtypeStruct((8, 2 * D), F32), jax.ShapeDtypeStruct((H_C, CHUNK, CHUNK), F32),
                   jax.ShapeDtypeStruct((CHUNK, H_C), F32)],
        scratch_shapes=[pltpu.VMEM((H_C, CHUNK, CHUNK), BF), pltpu.VMEM((H_C, CHUNK, CHUNK), BF),
                        pltpu.VMEM((tm, D), F32), pltpu.VMEM((tm, D), F32), pltpu.VMEM((CHUNK, D), F32)],
        compiler_params=_params(), name=name,
    )(dout, x, fpre, pre, vec, w_in, cvec, ws, bst, w_out)


def _gmean(x, g_ref):
    hi = x.astype(BF)
    lo = (x - hi.astype(F32)).astype(BF)
    return _dot(hi, g_ref[...]) + _dot(lo, g_ref[...])


def _log_sigmoid(lam):
    e = jnp.exp(-jnp.abs(lam))
    log1p = jnp.where(e < 1e-2, e * (1.0 - e * (0.5 - e * (1.0 / 3.0 - 0.25 * e))), jnp.log(1.0 + e))
    return jnp.minimum(lam, 0.0) - log1p


def _neg_expm1(y):
    series = -(y * (1.0 + y * (0.5 + y * (1.0 / 6.0 + y * (1.0 / 24.0 + y * (1.0 / 120.0))))))
    return jnp.where(y > -0.1, series, 1.0 - jnp.exp(y))


def _conv_causal(ext, taps_ref, bias, k_taps, halo, tm):
    acc = bias + taps_ref[0:1, :] * ext[halo - k_taps + 1:halo - k_taps + 1 + tm, :]
    for k in range(1, k_taps):
        off = halo - k_taps + 1 + k
        acc = acc + taps_ref[k:k + 1, :] * ext[off:off + tm, :]
    return acc


def _scan(a, u, tm, reverse):
    row = lax.broadcasted_iota(jnp.int32, (tm, W_A), 0)
    d = 1
    while d < tm:
        if reverse:
            keep = row < tm - d
            shift = tm - d
        else:
            keep = row >= d
            shift = d
        a_sh = jnp.where(keep, pltpu.roll(a, shift, 0), 1.0)
        u_sh = jnp.where(keep, pltpu.roll(u, shift, 0), 0.0)
        u = a * u_sh + u
        a = a * a_sh
        d *= 2
    return a, u


def _a_gates(xc, cv_ref, wr_ref, wi_ref):
    xcb = xc.astype(BF)
    r = _sigmoid(_dot(xcb, wr_ref[...]) + cv_ref[5:6, :])
    ig = _sigmoid(_dot(xcb, wi_ref[...]) + cv_ref[6:7, :])
    ls = _log_sigmoid(cv_ref[7:8, :])
    la = LRU_C * r * ls
    a = jnp.exp(la)
    m = jnp.sqrt(_neg_expm1(2.0 * la))
    return xcb, r, ig, ls, a, m


def _b_norm(vc, cv_ref, g_ref):
    mu = _gmean(vc, g_ref)
    dv = vc - mu
    rstd = lax.rsqrt(_gmean(dv * dv, g_ref) + EPS)
    vhat = dv * rstd
    vln = vhat * cv_ref[9:10, :] + cv_ref[10:11, :]
    return rstd, vhat, vln


def _mixab_fwd(x, vec, w_in, cv, w31, wr, wi, gmat, w_out, name):
    t_len = x.shape[0]
    tm = min(TM_MIX, t_len)

    def body(x_ref, vec_ref, win_ref, cv_ref, w31_ref, wr_ref, wi_ref, g_ref, wout_ref,
             xo_ref, f_ref, z_ref, hs_ref, ext_a, ext_b, hc):
        @pl.when(pl.program_id(0) == 0)
        def _():
            ext_a[0:HALO_A, :] = jnp.zeros((HALO_A, W_A), F32)
            ext_b[0:HALO_B, :] = jnp.zeros((HALO_B, W_B), F32)
            hc[...] = jnp.zeros((8, W_A), F32)

        x_t = x_ref[...]
        h, _, _, _ = _pre_fwd(x_t, vec_ref)
        hb = h.astype(BF)
        for j in range(N_DEV):
            z_ref[:, 256 * j:256 * (j + 1)] = _dot(hb, win_ref[j])
        ext_a[HALO_A:HALO_A + tm, :] = z_ref[:, W_A:2 * W_A]
        xc = _conv_causal(ext_a, cv_ref, cv_ref[4:5, :], CONV_A, HALO_A, tm)
        ext_a[0:HALO_A, :] = ext_a[tm:tm + HALO_A, :]
        _, _, ig, _, a, m = _a_gates(xc, cv_ref, wr_ref, wi_ref)
        a_cum, hloc = _scan(a, m * ig * xc, tm, False)
        hs = hloc + a_cum * hc[0:1, :]
        hs_ref[...] = hs
        hc[0:1, :] = hs[tm - 1:tm, :]
        gel, _ = _gelu(z_ref[:, 0:W_A])
        ya = hs * gel
        ext_b[HALO_B:HALO_B + tm, :] = z_ref[:, 2 * W_A:2 * W_A + W_B] * _sigmoid(z_ref[:, 2 * W_A + W_B:2 * W_A + 2 * W_B])
        vc = _conv_causal(ext_b, w31_ref, cv_ref[8:9, :], CONV_B, HALO_B, tm)
        ext_b[0:HALO_B, :] = ext_b[tm:tm + HALO_B, :]
        _, _, vln = _b_norm(vc, cv_ref, g_ref)
        yb = vln * _sigmoid(vln)
        fpre = _dot(ya.astype(BF), wout_ref[0:W_A, :]) + _dot(yb.astype(BF), wout_ref[W_A:W_A + W_B, :])
        f_ref[...] = fpre
        xo_ref[...] = _post_fwd(x_t, fpre, vec_ref, 1.0)

    return pl.pallas_call(
        body, grid=(t_len // tm,),
        in_specs=[_tile(tm, D), _full((8, D)), _full((N_DEV, D, 256)), _full((16, W_A)), _full((32, W_B)),
                  _full((W_A, W_A)), _full((W_A, W_A)), _full((W_B, W_B)), _full((D, D))],
        out_specs=[_tile(tm, D), _tile(tm, D), _tile(tm, 2 * D), _tile(tm, W_A)],
        out_shape=[jax.ShapeDtypeStruct((t_len, D), F32), jax.ShapeDtypeStruct((t_len, D), F32),
                   jax.ShapeDtypeStruct((t_len, 2 * D), F32), jax.ShapeDtypeStruct((t_len, W_A), F32)],
        scratch_shapes=[pltpu.VMEM((tm + HALO_A, W_A), F32), pltpu.VMEM((tm + HALO_B, W_B), F32), pltpu.VMEM((8, W_A), F32)],
        compiler_params=_params(), name=name,
    )(x, vec, w_in, cv, w31, wr, wi, gmat, w_out)


def _mixab_bwd(dout, x, fpre, z, hs, vec, w_in, cv, w31, wr, wi, gmat, w_out, name):
    t_len = x.shape[0]
    tm = min(TM_MIX, t_len)
    nt = t_len // tm

    def rev(i):
        return nt - 1 - i

    def rtile(ncol):
        return pl.BlockSpec((tm, ncol), lambda i: (rev(i), 0))

    def body(dout_ref, x_ref, f_ref, z_ref, zp_ref, hs_ref, hsp_ref, vec_ref, win_ref, cv_ref, w31_ref, wr_ref, wi_ref,
             g_ref, wout_ref,
             dx_ref, dz_ref, yab_ref, hb_ref, dfb_ref, xcb_ref, dri_ref, acc_ref, accs_ref, dw31_ref,
             ext_a, ext_b, ext_h, ext_dx, ext_dv, carry):
        i = pl.program_id(0)
        has_prev = (rev(i) > 0).astype(F32)

        @pl.when(i == 0)
        def _():
            acc_ref[...] = jnp.zeros((8, D), F32)
            accs_ref[...] = jnp.zeros((16, W_A), F32)
            dw31_ref[...] = jnp.zeros((32, W_B), F32)
            ext_dx[tm:tm + HALO_A, :] = jnp.zeros((HALO_A, W_A), F32)
            ext_dv[tm:tm + HALO_B, :] = jnp.zeros((HALO_B, W_B), F32)
            carry[...] = jnp.zeros((8, W_A), F32)

        dout_t = dout_ref[...]
        df = _post_bwd(dout_t, f_ref[...], vec_ref, acc_ref, 1.0)
        dfb = df.astype(BF)
        dfb_ref[...] = dfb
        h, xn, r_x, n = _pre_fwd(x_ref[...], vec_ref)
        hb_ref[...] = h.astype(BF)

        ag = z_ref[:, 0:W_A]
        bv = z_ref[:, 2 * W_A:2 * W_A + W_B]
        sg = _sigmoid(z_ref[:, 2 * W_A + W_B:2 * W_A + 2 * W_B])
        ext_a[0:HALO_A, :] = zp_ref[HALO_B - HALO_A:HALO_B, W_A:2 * W_A] * has_prev
        ext_a[HALO_A:HALO_A + tm, :] = z_ref[:, W_A:2 * W_A]
        xc = _conv_causal(ext_a, cv_ref, cv_ref[4:5, :], CONV_A, HALO_A, tm)
        xcb, r, ig, ls, a, m = _a_gates(xc, cv_ref, wr_ref, wi_ref)
        xcb_ref[...] = xcb
        hs_t = hs_ref[...]
        ext_h[0:8, :] = hsp_ref[...] * has_prev
        ext_h[8:8 + tm, :] = hs_t
        hprev = ext_h[7:7 + tm, :]
        gel, tg = _gelu(ag)
        ext_b[0:HALO_B, :] = (zp_ref[:, 2 * W_A:2 * W_A + W_B]
                              * _sigmoid(zp_ref[:, 2 * W_A + W_B:2 * W_A + 2 * W_B])) * has_prev
        ext_b[HALO_B:HALO_B + tm, :] = bv * sg
        vc = _conv_causal(ext_b, w31_ref, cv_ref[8:9, :], CONV_B, HALO_B, tm)
        rstd, vhat, vln = _b_norm(vc, cv_ref, g_ref)
        sv = _sigmoid(vln)
        yab_ref[:, 0:W_A] = (hs_t * gel).astype(BF)
        yab_ref[:, W_A:W_A + W_B] = (vln * sv).astype(BF)

        dya = _dot_nt(dfb, wout_ref[0:W_A, :])
        dyb = _dot_nt(dfb, wout_ref[W_A:W_A + W_B, :])

        dag = dya * hs_t * _gelu_grad(ag, tg)
        row = lax.broadcasted_iota(jnp.int32, (tm, W_A), 0)
        last = row == tm - 1
        a_next = jnp.where(last, 1.0, pltpu.roll(a, tm - 1, 0))
        u0 = dya * gel + jnp.where(last, carry[0:1, :], 0.0)
        _, dhs = _scan(a_next, u0, tm, True)
        carry[0:1, :] = a[0:1, :] * dhs[0:1, :]
        da = dhs * hprev
        dm = dhs * ig * xc
        di = dhs * m * xc
        dxc = dhs * m * ig
        dla = da * a - dm * (a * a) / m
        accs_ref[7:8, :] += _colsum(dla * r) * (LRU_C * _sigmoid(-cv_ref[7:8, :]))
        drp = (dla * (LRU_C * ls)) * r * (1.0 - r)
        dip = di * ig * (1.0 - ig)
        accs_ref[5:6, :] += _colsum(drp)
        accs_ref[6:7, :] += _colsum(dip)
        drpb = drp.astype(BF)
        dipb = dip.astype(BF)
        dri_ref[:, 0:W_A] = drpb
        dri_ref[:, W_A:2 * W_A] = dipb
        dxc = dxc + _dot_nt(drpb, wr_ref[...]) + _dot_nt(dipb, wi_ref[...])
        accs_ref[4:5, :] += _colsum(dxc)
        ext_dx[0:tm, :] = dxc
        dax = jnp.zeros((tm, W_A), F32)
        for k in range(CONV_A):
            accs_ref[k:k + 1, :] += _colsum(dxc * ext_a[HALO_A - CONV_A + 1 + k:HALO_A - CONV_A + 1 + k + tm, :])
            dax = dax + cv_ref[k:k + 1, :] * ext_dx[CONV_A - 1 - k:CONV_A - 1 - k + tm, :]
        ext_dx[tm:tm + HALO_A, :] = dxc[0:HALO_A, :]

        dvln = dyb * (sv * (1.0 + vln * (1.0 - sv)))
        accs_ref[9:10, :] += _colsum(dvln * vhat)
        accs_ref[10:11, :] += _colsum(dvln)
        dvhat = dvln * cv_ref[9:10, :]
        dvc = rstd * (dvhat - _gmean(dvhat, g_ref) - vhat * _gmean(dvhat * vhat, g_ref))
        accs_ref[8:9, :] += _colsum(dvc)
        ext_dv[0:tm, :] = dvc
        dvv = jnp.zeros((tm, W_B), F32)
        for k in range(CONV_B):
            dw31_ref[k:k + 1, :] += _colsum(dvc * ext_b[HALO_B - CONV_B + 1 + k:HALO_B - CONV_B + 1 + k + tm, :])
            dvv = dvv + w31_ref[k:k + 1, :] * ext_dv[CONV_B - 1 - k:CONV_B - 1 - k + tm, :]
        ext_dv[tm:tm + HALO_B, :] = dvc[0:HALO_B, :]

        dz_ref[:, 0:W_A] = dag.astype(BF)
        dz_ref[:, W_A:2 * W_A] = dax.astype(BF)
        dz_ref[:, 2 * W_A:2 * W_A + W_B] = (dvv * sg).astype(BF)
        dz_ref[:, 2 * W_A + W_B:2 * W_A + 2 * W_B] = (dvv * bv * sg * (1.0 - sg)).astype(BF)
        dh = jnp.zeros((tm, D), F32)
        for j in range(N_DEV):
            dh = dh + _dot_nt(dz_ref[:, 256 * j:256 * (j + 1)], win_ref[j])
        dx_ref[...] = _pre_bwd(dout_t, dh, xn, r_x, n, vec_ref, acc_ref)

    zp_spec = pl.BlockSpec((HALO_B, 2 * D), lambda i: (jnp.maximum(rev(i) * (tm // HALO_B) - 1, 0), 0))
    hsp_spec = pl.BlockSpec((8, W_A), lambda i: (jnp.maximum(rev(i) * (tm // 8) - 1, 0), 0))
    return pl.pallas_call(
        body, grid=(nt,),
        in_specs=[rtile(D), rtile(D), rtile(D), rtile(2 * D), zp_spec, rtile(W_A), hsp_spec, _full((8, D)),
                  _full((N_DEV, D, 256)), _full((16, W_A)), _full((32, W_B)), _full((W_A, W_A)), _full((W_A, W_A)),
                  _full((W_B, W_B)), _full((D, D))],
        out_specs=[rtile(D), rtile(2 * D), rtile(D), rtile(D), rtile(D), rtile(W_A), rtile(2 * W_A), _full((8, D)),
                   _full((16, W_A)), _full((32, W_B))],
        out_shape=[jax.ShapeDtypeStruct((t_len, D), F32), jax.ShapeDtypeStruct((t_len, 2 * D), BF),
                   jax.ShapeDtypeStruct((t_len, D), BF), jax.ShapeDtypeStruct((t_len, D), BF),
                   jax.ShapeDtypeStruct((t_len, D), BF), jax.ShapeDtypeStruct((t_len, W_A), BF),
                   jax.ShapeDtypeStruct((t_len, 2 * W_A), BF), jax.ShapeDtypeStruct((8, D), F32),
                   jax.ShapeDtypeStruct((16, W_A), F32), jax.ShapeDtypeStruct((32, W_B), F32)],
        scratch_shapes=[pltpu.VMEM((tm + HALO_A, W_A), F32), pltpu.VMEM((tm + HALO_B, W_B), F32),
                        pltpu.VMEM((tm + 8, W_A), F32), pltpu.VMEM((tm + HALO_A, W_A), F32),
                        pltpu.VMEM((tm + HALO_B, W_B), F32), pltpu.VMEM((8, W_A), F32)],
        compiler_params=_params(), name=name,
    )(dout, x, fpre, z, z, hs, hs, vec, w_in, cv, w31, wr, wi, gmat, w_out)


def _loss_head(y, tgt, name):
    t_len = y.shape[0]
    tm = min(TM_FFN, t_len)

    def body(y_ref, t_ref, dy_ref, loss_ref):
        @pl.when(pl.program_id(0) == 0)
        def _():
            loss_ref[...] = jnp.zeros((8, 128), F32)

        err = y_ref[...] - t_ref[...]
        dy_ref[...] = err * (1.0 / D)
        loss_ref[...] += jnp.sum(err * err) * (0.5 / D)

    return pl.pallas_call(
        body, grid=(t_len // tm,), in_specs=[_tile(tm, D), _tile(tm, D)],
        out_specs=[_tile(tm, D), _full((8, 128))],
        out_shape=[jax.ShapeDtypeStruct((t_len, D), F32), jax.ShapeDtypeStruct((8, 128), F32)],
        compiler_params=_params(), name=name,
    )(y, tgt)


def _vec(p, l, j):
    return jnp.concatenate([p["mod"][l, j], p["norm_pre"][l, j][None], p["norm_post"][l, j][None], jnp.zeros((3, D), F32)], 0)


def _ab_consts(p):
    gw = p["a_gate_w"]
    gb = p["a_gate_b"]
    half = W_A // 8
    wr = jax.scipy.linalg.block_diag(*[gw[hh, :, 0:half] for hh in range(8)]).astype(BF)
    wi = jax.scipy.linalg.block_diag(*[gw[hh, :, half:2 * half] for hh in range(8)]).astype(BF)
    rows = [p["a_conv_w"], p["a_conv_b"][None], gb[:, 0:half].reshape(1, W_A), gb[:, half:2 * half].reshape(1, W_A),
            p["a_lam"][None], p["b_conv_b"][None], p["b_norm_g"][None], p["b_norm_b"][None], jnp.zeros((5, W_A), F32)]
    cv = jnp.concatenate(rows, 0)
    w31 = jnp.concatenate([p["b_conv_w"], jnp.zeros((1, W_B), F32)], 0)
    grp = jnp.arange(W_B) // (W_B // 8)
    gmat = ((grp[:, None] == grp[None, :]).astype(F32) / (W_B // 8)).astype(BF)
    return cv, w31, wr, wi, gmat


def _local_step(x, tgt, p):
    g = {}
    saved = []
    cur = x
    ab_c = _ab_consts(p)
    c_cvec = jnp.concatenate([p["c_norm_g"][None], p["c_norm_b"][None], jnp.zeros((6, D), F32)], 0)
    c_bst = jnp.transpose(p["c_b_s"])
    c_bin = p["c_b_in"][None]
    for l in range(2):
        for j in range(3):
            vec = _vec(p, l, j)
            tag = f"l{l}s{j}"
            if j != 1:
                f = 2 * l + (0 if j == 0 else 1)
                nxt, fpre, gu = _ffn_fwd(cur, vec, p["w13g"], p["w2g"], f, 0.5, "ffn_fwd_" + tag)
                saved.append((cur, fpre, gu, vec))
            elif l == 0:
                nxt, fpre, z, hs = _mixab_fwd(cur, vec, p["ab_in"], *ab_c, p["ab_out"], "mixab_fwd_" + tag)
                saved.append((cur, fpre, z, hs, vec))
            else:
                nxt, fpre, pre = _mixc_fwd(cur, vec, p["c_in"], c_bin, c_cvec, p["c_w_s"], c_bst, p["c_out"], "mixc_fwd_" + tag)
                saved.append((cur, fpre, pre, vec))
            cur = nxt
    dcur, loss_blk = _loss_head(cur, tgt, "loss_head")
    accs = {}
    for l in (1, 0):
        for j in (2, 1, 0):
            tag = f"l{l}s{j}"
            sv = saved[3 * l + j]
            if j != 1:
                f = 2 * l + (0 if j == 0 else 1)
                xin, fpre, gu, vec = sv
                dcur, dgu, s, hb, dfb, acc = _ffn_bwd(dcur, xin, fpre, gu, vec, p["w13g"], p["w2g"], f, 0.5, "ffn_bwd_" + tag)
                g[f"w13_{f}"] = _wgrad(hb, dgu, N_DEV, D, FF_PAD, "share", "stack", D, BF, "wgrad_w13_" + tag)
                g[f"w2_{f}"] = _wgrad(s, dfb, FF_CHUNKS, FF_PAD, D, "stack", "share", FF_SHARD, BF, "wgrad_w2_" + tag)
            elif l == 0:
                xin, fpre, z, hs, vec = sv
                dcur, dz, yab, hb, dfb, xcb, dri, acc, accs_ab, dw31 = _mixab_bwd(
                    dcur, xin, fpre, z, hs, vec, p["ab_in"], *ab_c, p["ab_out"], "mixab_bwd_" + tag)
                g["ab_in"] = _wgrad(hb, dz, N_DEV, D, 256, "share", "cols", D, BF, "wgrad_ab_in")
                g["ab_out"] = _wgrad(yab, dfb, 1, D, D, "share", "share", D, BF, "wgrad_ab_out")
                g["gate"] = _wgrad(xcb, dri, 1, W_A, 2 * W_A, "share", "share", W_A, F32, "wgrad_gate")
                g["accs_ab"] = accs_ab
                g["dw31"] = dw31
            else:
                xin, fpre, pre, vec = sv
                dcur, dpre, pb, hb, dfb, acc, dbin, dws, dbst = _mixc_bwd(
                    dcur, xin, fpre, pre, vec, p["c_in"], c_cvec, p["c_w_s"], c_bst, p["c_out"], "mixc_bwd_" + tag)
                g["c_in"] = _wgrad(hb, dpre, N_DEV, D, 256, "share", "cols", D, BF, "wgrad_c_in")
                g["c_out"] = _wgrad(pb, dfb, 1, D, D, "share", "share", D, BF, "wgrad_c_out")
                g["c_small"] = (acc, dbin, dws, dbst)
            accs[f"{l}{j}"] = acc
    g["accs"] = accs
    return loss_blk, dcur, g


def _exchange(arrays, mode, name):
    n = len(arrays)

    def body(*refs):
        ins, outs = refs[:n], refs[n:2 * n]
        send_sems, recv_sems, loc_sems = refs[2 * n:]
        x, y, c = lax.axis_index("x"), lax.axis_index("y"), lax.axis_index("c")
        me = 4 * x + 2 * y + c

        def src(i, dev):
            return ins[i] if mode == "gather" else ins[i].at[dev]

        local = [pltpu.make_async_copy(src(i, me), outs[i].at[me], loc_sems.at[i]) for i in range(n)]
        for cp in local:
            cp.start()
        remote = []
        for mask in range(1, N_DEV):
            px = 1 - x if mask & 4 else x
            py = 1 - y if mask & 2 else y
            pc = 1 - c if mask & 1 else c
            peer = 4 * px + 2 * py + pc
            for i in range(n):
                k = i * (N_DEV - 1) + mask - 1
                pltpu.make_async_remote_copy(
                    src_ref=src(i, peer), dst_ref=outs[i].at[me], send_sem=send_sems.at[k], recv_sem=recv_sems.at[k],
                    device_id=(px, py, pc), device_id_type=pl.DeviceIdType.MESH).start()
                remote.append(pltpu.make_async_remote_copy(
                    src_ref=src(i, peer), dst_ref=outs[i].at[peer], send_sem=send_sems.at[k], recv_sem=recv_sems.at[k],
                    device_id=(px, py, pc), device_id_type=pl.DeviceIdType.MESH))
        for cp in remote:
            cp.wait()
        for cp in local:
            cp.wait()

    out_shape = [jax.ShapeDtypeStruct(((N_DEV,) + a.shape) if mode == "gather" else a.shape, a.dtype) for a in arrays]
    return pl.pallas_call(
        body, in_specs=[pl.BlockSpec(memory_space=pl.ANY)] * n, out_specs=[pl.BlockSpec(memory_space=pl.ANY)] * n,
        out_shape=out_shape,
        scratch_shapes=[pltpu.SemaphoreType.DMA((n * (N_DEV - 1),)), pltpu.SemaphoreType.DMA((n * (N_DEV - 1),)),
                        pltpu.SemaphoreType.DMA((n,))],
        name=name,
    )(*arrays)


def _sum_slots(a, name):
    def body(a_ref, o_ref):
        acc = a_ref[0]
        for s in range(1, N_DEV):
            acc = acc + a_ref[s]
        o_ref[...] = acc

    return pl.pallas_call(body, out_shape=jax.ShapeDtypeStruct(a.shape[1:], F32), name=name,
                          compiler_params=pltpu.CompilerParams(vmem_limit_bytes=VMEM_LIMIT))(a)


def _pack(pieces, mult):
    flat = jnp.concatenate([q.reshape(-1).astype(F32) for q in pieces])
    size = -(-flat.shape[0] // mult) * mult
    return jnp.pad(flat, (0, size - flat.shape[0])).reshape(size // 128, 128)


def _unpack(flat, shapes):
    out, off = [], 0
    for shp in shapes:
        size = math.prod(shp)
        out.append(flat[..., off:off + size].reshape(flat.shape[:-1] + tuple(shp)))
        off += size
    return out


def _mod_part(c_all, ada_w, ada_b_mine, name):
    cols = ada_w.shape[-1]

    def body(c_ref, w_ref, b_ref, o_ref):
        cv = c_ref[...]
        ca = cv * _sigmoid(cv)
        for l in range(2):
            o_ref[l] = jnp.dot(ca, w_ref[l], preferred_element_type=F32, precision=lax.Precision.HIGHEST) + b_ref[l:l + 1, :]

    return pl.pallas_call(body, out_shape=jax.ShapeDtypeStruct((2, N_DEV, cols), F32), name=name,
                          compiler_params=pltpu.CompilerParams(vmem_limit_bytes=VMEM_LIMIT))(c_all, ada_w, ada_b_mine)


def _ada_w_grad(c_all_t, dmod_mine, name):
    cols = dmod_mine.shape[-1]

    def body(ct_ref, d_ref, o_ref):
        cv = ct_ref[...]
        ca = cv * _sigmoid(cv)
        for l in range(2):
            acc = ca[:, 0:1] * d_ref[l, 0:1, :]
            for b in range(1, N_DEV):
                acc = acc + ca[:, b:b + 1] * d_ref[l, b:b + 1, :]
            o_ref[l] = acc

    return pl.pallas_call(body, out_shape=jax.ShapeDtypeStruct((2, D, cols), F32), name=name,
                          compiler_params=pltpu.CompilerParams(vmem_limit_bytes=VMEM_LIMIT))(c_all_t, dmod_mine)


def _adamw_math(w, g, m, v):
    m2 = ADAM_B1 * m + (1.0 - ADAM_B1) * g
    v2 = ADAM_B2 * v + (1.0 - ADAM_B2) * (g * g)
    m_hat = m2 / (1.0 - ADAM_B1 ** ADAM_STEP)
    v_hat = v2 / (1.0 - ADAM_B2 ** ADAM_STEP)
    delta = -ADAM_LR * (m_hat / (jnp.sqrt(v_hat) + ADAM_EPS) + ADAM_WD * w)
    return delta, m2, v2


def _adamw_big(w, g, m, v, name):
    rows, cols = w.shape
    br = next(b for b in (512, 352, 256, 128, 64, 32, 16, 8) if rows % b == 0)
    partial = g.ndim == 3

    def body(w_ref, g_ref, m_ref, v_ref, go_ref, d_ref, mo_ref, vo_ref):
        if partial:
            gsum = g_ref[0, :, 0:cols].astype(F32)
            for s in range(1, N_DEV):
                gsum = gsum + g_ref[s, :, 0:cols].astype(F32)
        else:
            gsum = g_ref[...]
        go_ref[...] = gsum
        d_ref[...], mo_ref[...], vo_ref[...] = _adamw_math(w_ref[...], gsum, m_ref[...], v_ref[...])

    blk = _tile(br, cols)
    g_spec = pl.BlockSpec((N_DEV, br, g.shape[-1]), lambda i: (0, i, 0)) if partial else blk
    shp = jax.ShapeDtypeStruct((rows, cols), F32)
    return pl.pallas_call(body, grid=(rows // br,), in_specs=[blk, g_spec, blk, blk], out_specs=[blk] * 4,
                          out_shape=[shp] * 4, compiler_params=_params(), name=name)(w, g, m, v)


def _adamw_small(ws, gs, ms, vs, name):
    n = len(ws)

    def body(*refs):
        for i in range(n):
            w_ref, g_ref, m_ref, v_ref = (refs[k * n + i] for k in range(4))
            d_ref, mo_ref, vo_ref = (refs[(4 + k) * n + i] for k in range(3))
            d_ref[...], mo_ref[...], vo_ref[...] = _adamw_math(w_ref[...], g_ref[...], m_ref[...], v_ref[...])

    shapes = [jax.ShapeDtypeStruct(w.shape, F32) for w in ws]
    outs = pl.pallas_call(body, out_shape=shapes * 3, name=name,
                          compiler_params=pltpu.CompilerParams(vmem_limit_bytes=VMEM_LIMIT))(*ws, *gs, *ms, *vs)
    return outs[:n], outs[n:2 * n], outs[2 * n:]


def _as2d(a):
    return a.reshape(-1, a.shape[-1])


def kernel(x, c, ada_w, ada_b, norm_pre, norm_post, ffn_w13, ffn_w2, ab_w_in, a_conv_w, a_conv_b, a_gate_w, a_gate_b, a_lam, b_conv_w, b_conv_b, b_norm_g, b_norm_b, ab_w_out, c_w_in, c_b_in, c_norm_g, c_norm_b, c_w_s, c_b_s, c_w_out, loss_target, m_ada_w, m_ada_b, m_norm_pre, m_norm_post, m_ffn_w13, m_ffn_w2, m_ab_w_in, m_a_conv_w, m_a_conv_b, m_a_gate_w, m_a_gate_b, m_a_lam, m_b_conv_w, m_b_conv_b, m_b_norm_g, m_b_norm_b, m_ab_w_out, m_c_w_in, m_c_b_in, m_c_norm_g, m_c_norm_b, m_c_w_s, m_c_b_s, m_c_w_out, v_ada_w, v_ada_b, v_norm_pre, v_norm_post, v_ffn_w13, v_ffn_w2, v_ab_w_in, v_a_conv_w, v_a_conv_b, v_a_gate_w, v_a_gate_b, v_a_lam, v_b_conv_w, v_b_conv_b, v_b_norm_g, v_b_norm_b, v_ab_w_out, v_c_w_in, v_c_b_in, v_c_norm_g, v_c_norm_b, v_c_w_s, v_c_b_s, v_c_w_out):
    me = 4 * lax.axis_index("x") + 2 * lax.axis_index("y") + lax.axis_index("c")
    weights = dict(ada_w=ada_w, ada_b=ada_b, norm_pre=norm_pre, norm_post=norm_post, ffn_w13=ffn_w13, ffn_w2=ffn_w2,
                   ab_w_in=ab_w_in, a_conv_w=a_conv_w, a_conv_b=a_conv_b, a_gate_w=a_gate_w, a_gate_b=a_gate_b, a_lam=a_lam,
                   b_conv_w=b_conv_w, b_conv_b=b_conv_b, b_norm_g=b_norm_g, b_norm_b=b_norm_b, ab_w_out=ab_w_out,
                   c_w_in=c_w_in, c_b_in=c_b_in, c_norm_g=c_norm_g, c_norm_b=c_norm_b, c_w_s=c_w_s, c_b_s=c_b_s, c_w_out=c_w_out)
    moms = dict(ada_w=m_ada_w, ada_b=m_ada_b, norm_pre=m_norm_pre, norm_post=m_norm_post, ffn_w13=m_ffn_w13, ffn_w2=m_ffn_w2,
                ab_w_in=m_ab_w_in, a_conv_w=m_a_conv_w, a_conv_b=m_a_conv_b, a_gate_w=m_a_gate_w, a_gate_b=m_a_gate_b,
                a_lam=m_a_lam, b_conv_w=m_b_conv_w, b_conv_b=m_b_conv_b, b_norm_g=m_b_norm_g, b_norm_b=m_b_norm_b,
                ab_w_out=m_ab_w_out, c_w_in=m_c_w_in, c_b_in=m_c_b_in, c_norm_g=m_c_norm_g, c_norm_b=m_c_norm_b,
                c_w_s=m_c_w_s, c_b_s=m_c_b_s, c_w_out=m_c_w_out)
    vars_ = dict(ada_w=v_ada_w, ada_b=v_ada_b, norm_pre=v_norm_pre, norm_post=v_norm_post, ffn_w13=v_ffn_w13, ffn_w2=v_ffn_w2,
                 ab_w_in=v_ab_w_in, a_conv_w=v_a_conv_w, a_conv_b=v_a_conv_b, a_gate_w=v_a_gate_w, a_gate_b=v_a_gate_b,
                 a_lam=v_a_lam, b_conv_w=v_b_conv_w, b_conv_b=v_b_conv_b, b_norm_g=v_b_norm_g, b_norm_b=v_b_norm_b,
                 ab_w_out=v_ab_w_out, c_w_in=v_c_w_in, c_b_in=v_c_b_in, c_norm_g=v_c_norm_g, c_norm_b=v_c_norm_b,
                 c_w_s=v_c_w_s, c_b_s=v_c_b_s, c_w_out=v_c_w_out)
    names = list(weights)

    w13b = jnp.pad(ffn_w13.astype(BF).reshape(4, D, FF_SHARD), ((0, 0), (0, 0), (0, FF_PAD - FF_SHARD)))
    small_shapes = [(D,), (2, 3, 128), (2, 3, 128), (CONV_A, 64), (CONV_B, 64), (256,), (128,), (128,)]
    small = _pack([c, norm_pre, norm_post, a_conv_w, b_conv_w, c_b_in, c_norm_g, c_norm_b], 1024)
    w13g, w2g, ab_in_g, ab_out_g, c_in_g, c_out_g, small_g = _exchange(
        [w13b, ffn_w2.astype(BF).reshape(4, W2_SHARD, D), ab_w_in[0].astype(BF), ab_w_out[0].astype(BF),
         c_w_in[0].astype(BF), c_w_out[0].astype(BF), small], "gather", "gather_weights")
    c_all, npre_g, npost_g, acw_g, bcw_g, cbin_g, cng_g, cnb_g = _unpack(small_g.reshape(N_DEV, -1), small_shapes)

    def cat_last(a):
        return jnp.moveaxis(a, 0, -2).reshape(a.shape[1:-1] + (N_DEV * a.shape[-1],))

    ada_b_mine = lax.dynamic_slice_in_dim(ada_b, me * ada_w.shape[-1], ada_w.shape[-1], axis=1)
    (mod_g,) = _exchange([_mod_part(c_all, ada_w, ada_b_mine, "mod_part")], "gather", "gather_mod")
    mod = cat_last(lax.dynamic_index_in_dim(mod_g, me, axis=2, keepdims=False)).reshape(2, 3, 3, D)

    p = dict(mod=mod, norm_pre=cat_last(npre_g), norm_post=cat_last(npost_g), w13g=w13g, w2g=w2g,
             ab_in=ab_in_g, ab_out=ab_out_g.reshape(D, D), c_in=c_in_g, c_out=c_out_g.reshape(D, D),
             a_conv_w=cat_last(acw_g), a_conv_b=a_conv_b[0], a_gate_w=a_gate_w[0], a_gate_b=a_gate_b[0], a_lam=a_lam[0],
             b_conv_w=cat_last(bcw_g), b_conv_b=b_conv_b[0], b_norm_g=b_norm_g[0], b_norm_b=b_norm_b[0],
             c_b_in=cat_last(cbin_g), c_norm_g=cat_last(cng_g), c_norm_b=cat_last(cnb_g), c_w_s=c_w_s[0], c_b_s=c_b_s[0])

    loss_blk, grad_x, g = _local_step(x[0], loss_target[0], p)
    loss = lax.psum(loss_blk[0, 0], ("x", "y", "c"))

    accs = g["accs"]
    dmod = jnp.stack([jnp.stack([accs[f"{l}{j}"][0:3] for j in range(3)]) for l in range(2)])
    dnpre = jnp.stack([jnp.stack([accs[f"{l}{j}"][3] for j in range(3)]) for l in range(2)])
    dnpost = jnp.stack([jnp.stack([accs[f"{l}{j}"][4] for j in range(3)]) for l in range(2)])
    sab = g["accs_ab"]
    half = W_A // 8
    dgate = g["gate"][0]
    dgw = jnp.stack([jnp.concatenate([dgate[half * hh:half * (hh + 1), half * hh:half * (hh + 1)],
                                      dgate[half * hh:half * (hh + 1), W_A + half * hh:W_A + half * (hh + 1)]], axis=1)
                     for hh in range(8)])
    dgb = jnp.concatenate([sab[5].reshape(8, half), sab[6].reshape(8, half)], axis=1)
    c_acc, c_dbin, c_dws, c_dbst = g["c_small"]
    red_shapes = [(2, 9216), (2, 3, D), (2, 3, D), (CONV_A, W_A), (W_A,), (8, half, 2 * half), (8, 2 * half), (W_A,),
                  (CONV_B, W_B), (W_B,), (W_B,), (W_B,), (2 * D,), (D,), (D,), (H_C, CHUNK, CHUNK), (H_C, CHUNK)]
    red = _pack([dmod.reshape(2, 9216), dnpre, dnpost, sab[0:4], sab[4], dgw, dgb, sab[7], g["dw31"][0:CONV_B], sab[8],
                 sab[9], sab[10], c_dbin[0], c_acc[5], c_acc[6], c_dws, jnp.transpose(c_dbst)], 1024)
    (red_g,) = _exchange([red], "gather", "gather_small_grads")
    red_sum = _sum_slots(red_g, "sum_small_grads").reshape(-1)
    (g_ada_b, g_npre, g_npost, g_acw, g_acb, g_agw, g_agb, g_alam, g_bcw, g_bcb, g_bng, g_bnb, g_cbin, g_cng, g_cnb,
     g_cws, g_cbs) = _unpack(red_sum, red_shapes)
    dmod_all = red_g.reshape(N_DEV, -1)[:, 0:2 * 9216].reshape(N_DEV, 2, 9216)
    ncol = ada_w.shape[-1]
    dmod_mine = jnp.moveaxis(lax.dynamic_slice_in_dim(dmod_all, me * ncol, ncol, axis=2), 0, 1)
    g_ada_w = _ada_w_grad(jnp.transpose(c_all), dmod_mine, "ada_w_grad")

    def mine(a, width):
        return lax.dynamic_slice_in_dim(a, me * width, width, axis=a.ndim - 1)

    small_grads = dict(
        ada_b=g_ada_b, norm_pre=mine(g_npre, 128), norm_post=mine(g_npost, 128), a_conv_w=mine(g_acw, 64)[None],
        a_conv_b=g_acb[None], a_gate_w=g_agw[None], a_gate_b=g_agb[None], a_lam=g_alam[None], b_conv_w=mine(g_bcw, 64)[None],
        b_conv_b=g_bcb[None], b_norm_g=g_bng[None], b_norm_b=g_bnb[None], c_b_in=mine(g_cbin, 256)[None],
        c_norm_g=mine(g_cng, 128)[None], c_norm_b=mine(g_cnb, 128)[None], c_w_s=g_cws[None], c_b_s=g_cbs[None])

    dw13 = jnp.stack([g[f"w13_{f}"] for f in range(4)], axis=1)
    dw2 = jnp.stack([g[f"w2_{f}"].reshape(N_DEV, W2_SHARD, D) for f in range(4)], axis=1)
    r_w13, r_w2, r_abin, r_about, r_cin, r_cout = _exchange(
        [dw13, dw2, g["ab_in"], g["ab_out"].reshape(N_DEV, D // N_DEV, D), g["c_in"], g["c_out"].reshape(N_DEV, D // N_DEV, D)],
        "scatter", "scatter_grads")
    big_partials = dict(ffn_w13=r_w13.reshape(N_DEV, 4 * D, FF_PAD), ffn_w2=r_w2.reshape(N_DEV, 4 * W2_SHARD, D),
                        ab_w_in=r_abin, ab_w_out=r_about, c_w_in=r_cin, c_w_out=r_cout, ada_w=_as2d(g_ada_w))

    grads, deltas, new_m, new_v = {}, {}, {}, {}
    for nm, gp in big_partials.items():
        shp = weights[nm].shape
        go, dl, mo, vo = _adamw_big(_as2d(weights[nm]), gp, _as2d(moms[nm]), _as2d(vars_[nm]), "adamw_" + nm)
        grads[nm], deltas[nm], new_m[nm], new_v[nm] = (a.reshape(shp) for a in (go, dl, mo, vo))
    snames = list(small_grads)
    dls, mos, vos = _adamw_small([_as2d(weights[nm]) for nm in snames], [_as2d(small_grads[nm]) for nm in snames],
                                 [_as2d(moms[nm]) for nm in snames], [_as2d(vars_[nm]) for nm in snames], "adamw_small")
    for k, nm in enumerate(snames):
        shp = weights[nm].shape
        grads[nm] = small_grads[nm].reshape(shp)
        deltas[nm], new_m[nm], new_v[nm] = dls[k].reshape(shp), mos[k].reshape(shp), vos[k].reshape(shp)

    return (loss, grad_x[None], *[grads[nm] for nm in names], *[deltas[nm] for nm in names],
            *[new_m[nm] for nm in names], *[new_v[nm] for nm in names])
```

```python
import functools
import math

import jax
import jax.numpy as jnp
from jax import lax
from jax.experimental import pallas as pl
from jax.experimental.pallas import tpu as pltpu

F32 = jnp.float32
BF = jnp.bfloat16

N_DEV = 8
D = 1024
EPS = 1e-6
D_FF = 2816
FF_SHARD = 704
FF_PAD = 768
FF_CHUNKS = 4
W2_SHARD = 352
W_A = 512
W_B = 512
CONV_A = 4
CONV_B = 31
HALO_A = 8
HALO_B = 32
LRU_C = 8.0
CHUNK = 128
H_C = 8
ADAM_LR = 0.001
ADAM_B1 = 0.9
ADAM_B2 = 0.999
ADAM_EPS = 1e-08
ADAM_WD = 0.01
ADAM_STEP = 10
VMEM_LIMIT = 56 * 1024 * 1024
GELU_C = math.sqrt(2.0 / math.pi)

TM_FFN = 512
TM_FFN_BWD = 256
TM_MIX = 256
TK_WGRAD = 2048
TK_WGRAD_WIDE = 1024


def _params(limit=VMEM_LIMIT):
    return pltpu.CompilerParams(dimension_semantics=("arbitrary",), vmem_limit_bytes=limit)


def _dot(a, b):
    return jnp.dot(a, b, preferred_element_type=F32)


def _dot_nt(a, b):
    return lax.dot_general(a, b, (((1,), (1,)), ((), ())), preferred_element_type=F32)


def _dot_tn(a, b):
    return lax.dot_general(a, b, (((0,), (0,)), ((), ())), preferred_element_type=F32)


def _sigmoid(x):
    return 1.0 / (1.0 + jnp.exp(-x))


def _gelu(x):
    t = jnp.tanh(GELU_C * (x + 0.044715 * x * x * x))
    return 0.5 * x * (1.0 + t), t


def _gelu_grad(x, t):
    return 0.5 * (1.0 + t) + 0.5 * x * (1.0 - t * t) * GELU_C * (1.0 + 3.0 * 0.044715 * x * x)


def _rms(x):
    r = lax.rsqrt(jnp.mean(x * x, axis=-1, keepdims=True) + EPS)
    return x * r, r


def _colsum(x):
    return jnp.sum(x, axis=0, keepdims=True)


def _pre_fwd(x, vec_ref):
    xn, r = _rms(x)
    n = xn * vec_ref[3:4, :]
    h = n * (1.0 + vec_ref[1:2, :]) + vec_ref[0:1, :]
    return h, xn, r, n


def _post_fwd(x, f, vec_ref, res_w):
    fn, _ = _rms(f)
    return x + (res_w * (1.0 + vec_ref[2:3, :])) * (fn * vec_ref[4:5, :])


def _post_bwd(dout, f, vec_ref, acc_ref, res_w):
    fn, r2 = _rms(f)
    post_g = vec_ref[4:5, :]
    dy = dout * (res_w * (1.0 + vec_ref[2:3, :]))
    acc_ref[2:3, :] += _colsum(dout * (res_w * (fn * post_g)))
    acc_ref[4:5, :] += _colsum(dy * fn)
    dfn = dy * post_g
    return r2 * (dfn - fn * jnp.mean(dfn * fn, axis=-1, keepdims=True))


def _pre_bwd(dout, dh, xn, r, n, vec_ref, acc_ref):
    acc_ref[0:1, :] += _colsum(dh)
    acc_ref[1:2, :] += _colsum(dh * n)
    dn = dh * (1.0 + vec_ref[1:2, :])
    acc_ref[3:4, :] += _colsum(dn * xn)
    dxn = dn * vec_ref[3:4, :]
    return dout + r * (dxn - xn * jnp.mean(dxn * xn, axis=-1, keepdims=True))


def _tile(tm, ncol):
    return pl.BlockSpec((tm, ncol), lambda i: (i, 0))


def _full(shape):
    return pl.BlockSpec(shape, lambda i: (0,) * len(shape))


def _any():
    return pl.BlockSpec(memory_space=pl.ANY)


def _load_ffn_weights(w13_hbm, w2_hbm, w13_v, w2_v, sems):
    w2_v[:, FF_SHARD:FF_PAD, :] = jnp.zeros((FF_CHUNKS, FF_PAD - FF_SHARD, D), BF)
    copies = [pltpu.make_async_copy(w13_hbm, w13_v, sems.at[0])]
    for j in range(N_DEV):
        copies.append(pltpu.make_async_copy(
            w2_hbm.at[j], w2_v.at[j // 2, pl.ds((j % 2) * W2_SHARD, W2_SHARD), :], sems.at[1 + j]))
    for cp in copies:
        cp.start()
    for cp in copies:
        cp.wait()


_FFN_SCRATCH = [pltpu.VMEM((N_DEV, D, FF_PAD), BF), pltpu.VMEM((FF_CHUNKS, FF_PAD, D), BF),
                pltpu.SemaphoreType.DMA((1 + N_DEV,))]


def _hosted(body, n_in, n_out, n_scratch, n_comm, mode, n_steps):
    if not n_comm:
        return body

    def hosted(*refs):
        ins, cin = refs[:n_in], refs[n_in:n_in + n_comm]
        outs, cout = refs[n_in + n_comm:n_in + n_comm + n_out], refs[n_in + n_comm + n_out:n_in + 2 * n_comm + n_out]
        scratch = refs[n_in + 2 * n_comm + n_out:]
        own, sems = scratch[:n_scratch], scratch[n_scratch:]

        @pl.when(pl.program_id(0) == 0)
        def _():
            _exchange_ops(cin, cout, sems, mode, "start")

        body(*ins, *outs, *own)

        @pl.when(pl.program_id(0) == n_steps - 1)
        def _():
            _exchange_ops(cin, cout, sems, mode, "wait")

    return hosted


def _ffn_fwd(x, vec, w13g, w2g, res_w, name, comm=()):
    t_len = x.shape[0]
    tm = min(TM_FFN, t_len)
    nc = len(comm)

    def body(x_ref, vec_ref, w13_hbm, w2_hbm, xo_ref, f_ref, gu_ref, w13_v, w2_v, sems):
        @pl.when(pl.program_id(0) == 0)
        def _():
            _load_ffn_weights(w13_hbm, w2_hbm, w13_v, w2_v, sems)

        x_t = x_ref[...]
        h, _, _, _ = _pre_fwd(x_t, vec_ref)
        hb = h.astype(BF)
        acc = jnp.zeros((tm, D), F32)
        for k in range(FF_CHUNKS):
            g = _dot(hb, w13_v[k])
            u = _dot(hb, w13_v[k + FF_CHUNKS])
            gu_ref[k] = g.astype(BF)
            gu_ref[k + FF_CHUNKS] = u.astype(BF)
            s = g * _sigmoid(g) * u
            acc = acc + _dot(s.astype(BF), w2_v[k])
        f_ref[...] = acc
        xo_ref[...] = _post_fwd(x_t, acc, vec_ref, res_w)

    nt = t_len // tm
    outs = pl.pallas_call(
        _hosted(body, 4, 3, 3, nc, "gather", nt), grid=(nt,),
        in_specs=[_tile(tm, D), _full((8, D)), _any(), _any()] + [_any()] * nc,
        out_specs=[_tile(tm, D), _tile(tm, D), pl.BlockSpec((N_DEV, tm, FF_PAD), lambda i: (0, i, 0))] + [_any()] * nc,
        out_shape=[jax.ShapeDtypeStruct((t_len, D), F32), jax.ShapeDtypeStruct((t_len, D), F32),
                   jax.ShapeDtypeStruct((N_DEV, t_len, FF_PAD), BF)] + _exchange_shapes(comm, "gather"),
        scratch_shapes=_FFN_SCRATCH + (_exchange_scratch(nc) if nc else []), compiler_params=_params(), name=name,
    )(x, vec, w13g, w2g, *comm)
    return outs[:3], outs[3:]


def _ffn_bwd(dout, x, fpre, gu, vec, w13g, w2g, res_w, name, comm=()):
    t_len = x.shape[0]
    tm = min(TM_FFN_BWD, t_len)
    nc = len(comm)

    def body(dout_ref, x_ref, f_ref, gu_ref, vec_ref, w13_hbm, w2_hbm,
             dx_ref, dgu_ref, s_ref, hb_ref, dfb_ref, acc_ref, w13_v, w2_v, sems):
        @pl.when(pl.program_id(0) == 0)
        def _():
            _load_ffn_weights(w13_hbm, w2_hbm, w13_v, w2_v, sems)
            acc_ref[...] = jnp.zeros((8, D), F32)

        dout_t = dout_ref[...]
        df = _post_bwd(dout_t, f_ref[...], vec_ref, acc_ref, res_w)
        dfb = df.astype(BF)
        dfb_ref[...] = dfb
        h, xn, r, n = _pre_fwd(x_ref[...], vec_ref)
        hb_ref[...] = h.astype(BF)
        dh = jnp.zeros((tm, D), F32)
        for k in range(FF_CHUNKS):
            g = gu_ref[k].astype(F32)
            u = gu_ref[k + FF_CHUNKS].astype(F32)
            sig = _sigmoid(g)
            sl = g * sig
            ds = _dot_nt(dfb, w2_v[k])
            dg = (ds * u * (sig * (1.0 + g * (1.0 - sig)))).astype(BF)
            du = (ds * sl).astype(BF)
            s_ref[k] = (sl * u).astype(BF)
            dgu_ref[k] = dg
            dgu_ref[k + FF_CHUNKS] = du
            dh = dh + _dot_nt(dg, w13_v[k]) + _dot_nt(du, w13_v[k + FF_CHUNKS])
        dx_ref[...] = _pre_bwd(dout_t, dh, xn, r, n, vec_ref, acc_ref)

    gu_spec = pl.BlockSpec((N_DEV, tm, FF_PAD), lambda i: (0, i, 0))
    nt = t_len // tm
    outs = pl.pallas_call(
        _hosted(body, 7, 6, 3, nc, "scatter", nt), grid=(nt,),
        in_specs=[_tile(tm, D), _tile(tm, D), _tile(tm, D), gu_spec, _full((8, D)), _any(), _any()] + [_any()] * nc,
        out_specs=[_tile(tm, D), gu_spec, pl.BlockSpec((FF_CHUNKS, tm, FF_PAD), lambda i: (0, i, 0)),
                   _tile(tm, D), _tile(tm, D), _full((8, D))] + [_any()] * nc,
        out_shape=[jax.ShapeDtypeStruct((t_len, D), F32), jax.ShapeDtypeStruct((N_DEV, t_len, FF_PAD), BF),
                   jax.ShapeDtypeStruct((FF_CHUNKS, t_len, FF_PAD), BF), jax.ShapeDtypeStruct((t_len, D), BF),
                   jax.ShapeDtypeStruct((t_len, D), BF), jax.ShapeDtypeStruct((8, D), F32)] + _exchange_shapes(comm, "scatter"),
        scratch_shapes=_FFN_SCRATCH + (_exchange_scratch(nc) if nc else []), compiler_params=_params(), name=name,
    )(dout, x, fpre, gu, vec, w13g, w2g, *comm)
    return outs[:6], outs[6:]


def _wgrad(a, b, j_count, m, n, a_mode, b_mode, out_rows, out_dtype, name, tk=TK_WGRAD, col_slots=1):
    t_len = a.shape[-2]
    tk = min(tk, t_len)
    nk = t_len // tk
    wn = n // col_slots

    def spec(mode, width):
        if mode == "stack":
            return pl.BlockSpec((None, tk, width), lambda j, t: (j, t, 0))
        return pl.BlockSpec((tk, width), lambda j, t: (t, 0))

    def body(a_ref, b_ref, o_ref, acc):
        t = pl.program_id(1)

        @pl.when(t == 0)
        def _():
            acc[...] = jnp.zeros((m, n), F32)

        acc[...] += _dot_tn(a_ref[...], b_ref[...])

        @pl.when(t == nk - 1)
        def _():
            if col_slots == 1:
                o_ref[...] = acc[0:out_rows, :].astype(out_dtype)
            else:
                for s in range(col_slots):
                    o_ref[s] = acc[0:out_rows, wn * s:wn * (s + 1)].astype(out_dtype)

    if col_slots == 1:
        out_spec = pl.BlockSpec((None, out_rows, n), lambda j, t: (j, 0, 0))
        out_shape = jax.ShapeDtypeStruct((j_count, out_rows, n), out_dtype)
    else:
        out_spec = pl.BlockSpec((col_slots, out_rows, wn), lambda j, t: (0, 0, 0))
        out_shape = jax.ShapeDtypeStruct((col_slots, out_rows, wn), out_dtype)
    return pl.pallas_call(
        body, grid=(j_count, nk),
        in_specs=[spec(a_mode, m), spec(b_mode, n)],
        out_specs=out_spec, out_shape=out_shape,
        scratch_shapes=[pltpu.VMEM((m, n), F32)],
        compiler_params=pltpu.CompilerParams(dimension_semantics=("arbitrary", "arbitrary"), vmem_limit_bytes=VMEM_LIMIT),
        name=name,
    )(a, b)


def _c_mask_weights(ws_ref, wsm, wsmt):
    row = lax.broadcasted_iota(jnp.int32, (CHUNK, CHUNK), 0)
    col = lax.broadcasted_iota(jnp.int32, (CHUNK, CHUNK), 1)
    for hh in range(H_C):
        w = jnp.where(row >= col, ws_ref[hh], 0.0)
        wsm[hh] = w.astype(BF)
        if wsmt is not None:
            wsmt[hh] = w.T.astype(BF)


def _c_inner(pre, cvec_ref, wsm, bst_ref, mix_sc, tm):
    z, t = _gelu(pre)
    u = z[:, 0:D]
    v = z[:, D:2 * D]
    mu = jnp.mean(v, axis=-1, keepdims=True)
    vc = v - mu
    rstd = lax.rsqrt(jnp.mean(vc * vc, axis=-1, keepdims=True) + EPS)
    vhat = vc * rstd
    vnb = (vhat * cvec_ref[0:1, :] + cvec_ref[1:2, :]).astype(BF)
    for nn in range(tm // CHUNK):
        for hh in range(H_C):
            rows = slice(CHUNK * nn, CHUNK * (nn + 1))
            cols = slice(CHUNK * hh, CHUNK * (hh + 1))
            mix_sc[rows, cols] = _dot(wsm[hh], vnb[rows, cols]) + bst_ref[:, hh:hh + 1]
    return u, t, rstd, vhat, vnb


def _mixc_fwd(x, vec, w_in, b_in, cvec, ws, bst, w_out, name):
    t_len = x.shape[0]
    tm = min(TM_MIX, t_len)

    def body(x_ref, vec_ref, win_ref, bin_ref, cvec_ref, ws_ref, bst_ref, wout_ref,
             xo_ref, f_ref, pre_ref, wsm, mix_sc):
        @pl.when(pl.program_id(0) == 0)
        def _():
            _c_mask_weights(ws_ref, wsm, None)

        x_t = x_ref[...]
        h, _, _, _ = _pre_fwd(x_t, vec_ref)
        hb = h.astype(BF)
        for j in range(N_DEV):
            cols = slice(256 * j, 256 * (j + 1))
            pre_ref[:, cols] = _dot(hb, win_ref[j]) + bin_ref[:, cols]
        u, _, _, _, _ = _c_inner(pre_ref[...], cvec_ref, wsm, bst_ref, mix_sc, tm)
        fpre = _dot((u * mix_sc[...]).astype(BF), wout_ref[...])
        f_ref[...] = fpre
        xo_ref[...] = _post_fwd(x_t, fpre, vec_ref, 1.0)

    return pl.pallas_call(
        body, grid=(t_len // tm,),
        in_specs=[_tile(tm, D), _full((8, D)), _full((N_DEV, D, 256)), _full((1, 2 * D)), _full((8, D)),
                  _full((H_C, CHUNK, CHUNK)), _full((CHUNK, H_C)), _full((D, D))],
        out_specs=[_tile(tm, D), _tile(tm, D), _tile(tm, 2 * D)],
        out_shape=[jax.ShapeDtypeStruct((t_len, D), F32), jax.ShapeDtypeStruct((t_len, D), F32),
                   jax.ShapeDtypeStruct((t_len, 2 * D), F32)],
        scratch_shapes=[pltpu.VMEM((H_C, CHUNK, CHUNK), BF), pltpu.VMEM((tm, D), F32)],
        compiler_params=_params(), name=name,
    )(x, vec, w_in, b_in, cvec, ws, bst, w_out)


def _mixc_bwd(dout, x, fpre, pre, vec, w_in, cvec, ws, bst, w_out, name):
    t_len = x.shape[0]
    tm = min(TM_MIX, t_len)
    nt = t_len // tm

    def body(dout_ref, x_ref, f_ref, pre_ref, vec_ref, win_ref, cvec_ref, ws_ref, bst_ref, wout_ref,
             dx_ref, dpre_ref, p_ref, hb_ref, dfb_ref, acc_ref, dbin_ref, dws_ref, dbst_ref,
             wsm, wsmt, mix_sc, dvn_sc, dmsum):
        i = pl.program_id(0)

        @pl.when(i == 0)
        def _():
            _c_mask_weights(ws_ref, wsm, wsmt)
            acc_ref[...] = jnp.zeros((8, D), F32)
            dbin_ref[...] = jnp.zeros((8, 2 * D), F32)
            dws_ref[...] = jnp.zeros((H_C, CHUNK, CHUNK), F32)
            dmsum[...] = jnp.zeros((CHUNK, D), F32)

        dout_t = dout_ref[...]
        df = _post_bwd(dout_t, f_ref[...], vec_ref, acc_ref, 1.0)
        dfb = df.astype(BF)
        dfb_ref[...] = dfb
        h, xn, r, n = _pre_fwd(x_ref[...], vec_ref)
        hb_ref[...] = h.astype(BF)
        pre_t = pre_ref[...]
        u, t, rstd, vhat, vnb = _c_inner(pre_t, cvec_ref, wsm, bst_ref, mix_sc, tm)
        mix = mix_sc[...]
        p_ref[...] = (u * mix).astype(BF)
        dp = _dot_nt(dfb, wout_ref[...])
        du = dp * mix
        dmix = dp * u
        dmb = dmix.astype(BF)
        for nn in range(tm // CHUNK):
            rows = slice(CHUNK * nn, CHUNK * (nn + 1))
            dmsum[...] += dmix[rows, :]
            for hh in range(H_C):
                cols = slice(CHUNK * hh, CHUNK * (hh + 1))
                dvn_sc[rows, cols] = _dot(wsmt[hh], dmb[rows, cols])
                dws_ref[hh] += _dot_nt(dmb[rows, cols], vnb[rows, cols])
        dvn = dvn_sc[...]
        acc_ref[5:6, :] += _colsum(dvn * vhat)
        acc_ref[6:7, :] += _colsum(dvn)
        dvhat = dvn * cvec_ref[0:1, :]
        dv = rstd * (dvhat - jnp.mean(dvhat, axis=-1, keepdims=True)
                     - vhat * jnp.mean(dvhat * vhat, axis=-1, keepdims=True))
        gg = _gelu_grad(pre_t, t)
        dpre_u = du * gg[:, 0:D]
        dpre_v = dv * gg[:, D:2 * D]
        dbin_ref[0:1, 0:D] += _colsum(dpre_u)
        dbin_ref[0:1, D:2 * D] += _colsum(dpre_v)
        dpre_ref[:, 0:D] = dpre_u.astype(BF)
        dpre_ref[:, D:2 * D] = dpre_v.astype(BF)
        dh = jnp.zeros((tm, D), F32)
        for j in range(N_DEV):
            dh = dh + _dot_nt(dpre_ref[:, 256 * j:256 * (j + 1)], win_ref[j])
        dx_ref[...] = _pre_bwd(dout_t, dh, xn, r, n, vec_ref, acc_ref)

        @pl.when(i == nt - 1)
        def _():
            row = lax.broadcasted_iota(jnp.int32, (CHUNK, CHUNK), 0)
            col = lax.broadcasted_iota(jnp.int32, (CHUNK, CHUNK), 1)
            for hh in range(H_C):
                dws_ref[hh] = jnp.where(row >= col, dws_ref[hh], 0.0)
                dbst_ref[:, hh:hh + 1] = jnp.sum(dmsum[:, CHUNK * hh:CHUNK * (hh + 1)], axis=1, keepdims=True)

    return pl.pallas_call(
        body, grid=(nt,),
        in_specs=[_tile(tm, D), _tile(tm, D), _tile(tm, D), _tile(tm, 2 * D), _full((8, D)), _full((N_DEV, D, 256)),
                  _full((8, D)), _full((H_C, CHUNK, CHUNK)), _full((CHUNK, H_C)), _full((D, D))],
        out_specs=[_tile(tm, D), _tile(tm, 2 * D), _tile(tm, D), _tile(tm, D), _tile(tm, D), _full((8, D)),
                   _full((8, 2 * D)), _full((H_C, CHUNK, CHUNK)), _full((CHUNK, H_C))],
        out_shape=[jax.ShapeDtypeStruct((t_len, D), F32), jax.ShapeDtypeStruct((t_len, 2 * D), BF),
                   jax.ShapeDtypeStruct((t_len, D), BF), jax.ShapeDtypeStruct((t_len, D), BF),
                   jax.ShapeDtypeStruct((t_len, D), BF), jax.ShapeDtypeStruct((8, D), F32),
                   jax.ShapeDtypeStruct((8, 2 * D), F32), jax.ShapeDtypeStruct((H_C, CHUNK, CHUNK), F32),
                   jax.ShapeDtypeStruct((CHUNK, H_C), F32)],
        scratch_shapes=[pltpu.VMEM((H_C, CHUNK, CHUNK), BF), pltpu.VMEM((H_C, CHUNK, CHUNK), BF),
                        pltpu.VMEM((tm, D), F32), pltpu.VMEM((tm, D), F32), pltpu.VMEM((CHUNK, D), F32)],
        compiler_params=_params(), name=name,
    )(dout, x, fpre, pre, vec, w_in, cvec, ws, bst, w_out)


def _gmean(x, g_ref):
    hi = x.astype(BF)
    lo = (x - hi.astype(F32)).astype(BF)
    return _dot(hi, g_ref[...]) + _dot(lo, g_ref[...])


def _log_sigmoid(lam):
    e = jnp.exp(-jnp.abs(lam))
    log1p = jnp.where(e < 1e-2, e * (1.0 - e * (0.5 - e * (1.0 / 3.0 - 0.25 * e))), jnp.log(1.0 + e))
    return jnp.minimum(lam, 0.0) - log1p


def _neg_expm1(y):
    series = -(y * (1.0 + y * (0.5 + y * (1.0 / 6.0 + y * (1.0 / 24.0 + y * (1.0 / 120.0))))))
    return jnp.where(y > -0.1, series, 1.0 - jnp.exp(y))


def _conv_causal(ext, taps_ref, bias, k_taps, halo, tm):
    acc = bias + taps_ref[0:1, :] * ext[halo - k_taps + 1:halo - k_taps + 1 + tm, :]
    for k in range(1, k_taps):
        off = halo - k_taps + 1 + k
        acc = acc + taps_ref[k:k + 1, :] * ext[off:off + tm, :]
    return acc


def _scan(a, u, tm, reverse):
    row = lax.broadcasted_iota(jnp.int32, (tm, W_A), 0)
    d = 1
    while d < tm:
        if reverse:
            keep = row < tm - d
            shift = tm - d
        else:
            keep = row >= d
            shift = d
        a_sh = jnp.where(keep, pltpu.roll(a, shift, 0), 1.0)
        u_sh = jnp.where(keep, pltpu.roll(u, shift, 0), 0.0)
        u = a * u_sh + u
        a = a * a_sh
        d *= 2
    return a, u


def _a_gates(xc, cv_ref, wr_ref, wi_ref):
    xcb = xc.astype(BF)
    r = _sigmoid(_dot(xcb, wr_ref[...]) + cv_ref[5:6, :])
    ig = _sigmoid(_dot(xcb, wi_ref[...]) + cv_ref[6:7, :])
    ls = _log_sigmoid(cv_ref[7:8, :])
    la = LRU_C * r * ls
    a = jnp.exp(la)
    m = jnp.sqrt(_neg_expm1(2.0 * la))
    return xcb, r, ig, ls, a, m


def _b_norm(vc, cv_ref, g_ref):
    mu = _gmean(vc, g_ref)
    dv = vc - mu
    rstd = lax.rsqrt(_gmean(dv * dv, g_ref) + EPS)
    vhat = dv * rstd
    vln = vhat * cv_ref[9:10, :] + cv_ref[10:11, :]
    return rstd, vhat, vln


def _mixab_fwd(x, vec, w_in, cv, w31, wr, wi, gmat, w_out, name):
    t_len = x.shape[0]
    tm = min(TM_MIX, t_len)

    def body(x_ref, vec_ref, win_ref, cv_ref, w31_ref, wr_ref, wi_ref, g_ref, wout_ref,
             xo_ref, f_ref, z_ref, hs_ref, ext_a, ext_b, hc):
        @pl.when(pl.program_id(0) == 0)
        def _():
            ext_a[0:HALO_A, :] = jnp.zeros((HALO_A, W_A), F32)
            ext_b[0:HALO_B, :] = jnp.zeros((HALO_B, W_B), F32)
            hc[...] = jnp.zeros((8, W_A), F32)

        x_t = x_ref[...]
        h, _, _, _ = _pre_fwd(x_t, vec_ref)
        hb = h.astype(BF)
        for j in range(N_DEV):
            z_ref[:, 256 * j:256 * (j + 1)] = _dot(hb, win_ref[j])
        ext_a[HALO_A:HALO_A + tm, :] = z_ref[:, W_A:2 * W_A]
        xc = _conv_causal(ext_a, cv_ref, cv_ref[4:5, :], CONV_A, HALO_A, tm)
        ext_a[0:HALO_A, :] = ext_a[tm:tm + HALO_A, :]
        _, _, ig, _, a, m = _a_gates(xc, cv_ref, wr_ref, wi_ref)
        a_cum, hloc = _scan(a, m * ig * xc, tm, False)
        hs = hloc + a_cum * hc[0:1, :]
        hs_ref[...] = hs
        hc[0:1, :] = hs[tm - 1:tm, :]
        gel, _ = _gelu(z_ref[:, 0:W_A])
        ya = hs * gel
        ext_b[HALO_B:HALO_B + tm, :] = z_ref[:, 2 * W_A:2 * W_A + W_B] * _sigmoid(z_ref[:, 2 * W_A + W_B:2 * W_A + 2 * W_B])
        vc = _conv_causal(ext_b, w31_ref, cv_ref[8:9, :], CONV_B, HALO_B, tm)
        ext_b[0:HALO_B, :] = ext_b[tm:tm + HALO_B, :]
        _, _, vln = _b_norm(vc, cv_ref, g_ref)
        yb = vln * _sigmoid(vln)
        fpre = _dot(ya.astype(BF), wout_ref[0:W_A, :]) + _dot(yb.astype(BF), wout_ref[W_A:W_A + W_B, :])
        f_ref[...] = fpre
        xo_ref[...] = _post_fwd(x_t, fpre, vec_ref, 1.0)

    return pl.pallas_call(
        body, grid=(t_len // tm,),
        in_specs=[_tile(tm, D), _full((8, D)), _full((N_DEV, D, 256)), _full((16, W_A)), _full((32, W_B)),
                  _full((W_A, W_A)), _full((W_A, W_A)), _full((W_B, W_B)), _full((D, D))],
        out_specs=[_tile(tm, D), _tile(tm, D), _tile(tm, 2 * D), _tile(tm, W_A)],
        out_shape=[jax.ShapeDtypeStruct((t_len, D), F32), jax.ShapeDtypeStruct((t_len, D), F32),
                   jax.ShapeDtypeStruct((t_len, 2 * D), F32), jax.ShapeDtypeStruct((t_len, W_A), F32)],
        scratch_shapes=[pltpu.VMEM((tm + HALO_A, W_A), F32), pltpu.VMEM((tm + HALO_B, W_B), F32), pltpu.VMEM((8, W_A), F32)],
        compiler_params=_params(), name=name,
    )(x, vec, w_in, cv, w31, wr, wi, gmat, w_out)


def _mixab_bwd(dout, x, fpre, z, hs, vec, w_in, cv, w31, wr, wi, gmat, w_out, name):
    t_len = x.shape[0]
    tm = min(TM_MIX, t_len)
    nt = t_len // tm

    def rev(i):
        return nt - 1 - i

    def rtile(ncol):
        return pl.BlockSpec((tm, ncol), lambda i: (rev(i), 0))

    def body(dout_ref, x_ref, f_ref, z_ref, zp_ref, hs_ref, hsp_ref, vec_ref, win_ref, cv_ref, w31_ref, wr_ref, wi_ref,
             g_ref, wout_ref,
             dx_ref, dz_ref, yab_ref, hb_ref, dfb_ref, xcb_ref, dri_ref, acc_ref, accs_ref, dw31_ref,
             ext_a, ext_b, ext_h, ext_dx, ext_dv, carry):
        i = pl.program_id(0)
        has_prev = (rev(i) > 0).astype(F32)

        @pl.when(i == 0)
        def _():
            acc_ref[...] = jnp.zeros((8, D), F32)
            accs_ref[...] = jnp.zeros((16, W_A), F32)
            dw31_ref[...] = jnp.zeros((32, W_B), F32)
            ext_dx[tm:tm + HALO_A, :] = jnp.zeros((HALO_A, W_A), F32)
            ext_dv[tm:tm + HALO_B, :] = jnp.zeros((HALO_B, W_B), F32)
            carry[...] = jnp.zeros((8, W_A), F32)

        dout_t = dout_ref[...]
        df = _post_bwd(dout_t, f_ref[...], vec_ref, acc_ref, 1.0)
        dfb = df.astype(BF)
        dfb_ref[...] = dfb
        h, xn, r_x, n = _pre_fwd(x_ref[...], vec_ref)
        hb_ref[...] = h.astype(BF)

        ag = z_ref[:, 0:W_A]
        bv = z_ref[:, 2 * W_A:2 * W_A + W_B]
        sg = _sigmoid(z_ref[:, 2 * W_A + W_B:2 * W_A + 2 * W_B])
        ext_a[0:HALO_A, :] = zp_ref[HALO_B - HALO_A:HALO_B, W_A:2 * W_A] * has_prev
        ext_a[HALO_A:HALO_A + tm, :] = z_ref[:, W_A:2 * W_A]
        xc = _conv_causal(ext_a, cv_ref, cv_ref[4:5, :], CONV_A, HALO_A, tm)
        xcb, r, ig, ls, a, m = _a_gates(xc, cv_ref, wr_ref, wi_ref)
        xcb_ref[...] = xcb
        hs_t = hs_ref[...]
        ext_h[0:8, :] = hsp_ref[...] * has_prev
        ext_h[8:8 + tm, :] = hs_t
        hprev = ext_h[7:7 + tm, :]
        gel, tg = _gelu(ag)
        ext_b[0:HALO_B, :] = (zp_ref[:, 2 * W_A:2 * W_A + W_B]
                              * _sigmoid(zp_ref[:, 2 * W_A + W_B:2 * W_A + 2 * W_B])) * has_prev
        ext_b[HALO_B:HALO_B + tm, :] = bv * sg
        vc = _conv_causal(ext_b, w31_ref, cv_ref[8:9, :], CONV_B, HALO_B, tm)
        rstd, vhat, vln = _b_norm(vc, cv_ref, g_ref)
        sv = _sigmoid(vln)
        yab_ref[:, 0:W_A] = (hs_t * gel).astype(BF)
        yab_ref[:, W_A:W_A + W_B] = (vln * sv).astype(BF)

        dya = _dot_nt(dfb, wout_ref[0:W_A, :])
        dyb = _dot_nt(dfb, wout_ref[W_A:W_A + W_B, :])

        dag = dya * hs_t * _gelu_grad(ag, tg)
        row = lax.broadcasted_iota(jnp.int32, (tm, W_A), 0)
        last = row == tm - 1
        a_next = jnp.where(last, 1.0, pltpu.roll(a, tm - 1, 0))
        u0 = dya * gel + jnp.where(last, carry[0:1, :], 0.0)
        _, dhs = _scan(a_next, u0, tm, True)
        carry[0:1, :] = a[0:1, :] * dhs[0:1, :]
        da = dhs * hprev
        dm = dhs * ig * xc
        di = dhs * m * xc
        dxc = dhs * m * ig
        dla = da * a - dm * (a * a) / m
        accs_ref[7:8, :] += _colsum(dla * r) * (LRU_C * _sigmoid(-cv_ref[7:8, :]))
        drp = (dla * (LRU_C * ls)) * r * (1.0 - r)
        dip = di * ig * (1.0 - ig)
        accs_ref[5:6, :] += _colsum(drp)
        accs_ref[6:7, :] += _colsum(dip)
        drpb = drp.astype(BF)
        dipb = dip.astype(BF)
        dri_ref[:, 0:W_A] = drpb
        dri_ref[:, W_A:2 * W_A] = dipb
        dxc = dxc + _dot_nt(drpb, wr_ref[...]) + _dot_nt(dipb, wi_ref[...])
        accs_ref[4:5, :] += _colsum(dxc)
        ext_dx[0:tm, :] = dxc
        dax = jnp.zeros((tm, W_A), F32)
        for k in range(CONV_A):
            accs_ref[k:k + 1, :] += _colsum(dxc * ext_a[HALO_A - CONV_A + 1 + k:HALO_A - CONV_A + 1 + k + tm, :])
            dax = dax + cv_ref[k:k + 1, :] * ext_dx[CONV_A - 1 - k:CONV_A - 1 - k + tm, :]
        ext_dx[tm:tm + HALO_A, :] = dxc[0:HALO_A, :]

        dvln = dyb * (sv * (1.0 + vln * (1.0 - sv)))
        accs_ref[9:10, :] += _colsum(dvln * vhat)
        accs_ref[10:11, :] += _colsum(dvln)
        dvhat = dvln * cv_ref[9:10, :]
        dvc = rstd * (dvhat - _gmean(dvhat, g_ref) - vhat * _gmean(dvhat * vhat, g_ref))
        accs_ref[8:9, :] += _colsum(dvc)
        ext_dv[0:tm, :] = dvc
        dvv = jnp.zeros((tm, W_B), F32)
        for k in range(CONV_B):
            dw31_ref[k:k + 1, :] += _colsum(dvc * ext_b[HALO_B - CONV_B + 1 + k:HALO_B - CONV_B + 1 + k + tm, :])
            dvv = dvv + w31_ref[k:k + 1, :] * ext_dv[CONV_B - 1 - k:CONV_B - 1 - k + tm, :]
        ext_dv[tm:tm + HALO_B, :] = dvc[0:HALO_B, :]

        dz_ref[:, 0:W_A] = dag.astype(BF)
        dz_ref[:, W_A:2 * W_A] = dax.astype(BF)
        dz_ref[:, 2 * W_A:2 * W_A + W_B] = (dvv * sg).astype(BF)
        dz_ref[:, 2 * W_A + W_B:2 * W_A + 2 * W_B] = (dvv * bv * sg * (1.0 - sg)).astype(BF)
        dh = jnp.zeros((tm, D), F32)
        for j in range(N_DEV):
            dh = dh + _dot_nt(dz_ref[:, 256 * j:256 * (j + 1)], win_ref[j])
        dx_ref[...] = _pre_bwd(dout_t, dh, xn, r_x, n, vec_ref, acc_ref)

    zp_spec = pl.BlockSpec((HALO_B, 2 * D), lambda i: (jnp.maximum(rev(i) * (tm // HALO_B) - 1, 0), 0))
    hsp_spec = pl.BlockSpec((8, W_A), lambda i: (jnp.maximum(rev(i) * (tm // 8) - 1, 0), 0))
    return pl.pallas_call(
        body, grid=(nt,),
        in_specs=[rtile(D), rtile(D), rtile(D), rtile(2 * D), zp_spec, rtile(W_A), hsp_spec, _full((8, D)),
                  _full((N_DEV, D, 256)), _full((16, W_A)), _full((32, W_B)), _full((W_A, W_A)), _full((W_A, W_A)),
                  _full((W_B, W_B)), _full((D, D))],
        out_specs=[rtile(D), rtile(2 * D), rtile(D), rtile(D), rtile(D), rtile(W_A), rtile(2 * W_A), _full((8, D)),
                   _full((16, W_A)), _full((32, W_B))],
        out_shape=[jax.ShapeDtypeStruct((t_len, D), F32), jax.ShapeDtypeStruct((t_len, 2 * D), BF),
                   jax.ShapeDtypeStruct((t_len, D), BF), jax.ShapeDtypeStruct((t_len, D), BF),
                   jax.ShapeDtypeStruct((t_len, D), BF), jax.ShapeDtypeStruct((t_len, W_A), BF),
                   jax.ShapeDtypeStruct((t_len, 2 * W_A), BF), jax.ShapeDtypeStruct((8, D), F32),
                   jax.ShapeDtypeStruct((16, W_A), F32), jax.ShapeDtypeStruct((32, W_B), F32)],
        scratch_shapes=[pltpu.VMEM((tm + HALO_A, W_A), F32), pltpu.VMEM((tm + HALO_B, W_B), F32),
                        pltpu.VMEM((tm + 8, W_A), F32), pltpu.VMEM((tm + HALO_A, W_A), F32),
                        pltpu.VMEM((tm + HALO_B, W_B), F32), pltpu.VMEM((8, W_A), F32)],
        compiler_params=_params(), name=name,
    )(dout, x, fpre, z, z, hs, hs, vec, w_in, cv, w31, wr, wi, gmat, w_out)


def _loss_head(y, tgt, name):
    t_len = y.shape[0]
    tm = min(TM_FFN, t_len)

    def body(y_ref, t_ref, dy_ref, loss_ref):
        @pl.when(pl.program_id(0) == 0)
        def _():
            loss_ref[...] = jnp.zeros((8, 128), F32)

        err = y_ref[...] - t_ref[...]
        dy_ref[...] = err * (1.0 / D)
        loss_ref[...] += jnp.sum(err * err) * (0.5 / D)

    return pl.pallas_call(
        body, grid=(t_len // tm,), in_specs=[_tile(tm, D), _tile(tm, D)],
        out_specs=[_tile(tm, D), _full((8, 128))],
        out_shape=[jax.ShapeDtypeStruct((t_len, D), F32), jax.ShapeDtypeStruct((8, 128), F32)],
        compiler_params=_params(), name=name,
    )(y, tgt)


def _vec(p, l, j):
    return jnp.concatenate([p["mod"][l, j], p["norm_pre"][l, j][None], p["norm_post"][l, j][None], jnp.zeros((3, D), F32)], 0)


def _ab_consts(p):
    gw = p["a_gate_w"]
    gb = p["a_gate_b"]
    half = W_A // 8
    wr = jax.scipy.linalg.block_diag(*[gw[hh, :, 0:half] for hh in range(8)]).astype(BF)
    wi = jax.scipy.linalg.block_diag(*[gw[hh, :, half:2 * half] for hh in range(8)]).astype(BF)
    rows = [p["a_conv_w"], p["a_conv_b"][None], gb[:, 0:half].reshape(1, W_A), gb[:, half:2 * half].reshape(1, W_A),
            p["a_lam"][None], p["b_conv_b"][None], p["b_norm_g"][None], p["b_norm_b"][None], jnp.zeros((5, W_A), F32)]
    cv = jnp.concatenate(rows, 0)
    w31 = jnp.concatenate([p["b_conv_w"], jnp.zeros((1, W_B), F32)], 0)
    grp = jnp.arange(W_B) // (W_B // 8)
    gmat = ((grp[:, None] == grp[None, :]).astype(F32) / (W_B // 8)).astype(BF)
    return cv, w31, wr, wi, gmat


SUBLAYERS = ("f0", "ab", "f1", "f2", "c", "f3")


def _local_step(x, tgt, p, plan=None):
    g = {}
    saved = []
    cur = x
    wsets = dict(p["wsets"])
    ab_c = _ab_consts(p)
    c_cvec = jnp.concatenate([p["c_norm_g"][None], p["c_norm_b"][None], jnp.zeros((6, D), F32)], 0)
    c_bst = jnp.transpose(p["c_b_s"])
    c_bin = p["c_b_in"][None]
    for s_idx, wname in enumerate(SUBLAYERS):
        l, j = divmod(s_idx, 3)
        vec = _vec(p, l, j)
        tag = f"l{l}s{j}"
        if j != 1:
            names = plan["gather"].get(s_idx, []) if plan else []
            comm = [a for nm in names for a in plan["shards"][nm]]
            (nxt, fpre, gu), got = _ffn_fwd(cur, vec, *wsets[wname], 0.5, "ffn_fwd_" + tag, comm=comm)
            for k, nm in enumerate(names):
                wsets[nm] = list(got[2 * k:2 * k + 2])
            saved.append((cur, fpre, gu, vec))
        elif l == 0:
            w_in, w_out = wsets[wname]
            nxt, fpre, z, hs = _mixab_fwd(cur, vec, w_in, *ab_c, w_out.reshape(D, D), "mixab_fwd_" + tag)
            saved.append((cur, fpre, z, hs, vec))
        else:
            w_in, w_out = wsets[wname]
            nxt, fpre, pre = _mixc_fwd(cur, vec, w_in, c_bin, c_cvec, p["c_w_s"], c_bst, w_out.reshape(D, D), "mixc_fwd_" + tag)
            saved.append((cur, fpre, pre, vec))
        cur = nxt
    dcur, loss_blk = _loss_head(cur, tgt, "loss_head")
    accs, pending, recv = {}, {}, {}
    for s_idx in reversed(range(6)):
        wname = SUBLAYERS[s_idx]
        l, j = divmod(s_idx, 3)
        tag = f"l{l}s{j}"
        sv = saved[s_idx]
        if j != 1:
            names = plan["scatter"].get(s_idx, []) if plan else []
            comm = [a for nm in names for a in pending.pop(nm)]
            xin, fpre, gu, vec = sv
            (dcur, dgu, s, hb, dfb, acc), got = _ffn_bwd(dcur, xin, fpre, gu, vec, *wsets[wname], 0.5, "ffn_bwd_" + tag, comm=comm)
            for k, nm in enumerate(names):
                recv[nm] = list(got[2 * k:2 * k + 2])
            dw13 = _wgrad(hb, dgu, N_DEV, D, FF_PAD, "share", "stack", D, BF, "wgrad_w13_" + tag)
            dw2 = _wgrad(s, dfb, FF_CHUNKS, FF_PAD, D, "stack", "share", FF_SHARD, BF, "wgrad_w2_" + tag)
            pending[wname] = [dw13, dw2.reshape(N_DEV, W2_SHARD, D)]
        elif l == 0:
            xin, fpre, z, hs, vec = sv
            w_in, w_out = wsets[wname]
            dcur, dz, yab, hb, dfb, xcb, dri, acc, accs_ab, dw31 = _mixab_bwd(
                dcur, xin, fpre, z, hs, vec, w_in, *ab_c, w_out.reshape(D, D), "mixab_bwd_" + tag)
            d_in = _wgrad(hb, dz, 1, D, 2 * D, "share", "share", D, BF, "wgrad_ab_in", tk=TK_WGRAD_WIDE, col_slots=N_DEV)
            d_out = _wgrad(yab, dfb, 1, D, D, "share", "share", D, BF, "wgrad_ab_out")
            pending[wname] = [d_in, d_out.reshape(N_DEV, D // N_DEV, D)]
            g["gate"] = _wgrad(xcb, dri, 1, W_A, 2 * W_A, "share", "share", W_A, F32, "wgrad_gate")
            g["accs_ab"] = accs_ab
            g["dw31"] = dw31
        else:
            xin, fpre, pre, vec = sv
            w_in, w_out = wsets[wname]
            dcur, dpre, pb, hb, dfb, acc, dbin, dws, dbst = _mixc_bwd(
                dcur, xin, fpre, pre, vec, w_in, c_cvec, p["c_w_s"], c_bst, w_out.reshape(D, D), "mixc_bwd_" + tag)
            d_in = _wgrad(hb, dpre, 1, D, 2 * D, "share", "share", D, BF, "wgrad_c_in", tk=TK_WGRAD_WIDE, col_slots=N_DEV)
            d_out = _wgrad(pb, dfb, 1, D, D, "share", "share", D, BF, "wgrad_c_out")
            pending[wname] = [d_in, d_out.reshape(N_DEV, D // N_DEV, D)]
            g["c_small"] = (acc, dbin, dws, dbst)
        accs[f"{l}{j}"] = acc
    g["accs"] = accs
    g["pending"] = pending
    g["recv"] = recv
    return loss_blk, dcur, g


def _exchange_ops(ins, outs, sems, mode, action):
    send_sems, recv_sems, loc_sems = sems
    n = len(ins)
    x, y, c = lax.axis_index("x"), lax.axis_index("y"), lax.axis_index("c")
    me = 4 * x + 2 * y + c

    def src(i, dev):
        return ins[i] if mode == "gather" else ins[i].at[dev]

    for i in range(n):
        cp = pltpu.make_async_copy(src(i, me), outs[i].at[me], loc_sems.at[i])
        if action == "start":
            cp.start()
        else:
            cp.wait()
    for mask in range(1, N_DEV):
        px = 1 - x if mask & 4 else x
        py = 1 - y if mask & 2 else y
        pc = 1 - c if mask & 1 else c
        peer = 4 * px + 2 * py + pc
        for i in range(n):
            k = i * (N_DEV - 1) + mask - 1
            cp = pltpu.make_async_remote_copy(
                src_ref=src(i, peer), dst_ref=outs[i].at[me if action == "start" else peer],
                send_sem=send_sems.at[k], recv_sem=recv_sems.at[k],
                device_id=(px, py, pc), device_id_type=pl.DeviceIdType.MESH)
            if action == "start":
                cp.start()
            else:
                cp.wait()


def _exchange_scratch(n):
    return [pltpu.SemaphoreType.DMA((n * (N_DEV - 1),)), pltpu.SemaphoreType.DMA((n * (N_DEV - 1),)),
            pltpu.SemaphoreType.DMA((n,))]


def _exchange_shapes(arrays, mode):
    return [jax.ShapeDtypeStruct(((N_DEV,) + a.shape) if mode == "gather" else a.shape, a.dtype) for a in arrays]


def _exchange(arrays, mode, name):
    n = len(arrays)

    def body(*refs):
        ins, outs, sems = refs[:n], refs[n:2 * n], refs[2 * n:]
        _exchange_ops(ins, outs, sems, mode, "start")
        _exchange_ops(ins, outs, sems, mode, "wait")

    return pl.pallas_call(
        body, in_specs=[pl.BlockSpec(memory_space=pl.ANY)] * n, out_specs=[pl.BlockSpec(memory_space=pl.ANY)] * n,
        out_shape=_exchange_shapes(arrays, mode), scratch_shapes=_exchange_scratch(n), name=name,
    )(*arrays)


def _sum_slots(a, name):
    def body(a_ref, o_ref):
        acc = a_ref[0]
        for s in range(1, N_DEV):
            acc = acc + a_ref[s]
        o_ref[...] = acc

    return pl.pallas_call(body, out_shape=jax.ShapeDtypeStruct(a.shape[1:], F32), name=name,
                          compiler_params=pltpu.CompilerParams(vmem_limit_bytes=VMEM_LIMIT))(a)


def _pack(pieces, mult):
    flat = jnp.concatenate([q.reshape(-1).astype(F32) for q in pieces])
    size = -(-flat.shape[0] // mult) * mult
    return jnp.pad(flat, (0, size - flat.shape[0])).reshape(size // 128, 128)


def _unpack(flat, shapes):
    out, off = [], 0
    for shp in shapes:
        size = math.prod(shp)
        out.append(flat[..., off:off + size].reshape(flat.shape[:-1] + tuple(shp)))
        off += size
    return out


def _mod_part(c_all, ada_w, ada_b_mine, name):
    cols = ada_w.shape[-1]

    def body(c_ref, w_ref, b_ref, o_ref):
        cv = c_ref[...]
        ca = cv * _sigmoid(cv)
        for l in range(2):
            o_ref[l] = jnp.dot(ca, w_ref[l], preferred_element_type=F32, precision=lax.Precision.HIGHEST) + b_ref[l:l + 1, :]

    return pl.pallas_call(body, out_shape=jax.ShapeDtypeStruct((2, N_DEV, cols), F32), name=name,
                          compiler_params=pltpu.CompilerParams(vmem_limit_bytes=VMEM_LIMIT))(c_all, ada_w, ada_b_mine)


def _ada_w_grad(c_all_t, dmod_mine, name):
    cols = dmod_mine.shape[-1]

    def body(ct_ref, d_ref, o_ref):
        cv = ct_ref[...]
        ca = cv * _sigmoid(cv)
        for l in range(2):
            acc = ca[:, 0:1] * d_ref[l, 0:1, :]
            for b in range(1, N_DEV):
                acc = acc + ca[:, b:b + 1] * d_ref[l, b:b + 1, :]
            o_ref[l] = acc

    return pl.pallas_call(body, out_shape=jax.ShapeDtypeStruct((2, D, cols), F32), name=name,
                          compiler_params=pltpu.CompilerParams(vmem_limit_bytes=VMEM_LIMIT))(c_all_t, dmod_mine)


def _adamw_math(w, g, m, v):
    m2 = ADAM_B1 * m + (1.0 - ADAM_B1) * g
    v2 = ADAM_B2 * v + (1.0 - ADAM_B2) * (g * g)
    m_hat = m2 / (1.0 - ADAM_B1 ** ADAM_STEP)
    v_hat = v2 / (1.0 - ADAM_B2 ** ADAM_STEP)
    delta = -ADAM_LR * (m_hat / (jnp.sqrt(v_hat) + ADAM_EPS) + ADAM_WD * w)
    return delta, m2, v2


def _adamw_big(w, g, m, v, name):
    rows, cols = w.shape
    br = next(b for b in (512, 352, 256, 128, 64, 32, 16, 8) if rows % b == 0)
    partial = g.ndim == 3

    def body(w_ref, g_ref, m_ref, v_ref, go_ref, d_ref, mo_ref, vo_ref):
        if partial:
            gsum = g_ref[0, :, 0:cols].astype(F32)
            for s in range(1, N_DEV):
                gsum = gsum + g_ref[s, :, 0:cols].astype(F32)
        else:
            gsum = g_ref[...]
        go_ref[...] = gsum
        d_ref[...], mo_ref[...], vo_ref[...] = _adamw_math(w_ref[...], gsum, m_ref[...], v_ref[...])

    blk = _tile(br, cols)
    g_spec = pl.BlockSpec((N_DEV, br, g.shape[-1]), lambda i: (0, i, 0)) if partial else blk
    shp = jax.ShapeDtypeStruct((rows, cols), F32)
    return pl.pallas_call(body, grid=(rows // br,), in_specs=[blk, g_spec, blk, blk], out_specs=[blk] * 4,
                          out_shape=[shp] * 4, compiler_params=_params(), name=name)(w, g, m, v)


def _adamw_small(ws, gs, ms, vs, name):
    n = len(ws)

    def body(*refs):
        for i in range(n):
            w_ref, g_ref, m_ref, v_ref = (refs[k * n + i] for k in range(4))
            d_ref, mo_ref, vo_ref = (refs[(4 + k) * n + i] for k in range(3))
            d_ref[...], mo_ref[...], vo_ref[...] = _adamw_math(w_ref[...], g_ref[...], m_ref[...], v_ref[...])

    shapes = [jax.ShapeDtypeStruct(w.shape, F32) for w in ws]
    outs = pl.pallas_call(body, out_shape=shapes * 3, name=name,
                          compiler_params=pltpu.CompilerParams(vmem_limit_bytes=VMEM_LIMIT))(*ws, *gs, *ms, *vs)
    return outs[:n], outs[n:2 * n], outs[2 * n:]


def _as2d(a):
    return a.reshape(-1, a.shape[-1])


def kernel(x, c, ada_w, ada_b, norm_pre, norm_post, ffn_w13, ffn_w2, ab_w_in, a_conv_w, a_conv_b, a_gate_w, a_gate_b, a_lam, b_conv_w, b_conv_b, b_norm_g, b_norm_b, ab_w_out, c_w_in, c_b_in, c_norm_g, c_norm_b, c_w_s, c_b_s, c_w_out, loss_target, m_ada_w, m_ada_b, m_norm_pre, m_norm_post, m_ffn_w13, m_ffn_w2, m_ab_w_in, m_a_conv_w, m_a_conv_b, m_a_gate_w, m_a_gate_b, m_a_lam, m_b_conv_w, m_b_conv_b, m_b_norm_g, m_b_norm_b, m_ab_w_out, m_c_w_in, m_c_b_in, m_c_norm_g, m_c_norm_b, m_c_w_s, m_c_b_s, m_c_w_out, v_ada_w, v_ada_b, v_norm_pre, v_norm_post, v_ffn_w13, v_ffn_w2, v_ab_w_in, v_a_conv_w, v_a_conv_b, v_a_gate_w, v_a_gate_b, v_a_lam, v_b_conv_w, v_b_conv_b, v_b_norm_g, v_b_norm_b, v_ab_w_out, v_c_w_in, v_c_b_in, v_c_norm_g, v_c_norm_b, v_c_w_s, v_c_b_s, v_c_w_out):
    me = 4 * lax.axis_index("x") + 2 * lax.axis_index("y") + lax.axis_index("c")
    weights = dict(ada_w=ada_w, ada_b=ada_b, norm_pre=norm_pre, norm_post=norm_post, ffn_w13=ffn_w13, ffn_w2=ffn_w2,
                   ab_w_in=ab_w_in, a_conv_w=a_conv_w, a_conv_b=a_conv_b, a_gate_w=a_gate_w, a_gate_b=a_gate_b, a_lam=a_lam,
                   b_conv_w=b_conv_w, b_conv_b=b_conv_b, b_norm_g=b_norm_g, b_norm_b=b_norm_b, ab_w_out=ab_w_out,
                   c_w_in=c_w_in, c_b_in=c_b_in, c_norm_g=c_norm_g, c_norm_b=c_norm_b, c_w_s=c_w_s, c_b_s=c_b_s, c_w_out=c_w_out)
    moms = dict(ada_w=m_ada_w, ada_b=m_ada_b, norm_pre=m_norm_pre, norm_post=m_norm_post, ffn_w13=m_ffn_w13, ffn_w2=m_ffn_w2,
                ab_w_in=m_ab_w_in, a_conv_w=m_a_conv_w, a_conv_b=m_a_conv_b, a_gate_w=m_a_gate_w, a_gate_b=m_a_gate_b,
                a_lam=m_a_lam, b_conv_w=m_b_conv_w, b_conv_b=m_b_conv_b, b_norm_g=m_b_norm_g, b_norm_b=m_b_norm_b,
                ab_w_out=m_ab_w_out, c_w_in=m_c_w_in, c_b_in=m_c_b_in, c_norm_g=m_c_norm_g, c_norm_b=m_c_norm_b,
                c_w_s=m_c_w_s, c_b_s=m_c_b_s, c_w_out=m_c_w_out)
    vars_ = dict(ada_w=v_ada_w, ada_b=v_ada_b, norm_pre=v_norm_pre, norm_post=v_norm_post, ffn_w13=v_ffn_w13, ffn_w2=v_ffn_w2,
                 ab_w_in=v_ab_w_in, a_conv_w=v_a_conv_w, a_conv_b=v_a_conv_b, a_gate_w=v_a_gate_w, a_gate_b=v_a_gate_b,
                 a_lam=v_a_lam, b_conv_w=v_b_conv_w, b_conv_b=v_b_conv_b, b_norm_g=v_b_norm_g, b_norm_b=v_b_norm_b,
                 ab_w_out=v_ab_w_out, c_w_in=v_c_w_in, c_b_in=v_c_b_in, c_norm_g=v_c_norm_g, c_norm_b=v_c_norm_b,
                 c_w_s=v_c_w_s, c_b_s=v_c_b_s, c_w_out=v_c_w_out)
    names = list(weights)

    w13b = jnp.pad(ffn_w13.astype(BF).reshape(4, D, FF_SHARD), ((0, 0), (0, 0), (0, FF_PAD - FF_SHARD)))
    small_shapes = [(D,), (2, 3, 128), (2, 3, 128), (CONV_A, 64), (CONV_B, 64), (256,), (128,), (128,)]
    small = _pack([c, norm_pre, norm_post, a_conv_w, b_conv_w, c_b_in, c_norm_g, c_norm_b], 1024)
    w2b = ffn_w2.astype(BF).reshape(4, W2_SHARD, D)
    shards = {f"f{f}": [w13b[f], w2b[f]] for f in range(4)}
    shards["ab"] = [ab_w_in[0].astype(BF), ab_w_out[0].astype(BF)]
    shards["c"] = [c_w_in[0].astype(BF), c_w_out[0].astype(BF)]
    w13g0, w2g0, small_g = _exchange(shards["f0"] + [small], "gather", "gather_first")
    plan = dict(shards=shards, gather={0: ["ab", "f1"], 2: ["f2"], 3: ["c", "f3"]},
                scatter={3: ["f3", "c"], 2: ["f2"], 0: ["f1", "ab"]})
    c_all, npre_g, npost_g, acw_g, bcw_g, cbin_g, cng_g, cnb_g = _unpack(small_g.reshape(N_DEV, -1), small_shapes)

    def cat_last(a):
        return jnp.moveaxis(a, 0, -2).reshape(a.shape[1:-1] + (N_DEV * a.shape[-1],))

    ada_b_mine = lax.dynamic_slice_in_dim(ada_b, me * ada_w.shape[-1], ada_w.shape[-1], axis=1)
    (mod_g,) = _exchange([_mod_part(c_all, ada_w, ada_b_mine, "mod_part")], "gather", "gather_mod")
    mod = cat_last(lax.dynamic_index_in_dim(mod_g, me, axis=2, keepdims=False)).reshape(2, 3, 3, D)

    p = dict(mod=mod, norm_pre=cat_last(npre_g), norm_post=cat_last(npost_g), wsets={"f0": [w13g0, w2g0]},
             a_conv_w=cat_last(acw_g), a_conv_b=a_conv_b[0], a_gate_w=a_gate_w[0], a_gate_b=a_gate_b[0], a_lam=a_lam[0],
             b_conv_w=cat_last(bcw_g), b_conv_b=b_conv_b[0], b_norm_g=b_norm_g[0], b_norm_b=b_norm_b[0],
             c_b_in=cat_last(cbin_g), c_norm_g=cat_last(cng_g), c_norm_b=cat_last(cnb_g), c_w_s=c_w_s[0], c_b_s=c_b_s[0])

    loss_blk, grad_x, g = _local_step(x[0], loss_target[0], p, plan)
    loss = lax.psum(loss_blk[0, 0], ("x", "y", "c"))

    accs = g["accs"]
    dmod = jnp.stack([jnp.stack([accs[f"{l}{j}"][0:3] for j in range(3)]) for l in range(2)])
    dnpre = jnp.stack([jnp.stack([accs[f"{l}{j}"][3] for j in range(3)]) for l in range(2)])
    dnpost = jnp.stack([jnp.stack([accs[f"{l}{j}"][4] for j in range(3)]) for l in range(2)])
    sab = g["accs_ab"]
    half = W_A // 8
    dgate = g["gate"][0]
    dgw = jnp.stack([jnp.concatenate([dgate[half * hh:half * (hh + 1), half * hh:half * (hh + 1)],
                                      dgate[half * hh:half * (hh + 1), W_A + half * hh:W_A + half * (hh + 1)]], axis=1)
                     for hh in range(8)])
    dgb = jnp.concatenate([sab[5].reshape(8, half), sab[6].reshape(8, half)], axis=1)
    c_acc, c_dbin, c_dws, c_dbst = g["c_small"]
    red_shapes = [(2, 9216), (2, 3, D), (2, 3, D), (CONV_A, W_A), (W_A,), (8, half, 2 * half), (8, 2 * half), (W_A,),
                  (CONV_B, W_B), (W_B,), (W_B,), (W_B,), (2 * D,), (D,), (D,), (H_C, CHUNK, CHUNK), (H_C, CHUNK)]
    red = _pack([dmod.reshape(2, 9216), dnpre, dnpost, sab[0:4], sab[4], dgw, dgb, sab[7], g["dw31"][0:CONV_B], sab[8],
                 sab[9], sab[10], c_dbin[0], c_acc[5], c_acc[6], c_dws, jnp.transpose(c_dbst)], 1024)
    (red_g,) = _exchange([red], "gather", "gather_small_grads")
    red_sum = _sum_slots(red_g, "sum_small_grads").reshape(-1)
    (g_ada_b, g_npre, g_npost, g_acw, g_acb, g_agw, g_agb, g_alam, g_bcw, g_bcb, g_bng, g_bnb, g_cbin, g_cng, g_cnb,
     g_cws, g_cbs) = _unpack(red_sum, red_shapes)
    dmod_all = red_g.reshape(N_DEV, -1)[:, 0:2 * 9216].reshape(N_DEV, 2, 9216)
    ncol = ada_w.shape[-1]
    dmod_mine = jnp.moveaxis(lax.dynamic_slice_in_dim(dmod_all, me * ncol, ncol, axis=2), 0, 1)
    g_ada_w = _ada_w_grad(jnp.transpose(c_all), dmod_mine, "ada_w_grad")

    def mine(a, width):
        return lax.dynamic_slice_in_dim(a, me * width, width, axis=a.ndim - 1)

    small_grads = dict(
        ada_b=g_ada_b, norm_pre=mine(g_npre, 128), norm_post=mine(g_npost, 128), a_conv_w=mine(g_acw, 64)[None],
        a_conv_b=g_acb[None], a_gate_w=g_agw[None], a_gate_b=g_agb[None], a_lam=g_alam[None], b_conv_w=mine(g_bcw, 64)[None],
        b_conv_b=g_bcb[None], b_norm_g=g_bng[None], b_norm_b=g_bnb[None], c_b_in=mine(g_cbin, 256)[None],
        c_norm_g=mine(g_cng, 128)[None], c_norm_b=mine(g_cnb, 128)[None], c_w_s=g_cws[None], c_b_s=g_cbs[None])

    recv = dict(g["recv"])
    left = sorted(g["pending"])
    got = _exchange([a for nm in left for a in g["pending"][nm]], "scatter", "scatter_last")
    for k, nm in enumerate(left):
        recv[nm] = list(got[2 * k:2 * k + 2])
    r_w13 = jnp.stack([recv[f"f{f}"][0] for f in range(4)], axis=1)
    r_w2 = jnp.stack([recv[f"f{f}"][1] for f in range(4)], axis=1)
    big_partials = dict(ffn_w13=r_w13.reshape(N_DEV, 4 * D, FF_PAD), ffn_w2=r_w2.reshape(N_DEV, 4 * W2_SHARD, D),
                        ab_w_in=recv["ab"][0], ab_w_out=recv["ab"][1], c_w_in=recv["c"][0], c_w_out=recv["c"][1],
                        ada_w=_as2d(g_ada_w))

    grads, deltas, new_m, new_v = {}, {}, {}, {}
    for nm, gp in big_partials.items():
        shp = weights[nm].shape
        go, dl, mo, vo = _adamw_big(_as2d(weights[nm]), gp, _as2d(moms[nm]), _as2d(vars_[nm]), "adamw_" + nm)
        grads[nm], deltas[nm], new_m[nm], new_v[nm] = (a.reshape(shp) for a in (go, dl, mo, vo))
    snames = list(small_grads)
    dls, mos, vos = _adamw_small([_as2d(weights[nm]) for nm in snames], [_as2d(small_grads[nm]) for nm in snames],
                                 [_as2d(moms[nm]) for nm in snames], [_as2d(vars_[nm]) for nm in snames], "adamw_small")
    for k, nm in enumerate(snames):
        shp = weights[nm].shape
        grads[nm] = small_grads[nm].reshape(shp)
        deltas[nm], new_m[nm], new_v[nm] = dls[k].reshape(shp), mos[k].reshape(shp), vos[k].reshape(shp)

    return (loss, grad_x[None], *[grads[nm] for nm in names], *[deltas[nm] for nm in names],
            *[new_m[nm] for nm in names], *[new_v[nm] for nm in names])
```

```python
import functools
import math

import jax
import jax.numpy as jnp
from jax import lax
from jax.experimental import pallas as pl
from jax.experimental.pallas import tpu as pltpu

F32 = jnp.float32
BF = jnp.bfloat16

N_DEV = 8
D = 1024
EPS = 1e-6
D_FF = 2816
FF_SHARD = 704
FF_PAD = 768
FF_CHUNKS = 4
W2_SHARD = 352
W_A = 512
W_B = 512
CONV_A = 4
CONV_B = 31
HALO_A = 8
HALO_B = 32
LRU_C = 8.0
CHUNK = 128
H_C = 8
ADAM_LR = 0.001
ADAM_B1 = 0.9
ADAM_B2 = 0.999
ADAM_EPS = 1e-08
ADAM_WD = 0.01
ADAM_STEP = 10
VMEM_LIMIT = 56 * 1024 * 1024
GELU_C = math.sqrt(2.0 / math.pi)

TM_FFN = 512
TM_FFN_BWD = 256
TM_MIX = 256
TK_WGRAD = 2048
TK_WGRAD_WIDE = 1024


def _params(limit=VMEM_LIMIT):
    return pltpu.CompilerParams(dimension_semantics=("arbitrary",), vmem_limit_bytes=limit)


def _dot(a, b):
    return jnp.dot(a, b, preferred_element_type=F32)


def _dot_nt(a, b):
    return lax.dot_general(a, b, (((1,), (1,)), ((), ())), preferred_element_type=F32)


def _dot_tn(a, b):
    return lax.dot_general(a, b, (((0,), (0,)), ((), ())), preferred_element_type=F32)


def _sigmoid(x):
    return 1.0 / (1.0 + jnp.exp(-x))


def _gelu(x):
    t = jnp.tanh(GELU_C * (x + 0.044715 * x * x * x))
    return 0.5 * x * (1.0 + t), t


def _gelu_grad(x, t):
    return 0.5 * (1.0 + t) + 0.5 * x * (1.0 - t * t) * GELU_C * (1.0 + 3.0 * 0.044715 * x * x)


def _rms(x):
    r = lax.rsqrt(jnp.mean(x * x, axis=-1, keepdims=True) + EPS)
    return x * r, r


def _colsum(x):
    return jnp.sum(x, axis=0, keepdims=True)


def _pre_fwd(x, vec_ref):
    xn, r = _rms(x)
    n = xn * vec_ref[3:4, :]
    h = n * (1.0 + vec_ref[1:2, :]) + vec_ref[0:1, :]
    return h, xn, r, n


def _post_fwd(x, f, vec_ref, res_w):
    fn, _ = _rms(f)
    return x + (res_w * (1.0 + vec_ref[2:3, :])) * (fn * vec_ref[4:5, :])


def _post_bwd(dout, f, vec_ref, acc_ref, res_w):
    fn, r2 = _rms(f)
    post_g = vec_ref[4:5, :]
    dy = dout * (res_w * (1.0 + vec_ref[2:3, :]))
    acc_ref[2:3, :] += _colsum(dout * (res_w * (fn * post_g)))
    acc_ref[4:5, :] += _colsum(dy * fn)
    dfn = dy * post_g
    return r2 * (dfn - fn * jnp.mean(dfn * fn, axis=-1, keepdims=True))


def _pre_bwd(dout, dh, xn, r, n, vec_ref, acc_ref):
    acc_ref[0:1, :] += _colsum(dh)
    acc_ref[1:2, :] += _colsum(dh * n)
    dn = dh * (1.0 + vec_ref[1:2, :])
    acc_ref[3:4, :] += _colsum(dn * xn)
    dxn = dn * vec_ref[3:4, :]
    return dout + r * (dxn - xn * jnp.mean(dxn * xn, axis=-1, keepdims=True))


def _tile(tm, ncol):
    return pl.BlockSpec((tm, ncol), lambda i: (i, 0))


def _full(shape):
    return pl.BlockSpec(shape, lambda i: (0,) * len(shape))


def _any():
    return pl.BlockSpec(memory_space=pl.ANY)


def _load_ffn_weights(w13_hbm, w2_hbm, w13_v, w2_v, sems):
    w2_v[:, FF_SHARD:FF_PAD, :] = jnp.zeros((FF_CHUNKS, FF_PAD - FF_SHARD, D), BF)
    copies = [pltpu.make_async_copy(w13_hbm, w13_v, sems.at[0])]
    for j in range(N_DEV):
        copies.append(pltpu.make_async_copy(
            w2_hbm.at[j], w2_v.at[j // 2, pl.ds((j % 2) * W2_SHARD, W2_SHARD), :], sems.at[1 + j]))
    for cp in copies:
        cp.start()
    for cp in copies:
        cp.wait()


_FFN_SCRATCH = [pltpu.VMEM((N_DEV, D, FF_PAD), BF), pltpu.VMEM((FF_CHUNKS, FF_PAD, D), BF),
                pltpu.SemaphoreType.DMA((1 + N_DEV,))]


def _hosted(body, n_in, n_out, n_scratch, n_comm, mode, n_steps):
    if not n_comm:
        return body

    def hosted(*refs):
        ins, cin = refs[:n_in], refs[n_in:n_in + n_comm]
        outs, cout = refs[n_in + n_comm:n_in + n_comm + n_out], refs[n_in + n_comm + n_out:n_in + 2 * n_comm + n_out]
        scratch = refs[n_in + 2 * n_comm + n_out:]
        own, sems = scratch[:n_scratch], scratch[n_scratch:]

        @pl.when(pl.program_id(0) == 0)
        def _():
            _exchange_ops(cin, cout, sems, mode, "start")

        body(*ins, *outs, *own)

        @pl.when(pl.program_id(0) == n_steps - 1)
        def _():
            _exchange_ops(cin, cout, sems, mode, "wait")

    return hosted


def _ffn_fwd(x, vec, w13g, w2g, res_w, name, comm=()):
    t_len = x.shape[0]
    tm = min(TM_FFN, t_len)
    nc = len(comm)

    def body(x_ref, vec_ref, w13_hbm, w2_hbm, xo_ref, f_ref, gu_ref, w13_v, w2_v, sems):
        @pl.when(pl.program_id(0) == 0)
        def _():
            _load_ffn_weights(w13_hbm, w2_hbm, w13_v, w2_v, sems)

        x_t = x_ref[...]
        h, _, _, _ = _pre_fwd(x_t, vec_ref)
        hb = h.astype(BF)
        acc = jnp.zeros((tm, D), F32)
        for k in range(FF_CHUNKS):
            g = _dot(hb, w13_v[k])
            u = _dot(hb, w13_v[k + FF_CHUNKS])
            gu_ref[k] = g.astype(BF)
            gu_ref[k + FF_CHUNKS] = u.astype(BF)
            s = g * _sigmoid(g) * u
            acc = acc + _dot(s.astype(BF), w2_v[k])
        f_ref[...] = acc
        xo_ref[...] = _post_fwd(x_t, acc, vec_ref, res_w)

    nt = t_len // tm
    outs = pl.pallas_call(
        _hosted(body, 4, 3, 3, nc, "gather", nt), grid=(nt,),
        in_specs=[_tile(tm, D), _full((8, D)), _any(), _any()] + [_any()] * nc,
        out_specs=[_tile(tm, D), _tile(tm, D), pl.BlockSpec((N_DEV, tm, FF_PAD), lambda i: (0, i, 0))] + [_any()] * nc,
        out_shape=[jax.ShapeDtypeStruct((t_len, D), F32), jax.ShapeDtypeStruct((t_len, D), F32),
                   jax.ShapeDtypeStruct((N_DEV, t_len, FF_PAD), BF)] + _exchange_shapes(comm, "gather"),
        scratch_shapes=_FFN_SCRATCH + (_exchange_scratch(nc) if nc else []), compiler_params=_params(), name=name,
    )(x, vec, w13g, w2g, *comm)
    return outs[:3], outs[3:]


def _ffn_bwd(dout, x, fpre, gu, vec, w13g, w2g, res_w, name, comm=()):
    t_len = x.shape[0]
    tm = min(TM_FFN_BWD, t_len)
    nc = len(comm)

    def body(dout_ref, x_ref, f_ref, gu_ref, vec_ref, w13_hbm, w2_hbm,
             dx_ref, dgu_ref, s_ref, hb_ref, dfb_ref, acc_ref, w13_v, w2_v, sems):
        @pl.when(pl.program_id(0) == 0)
        def _():
            _load_ffn_weights(w13_hbm, w2_hbm, w13_v, w2_v, sems)
            acc_ref[...] = jnp.zeros((8, D), F32)

        dout_t = dout_ref[...]
        df = _post_bwd(dout_t, f_ref[...], vec_ref, acc_ref, res_w)
        dfb = df.astype(BF)
        dfb_ref[...] = dfb
        h, xn, r, n = _pre_fwd(x_ref[...], vec_ref)
        hb_ref[...] = h.astype(BF)
        dh = jnp.zeros((tm, D), F32)
        for k in range(FF_CHUNKS):
            g = gu_ref[k].astype(F32)
            u = gu_ref[k + FF_CHUNKS].astype(F32)
            sig = _sigmoid(g)
            sl = g * sig
            ds = _dot_nt(dfb, w2_v[k])
            dg = (ds * u * (sig * (1.0 + g * (1.0 - sig)))).astype(BF)
            du = (ds * sl).astype(BF)
            s_ref[k] = (sl * u).astype(BF)
            dgu_ref[k] = dg
            dgu_ref[k + FF_CHUNKS] = du
            dh = dh + _dot_nt(dg, w13_v[k]) + _dot_nt(du, w13_v[k + FF_CHUNKS])
        dx_ref[...] = _pre_bwd(dout_t, dh, xn, r, n, vec_ref, acc_ref)

    gu_spec = pl.BlockSpec((N_DEV, tm, FF_PAD), lambda i: (0, i, 0))
    nt = t_len // tm
    outs = pl.pallas_call(
        _hosted(body, 7, 6, 3, nc, "scatter", nt), grid=(nt,),
        in_specs=[_tile(tm, D), _tile(tm, D), _tile(tm, D), gu_spec, _full((8, D)), _any(), _any()] + [_any()] * nc,
        out_specs=[_tile(tm, D), gu_spec, pl.BlockSpec((FF_CHUNKS, tm, FF_PAD), lambda i: (0, i, 0)),
                   _tile(tm, D), _tile(tm, D), _full((8, D))] + [_any()] * nc,
        out_shape=[jax.ShapeDtypeStruct((t_len, D), F32), jax.ShapeDtypeStruct((N_DEV, t_len, FF_PAD), BF),
                   jax.ShapeDtypeStruct((FF_CHUNKS, t_len, FF_PAD), BF), jax.ShapeDtypeStruct((t_len, D), BF),
                   jax.ShapeDtypeStruct((t_len, D), BF), jax.ShapeDtypeStruct((8, D), F32)] + _exchange_shapes(comm, "scatter"),
        scratch_shapes=_FFN_SCRATCH + (_exchange_scratch(nc) if nc else []), compiler_params=_params(), name=name,
    )(dout, x, fpre, gu, vec, w13g, w2g, *comm)
    return outs[:6], outs[6:]


def _wgrad(a, b, j_count, m, n, a_mode, b_mode, out_rows, out_dtype, name, tk=TK_WGRAD, col_slots=1):
    t_len = a.shape[-2]
    tk = min(tk, t_len)
    nk = t_len // tk
    wn = n // col_slots

    def spec(mode, width):
        if mode == "stack":
            return pl.BlockSpec((None, tk, width), lambda j, t: (j, t, 0))
        return pl.BlockSpec((tk, width), lambda j, t: (t, 0))

    def body(a_ref, b_ref, o_ref, acc):
        t = pl.program_id(1)

        @pl.when(t == 0)
        def _():
            acc[...] = jnp.zeros((m, n), F32)

        acc[...] += _dot_tn(a_ref[...], b_ref[...])

        @pl.when(t == nk - 1)
        def _():
            if col_slots == 1:
                o_ref[...] = acc[0:out_rows, :].astype(out_dtype)
            else:
                for s in range(col_slots):
                    o_ref[s] = acc[0:out_rows, wn * s:wn * (s + 1)].astype(out_dtype)

    if col_slots == 1:
        out_spec = pl.BlockSpec((None, out_rows, n), lambda j, t: (j, 0, 0))
        out_shape = jax.ShapeDtypeStruct((j_count, out_rows, n), out_dtype)
    else:
        out_spec = pl.BlockSpec((col_slots, out_rows, wn), lambda j, t: (0, 0, 0))
        out_shape = jax.ShapeDtypeStruct((col_slots, out_rows, wn), out_dtype)
    return pl.pallas_call(
        body, grid=(j_count, nk),
        in_specs=[spec(a_mode, m), spec(b_mode, n)],
        out_specs=out_spec, out_shape=out_shape,
        scratch_shapes=[pltpu.VMEM((m, n), F32)],
        compiler_params=pltpu.CompilerParams(dimension_semantics=("arbitrary", "arbitrary"), vmem_limit_bytes=VMEM_LIMIT),
        name=name,
    )(a, b)


def _c_mask_weights(ws_ref, wsm, wsmt):
    row = lax.broadcasted_iota(jnp.int32, (CHUNK, CHUNK), 0)
    col = lax.broadcasted_iota(jnp.int32, (CHUNK, CHUNK), 1)
    for hh in range(H_C):
        w = jnp.where(row >= col, ws_ref[hh], 0.0)
        wsm[hh] = w.astype(BF)
        if wsmt is not None:
            wsmt[hh] = w.T.astype(BF)


def _c_inner(pre, cvec_ref, wsm, bst_ref, mix_sc, tm):
    z, t = _gelu(pre)
    u = z[:, 0:D]
    v = z[:, D:2 * D]
    mu = jnp.mean(v, axis=-1, keepdims=True)
    vc = v - mu
    rstd = lax.rsqrt(jnp.mean(vc * vc, axis=-1, keepdims=True) + EPS)
    vhat = vc * rstd
    vnb = (vhat * cvec_ref[0:1, :] + cvec_ref[1:2, :]).astype(BF)
    for nn in range(tm // CHUNK):
        for hh in range(H_C):
            rows = slice(CHUNK * nn, CHUNK * (nn + 1))
            cols = slice(CHUNK * hh, CHUNK * (hh + 1))
            mix_sc[rows, cols] = _dot(wsm[hh], vnb[rows, cols]) + bst_ref[:, hh:hh + 1]
    return u, t, rstd, vhat, vnb


def _mixc_fwd(x, vec, w_in, b_in, cvec, ws, bst, w_out, name):
    t_len = x.shape[0]
    tm = min(TM_MIX, t_len)

    def body(x_ref, vec_ref, win_ref, bin_ref, cvec_ref, ws_ref, bst_ref, wout_ref,
             xo_ref, f_ref, pre_ref, wsm, mix_sc):
        @pl.when(pl.program_id(0) == 0)
        def _():
            _c_mask_weights(ws_ref, wsm, None)

        x_t = x_ref[...]
        h, _, _, _ = _pre_fwd(x_t, vec_ref)
        hb = h.astype(BF)
        for j in range(N_DEV):
            cols = slice(256 * j, 256 * (j + 1))
            pre_ref[:, cols] = _dot(hb, win_ref[j]) + bin_ref[:, cols]
        u, _, _, _, _ = _c_inner(pre_ref[...], cvec_ref, wsm, bst_ref, mix_sc, tm)
        fpre = _dot((u * mix_sc[...]).astype(BF), wout_ref[...])
        f_ref[...] = fpre
        xo_ref[...] = _post_fwd(x_t, fpre, vec_ref, 1.0)

    return pl.pallas_call(
        body, grid=(t_len // tm,),
        in_specs=[_tile(tm, D), _full((8, D)), _full((N_DEV, D, 256)), _full((1, 2 * D)), _full((8, D)),
                  _full((H_C, CHUNK, CHUNK)), _full((CHUNK, H_C)), _full((D, D))],
        out_specs=[_tile(tm, D), _tile(tm, D), _tile(tm, 2 * D)],
        out_shape=[jax.ShapeDtypeStruct((t_len, D), F32), jax.ShapeDtypeStruct((t_len, D), F32),
                   jax.ShapeDtypeStruct((t_len, 2 * D), F32)],
        scratch_shapes=[pltpu.VMEM((H_C, CHUNK, CHUNK), BF), pltpu.VMEM((tm, D), F32)],
        compiler_params=_params(), name=name,
    )(x, vec, w_in, b_in, cvec, ws, bst, w_out)


def _mixc_bwd(dout, x, fpre, pre, vec, w_in, cvec, ws, bst, w_out, name):
    t_len = x.shape[0]
    tm = min(TM_MIX, t_len)
    nt = t_len // tm

    def body(dout_ref, x_ref, f_ref, pre_ref, vec_ref, win_ref, cvec_ref, ws_ref, bst_ref, wout_ref,
             dx_ref, dpre_ref, p_ref, hb_ref, dfb_ref, acc_ref, dbin_ref, dws_ref, dbst_ref,
             wsm, wsmt, mix_sc, dvn_sc, dmsum):
        i = pl.program_id(0)

        @pl.when(i == 0)
        def _():
            _c_mask_weights(ws_ref, wsm, wsmt)
            acc_ref[...] = jnp.zeros((8, D), F32)
            dbin_ref[...] = jnp.zeros((8, 2 * D), F32)
            dws_ref[...] = jnp.zeros((H_C, CHUNK, CHUNK), F32)
            dmsum[...] = jnp.zeros((CHUNK, D), F32)

        dout_t = dout_ref[...]
        df = _post_bwd(dout_t, f_ref[...], vec_ref, acc_ref, 1.0)
        dfb = df.astype(BF)
        dfb_ref[...] = dfb
        h, xn, r, n = _pre_fwd(x_ref[...], vec_ref)
        hb_ref[...] = h.astype(BF)
        pre_t = pre_ref[...]
        u, t, rstd, vhat, vnb = _c_inner(pre_t, cvec_ref, wsm, bst_ref, mix_sc, tm)
        mix = mix_sc[...]
        p_ref[...] = (u * mix).astype(BF)
        dp = _dot_nt(dfb, wout_ref[...])
        du = dp * mix
        dmix = dp * u
        dmb = dmix.astype(BF)
        for nn in range(tm // CHUNK):
            rows = slice(CHUNK * nn, CHUNK * (nn + 1))
            dmsum[...] += dmix[rows, :]
            for hh in range(H_C):
                cols = slice(CHUNK * hh, CHUNK * (hh + 1))
                dvn_sc[rows, cols] = _dot(wsmt[hh], dmb[rows, cols])
                dws_ref[hh] += _dot_nt(dmb[rows, cols], vnb[rows, cols])
        dvn = dvn_sc[...]
        acc_ref[5:6, :] += _colsum(dvn * vhat)
        acc_ref[6:7, :] += _colsum(dvn)
        dvhat = dvn * cvec_ref[0:1, :]
        dv = rstd * (dvhat - jnp.mean(dvhat, axis=-1, keepdims=True)
                     - vhat * jnp.mean(dvhat * vhat, axis=-1, keepdims=True))
        gg = _gelu_grad(pre_t, t)
        dpre_u = du * gg[:, 0:D]
        dpre_v = dv * gg[:, D:2 * D]
        dbin_ref[0:1, 0:D] += _colsum(dpre_u)
        dbin_ref[0:1, D:2 * D] += _colsum(dpre_v)
        dpre_ref[:, 0:D] = dpre_u.astype(BF)
        dpre_ref[:, D:2 * D] = dpre_v.astype(BF)
        dh = jnp.zeros((tm, D), F32)
        for j in range(N_DEV):
            dh = dh + _dot_nt(dpre_ref[:, 256 * j:256 * (j + 1)], win_ref[j])
        dx_ref[...] = _pre_bwd(dout_t, dh, xn, r, n, vec_ref, acc_ref)

        @pl.when(i == nt - 1)
        def _():
            row = lax.broadcasted_iota(jnp.int32, (CHUNK, CHUNK), 0)
            col = lax.broadcasted_iota(jnp.int32, (CHUNK, CHUNK), 1)
            for hh in range(H_C):
                dws_ref[hh] = jnp.where(row >= col, dws_ref[hh], 0.0)
                dbst_ref[:, hh:hh + 1] = jnp.sum(dmsum[:, CHUNK * hh:CHUNK * (hh + 1)], axis=1, keepdims=True)

    return pl.pallas_call(
        body, grid=(nt,),
        in_specs=[_tile(tm, D), _tile(tm, D), _tile(tm, D), _tile(tm, 2 * D), _full((8, D)), _full((N_DEV, D, 256)),
                  _full((8, D)), _full((H_C, CHUNK, CHUNK)), _full((CHUNK, H_C)), _full((D, D))],
        out_specs=[_tile(tm, D), _tile(tm, 2 * D), _tile(tm, D), _tile(tm, D), _tile(tm, D), _full((8, D)),
                   _full((8, 2 * D)), _full((H_C, CHUNK, CHUNK)), _full((CHUNK, H_C))],
        out_shape=[jax.ShapeDtypeStruct((t_len, D), F32), jax.ShapeDtypeStruct((t_len, 2 * D), BF),
                   jax.ShapeDtypeStruct((t_len, D), BF), jax.ShapeDtypeStruct((t_len, D), BF),
                   jax.ShapeDtypeStruct((t_len, D), BF), jax.ShapeDtypeStruct((8, D), F32),
                   jax.ShapeDtypeStruct((8, 2 * D), F32), jax.ShapeDtypeStruct((H_C, CHUNK, CHUNK), F32),
                   jax.ShapeDtypeStruct((CHUNK, H_C), F32)],
        scratch_shapes=[pltpu.VMEM((H_C, CHUNK, CHUNK), BF), pltpu.VMEM((H_C, CHUNK, CHUNK), BF),
                        pltpu.VMEM((tm, D), F32), pltpu.VMEM((tm, D), F32), pltpu.VMEM((CHUNK, D), F32)],
        compiler_params=_params(), name=name,
    )(dout, x, fpre, pre, vec, w_in, cvec, ws, bst, w_out)


def _gmean(x, g_ref):
    hi = x.astype(BF)
    lo = (x - hi.astype(F32)).astype(BF)
    return _dot(hi, g_ref[...]) + _dot(lo, g_ref[...])


def _log_sigmoid(lam):
    e = jnp.exp(-jnp.abs(lam))
    log1p = jnp.where(e < 1e-2, e * (1.0 - e * (0.5 - e * (1.0 / 3.0 - 0.25 * e))), jnp.log(1.0 + e))
    return jnp.minimum(lam, 0.0) - log1p


def _neg_expm1(y):
    series = -(y * (1.0 + y * (0.5 + y * (1.0 / 6.0 + y * (1.0 / 24.0 + y * (1.0 / 120.0))))))
    return jnp.where(y > -0.1, series, 1.0 - jnp.exp(y))


def _conv_causal(ext, taps_ref, bias, k_taps, halo, tm):
    acc = bias + taps_ref[0:1, :] * ext[halo - k_taps + 1:halo - k_taps + 1 + tm, :]
    for k in range(1, k_taps):
        off = halo - k_taps + 1 + k
        acc = acc + taps_ref[k:k + 1, :] * ext[off:off + tm, :]
    return acc


def _build_shifted(sh_ref, e, n_rows):
    sh_ref[0] = e
    for r in range(1, 8):
        sh_ref[r] = pltpu.roll(e, n_rows - r, 0)


def _shifted_rows(sh_ref, off, tm):
    base = off - off % 8
    return sh_ref[off % 8, base:base + tm, :]


def _scan(a, u, tm, reverse):
    row = lax.broadcasted_iota(jnp.int32, (tm, W_A), 0)
    d = 1
    while d < tm:
        if reverse:
            keep = row < tm - d
            shift = tm - d
        else:
            keep = row >= d
            shift = d
        a_sh = jnp.where(keep, pltpu.roll(a, shift, 0), 1.0)
        u_sh = jnp.where(keep, pltpu.roll(u, shift, 0), 0.0)
        u = a * u_sh + u
        a = a * a_sh
        d *= 2
    return a, u


def _a_gates(xc, cv_ref, wr_ref, wi_ref):
    xcb = xc.astype(BF)
    r = _sigmoid(_dot(xcb, wr_ref[...]) + cv_ref[5:6, :])
    ig = _sigmoid(_dot(xcb, wi_ref[...]) + cv_ref[6:7, :])
    ls = _log_sigmoid(cv_ref[7:8, :])
    la = LRU_C * r * ls
    a = jnp.exp(la)
    m = jnp.sqrt(_neg_expm1(2.0 * la))
    return xcb, r, ig, ls, a, m


def _b_norm(vc, cv_ref, g_ref):
    mu = _gmean(vc, g_ref)
    dv = vc - mu
    rstd = lax.rsqrt(_gmean(dv * dv, g_ref) + EPS)
    vhat = dv * rstd
    vln = vhat * cv_ref[9:10, :] + cv_ref[10:11, :]
    return rstd, vhat, vln


def _mixab_fwd(x, vec, w_in, cv, w31, wr, wi, gmat, w_out, name):
    t_len = x.shape[0]
    tm = min(TM_MIX, t_len)

    def body(x_ref, vec_ref, win_ref, cv_ref, w31_ref, wr_ref, wi_ref, g_ref, wout_ref,
             xo_ref, f_ref, z_ref, hs_ref, cvs_ref, ext_a, ext_b, hc, shifted):
        @pl.when(pl.program_id(0) == 0)
        def _():
            ext_a[0:HALO_A, :] = jnp.zeros((HALO_A, W_A), F32)
            ext_b[0:HALO_B, :] = jnp.zeros((HALO_B, W_B), F32)
            hc[...] = jnp.zeros((8, W_A), F32)

        x_t = x_ref[...]
        h, _, _, _ = _pre_fwd(x_t, vec_ref)
        hb = h.astype(BF)
        for j in range(N_DEV):
            z_ref[:, 256 * j:256 * (j + 1)] = _dot(hb, win_ref[j])
        ext_a[HALO_A:HALO_A + tm, :] = z_ref[:, W_A:2 * W_A]
        xc = _conv_causal(ext_a, cv_ref, cv_ref[4:5, :], CONV_A, HALO_A, tm)
        ext_a[0:HALO_A, :] = ext_a[tm:tm + HALO_A, :]
        cvs_ref[:, 0:W_A] = xc
        _, _, ig, _, a, m = _a_gates(xc, cv_ref, wr_ref, wi_ref)
        a_cum, hloc = _scan(a, m * ig * xc, tm, False)
        hs = hloc + a_cum * hc[0:1, :]
        hs_ref[...] = hs
        hc[0:1, :] = hs[tm - 1:tm, :]
        gel, _ = _gelu(z_ref[:, 0:W_A])
        ya = hs * gel
        ext_b[HALO_B:HALO_B + tm, :] = z_ref[:, 2 * W_A:2 * W_A + W_B] * _sigmoid(z_ref[:, 2 * W_A + W_B:2 * W_A + 2 * W_B])
        _build_shifted(shifted, ext_b[...], tm + HALO_B)
        vc = cv_ref[8:9, :] + w31_ref[0:1, :] * _shifted_rows(shifted, HALO_B - CONV_B + 1, tm)
        for k in range(1, CONV_B):
            vc = vc + w31_ref[k:k + 1, :] * _shifted_rows(shifted, HALO_B - CONV_B + 1 + k, tm)
        ext_b[0:HALO_B, :] = ext_b[tm:tm + HALO_B, :]
        cvs_ref[:, W_A:W_A + W_B] = vc
        _, _, vln = _b_norm(vc, cv_ref, g_ref)
        yb = vln * _sigmoid(vln)
        fpre = _dot(ya.astype(BF), wout_ref[0:W_A, :]) + _dot(yb.astype(BF), wout_ref[W_A:W_A + W_B, :])
        f_ref[...] = fpre
        xo_ref[...] = _post_fwd(x_t, fpre, vec_ref, 1.0)

    return pl.pallas_call(
        body, grid=(t_len // tm,),
        in_specs=[_tile(tm, D), _full((8, D)), _full((N_DEV, D, 256)), _full((16, W_A)), _full((32, W_B)),
                  _full((W_A, W_A)), _full((W_A, W_A)), _full((W_B, W_B)), _full((D, D))],
        out_specs=[_tile(tm, D), _tile(tm, D), _tile(tm, 2 * D), _tile(tm, W_A), _tile(tm, W_A + W_B)],
        out_shape=[jax.ShapeDtypeStruct((t_len, D), F32), jax.ShapeDtypeStruct((t_len, D), F32),
                   jax.ShapeDtypeStruct((t_len, 2 * D), F32), jax.ShapeDtypeStruct((t_len, W_A), F32),
                   jax.ShapeDtypeStruct((t_len, W_A + W_B), F32)],
        scratch_shapes=[pltpu.VMEM((tm + HALO_A, W_A), F32), pltpu.VMEM((tm + HALO_B, W_B), F32), pltpu.VMEM((8, W_A), F32),
                        pltpu.VMEM((8, tm + HALO_B, W_B), F32)],
        compiler_params=_params(), name=name,
    )(x, vec, w_in, cv, w31, wr, wi, gmat, w_out)


def _mixab_bwd(dout, x, fpre, z, cvs, hs, vec, w_in, cv, w31, wr, wi, gmat, w_out, name):
    t_len = x.shape[0]
    tm = min(TM_MIX, t_len)
    nt = t_len // tm

    def rev(i):
        return nt - 1 - i

    def rtile(ncol):
        return pl.BlockSpec((tm, ncol), lambda i: (rev(i), 0))

    def body(dout_ref, x_ref, f_ref, z_ref, cvs_ref, hs_ref, hsp_ref, vec_ref, win_ref, cv_ref, w31_ref, wr_ref, wi_ref,
             g_ref, wout_ref,
             dx_ref, dz_ref, yab_ref, hb_ref, dfb_ref, xcb_ref, dri_ref, acc_ref, accs_ref, dw31_ref,
             ext_h, ext_dx, ext_dv, carry, shifted):
        i = pl.program_id(0)
        has_prev = (rev(i) > 0).astype(F32)

        @pl.when(i == 0)
        def _():
            acc_ref[...] = jnp.zeros((8, D), F32)
            accs_ref[...] = jnp.zeros((16, W_A), F32)
            dw31_ref[...] = jnp.zeros((32, W_B), F32)
            ext_dx[tm:tm + HALO_A, :] = jnp.zeros((HALO_A, W_A), F32)
            ext_dv[tm:tm + HALO_B, :] = jnp.zeros((HALO_B, W_B), F32)
            carry[...] = jnp.zeros((8, W_A), F32)

        dout_t = dout_ref[...]
        df = _post_bwd(dout_t, f_ref[...], vec_ref, acc_ref, 1.0)
        dfb = df.astype(BF)
        dfb_ref[...] = dfb
        h, xn, r_x, n = _pre_fwd(x_ref[...], vec_ref)
        hb_ref[...] = h.astype(BF)

        ag = z_ref[:, 0:W_A]
        ax = z_ref[:, W_A:2 * W_A]
        bv = z_ref[:, 2 * W_A:2 * W_A + W_B]
        sg = _sigmoid(z_ref[:, 2 * W_A + W_B:2 * W_A + 2 * W_B])
        vv = bv * sg
        xc = cvs_ref[:, 0:W_A]
        xcb, r, ig, ls, a, m = _a_gates(xc, cv_ref, wr_ref, wi_ref)
        xcb_ref[...] = xcb
        hs_t = hs_ref[...]
        ext_h[0:8, :] = hsp_ref[...] * has_prev
        ext_h[8:8 + tm, :] = hs_t
        hprev = ext_h[7:7 + tm, :]
        gel, tg = _gelu(ag)
        rstd, vhat, vln = _b_norm(cvs_ref[:, W_A:W_A + W_B], cv_ref, g_ref)
        sv = _sigmoid(vln)
        yab_ref[:, 0:W_A] = (hs_t * gel).astype(BF)
        yab_ref[:, W_A:W_A + W_B] = (vln * sv).astype(BF)

        dya = _dot_nt(dfb, wout_ref[0:W_A, :])
        dyb = _dot_nt(dfb, wout_ref[W_A:W_A + W_B, :])

        dag = dya * hs_t * _gelu_grad(ag, tg)
        row = lax.broadcasted_iota(jnp.int32, (tm, W_A), 0)
        last = row == tm - 1
        a_next = jnp.where(last, 1.0, pltpu.roll(a, tm - 1, 0))
        u0 = dya * gel + jnp.where(last, carry[0:1, :], 0.0)
        _, dhs = _scan(a_next, u0, tm, True)
        carry[0:1, :] = a[0:1, :] * dhs[0:1, :]
        da = dhs * hprev
        dm = dhs * ig * xc
        di = dhs * m * xc
        dxc = dhs * m * ig
        dla = da * a - dm * (a * a) / m
        accs_ref[7:8, :] += _colsum(dla * r) * (LRU_C * _sigmoid(-cv_ref[7:8, :]))
        drp = (dla * (LRU_C * ls)) * r * (1.0 - r)
        dip = di * ig * (1.0 - ig)
        accs_ref[5:6, :] += _colsum(drp)
        accs_ref[6:7, :] += _colsum(dip)
        drpb = drp.astype(BF)
        dipb = dip.astype(BF)
        dri_ref[:, 0:W_A] = drpb
        dri_ref[:, W_A:2 * W_A] = dipb
        dxc = dxc + _dot_nt(drpb, wr_ref[...]) + _dot_nt(dipb, wi_ref[...])
        accs_ref[4:5, :] += _colsum(dxc)
        ext_dx[0:tm, :] = dxc
        dax = jnp.zeros((tm, W_A), F32)
        for k in range(CONV_A):
            ahead = ext_dx[CONV_A - 1 - k:CONV_A - 1 - k + tm, :]
            accs_ref[k:k + 1, :] += _colsum(ax * ahead)
            dax = dax + cv_ref[k:k + 1, :] * ahead
        ext_dx[tm:tm + HALO_A, :] = dxc[0:HALO_A, :]

        dvln = dyb * (sv * (1.0 + vln * (1.0 - sv)))
        accs_ref[9:10, :] += _colsum(dvln * vhat)
        accs_ref[10:11, :] += _colsum(dvln)
        dvhat = dvln * cv_ref[9:10, :]
        dvc = rstd * (dvhat - _gmean(dvhat, g_ref) - vhat * _gmean(dvhat * vhat, g_ref))
        accs_ref[8:9, :] += _colsum(dvc)
        ext_dv[0:tm, :] = dvc
        _build_shifted(shifted, ext_dv[...], tm + HALO_B)
        dvv = jnp.zeros((tm, W_B), F32)
        for k in range(CONV_B):
            ahead = _shifted_rows(shifted, CONV_B - 1 - k, tm)
            dw31_ref[k:k + 1, :] += _colsum(vv * ahead)
            dvv = dvv + w31_ref[k:k + 1, :] * ahead
        ext_dv[tm:tm + HALO_B, :] = dvc[0:HALO_B, :]

        dz_ref[:, 0:W_A] = dag.astype(BF)
        dz_ref[:, W_A:2 * W_A] = dax.astype(BF)
        dz_ref[:, 2 * W_A:2 * W_A + W_B] = (dvv * sg).astype(BF)
        dz_ref[:, 2 * W_A + W_B:2 * W_A + 2 * W_B] = (dvv * vv * (1.0 - sg)).astype(BF)
        dh = jnp.zeros((tm, D), F32)
        for j in range(N_DEV):
            dh = dh + _dot_nt(dz_ref[:, 256 * j:256 * (j + 1)], win_ref[j])
        dx_ref[...] = _pre_bwd(dout_t, dh, xn, r_x, n, vec_ref, acc_ref)

    hsp_spec = pl.BlockSpec((8, W_A), lambda i: (jnp.maximum(rev(i) * (tm // 8) - 1, 0), 0))
    return pl.pallas_call(
        body, grid=(nt,),
        in_specs=[rtile(D), rtile(D), rtile(D), rtile(2 * D), rtile(W_A + W_B), rtile(W_A), hsp_spec, _full((8, D)),
                  _full((N_DEV, D, 256)), _full((16, W_A)), _full((32, W_B)), _full((W_A, W_A)), _full((W_A, W_A)),
                  _full((W_B, W_B)), _full((D, D))],
        out_specs=[rtile(D), rtile(2 * D), rtile(D), rtile(D), rtile(D), rtile(W_A), rtile(2 * W_A), _full((8, D)),
                   _full((16, W_A)), _full((32, W_B))],
        out_shape=[jax.ShapeDtypeStruct((t_len, D), F32), jax.ShapeDtypeStruct((t_len, 2 * D), BF),
                   jax.ShapeDtypeStruct((t_len, D), BF), jax.ShapeDtypeStruct((t_len, D), BF),
                   jax.ShapeDtypeStruct((t_len, D), BF), jax.ShapeDtypeStruct((t_len, W_A), BF),
                   jax.ShapeDtypeStruct((t_len, 2 * W_A), BF), jax.ShapeDtypeStruct((8, D), F32),
                   jax.ShapeDtypeStruct((16, W_A), F32), jax.ShapeDtypeStruct((32, W_B), F32)],
        scratch_shapes=[pltpu.VMEM((tm + 8, W_A), F32), pltpu.VMEM((tm + HALO_A, W_A), F32),
                        pltpu.VMEM((tm + HALO_B, W_B), F32), pltpu.VMEM((8, W_A), F32),
                        pltpu.VMEM((8, tm + HALO_B, W_B), F32)],
        compiler_params=_params(), name=name,
    )(dout, x, fpre, z, cvs, hs, hs, vec, w_in, cv, w31, wr, wi, gmat, w_out)


def _loss_head(y, tgt, name):
    t_len = y.shape[0]
    tm = min(TM_FFN, t_len)

    def body(y_ref, t_ref, dy_ref, loss_ref):
        @pl.when(pl.program_id(0) == 0)
        def _():
            loss_ref[...] = jnp.zeros((8, 128), F32)

        err = y_ref[...] - t_ref[...]
        dy_ref[...] = err * (1.0 / D)
        loss_ref[...] += jnp.sum(err * err) * (0.5 / D)

    return pl.pallas_call(
        body, grid=(t_len // tm,), in_specs=[_tile(tm, D), _tile(tm, D)],
        out_specs=[_tile(tm, D), _full((8, 128))],
        out_shape=[jax.ShapeDtypeStruct((t_len, D), F32), jax.ShapeDtypeStruct((8, 128), F32)],
        compiler_params=_params(), name=name,
    )(y, tgt)


def _vec(p, l, j):
    return jnp.concatenate([p["mod"][l, j], p["norm_pre"][l, j][None], p["norm_post"][l, j][None], jnp.zeros((3, D), F32)], 0)


def _ab_consts(p):
    gw = p["a_gate_w"]
    gb = p["a_gate_b"]
    half = W_A // 8
    wr = jax.scipy.linalg.block_diag(*[gw[hh, :, 0:half] for hh in range(8)]).astype(BF)
    wi = jax.scipy.linalg.block_diag(*[gw[hh, :, half:2 * half] for hh in range(8)]).astype(BF)
    rows = [p["a_conv_w"], p["a_conv_b"][None], gb[:, 0:half].reshape(1, W_A), gb[:, half:2 * half].reshape(1, W_A),
            p["a_lam"][None], p["b_conv_b"][None], p["b_norm_g"][None], p["b_norm_b"][None], jnp.zeros((5, W_A), F32)]
    cv = jnp.concatenate(rows, 0)
    w31 = jnp.concatenate([p["b_conv_w"], jnp.zeros((1, W_B), F32)], 0)
    grp = jnp.arange(W_B) // (W_B // 8)
    gmat = ((grp[:, None] == grp[None, :]).astype(F32) / (W_B // 8)).astype(BF)
    return cv, w31, wr, wi, gmat


SUBLAYERS = ("f0", "ab", "f1", "f2", "c", "f3")


def _local_step(x, tgt, p, plan=None):
    g = {}
    saved = []
    cur = x
    wsets = dict(p["wsets"])
    ab_c = _ab_consts(p)
    c_cvec = jnp.concatenate([p["c_norm_g"][None], p["c_norm_b"][None], jnp.zeros((6, D), F32)], 0)
    c_bst = jnp.transpose(p["c_b_s"])
    c_bin = p["c_b_in"][None]
    for s_idx, wname in enumerate(SUBLAYERS):
        l, j = divmod(s_idx, 3)
        vec = _vec(p, l, j)
        tag = f"l{l}s{j}"
        if j != 1:
            names = plan["gather"].get(s_idx, []) if plan else []
            comm = [a for nm in names for a in plan["shards"][nm]]
            (nxt, fpre, gu), got = _ffn_fwd(cur, vec, *wsets[wname], 0.5, "ffn_fwd_" + tag, comm=comm)
            for k, nm in enumerate(names):
                wsets[nm] = list(got[2 * k:2 * k + 2])
            saved.append((cur, fpre, gu, vec))
        elif l == 0:
            w_in, w_out = wsets[wname]
            nxt, fpre, z, hs, cvs = _mixab_fwd(cur, vec, w_in, *ab_c, w_out.reshape(D, D), "mixab_fwd_" + tag)
            saved.append((cur, fpre, z, hs, cvs, vec))
        else:
            w_in, w_out = wsets[wname]
            nxt, fpre, pre = _mixc_fwd(cur, vec, w_in, c_bin, c_cvec, p["c_w_s"], c_bst, w_out.reshape(D, D), "mixc_fwd_" + tag)
            saved.append((cur, fpre, pre, vec))
        cur = nxt
    dcur, loss_blk = _loss_head(cur, tgt, "loss_head")
    accs, pending, recv = {}, {}, {}
    for s_idx in reversed(range(6)):
        wname = SUBLAYERS[s_idx]
        l, j = divmod(s_idx, 3)
        tag = f"l{l}s{j}"
        sv = saved[s_idx]
        if j != 1:
            names = plan["scatter"].get(s_idx, []) if plan else []
            comm = [a for nm in names for a in pending.pop(nm)]
            xin, fpre, gu, vec = sv
            (dcur, dgu, s, hb, dfb, acc), got = _ffn_bwd(dcur, xin, fpre, gu, vec, *wsets[wname], 0.5, "ffn_bwd_" + tag, comm=comm)
            for k, nm in enumerate(names):
                recv[nm] = list(got[2 * k:2 * k + 2])
            dw13 = _wgrad(hb, dgu, N_DEV, D, FF_PAD, "share", "stack", D, BF, "wgrad_w13_" + tag)
            dw2 = _wgrad(s, dfb, FF_CHUNKS, FF_PAD, D, "stack", "share", FF_SHARD, BF, "wgrad_w2_" + tag)
            pending[wname] = [dw13, dw2.reshape(N_DEV, W2_SHARD, D)]
        elif l == 0:
            xin, fpre, z, hs, cvs, vec = sv
            w_in, w_out = wsets[wname]
            dcur, dz, yab, hb, dfb, xcb, dri, acc, accs_ab, dw31 = _mixab_bwd(
                dcur, xin, fpre, z, cvs, hs, vec, w_in, *ab_c, w_out.reshape(D, D), "mixab_bwd_" + tag)
            d_in = _wgrad(hb, dz, 1, D, 2 * D, "share", "share", D, BF, "wgrad_ab_in", tk=TK_WGRAD_WIDE, col_slots=N_DEV)
            d_out = _wgrad(yab, dfb, 1, D, D, "share", "share", D, BF, "wgrad_ab_out")
            pending[wname] = [d_in, d_out.reshape(N_DEV, D // N_DEV, D)]
            g["gate"] = _wgrad(xcb, dri, 1, W_A, 2 * W_A, "share", "share", W_A, F32, "wgrad_gate")
            g["accs_ab"] = accs_ab
            g["dw31"] = dw31
        else:
            xin, fpre, pre, vec = sv
            w_in, w_out = wsets[wname]
            dcur, dpre, pb, hb, dfb, acc, dbin, dws, dbst = _mixc_bwd(
                dcur, xin, fpre, pre, vec, w_in, c_cvec, p["c_w_s"], c_bst, w_out.reshape(D, D), "mixc_bwd_" + tag)
            d_in = _wgrad(hb, dpre, 1, D, 2 * D, "share", "share", D, BF, "wgrad_c_in", tk=TK_WGRAD_WIDE, col_slots=N_DEV)
            d_out = _wgrad(pb, dfb, 1, D, D, "share", "share", D, BF, "wgrad_c_out")
            pending[wname] = [d_in, d_out.reshape(N_DEV, D // N_DEV, D)]
            g["c_small"] = (acc, dbin, dws, dbst)
        accs[f"{l}{j}"] = acc
    g["accs"] = accs
    g["pending"] = pending
    g["recv"] = recv
    return loss_blk, dcur, g


def _exchange_ops(ins, outs, sems, mode, action):
    send_sems, recv_sems, loc_sems = sems
    n = len(ins)
    x, y, c = lax.axis_index("x"), lax.axis_index("y"), lax.axis_index("c")
    me = 4 * x + 2 * y + c

    def src(i, dev):
        return ins[i] if mode == "gather" else ins[i].at[dev]

    for i in range(n):
        cp = pltpu.make_async_copy(src(i, me), outs[i].at[me], loc_sems.at[i])
        if action == "start":
            cp.start()
        else:
            cp.wait()
    for mask in range(1, N_DEV):
        px = 1 - x if mask & 4 else x
        py = 1 - y if mask & 2 else y
        pc = 1 - c if mask & 1 else c
        peer = 4 * px + 2 * py + pc
        for i in range(n):
            k = i * (N_DEV - 1) + mask - 1
            cp = pltpu.make_async_remote_copy(
                src_ref=src(i, peer), dst_ref=outs[i].at[me if action == "start" else peer],
                send_sem=send_sems.at[k], recv_sem=recv_sems.at[k],
                device_id=(px, py, pc), device_id_type=pl.DeviceIdType.MESH)
            if action == "start":
                cp.start()
            else:
                cp.wait()


def _exchange_scratch(n):
    return [pltpu.SemaphoreType.DMA((n * (N_DEV - 1),)), pltpu.SemaphoreType.DMA((n * (N_DEV - 1),)),
            pltpu.SemaphoreType.DMA((n,))]


def _exchange_shapes(arrays, mode):
    return [jax.ShapeDtypeStruct(((N_DEV,) + a.shape) if mode == "gather" else a.shape, a.dtype) for a in arrays]


def _exchange(arrays, mode, name):
    n = len(arrays)

    def body(*refs):
        ins, outs, sems = refs[:n], refs[n:2 * n], refs[2 * n:]
        _exchange_ops(ins, outs, sems, mode, "start")
        _exchange_ops(ins, outs, sems, mode, "wait")

    return pl.pallas_call(
        body, in_specs=[pl.BlockSpec(memory_space=pl.ANY)] * n, out_specs=[pl.BlockSpec(memory_space=pl.ANY)] * n,
        out_shape=_exchange_shapes(arrays, mode), scratch_shapes=_exchange_scratch(n), name=name,
    )(*arrays)


def _sum_slots(a, name):
    def body(a_ref, o_ref):
        acc = a_ref[0]
        for s in range(1, N_DEV):
            acc = acc + a_ref[s]
        o_ref[...] = acc

    return pl.pallas_call(body, out_shape=jax.ShapeDtypeStruct(a.shape[1:], F32), name=name,
                          compiler_params=pltpu.CompilerParams(vmem_limit_bytes=VMEM_LIMIT))(a)


def _pack(pieces, mult):
    flat = jnp.concatenate([q.reshape(-1).astype(F32) for q in pieces])
    size = -(-flat.shape[0] // mult) * mult
    return jnp.pad(flat, (0, size - flat.shape[0])).reshape(size // 128, 128)


def _unpack(flat, shapes):
    out, off = [], 0
    for shp in shapes:
        size = math.prod(shp)
        out.append(flat[..., off:off + size].reshape(flat.shape[:-1] + tuple(shp)))
        off += size
    return out


def _mod_part(c_all, ada_w, ada_b_mine, name):
    cols = ada_w.shape[-1]

    def body(c_ref, w_ref, b_ref, o_ref):
        cv = c_ref[...]
        ca = cv * _sigmoid(cv)
        for l in range(2):
            o_ref[l] = jnp.dot(ca, w_ref[l], preferred_element_type=F32, precision=lax.Precision.HIGHEST) + b_ref[l:l + 1, :]

    return pl.pallas_call(body, out_shape=jax.ShapeDtypeStruct((2, N_DEV, cols), F32), name=name,
                          compiler_params=pltpu.CompilerParams(vmem_limit_bytes=VMEM_LIMIT))(c_all, ada_w, ada_b_mine)


def _ada_w_grad(c_all_t, dmod_mine, name):
    cols = dmod_mine.shape[-1]

    def body(ct_ref, d_ref, o_ref):
        cv = ct_ref[...]
        ca = cv * _sigmoid(cv)
        for l in range(2):
            acc = ca[:, 0:1] * d_ref[l, 0:1, :]
            for b in range(1, N_DEV):
                acc = acc + ca[:, b:b + 1] * d_ref[l, b:b + 1, :]
            o_ref[l] = acc

    return pl.pallas_call(body, out_shape=jax.ShapeDtypeStruct((2, D, cols), F32), name=name,
                          compiler_params=pltpu.CompilerParams(vmem_limit_bytes=VMEM_LIMIT))(c_all_t, dmod_mine)


def _adamw_math(w, g, m, v):
    m2 = ADAM_B1 * m + (1.0 - ADAM_B1) * g
    v2 = ADAM_B2 * v + (1.0 - ADAM_B2) * (g * g)
    m_hat = m2 / (1.0 - ADAM_B1 ** ADAM_STEP)
    v_hat = v2 / (1.0 - ADAM_B2 ** ADAM_STEP)
    delta = -ADAM_LR * (m_hat / (jnp.sqrt(v_hat) + ADAM_EPS) + ADAM_WD * w)
    return delta, m2, v2


def _adamw_big(w, g, m, v, name):
    rows, cols = w.shape
    br = next(b for b in (512, 352, 256, 128, 64, 32, 16, 8) if rows % b == 0)
    partial = g.ndim == 3

    def body(w_ref, g_ref, m_ref, v_ref, go_ref, d_ref, mo_ref, vo_ref):
        if partial:
            gsum = g_ref[0, :, 0:cols].astype(F32)
            for s in range(1, N_DEV):
                gsum = gsum + g_ref[s, :, 0:cols].astype(F32)
        else:
            gsum = g_ref[...]
        go_ref[...] = gsum
        d_ref[...], mo_ref[...], vo_ref[...] = _adamw_math(w_ref[...], gsum, m_ref[...], v_ref[...])

    blk = _tile(br, cols)
    g_spec = pl.BlockSpec((N_DEV, br, g.shape[-1]), lambda i: (0, i, 0)) if partial else blk
    shp = jax.ShapeDtypeStruct((rows, cols), F32)
    return pl.pallas_call(body, grid=(rows // br,), in_specs=[blk, g_spec, blk, blk], out_specs=[blk] * 4,
                          out_shape=[shp] * 4, compiler_params=_params(), name=name)(w, g, m, v)


def _adamw_small(ws, gs, ms, vs, name):
    n = len(ws)

    def body(*refs):
        for i in range(n):
            w_ref, g_ref, m_ref, v_ref = (refs[k * n + i] for k in range(4))
            d_ref, mo_ref, vo_ref = (refs[(4 + k) * n + i] for k in range(3))
            d_ref[...], mo_ref[...], vo_ref[...] = _adamw_math(w_ref[...], g_ref[...], m_ref[...], v_ref[...])

    shapes = [jax.ShapeDtypeStruct(w.shape, F32) for w in ws]
    outs = pl.pallas_call(body, out_shape=shapes * 3, name=name,
                          compiler_params=pltpu.CompilerParams(vmem_limit_bytes=VMEM_LIMIT))(*ws, *gs, *ms, *vs)
    return outs[:n], outs[n:2 * n], outs[2 * n:]


def _as2d(a):
    return a.reshape(-1, a.shape[-1])


def kernel(x, c, ada_w, ada_b, norm_pre, norm_post, ffn_w13, ffn_w2, ab_w_in, a_conv_w, a_conv_b, a_gate_w, a_gate_b, a_lam, b_conv_w, b_conv_b, b_norm_g, b_norm_b, ab_w_out, c_w_in, c_b_in, c_norm_g, c_norm_b, c_w_s, c_b_s, c_w_out, loss_target, m_ada_w, m_ada_b, m_norm_pre, m_norm_post, m_ffn_w13, m_ffn_w2, m_ab_w_in, m_a_conv_w, m_a_conv_b, m_a_gate_w, m_a_gate_b, m_a_lam, m_b_conv_w, m_b_conv_b, m_b_norm_g, m_b_norm_b, m_ab_w_out, m_c_w_in, m_c_b_in, m_c_norm_g, m_c_norm_b, m_c_w_s, m_c_b_s, m_c_w_out, v_ada_w, v_ada_b, v_norm_pre, v_norm_post, v_ffn_w13, v_ffn_w2, v_ab_w_in, v_a_conv_w, v_a_conv_b, v_a_gate_w, v_a_gate_b, v_a_lam, v_b_conv_w, v_b_conv_b, v_b_norm_g, v_b_norm_b, v_ab_w_out, v_c_w_in, v_c_b_in, v_c_norm_g, v_c_norm_b, v_c_w_s, v_c_b_s, v_c_w_out):
    me = 4 * lax.axis_index("x") + 2 * lax.axis_index("y") + lax.axis_index("c")
    weights = dict(ada_w=ada_w, ada_b=ada_b, norm_pre=norm_pre, norm_post=norm_post, ffn_w13=ffn_w13, ffn_w2=ffn_w2,
                   ab_w_in=ab_w_in, a_conv_w=a_conv_w, a_conv_b=a_conv_b, a_gate_w=a_gate_w, a_gate_b=a_gate_b, a_lam=a_lam,
                   b_conv_w=b_conv_w, b_conv_b=b_conv_b, b_norm_g=b_norm_g, b_norm_b=b_norm_b, ab_w_out=ab_w_out,
                   c_w_in=c_w_in, c_b_in=c_b_in, c_norm_g=c_norm_g, c_norm_b=c_norm_b, c_w_s=c_w_s, c_b_s=c_b_s, c_w_out=c_w_out)
    moms = dict(ada_w=m_ada_w, ada_b=m_ada_b, norm_pre=m_norm_pre, norm_post=m_norm_post, ffn_w13=m_ffn_w13, ffn_w2=m_ffn_w2,
                ab_w_in=m_ab_w_in, a_conv_w=m_a_conv_w, a_conv_b=m_a_conv_b, a_gate_w=m_a_gate_w, a_gate_b=m_a_gate_b,
                a_lam=m_a_lam, b_conv_w=m_b_conv_w, b_conv_b=m_b_conv_b, b_norm_g=m_b_norm_g, b_norm_b=m_b_norm_b,
                ab_w_out=m_ab_w_out, c_w_in=m_c_w_in, c_b_in=m_c_b_in, c_norm_g=m_c_norm_g, c_norm_b=m_c_norm_b,
                c_w_s=m_c_w_s, c_b_s=m_c_b_s, c_w_out=m_c_w_out)
    vars_ = dict(ada_w=v_ada_w, ada_b=v_ada_b, norm_pre=v_norm_pre, norm_post=v_norm_post, ffn_w13=v_ffn_w13, ffn_w2=v_ffn_w2,
                 ab_w_in=v_ab_w_in, a_conv_w=v_a_conv_w, a_conv_b=v_a_conv_b, a_gate_w=v_a_gate_w, a_gate_b=v_a_gate_b,
                 a_lam=v_a_lam, b_conv_w=v_b_conv_w, b_conv_b=v_b_conv_b, b_norm_g=v_b_norm_g, b_norm_b=v_b_norm_b,
                 ab_w_out=v_ab_w_out, c_w_in=v_c_w_in, c_b_in=v_c_b_in, c_norm_g=v_c_norm_g, c_norm_b=v_c_norm_b,
                 c_w_s=v_c_w_s, c_b_s=v_c_b_s, c_w_out=v_c_w_out)
    names = list(weights)

    w13b = jnp.pad(ffn_w13.astype(BF).reshape(4, D, FF_SHARD), ((0, 0), (0, 0), (0, FF_PAD - FF_SHARD)))
    small_shapes = [(D,), (2, 3, 128), (2, 3, 128), (CONV_A, 64), (CONV_B, 64), (256,), (128,), (128,)]
    small = _pack([c, norm_pre, norm_post, a_conv_w, b_conv_w, c_b_in, c_norm_g, c_norm_b], 1024)
    w2b = ffn_w2.astype(BF).reshape(4, W2_SHARD, D)
    shards = {f"f{f}": [w13b[f], w2b[f]] for f in range(4)}
    shards["ab"] = [ab_w_in[0].astype(BF), ab_w_out[0].astype(BF)]
    shards["c"] = [c_w_in[0].astype(BF), c_w_out[0].astype(BF)]
    w13g0, w2g0, small_g = _exchange(shards["f0"] + [small], "gather", "gather_first")
    plan = dict(shards=shards, gather={0: ["ab", "f1"], 2: ["f2"], 3: ["c", "f3"]},
                scatter={3: ["f3", "c"], 2: ["f2"], 0: ["f1", "ab"]})
    c_all, npre_g, npost_g, acw_g, bcw_g, cbin_g, cng_g, cnb_g = _unpack(small_g.reshape(N_DEV, -1), small_shapes)

    def cat_last(a):
        return jnp.moveaxis(a, 0, -2).reshape(a.shape[1:-1] + (N_DEV * a.shape[-1],))

    ada_b_mine = lax.dynamic_slice_in_dim(ada_b, me * ada_w.shape[-1], ada_w.shape[-1], axis=1)
    (mod_g,) = _exchange([_mod_part(c_all, ada_w, ada_b_mine, "mod_part")], "gather", "gather_mod")
    mod = cat_last(lax.dynamic_index_in_dim(mod_g, me, axis=2, keepdims=False)).reshape(2, 3, 3, D)

    p = dict(mod=mod, norm_pre=cat_last(npre_g), norm_post=cat_last(npost_g), wsets={"f0": [w13g0, w2g0]},
             a_conv_w=cat_last(acw_g), a_conv_b=a_conv_b[0], a_gate_w=a_gate_w[0], a_gate_b=a_gate_b[0], a_lam=a_lam[0],
             b_conv_w=cat_last(bcw_g), b_conv_b=b_conv_b[0], b_norm_g=b_norm_g[0], b_norm_b=b_norm_b[0],
             c_b_in=cat_last(cbin_g), c_norm_g=cat_last(cng_g), c_norm_b=cat_last(cnb_g), c_w_s=c_w_s[0], c_b_s=c_b_s[0])

    loss_blk, grad_x, g = _local_step(x[0], loss_target[0], p, plan)
    loss = lax.psum(loss_blk[0, 0], ("x", "y", "c"))

    accs = g["accs"]
    dmod = jnp.stack([jnp.stack([accs[f"{l}{j}"][0:3] for j in range(3)]) for l in range(2)])
    dnpre = jnp.stack([jnp.stack([accs[f"{l}{j}"][3] for j in range(3)]) for l in range(2)])
    dnpost = jnp.stack([jnp.stack([accs[f"{l}{j}"][4] for j in range(3)]) for l in range(2)])
    sab = g["accs_ab"]
    half = W_A // 8
    dgate = g["gate"][0]
    dgw = jnp.stack([jnp.concatenate([dgate[half * hh:half * (hh + 1), half * hh:half * (hh + 1)],
                                      dgate[half * hh:half * (hh + 1), W_A + half * hh:W_A + half * (hh + 1)]], axis=1)
                     for hh in range(8)])
    dgb = jnp.concatenate([sab[5].reshape(8, half), sab[6].reshape(8, half)], axis=1)
    c_acc, c_dbin, c_dws, c_dbst = g["c_small"]
    red_shapes = [(2, 9216), (2, 3, D), (2, 3, D), (CONV_A, W_A), (W_A,), (8, half, 2 * half), (8, 2 * half), (W_A,),
                  (CONV_B, W_B), (W_B,), (W_B,), (W_B,), (2 * D,), (D,), (D,), (H_C, CHUNK, CHUNK), (H_C, CHUNK)]
    red = _pack([dmod.reshape(2, 9216), dnpre, dnpost, sab[0:4], sab[4], dgw, dgb, sab[7], g["dw31"][0:CONV_B], sab[8],
                 sab[9], sab[10], c_dbin[0], c_acc[5], c_acc[6], c_dws, jnp.transpose(c_dbst)], N_DEV * 1024)
    (red_r,) = _exchange([red.reshape(N_DEV, -1, 128)], "scatter", "scatter_small_grads")
    red_all, dmod_all = _exchange([_sum_slots(red_r, "sum_small_grads"), dmod.reshape(-1, 128)], "gather", "gather_small_grads")
    red_sum = red_all.reshape(-1)
    (g_ada_b, g_npre, g_npost, g_acw, g_acb, g_agw, g_agb, g_alam, g_bcw, g_bcb, g_bng, g_bnb, g_cbin, g_cng, g_cnb,
     g_cws, g_cbs) = _unpack(red_sum, red_shapes)
    dmod_all = dmod_all.reshape(N_DEV, 2, 9216)
    ncol = ada_w.shape[-1]
    dmod_mine = jnp.moveaxis(lax.dynamic_slice_in_dim(dmod_all, me * ncol, ncol, axis=2), 0, 1)
    g_ada_w = _ada_w_grad(jnp.transpose(c_all), dmod_mine, "ada_w_grad")

    def mine(a, width):
        return lax.dynamic_slice_in_dim(a, me * width, width, axis=a.ndim - 1)

    small_grads = dict(
        ada_b=g_ada_b, norm_pre=mine(g_npre, 128), norm_post=mine(g_npost, 128), a_conv_w=mine(g_acw, 64)[None],
        a_conv_b=g_acb[None], a_gate_w=g_agw[None], a_gate_b=g_agb[None], a_lam=g_alam[None], b_conv_w=mine(g_bcw, 64)[None],
        b_conv_b=g_bcb[None], b_norm_g=g_bng[None], b_norm_b=g_bnb[None], c_b_in=mine(g_cbin, 256)[None],
        c_norm_g=mine(g_cng, 128)[None], c_norm_b=mine(g_cnb, 128)[None], c_w_s=g_cws[None], c_b_s=g_cbs[None])

    recv = dict(g["recv"])
    left = sorted(g["pending"])
    got = _exchange([a for nm in left for a in g["pending"][nm]], "scatter", "scatter_last")
    for k, nm in enumerate(left):
        recv[nm] = list(got[2 * k:2 * k + 2])
    r_w13 = jnp.stack([recv[f"f{f}"][0] for f in range(4)], axis=1)
    r_w2 = jnp.stack([recv[f"f{f}"][1] for f in range(4)], axis=1)
    big_partials = dict(ffn_w13=r_w13.reshape(N_DEV, 4 * D, FF_PAD), ffn_w2=r_w2.reshape(N_DEV, 4 * W2_SHARD, D),
                        ab_w_in=recv["ab"][0], ab_w_out=recv["ab"][1], c_w_in=recv["c"][0], c_w_out=recv["c"][1],
                        ada_w=_as2d(g_ada_w))

    grads, deltas, new_m, new_v = {}, {}, {}, {}
    for nm, gp in big_partials.items():
        shp = weights[nm].shape
        go, dl, mo, vo = _adamw_big(_as2d(weights[nm]), gp, _as2d(moms[nm]), _as2d(vars_[nm]), "adamw_" + nm)
        grads[nm], deltas[nm], new_m[nm], new_v[nm] = (a.reshape(shp) for a in (go, dl, mo, vo))
    snames = list(small_grads)
    dls, mos, vos = _adamw_small([_as2d(weights[nm]) for nm in snames], [_as2d(small_grads[nm]) for nm in snames],
                                 [_as2d(moms[nm]) for nm in snames], [_as2d(vars_[nm]) for nm in snames], "adamw_small")
    for k, nm in enumerate(snames):
        shp = weights[nm].shape
        grads[nm] = small_grads[nm].reshape(shp)
        deltas[nm], new_m[nm], new_v[nm] = dls[k].reshape(shp), mos[k].reshape(shp), vos[k].reshape(shp)

    return (loss, grad_x[None], *[grads[nm] for nm in names], *[deltas[nm] for nm in names],
            *[new_m[nm] for nm in names], *[new_v[nm] for nm in names])
```

```python
import functools
import math

import jax
import jax.numpy as jnp
from jax import lax
from jax.experimental import pallas as pl
from jax.experimental.pallas import tpu as pltpu

F32 = jnp.float32
BF = jnp.bfloat16

N_DEV = 8
D = 1024
EPS = 1e-6
D_FF = 2816
FF_SHARD = 704
FF_PAD = 768
FF_CHUNKS = 4
W2_SHARD = 352
W_A = 512
W_B = 512
CONV_A = 4
CONV_B = 31
HALO_A = 8
HALO_B = 32
LRU_C = 8.0
CHUNK = 128
H_C = 8
ADAM_LR = 0.001
ADAM_B1 = 0.9
ADAM_B2 = 0.999
ADAM_EPS = 1e-08
ADAM_WD = 0.01
ADAM_STEP = 10
VMEM_LIMIT = 56 * 1024 * 1024
GELU_C = math.sqrt(2.0 / math.pi)

TM_FFN = 512
TM_FFN_BWD = 256
TM_MIX = 256
TK_WGRAD = 2048
TK_WGRAD_WIDE = 1024


def _params(limit=VMEM_LIMIT):
    return pltpu.CompilerParams(dimension_semantics=("arbitrary",), vmem_limit_bytes=limit)


def _dot(a, b):
    return jnp.dot(a, b, preferred_element_type=F32)


def _dot_nt(a, b):
    return lax.dot_general(a, b, (((1,), (1,)), ((), ())), preferred_element_type=F32)


def _dot_tn(a, b):
    return lax.dot_general(a, b, (((0,), (0,)), ((), ())), preferred_element_type=F32)


def _sigmoid(x):
    return 0.5 + 0.5 * jnp.tanh(0.5 * x)


def _gelu(x):
    t = jnp.tanh(GELU_C * (x + 0.044715 * x * x * x))
    return 0.5 * x * (1.0 + t), t


def _gelu_grad(x, t):
    return 0.5 * (1.0 + t) + 0.5 * x * (1.0 - t * t) * GELU_C * (1.0 + 3.0 * 0.044715 * x * x)


def _rms(x):
    r = lax.rsqrt(jnp.mean(x * x, axis=-1, keepdims=True) + EPS)
    return x * r, r


def _colsum(x):
    return jnp.sum(x, axis=0, keepdims=True)


def _pre_fwd(x, vec_ref):
    xn, r = _rms(x)
    n = xn * vec_ref[3:4, :]
    h = n * (1.0 + vec_ref[1:2, :]) + vec_ref[0:1, :]
    return h, xn, r, n


def _post_fwd(x, f, vec_ref, res_w):
    fn, _ = _rms(f)
    return x + (res_w * (1.0 + vec_ref[2:3, :])) * (fn * vec_ref[4:5, :])


def _post_bwd(dout, f, vec_ref, acc_ref, res_w):
    fn, r2 = _rms(f)
    post_g = vec_ref[4:5, :]
    dy = dout * (res_w * (1.0 + vec_ref[2:3, :]))
    acc_ref[2:3, :] += _colsum(dout * (res_w * (fn * post_g)))
    acc_ref[4:5, :] += _colsum(dy * fn)
    dfn = dy * post_g
    return r2 * (dfn - fn * jnp.mean(dfn * fn, axis=-1, keepdims=True))


def _pre_bwd(dout, dh, xn, r, n, vec_ref, acc_ref):
    acc_ref[0:1, :] += _colsum(dh)
    acc_ref[1:2, :] += _colsum(dh * n)
    dn = dh * (1.0 + vec_ref[1:2, :])
    acc_ref[3:4, :] += _colsum(dn * xn)
    dxn = dn * vec_ref[3:4, :]
    return dout + r * (dxn - xn * jnp.mean(dxn * xn, axis=-1, keepdims=True))


def _tile(tm, ncol):
    return pl.BlockSpec((tm, ncol), lambda i: (i, 0))


def _full(shape):
    return pl.BlockSpec(shape, lambda i: (0,) * len(shape))


def _any():
    return pl.BlockSpec(memory_space=pl.ANY)


def _load_ffn_weights(w13_hbm, w2_hbm, w13_v, w2_v, sems):
    w2_v[:, FF_SHARD:FF_PAD, :] = jnp.zeros((FF_CHUNKS, FF_PAD - FF_SHARD, D), BF)
    copies = [pltpu.make_async_copy(w13_hbm, w13_v, sems.at[0])]
    for j in range(N_DEV):
        copies.append(pltpu.make_async_copy(
            w2_hbm.at[j], w2_v.at[j // 2, pl.ds((j % 2) * W2_SHARD, W2_SHARD), :], sems.at[1 + j]))
    for cp in copies:
        cp.start()
    for cp in copies:
        cp.wait()


_FFN_SCRATCH = [pltpu.VMEM((N_DEV, D, FF_PAD), BF), pltpu.VMEM((FF_CHUNKS, FF_PAD, D), BF),
                pltpu.SemaphoreType.DMA((1 + N_DEV,))]


def _hosted(body, n_in, n_out, n_scratch, n_comm, mode, grid):
    if not n_comm:
        return body

    def at(corner):
        hit = pl.program_id(0) == corner[0]
        for d in range(1, len(grid)):
            hit = hit & (pl.program_id(d) == corner[d])
        return hit

    def hosted(*refs):
        ins, cin = refs[:n_in], refs[n_in:n_in + n_comm]
        outs, cout = refs[n_in + n_comm:n_in + n_comm + n_out], refs[n_in + n_comm + n_out:n_in + 2 * n_comm + n_out]
        scratch = refs[n_in + 2 * n_comm + n_out:]
        own, sems = scratch[:n_scratch], scratch[n_scratch:]

        @pl.when(at([0] * len(grid)))
        def _():
            _exchange_ops(cin, cout, sems, mode, "start")

        body(*ins, *outs, *own)

        @pl.when(at([n - 1 for n in grid]))
        def _():
            _exchange_ops(cin, cout, sems, mode, "wait")

    return hosted


def _ffn_fwd(x, vec, w13g, w2g, res_w, name, comm=(), tgt=None):
    t_len = x.shape[0]
    tm = min(TM_FFN, t_len)
    nc = len(comm)
    head = tgt is not None

    def body(*refs):
        if head:
            x_ref, vec_ref, w13_hbm, w2_hbm, t_ref, xo_ref, f_ref, gu_ref, loss_ref, w13_v, w2_v, sems = refs
        else:
            x_ref, vec_ref, w13_hbm, w2_hbm, xo_ref, f_ref, gu_ref, w13_v, w2_v, sems = refs

        @pl.when(pl.program_id(0) == 0)
        def _():
            _load_ffn_weights(w13_hbm, w2_hbm, w13_v, w2_v, sems)
            if head:
                loss_ref[...] = jnp.zeros((8, 128), F32)

        x_t = x_ref[...]
        h, _, _, _ = _pre_fwd(x_t, vec_ref)
        hb = h.astype(BF)
        acc = jnp.zeros((tm, D), F32)
        for k in range(FF_CHUNKS):
            g = _dot(hb, w13_v[k])
            u = _dot(hb, w13_v[k + FF_CHUNKS])
            gu_ref[k] = g.astype(BF)
            gu_ref[k + FF_CHUNKS] = u.astype(BF)
            s = g * _sigmoid(g) * u
            acc = acc + _dot(s.astype(BF), w2_v[k])
        f_ref[...] = acc
        xo = _post_fwd(x_t, acc, vec_ref, res_w)
        if head:
            err = xo - t_ref[...]
            xo_ref[...] = err * (1.0 / D)
            loss_ref[...] += jnp.sum(err * err) * (0.5 / D)
        else:
            xo_ref[...] = xo

    nt = t_len // tm
    n_in, n_out = (5, 4) if head else (4, 3)
    outs = pl.pallas_call(
        _hosted(body, n_in, n_out, 3, nc, "gather", (nt,)), grid=(nt,),
        in_specs=[_tile(tm, D), _full((8, D)), _any(), _any()] + ([_tile(tm, D)] if head else []) + [_any()] * nc,
        out_specs=[_tile(tm, D), _tile(tm, D), pl.BlockSpec((N_DEV, tm, FF_PAD), lambda i: (0, i, 0))]
        + ([_full((8, 128))] if head else []) + [_any()] * nc,
        out_shape=[jax.ShapeDtypeStruct((t_len, D), F32), jax.ShapeDtypeStruct((t_len, D), F32),
                   jax.ShapeDtypeStruct((N_DEV, t_len, FF_PAD), BF)]
        + ([jax.ShapeDtypeStruct((8, 128), F32)] if head else []) + _exchange_shapes(comm, "gather"),
        scratch_shapes=_FFN_SCRATCH + (_exchange_scratch(nc) if nc else []), compiler_params=_params(), name=name,
    )(x, vec, w13g, w2g, *([tgt] if head else []), *comm)
    return outs[:n_out], outs[n_out:]


def _ffn_bwd(dout, x, fpre, gu, vec, w13g, w2g, res_w, name, comm=()):
    t_len = x.shape[0]
    tm = min(TM_FFN_BWD, t_len)
    nc = len(comm)

    def body(dout_ref, x_ref, f_ref, gu_ref, vec_ref, w13_hbm, w2_hbm,
             dx_ref, dgu_ref, s_ref, hb_ref, dfb_ref, acc_ref, w13_v, w2_v, sems):
        @pl.when(pl.program_id(0) == 0)
        def _():
            _load_ffn_weights(w13_hbm, w2_hbm, w13_v, w2_v, sems)
            acc_ref[...] = jnp.zeros((8, D), F32)

        dout_t = dout_ref[...]
        df = _post_bwd(dout_t, f_ref[...], vec_ref, acc_ref, res_w)
        dfb = df.astype(BF)
        dfb_ref[...] = dfb
        h, xn, r, n = _pre_fwd(x_ref[...], vec_ref)
        hb_ref[...] = h.astype(BF)
        dh = jnp.zeros((tm, D), F32)
        for k in range(FF_CHUNKS):
            g = gu_ref[k].astype(F32)
            u = gu_ref[k + FF_CHUNKS].astype(F32)
            sig = _sigmoid(g)
            sl = g * sig
            ds = _dot_nt(dfb, w2_v[k])
            dg = (ds * u * (sig + sl * (1.0 - sig))).astype(BF)
            du = (ds * sl).astype(BF)
            s_ref[k] = (sl * u).astype(BF)
            dgu_ref[k] = dg
            dgu_ref[k + FF_CHUNKS] = du
            dh = dh + _dot_nt(dg, w13_v[k]) + _dot_nt(du, w13_v[k + FF_CHUNKS])
        dx_ref[...] = _pre_bwd(dout_t, dh, xn, r, n, vec_ref, acc_ref)

    gu_spec = pl.BlockSpec((N_DEV, tm, FF_PAD), lambda i: (0, i, 0))
    nt = t_len // tm
    outs = pl.pallas_call(
        _hosted(body, 7, 6, 3, nc, "scatter", (nt,)), grid=(nt,),
        in_specs=[_tile(tm, D), _tile(tm, D), _tile(tm, D), gu_spec, _full((8, D)), _any(), _any()] + [_any()] * nc,
        out_specs=[_tile(tm, D), gu_spec, pl.BlockSpec((FF_CHUNKS, tm, FF_PAD), lambda i: (0, i, 0)),
                   _tile(tm, D), _tile(tm, D), _full((8, D))] + [_any()] * nc,
        out_shape=[jax.ShapeDtypeStruct((t_len, D), F32), jax.ShapeDtypeStruct((N_DEV, t_len, FF_PAD), BF),
                   jax.ShapeDtypeStruct((FF_CHUNKS, t_len, FF_PAD), BF), jax.ShapeDtypeStruct((t_len, D), BF),
                   jax.ShapeDtypeStruct((t_len, D), BF), jax.ShapeDtypeStruct((8, D), F32)] + _exchange_shapes(comm, "scatter"),
        scratch_shapes=_FFN_SCRATCH + (_exchange_scratch(nc) if nc else []), compiler_params=_params(), name=name,
    )(dout, x, fpre, gu, vec, w13g, w2g, *comm)
    return outs[:6], outs[6:]


def _wgrad(a, b, j_count, m, n, a_mode, b_mode, out_rows, out_dtype, name, tk=TK_WGRAD, col_slots=1, comm=()):
    t_len = a.shape[-2]
    tk = min(tk, t_len)
    nk = t_len // tk
    wn = n // col_slots
    nc = len(comm)

    def spec(mode, width):
        if mode == "stack":
            return pl.BlockSpec((None, tk, width), lambda j, t: (j, t, 0))
        return pl.BlockSpec((tk, width), lambda j, t: (t, 0))

    def body(a_ref, b_ref, o_ref, acc):
        t = pl.program_id(1)

        @pl.when(t == 0)
        def _():
            acc[...] = jnp.zeros((m, n), F32)

        acc[...] += _dot_tn(a_ref[...], b_ref[...])

        @pl.when(t == nk - 1)
        def _():
            if col_slots == 1:
                o_ref[...] = acc[0:out_rows, :].astype(out_dtype)
            else:
                for s in range(col_slots):
                    o_ref[s] = acc[0:out_rows, wn * s:wn * (s + 1)].astype(out_dtype)

    if col_slots == 1:
        out_spec = pl.BlockSpec((None, out_rows, n), lambda j, t: (j, 0, 0))
        out_shape = jax.ShapeDtypeStruct((j_count, out_rows, n), out_dtype)
    else:
        out_spec = pl.BlockSpec((col_slots, out_rows, wn), lambda j, t: (0, 0, 0))
        out_shape = jax.ShapeDtypeStruct((col_slots, out_rows, wn), out_dtype)
    outs = pl.pallas_call(
        _hosted(body, 2, 1, 1, nc, "scatter", (j_count, nk)), grid=(j_count, nk),
        in_specs=[spec(a_mode, m), spec(b_mode, n)] + [_any()] * nc,
        out_specs=[out_spec] + [_any()] * nc, out_shape=[out_shape] + _exchange_shapes(comm, "scatter"),
        scratch_shapes=[pltpu.VMEM((m, n), F32)] + (_exchange_scratch(nc) if nc else []),
        compiler_params=pltpu.CompilerParams(dimension_semantics=("arbitrary", "arbitrary"), vmem_limit_bytes=VMEM_LIMIT),
        name=name,
    )(a, b, *comm)
    return (outs[0], list(outs[1:])) if nc else outs[0]


def _c_mask_weights(ws_ref, wsm, wsmt):
    row = lax.broadcasted_iota(jnp.int32, (CHUNK, CHUNK), 0)
    col = lax.broadcasted_iota(jnp.int32, (CHUNK, CHUNK), 1)
    for hh in range(H_C):
        w = jnp.where(row >= col, ws_ref[hh], 0.0)
        wsm[hh] = w.astype(BF)
        if wsmt is not None:
            wsmt[hh] = w.T.astype(BF)


def _c_inner(pre, cvec_ref, wsm, bst_ref, mix_sc, tm):
    z, t = _gelu(pre)
    u = z[:, 0:D]
    v = z[:, D:2 * D]
    mu = jnp.mean(v, axis=-1, keepdims=True)
    vc = v - mu
    rstd = lax.rsqrt(jnp.mean(vc * vc, axis=-1, keepdims=True) + EPS)
    vhat = vc * rstd
    vnb = (vhat * cvec_ref[0:1, :] + cvec_ref[1:2, :]).astype(BF)
    for nn in range(tm // CHUNK):
        for hh in range(H_C):
            rows = slice(CHUNK * nn, CHUNK * (nn + 1))
            cols = slice(CHUNK * hh, CHUNK * (hh + 1))
            mix_sc[rows, cols] = _dot(wsm[hh], vnb[rows, cols]) + bst_ref[:, hh:hh + 1]
    return u, t, rstd, vhat, vnb


def _mixc_fwd(x, vec, w_in, b_in, cvec, ws, bst, w_out, name):
    t_len = x.shape[0]
    tm = min(TM_MIX, t_len)

    def body(x_ref, vec_ref, win_ref, bin_ref, cvec_ref, ws_ref, bst_ref, wout_ref,
             xo_ref, f_ref, pre_ref, wsm, mix_sc):
        @pl.when(pl.program_id(0) == 0)
        def _():
            _c_mask_weights(ws_ref, wsm, None)

        x_t = x_ref[...]
        h, _, _, _ = _pre_fwd(x_t, vec_ref)
        hb = h.astype(BF)
        for j in range(N_DEV):
            cols = slice(256 * j, 256 * (j + 1))
            pre_ref[:, cols] = _dot(hb, win_ref[j]) + bin_ref[:, cols]
        u, _, _, _, _ = _c_inner(pre_ref[...], cvec_ref, wsm, bst_ref, mix_sc, tm)
        fpre = _dot((u * mix_sc[...]).astype(BF), wout_ref[...])
        f_ref[...] = fpre
        xo_ref[...] = _post_fwd(x_t, fpre, vec_ref, 1.0)

    return pl.pallas_call(
        body, grid=(t_len // tm,),
        in_specs=[_tile(tm, D), _full((8, D)), _full((N_DEV, D, 256)), _full((1, 2 * D)), _full((8, D)),
                  _full((H_C, CHUNK, CHUNK)), _full((CHUNK, H_C)), _full((D, D))],
        out_specs=[_tile(tm, D), _tile(tm, D), _tile(tm, 2 * D)],
        out_shape=[jax.ShapeDtypeStruct((t_len, D), F32), jax.ShapeDtypeStruct((t_len, D), F32),
                   jax.ShapeDtypeStruct((t_len, 2 * D), F32)],
        scratch_shapes=[pltpu.VMEM((H_C, CHUNK, CHUNK), BF), pltpu.VMEM((tm, D), F32)],
        compiler_params=_params(), name=name,
    )(x, vec, w_in, b_in, cvec, ws, bst, w_out)


def _mixc_bwd(dout, x, fpre, pre, vec, w_in, cvec, ws, bst, w_out, name):
    t_len = x.shape[0]
    tm = min(TM_MIX, t_len)
    nt = t_len // tm

    def body(dout_ref, x_ref, f_ref, pre_ref, vec_ref, win_ref, cvec_ref, ws_ref, bst_ref, wout_ref,
             dx_ref, dpre_ref, p_ref, hb_ref, dfb_ref, acc_ref, dbin_ref, dws_ref, dbst_ref,
             wsm, wsmt, mix_sc, dvn_sc, dmsum):
        i = pl.program_id(0)

        @pl.when(i == 0)
        def _():
            _c_mask_weights(ws_ref, wsm, wsmt)
            acc_ref[...] = jnp.zeros((8, D), F32)
            dbin_ref[...] = jnp.zeros((8, 2 * D), F32)
            dws_ref[...] = jnp.zeros((H_C, CHUNK, CHUNK), F32)
            dmsum[...] = jnp.zeros((CHUNK, D), F32)

        dout_t = dout_ref[...]
        df = _post_bwd(dout_t, f_ref[...], vec_ref, acc_ref, 1.0)
        dfb = df.astype(BF)
        dfb_ref[...] = dfb
        h, xn, r, n = _pre_fwd(x_ref[...], vec_ref)
        hb_ref[...] = h.astype(BF)
        pre_t = pre_ref[...]
        u, t, rstd, vhat, vnb = _c_inner(pre_t, cvec_ref, wsm, bst_ref, mix_sc, tm)
        mix = mix_sc[...]
        p_ref[...] = (u * mix).astype(BF)
        dp = _dot_nt(dfb, wout_ref[...])
        du = dp * mix
        dmix = dp * u
        dmb = dmix.astype(BF)
        for nn in range(tm // CHUNK):
            rows = slice(CHUNK * nn, CHUNK * (nn + 1))
            dmsum[...] += dmix[rows, :]
            for hh in range(H_C):
                cols = slice(CHUNK * hh, CHUNK * (hh + 1))
                dvn_sc[rows, cols] = _dot(wsmt[hh], dmb[rows, cols])
                dws_ref[hh] += _dot_nt(dmb[rows, cols], vnb[rows, cols])
        dvn = dvn_sc[...]
        acc_ref[5:6, :] += _colsum(dvn * vhat)
        acc_ref[6:7, :] += _colsum(dvn)
        dvhat = dvn * cvec_ref[0:1, :]
        dv = rstd * (dvhat - jnp.mean(dvhat, axis=-1, keepdims=True)
                     - vhat * jnp.mean(dvhat * vhat, axis=-1, keepdims=True))
        gg = _gelu_grad(pre_t, t)
        dpre_u = du * gg[:, 0:D]
        dpre_v = dv * gg[:, D:2 * D]
        dbin_ref[0:1, 0:D] += _colsum(dpre_u)
        dbin_ref[0:1, D:2 * D] += _colsum(dpre_v)
        dpre_ref[:, 0:D] = dpre_u.astype(BF)
        dpre_ref[:, D:2 * D] = dpre_v.astype(BF)
        dh = jnp.zeros((tm, D), F32)
        for j in range(N_DEV):
            dh = dh + _dot_nt(dpre_ref[:, 256 * j:256 * (j + 1)], win_ref[j])
        dx_ref[...] = _pre_bwd(dout_t, dh, xn, r, n, vec_ref, acc_ref)

        @pl.when(i == nt - 1)
        def _():
            row = lax.broadcasted_iota(jnp.int32, (CHUNK, CHUNK), 0)
            col = lax.broadcasted_iota(jnp.int32, (CHUNK, CHUNK), 1)
            for hh in range(H_C):
                dws_ref[hh] = jnp.where(row >= col, dws_ref[hh], 0.0)
                dbst_ref[:, hh:hh + 1] = jnp.sum(dmsum[:, CHUNK * hh:CHUNK * (hh + 1)], axis=1, keepdims=True)

    return pl.pallas_call(
        body, grid=(nt,),
        in_specs=[_tile(tm, D), _tile(tm, D), _tile(tm, D), _tile(tm, 2 * D), _full((8, D)), _full((N_DEV, D, 256)),
                  _full((8, D)), _full((H_C, CHUNK, CHUNK)), _full((CHUNK, H_C)), _full((D, D))],
        out_specs=[_tile(tm, D), _tile(tm, 2 * D), _tile(tm, D), _tile(tm, D), _tile(tm, D), _full((8, D)),
                   _full((8, 2 * D)), _full((H_C, CHUNK, CHUNK)), _full((CHUNK, H_C))],
        out_shape=[jax.ShapeDtypeStruct((t_len, D), F32), jax.ShapeDtypeStruct((t_len, 2 * D), BF),
                   jax.ShapeDtypeStruct((t_len, D), BF), jax.ShapeDtypeStruct((t_len, D), BF),
                   jax.ShapeDtypeStruct((t_len, D), BF), jax.ShapeDtypeStruct((8, D), F32),
                   jax.ShapeDtypeStruct((8, 2 * D), F32), jax.ShapeDtypeStruct((H_C, CHUNK, CHUNK), F32),
                   jax.ShapeDtypeStruct((CHUNK, H_C), F32)],
        scratch_shapes=[pltpu.VMEM((H_C, CHUNK, CHUNK), BF), pltpu.VMEM((H_C, CHUNK, CHUNK), BF),
                        pltpu.VMEM((tm, D), F32), pltpu.VMEM((tm, D), F32), pltpu.VMEM((CHUNK, D), F32)],
        compiler_params=_params(), name=name,
    )(dout, x, fpre, pre, vec, w_in, cvec, ws, bst, w_out)


def _gmean(x, g_ref):
    hi = x.astype(BF)
    lo = (x - hi.astype(F32)).astype(BF)
    return _dot(hi, g_ref[...]) + _dot(lo, g_ref[...])


def _log_sigmoid(lam):
    e = jnp.exp(-jnp.abs(lam))
    log1p = jnp.where(e < 1e-2, e * (1.0 - e * (0.5 - e * (1.0 / 3.0 - 0.25 * e))), jnp.log(1.0 + e))
    return jnp.minimum(lam, 0.0) - log1p


def _neg_expm1(y):
    series = -(y * (1.0 + y * (0.5 + y * (1.0 / 6.0 + y * (1.0 / 24.0 + y * (1.0 / 120.0))))))
    return jnp.where(y > -0.1, series, 1.0 - jnp.exp(y))


def _conv_causal(ext, taps_ref, bias, k_taps, halo, tm):
    acc = bias + taps_ref[0:1, :] * ext[halo - k_taps + 1:halo - k_taps + 1 + tm, :]
    for k in range(1, k_taps):
        off = halo - k_taps + 1 + k
        acc = acc + taps_ref[k:k + 1, :] * ext[off:off + tm, :]
    return acc


def _build_shifted(sh_ref, e, n_rows):
    sh_ref[0] = e
    for r in range(1, 8):
        sh_ref[r] = pltpu.roll(e, n_rows - r, 0)


def _shifted_rows(sh_ref, off, tm):
    base = off - off % 8
    return sh_ref[off % 8, base:base + tm, :]


def _scan(a, u, tm, reverse):
    row = lax.broadcasted_iota(jnp.int32, (tm, W_A), 0)
    d = 1
    while d < tm:
        if reverse:
            keep = row < tm - d
            shift = tm - d
        else:
            keep = row >= d
            shift = d
        a_sh = jnp.where(keep, pltpu.roll(a, shift, 0), 1.0)
        u_sh = jnp.where(keep, pltpu.roll(u, shift, 0), 0.0)
        u = a * u_sh + u
        a = a * a_sh
        d *= 2
    return a, u


def _a_gates(xc, cv_ref, wr_ref, wi_ref):
    xcb = xc.astype(BF)
    r = _sigmoid(_dot(xcb, wr_ref[...]) + cv_ref[5:6, :])
    ig = _sigmoid(_dot(xcb, wi_ref[...]) + cv_ref[6:7, :])
    ls = _log_sigmoid(cv_ref[7:8, :])
    la = LRU_C * r * ls
    a = jnp.exp(la)
    m = jnp.sqrt(_neg_expm1(2.0 * la))
    return xcb, r, ig, ls, a, m


def _b_norm(vc, cv_ref, g_ref):
    mu = _gmean(vc, g_ref)
    dv = vc - mu
    rstd = lax.rsqrt(_gmean(dv * dv, g_ref) + EPS)
    vhat = dv * rstd
    vln = vhat * cv_ref[9:10, :] + cv_ref[10:11, :]
    return rstd, vhat, vln


def _mixab_fwd(x, vec, w_in, cv, w31, wr, wi, gmat, w_out, name):
    t_len = x.shape[0]
    tm = min(TM_MIX, t_len)

    def body(x_ref, vec_ref, win_ref, cv_ref, w31_ref, wr_ref, wi_ref, g_ref, wout_ref,
             xo_ref, f_ref, z_ref, hs_ref, cvs_ref, ext_a, ext_b, hc, shifted):
        @pl.when(pl.program_id(0) == 0)
        def _():
            ext_a[0:HALO_A, :] = jnp.zeros((HALO_A, W_A), F32)
            ext_b[0:HALO_B, :] = jnp.zeros((HALO_B, W_B), F32)
            hc[...] = jnp.zeros((8, W_A), F32)

        x_t = x_ref[...]
        h, _, _, _ = _pre_fwd(x_t, vec_ref)
        hb = h.astype(BF)
        for j in range(N_DEV):
            z_ref[:, 256 * j:256 * (j + 1)] = _dot(hb, win_ref[j])
        ext_a[HALO_A:HALO_A + tm, :] = z_ref[:, W_A:2 * W_A]
        xc = _conv_causal(ext_a, cv_ref, cv_ref[4:5, :], CONV_A, HALO_A, tm)
        ext_a[0:HALO_A, :] = ext_a[tm:tm + HALO_A, :]
        cvs_ref[:, 0:W_A] = xc
        _, _, ig, _, a, m = _a_gates(xc, cv_ref, wr_ref, wi_ref)
        a_cum, hloc = _scan(a, m * ig * xc, tm, False)
        hs = hloc + a_cum * hc[0:1, :]
        hs_ref[...] = hs
        hc[0:1, :] = hs[tm - 1:tm, :]
        gel, _ = _gelu(z_ref[:, 0:W_A])
        ya = hs * gel
        ext_b[HALO_B:HALO_B + tm, :] = z_ref[:, 2 * W_A:2 * W_A + W_B] * _sigmoid(z_ref[:, 2 * W_A + W_B:2 * W_A + 2 * W_B])
        _build_shifted(shifted, ext_b[...], tm + HALO_B)
        vc = cv_ref[8:9, :] + w31_ref[0:1, :] * _shifted_rows(shifted, HALO_B - CONV_B + 1, tm)
        for k in range(1, CONV_B):
            vc = vc + w31_ref[k:k + 1, :] * _shifted_rows(shifted, HALO_B - CONV_B + 1 + k, tm)
        ext_b[0:HALO_B, :] = ext_b[tm:tm + HALO_B, :]
        cvs_ref[:, W_A:W_A + W_B] = vc
        _, _, vln = _b_norm(vc, cv_ref, g_ref)
        yb = vln * _sigmoid(vln)
        fpre = _dot(ya.astype(BF), wout_ref[0:W_A, :]) + _dot(yb.astype(BF), wout_ref[W_A:W_A + W_B, :])
        f_ref[...] = fpre
        xo_ref[...] = _post_fwd(x_t, fpre, vec_ref, 1.0)

    return pl.pallas_call(
        body, grid=(t_len // tm,),
        in_specs=[_tile(tm, D), _full((8, D)), _full((N_DEV, D, 256)), _full((16, W_A)), _full((32, W_B)),
                  _full((W_A, W_A)), _full((W_A, W_A)), _full((W_B, W_B)), _full((D, D))],
        out_specs=[_tile(tm, D), _tile(tm, D), _tile(tm, 2 * D), _tile(tm, W_A), _tile(tm, W_A + W_B)],
        out_shape=[jax.ShapeDtypeStruct((t_len, D), F32), jax.ShapeDtypeStruct((t_len, D), F32),
                   jax.ShapeDtypeStruct((t_len, 2 * D), F32), jax.ShapeDtypeStruct((t_len, W_A), F32),
                   jax.ShapeDtypeStruct((t_len, W_A + W_B), F32)],
        scratch_shapes=[pltpu.VMEM((tm + HALO_A, W_A), F32), pltpu.VMEM((tm + HALO_B, W_B), F32), pltpu.VMEM((8, W_A), F32),
                        pltpu.VMEM((8, tm + HALO_B, W_B), F32)],
        compiler_params=_params(), name=name,
    )(x, vec, w_in, cv, w31, wr, wi, gmat, w_out)


def _mixab_bwd(dout, x, fpre, z, cvs, hs, vec, w_in, cv, w31, wr, wi, gmat, w_out, name, comm=()):
    t_len = x.shape[0]
    tm = min(TM_MIX, t_len)
    nt = t_len // tm

    def rev(i):
        return nt - 1 - i

    def rtile(ncol):
        return pl.BlockSpec((tm, ncol), lambda i: (rev(i), 0))

    def body(dout_ref, x_ref, f_ref, z_ref, cvs_ref, hs_ref, hsp_ref, vec_ref, win_ref, cv_ref, w31_ref, wr_ref, wi_ref,
             g_ref, wout_ref,
             dx_ref, dz_ref, yab_ref, hb_ref, dfb_ref, xcb_ref, dri_ref, acc_ref, accs_ref, dw31_ref,
             ext_h, ext_dx, ext_dv, carry, shifted):
        i = pl.program_id(0)
        has_prev = (rev(i) > 0).astype(F32)

        @pl.when(i == 0)
        def _():
            acc_ref[...] = jnp.zeros((8, D), F32)
            accs_ref[...] = jnp.zeros((16, W_A), F32)
            dw31_ref[...] = jnp.zeros((32, W_B), F32)
            ext_dx[tm:tm + HALO_A, :] = jnp.zeros((HALO_A, W_A), F32)
            ext_dv[tm:tm + HALO_B, :] = jnp.zeros((HALO_B, W_B), F32)
            carry[...] = jnp.zeros((8, W_A), F32)

        dout_t = dout_ref[...]
        df = _post_bwd(dout_t, f_ref[...], vec_ref, acc_ref, 1.0)
        dfb = df.astype(BF)
        dfb_ref[...] = dfb
        h, xn, r_x, n = _pre_fwd(x_ref[...], vec_ref)
        hb_ref[...] = h.astype(BF)

        ag = z_ref[:, 0:W_A]
        ax = z_ref[:, W_A:2 * W_A]
        bv = z_ref[:, 2 * W_A:2 * W_A + W_B]
        sg = _sigmoid(z_ref[:, 2 * W_A + W_B:2 * W_A + 2 * W_B])
        vv = bv * sg
        xc = cvs_ref[:, 0:W_A]
        xcb, r, ig, ls, a, m = _a_gates(xc, cv_ref, wr_ref, wi_ref)
        xcb_ref[...] = xcb
        hs_t = hs_ref[...]
        ext_h[0:8, :] = hsp_ref[...] * has_prev
        ext_h[8:8 + tm, :] = hs_t
        hprev = ext_h[7:7 + tm, :]
        gel, tg = _gelu(ag)
        rstd, vhat, vln = _b_norm(cvs_ref[:, W_A:W_A + W_B], cv_ref, g_ref)
        sv = _sigmoid(vln)
        yab_ref[:, 0:W_A] = (hs_t * gel).astype(BF)
        yab_ref[:, W_A:W_A + W_B] = (vln * sv).astype(BF)

        dya = _dot_nt(dfb, wout_ref[0:W_A, :])
        dyb = _dot_nt(dfb, wout_ref[W_A:W_A + W_B, :])

        dag = dya * hs_t * _gelu_grad(ag, tg)
        row = lax.broadcasted_iota(jnp.int32, (tm, W_A), 0)
        last = row == tm - 1
        a_next = jnp.where(last, 1.0, pltpu.roll(a, tm - 1, 0))
        u0 = dya * gel + jnp.where(last, carry[0:1, :], 0.0)
        _, dhs = _scan(a_next, u0, tm, True)
        carry[0:1, :] = a[0:1, :] * dhs[0:1, :]
        da = dhs * hprev
        dm = dhs * ig * xc
        di = dhs * m * xc
        dxc = dhs * m * ig
        dla = da * a - dm * (a * a) / m
        accs_ref[7:8, :] += _colsum(dla * r) * (LRU_C * _sigmoid(-cv_ref[7:8, :]))
        drp = (dla * (LRU_C * ls)) * r * (1.0 - r)
        dip = di * ig * (1.0 - ig)
        accs_ref[5:6, :] += _colsum(drp)
        accs_ref[6:7, :] += _colsum(dip)
        drpb = drp.astype(BF)
        dipb = dip.astype(BF)
        dri_ref[:, 0:W_A] = drpb
        dri_ref[:, W_A:2 * W_A] = dipb
        dxc = dxc + _dot_nt(drpb, wr_ref[...]) + _dot_nt(dipb, wi_ref[...])
        accs_ref[4:5, :] += _colsum(dxc)
        ext_dx[0:tm, :] = dxc
        dax = jnp.zeros((tm, W_A), F32)
        for k in range(CONV_A):
            ahead = ext_dx[CONV_A - 1 - k:CONV_A - 1 - k + tm, :]
            accs_ref[k:k + 1, :] += _colsum(ax * ahead)
            dax = dax + cv_ref[k:k + 1, :] * ahead
        ext_dx[tm:tm + HALO_A, :] = dxc[0:HALO_A, :]

        dvln = dyb * (sv * (1.0 + vln * (1.0 - sv)))
        accs_ref[9:10, :] += _colsum(dvln * vhat)
        accs_ref[10:11, :] += _colsum(dvln)
        dvhat = dvln * cv_ref[9:10, :]
        dvc = rstd * (dvhat - _gmean(dvhat, g_ref) - vhat * _gmean(dvhat * vhat, g_ref))
        accs_ref[8:9, :] += _colsum(dvc)
        ext_dv[0:tm, :] = dvc
        _build_shifted(shifted, ext_dv[...], tm + HALO_B)
        dvv = jnp.zeros((tm, W_B), F32)
        for k in range(CONV_B):
            ahead = _shifted_rows(shifted, CONV_B - 1 - k, tm)
            dw31_ref[k:k + 1, :] += _colsum(vv * ahead)
            dvv = dvv + w31_ref[k:k + 1, :] * ahead
        ext_dv[tm:tm + HALO_B, :] = dvc[0:HALO_B, :]

        dz_ref[:, 0:W_A] = dag.astype(BF)
        dz_ref[:, W_A:2 * W_A] = dax.astype(BF)
        dz_ref[:, 2 * W_A:2 * W_A + W_B] = (dvv * sg).astype(BF)
        dz_ref[:, 2 * W_A + W_B:2 * W_A + 2 * W_B] = (dvv * vv * (1.0 - sg)).astype(BF)
        dh = jnp.zeros((tm, D), F32)
        for j in range(N_DEV):
            dh = dh + _dot_nt(dz_ref[:, 256 * j:256 * (j + 1)], win_ref[j])
        dx_ref[...] = _pre_bwd(dout_t, dh, xn, r_x, n, vec_ref, acc_ref)

    hsp_spec = pl.BlockSpec((8, W_A), lambda i: (jnp.maximum(rev(i) * (tm // 8) - 1, 0), 0))
    nc = len(comm)
    outs = pl.pallas_call(
        _hosted(body, 15, 10, 5, nc, "scatter", (nt,)), grid=(nt,),
        in_specs=[rtile(D), rtile(D), rtile(D), rtile(2 * D), rtile(W_A + W_B), rtile(W_A), hsp_spec, _full((8, D)),
                  _full((N_DEV, D, 256)), _full((16, W_A)), _full((32, W_B)), _full((W_A, W_A)), _full((W_A, W_A)),
                  _full((W_B, W_B)), _full((D, D))] + [_any()] * nc,
        out_specs=[rtile(D), rtile(2 * D), rtile(D), rtile(D), rtile(D), rtile(W_A), rtile(2 * W_A), _full((8, D)),
                   _full((16, W_A)), _full((32, W_B))] + [_any()] * nc,
        out_shape=[jax.ShapeDtypeStruct((t_len, D), F32), jax.ShapeDtypeStruct((t_len, 2 * D), BF),
                   jax.ShapeDtypeStruct((t_len, D), BF), jax.ShapeDtypeStruct((t_len, D), BF),
                   jax.ShapeDtypeStruct((t_len, D), BF), jax.ShapeDtypeStruct((t_len, W_A), BF),
                   jax.ShapeDtypeStruct((t_len, 2 * W_A), BF), jax.ShapeDtypeStruct((8, D), F32),
                   jax.ShapeDtypeStruct((16, W_A), F32), jax.ShapeDtypeStruct((32, W_B), F32)] + _exchange_shapes(comm, "scatter"),
        scratch_shapes=[pltpu.VMEM((tm + 8, W_A), F32), pltpu.VMEM((tm + HALO_A, W_A), F32),
                        pltpu.VMEM((tm + HALO_B, W_B), F32), pltpu.VMEM((8, W_A), F32),
                        pltpu.VMEM((8, tm + HALO_B, W_B), F32)] + (_exchange_scratch(nc) if nc else []),
        compiler_params=_params(), name=name,
    )(dout, x, fpre, z, cvs, hs, hs, vec, w_in, cv, w31, wr, wi, gmat, w_out, *comm)
    return outs[:10], list(outs[10:])


def _vec(p, l, j):
    return jnp.concatenate([p["mod"][l, j], p["norm_pre"][l, j][None], p["norm_post"][l, j][None], jnp.zeros((3, D), F32)], 0)


def _ab_consts(p):
    gw = p["a_gate_w"]
    gb = p["a_gate_b"]
    half = W_A // 8
    wr = jax.scipy.linalg.block_diag(*[gw[hh, :, 0:half] for hh in range(8)]).astype(BF)
    wi = jax.scipy.linalg.block_diag(*[gw[hh, :, half:2 * half] for hh in range(8)]).astype(BF)
    rows = [p["a_conv_w"], p["a_conv_b"][None], gb[:, 0:half].reshape(1, W_A), gb[:, half:2 * half].reshape(1, W_A),
            p["a_lam"][None], p["b_conv_b"][None], p["b_norm_g"][None], p["b_norm_b"][None], jnp.zeros((5, W_A), F32)]
    cv = jnp.concatenate(rows, 0)
    w31 = jnp.concatenate([p["b_conv_w"], jnp.zeros((1, W_B), F32)], 0)
    grp = jnp.arange(W_B) // (W_B // 8)
    gmat = ((grp[:, None] == grp[None, :]).astype(F32) / (W_B // 8)).astype(BF)
    return cv, w31, wr, wi, gmat


SUBLAYERS = ("f0", "ab", "f1", "f2", "c", "f3")


def _local_step(x, tgt, p, plan=None):
    g = {}
    saved = []
    cur = x
    wsets = dict(p["wsets"])
    ab_c = _ab_consts(p)
    c_cvec = jnp.concatenate([p["c_norm_g"][None], p["c_norm_b"][None], jnp.zeros((6, D), F32)], 0)
    c_bst = jnp.transpose(p["c_b_s"])
    c_bin = p["c_b_in"][None]
    for s_idx, wname in enumerate(SUBLAYERS):
        l, j = divmod(s_idx, 3)
        vec = _vec(p, l, j)
        tag = f"l{l}s{j}"
        if j != 1:
            names = plan["gather"].get(s_idx, []) if plan else []
            comm = [a for nm in names for a in plan["shards"][nm]]
            res, got = _ffn_fwd(cur, vec, *wsets[wname], 0.5, "ffn_fwd_" + tag, comm=comm, tgt=tgt if s_idx == 5 else None)
            nxt, fpre, gu = res[:3]
            for k, nm in enumerate(names):
                wsets[nm] = list(got[2 * k:2 * k + 2])
            saved.append((cur, fpre, gu, vec))
            if s_idx == 5:
                loss_blk = res[3]
        elif l == 0:
            w_in, w_out = wsets[wname]
            nxt, fpre, z, hs, cvs = _mixab_fwd(cur, vec, w_in, *ab_c, w_out.reshape(D, D), "mixab_fwd_" + tag)
            saved.append((cur, fpre, z, hs, cvs, vec))
        else:
            w_in, w_out = wsets[wname]
            nxt, fpre, pre = _mixc_fwd(cur, vec, w_in, c_bin, c_cvec, p["c_w_s"], c_bst, w_out.reshape(D, D), "mixc_fwd_" + tag)
            saved.append((cur, fpre, pre, vec))
        cur = nxt
    dcur = cur
    accs, pending, recv = {}, {}, {}

    def take(host):
        keys = plan["scatter"].get(host, []) if plan else []
        return keys, [pending.pop(k) for k in keys]

    def put(keys, got):
        recv.update(zip(keys, got))

    for s_idx in reversed(range(6)):
        wname = SUBLAYERS[s_idx]
        l, j = divmod(s_idx, 3)
        tag = f"l{l}s{j}"
        sv = saved[s_idx]
        if j != 1:
            keys, comm = take("ffn_bwd_" + tag)
            xin, fpre, gu, vec = sv
            (dcur, dgu, s, hb, dfb, acc), got = _ffn_bwd(dcur, xin, fpre, gu, vec, *wsets[wname], 0.5, "ffn_bwd_" + tag, comm=comm)
            put(keys, got)
            keys, comm = take("wgrad_w13_" + tag)
            dw13 = _wgrad(hb, dgu, N_DEV, D, FF_PAD, "share", "stack", D, BF, "wgrad_w13_" + tag, comm=comm)
            if keys:
                dw13, got = dw13
                put(keys, got)
            pending[wname + ".0"] = dw13
            keys, comm = take("wgrad_w2_" + tag)
            dw2 = _wgrad(s, dfb, FF_CHUNKS, FF_PAD, D, "stack", "share", FF_SHARD, BF, "wgrad_w2_" + tag, comm=comm)
            if keys:
                dw2, got = dw2
                put(keys, got)
            pending[wname + ".1"] = dw2.reshape(N_DEV, W2_SHARD, D)
        elif l == 0:
            xin, fpre, z, hs, cvs, vec = sv
            w_in, w_out = wsets[wname]
            keys, comm = take("mixab_bwd_" + tag)
            (dcur, dz, yab, hb, dfb, xcb, dri, acc, accs_ab, dw31), got = _mixab_bwd(
                dcur, xin, fpre, z, cvs, hs, vec, w_in, *ab_c, w_out.reshape(D, D), "mixab_bwd_" + tag, comm=comm)
            put(keys, got)
            d_in = _wgrad(hb, dz, 1, D, 2 * D, "share", "share", D, BF, "wgrad_ab_in", tk=TK_WGRAD_WIDE, col_slots=N_DEV)
            d_out = _wgrad(yab, dfb, 1, D, D, "share", "share", D, BF, "wgrad_ab_out")
            pending[wname + ".0"], pending[wname + ".1"] = d_in, d_out.reshape(N_DEV, D // N_DEV, D)
            g["gate"] = _wgrad(xcb, dri, 1, W_A, 2 * W_A, "share", "share", W_A, F32, "wgrad_gate")
            g["accs_ab"] = accs_ab
            g["dw31"] = dw31
        else:
            xin, fpre, pre, vec = sv
            w_in, w_out = wsets[wname]
            dcur, dpre, pb, hb, dfb, acc, dbin, dws, dbst = _mixc_bwd(
                dcur, xin, fpre, pre, vec, w_in, c_cvec, p["c_w_s"], c_bst, w_out.reshape(D, D), "mixc_bwd_" + tag)
            d_in = _wgrad(hb, dpre, 1, D, 2 * D, "share", "share", D, BF, "wgrad_c_in", tk=TK_WGRAD_WIDE, col_slots=N_DEV)
            d_out = _wgrad(pb, dfb, 1, D, D, "share", "share", D, BF, "wgrad_c_out")
            pending[wname + ".0"], pending[wname + ".1"] = d_in, d_out.reshape(N_DEV, D // N_DEV, D)
            g["c_small"] = (acc, dbin, dws, dbst)
        accs[f"{l}{j}"] = acc
    g["accs"] = accs
    g["pending"] = pending
    g["recv"] = recv
    return loss_blk, dcur, g


def _exchange_ops(ins, outs, sems, mode, action):
    send_sems, recv_sems, loc_sems = sems
    n = len(ins)
    x, y, c = lax.axis_index("x"), lax.axis_index("y"), lax.axis_index("c")
    me = 4 * x + 2 * y + c

    def src(i, dev):
        return ins[i] if mode == "gather" else ins[i].at[dev]

    for i in range(n):
        cp = pltpu.make_async_copy(src(i, me), outs[i].at[me], loc_sems.at[i])
        if action == "start":
            cp.start()
        else:
            cp.wait()
    for mask in range(1, N_DEV):
        px = 1 - x if mask & 4 else x
        py = 1 - y if mask & 2 else y
        pc = 1 - c if mask & 1 else c
        peer = 4 * px + 2 * py + pc
        for i in range(n):
            k = i * (N_DEV - 1) + mask - 1
            cp = pltpu.make_async_remote_copy(
                src_ref=src(i, peer), dst_ref=outs[i].at[me if action == "start" else peer],
                send_sem=send_sems.at[k], recv_sem=recv_sems.at[k],
                device_id=(px, py, pc), device_id_type=pl.DeviceIdType.MESH)
            if action == "start":
                cp.start()
            else:
                cp.wait()


def _exchange_scratch(n):
    return [pltpu.SemaphoreType.DMA((n * (N_DEV - 1),)), pltpu.SemaphoreType.DMA((n * (N_DEV - 1),)),
            pltpu.SemaphoreType.DMA((n,))]


def _exchange_shapes(arrays, mode):
    return [jax.ShapeDtypeStruct(((N_DEV,) + a.shape) if mode == "gather" else a.shape, a.dtype) for a in arrays]


def _exchange(arrays, mode, name):
    n = len(arrays)

    def body(*refs):
        ins, outs, sems = refs[:n], refs[n:2 * n], refs[2 * n:]
        _exchange_ops(ins, outs, sems, mode, "start")
        _exchange_ops(ins, outs, sems, mode, "wait")

    return pl.pallas_call(
        body, in_specs=[pl.BlockSpec(memory_space=pl.ANY)] * n, out_specs=[pl.BlockSpec(memory_space=pl.ANY)] * n,
        out_shape=_exchange_shapes(arrays, mode), scratch_shapes=_exchange_scratch(n), name=name,
    )(*arrays)


def _sum_slots(a, name):
    def body(a_ref, o_ref):
        acc = a_ref[0]
        for s in range(1, N_DEV):
            acc = acc + a_ref[s]
        o_ref[...] = acc

    return pl.pallas_call(body, out_shape=jax.ShapeDtypeStruct(a.shape[1:], F32), name=name,
                          compiler_params=pltpu.CompilerParams(vmem_limit_bytes=VMEM_LIMIT))(a)


def _pack(pieces, mult):
    flat = jnp.concatenate([q.reshape(-1).astype(F32) for q in pieces])
    size = -(-flat.shape[0] // mult) * mult
    return jnp.pad(flat, (0, size - flat.shape[0])).reshape(size // 128, 128)


def _unpack(flat, shapes):
    out, off = [], 0
    for shp in shapes:
        size = math.prod(shp)
        out.append(flat[..., off:off + size].reshape(flat.shape[:-1] + tuple(shp)))
        off += size
    return out


def _mod_part(c_all, ada_w, ada_b_mine, name):
    cols = ada_w.shape[-1]

    def body(c_ref, w_ref, b_ref, o_ref):
        cv = c_ref[...]
        ca = cv * _sigmoid(cv)
        for l in range(2):
            o_ref[l] = jnp.dot(ca, w_ref[l], preferred_element_type=F32, precision=lax.Precision.HIGHEST) + b_ref[l:l + 1, :]

    return pl.pallas_call(body, out_shape=jax.ShapeDtypeStruct((2, N_DEV, cols), F32), name=name,
                          compiler_params=pltpu.CompilerParams(vmem_limit_bytes=VMEM_LIMIT))(c_all, ada_w, ada_b_mine)


def _ada_w_grad(c_all_t, dmod_mine, name):
    cols = dmod_mine.shape[-1]

    def body(ct_ref, d_ref, o_ref):
        cv = ct_ref[...]
        ca = cv * _sigmoid(cv)
        for l in range(2):
            acc = ca[:, 0:1] * d_ref[l, 0:1, :]
            for b in range(1, N_DEV):
                acc = acc + ca[:, b:b + 1] * d_ref[l, b:b + 1, :]
            o_ref[l] = acc

    return pl.pallas_call(body, out_shape=jax.ShapeDtypeStruct((2, D, cols), F32), name=name,
                          compiler_params=pltpu.CompilerParams(vmem_limit_bytes=VMEM_LIMIT))(c_all_t, dmod_mine)


def _adamw_math(w, g, m, v):
    m2 = ADAM_B1 * m + (1.0 - ADAM_B1) * g
    v2 = ADAM_B2 * v + (1.0 - ADAM_B2) * (g * g)
    m_hat = m2 / (1.0 - ADAM_B1 ** ADAM_STEP)
    v_hat = v2 / (1.0 - ADAM_B2 ** ADAM_STEP)
    delta = -ADAM_LR * (m_hat / (jnp.sqrt(v_hat) + ADAM_EPS) + ADAM_WD * w)
    return delta, m2, v2


def _adamw_big(w, g, m, v, name):
    n_l, rows, cols = w.shape
    parts = list(g) if isinstance(g, (list, tuple)) else None
    sizes = (512, 352, 256, 128, 64, 32, 16, 8) if parts is None or len(parts) == 1 else (128, 64, 32, 16, 8)
    br = next(b for b in sizes if rows % b == 0)
    nr = rows // br

    def body(*refs):
        w_ref, g_refs, (m_ref, v_ref, go_ref, d_ref, mo_ref, vo_ref) = refs[0], refs[1:-6], refs[-6:]

        def update(gsum):
            go_ref[...] = gsum
            d_ref[...], mo_ref[...], vo_ref[...] = _adamw_math(w_ref[...], gsum, m_ref[...], v_ref[...])

        if parts is None:
            update(g_refs[0][...])
        else:
            for l, g_ref in enumerate(g_refs):
                @pl.when(pl.program_id(0) == l)
                def _(g_ref=g_ref):
                    gsum = g_ref[0, :, 0:cols].astype(F32)
                    for s in range(1, N_DEV):
                        gsum = gsum + g_ref[s, :, 0:cols].astype(F32)
                    update(gsum)

    blk = pl.BlockSpec((None, br, cols), lambda l, i: (l, i, 0))
    if parts is None:
        g_specs, g_args = [blk], [g]
    else:
        def part_spec(l_mine, width):
            return pl.BlockSpec((N_DEV, br, width),
                                lambda l, i: (0, jnp.where(l < l_mine, 0, jnp.where(l == l_mine, i, nr - 1)), 0))
        g_specs, g_args = [part_spec(l, p.shape[-1]) for l, p in enumerate(parts)], parts
    shp = jax.ShapeDtypeStruct((n_l, rows, cols), F32)
    return pl.pallas_call(
        body, grid=(n_l, nr), in_specs=[blk] + g_specs + [blk, blk], out_specs=[blk] * 4, out_shape=[shp] * 4,
        compiler_params=pltpu.CompilerParams(dimension_semantics=("arbitrary", "arbitrary"), vmem_limit_bytes=VMEM_LIMIT),
        name=name)(w, *g_args, m, v)


def _adamw_small(ws, gs, ms, vs, name):
    n = len(ws)

    def body(*refs):
        for i in range(n):
            w_ref, g_ref, m_ref, v_ref = (refs[k * n + i] for k in range(4))
            d_ref, mo_ref, vo_ref = (refs[(4 + k) * n + i] for k in range(3))
            d_ref[...], mo_ref[...], vo_ref[...] = _adamw_math(w_ref[...], g_ref[...], m_ref[...], v_ref[...])

    shapes = [jax.ShapeDtypeStruct(w.shape, F32) for w in ws]
    outs = pl.pallas_call(body, out_shape=shapes * 3, name=name,
                          compiler_params=pltpu.CompilerParams(vmem_limit_bytes=VMEM_LIMIT))(*ws, *gs, *ms, *vs)
    return outs[:n], outs[n:2 * n], outs[2 * n:]


def _as2d(a):
    return a.reshape(-1, a.shape[-1])


def kernel(x, c, ada_w, ada_b, norm_pre, norm_post, ffn_w13, ffn_w2, ab_w_in, a_conv_w, a_conv_b, a_gate_w, a_gate_b, a_lam, b_conv_w, b_conv_b, b_norm_g, b_norm_b, ab_w_out, c_w_in, c_b_in, c_norm_g, c_norm_b, c_w_s, c_b_s, c_w_out, loss_target, m_ada_w, m_ada_b, m_norm_pre, m_norm_post, m_ffn_w13, m_ffn_w2, m_ab_w_in, m_a_conv_w, m_a_conv_b, m_a_gate_w, m_a_gate_b, m_a_lam, m_b_conv_w, m_b_conv_b, m_b_norm_g, m_b_norm_b, m_ab_w_out, m_c_w_in, m_c_b_in, m_c_norm_g, m_c_norm_b, m_c_w_s, m_c_b_s, m_c_w_out, v_ada_w, v_ada_b, v_norm_pre, v_norm_post, v_ffn_w13, v_ffn_w2, v_ab_w_in, v_a_conv_w, v_a_conv_b, v_a_gate_w, v_a_gate_b, v_a_lam, v_b_conv_w, v_b_conv_b, v_b_norm_g, v_b_norm_b, v_ab_w_out, v_c_w_in, v_c_b_in, v_c_norm_g, v_c_norm_b, v_c_w_s, v_c_b_s, v_c_w_out):
    me = 4 * lax.axis_index("x") + 2 * lax.axis_index("y") + lax.axis_index("c")
    weights = dict(ada_w=ada_w, ada_b=ada_b, norm_pre=norm_pre, norm_post=norm_post, ffn_w13=ffn_w13, ffn_w2=ffn_w2,
                   ab_w_in=ab_w_in, a_conv_w=a_conv_w, a_conv_b=a_conv_b, a_gate_w=a_gate_w, a_gate_b=a_gate_b, a_lam=a_lam,
                   b_conv_w=b_conv_w, b_conv_b=b_conv_b, b_norm_g=b_norm_g, b_norm_b=b_norm_b, ab_w_out=ab_w_out,
                   c_w_in=c_w_in, c_b_in=c_b_in, c_norm_g=c_norm_g, c_norm_b=c_norm_b, c_w_s=c_w_s, c_b_s=c_b_s, c_w_out=c_w_out)
    moms = dict(ada_w=m_ada_w, ada_b=m_ada_b, norm_pre=m_norm_pre, norm_post=m_norm_post, ffn_w13=m_ffn_w13, ffn_w2=m_ffn_w2,
                ab_w_in=m_ab_w_in, a_conv_w=m_a_conv_w, a_conv_b=m_a_conv_b, a_gate_w=m_a_gate_w, a_gate_b=m_a_gate_b,
                a_lam=m_a_lam, b_conv_w=m_b_conv_w, b_conv_b=m_b_conv_b, b_norm_g=m_b_norm_g, b_norm_b=m_b_norm_b,
                ab_w_out=m_ab_w_out, c_w_in=m_c_w_in, c_b_in=m_c_b_in, c_norm_g=m_c_norm_g, c_norm_b=m_c_norm_b,
                c_w_s=m_c_w_s, c_b_s=m_c_b_s, c_w_out=m_c_w_out)
    vars_ = dict(ada_w=v_ada_w, ada_b=v_ada_b, norm_pre=v_norm_pre, norm_post=v_norm_post, ffn_w13=v_ffn_w13, ffn_w2=v_ffn_w2,
                 ab_w_in=v_ab_w_in, a_conv_w=v_a_conv_w, a_conv_b=v_a_conv_b, a_gate_w=v_a_gate_w, a_gate_b=v_a_gate_b,
                 a_lam=v_a_lam, b_conv_w=v_b_conv_w, b_conv_b=v_b_conv_b, b_norm_g=v_b_norm_g, b_norm_b=v_b_norm_b,
                 ab_w_out=v_ab_w_out, c_w_in=v_c_w_in, c_b_in=v_c_b_in, c_norm_g=v_c_norm_g, c_norm_b=v_c_norm_b,
                 c_w_s=v_c_w_s, c_b_s=v_c_b_s, c_w_out=v_c_w_out)
    names = list(weights)

    w13b = jnp.pad(ffn_w13.astype(BF).reshape(4, D, FF_SHARD), ((0, 0), (0, 0), (0, FF_PAD - FF_SHARD)))
    small_shapes = [(D,), (2, 3, 128), (2, 3, 128), (CONV_A, 64), (CONV_B, 64), (256,), (128,), (128,)]
    small = _pack([c, norm_pre, norm_post, a_conv_w, b_conv_w, c_b_in, c_norm_g, c_norm_b], 1024)
    w2b = ffn_w2.astype(BF).reshape(4, W2_SHARD, D)
    shards = {f"f{f}": [w13b[f], w2b[f]] for f in range(4)}
    shards["ab"] = [ab_w_in[0].astype(BF), ab_w_out[0].astype(BF)]
    shards["c"] = [c_w_in[0].astype(BF), c_w_out[0].astype(BF)]
    w13g0, w2g0, small_g = _exchange(shards["f0"] + [small], "gather", "gather_first")
    plan = dict(shards=shards, gather={0: ["ab", "f1"], 2: ["f2"], 3: ["c", "f3"]},
                scatter={"ffn_bwd_l1s0": ["f3.0", "f3.1", "c.0", "c.1"], "ffn_bwd_l0s2": ["f2.0", "f2.1"],
                         "mixab_bwd_l0s1": ["f1.0", "f1.1"], "wgrad_w13_l0s0": ["ab.0", "ab.1"], "wgrad_w2_l0s0": ["f0.0"]})
    c_all, npre_g, npost_g, acw_g, bcw_g, cbin_g, cng_g, cnb_g = _unpack(small_g.reshape(N_DEV, -1), small_shapes)

    def cat_last(a):
        return jnp.moveaxis(a, 0, -2).reshape(a.shape[1:-1] + (N_DEV * a.shape[-1],))

    ada_b_mine = lax.dynamic_slice_in_dim(ada_b, me * ada_w.shape[-1], ada_w.shape[-1], axis=1)
    (mod_g,) = _exchange([_mod_part(c_all, ada_w, ada_b_mine, "mod_part")], "gather", "gather_mod")
    mod = cat_last(lax.dynamic_index_in_dim(mod_g, me, axis=2, keepdims=False)).reshape(2, 3, 3, D)

    p = dict(mod=mod, norm_pre=cat_last(npre_g), norm_post=cat_last(npost_g), wsets={"f0": [w13g0, w2g0]},
             a_conv_w=cat_last(acw_g), a_conv_b=a_conv_b[0], a_gate_w=a_gate_w[0], a_gate_b=a_gate_b[0], a_lam=a_lam[0],
             b_conv_w=cat_last(bcw_g), b_conv_b=b_conv_b[0], b_norm_g=b_norm_g[0], b_norm_b=b_norm_b[0],
             c_b_in=cat_last(cbin_g), c_norm_g=cat_last(cng_g), c_norm_b=cat_last(cnb_g), c_w_s=c_w_s[0], c_b_s=c_b_s[0])

    loss_blk, grad_x, g = _local_step(x[0], loss_target[0], p, plan)
    loss = lax.psum(loss_blk[0, 0], ("x", "y", "c"))

    accs = g["accs"]
    dmod = jnp.stack([jnp.stack([accs[f"{l}{j}"][0:3] for j in range(3)]) for l in range(2)])
    dnpre = jnp.stack([jnp.stack([accs[f"{l}{j}"][3] for j in range(3)]) for l in range(2)])
    dnpost = jnp.stack([jnp.stack([accs[f"{l}{j}"][4] for j in range(3)]) for l in range(2)])
    sab = g["accs_ab"]
    half = W_A // 8
    dgate = g["gate"][0]
    dgw = jnp.stack([jnp.concatenate([dgate[half * hh:half * (hh + 1), half * hh:half * (hh + 1)],
                                      dgate[half * hh:half * (hh + 1), W_A + half * hh:W_A + half * (hh + 1)]], axis=1)
                     for hh in range(8)])
    dgb = jnp.concatenate([sab[5].reshape(8, half), sab[6].reshape(8, half)], axis=1)
    c_acc, c_dbin, c_dws, c_dbst = g["c_small"]
    red_shapes = [(2, 9216), (2, 3, D), (2, 3, D), (CONV_A, W_A), (W_A,), (8, half, 2 * half), (8, 2 * half), (W_A,),
                  (CONV_B, W_B), (W_B,), (W_B,), (W_B,), (2 * D,), (D,), (D,), (H_C, CHUNK, CHUNK), (H_C, CHUNK)]
    red = _pack([dmod.reshape(2, 9216), dnpre, dnpost, sab[0:4], sab[4], dgw, dgb, sab[7], g["dw31"][0:CONV_B], sab[8],
                 sab[9], sab[10], c_dbin[0], c_acc[5], c_acc[6], c_dws, jnp.transpose(c_dbst)], N_DEV * 1024)
    (red_r,) = _exchange([red.reshape(N_DEV, -1, 128)], "scatter", "scatter_small_grads")
    red_all, dmod_all = _exchange([_sum_slots(red_r, "sum_small_grads"), dmod.reshape(-1, 128)], "gather", "gather_small_grads")
    red_sum = red_all.reshape(-1)
    (g_ada_b, g_npre, g_npost, g_acw, g_acb, g_agw, g_agb, g_alam, g_bcw, g_bcb, g_bng, g_bnb, g_cbin, g_cng, g_cnb,
     g_cws, g_cbs) = _unpack(red_sum, red_shapes)
    dmod_all = dmod_all.reshape(N_DEV, 2, 9216)
    ncol = ada_w.shape[-1]
    dmod_mine = jnp.moveaxis(lax.dynamic_slice_in_dim(dmod_all, me * ncol, ncol, axis=2), 0, 1)
    g_ada_w = _ada_w_grad(jnp.transpose(c_all), dmod_mine, "ada_w_grad")

    def mine(a, width):
        return lax.dynamic_slice_in_dim(a, me * width, width, axis=a.ndim - 1)

    small_grads = dict(
        ada_b=g_ada_b, norm_pre=mine(g_npre, 128), norm_post=mine(g_npost, 128), a_conv_w=mine(g_acw, 64)[None],
        a_conv_b=g_acb[None], a_gate_w=g_agw[None], a_gate_b=g_agb[None], a_lam=g_alam[None], b_conv_w=mine(g_bcw, 64)[None],
        b_conv_b=g_bcb[None], b_norm_g=g_bng[None], b_norm_b=g_bnb[None], c_b_in=mine(g_cbin, 256)[None],
        c_norm_g=mine(g_cng, 128)[None], c_norm_b=mine(g_cnb, 128)[None], c_w_s=g_cws[None], c_b_s=g_cbs[None])

    recv = dict(g["recv"])
    left = sorted(g["pending"])
    recv.update(zip(left, _exchange([g["pending"][k] for k in left], "scatter", "scatter_last")))
    big_partials = dict(ffn_w13=[recv[f"f{f}.0"] for f in range(4)], ffn_w2=[recv[f"f{f}.1"] for f in range(4)],
                        ab_w_in=[recv["ab.0"]], ab_w_out=[recv["ab.1"]], c_w_in=[recv["c.0"]], c_w_out=[recv["c.1"]],
                        ada_w=g_ada_w)

    grads, deltas, new_m, new_v = {}, {}, {}, {}

    def as3d(a):
        return a.reshape((-1,) + a.shape[-2:])

    for nm, gp in big_partials.items():
        shp = weights[nm].shape
        go, dl, mo, vo = _adamw_big(as3d(weights[nm]), gp, as3d(moms[nm]), as3d(vars_[nm]), "adamw_" + nm)
        grads[nm], deltas[nm], new_m[nm], new_v[nm] = (a.reshape(shp) for a in (go, dl, mo, vo))
    snames = list(small_grads)
    dls, mos, vos = _adamw_small([_as2d(weights[nm]) for nm in snames], [_as2d(small_grads[nm]) for nm in snames],
                                 [_as2d(moms[nm]) for nm in snames], [_as2d(vars_[nm]) for nm in snames], "adamw_small")
    for k, nm in enumerate(snames):
        shp = weights[nm].shape
        grads[nm] = small_grads[nm].reshape(shp)
        deltas[nm], new_m[nm], new_v[nm] = dls[k].reshape(shp), mos[k].reshape(shp), vos[k].reshape(shp)

    return (loss, grad_x[None], *[grads[nm] for nm in names], *[deltas[nm] for nm in names],
            *[new_m[nm] for nm in names], *[new_v[nm] for nm in names])
```

```python
import functools
import math

import jax
import jax.numpy as jnp
from jax import lax
from jax.experimental import pallas as pl
from jax.experimental.pallas import tpu as pltpu

F32 = jnp.float32
BF = jnp.bfloat16

N_DEV = 8
D = 1024
EPS = 1e-6
D_FF = 2816
FF_SHARD = 704
FF_PAD = 768
FF_MAIN = 640
FF_TAIL = FF_SHARD - FF_MAIN
W2_SHARD = 352
W_A = 512
W_B = 512
CONV_A = 4
CONV_B = 31
HALO_A = 8
HALO_B = 32
LRU_C = 8.0
CHUNK = 128
H_C = 8
ADAM_LR = 0.001
ADAM_B1 = 0.9
ADAM_B2 = 0.999
ADAM_EPS = 1e-08
ADAM_WD = 0.01
ADAM_STEP = 10
VMEM_LIMIT = 62 * 1024 * 1024
GELU_C = math.sqrt(2.0 / math.pi)

TM_FFN = 512
TM_FFN_BWD = 256
TM_MIX = 256
TK_WGRAD = 2048
TK_WGRAD_WIDE = 1024


def _params(limit=VMEM_LIMIT):
    return pltpu.CompilerParams(dimension_semantics=("arbitrary",), vmem_limit_bytes=limit)


def _dot(a, b):
    return jnp.dot(a, b, preferred_element_type=F32)


def _dot_nt(a, b):
    return lax.dot_general(a, b, (((1,), (1,)), ((), ())), preferred_element_type=F32)


def _dot_tn(a, b):
    return lax.dot_general(a, b, (((0,), (0,)), ((), ())), preferred_element_type=F32)


def _sigmoid(x):
    return 0.5 + 0.5 * jnp.tanh(0.5 * x)


def _gelu(x):
    t = jnp.tanh(GELU_C * (x + 0.044715 * x * x * x))
    return 0.5 * x * (1.0 + t), t


def _gelu_grad(x, t):
    return 0.5 * (1.0 + t) + 0.5 * x * (1.0 - t * t) * GELU_C * (1.0 + 3.0 * 0.044715 * x * x)


def _rms(x):
    r = lax.rsqrt(jnp.mean(x * x, axis=-1, keepdims=True) + EPS)
    return x * r, r


def _colsum(x):
    return jnp.sum(x, axis=0, keepdims=True)


def _pre_fwd(x, vec_ref):
    xn, r = _rms(x)
    n = xn * vec_ref[3:4, :]
    h = n * (1.0 + vec_ref[1:2, :]) + vec_ref[0:1, :]
    return h, xn, r, n


def _post_fwd(x, f, vec_ref, res_w):
    fn, _ = _rms(f)
    return x + (res_w * (1.0 + vec_ref[2:3, :])) * (fn * vec_ref[4:5, :])


def _post_bwd(dout, f, vec_ref, acc_ref, res_w):
    fn, r2 = _rms(f)
    post_g = vec_ref[4:5, :]
    dy = dout * (res_w * (1.0 + vec_ref[2:3, :]))
    acc_ref[2:3, :] += _colsum(dout * (res_w * (fn * post_g)))
    acc_ref[4:5, :] += _colsum(dy * fn)
    dfn = dy * post_g
    return r2 * (dfn - fn * jnp.mean(dfn * fn, axis=-1, keepdims=True))


def _pre_bwd(dout, dh, xn, r, n, vec_ref, acc_ref):
    acc_ref[0:1, :] += _colsum(dh)
    acc_ref[1:2, :] += _colsum(dh * n)
    dn = dh * (1.0 + vec_ref[1:2, :])
    acc_ref[3:4, :] += _colsum(dn * xn)
    dxn = dn * vec_ref[3:4, :]
    return dout + r * (dxn - xn * jnp.mean(dxn * xn, axis=-1, keepdims=True))


def _tile(tm, ncol):
    return pl.BlockSpec((tm, ncol), lambda i: (i, 0))


def _full(shape):
    return pl.BlockSpec(shape, lambda i: (0,) * len(shape))


def _any():
    return pl.BlockSpec(memory_space=pl.ANY)


def _load_ffn_weights(w13_hbm, w2_hbm, w13_v, w2_v, tails, sems):
    copies = []
    for j in range(N_DEV):
        half, k = divmod(j, 4)
        copies.append((w13_hbm.at[j, :, pl.ds(0, FF_MAIN)], w13_v.at[half, :, pl.ds(FF_MAIN * k, FF_MAIN)]))
        copies.append((w13_hbm.at[j, :, pl.ds(FF_MAIN, 128)], tails.at[j]))
    for k in range(4):
        copies.append((w2_hbm.at[2 * k], w2_v.at[pl.ds(FF_MAIN * k, W2_SHARD), :]))
        copies.append((w2_hbm.at[2 * k + 1, pl.ds(0, FF_MAIN - W2_SHARD), :],
                       w2_v.at[pl.ds(FF_MAIN * k + W2_SHARD, FF_MAIN - W2_SHARD), :]))
        copies.append((w2_hbm.at[2 * k + 1, pl.ds(FF_MAIN - W2_SHARD, FF_TAIL), :],
                       w2_v.at[pl.ds(4 * FF_MAIN + FF_TAIL * k, FF_TAIL), :]))
    copies = [pltpu.make_async_copy(src, dst, sems.at[n]) for n, (src, dst) in enumerate(copies)]
    for cp in copies:
        cp.start()
    for cp in copies:
        cp.wait()
    for pair in range(4):
        half, kk = divmod(pair, 2)
        w13_v[half, :, 4 * FF_MAIN + 128 * kk:4 * FF_MAIN + 128 * (kk + 1)] = tails[2 * pair] + tails[2 * pair + 1]


N_FFN_COPIES = 2 * N_DEV + 12
_FFN_SCRATCH = [pltpu.VMEM((2, D, D_FF), BF), pltpu.VMEM((D_FF, D), BF), pltpu.VMEM((N_DEV, D, 128), BF),
                pltpu.SemaphoreType.DMA((N_FFN_COPIES,))]
HID_CHUNKS = ((0, 768), (768, 768), (1536, 768), (2304, 512))


def _hosted(body, n_in, n_out, n_scratch, n_comm, mode, grid):
    if not n_comm:
        return body

    def at(corner):
        hit = pl.program_id(0) == corner[0]
        for d in range(1, len(grid)):
            hit = hit & (pl.program_id(d) == corner[d])
        return hit

    def hosted(*refs):
        ins, cin = refs[:n_in], refs[n_in:n_in + n_comm]
        outs, cout = refs[n_in + n_comm:n_in + n_comm + n_out], refs[n_in + n_comm + n_out:n_in + 2 * n_comm + n_out]
        scratch = refs[n_in + 2 * n_comm + n_out:]
        own, sems = scratch[:n_scratch], scratch[n_scratch:]

        @pl.when(at([0] * len(grid)))
        def _():
            _exchange_ops(cin, cout, sems, mode, "start")

        body(*ins, *outs, *own)

        @pl.when(at([n - 1 for n in grid]))
        def _():
            _exchange_ops(cin, cout, sems, mode, "wait")

    return hosted


def _ffn_fwd(x, vec, w13g, w2g, res_w, name, comm=(), tgt=None):
    t_len = x.shape[0]
    tm = min(TM_FFN, t_len)
    nc = len(comm)
    head = tgt is not None

    def body(*refs):
        if head:
            x_ref, vec_ref, w13_hbm, w2_hbm, t_ref, xo_ref, f_ref, jac_ref, s_ref, loss_ref, w13_v, w2_v, tails, sems = refs
        else:
            x_ref, vec_ref, w13_hbm, w2_hbm, xo_ref, f_ref, jac_ref, s_ref, w13_v, w2_v, tails, sems = refs

        @pl.when(pl.program_id(0) == 0)
        def _():
            _load_ffn_weights(w13_hbm, w2_hbm, w13_v, w2_v, tails, sems)
            if head:
                loss_ref[...] = jnp.zeros((8, 128), F32)

        x_t = x_ref[...]
        h, _, _, _ = _pre_fwd(x_t, vec_ref)
        hb = h.astype(BF)
        acc = jnp.zeros((tm, D), F32)
        for c0, cw in HID_CHUNKS:
            g = _dot(hb, w13_v[0, :, c0:c0 + cw])
            u = _dot(hb, w13_v[1, :, c0:c0 + cw])
            sig = _sigmoid(g)
            sl = g * sig
            jac_ref[0, :, c0:c0 + cw] = (u * (sig + sl * (1.0 - sig))).astype(BF)
            jac_ref[1, :, c0:c0 + cw] = sl.astype(BF)
            sb = (sl * u).astype(BF)
            s_ref[:, c0:c0 + cw] = sb
            acc = acc + _dot(sb, w2_v[c0:c0 + cw, :])
        f_ref[...] = acc
        xo = _post_fwd(x_t, acc, vec_ref, res_w)
        if head:
            err = xo - t_ref[...]
            xo_ref[...] = err * (1.0 / D)
            loss_ref[...] += jnp.sum(err * err) * (0.5 / D)
        else:
            xo_ref[...] = xo

    nt = t_len // tm
    n_in, n_out = (5, 5) if head else (4, 4)
    outs = pl.pallas_call(
        _hosted(body, n_in, n_out, 4, nc, "gather", (nt,)), grid=(nt,),
        in_specs=[_tile(tm, D), _full((8, D)), _any(), _any()] + ([_tile(tm, D)] if head else []) + [_any()] * nc,
        out_specs=[_tile(tm, D), _tile(tm, D), pl.BlockSpec((2, tm, D_FF), lambda i: (0, i, 0)), _tile(tm, D_FF)]
        + ([_full((8, 128))] if head else []) + [_any()] * nc,
        out_shape=[jax.ShapeDtypeStruct((t_len, D), F32), jax.ShapeDtypeStruct((t_len, D), F32),
                   jax.ShapeDtypeStruct((2, t_len, D_FF), BF), jax.ShapeDtypeStruct((t_len, D_FF), BF)]
        + ([jax.ShapeDtypeStruct((8, 128), F32)] if head else []) + _exchange_shapes(comm, "gather"),
        scratch_shapes=_FFN_SCRATCH + (_exchange_scratch(nc) if nc else []), compiler_params=_params(), name=name,
    )(x, vec, w13g, w2g, *([tgt] if head else []), *comm)
    return outs[:n_out], outs[n_out:]


def _ffn_bwd(dout, x, fpre, jac, vec, w13g, w2g, res_w, name, comm=()):
    t_len = x.shape[0]
    tm = min(TM_FFN_BWD, t_len)
    nc = len(comm)

    def body(dout_ref, x_ref, f_ref, jac_ref, vec_ref, w13_hbm, w2_hbm,
             dx_ref, dgu_ref, hb_ref, dfb_ref, acc_ref, w13_v, w2_v, tails, sems):
        @pl.when(pl.program_id(0) == 0)
        def _():
            _load_ffn_weights(w13_hbm, w2_hbm, w13_v, w2_v, tails, sems)
            acc_ref[...] = jnp.zeros((8, D), F32)

        dout_t = dout_ref[...]
        df = _post_bwd(dout_t, f_ref[...], vec_ref, acc_ref, res_w)
        dfb = df.astype(BF)
        dfb_ref[...] = dfb
        h, xn, r, n = _pre_fwd(x_ref[...], vec_ref)
        hb_ref[...] = h.astype(BF)
        dh = jnp.zeros((tm, D), F32)
        for c0, cw in HID_CHUNKS:
            ds = _dot_nt(dfb, w2_v[c0:c0 + cw, :])
            dg = (ds * jac_ref[0, :, c0:c0 + cw].astype(F32)).astype(BF)
            du = (ds * jac_ref[1, :, c0:c0 + cw].astype(F32)).astype(BF)
            dgu_ref[0, :, c0:c0 + cw] = dg
            dgu_ref[1, :, c0:c0 + cw] = du
            dh = dh + _dot_nt(dg, w13_v[0, :, c0:c0 + cw]) + _dot_nt(du, w13_v[1, :, c0:c0 + cw])
        dx_ref[...] = _pre_bwd(dout_t, dh, xn, r, n, vec_ref, acc_ref)

    gu_spec = pl.BlockSpec((2, tm, D_FF), lambda i: (0, i, 0))
    nt = t_len // tm
    outs = pl.pallas_call(
        _hosted(body, 7, 5, 4, nc, "scatter", (nt,)), grid=(nt,),
        in_specs=[_tile(tm, D), _tile(tm, D), _tile(tm, D), gu_spec, _full((8, D)), _any(), _any()] + [_any()] * nc,
        out_specs=[_tile(tm, D), gu_spec, _tile(tm, D), _tile(tm, D), _full((8, D))] + [_any()] * nc,
        out_shape=[jax.ShapeDtypeStruct((t_len, D), F32), jax.ShapeDtypeStruct((2, t_len, D_FF), BF),
                   jax.ShapeDtypeStruct((t_len, D), BF),
                   jax.ShapeDtypeStruct((t_len, D), BF), jax.ShapeDtypeStruct((8, D), F32)] + _exchange_shapes(comm, "scatter"),
        scratch_shapes=_FFN_SCRATCH + (_exchange_scratch(nc) if nc else []), compiler_params=_params(), name=name,
    )(dout, x, fpre, jac, vec, w13g, w2g, *comm)
    return outs[:5], outs[5:]


def _w13_slots(acc, o_ref):
    for k in range(4):
        o_ref[k, :, 0:FF_MAIN] = acc[:, FF_MAIN * k:FF_MAIN * (k + 1)].astype(BF)
        pair_tile = acc[:, 4 * FF_MAIN + 128 * (k // 2):4 * FF_MAIN + 128 * (k // 2 + 1)]
        o_ref[k, :, FF_MAIN:FF_PAD] = (pair_tile if k % 2 == 0 else pltpu.roll(pair_tile, FF_TAIL, 1)).astype(BF)


def _w2_slots(acc, o_ref):
    rest = FF_MAIN - W2_SHARD
    for k in range(4):
        o_ref[2 * k] = acc[FF_MAIN * k:FF_MAIN * k + W2_SHARD, :].astype(BF)
        o_ref[2 * k + 1, 0:rest, :] = acc[FF_MAIN * k + W2_SHARD:FF_MAIN * (k + 1), :].astype(BF)
        o_ref[2 * k + 1, rest:W2_SHARD, :] = acc[4 * FF_MAIN + FF_TAIL * k:4 * FF_MAIN + FF_TAIL * (k + 1), :].astype(BF)


def _wgrad(a, b, j_count, m, n, a_mode, b_mode, out_rows, out_dtype, name, tk=TK_WGRAD, col_slots=1, comm=(),
           slots=None):
    t_len = a.shape[-2]
    tk = min(tk, t_len)
    nk = t_len // tk
    wn = n // col_slots
    nc = len(comm)

    def spec(mode, width):
        if mode == "stack":
            return pl.BlockSpec((None, tk, width), lambda j, t: (j, t, 0))
        return pl.BlockSpec((tk, width), lambda j, t: (t, 0))

    def body(a_ref, b_ref, o_ref, acc):
        t = pl.program_id(1)

        @pl.when(t == 0)
        def _():
            acc[...] = jnp.zeros((m, n), F32)

        acc[...] += _dot_tn(a_ref[...], b_ref[...])

        @pl.when(t == nk - 1)
        def _():
            if slots is not None:
                slots[0](acc, o_ref)
            elif col_slots == 1:
                o_ref[...] = acc[0:out_rows, :].astype(out_dtype)
            else:
                for s in range(col_slots):
                    o_ref[s] = acc[0:out_rows, wn * s:wn * (s + 1)].astype(out_dtype)

    if slots is not None:
        blk = slots[1]
        out_spec = pl.BlockSpec(blk, lambda j, t: (j,) + (0,) * (len(blk) - 1))
        out_shape = jax.ShapeDtypeStruct((j_count * blk[0],) + blk[1:], BF)
    elif col_slots == 1:
        out_spec = pl.BlockSpec((None, out_rows, n), lambda j, t: (j, 0, 0))
        out_shape = jax.ShapeDtypeStruct((j_count, out_rows, n), out_dtype)
    else:
        out_spec = pl.BlockSpec((col_slots, out_rows, wn), lambda j, t: (0, 0, 0))
        out_shape = jax.ShapeDtypeStruct((col_slots, out_rows, wn), out_dtype)
    outs = pl.pallas_call(
        _hosted(body, 2, 1, 1, nc, "scatter", (j_count, nk)), grid=(j_count, nk),
        in_specs=[spec(a_mode, m), spec(b_mode, n)] + [_any()] * nc,
        out_specs=[out_spec] + [_any()] * nc, out_shape=[out_shape] + _exchange_shapes(comm, "scatter"),
        scratch_shapes=[pltpu.VMEM((m, n), F32)] + (_exchange_scratch(nc) if nc else []),
        compiler_params=pltpu.CompilerParams(dimension_semantics=("arbitrary", "arbitrary"), vmem_limit_bytes=VMEM_LIMIT),
        name=name,
    )(a, b, *comm)
    return (outs[0], list(outs[1:])) if nc else outs[0]


def _c_mask_weights(ws_ref, wsm, wsmt):
    row = lax.broadcasted_iota(jnp.int32, (CHUNK, CHUNK), 0)
    col = lax.broadcasted_iota(jnp.int32, (CHUNK, CHUNK), 1)
    for hh in range(H_C):
        w = jnp.where(row >= col, ws_ref[hh], 0.0)
        wsm[hh] = w.astype(BF)
        if wsmt is not None:
            wsmt[hh] = w.T.astype(BF)


def _c_inner(pre, cvec_ref, wsm, bst_ref, mix_sc, tm):
    z, t = _gelu(pre)
    u = z[:, 0:D]
    v = z[:, D:2 * D]
    mu = jnp.mean(v, axis=-1, keepdims=True)
    vc = v - mu
    rstd = lax.rsqrt(jnp.mean(vc * vc, axis=-1, keepdims=True) + EPS)
    vhat = vc * rstd
    vnb = (vhat * cvec_ref[0:1, :] + cvec_ref[1:2, :]).astype(BF)
    for nn in range(tm // CHUNK):
        for hh in range(H_C):
            rows = slice(CHUNK * nn, CHUNK * (nn + 1))
            cols = slice(CHUNK * hh, CHUNK * (hh + 1))
            mix_sc[rows, cols] = _dot(wsm[hh], vnb[rows, cols]) + bst_ref[:, hh:hh + 1]
    return u, t, rstd, vhat, vnb


def _mixc_fwd(x, vec, w_in, b_in, cvec, ws, bst, w_out, name):
    t_len = x.shape[0]
    tm = min(TM_MIX, t_len)

    def body(x_ref, vec_ref, win_ref, bin_ref, cvec_ref, ws_ref, bst_ref, wout_ref,
             xo_ref, f_ref, pre_ref, wsm, mix_sc):
        @pl.when(pl.program_id(0) == 0)
        def _():
            _c_mask_weights(ws_ref, wsm, None)

        x_t = x_ref[...]
        h, _, _, _ = _pre_fwd(x_t, vec_ref)
        hb = h.astype(BF)
        for j in range(N_DEV):
            cols = slice(256 * j, 256 * (j + 1))
            pre_ref[:, cols] = _dot(hb, win_ref[j]) + bin_ref[:, cols]
        u, _, _, _, _ = _c_inner(pre_ref[...], cvec_ref, wsm, bst_ref, mix_sc, tm)
        fpre = _dot((u * mix_sc[...]).astype(BF), wout_ref[...])
        f_ref[...] = fpre
        xo_ref[...] = _post_fwd(x_t, fpre, vec_ref, 1.0)

    return pl.pallas_call(
        body, grid=(t_len // tm,),
        in_specs=[_tile(tm, D), _full((8, D)), _full((N_DEV, D, 256)), _full((1, 2 * D)), _full((8, D)),
                  _full((H_C, CHUNK, CHUNK)), _full((CHUNK, H_C)), _full((D, D))],
        out_specs=[_tile(tm, D), _tile(tm, D), _tile(tm, 2 * D)],
        out_shape=[jax.ShapeDtypeStruct((t_len, D), F32), jax.ShapeDtypeStruct((t_len, D), F32),
                   jax.ShapeDtypeStruct((t_len, 2 * D), F32)],
        scratch_shapes=[pltpu.VMEM((H_C, CHUNK, CHUNK), BF), pltpu.VMEM((tm, D), F32)],
        compiler_params=_params(), name=name,
    )(x, vec, w_in, b_in, cvec, ws, bst, w_out)


def _mixc_bwd(dout, x, fpre, pre, vec, w_in, cvec, ws, bst, w_out, name):
    t_len = x.shape[0]
    tm = min(TM_MIX, t_len)
    nt = t_len // tm

    def body(dout_ref, x_ref, f_ref, pre_ref, vec_ref, win_ref, cvec_ref, ws_ref, bst_ref, wout_ref,
             dx_ref, dpre_ref, p_ref, hb_ref, dfb_ref, acc_ref, dbin_ref, dws_ref, dbst_ref,
             wsm, wsmt, mix_sc, dvn_sc, dmsum):
        i = pl.program_id(0)

        @pl.when(i == 0)
        def _():
            _c_mask_weights(ws_ref, wsm, wsmt)
            acc_ref[...] = jnp.zeros((8, D), F32)
            dbin_ref[...] = jnp.zeros((8, 2 * D), F32)
            dws_ref[...] = jnp.zeros((H_C, CHUNK, CHUNK), F32)
            dmsum[...] = jnp.zeros((CHUNK, D), F32)

        dout_t = dout_ref[...]
        df = _post_bwd(dout_t, f_ref[...], vec_ref, acc_ref, 1.0)
        dfb = df.astype(BF)
        dfb_ref[...] = dfb
        h, xn, r, n = _pre_fwd(x_ref[...], vec_ref)
        hb_ref[...] = h.astype(BF)
        pre_t = pre_ref[...]
        u, t, rstd, vhat, vnb = _c_inner(pre_t, cvec_ref, wsm, bst_ref, mix_sc, tm)
        mix = mix_sc[...]
        p_ref[...] = (u * mix).astype(BF)
        dp = _dot_nt(dfb, wout_ref[...])
        du = dp * mix
        dmix = dp * u
        dmb = dmix.astype(BF)
        for nn in range(tm // CHUNK):
            rows = slice(CHUNK * nn, CHUNK * (nn + 1))
            dmsum[...] += dmix[rows, :]
            for hh in range(H_C):
                cols = slice(CHUNK * hh, CHUNK * (hh + 1))
                dvn_sc[rows, cols] = _dot(wsmt[hh], dmb[rows, cols])
                dws_ref[hh] += _dot_nt(dmb[rows, cols], vnb[rows, cols])
        dvn = dvn_sc[...]
        acc_ref[5:6, :] += _colsum(dvn * vhat)
        acc_ref[6:7, :] += _colsum(dvn)
        dvhat = dvn * cvec_ref[0:1, :]
        dv = rstd * (dvhat - jnp.mean(dvhat, axis=-1, keepdims=True)
                     - vhat * jnp.mean(dvhat * vhat, axis=-1, keepdims=True))
        gg = _gelu_grad(pre_t, t)
        dpre_u = du * gg[:, 0:D]
        dpre_v = dv * gg[:, D:2 * D]
        dbin_ref[0:1, 0:D] += _colsum(dpre_u)
        dbin_ref[0:1, D:2 * D] += _colsum(dpre_v)
        dpre_ref[:, 0:D] = dpre_u.astype(BF)
        dpre_ref[:, D:2 * D] = dpre_v.astype(BF)
        dh = jnp.zeros((tm, D), F32)
        for j in range(N_DEV):
            dh = dh + _dot_nt(dpre_ref[:, 256 * j:256 * (j + 1)], win_ref[j])
        dx_ref[...] = _pre_bwd(dout_t, dh, xn, r, n, vec_ref, acc_ref)

        @pl.when(i == nt - 1)
        def _():
            row = lax.broadcasted_iota(jnp.int32, (CHUNK, CHUNK), 0)
            col = lax.broadcasted_iota(jnp.int32, (CHUNK, CHUNK), 1)
            for hh in range(H_C):
                dws_ref[hh] = jnp.where(row >= col, dws_ref[hh], 0.0)
                dbst_ref[:, hh:hh + 1] = jnp.sum(dmsum[:, CHUNK * hh:CHUNK * (hh + 1)], axis=1, keepdims=True)

    return pl.pallas_call(
        body, grid=(nt,),
        in_specs=[_tile(tm, D), _tile(tm, D), _tile(tm, D), _tile(tm, 2 * D), _full((8, D)), _full((N_DEV, D, 256)),
                  _full((8, D)), _full((H_C, CHUNK, CHUNK)), _full((CHUNK, H_C)), _full((D, D))],
        out_specs=[_tile(tm, D), _tile(tm, 2 * D), _tile(tm, D), _tile(tm, D), _tile(tm, D), _full((8, D)),
                   _full((8, 2 * D)), _full((H_C, CHUNK, CHUNK)), _full((CHUNK, H_C))],
        out_shape=[jax.ShapeDtypeStruct((t_len, D), F32), jax.ShapeDtypeStruct((t_len, 2 * D), BF),
                   jax.ShapeDtypeStruct((t_len, D), BF), jax.ShapeDtypeStruct((t_len, D), BF),
                   jax.ShapeDtypeStruct((t_len, D), BF), jax.ShapeDtypeStruct((8, D), F32),
                   jax.ShapeDtypeStruct((8, 2 * D), F32), jax.ShapeDtypeStruct((H_C, CHUNK, CHUNK), F32),
                   jax.ShapeDtypeStruct((CHUNK, H_C), F32)],
        scratch_shapes=[pltpu.VMEM((H_C, CHUNK, CHUNK), BF), pltpu.VMEM((H_C, CHUNK, CHUNK), BF),
                        pltpu.VMEM((tm, D), F32), pltpu.VMEM((tm, D), F32), pltpu.VMEM((CHUNK, D), F32)],
        compiler_params=_params(), name=name,
    )(dout, x, fpre, pre, vec, w_in, cvec, ws, bst, w_out)


def _gmean(x, g_ref):
    hi = x.astype(BF)
    lo = (x - hi.astype(F32)).astype(BF)
    return _dot(hi, g_ref[...]) + _dot(lo, g_ref[...])


def _log_sigmoid(lam):
    e = jnp.exp(-jnp.abs(lam))
    log1p = jnp.where(e < 1e-2, e * (1.0 - e * (0.5 - e * (1.0 / 3.0 - 0.25 * e))), jnp.log(1.0 + e))
    return jnp.minimum(lam, 0.0) - log1p


def _neg_expm1(y):
    series = -(y * (1.0 + y * (0.5 + y * (1.0 / 6.0 + y * (1.0 / 24.0 + y * (1.0 / 120.0))))))
    return jnp.where(y > -0.1, series, 1.0 - jnp.exp(y))


def _conv_causal(ext, taps_ref, bias, k_taps, halo, tm):
    acc = bias + taps_ref[0:1, :] * ext[halo - k_taps + 1:halo - k_taps + 1 + tm, :]
    for k in range(1, k_taps):
        off = halo - k_taps + 1 + k
        acc = acc + taps_ref[k:k + 1, :] * ext[off:off + tm, :]
    return acc


def _build_shifted(sh_ref, e, n_rows):
    sh_ref[0] = e
    for r in range(1, 8):
        sh_ref[r] = pltpu.roll(e, n_rows - r, 0)


def _shifted_rows(sh_ref, off, tm):
    base = off - off % 8
    return sh_ref[off % 8, base:base + tm, :]


def _scan(a, u, tm, reverse):
    row = lax.broadcasted_iota(jnp.int32, (tm, W_A), 0)
    d = 1
    while d < tm:
        if reverse:
            keep = row < tm - d
            shift = tm - d
        else:
            keep = row >= d
            shift = d
        a_sh = jnp.where(keep, pltpu.roll(a, shift, 0), 1.0)
        u_sh = jnp.where(keep, pltpu.roll(u, shift, 0), 0.0)
        u = a * u_sh + u
        a = a * a_sh
        d *= 2
    return a, u


def _a_gates(xc, cv_ref, wr_ref, wi_ref):
    xcb = xc.astype(BF)
    r = _sigmoid(_dot(xcb, wr_ref[...]) + cv_ref[5:6, :])
    ig = _sigmoid(_dot(xcb, wi_ref[...]) + cv_ref[6:7, :])
    ls = _log_sigmoid(cv_ref[7:8, :])
    la = LRU_C * r * ls
    a = jnp.exp(la)
    m = jnp.sqrt(_neg_expm1(2.0 * la))
    return xcb, r, ig, ls, a, m


def _b_norm(vc, cv_ref, g_ref):
    mu = _gmean(vc, g_ref)
    dv = vc - mu
    rstd = lax.rsqrt(_gmean(dv * dv, g_ref) + EPS)
    vhat = dv * rstd
    vln = vhat * cv_ref[9:10, :] + cv_ref[10:11, :]
    return rstd, vhat, vln


def _mixab_fwd(x, vec, w_in, cv, w31, wr, wi, gmat, w_out, name):
    t_len = x.shape[0]
    tm = min(TM_MIX, t_len)

    def body(x_ref, vec_ref, win_ref, cv_ref, w31_ref, wr_ref, wi_ref, g_ref, wout_ref,
             xo_ref, f_ref, z_ref, hs_ref, cvs_ref, ext_a, ext_b, hc, shifted):
        @pl.when(pl.program_id(0) == 0)
        def _():
            ext_a[0:HALO_A, :] = jnp.zeros((HALO_A, W_A), F32)
            ext_b[0:HALO_B, :] = jnp.zeros((HALO_B, W_B), F32)
            hc[...] = jnp.zeros((8, W_A), F32)

        x_t = x_ref[...]
        h, _, _, _ = _pre_fwd(x_t, vec_ref)
        hb = h.astype(BF)
        for j in range(N_DEV):
            z_ref[:, 256 * j:256 * (j + 1)] = _dot(hb, win_ref[j])
        ext_a[HALO_A:HALO_A + tm, :] = z_ref[:, W_A:2 * W_A]
        xc = _conv_causal(ext_a, cv_ref, cv_ref[4:5, :], CONV_A, HALO_A, tm)
        ext_a[0:HALO_A, :] = ext_a[tm:tm + HALO_A, :]
        cvs_ref[:, 0:W_A] = xc
        _, _, ig, _, a, m = _a_gates(xc, cv_ref, wr_ref, wi_ref)
        a_cum, hloc = _scan(a, m * ig * xc, tm, False)
        hs = hloc + a_cum * hc[0:1, :]
        hs_ref[...] = hs
        hc[0:1, :] = hs[tm - 1:tm, :]
        gel, _ = _gelu(z_ref[:, 0:W_A])
        ya = hs * gel
        ext_b[HALO_B:HALO_B + tm, :] = z_ref[:, 2 * W_A:2 * W_A + W_B] * _sigmoid(z_ref[:, 2 * W_A + W_B:2 * W_A + 2 * W_B])
        _build_shifted(shifted, ext_b[...], tm + HALO_B)
        vc = cv_ref[8:9, :] + w31_ref[0:1, :] * _shifted_rows(shifted, HALO_B - CONV_B + 1, tm)
        for k in range(1, CONV_B):
            vc = vc + w31_ref[k:k + 1, :] * _shifted_rows(shifted, HALO_B - CONV_B + 1 + k, tm)
        ext_b[0:HALO_B, :] = ext_b[tm:tm + HALO_B, :]
        cvs_ref[:, W_A:W_A + W_B] = vc
        _, _, vln = _b_norm(vc, cv_ref, g_ref)
        yb = vln * _sigmoid(vln)
        fpre = _dot(ya.astype(BF), wout_ref[0:W_A, :]) + _dot(yb.astype(BF), wout_ref[W_A:W_A + W_B, :])
        f_ref[...] = fpre
        xo_ref[...] = _post_fwd(x_t, fpre, vec_ref, 1.0)

    return pl.pallas_call(
        body, grid=(t_len // tm,),
        in_specs=[_tile(tm, D), _full((8, D)), _full((N_DEV, D, 256)), _full((16, W_A)), _full((32, W_B)),
                  _full((W_A, W_A)), _full((W_A, W_A)), _full((W_B, W_B)), _full((D, D))],
        out_specs=[_tile(tm, D), _tile(tm, D), _tile(tm, 2 * D), _tile(tm, W_A), _tile(tm, W_A + W_B)],
        out_shape=[jax.ShapeDtypeStruct((t_len, D), F32), jax.ShapeDtypeStruct((t_len, D), F32),
                   jax.ShapeDtypeStruct((t_len, 2 * D), F32), jax.ShapeDtypeStruct((t_len, W_A), F32),
                   jax.ShapeDtypeStruct((t_len, W_A + W_B), F32)],
        scratch_shapes=[pltpu.VMEM((tm + HALO_A, W_A), F32), pltpu.VMEM((tm + HALO_B, W_B), F32), pltpu.VMEM((8, W_A), F32),
                        pltpu.VMEM((8, tm + HALO_B, W_B), F32)],
        compiler_params=_params(), name=name,
    )(x, vec, w_in, cv, w31, wr, wi, gmat, w_out)


def _mixab_bwd(dout, x, fpre, z, cvs, hs, vec, w_in, cv, w31, wr, wi, gmat, w_out, name, comm=()):
    t_len = x.shape[0]
    tm = min(TM_MIX, t_len)
    nt = t_len // tm

    def rev(i):
        return nt - 1 - i

    def rtile(ncol):
        return pl.BlockSpec((tm, ncol), lambda i: (rev(i), 0))

    def body(dout_ref, x_ref, f_ref, z_ref, cvs_ref, hs_ref, hsp_ref, vec_ref, win_ref, cv_ref, w31_ref, wr_ref, wi_ref,
             g_ref, wout_ref,
             dx_ref, dz_ref, yab_ref, hb_ref, dfb_ref, xcb_ref, dri_ref, acc_ref, accs_ref, dw31_ref,
             ext_h, ext_dx, ext_dv, carry, shifted):
        i = pl.program_id(0)
        has_prev = (rev(i) > 0).astype(F32)

        @pl.when(i == 0)
        def _():
            acc_ref[...] = jnp.zeros((8, D), F32)
            accs_ref[...] = jnp.zeros((16, W_A), F32)
            dw31_ref[...] = jnp.zeros((32, W_B), F32)
            ext_dx[tm:tm + HALO_A, :] = jnp.zeros((HALO_A, W_A), F32)
            ext_dv[tm:tm + HALO_B, :] = jnp.zeros((HALO_B, W_B), F32)
            carry[...] = jnp.zeros((8, W_A), F32)

        dout_t = dout_ref[...]
        df = _post_bwd(dout_t, f_ref[...], vec_ref, acc_ref, 1.0)
        dfb = df.astype(BF)
        dfb_ref[...] = dfb
        h, xn, r_x, n = _pre_fwd(x_ref[...], vec_ref)
        hb_ref[...] = h.astype(BF)

        ag = z_ref[:, 0:W_A]
        ax = z_ref[:, W_A:2 * W_A]
        bv = z_ref[:, 2 * W_A:2 * W_A + W_B]
        sg = _sigmoid(z_ref[:, 2 * W_A + W_B:2 * W_A + 2 * W_B])
        vv = bv * sg
        xc = cvs_ref[:, 0:W_A]
        xcb, r, ig, ls, a, m = _a_gates(xc, cv_ref, wr_ref, wi_ref)
        xcb_ref[...] = xcb
        hs_t = hs_ref[...]
        ext_h[0:8, :] = hsp_ref[...] * has_prev
        ext_h[8:8 + tm, :] = hs_t
        hprev = ext_h[7:7 + tm, :]
        gel, tg = _gelu(ag)
        rstd, vhat, vln = _b_norm(cvs_ref[:, W_A:W_A + W_B], cv_ref, g_ref)
        sv = _sigmoid(vln)
        yab_ref[:, 0:W_A] = (hs_t * gel).astype(BF)
        yab_ref[:, W_A:W_A + W_B] = (vln * sv).astype(BF)

        dya = _dot_nt(dfb, wout_ref[0:W_A, :])
        dyb = _dot_nt(dfb, wout_ref[W_A:W_A + W_B, :])

        dag = dya * hs_t * _gelu_grad(ag, tg)
        row = lax.broadcasted_iota(jnp.int32, (tm, W_A), 0)
        last = row == tm - 1
        a_next = jnp.where(last, 1.0, pltpu.roll(a, tm - 1, 0))
        u0 = dya * gel + jnp.where(last, carry[0:1, :], 0.0)
        _, dhs = _scan(a_next, u0, tm, True)
        carry[0:1, :] = a[0:1, :] * dhs[0:1, :]
        da = dhs * hprev
        dm = dhs * ig * xc
        di = dhs * m * xc
        dxc = dhs * m * ig
        dla = da * a - dm * (a * a) / m
        accs_ref[7:8, :] += _colsum(dla * r) * (LRU_C * _sigmoid(-cv_ref[7:8, :]))
        drp = (dla * (LRU_C * ls)) * r * (1.0 - r)
        dip = di * ig * (1.0 - ig)
        accs_ref[5:6, :] += _colsum(drp)
        accs_ref[6:7, :] += _colsum(dip)
        drpb = drp.astype(BF)
        dipb = dip.astype(BF)
        dri_ref[:, 0:W_A] = drpb
        dri_ref[:, W_A:2 * W_A] = dipb
        dxc = dxc + _dot_nt(drpb, wr_ref[...]) + _dot_nt(dipb, wi_ref[...])
        accs_ref[4:5, :] += _colsum(dxc)
        ext_dx[0:tm, :] = dxc
        dax = jnp.zeros((tm, W_A), F32)
        for k in range(CONV_A):
            ahead = ext_dx[CONV_A - 1 - k:CONV_A - 1 - k + tm, :]
            accs_ref[k:k + 1, :] += _colsum(ax * ahead)
            dax = dax + cv_ref[k:k + 1, :] * ahead
        ext_dx[tm:tm + HALO_A, :] = dxc[0:HALO_A, :]

        dvln = dyb * (sv * (1.0 + vln * (1.0 - sv)))
        accs_ref[9:10, :] += _colsum(dvln * vhat)
        accs_ref[10:11, :] += _colsum(dvln)
        dvhat = dvln * cv_ref[9:10, :]
        dvc = rstd * (dvhat - _gmean(dvhat, g_ref) - vhat * _gmean(dvhat * vhat, g_ref))
        accs_ref[8:9, :] += _colsum(dvc)
        ext_dv[0:tm, :] = dvc
        _build_shifted(shifted, ext_dv[...], tm + HALO_B)
        dvv = jnp.zeros((tm, W_B), F32)
        for k in range(CONV_B):
            ahead = _shifted_rows(shifted, CONV_B - 1 - k, tm)
            dw31_ref[k:k + 1, :] += _colsum(vv * ahead)
            dvv = dvv + w31_ref[k:k + 1, :] * ahead
        ext_dv[tm:tm + HALO_B, :] = dvc[0:HALO_B, :]

        dz_ref[:, 0:W_A] = dag.astype(BF)
        dz_ref[:, W_A:2 * W_A] = dax.astype(BF)
        dz_ref[:, 2 * W_A:2 * W_A + W_B] = (dvv * sg).astype(BF)
        dz_ref[:, 2 * W_A + W_B:2 * W_A + 2 * W_B] = (dvv * vv * (1.0 - sg)).astype(BF)
        dh = jnp.zeros((tm, D), F32)
        for j in range(N_DEV):
            dh = dh + _dot_nt(dz_ref[:, 256 * j:256 * (j + 1)], win_ref[j])
        dx_ref[...] = _pre_bwd(dout_t, dh, xn, r_x, n, vec_ref, acc_ref)

    hsp_spec = pl.BlockSpec((8, W_A), lambda i: (jnp.maximum(rev(i) * (tm // 8) - 1, 0), 0))
    nc = len(comm)
    outs = pl.pallas_call(
        _hosted(body, 15, 10, 5, nc, "scatter", (nt,)), grid=(nt,),
        in_specs=[rtile(D), rtile(D), rtile(D), rtile(2 * D), rtile(W_A + W_B), rtile(W_A), hsp_spec, _full((8, D)),
                  _full((N_DEV, D, 256)), _full((16, W_A)), _full((32, W_B)), _full((W_A, W_A)), _full((W_A, W_A)),
                  _full((W_B, W_B)), _full((D, D))] + [_any()] * nc,
        out_specs=[rtile(D), rtile(2 * D), rtile(D), rtile(D), rtile(D), rtile(W_A), rtile(2 * W_A), _full((8, D)),
                   _full((16, W_A)), _full((32, W_B))] + [_any()] * nc,
        out_shape=[jax.ShapeDtypeStruct((t_len, D), F32), jax.ShapeDtypeStruct((t_len, 2 * D), BF),
                   jax.ShapeDtypeStruct((t_len, D), BF), jax.ShapeDtypeStruct((t_len, D), BF),
                   jax.ShapeDtypeStruct((t_len, D), BF), jax.ShapeDtypeStruct((t_len, W_A), BF),
                   jax.ShapeDtypeStruct((t_len, 2 * W_A), BF), jax.ShapeDtypeStruct((8, D), F32),
                   jax.ShapeDtypeStruct((16, W_A), F32), jax.ShapeDtypeStruct((32, W_B), F32)] + _exchange_shapes(comm, "scatter"),
        scratch_shapes=[pltpu.VMEM((tm + 8, W_A), F32), pltpu.VMEM((tm + HALO_A, W_A), F32),
                        pltpu.VMEM((tm + HALO_B, W_B), F32), pltpu.VMEM((8, W_A), F32),
                        pltpu.VMEM((8, tm + HALO_B, W_B), F32)] + (_exchange_scratch(nc) if nc else []),
        compiler_params=_params(), name=name,
    )(dout, x, fpre, z, cvs, hs, hs, vec, w_in, cv, w31, wr, wi, gmat, w_out, *comm)
    return outs[:10], list(outs[10:])


def _vec(p, l, j):
    return jnp.concatenate([p["mod"][l, j], p["norm_pre"][l, j][None], p["norm_post"][l, j][None], jnp.zeros((3, D), F32)], 0)


def _ab_consts(p):
    gw = p["a_gate_w"]
    gb = p["a_gate_b"]
    half = W_A // 8
    wr = jax.scipy.linalg.block_diag(*[gw[hh, :, 0:half] for hh in range(8)]).astype(BF)
    wi = jax.scipy.linalg.block_diag(*[gw[hh, :, half:2 * half] for hh in range(8)]).astype(BF)
    rows = [p["a_conv_w"], p["a_conv_b"][None], gb[:, 0:half].reshape(1, W_A), gb[:, half:2 * half].reshape(1, W_A),
            p["a_lam"][None], p["b_conv_b"][None], p["b_norm_g"][None], p["b_norm_b"][None], jnp.zeros((5, W_A), F32)]
    cv = jnp.concatenate(rows, 0)
    w31 = jnp.concatenate([p["b_conv_w"], jnp.zeros((1, W_B), F32)], 0)
    grp = jnp.arange(W_B) // (W_B // 8)
    gmat = ((grp[:, None] == grp[None, :]).astype(F32) / (W_B // 8)).astype(BF)
    return cv, w31, wr, wi, gmat


SUBLAYERS = ("f0", "ab", "f1", "f2", "c", "f3")


def _local_step(x, tgt, p, plan=None):
    g = {}
    saved = []
    cur = x
    wsets = dict(p["wsets"])
    ab_c = _ab_consts(p)
    c_cvec = jnp.concatenate([p["c_norm_g"][None], p["c_norm_b"][None], jnp.zeros((6, D), F32)], 0)
    c_bst = jnp.transpose(p["c_b_s"])
    c_bin = p["c_b_in"][None]
    for s_idx, wname in enumerate(SUBLAYERS):
        l, j = divmod(s_idx, 3)
        vec = _vec(p, l, j)
        tag = f"l{l}s{j}"
        if j != 1:
            names = plan["gather"].get(s_idx, []) if plan else []
            comm = [a for nm in names for a in plan["shards"][nm]]
            res, got = _ffn_fwd(cur, vec, *wsets[wname], 0.5, "ffn_fwd_" + tag, comm=comm, tgt=tgt if s_idx == 5 else None)
            nxt, fpre, jac, s_act = res[:4]
            for k, nm in enumerate(names):
                wsets[nm] = list(got[2 * k:2 * k + 2])
            saved.append((cur, fpre, jac, s_act, vec))
            if s_idx == 5:
                loss_blk = res[4]
        elif l == 0:
            w_in, w_out = wsets[wname]
            nxt, fpre, z, hs, cvs = _mixab_fwd(cur, vec, w_in, *ab_c, w_out.reshape(D, D), "mixab_fwd_" + tag)
            saved.append((cur, fpre, z, hs, cvs, vec))
        else:
            w_in, w_out = wsets[wname]
            nxt, fpre, pre = _mixc_fwd(cur, vec, w_in, c_bin, c_cvec, p["c_w_s"], c_bst, w_out.reshape(D, D), "mixc_fwd_" + tag)
            saved.append((cur, fpre, pre, vec))
        cur = nxt
    dcur = cur
    accs, pending, recv = {}, {}, {}

    def take(host):
        keys = plan["scatter"].get(host, []) if plan else []
        return keys, [pending.pop(k) for k in keys]

    def put(keys, got):
        recv.update(zip(keys, got))

    for s_idx in reversed(range(6)):
        wname = SUBLAYERS[s_idx]
        l, j = divmod(s_idx, 3)
        tag = f"l{l}s{j}"
        sv = saved[s_idx]
        if j != 1:
            keys, comm = take("ffn_bwd_" + tag)
            xin, fpre, jac, s, vec = sv
            (dcur, dgu, hb, dfb, acc), got = _ffn_bwd(dcur, xin, fpre, jac, vec, *wsets[wname], 0.5, "ffn_bwd_" + tag, comm=comm)
            put(keys, got)
            keys, comm = take("wgrad_w13_" + tag)
            dw13 = _wgrad(hb, dgu, 2, D, D_FF, "share", "stack", D, BF, "wgrad_w13_" + tag, tk=TK_WGRAD_WIDE, comm=comm,
                          slots=(_w13_slots, (4, D, FF_PAD)))
            if keys:
                dw13, got = dw13
                put(keys, got)
            pending[wname + ".0"] = dw13
            keys, comm = take("wgrad_w2_" + tag)
            dw2 = _wgrad(s, dfb, 1, D_FF, D, "share", "share", D_FF, BF, "wgrad_w2_" + tag, tk=TK_WGRAD_WIDE, comm=comm,
                         slots=(_w2_slots, (N_DEV, W2_SHARD, D)))
            if keys:
                dw2, got = dw2
                put(keys, got)
            pending[wname + ".1"] = dw2
        elif l == 0:
            xin, fpre, z, hs, cvs, vec = sv
            w_in, w_out = wsets[wname]
            keys, comm = take("mixab_bwd_" + tag)
            (dcur, dz, yab, hb, dfb, xcb, dri, acc, accs_ab, dw31), got = _mixab_bwd(
                dcur, xin, fpre, z, cvs, hs, vec, w_in, *ab_c, w_out.reshape(D, D), "mixab_bwd_" + tag, comm=comm)
            put(keys, got)
            d_in = _wgrad(hb, dz, 1, D, 2 * D, "share", "share", D, BF, "wgrad_ab_in", tk=TK_WGRAD_WIDE, col_slots=N_DEV)
            d_out = _wgrad(yab, dfb, 1, D, D, "share", "share", D, BF, "wgrad_ab_out")
            pending[wname + ".0"], pending[wname + ".1"] = d_in, d_out.reshape(N_DEV, D // N_DEV, D)
            g["gate"] = _wgrad(xcb, dri, 1, W_A, 2 * W_A, "share", "share", W_A, F32, "wgrad_gate")
            g["accs_ab"] = accs_ab
            g["dw31"] = dw31
        else:
            xin, fpre, pre, vec = sv
            w_in, w_out = wsets[wname]
            dcur, dpre, pb, hb, dfb, acc, dbin, dws, dbst = _mixc_bwd(
                dcur, xin, fpre, pre, vec, w_in, c_cvec, p["c_w_s"], c_bst, w_out.reshape(D, D), "mixc_bwd_" + tag)
            d_in = _wgrad(hb, dpre, 1, D, 2 * D, "share", "share", D, BF, "wgrad_c_in", tk=TK_WGRAD_WIDE, col_slots=N_DEV)
            d_out = _wgrad(pb, dfb, 1, D, D, "share", "share", D, BF, "wgrad_c_out")
            pending[wname + ".0"], pending[wname + ".1"] = d_in, d_out.reshape(N_DEV, D // N_DEV, D)
            g["c_small"] = (acc, dbin, dws, dbst)
        accs[f"{l}{j}"] = acc
    g["accs"] = accs
    g["pending"] = pending
    g["recv"] = recv
    return loss_blk, dcur, g


def _exchange_ops(ins, outs, sems, mode, action):
    send_sems, recv_sems, loc_sems = sems
    n = len(ins)
    x, y, c = lax.axis_index("x"), lax.axis_index("y"), lax.axis_index("c")
    me = 4 * x + 2 * y + c

    def src(i, dev):
        return ins[i] if mode == "gather" else ins[i].at[dev]

    for i in range(n):
        cp = pltpu.make_async_copy(src(i, me), outs[i].at[me], loc_sems.at[i])
        if action == "start":
            cp.start()
        else:
            cp.wait()
    for mask in range(1, N_DEV):
        px = 1 - x if mask & 4 else x
        py = 1 - y if mask & 2 else y
        pc = 1 - c if mask & 1 else c
        peer = 4 * px + 2 * py + pc
        for i in range(n):
            k = i * (N_DEV - 1) + mask - 1
            cp = pltpu.make_async_remote_copy(
                src_ref=src(i, peer), dst_ref=outs[i].at[me if action == "start" else peer],
                send_sem=send_sems.at[k], recv_sem=recv_sems.at[k],
                device_id=(px, py, pc), device_id_type=pl.DeviceIdType.MESH)
            if action == "start":
                cp.start()
            else:
                cp.wait()


def _exchange_scratch(n):
    return [pltpu.SemaphoreType.DMA((n * (N_DEV - 1),)), pltpu.SemaphoreType.DMA((n * (N_DEV - 1),)),
            pltpu.SemaphoreType.DMA((n,))]


def _exchange_shapes(arrays, mode):
    return [jax.ShapeDtypeStruct(((N_DEV,) + a.shape) if mode == "gather" else a.shape, a.dtype) for a in arrays]


def _exchange(arrays, mode, name):
    n = len(arrays)

    def body(*refs):
        ins, outs, sems = refs[:n], refs[n:2 * n], refs[2 * n:]
        _exchange_ops(ins, outs, sems, mode, "start")
        _exchange_ops(ins, outs, sems, mode, "wait")

    return pl.pallas_call(
        body, in_specs=[pl.BlockSpec(memory_space=pl.ANY)] * n, out_specs=[pl.BlockSpec(memory_space=pl.ANY)] * n,
        out_shape=_exchange_shapes(arrays, mode), scratch_shapes=_exchange_scratch(n), name=name,
    )(*arrays)


def _sum_slots(a, name):
    def body(a_ref, o_ref):
        acc = a_ref[0]
        for s in range(1, N_DEV):
            acc = acc + a_ref[s]
        o_ref[...] = acc

    return pl.pallas_call(body, out_shape=jax.ShapeDtypeStruct(a.shape[1:], F32), name=name,
                          compiler_params=pltpu.CompilerParams(vmem_limit_bytes=VMEM_LIMIT))(a)


def _pack(pieces, mult):
    flat = jnp.concatenate([q.reshape(-1).astype(F32) for q in pieces])
    size = -(-flat.shape[0] // mult) * mult
    return jnp.pad(flat, (0, size - flat.shape[0])).reshape(size // 128, 128)


def _unpack(flat, shapes):
    out, off = [], 0
    for shp in shapes:
        size = math.prod(shp)
        out.append(flat[..., off:off + size].reshape(flat.shape[:-1] + tuple(shp)))
        off += size
    return out


def _mod_part(c_all, ada_w, ada_b_mine, name):
    cols = ada_w.shape[-1]

    def body(c_ref, w_ref, b_ref, o_ref):
        cv = c_ref[...]
        ca = cv * _sigmoid(cv)
        for l in range(2):
            o_ref[l] = jnp.dot(ca, w_ref[l], preferred_element_type=F32, precision=lax.Precision.HIGHEST) + b_ref[l:l + 1, :]

    return pl.pallas_call(body, out_shape=jax.ShapeDtypeStruct((2, N_DEV, cols), F32), name=name,
                          compiler_params=pltpu.CompilerParams(vmem_limit_bytes=VMEM_LIMIT))(c_all, ada_w, ada_b_mine)


def _ada_w_grad(c_all_t, dmod_mine, name):
    cols = dmod_mine.shape[-1]

    def body(ct_ref, d_ref, o_ref):
        cv = ct_ref[...]
        ca = cv * _sigmoid(cv)
        for l in range(2):
            acc = ca[:, 0:1] * d_ref[l, 0:1, :]
            for b in range(1, N_DEV):
                acc = acc + ca[:, b:b + 1] * d_ref[l, b:b + 1, :]
            o_ref[l] = acc

    return pl.pallas_call(body, out_shape=jax.ShapeDtypeStruct((2, D, cols), F32), name=name,
                          compiler_params=pltpu.CompilerParams(vmem_limit_bytes=VMEM_LIMIT))(c_all_t, dmod_mine)


def _adamw_math(w, g, m, v):
    m2 = ADAM_B1 * m + (1.0 - ADAM_B1) * g
    v2 = ADAM_B2 * v + (1.0 - ADAM_B2) * (g * g)
    m_hat = m2 / (1.0 - ADAM_B1 ** ADAM_STEP)
    v_hat = v2 / (1.0 - ADAM_B2 ** ADAM_STEP)
    delta = -ADAM_LR * (m_hat / (jnp.sqrt(v_hat) + ADAM_EPS) + ADAM_WD * w)
    return delta, m2, v2


def _adamw_big(w, g, m, v, name):
    n_l, rows, cols = w.shape
    parts = list(g) if isinstance(g, (list, tuple)) else None
    sizes = (512, 352, 256, 128, 64, 32, 16, 8) if parts is None or len(parts) == 1 else (128, 64, 32, 16, 8)
    br = next(b for b in sizes if rows % b == 0)
    nr = rows // br

    def body(*refs):
        w_ref, g_refs, (m_ref, v_ref, go_ref, d_ref, mo_ref, vo_ref) = refs[0], refs[1:-6], refs[-6:]

        def update(gsum):
            go_ref[...] = gsum
            d_ref[...], mo_ref[...], vo_ref[...] = _adamw_math(w_ref[...], gsum, m_ref[...], v_ref[...])

        if parts is None:
            update(g_refs[0][...])
        else:
            for l, g_ref in enumerate(g_refs):
                @pl.when(pl.program_id(0) == l)
                def _(g_ref=g_ref):
                    gsum = g_ref[0, :, 0:cols].astype(F32)
                    for s in range(1, N_DEV):
                        gsum = gsum + g_ref[s, :, 0:cols].astype(F32)
                    update(gsum)

    blk = pl.BlockSpec((None, br, cols), lambda l, i: (l, i, 0))
    if parts is None:
        g_specs, g_args = [blk], [g]
    else:
        def part_spec(l_mine, width):
            return pl.BlockSpec((N_DEV, br, width),
                                lambda l, i: (0, jnp.where(l < l_mine, 0, jnp.where(l == l_mine, i, nr - 1)), 0))
        g_specs, g_args = [part_spec(l, p.shape[-1]) for l, p in enumerate(parts)], parts
    shp = jax.ShapeDtypeStruct((n_l, rows, cols), F32)
    return pl.pallas_call(
        body, grid=(n_l, nr), in_specs=[blk] + g_specs + [blk, blk], out_specs=[blk] * 4, out_shape=[shp] * 4,
        compiler_params=pltpu.CompilerParams(dimension_semantics=("arbitrary", "arbitrary"), vmem_limit_bytes=VMEM_LIMIT),
        name=name)(w, *g_args, m, v)


def _adamw_small(ws, gs, ms, vs, name):
    n = len(ws)

    def body(*refs):
        for i in range(n):
            w_ref, g_ref, m_ref, v_ref = (refs[k * n + i] for k in range(4))
            d_ref, mo_ref, vo_ref = (refs[(4 + k) * n + i] for k in range(3))
            d_ref[...], mo_ref[...], vo_ref[...] = _adamw_math(w_ref[...], g_ref[...], m_ref[...], v_ref[...])

    shapes = [jax.ShapeDtypeStruct(w.shape, F32) for w in ws]
    outs = pl.pallas_call(body, out_shape=shapes * 3, name=name,
                          compiler_params=pltpu.CompilerParams(vmem_limit_bytes=VMEM_LIMIT))(*ws, *gs, *ms, *vs)
    return outs[:n], outs[n:2 * n], outs[2 * n:]


def _as2d(a):
    return a.reshape(-1, a.shape[-1])


def kernel(x, c, ada_w, ada_b, norm_pre, norm_post, ffn_w13, ffn_w2, ab_w_in, a_conv_w, a_conv_b, a_gate_w, a_gate_b, a_lam, b_conv_w, b_conv_b, b_norm_g, b_norm_b, ab_w_out, c_w_in, c_b_in, c_norm_g, c_norm_b, c_w_s, c_b_s, c_w_out, loss_target, m_ada_w, m_ada_b, m_norm_pre, m_norm_post, m_ffn_w13, m_ffn_w2, m_ab_w_in, m_a_conv_w, m_a_conv_b, m_a_gate_w, m_a_gate_b, m_a_lam, m_b_conv_w, m_b_conv_b, m_b_norm_g, m_b_norm_b, m_ab_w_out, m_c_w_in, m_c_b_in, m_c_norm_g, m_c_norm_b, m_c_w_s, m_c_b_s, m_c_w_out, v_ada_w, v_ada_b, v_norm_pre, v_norm_post, v_ffn_w13, v_ffn_w2, v_ab_w_in, v_a_conv_w, v_a_conv_b, v_a_gate_w, v_a_gate_b, v_a_lam, v_b_conv_w, v_b_conv_b, v_b_norm_g, v_b_norm_b, v_ab_w_out, v_c_w_in, v_c_b_in, v_c_norm_g, v_c_norm_b, v_c_w_s, v_c_b_s, v_c_w_out):
    me = 4 * lax.axis_index("x") + 2 * lax.axis_index("y") + lax.axis_index("c")
    weights = dict(ada_w=ada_w, ada_b=ada_b, norm_pre=norm_pre, norm_post=norm_post, ffn_w13=ffn_w13, ffn_w2=ffn_w2,
                   ab_w_in=ab_w_in, a_conv_w=a_conv_w, a_conv_b=a_conv_b, a_gate_w=a_gate_w, a_gate_b=a_gate_b, a_lam=a_lam,
                   b_conv_w=b_conv_w, b_conv_b=b_conv_b, b_norm_g=b_norm_g, b_norm_b=b_norm_b, ab_w_out=ab_w_out,
                   c_w_in=c_w_in, c_b_in=c_b_in, c_norm_g=c_norm_g, c_norm_b=c_norm_b, c_w_s=c_w_s, c_b_s=c_b_s, c_w_out=c_w_out)
    moms = dict(ada_w=m_ada_w, ada_b=m_ada_b, norm_pre=m_norm_pre, norm_post=m_norm_post, ffn_w13=m_ffn_w13, ffn_w2=m_ffn_w2,
                ab_w_in=m_ab_w_in, a_conv_w=m_a_conv_w, a_conv_b=m_a_conv_b, a_gate_w=m_a_gate_w, a_gate_b=m_a_gate_b,
                a_lam=m_a_lam, b_conv_w=m_b_conv_w, b_conv_b=m_b_conv_b, b_norm_g=m_b_norm_g, b_norm_b=m_b_norm_b,
                ab_w_out=m_ab_w_out, c_w_in=m_c_w_in, c_b_in=m_c_b_in, c_norm_g=m_c_norm_g, c_norm_b=m_c_norm_b,
                c_w_s=m_c_w_s, c_b_s=m_c_b_s, c_w_out=m_c_w_out)
    vars_ = dict(ada_w=v_ada_w, ada_b=v_ada_b, norm_pre=v_norm_pre, norm_post=v_norm_post, ffn_w13=v_ffn_w13, ffn_w2=v_ffn_w2,
                 ab_w_in=v_ab_w_in, a_conv_w=v_a_conv_w, a_conv_b=v_a_conv_b, a_gate_w=v_a_gate_w, a_gate_b=v_a_gate_b,
                 a_lam=v_a_lam, b_conv_w=v_b_conv_w, b_conv_b=v_b_conv_b, b_norm_g=v_b_norm_g, b_norm_b=v_b_norm_b,
                 ab_w_out=v_ab_w_out, c_w_in=v_c_w_in, c_b_in=v_c_b_in, c_norm_g=v_c_norm_g, c_norm_b=v_c_norm_b,
                 c_w_s=v_c_w_s, c_b_s=v_c_b_s, c_w_out=v_c_w_out)
    names = list(weights)

    w13s = ffn_w13.astype(BF).reshape(4, D, FF_SHARD)
    tail, blank = w13s[..., FF_MAIN:], jnp.zeros((4, D, FF_TAIL), BF)
    tail_tile = jnp.where(me % 2 == 1, jnp.concatenate([blank, tail], -1), jnp.concatenate([tail, blank], -1))
    w13b = jnp.concatenate([w13s[..., :FF_MAIN], tail_tile], -1)
    small_shapes = [(D,), (2, 3, 128), (2, 3, 128), (CONV_A, 64), (CONV_B, 64), (256,), (128,), (128,)]
    small = _pack([c, norm_pre, norm_post, a_conv_w, b_conv_w, c_b_in, c_norm_g, c_norm_b], 1024)
    w2b = ffn_w2.astype(BF).reshape(4, W2_SHARD, D)
    shards = {f"f{f}": [w13b[f], w2b[f]] for f in range(4)}
    shards["ab"] = [ab_w_in[0].astype(BF), ab_w_out[0].astype(BF)]
    shards["c"] = [c_w_in[0].astype(BF), c_w_out[0].astype(BF)]
    w13g0, w2g0, small_g = _exchange(shards["f0"] + [small], "gather", "gather_first")
    plan = dict(shards=shards, gather={0: ["ab", "f1"], 2: ["f2"], 3: ["c", "f3"]},
                scatter={"ffn_bwd_l1s0": ["f3.0", "f3.1", "c.0", "c.1"], "ffn_bwd_l0s2": ["f2.0", "f2.1"],
                         "mixab_bwd_l0s1": ["f1.0", "f1.1"], "wgrad_w13_l0s0": ["ab.0", "ab.1"], "wgrad_w2_l0s0": ["f0.0"]})
    c_all, npre_g, npost_g, acw_g, bcw_g, cbin_g, cng_g, cnb_g = _unpack(small_g.reshape(N_DEV, -1), small_shapes)

    def cat_last(a):
        return jnp.moveaxis(a, 0, -2).reshape(a.shape[1:-1] + (N_DEV * a.shape[-1],))

    ada_b_mine = lax.dynamic_slice_in_dim(ada_b, me * ada_w.shape[-1], ada_w.shape[-1], axis=1)
    (mod_g,) = _exchange([_mod_part(c_all, ada_w, ada_b_mine, "mod_part")], "gather", "gather_mod")
    mod = cat_last(lax.dynamic_index_in_dim(mod_g, me, axis=2, keepdims=False)).reshape(2, 3, 3, D)

    p = dict(mod=mod, norm_pre=cat_last(npre_g), norm_post=cat_last(npost_g), wsets={"f0": [w13g0, w2g0]},
             a_conv_w=cat_last(acw_g), a_conv_b=a_conv_b[0], a_gate_w=a_gate_w[0], a_gate_b=a_gate_b[0], a_lam=a_lam[0],
             b_conv_w=cat_last(bcw_g), b_conv_b=b_conv_b[0], b_norm_g=b_norm_g[0], b_norm_b=b_norm_b[0],
             c_b_in=cat_last(cbin_g), c_norm_g=cat_last(cng_g), c_norm_b=cat_last(cnb_g), c_w_s=c_w_s[0], c_b_s=c_b_s[0])

    loss_blk, grad_x, g = _local_step(x[0], loss_target[0], p, plan)
    loss = lax.psum(loss_blk[0, 0], ("x", "y", "c"))

    accs = g["accs"]
    dmod = jnp.stack([jnp.stack([accs[f"{l}{j}"][0:3] for j in range(3)]) for l in range(2)])
    dnpre = jnp.stack([jnp.stack([accs[f"{l}{j}"][3] for j in range(3)]) for l in range(2)])
    dnpost = jnp.stack([jnp.stack([accs[f"{l}{j}"][4] for j in range(3)]) for l in range(2)])
    sab = g["accs_ab"]
    half = W_A // 8
    dgate = g["gate"][0]
    dgw = jnp.stack([jnp.concatenate([dgate[half * hh:half * (hh + 1), half * hh:half * (hh + 1)],
                                      dgate[half * hh:half * (hh + 1), W_A + half * hh:W_A + half * (hh + 1)]], axis=1)
                     for hh in range(8)])
    dgb = jnp.concatenate([sab[5].reshape(8, half), sab[6].reshape(8, half)], axis=1)
    c_acc, c_dbin, c_dws, c_dbst = g["c_small"]
    red_shapes = [(2, 9216), (2, 3, D), (2, 3, D), (CONV_A, W_A), (W_A,), (8, half, 2 * half), (8, 2 * half), (W_A,),
                  (CONV_B, W_B), (W_B,), (W_B,), (W_B,), (2 * D,), (D,), (D,), (H_C, CHUNK, CHUNK), (H_C, CHUNK)]
    red = _pack([dmod.reshape(2, 9216), dnpre, dnpost, sab[0:4], sab[4], dgw, dgb, sab[7], g["dw31"][0:CONV_B], sab[8],
                 sab[9], sab[10], c_dbin[0], c_acc[5], c_acc[6], c_dws, jnp.transpose(c_dbst)], N_DEV * 1024)
    (red_r,) = _exchange([red.reshape(N_DEV, -1, 128)], "scatter", "scatter_small_grads")
    red_all, dmod_all = _exchange([_sum_slots(red_r, "sum_small_grads"), dmod.reshape(-1, 128)], "gather", "gather_small_grads")
    red_sum = red_all.reshape(-1)
    (g_ada_b, g_npre, g_npost, g_acw, g_acb, g_agw, g_agb, g_alam, g_bcw, g_bcb, g_bng, g_bnb, g_cbin, g_cng, g_cnb,
     g_cws, g_cbs) = _unpack(red_sum, red_shapes)
    dmod_all = dmod_all.reshape(N_DEV, 2, 9216)
    ncol = ada_w.shape[-1]
    dmod_mine = jnp.moveaxis(lax.dynamic_slice_in_dim(dmod_all, me * ncol, ncol, axis=2), 0, 1)
    g_ada_w = _ada_w_grad(jnp.transpose(c_all), dmod_mine, "ada_w_grad")

    def mine(a, width):
        return lax.dynamic_slice_in_dim(a, me * width, width, axis=a.ndim - 1)

    small_grads = dict(
        ada_b=g_ada_b, norm_pre=mine(g_npre, 128), norm_post=mine(g_npost, 128), a_conv_w=mine(g_acw, 64)[None],
        a_conv_b=g_acb[None], a_gate_w=g_agw[None], a_gate_b=g_agb[None], a_lam=g_alam[None], b_conv_w=mine(g_bcw, 64)[None],
        b_conv_b=g_bcb[None], b_norm_g=g_bng[None], b_norm_b=g_bnb[None], c_b_in=mine(g_cbin, 256)[None],
        c_norm_g=mine(g_cng, 128)[None], c_norm_b=mine(g_cnb, 128)[None], c_w_s=g_cws[None], c_b_s=g_cbs[None])

    recv = dict(g["recv"])
    left = sorted(g["pending"])
    recv.update(zip(left, _exchange([g["pending"][k] for k in left], "scatter", "scatter_last")))
    big_partials = dict(ffn_w13=[recv[f"f{f}.0"] for f in range(4)], ffn_w2=[recv[f"f{f}.1"] for f in range(4)],
                        ab_w_in=[recv["ab.0"]], ab_w_out=[recv["ab.1"]], c_w_in=[recv["c.0"]], c_w_out=[recv["c.1"]],
                        ada_w=g_ada_w)

    grads, deltas, new_m, new_v = {}, {}, {}, {}

    def as3d(a):
        return a.reshape((-1,) + a.shape[-2:])

    for nm, gp in big_partials.items():
        shp = weights[nm].shape
        go, dl, mo, vo = _adamw_big(as3d(weights[nm]), gp, as3d(moms[nm]), as3d(vars_[nm]), "adamw_" + nm)
        grads[nm], deltas[nm], new_m[nm], new_v[nm] = (a.reshape(shp) for a in (go, dl, mo, vo))
    snames = list(small_grads)
    dls, mos, vos = _adamw_small([_as2d(weights[nm]) for nm in snames], [_as2d(small_grads[nm]) for nm in snames],
                                 [_as2d(moms[nm]) for nm in snames], [_as2d(vars_[nm]) for nm in snames], "adamw_small")
    for k, nm in enumerate(snames):
        shp = weights[nm].shape
        grads[nm] = small_grads[nm].reshape(shp)
        deltas[nm], new_m[nm], new_v[nm] = dls[k].reshape(shp), mos[k].reshape(shp), vos[k].reshape(shp)

    return (loss, grad_x[None], *[grads[nm] for nm in names], *[deltas[nm] for nm in names],
            *[new_m[nm] for nm in names], *[new_v[nm] for nm in names])
```

```python
import functools
import math

import jax
import jax.numpy as jnp
from jax import lax
from jax.experimental import pallas as pl
from jax.experimental.pallas import tpu as pltpu

F32 = jnp.float32
BF = jnp.bfloat16

N_DEV = 8
D = 1024
EPS = 1e-6
D_FF = 2816
FF_SHARD = 704
FF_PAD = 768
FF_MAIN = 640
FF_TAIL = FF_SHARD - FF_MAIN
W2_SHARD = 352
W_A = 512
W_B = 512
CONV_A = 4
CONV_B = 31
HALO_A = 8
HALO_B = 32
LRU_C = 8.0
CHUNK = 128
H_C = 8
ADAM_LR = 0.001
ADAM_B1 = 0.9
ADAM_B2 = 0.999
ADAM_EPS = 1e-08
ADAM_WD = 0.01
ADAM_STEP = 10
VMEM_LIMIT = 62 * 1024 * 1024
GELU_C = math.sqrt(2.0 / math.pi)

TM_FFN = 512
TM_FFN_BWD = 512
TM_MIX = 256
TM_MIXC_FWD = 512
TK_WGRAD = 2048


def _params(limit=VMEM_LIMIT):
    return pltpu.CompilerParams(dimension_semantics=("arbitrary",), vmem_limit_bytes=limit)


def _dot(a, b):
    return jnp.dot(a, b, preferred_element_type=F32)


def _dot_nt(a, b):
    return lax.dot_general(a, b, (((1,), (1,)), ((), ())), preferred_element_type=F32)


def _dot_tn(a, b):
    return lax.dot_general(a, b, (((0,), (0,)), ((), ())), preferred_element_type=F32)


def _sigmoid(x):
    return 0.5 + 0.5 * jnp.tanh(0.5 * x)


def _gelu(x):
    t = jnp.tanh(GELU_C * (x + 0.044715 * x * x * x))
    return 0.5 * x * (1.0 + t), t


def _gelu_grad(x, t):
    return 0.5 * (1.0 + t) + 0.5 * x * (1.0 - t * t) * GELU_C * (1.0 + 3.0 * 0.044715 * x * x)


def _rms(x):
    r = lax.rsqrt(jnp.mean(x * x, axis=-1, keepdims=True) + EPS)
    return x * r, r


def _colsum(x):
    return jnp.sum(x, axis=0, keepdims=True)


def _pre_fwd(x, vec_ref):
    xn, r = _rms(x)
    n = xn * vec_ref[3:4, :]
    h = n * (1.0 + vec_ref[1:2, :]) + vec_ref[0:1, :]
    return h, xn, r, n


def _post_fwd(x, f, vec_ref, res_w):
    fn, _ = _rms(f)
    return x + (res_w * (1.0 + vec_ref[2:3, :])) * (fn * vec_ref[4:5, :])


def _post_bwd(dout, f, vec_ref, acc_ref, res_w):
    fn, r2 = _rms(f)
    post_g = vec_ref[4:5, :]
    dy = dout * (res_w * (1.0 + vec_ref[2:3, :]))
    acc_ref[2:3, :] += _colsum(dout * (res_w * (fn * post_g)))
    acc_ref[4:5, :] += _colsum(dy * fn)
    dfn = dy * post_g
    return r2 * (dfn - fn * jnp.mean(dfn * fn, axis=-1, keepdims=True))


def _pre_bwd(dout, dh, xn, r, n, vec_ref, acc_ref):
    acc_ref[0:1, :] += _colsum(dh)
    acc_ref[1:2, :] += _colsum(dh * n)
    dn = dh * (1.0 + vec_ref[1:2, :])
    acc_ref[3:4, :] += _colsum(dn * xn)
    dxn = dn * vec_ref[3:4, :]
    return dout + r * (dxn - xn * jnp.mean(dxn * xn, axis=-1, keepdims=True))


def _tile(tm, ncol):
    return pl.BlockSpec((tm, ncol), lambda i: (i, 0))


def _full(shape):
    return pl.BlockSpec(shape, lambda i: (0,) * len(shape))


def _any():
    return pl.BlockSpec(memory_space=pl.ANY)


def _load_ffn_weights(w13_hbm, w2_hbm, w13_v, w2_v, tails, sems):
    copies = []
    if w13_hbm is not None:
        for j in range(N_DEV):
            half, k = divmod(j, 4)
            copies.append((w13_hbm.at[j, :, pl.ds(0, FF_MAIN)], w13_v.at[half, :, pl.ds(FF_MAIN * k, FF_MAIN)]))
            copies.append((w13_hbm.at[j, :, pl.ds(FF_MAIN, 128)], tails.at[j]))
    if w2_hbm is not None:
        for k in range(4):
            copies.append((w2_hbm.at[2 * k], w2_v.at[pl.ds(FF_MAIN * k, W2_SHARD), :]))
            copies.append((w2_hbm.at[2 * k + 1, pl.ds(0, FF_MAIN - W2_SHARD), :],
                           w2_v.at[pl.ds(FF_MAIN * k + W2_SHARD, FF_MAIN - W2_SHARD), :]))
            copies.append((w2_hbm.at[2 * k + 1, pl.ds(FF_MAIN - W2_SHARD, FF_TAIL), :],
                           w2_v.at[pl.ds(4 * FF_MAIN + FF_TAIL * k, FF_TAIL), :]))
    copies = [pltpu.make_async_copy(src, dst, sems.at[n]) for n, (src, dst) in enumerate(copies)]
    for cp in copies:
        cp.start()
    for cp in copies:
        cp.wait()
    if w13_hbm is not None:
        for pair in range(4):
            half, kk = divmod(pair, 2)
            w13_v[half, :, 4 * FF_MAIN + 128 * kk:4 * FF_MAIN + 128 * (kk + 1)] = tails[2 * pair] + tails[2 * pair + 1]


N_W13_COPIES = 2 * N_DEV
N_W2_COPIES = 12
_W13_SCRATCH = [pltpu.VMEM((2, D, D_FF), BF), pltpu.VMEM((N_DEV, D, 128), BF), pltpu.SemaphoreType.DMA((N_W13_COPIES,))]
_W2_SCRATCH = [pltpu.VMEM((D_FF, D), BF), pltpu.SemaphoreType.DMA((N_W2_COPIES,))]
_FFN_SCRATCH = [pltpu.VMEM((2, D, D_FF), BF), pltpu.VMEM((D_FF, D), BF), pltpu.VMEM((N_DEV, D, 128), BF),
                pltpu.SemaphoreType.DMA((N_W13_COPIES + N_W2_COPIES,))]
HID_CHUNKS = ((0, 768), (768, 768), (1536, 768), (2304, 512))


def _hosted(body, n_in, n_out, n_scratch, n_comm, mode, grid):
    if not n_comm:
        return body

    def at(corner):
        hit = pl.program_id(0) == corner[0]
        for d in range(1, len(grid)):
            hit = hit & (pl.program_id(d) == corner[d])
        return hit

    def hosted(*refs):
        ins, cin = refs[:n_in], refs[n_in:n_in + n_comm]
        outs, cout = refs[n_in + n_comm:n_in + n_comm + n_out], refs[n_in + n_comm + n_out:n_in + 2 * n_comm + n_out]
        scratch = refs[n_in + 2 * n_comm + n_out:]
        own, sems = scratch[:n_scratch], scratch[n_scratch:]

        @pl.when(at([0] * len(grid)))
        def _():
            _exchange_ops(cin, cout, sems, mode, "start")

        body(*ins, *outs, *own)

        @pl.when(at([n - 1 for n in grid]))
        def _():
            _exchange_ops(cin, cout, sems, mode, "wait")

    return hosted


def _ffn_fwd(x, vec, w13g, w2g, res_w, name, comm=(), tgt=None):
    t_len = x.shape[0]
    tm = min(TM_FFN, t_len)
    nc = len(comm)
    head = tgt is not None

    def body(*refs):
        if head:
            x_ref, vec_ref, w13_hbm, w2_hbm, t_ref, xo_ref, f_ref, jac_ref, s_ref, loss_ref, w13_v, w2_v, tails, sems = refs
        else:
            x_ref, vec_ref, w13_hbm, w2_hbm, xo_ref, f_ref, jac_ref, s_ref, w13_v, w2_v, tails, sems = refs

        @pl.when(pl.program_id(0) == 0)
        def _():
            _load_ffn_weights(w13_hbm, w2_hbm, w13_v, w2_v, tails, sems)
            if head:
                loss_ref[...] = jnp.zeros((8, 128), F32)

        x_t = x_ref[...]
        h, _, _, _ = _pre_fwd(x_t, vec_ref)
        hb = h.astype(BF)
        acc = jnp.zeros((tm, D), F32)
        for c0, cw in HID_CHUNKS:
            g = _dot(hb, w13_v[0, :, c0:c0 + cw])
            u = _dot(hb, w13_v[1, :, c0:c0 + cw])
            sig = _sigmoid(g)
            sl = g * sig
            jac_ref[0, :, c0:c0 + cw] = (u * (sig + sl * (1.0 - sig))).astype(BF)
            jac_ref[1, :, c0:c0 + cw] = sl.astype(BF)
            sb = (sl * u).astype(BF)
            s_ref[:, c0:c0 + cw] = sb
            acc = acc + _dot(sb, w2_v[c0:c0 + cw, :])
        f_ref[...] = acc
        xo = _post_fwd(x_t, acc, vec_ref, res_w)
        if head:
            err = xo - t_ref[...]
            xo_ref[...] = err * (1.0 / D)
            loss_ref[...] += jnp.sum(err * err) * (0.5 / D)
        else:
            xo_ref[...] = xo

    nt = t_len // tm
    n_in, n_out = (5, 5) if head else (4, 4)
    outs = pl.pallas_call(
        _hosted(body, n_in, n_out, 4, nc, "gather", (nt,)), grid=(nt,),
        in_specs=[_tile(tm, D), _full((8, D)), _any(), _any()] + ([_tile(tm, D)] if head else []) + [_any()] * nc,
        out_specs=[_tile(tm, D), _tile(tm, D), pl.BlockSpec((2, tm, D_FF), lambda i: (0, i, 0)), _tile(tm, D_FF)]
        + ([_full((8, 128))] if head else []) + [_any()] * nc,
        out_shape=[jax.ShapeDtypeStruct((t_len, D), F32), jax.ShapeDtypeStruct((t_len, D), F32),
                   jax.ShapeDtypeStruct((2, t_len, D_FF), BF), jax.ShapeDtypeStruct((t_len, D_FF), BF)]
        + ([jax.ShapeDtypeStruct((8, 128), F32)] if head else []) + _exchange_shapes(comm, "gather"),
        scratch_shapes=_FFN_SCRATCH + (_exchange_scratch(nc) if nc else []), compiler_params=_params(), name=name,
    )(x, vec, w13g, w2g, *([tgt] if head else []), *comm)
    return outs[:n_out], outs[n_out:]


def _ffn_bwd(dout, x, fpre, jac, vec, w13g, w2g, res_w, name, comm=()):
    t_len = x.shape[0]
    tm = min(TM_FFN_BWD, t_len)
    nt = t_len // tm
    nc = len(comm)
    gu_spec = pl.BlockSpec((2, tm, D_FF), lambda i: (0, i, 0))

    def body_ds(dout_ref, f_ref, jac_ref, vec_ref, w2_hbm, dgu_ref, dfb_ref, acc_ref, w2_v, sems):
        @pl.when(pl.program_id(0) == 0)
        def _():
            _load_ffn_weights(None, w2_hbm, None, w2_v, None, sems)
            acc_ref[...] = jnp.zeros((8, D), F32)

        df = _post_bwd(dout_ref[...], f_ref[...], vec_ref, acc_ref, res_w)
        dfb = df.astype(BF)
        dfb_ref[...] = dfb
        for c0, cw in HID_CHUNKS:
            ds = _dot_nt(dfb, w2_v[c0:c0 + cw, :])
            dgu_ref[0, :, c0:c0 + cw] = (ds * jac_ref[0, :, c0:c0 + cw].astype(F32)).astype(BF)
            dgu_ref[1, :, c0:c0 + cw] = (ds * jac_ref[1, :, c0:c0 + cw].astype(F32)).astype(BF)

    dgu, dfb, acc_post = pl.pallas_call(
        body_ds, grid=(nt,),
        in_specs=[_tile(tm, D), _tile(tm, D), gu_spec, _full((8, D)), _any()],
        out_specs=[gu_spec, _tile(tm, D), _full((8, D))],
        out_shape=[jax.ShapeDtypeStruct((2, t_len, D_FF), BF), jax.ShapeDtypeStruct((t_len, D), BF),
                   jax.ShapeDtypeStruct((8, D), F32)],
        scratch_shapes=_W2_SCRATCH, compiler_params=_params(), name=name + "_ds",
    )(dout, fpre, jac, vec, w2g)

    def body_dh(dout_ref, x_ref, dgu_ref, vec_ref, accp_ref, w13_hbm, dx_ref, hb_ref, acc_ref, w13_v, tails, sems):
        @pl.when(pl.program_id(0) == 0)
        def _():
            _load_ffn_weights(w13_hbm, None, w13_v, None, tails, sems)
            acc_ref[...] = accp_ref[...]

        h, xn, r, n = _pre_fwd(x_ref[...], vec_ref)
        hb_ref[...] = h.astype(BF)
        dh = jnp.zeros((tm, D), F32)
        for c0, cw in HID_CHUNKS:
            dh = dh + (_dot_nt(dgu_ref[0, :, c0:c0 + cw], w13_v[0, :, c0:c0 + cw])
                       + _dot_nt(dgu_ref[1, :, c0:c0 + cw], w13_v[1, :, c0:c0 + cw]))
        dx_ref[...] = _pre_bwd(dout_ref[...], dh, xn, r, n, vec_ref, acc_ref)

    outs = pl.pallas_call(
        _hosted(body_dh, 6, 3, 3, nc, "scatter", (nt,)), grid=(nt,),
        in_specs=[_tile(tm, D), _tile(tm, D), gu_spec, _full((8, D)), _full((8, D)), _any()] + [_any()] * nc,
        out_specs=[_tile(tm, D), _tile(tm, D), _full((8, D))] + [_any()] * nc,
        out_shape=[jax.ShapeDtypeStruct((t_len, D), F32), jax.ShapeDtypeStruct((t_len, D), BF),
                   jax.ShapeDtypeStruct((8, D), F32)] + _exchange_shapes(comm, "scatter"),
        scratch_shapes=_W13_SCRATCH + (_exchange_scratch(nc) if nc else []), compiler_params=_params(), name=name + "_dh",
    )(dout, x, dgu, vec, acc_post, w13g, *comm)
    dx, hb, acc = outs[:3]
    return (dx, dgu, hb, dfb, acc), outs[3:]


def _w13_slots(acc, o_ref):
    for k in range(4):
        o_ref[k, :, 0:FF_MAIN] = acc[:, FF_MAIN * k:FF_MAIN * (k + 1)].astype(BF)
        pair_tile = acc[:, 4 * FF_MAIN + 128 * (k // 2):4 * FF_MAIN + 128 * (k // 2 + 1)]
        o_ref[k, :, FF_MAIN:FF_PAD] = (pair_tile if k % 2 == 0 else pltpu.roll(pair_tile, FF_TAIL, 1)).astype(BF)


def _w2_slots(acc, o_ref):
    rest = FF_MAIN - W2_SHARD
    for k in range(4):
        o_ref[2 * k] = acc[FF_MAIN * k:FF_MAIN * k + W2_SHARD, :].astype(BF)
        o_ref[2 * k + 1, 0:rest, :] = acc[FF_MAIN * k + W2_SHARD:FF_MAIN * (k + 1), :].astype(BF)
        o_ref[2 * k + 1, rest:W2_SHARD, :] = acc[4 * FF_MAIN + FF_TAIL * k:4 * FF_MAIN + FF_TAIL * (k + 1), :].astype(BF)


def _wgrad(a, b, j_count, m, n, a_mode, b_mode, out_rows, out_dtype, name, tk=TK_WGRAD, col_slots=1, comm=(),
           slots=None):
    t_len = a.shape[-2]
    tk = min(tk, t_len)
    nk = t_len // tk
    wn = n // col_slots
    nc = len(comm)

    def spec(mode, width):
        if mode == "stack":
            return pl.BlockSpec((None, tk, width), lambda j, t: (j, t, 0))
        return pl.BlockSpec((tk, width), lambda j, t: (t, 0))

    def body(a_ref, b_ref, o_ref, acc):
        t = pl.program_id(1)

        @pl.when(t == 0)
        def _():
            acc[...] = jnp.zeros((m, n), F32)

        acc[...] += _dot_tn(a_ref[...], b_ref[...])

        @pl.when(t == nk - 1)
        def _():
            if slots is not None:
                slots[0](acc, o_ref)
            elif col_slots == 1:
                o_ref[...] = acc[0:out_rows, :].astype(out_dtype)
            else:
                for s in range(col_slots):
                    o_ref[s] = acc[0:out_rows, wn * s:wn * (s + 1)].astype(out_dtype)

    if slots is not None:
        blk = slots[1]
        out_spec = pl.BlockSpec(blk, lambda j, t: (j,) + (0,) * (len(blk) - 1))
        out_shape = jax.ShapeDtypeStruct((j_count * blk[0],) + blk[1:], BF)
    elif col_slots == 1:
        out_spec = pl.BlockSpec((None, out_rows, n), lambda j, t: (j, 0, 0))
        out_shape = jax.ShapeDtypeStruct((j_count, out_rows, n), out_dtype)
    else:
        out_spec = pl.BlockSpec((col_slots, out_rows, wn), lambda j, t: (0, 0, 0))
        out_shape = jax.ShapeDtypeStruct((col_slots, out_rows, wn), out_dtype)
    outs = pl.pallas_call(
        _hosted(body, 2, 1, 1, nc, "scatter", (j_count, nk)), grid=(j_count, nk),
        in_specs=[spec(a_mode, m), spec(b_mode, n)] + [_any()] * nc,
        out_specs=[out_spec] + [_any()] * nc, out_shape=[out_shape] + _exchange_shapes(comm, "scatter"),
        scratch_shapes=[pltpu.VMEM((m, n), F32)] + (_exchange_scratch(nc) if nc else []),
        compiler_params=pltpu.CompilerParams(dimension_semantics=("arbitrary", "arbitrary"), vmem_limit_bytes=VMEM_LIMIT),
        name=name,
    )(a, b, *comm)
    return (outs[0], list(outs[1:])) if nc else outs[0]


def _c_mask_weights(ws_ref, wsm, wsmt):
    row = lax.broadcasted_iota(jnp.int32, (CHUNK, CHUNK), 0)
    col = lax.broadcasted_iota(jnp.int32, (CHUNK, CHUNK), 1)
    for hh in range(H_C):
        w = jnp.where(row >= col, ws_ref[hh], 0.0)
        wsm[hh] = w.astype(BF)
        if wsmt is not None:
            wsmt[hh] = w.T.astype(BF)


def _c_inner(pre, cvec_ref, wsm, bst_ref, mix_sc, tm):
    z, t = _gelu(pre)
    u = z[:, 0:D]
    v = z[:, D:2 * D]
    mu = jnp.mean(v, axis=-1, keepdims=True)
    vc = v - mu
    rstd = lax.rsqrt(jnp.mean(vc * vc, axis=-1, keepdims=True) + EPS)
    vhat = vc * rstd
    vnb = (vhat * cvec_ref[0:1, :] + cvec_ref[1:2, :]).astype(BF)
    for nn in range(tm // CHUNK):
        for hh in range(H_C):
            rows = slice(CHUNK * nn, CHUNK * (nn + 1))
            cols = slice(CHUNK * hh, CHUNK * (hh + 1))
            mix_sc[rows, cols] = _dot(wsm[hh], vnb[rows, cols]) + bst_ref[:, hh:hh + 1]
    return u, t, rstd, vhat, vnb


def _mixc_fwd(x, vec, w_in, b_in, cvec, ws, bst, w_out, name):
    t_len = x.shape[0]
    tm = min(TM_MIXC_FWD, t_len)

    def body(x_ref, vec_ref, win_ref, bin_ref, cvec_ref, ws_ref, bst_ref, wout_ref,
             xo_ref, f_ref, pre_ref, wsm, mix_sc):
        @pl.when(pl.program_id(0) == 0)
        def _():
            _c_mask_weights(ws_ref, wsm, None)

        x_t = x_ref[...]
        h, _, _, _ = _pre_fwd(x_t, vec_ref)
        hb = h.astype(BF)
        for j in range(N_DEV):
            cols = slice(256 * j, 256 * (j + 1))
            pre_ref[:, cols] = _dot(hb, win_ref[j]) + bin_ref[:, cols]
        u, _, _, _, _ = _c_inner(pre_ref[...], cvec_ref, wsm, bst_ref, mix_sc, tm)
        fpre = _dot((u * mix_sc[...]).astype(BF), wout_ref[...])
        f_ref[...] = fpre
        xo_ref[...] = _post_fwd(x_t, fpre, vec_ref, 1.0)

    return pl.pallas_call(
        body, grid=(t_len // tm,),
        in_specs=[_tile(tm, D), _full((8, D)), _full((N_DEV, D, 256)), _full((1, 2 * D)), _full((8, D)),
                  _full((H_C, CHUNK, CHUNK)), _full((CHUNK, H_C)), _full((D, D))],
        out_specs=[_tile(tm, D), _tile(tm, D), _tile(tm, 2 * D)],
        out_shape=[jax.ShapeDtypeStruct((t_len, D), F32), jax.ShapeDtypeStruct((t_len, D), F32),
                   jax.ShapeDtypeStruct((t_len, 2 * D), F32)],
        scratch_shapes=[pltpu.VMEM((H_C, CHUNK, CHUNK), BF), pltpu.VMEM((tm, D), F32)],
        compiler_params=_params(), name=name,
    )(x, vec, w_in, b_in, cvec, ws, bst, w_out)


def _mixc_bwd(dout, x, fpre, pre, vec, w_in, cvec, ws, bst, w_out, name):
    t_len = x.shape[0]
    tm = min(TM_MIX, t_len)
    nt = t_len // tm

    def body(dout_ref, x_ref, f_ref, pre_ref, vec_ref, win_ref, cvec_ref, ws_ref, bst_ref, wout_ref,
             dx_ref, dpre_ref, p_ref, hb_ref, dfb_ref, acc_ref, dbin_ref, dws_ref, dbst_ref,
             wsm, wsmt, mix_sc, dvn_sc, dmsum):
        i = pl.program_id(0)

        @pl.when(i == 0)
        def _():
            _c_mask_weights(ws_ref, wsm, wsmt)
            acc_ref[...] = jnp.zeros((8, D), F32)
            dbin_ref[...] = jnp.zeros((8, 2 * D), F32)
            dws_ref[...] = jnp.zeros((H_C, CHUNK, CHUNK), F32)
            dmsum[...] = jnp.zeros((CHUNK, D), F32)

        dout_t = dout_ref[...]
        df = _post_bwd(dout_t, f_ref[...], vec_ref, acc_ref, 1.0)
        dfb = df.astype(BF)
        dfb_ref[...] = dfb
        h, xn, r, n = _pre_fwd(x_ref[...], vec_ref)
        hb_ref[...] = h.astype(BF)
        pre_t = pre_ref[...]
        u, t, rstd, vhat, vnb = _c_inner(pre_t, cvec_ref, wsm, bst_ref, mix_sc, tm)
        mix = mix_sc[...]
        p_ref[...] = (u * mix).astype(BF)
        dp = _dot_nt(dfb, wout_ref[...])
        du = dp * mix
        dmix = dp * u
        dmb = dmix.astype(BF)
        for nn in range(tm // CHUNK):
            rows = slice(CHUNK * nn, CHUNK * (nn + 1))
            dmsum[...] += dmix[rows, :]
            for hh in range(H_C):
                cols = slice(CHUNK * hh, CHUNK * (hh + 1))
                dvn_sc[rows, cols] = _dot(wsmt[hh], dmb[rows, cols])
                dws_ref[hh] += _dot_nt(dmb[rows, cols], vnb[rows, cols])
        dvn = dvn_sc[...]
        acc_ref[5:6, :] += _colsum(dvn * vhat)
        acc_ref[6:7, :] += _colsum(dvn)
        dvhat = dvn * cvec_ref[0:1, :]
        dv = rstd * (dvhat - jnp.mean(dvhat, axis=-1, keepdims=True)
                     - vhat * jnp.mean(dvhat * vhat, axis=-1, keepdims=True))
        gg = _gelu_grad(pre_t, t)
        dpre_u = du * gg[:, 0:D]
        dpre_v = dv * gg[:, D:2 * D]
        dbin_ref[0:1, 0:D] += _colsum(dpre_u)
        dbin_ref[0:1, D:2 * D] += _colsum(dpre_v)
        dpre_ref[:, 0:D] = dpre_u.astype(BF)
        dpre_ref[:, D:2 * D] = dpre_v.astype(BF)
        dh = jnp.zeros((tm, D), F32)
        for j in range(N_DEV):
            dh = dh + _dot_nt(dpre_ref[:, 256 * j:256 * (j + 1)], win_ref[j])
        dx_ref[...] = _pre_bwd(dout_t, dh, xn, r, n, vec_ref, acc_ref)

        @pl.when(i == nt - 1)
        def _():
            row = lax.broadcasted_iota(jnp.int32, (CHUNK, CHUNK), 0)
            col = lax.broadcasted_iota(jnp.int32, (CHUNK, CHUNK), 1)
            for hh in range(H_C):
                dws_ref[hh] = jnp.where(row >= col, dws_ref[hh], 0.0)
                dbst_ref[:, hh:hh + 1] = jnp.sum(dmsum[:, CHUNK * hh:CHUNK * (hh + 1)], axis=1, keepdims=True)

    return pl.pallas_call(
        body, grid=(nt,),
        in_specs=[_tile(tm, D), _tile(tm, D), _tile(tm, D), _tile(tm, 2 * D), _full((8, D)), _full((N_DEV, D, 256)),
                  _full((8, D)), _full((H_C, CHUNK, CHUNK)), _full((CHUNK, H_C)), _full((D, D))],
        out_specs=[_tile(tm, D), _tile(tm, 2 * D), _tile(tm, D), _tile(tm, D), _tile(tm, D), _full((8, D)),
                   _full((8, 2 * D)), _full((H_C, CHUNK, CHUNK)), _full((CHUNK, H_C))],
        out_shape=[jax.ShapeDtypeStruct((t_len, D), F32), jax.ShapeDtypeStruct((t_len, 2 * D), BF),
                   jax.ShapeDtypeStruct((t_len, D), BF), jax.ShapeDtypeStruct((t_len, D), BF),
                   jax.ShapeDtypeStruct((t_len, D), BF), jax.ShapeDtypeStruct((8, D), F32),
                   jax.ShapeDtypeStruct((8, 2 * D), F32), jax.ShapeDtypeStruct((H_C, CHUNK, CHUNK), F32),
                   jax.ShapeDtypeStruct((CHUNK, H_C), F32)],
        scratch_shapes=[pltpu.VMEM((H_C, CHUNK, CHUNK), BF), pltpu.VMEM((H_C, CHUNK, CHUNK), BF),
                        pltpu.VMEM((tm, D), F32), pltpu.VMEM((tm, D), F32), pltpu.VMEM((CHUNK, D), F32)],
        compiler_params=_params(), name=name,
    )(dout, x, fpre, pre, vec, w_in, cvec, ws, bst, w_out)


def _gmean(x, g_ref):
    hi = x.astype(BF)
    lo = (x - hi.astype(F32)).astype(BF)
    return _dot(hi, g_ref[...]) + _dot(lo, g_ref[...])


def _log_sigmoid(lam):
    e = jnp.exp(-jnp.abs(lam))
    log1p = jnp.where(e < 1e-2, e * (1.0 - e * (0.5 - e * (1.0 / 3.0 - 0.25 * e))), jnp.log(1.0 + e))
    return jnp.minimum(lam, 0.0) - log1p


def _neg_expm1(y):
    series = -(y * (1.0 + y * (0.5 + y * (1.0 / 6.0 + y * (1.0 / 24.0 + y * (1.0 / 120.0))))))
    return jnp.where(y > -0.1, series, 1.0 - jnp.exp(y))


def _conv_causal(ext, taps_ref, bias, k_taps, halo, tm):
    acc = bias + taps_ref[0:1, :] * ext[halo - k_taps + 1:halo - k_taps + 1 + tm, :]
    for k in range(1, k_taps):
        off = halo - k_taps + 1 + k
        acc = acc + taps_ref[k:k + 1, :] * ext[off:off + tm, :]
    return acc


def _build_shifted(sh_ref, e, n_rows):
    sh_ref[0] = e
    for r in range(1, 8):
        sh_ref[r] = pltpu.roll(e, n_rows - r, 0)


def _shifted_rows(sh_ref, off, tm):
    base = off - off % 8
    return sh_ref[off % 8, base:base + tm, :]


def _scan(a, u, tm, reverse):
    row = lax.broadcasted_iota(jnp.int32, (tm, W_A), 0)
    d = 1
    while d < tm:
        if reverse:
            keep = row < tm - d
            shift = tm - d
        else:
            keep = row >= d
            shift = d
        a_sh = jnp.where(keep, pltpu.roll(a, shift, 0), 1.0)
        u_sh = jnp.where(keep, pltpu.roll(u, shift, 0), 0.0)
        u = a * u_sh + u
        a = a * a_sh
        d *= 2
    return a, u


def _a_gates(xc, cv_ref, wr_ref, wi_ref):
    xcb = xc.astype(BF)
    r = _sigmoid(_dot(xcb, wr_ref[...]) + cv_ref[5:6, :])
    ig = _sigmoid(_dot(xcb, wi_ref[...]) + cv_ref[6:7, :])
    ls = _log_sigmoid(cv_ref[7:8, :])
    la = LRU_C * r * ls
    a = jnp.exp(la)
    m = jnp.sqrt(_neg_expm1(2.0 * la))
    return xcb, r, ig, ls, a, m


def _b_norm(vc, cv_ref, g_ref):
    mu = _gmean(vc, g_ref)
    dv = vc - mu
    rstd = lax.rsqrt(_gmean(dv * dv, g_ref) + EPS)
    vhat = dv * rstd
    vln = vhat * cv_ref[9:10, :] + cv_ref[10:11, :]
    return rstd, vhat, vln


def _mixab_fwd(x, vec, w_in, cv, w31, wr, wi, gmat, w_out, name):
    t_len = x.shape[0]
    tm = min(TM_MIX, t_len)

    def body(x_ref, vec_ref, win_ref, cv_ref, w31_ref, wr_ref, wi_ref, g_ref, wout_ref,
             xo_ref, f_ref, z_ref, hs_ref, cvs_ref, ext_a, ext_b, hc, shifted):
        @pl.when(pl.program_id(0) == 0)
        def _():
            ext_a[0:HALO_A, :] = jnp.zeros((HALO_A, W_A), F32)
            ext_b[0:HALO_B, :] = jnp.zeros((HALO_B, W_B), F32)
            hc[...] = jnp.zeros((8, W_A), F32)

        x_t = x_ref[...]
        h, _, _, _ = _pre_fwd(x_t, vec_ref)
        hb = h.astype(BF)
        for j in range(N_DEV):
            z_ref[:, 256 * j:256 * (j + 1)] = _dot(hb, win_ref[j])
        ext_a[HALO_A:HALO_A + tm, :] = z_ref[:, W_A:2 * W_A]
        xc = _conv_causal(ext_a, cv_ref, cv_ref[4:5, :], CONV_A, HALO_A, tm)
        ext_a[0:HALO_A, :] = ext_a[tm:tm + HALO_A, :]
        cvs_ref[:, 0:W_A] = xc
        _, _, ig, _, a, m = _a_gates(xc, cv_ref, wr_ref, wi_ref)
        a_cum, hloc = _scan(a, m * ig * xc, tm, False)
        hs = hloc + a_cum * hc[0:1, :]
        hs_ref[...] = hs
        hc[0:1, :] = hs[tm - 1:tm, :]
        gel, _ = _gelu(z_ref[:, 0:W_A])
        ya = hs * gel
        ext_b[HALO_B:HALO_B + tm, :] = z_ref[:, 2 * W_A:2 * W_A + W_B] * _sigmoid(z_ref[:, 2 * W_A + W_B:2 * W_A + 2 * W_B])
        _build_shifted(shifted, ext_b[...], tm + HALO_B)
        vc = cv_ref[8:9, :] + w31_ref[0:1, :] * _shifted_rows(shifted, HALO_B - CONV_B + 1, tm)
        for k in range(1, CONV_B):
            vc = vc + w31_ref[k:k + 1, :] * _shifted_rows(shifted, HALO_B - CONV_B + 1 + k, tm)
        ext_b[0:HALO_B, :] = ext_b[tm:tm + HALO_B, :]
        cvs_ref[:, W_A:W_A + W_B] = vc
        _, _, vln = _b_norm(vc, cv_ref, g_ref)
        yb = vln * _sigmoid(vln)
        fpre = _dot(ya.astype(BF), wout_ref[0:W_A, :]) + _dot(yb.astype(BF), wout_ref[W_A:W_A + W_B, :])
        f_ref[...] = fpre
        xo_ref[...] = _post_fwd(x_t, fpre, vec_ref, 1.0)

    return pl.pallas_call(
        body, grid=(t_len // tm,),
        in_specs=[_tile(tm, D), _full((8, D)), _full((N_DEV, D, 256)), _full((16, W_A)), _full((32, W_B)),
                  _full((W_A, W_A)), _full((W_A, W_A)), _full((W_B, W_B)), _full((D, D))],
        out_specs=[_tile(tm, D), _tile(tm, D), _tile(tm, 2 * D), _tile(tm, W_A), _tile(tm, W_A + W_B)],
        out_shape=[jax.ShapeDtypeStruct((t_len, D), F32), jax.ShapeDtypeStruct((t_len, D), F32),
                   jax.ShapeDtypeStruct((t_len, 2 * D), F32), jax.ShapeDtypeStruct((t_len, W_A), F32),
                   jax.ShapeDtypeStruct((t_len, W_A + W_B), F32)],
        scratch_shapes=[pltpu.VMEM((tm + HALO_A, W_A), F32), pltpu.VMEM((tm + HALO_B, W_B), F32), pltpu.VMEM((8, W_A), F32),
                        pltpu.VMEM((8, tm + HALO_B, W_B), F32)],
        compiler_params=_params(), name=name,
    )(x, vec, w_in, cv, w31, wr, wi, gmat, w_out)


def _mixab_bwd(dout, x, fpre, z, cvs, hs, vec, w_in, cv, w31, wr, wi, gmat, w_out, name, comm=()):
    t_len = x.shape[0]
    tm = min(TM_MIX, t_len)
    nt = t_len // tm

    def rev(i):
        return nt - 1 - i

    def rtile(ncol):
        return pl.BlockSpec((tm, ncol), lambda i: (rev(i), 0))

    def body(dout_ref, x_ref, f_ref, z_ref, cvs_ref, hs_ref, hsp_ref, vec_ref, win_ref, cv_ref, w31_ref, wr_ref, wi_ref,
             g_ref, wout_ref,
             dx_ref, dz_ref, yab_ref, hb_ref, dfb_ref, xcb_ref, dri_ref, acc_ref, accs_ref, dw31_ref,
             ext_h, ext_dx, ext_dv, carry, shifted):
        i = pl.program_id(0)
        has_prev = (rev(i) > 0).astype(F32)

        @pl.when(i == 0)
        def _():
            acc_ref[...] = jnp.zeros((8, D), F32)
            accs_ref[...] = jnp.zeros((16, W_A), F32)
            dw31_ref[...] = jnp.zeros((32, W_B), F32)
            ext_dx[tm:tm + HALO_A, :] = jnp.zeros((HALO_A, W_A), F32)
            ext_dv[tm:tm + HALO_B, :] = jnp.zeros((HALO_B, W_B), F32)
            carry[...] = jnp.zeros((8, W_A), F32)

        dout_t = dout_ref[...]
        df = _post_bwd(dout_t, f_ref[...], vec_ref, acc_ref, 1.0)
        dfb = df.astype(BF)
        dfb_ref[...] = dfb
        h, xn, r_x, n = _pre_fwd(x_ref[...], vec_ref)
        hb_ref[...] = h.astype(BF)

        ag = z_ref[:, 0:W_A]
        ax = z_ref[:, W_A:2 * W_A]
        bv = z_ref[:, 2 * W_A:2 * W_A + W_B]
        sg = _sigmoid(z_ref[:, 2 * W_A + W_B:2 * W_A + 2 * W_B])
        vv = bv * sg
        xc = cvs_ref[:, 0:W_A]
        xcb, r, ig, ls, a, m = _a_gates(xc, cv_ref, wr_ref, wi_ref)
        xcb_ref[...] = xcb
        hs_t = hs_ref[...]
        ext_h[0:8, :] = hsp_ref[...] * has_prev
        ext_h[8:8 + tm, :] = hs_t
        hprev = ext_h[7:7 + tm, :]
        gel, tg = _gelu(ag)
        rstd, vhat, vln = _b_norm(cvs_ref[:, W_A:W_A + W_B], cv_ref, g_ref)
        sv = _sigmoid(vln)
        yab_ref[:, 0:W_A] = (hs_t * gel).astype(BF)
        yab_ref[:, W_A:W_A + W_B] = (vln * sv).astype(BF)

        dya = _dot_nt(dfb, wout_ref[0:W_A, :])
        dyb = _dot_nt(dfb, wout_ref[W_A:W_A + W_B, :])

        dag = dya * hs_t * _gelu_grad(ag, tg)
        row = lax.broadcasted_iota(jnp.int32, (tm, W_A), 0)
        last = row == tm - 1
        a_next = jnp.where(last, 1.0, pltpu.roll(a, tm - 1, 0))
        u0 = dya * gel + jnp.where(last, carry[0:1, :], 0.0)
        _, dhs = _scan(a_next, u0, tm, True)
        carry[0:1, :] = a[0:1, :] * dhs[0:1, :]
        da = dhs * hprev
        dm = dhs * ig * xc
        di = dhs * m * xc
        dxc = dhs * m * ig
        dla = da * a - dm * (a * a) / m
        accs_ref[7:8, :] += _colsum(dla * r) * (LRU_C * _sigmoid(-cv_ref[7:8, :]))
        drp = (dla * (LRU_C * ls)) * r * (1.0 - r)
        dip = di * ig * (1.0 - ig)
        accs_ref[5:6, :] += _colsum(drp)
        accs_ref[6:7, :] += _colsum(dip)
        drpb = drp.astype(BF)
        dipb = dip.astype(BF)
        dri_ref[:, 0:W_A] = drpb
        dri_ref[:, W_A:2 * W_A] = dipb
        dxc = dxc + _dot_nt(drpb, wr_ref[...]) + _dot_nt(dipb, wi_ref[...])
        accs_ref[4:5, :] += _colsum(dxc)
        ext_dx[0:tm, :] = dxc
        dax = jnp.zeros((tm, W_A), F32)
        for k in range(CONV_A):
            ahead = ext_dx[CONV_A - 1 - k:CONV_A - 1 - k + tm, :]
            accs_ref[k:k + 1, :] += _colsum(ax * ahead)
            dax = dax + cv_ref[k:k + 1, :] * ahead
        ext_dx[tm:tm + HALO_A, :] = dxc[0:HALO_A, :]

        dvln = dyb * (sv * (1.0 + vln * (1.0 - sv)))
        accs_ref[9:10, :] += _colsum(dvln * vhat)
        accs_ref[10:11, :] += _colsum(dvln)
        dvhat = dvln * cv_ref[9:10, :]
        dvc = rstd * (dvhat - _gmean(dvhat, g_ref) - vhat * _gmean(dvhat * vhat, g_ref))
        accs_ref[8:9, :] += _colsum(dvc)
        ext_dv[0:tm, :] = dvc
        _build_shifted(shifted, ext_dv[...], tm + HALO_B)
        dvv = jnp.zeros((tm, W_B), F32)
        for k in range(CONV_B):
            ahead = _shifted_rows(shifted, CONV_B - 1 - k, tm)
            dw31_ref[k:k + 1, :] += _colsum(vv * ahead)
            dvv = dvv + w31_ref[k:k + 1, :] * ahead
        ext_dv[tm:tm + HALO_B, :] = dvc[0:HALO_B, :]

        dz_ref[:, 0:W_A] = dag.astype(BF)
        dz_ref[:, W_A:2 * W_A] = dax.astype(BF)
        dz_ref[:, 2 * W_A:2 * W_A + W_B] = (dvv * sg).astype(BF)
        dz_ref[:, 2 * W_A + W_B:2 * W_A + 2 * W_B] = (dvv * vv * (1.0 - sg)).astype(BF)
        dh = jnp.zeros((tm, D), F32)
        for j in range(N_DEV):
            dh = dh + _dot_nt(dz_ref[:, 256 * j:256 * (j + 1)], win_ref[j])
        dx_ref[...] = _pre_bwd(dout_t, dh, xn, r_x, n, vec_ref, acc_ref)

    hsp_spec = pl.BlockSpec((8, W_A), lambda i: (jnp.maximum(rev(i) * (tm // 8) - 1, 0), 0))
    nc = len(comm)
    outs = pl.pallas_call(
        _hosted(body, 15, 10, 5, nc, "scatter", (nt,)), grid=(nt,),
        in_specs=[rtile(D), rtile(D), rtile(D), rtile(2 * D), rtile(W_A + W_B), rtile(W_A), hsp_spec, _full((8, D)),
                  _full((N_DEV, D, 256)), _full((16, W_A)), _full((32, W_B)), _full((W_A, W_A)), _full((W_A, W_A)),
                  _full((W_B, W_B)), _full((D, D))] + [_any()] * nc,
        out_specs=[rtile(D), rtile(2 * D), rtile(D), rtile(D), rtile(D), rtile(W_A), rtile(2 * W_A), _full((8, D)),
                   _full((16, W_A)), _full((32, W_B))] + [_any()] * nc,
        out_shape=[jax.ShapeDtypeStruct((t_len, D), F32), jax.ShapeDtypeStruct((t_len, 2 * D), BF),
                   jax.ShapeDtypeStruct((t_len, D), BF), jax.ShapeDtypeStruct((t_len, D), BF),
                   jax.ShapeDtypeStruct((t_len, D), BF), jax.ShapeDtypeStruct((t_len, W_A), BF),
                   jax.ShapeDtypeStruct((t_len, 2 * W_A), BF), jax.ShapeDtypeStruct((8, D), F32),
                   jax.ShapeDtypeStruct((16, W_A), F32), jax.ShapeDtypeStruct((32, W_B), F32)] + _exchange_shapes(comm, "scatter"),
        scratch_shapes=[pltpu.VMEM((tm + 8, W_A), F32), pltpu.VMEM((tm + HALO_A, W_A), F32),
                        pltpu.VMEM((tm + HALO_B, W_B), F32), pltpu.VMEM((8, W_A), F32),
                        pltpu.VMEM((8, tm + HALO_B, W_B), F32)] + (_exchange_scratch(nc) if nc else []),
        compiler_params=_params(), name=name,
    )(dout, x, fpre, z, cvs, hs, hs, vec, w_in, cv, w31, wr, wi, gmat, w_out, *comm)
    return outs[:10], list(outs[10:])


def _vec(p, l, j):
    return jnp.concatenate([p["mod"][l, j], p["norm_pre"][l, j][None], p["norm_post"][l, j][None], jnp.zeros((3, D), F32)], 0)


def _ab_consts(p):
    gw = p["a_gate_w"]
    gb = p["a_gate_b"]
    half = W_A // 8
    wr = jax.scipy.linalg.block_diag(*[gw[hh, :, 0:half] for hh in range(8)]).astype(BF)
    wi = jax.scipy.linalg.block_diag(*[gw[hh, :, half:2 * half] for hh in range(8)]).astype(BF)
    rows = [p["a_conv_w"], p["a_conv_b"][None], gb[:, 0:half].reshape(1, W_A), gb[:, half:2 * half].reshape(1, W_A),
            p["a_lam"][None], p["b_conv_b"][None], p["b_norm_g"][None], p["b_norm_b"][None], jnp.zeros((5, W_A), F32)]
    cv = jnp.concatenate(rows, 0)
    w31 = jnp.concatenate([p["b_conv_w"], jnp.zeros((1, W_B), F32)], 0)
    grp = jnp.arange(W_B) // (W_B // 8)
    gmat = ((grp[:, None] == grp[None, :]).astype(F32) / (W_B // 8)).astype(BF)
    return cv, w31, wr, wi, gmat


SUBLAYERS = ("f0", "ab", "f1", "f2", "c", "f3")


def _local_step(x, tgt, p, plan=None):
    g = {}
    saved = []
    cur = x
    wsets = dict(p["wsets"])
    ab_c = _ab_consts(p)
    c_cvec = jnp.concatenate([p["c_norm_g"][None], p["c_norm_b"][None], jnp.zeros((6, D), F32)], 0)
    c_bst = jnp.transpose(p["c_b_s"])
    c_bin = p["c_b_in"][None]
    for s_idx, wname in enumerate(SUBLAYERS):
        l, j = divmod(s_idx, 3)
        vec = _vec(p, l, j)
        tag = f"l{l}s{j}"
        if j != 1:
            names = plan["gather"].get(s_idx, []) if plan else []
            comm = [a for nm in names for a in plan["shards"][nm]]
            res, got = _ffn_fwd(cur, vec, *wsets[wname], 0.5, "ffn_fwd_" + tag, comm=comm, tgt=tgt if s_idx == 5 else None)
            nxt, fpre, jac, s_act = res[:4]
            for k, nm in enumerate(names):
                wsets[nm] = list(got[2 * k:2 * k + 2])
            saved.append((cur, fpre, jac, s_act, vec))
            if s_idx == 5:
                loss_blk = res[4]
        elif l == 0:
            w_in, w_out = wsets[wname]
            nxt, fpre, z, hs, cvs = _mixab_fwd(cur, vec, w_in, *ab_c, w_out.reshape(D, D), "mixab_fwd_" + tag)
            saved.append((cur, fpre, z, hs, cvs, vec))
        else:
            w_in, w_out = wsets[wname]
            nxt, fpre, pre = _mixc_fwd(cur, vec, w_in, c_bin, c_cvec, p["c_w_s"], c_bst, w_out.reshape(D, D), "mixc_fwd_" + tag)
            saved.append((cur, fpre, pre, vec))
        cur = nxt
    dcur = cur
    accs, pending, recv = {}, {}, {}

    def take(host):
        keys = plan["scatter"].get(host, []) if plan else []
        return keys, [pending.pop(k) for k in keys]

    def put(keys, got):
        recv.update(zip(keys, got))

    for s_idx in reversed(range(6)):
        wname = SUBLAYERS[s_idx]
        l, j = divmod(s_idx, 3)
        tag = f"l{l}s{j}"
        sv = saved[s_idx]
        if j != 1:
            keys, comm = take("ffn_bwd_" + tag)
            xin, fpre, jac, s, vec = sv
            (dcur, dgu, hb, dfb, acc), got = _ffn_bwd(dcur, xin, fpre, jac, vec, *wsets[wname], 0.5, "ffn_bwd_" + tag, comm=comm)
            put(keys, got)
            keys, comm = take("wgrad_w13_" + tag)
            dw13 = _wgrad(hb, dgu, 2, D, D_FF, "share", "stack", D, BF, "wgrad_w13_" + tag, tk=TK_WGRAD, comm=comm,
                          slots=(_w13_slots, (4, D, FF_PAD)))
            if keys:
                dw13, got = dw13
                put(keys, got)
            pending[wname + ".0"] = dw13
            keys, comm = take("wgrad_w2_" + tag)
            dw2 = _wgrad(s, dfb, 1, D_FF, D, "share", "share", D_FF, BF, "wgrad_w2_" + tag, tk=TK_WGRAD, comm=comm,
                         slots=(_w2_slots, (N_DEV, W2_SHARD, D)))
            if keys:
                dw2, got = dw2
                put(keys, got)
            pending[wname + ".1"] = dw2
        elif l == 0:
            xin, fpre, z, hs, cvs, vec = sv
            w_in, w_out = wsets[wname]
            keys, comm = take("mixab_bwd_" + tag)
            (dcur, dz, yab, hb, dfb, xcb, dri, acc, accs_ab, dw31), got = _mixab_bwd(
                dcur, xin, fpre, z, cvs, hs, vec, w_in, *ab_c, w_out.reshape(D, D), "mixab_bwd_" + tag, comm=comm)
            put(keys, got)
            d_in = _wgrad(hb, dz, 1, D, 2 * D, "share", "share", D, BF, "wgrad_ab_in", tk=TK_WGRAD, col_slots=N_DEV)
            d_out = _wgrad(yab, dfb, 1, D, D, "share", "share", D, BF, "wgrad_ab_out")
            pending[wname + ".0"], pending[wname + ".1"] = d_in, d_out.reshape(N_DEV, D // N_DEV, D)
            g["gate"] = _wgrad(xcb, dri, 1, W_A, 2 * W_A, "share", "share", W_A, F32, "wgrad_gate")
            g["accs_ab"] = accs_ab
            g["dw31"] = dw31
        else:
            xin, fpre, pre, vec = sv
            w_in, w_out = wsets[wname]
            dcur, dpre, pb, hb, dfb, acc, dbin, dws, dbst = _mixc_bwd(
                dcur, xin, fpre, pre, vec, w_in, c_cvec, p["c_w_s"], c_bst, w_out.reshape(D, D), "mixc_bwd_" + tag)
            d_in = _wgrad(hb, dpre, 1, D, 2 * D, "share", "share", D, BF, "wgrad_c_in", tk=TK_WGRAD, col_slots=N_DEV)
            d_out = _wgrad(pb, dfb, 1, D, D, "share", "share", D, BF, "wgrad_c_out")
            pending[wname + ".0"], pending[wname + ".1"] = d_in, d_out.reshape(N_DEV, D // N_DEV, D)
            g["c_small"] = (acc, dbin, dws, dbst)
        accs[f"{l}{j}"] = acc
    g["accs"] = accs
    g["pending"] = pending
    g["recv"] = recv
    return loss_blk, dcur, g


def _exchange_ops(ins, outs, sems, mode, action):
    send_sems, recv_sems, loc_sems = sems
    n = len(ins)
    x, y, c = lax.axis_index("x"), lax.axis_index("y"), lax.axis_index("c")
    me = 4 * x + 2 * y + c

    def src(i, dev):
        return ins[i] if mode == "gather" else ins[i].at[dev]

    for i in range(n):
        cp = pltpu.make_async_copy(src(i, me), outs[i].at[me], loc_sems.at[i])
        if action == "start":
            cp.start()
        else:
            cp.wait()
    for mask in range(1, N_DEV):
        px = 1 - x if mask & 4 else x
        py = 1 - y if mask & 2 else y
        pc = 1 - c if mask & 1 else c
        peer = 4 * px + 2 * py + pc
        for i in range(n):
            k = i * (N_DEV - 1) + mask - 1
            cp = pltpu.make_async_remote_copy(
                src_ref=src(i, peer), dst_ref=outs[i].at[me if action == "start" else peer],
                send_sem=send_sems.at[k], recv_sem=recv_sems.at[k],
                device_id=(px, py, pc), device_id_type=pl.DeviceIdType.MESH)
            if action == "start":
                cp.start()
            else:
                cp.wait()


def _exchange_scratch(n):
    return [pltpu.SemaphoreType.DMA((n * (N_DEV - 1),)), pltpu.SemaphoreType.DMA((n * (N_DEV - 1),)),
            pltpu.SemaphoreType.DMA((n,))]


def _exchange_shapes(arrays, mode):
    return [jax.ShapeDtypeStruct(((N_DEV,) + a.shape) if mode == "gather" else a.shape, a.dtype) for a in arrays]


def _exchange(arrays, mode, name):
    n = len(arrays)

    def body(*refs):
        ins, outs, sems = refs[:n], refs[n:2 * n], refs[2 * n:]
        _exchange_ops(ins, outs, sems, mode, "start")
        _exchange_ops(ins, outs, sems, mode, "wait")

    return pl.pallas_call(
        body, in_specs=[pl.BlockSpec(memory_space=pl.ANY)] * n, out_specs=[pl.BlockSpec(memory_space=pl.ANY)] * n,
        out_shape=_exchange_shapes(arrays, mode), scratch_shapes=_exchange_scratch(n), name=name,
    )(*arrays)


def _gather_two_level(arrays, name):
    n = len(arrays)
    per = N_DEV - 1

    def body(*refs):
        ins, outs = refs[:n], refs[n:2 * n]
        send_sems, recv_sems, loc_sems = refs[2 * n:]
        x, y, c = lax.axis_index("x"), lax.axis_index("y"), lax.axis_index("c")
        me, sibling = (x, y, c), (x, y, 1 - c)
        chips = [(1 - x, y), (x, 1 - y), (1 - x, 1 - y)]

        def rows(i, dev):
            return outs[i].at[4 * dev[0] + 2 * dev[1] + dev[2]]

        def copy(i, k, block, to, src=None):
            return pltpu.make_async_remote_copy(
                src_ref=rows(i, block) if src is None else src, dst_ref=rows(i, block),
                send_sem=send_sems.at[i * per + k], recv_sem=recv_sems.at[i * per + k],
                device_id=to, device_id_type=pl.DeviceIdType.MESH)

        mine = [pltpu.make_async_copy(ins[i], rows(i, me), loc_sems.at[i]) for i in range(n)]
        for cp in mine:
            cp.start()
        first = []
        for i in range(n):
            first.append(copy(i, 0, me, sibling, src=ins[i]))
            first += [copy(i, 1 + j, me, (*chip, c), src=ins[i]) for j, chip in enumerate(chips)]
        for cp in first:
            cp.start()
        passed = []
        for j, chip in enumerate(chips):
            for i in range(n):
                copy(i, 1 + j, (*chip, c), me).wait_recv()
                fwd = copy(i, 4 + j, (*chip, c), sibling)
                fwd.start()
                passed.append(fwd)
        for i in range(n):
            copy(i, 0, sibling, me).wait_recv()
            for j, chip in enumerate(chips):
                copy(i, 4 + j, (*chip, 1 - c), me).wait_recv()
        for cp in first + passed:
            cp.wait_send()
        for cp in mine:
            cp.wait()

    return pl.pallas_call(
        body, in_specs=[pl.BlockSpec(memory_space=pl.ANY)] * n, out_specs=[pl.BlockSpec(memory_space=pl.ANY)] * n,
        out_shape=_exchange_shapes(arrays, "gather"), scratch_shapes=_exchange_scratch(n), name=name,
    )(*arrays)


def _sum_slots(a, name):
    def body(a_ref, o_ref):
        acc = a_ref[0]
        for s in range(1, N_DEV):
            acc = acc + a_ref[s]
        o_ref[...] = acc

    return pl.pallas_call(body, out_shape=jax.ShapeDtypeStruct(a.shape[1:], F32), name=name,
                          compiler_params=pltpu.CompilerParams(vmem_limit_bytes=VMEM_LIMIT))(a)


def _pack(pieces, mult):
    flat = jnp.concatenate([q.reshape(-1).astype(F32) for q in pieces])
    size = -(-flat.shape[0] // mult) * mult
    return jnp.pad(flat, (0, size - flat.shape[0])).reshape(size // 128, 128)


def _unpack(flat, shapes):
    out, off = [], 0
    for shp in shapes:
        size = math.prod(shp)
        out.append(flat[..., off:off + size].reshape(flat.shape[:-1] + tuple(shp)))
        off += size
    return out


def _mod_part(c_all, ada_w, ada_b_mine, name):
    cols = ada_w.shape[-1]

    def body(c_ref, w_ref, b_ref, o_ref):
        cv = c_ref[...]
        ca = cv * _sigmoid(cv)
        for l in range(2):
            o_ref[l] = jnp.dot(ca, w_ref[l], preferred_element_type=F32, precision=lax.Precision.HIGHEST) + b_ref[l:l + 1, :]

    return pl.pallas_call(body, out_shape=jax.ShapeDtypeStruct((2, N_DEV, cols), F32), name=name,
                          compiler_params=pltpu.CompilerParams(vmem_limit_bytes=VMEM_LIMIT))(c_all, ada_w, ada_b_mine)


def _ada_w_grad(c_all_t, dmod_mine, name):
    cols = dmod_mine.shape[-1]

    def body(ct_ref, d_ref, o_ref):
        cv = ct_ref[...]
        ca = cv * _sigmoid(cv)
        for l in range(2):
            acc = ca[:, 0:1] * d_ref[l, 0:1, :]
            for b in range(1, N_DEV):
                acc = acc + ca[:, b:b + 1] * d_ref[l, b:b + 1, :]
            o_ref[l] = acc

    return pl.pallas_call(body, out_shape=jax.ShapeDtypeStruct((2, D, cols), F32), name=name,
                          compiler_params=pltpu.CompilerParams(vmem_limit_bytes=VMEM_LIMIT))(c_all_t, dmod_mine)


def _adamw_math(w, g, m, v):
    m2 = ADAM_B1 * m + (1.0 - ADAM_B1) * g
    v2 = ADAM_B2 * v + (1.0 - ADAM_B2) * (g * g)
    m_hat = m2 / (1.0 - ADAM_B1 ** ADAM_STEP)
    v_hat = v2 / (1.0 - ADAM_B2 ** ADAM_STEP)
    delta = -ADAM_LR * (m_hat / (jnp.sqrt(v_hat) + ADAM_EPS) + ADAM_WD * w)
    return delta, m2, v2


def _adamw_big(w, g, m, v, name):
    n_l, rows, cols = w.shape
    parts = list(g) if isinstance(g, (list, tuple)) else None
    sizes = (512, 352, 256, 128, 64, 32, 16, 8) if parts is None or len(parts) == 1 else (128, 64, 32, 16, 8)
    br = next(b for b in sizes if rows % b == 0)
    nr = rows // br

    def body(*refs):
        w_ref, g_refs, (m_ref, v_ref, go_ref, d_ref, mo_ref, vo_ref) = refs[0], refs[1:-6], refs[-6:]

        def update(gsum):
            go_ref[...] = gsum
            d_ref[...], mo_ref[...], vo_ref[...] = _adamw_math(w_ref[...], gsum, m_ref[...], v_ref[...])

        if parts is None:
            update(g_refs[0][...])
        else:
            for l, g_ref in enumerate(g_refs):
                @pl.when(pl.program_id(0) == l)
                def _(g_ref=g_ref):
                    gsum = g_ref[0, :, 0:cols].astype(F32)
                    for s in range(1, N_DEV):
                        gsum = gsum + g_ref[s, :, 0:cols].astype(F32)
                    update(gsum)

    blk = pl.BlockSpec((None, br, cols), lambda l, i: (l, i, 0))
    if parts is None:
        g_specs, g_args = [blk], [g]
    else:
        def part_spec(l_mine, width):
            return pl.BlockSpec((N_DEV, br, width),
                                lambda l, i: (0, jnp.where(l < l_mine, 0, jnp.where(l == l_mine, i, nr - 1)), 0))
        g_specs, g_args = [part_spec(l, p.shape[-1]) for l, p in enumerate(parts)], parts
    shp = jax.ShapeDtypeStruct((n_l, rows, cols), F32)
    return pl.pallas_call(
        body, grid=(n_l, nr), in_specs=[blk] + g_specs + [blk, blk], out_specs=[blk] * 4, out_shape=[shp] * 4,
        compiler_params=pltpu.CompilerParams(dimension_semantics=("arbitrary", "arbitrary"), vmem_limit_bytes=VMEM_LIMIT),
        name=name)(w, *g_args, m, v)


def _adamw_small(ws, gs, ms, vs, name):
    n = len(ws)

    def body(*refs):
        for i in range(n):
            w_ref, g_ref, m_ref, v_ref = (refs[k * n + i] for k in range(4))
            d_ref, mo_ref, vo_ref = (refs[(4 + k) * n + i] for k in range(3))
            d_ref[...], mo_ref[...], vo_ref[...] = _adamw_math(w_ref[...], g_ref[...], m_ref[...], v_ref[...])

    shapes = [jax.ShapeDtypeStruct(w.shape, F32) for w in ws]
    outs = pl.pallas_call(body, out_shape=shapes * 3, name=name,
                          compiler_params=pltpu.CompilerParams(vmem_limit_bytes=VMEM_LIMIT))(*ws, *gs, *ms, *vs)
    return outs[:n], outs[n:2 * n], outs[2 * n:]


def _as2d(a):
    return a.reshape(-1, a.shape[-1])


def kernel(x, c, ada_w, ada_b, norm_pre, norm_post, ffn_w13, ffn_w2, ab_w_in, a_conv_w, a_conv_b, a_gate_w, a_gate_b, a_lam, b_conv_w, b_conv_b, b_norm_g, b_norm_b, ab_w_out, c_w_in, c_b_in, c_norm_g, c_norm_b, c_w_s, c_b_s, c_w_out, loss_target, m_ada_w, m_ada_b, m_norm_pre, m_norm_post, m_ffn_w13, m_ffn_w2, m_ab_w_in, m_a_conv_w, m_a_conv_b, m_a_gate_w, m_a_gate_b, m_a_lam, m_b_conv_w, m_b_conv_b, m_b_norm_g, m_b_norm_b, m_ab_w_out, m_c_w_in, m_c_b_in, m_c_norm_g, m_c_norm_b, m_c_w_s, m_c_b_s, m_c_w_out, v_ada_w, v_ada_b, v_norm_pre, v_norm_post, v_ffn_w13, v_ffn_w2, v_ab_w_in, v_a_conv_w, v_a_conv_b, v_a_gate_w, v_a_gate_b, v_a_lam, v_b_conv_w, v_b_conv_b, v_b_norm_g, v_b_norm_b, v_ab_w_out, v_c_w_in, v_c_b_in, v_c_norm_g, v_c_norm_b, v_c_w_s, v_c_b_s, v_c_w_out):
    me = 4 * lax.axis_index("x") + 2 * lax.axis_index("y") + lax.axis_index("c")
    weights = dict(ada_w=ada_w, ada_b=ada_b, norm_pre=norm_pre, norm_post=norm_post, ffn_w13=ffn_w13, ffn_w2=ffn_w2,
                   ab_w_in=ab_w_in, a_conv_w=a_conv_w, a_conv_b=a_conv_b, a_gate_w=a_gate_w, a_gate_b=a_gate_b, a_lam=a_lam,
                   b_conv_w=b_conv_w, b_conv_b=b_conv_b, b_norm_g=b_norm_g, b_norm_b=b_norm_b, ab_w_out=ab_w_out,
                   c_w_in=c_w_in, c_b_in=c_b_in, c_norm_g=c_norm_g, c_norm_b=c_norm_b, c_w_s=c_w_s, c_b_s=c_b_s, c_w_out=c_w_out)
    moms = dict(ada_w=m_ada_w, ada_b=m_ada_b, norm_pre=m_norm_pre, norm_post=m_norm_post, ffn_w13=m_ffn_w13, ffn_w2=m_ffn_w2,
                ab_w_in=m_ab_w_in, a_conv_w=m_a_conv_w, a_conv_b=m_a_conv_b, a_gate_w=m_a_gate_w, a_gate_b=m_a_gate_b,
                a_lam=m_a_lam, b_conv_w=m_b_conv_w, b_conv_b=m_b_conv_b, b_norm_g=m_b_norm_g, b_norm_b=m_b_norm_b,
                ab_w_out=m_ab_w_out, c_w_in=m_c_w_in, c_b_in=m_c_b_in, c_norm_g=m_c_norm_g, c_norm_b=m_c_norm_b,
                c_w_s=m_c_w_s, c_b_s=m_c_b_s, c_w_out=m_c_w_out)
    vars_ = dict(ada_w=v_ada_w, ada_b=v_ada_b, norm_pre=v_norm_pre, norm_post=v_norm_post, ffn_w13=v_ffn_w13, ffn_w2=v_ffn_w2,
                 ab_w_in=v_ab_w_in, a_conv_w=v_a_conv_w, a_conv_b=v_a_conv_b, a_gate_w=v_a_gate_w, a_gate_b=v_a_gate_b,
                 a_lam=v_a_lam, b_conv_w=v_b_conv_w, b_conv_b=v_b_conv_b, b_norm_g=v_b_norm_g, b_norm_b=v_b_norm_b,
                 ab_w_out=v_ab_w_out, c_w_in=v_c_w_in, c_b_in=v_c_b_in, c_norm_g=v_c_norm_g, c_norm_b=v_c_norm_b,
                 c_w_s=v_c_w_s, c_b_s=v_c_b_s, c_w_out=v_c_w_out)
    names = list(weights)

    w13s = ffn_w13.astype(BF).reshape(4, D, FF_SHARD)
    tail, blank = w13s[..., FF_MAIN:], jnp.zeros((4, D, FF_TAIL), BF)
    tail_tile = jnp.where(me % 2 == 1, jnp.concatenate([blank, tail], -1), jnp.concatenate([tail, blank], -1))
    w13b = jnp.concatenate([w13s[..., :FF_MAIN], tail_tile], -1)
    small_shapes = [(D,), (2, 3, 128), (2, 3, 128), (CONV_A, 64), (CONV_B, 64), (256,), (128,), (128,)]
    small = _pack([c, norm_pre, norm_post, a_conv_w, b_conv_w, c_b_in, c_norm_g, c_norm_b], 1024)
    w2b = ffn_w2.astype(BF).reshape(4, W2_SHARD, D)
    shards = {f"f{f}": [w13b[f], w2b[f]] for f in range(4)}
    shards["ab"] = [ab_w_in[0].astype(BF), ab_w_out[0].astype(BF)]
    shards["c"] = [c_w_in[0].astype(BF), c_w_out[0].astype(BF)]
    w13g0, w2g0, small_g = _gather_two_level(shards["f0"] + [small], "gather_first")
    plan = dict(shards=shards, gather={0: ["ab", "f1"], 2: ["f2"], 3: ["c", "f3"]},
                scatter={"ffn_bwd_l1s0": ["f3.0", "f3.1", "c.0", "c.1"], "ffn_bwd_l0s2": ["f2.0", "f2.1"],
                         "mixab_bwd_l0s1": ["f1.0", "f1.1"], "wgrad_w13_l0s0": ["ab.0", "ab.1"], "wgrad_w2_l0s0": ["f0.0"]})
    c_all, npre_g, npost_g, acw_g, bcw_g, cbin_g, cng_g, cnb_g = _unpack(small_g.reshape(N_DEV, -1), small_shapes)

    def cat_last(a):
        return jnp.moveaxis(a, 0, -2).reshape(a.shape[1:-1] + (N_DEV * a.shape[-1],))

    ada_b_mine = lax.dynamic_slice_in_dim(ada_b, me * ada_w.shape[-1], ada_w.shape[-1], axis=1)
    (mod_g,) = _exchange([_mod_part(c_all, ada_w, ada_b_mine, "mod_part")], "gather", "gather_mod")
    mod = cat_last(lax.dynamic_index_in_dim(mod_g, me, axis=2, keepdims=False)).reshape(2, 3, 3, D)

    p = dict(mod=mod, norm_pre=cat_last(npre_g), norm_post=cat_last(npost_g), wsets={"f0": [w13g0, w2g0]},
             a_conv_w=cat_last(acw_g), a_conv_b=a_conv_b[0], a_gate_w=a_gate_w[0], a_gate_b=a_gate_b[0], a_lam=a_lam[0],
             b_conv_w=cat_last(bcw_g), b_conv_b=b_conv_b[0], b_norm_g=b_norm_g[0], b_norm_b=b_norm_b[0],
             c_b_in=cat_last(cbin_g), c_norm_g=cat_last(cng_g), c_norm_b=cat_last(cnb_g), c_w_s=c_w_s[0], c_b_s=c_b_s[0])

    loss_blk, grad_x, g = _local_step(x[0], loss_target[0], p, plan)
    loss = lax.psum(loss_blk[0, 0], ("x", "y", "c"))

    accs = g["accs"]
    dmod = jnp.stack([jnp.stack([accs[f"{l}{j}"][0:3] for j in range(3)]) for l in range(2)])
    dnpre = jnp.stack([jnp.stack([accs[f"{l}{j}"][3] for j in range(3)]) for l in range(2)])
    dnpost = jnp.stack([jnp.stack([accs[f"{l}{j}"][4] for j in range(3)]) for l in range(2)])
    sab = g["accs_ab"]
    half = W_A // 8
    dgate = g["gate"][0]
    dgw = jnp.stack([jnp.concatenate([dgate[half * hh:half * (hh + 1), half * hh:half * (hh + 1)],
                                      dgate[half * hh:half * (hh + 1), W_A + half * hh:W_A + half * (hh + 1)]], axis=1)
                     for hh in range(8)])
    dgb = jnp.concatenate([sab[5].reshape(8, half), sab[6].reshape(8, half)], axis=1)
    c_acc, c_dbin, c_dws, c_dbst = g["c_small"]
    red_shapes = [(2, 9216), (2, 3, D), (2, 3, D), (CONV_A, W_A), (W_A,), (8, half, 2 * half), (8, 2 * half), (W_A,),
                  (CONV_B, W_B), (W_B,), (W_B,), (W_B,), (2 * D,), (D,), (D,), (H_C, CHUNK, CHUNK), (H_C, CHUNK)]
    red = _pack([dmod.reshape(2, 9216), dnpre, dnpost, sab[0:4], sab[4], dgw, dgb, sab[7], g["dw31"][0:CONV_B], sab[8],
                 sab[9], sab[10], c_dbin[0], c_acc[5], c_acc[6], c_dws, jnp.transpose(c_dbst)], N_DEV * 1024)
    (red_r,) = _exchange([red.reshape(N_DEV, -1, 128)], "scatter", "scatter_small_grads")
    red_all, dmod_all = _exchange([_sum_slots(red_r, "sum_small_grads"), dmod.reshape(-1, 128)], "gather", "gather_small_grads")
    red_sum = red_all.reshape(-1)
    (g_ada_b, g_npre, g_npost, g_acw, g_acb, g_agw, g_agb, g_alam, g_bcw, g_bcb, g_bng, g_bnb, g_cbin, g_cng, g_cnb,
     g_cws, g_cbs) = _unpack(red_sum, red_shapes)
    dmod_all = dmod_all.reshape(N_DEV, 2, 9216)
    ncol = ada_w.shape[-1]
    dmod_mine = jnp.moveaxis(lax.dynamic_slice_in_dim(dmod_all, me * ncol, ncol, axis=2), 0, 1)
    g_ada_w = _ada_w_grad(jnp.transpose(c_all), dmod_mine, "ada_w_grad")

    def mine(a, width):
        return lax.dynamic_slice_in_dim(a, me * width, width, axis=a.ndim - 1)

    small_grads = dict(
        ada_b=g_ada_b, norm_pre=mine(g_npre, 128), norm_post=mine(g_npost, 128), a_conv_w=mine(g_acw, 64)[None],
        a_conv_b=g_acb[None], a_gate_w=g_agw[None], a_gate_b=g_agb[None], a_lam=g_alam[None], b_conv_w=mine(g_bcw, 64)[None],
        b_conv_b=g_bcb[None], b_norm_g=g_bng[None], b_norm_b=g_bnb[None], c_b_in=mine(g_cbin, 256)[None],
        c_norm_g=mine(g_cng, 128)[None], c_norm_b=mine(g_cnb, 128)[None], c_w_s=g_cws[None], c_b_s=g_cbs[None])

    recv = dict(g["recv"])
    left = sorted(g["pending"])
    recv.update(zip(left, _exchange([g["pending"][k] for k in left], "scatter", "scatter_last")))
    big_partials = dict(ffn_w13=[recv[f"f{f}.0"] for f in range(4)], ffn_w2=[recv[f"f{f}.1"] for f in range(4)],
                        ab_w_in=[recv["ab.0"]], ab_w_out=[recv["ab.1"]], c_w_in=[recv["c.0"]], c_w_out=[recv["c.1"]],
                        ada_w=g_ada_w)

    grads, deltas, new_m, new_v = {}, {}, {}, {}

    def as3d(a):
        return a.reshape((-1,) + a.shape[-2:])

    for nm, gp in big_partials.items():
        shp = weights[nm].shape
        go, dl, mo, vo = _adamw_big(as3d(weights[nm]), gp, as3d(moms[nm]), as3d(vars_[nm]), "adamw_" + nm)
        grads[nm], deltas[nm], new_m[nm], new_v[nm] = (a.reshape(shp) for a in (go, dl, mo, vo))
    snames = list(small_grads)
    dls, mos, vos = _adamw_small([_as2d(weights[nm]) for nm in snames], [_as2d(small_grads[nm]) for nm in snames],
                                 [_as2d(moms[nm]) for nm in snames], [_as2d(vars_[nm]) for nm in snames], "adamw_small")
    for k, nm in enumerate(snames):
        shp = weights[nm].shape
        grads[nm] = small_grads[nm].reshape(shp)
        deltas[nm], new_m[nm], new_v[nm] = dls[k].reshape(shp), mos[k].reshape(shp), vos[k].reshape(shp)

    return (loss, grad_x[None], *[grads[nm] for nm in names], *[deltas[nm] for nm in names],
            *[new_m[nm] for nm in names], *[new_v[nm] for nm in names])
```

```python
import functools
import math

import jax
import jax.numpy as jnp
from jax import lax
from jax.experimental import pallas as pl
from jax.experimental.pallas import tpu as pltpu

F32 = jnp.float32
BF = jnp.bfloat16

N_DEV = 8
D = 1024
EPS = 1e-6
D_FF = 2816
FF_SHARD = 704
FF_PAD = 768
FF_MAIN = 640
FF_TAIL = FF_SHARD - FF_MAIN
W2_SHARD = 352
W_A = 512
W_B = 512
CONV_A = 4
CONV_B = 31
HALO_A = 8
HALO_B = 32
LRU_C = 8.0
CHUNK = 128
H_C = 8
ADAM_LR = 0.001
ADAM_B1 = 0.9
ADAM_B2 = 0.999
ADAM_EPS = 1e-08
ADAM_WD = 0.01
ADAM_STEP = 10
VMEM_LIMIT = 62 * 1024 * 1024
GELU_C = math.sqrt(2.0 / math.pi)

TM_FFN = 512
TM_FFN_BWD = 256
TM_MIX = 256
TM_MIXC_FWD = 512
TK_WGRAD = 2048


def _params(limit=VMEM_LIMIT):
    return pltpu.CompilerParams(dimension_semantics=("arbitrary",), vmem_limit_bytes=limit)


def _dot(a, b):
    return jnp.dot(a, b, preferred_element_type=F32)


def _dot_nt(a, b):
    return lax.dot_general(a, b, (((1,), (1,)), ((), ())), preferred_element_type=F32)


def _dot_tn(a, b):
    return lax.dot_general(a, b, (((0,), (0,)), ((), ())), preferred_element_type=F32)


def _sigmoid(x):
    return 0.5 + 0.5 * jnp.tanh(0.5 * x)


def _gelu(x):
    t = jnp.tanh(GELU_C * (x + 0.044715 * x * x * x))
    return 0.5 * x * (1.0 + t), t


def _gelu_grad(x, t):
    return 0.5 * (1.0 + t) + 0.5 * x * (1.0 - t * t) * GELU_C * (1.0 + 3.0 * 0.044715 * x * x)


def _rms(x):
    r = lax.rsqrt(jnp.mean(x * x, axis=-1, keepdims=True) + EPS)
    return x * r, r


def _colsum(x):
    return jnp.sum(x, axis=0, keepdims=True)


def _pre_fwd(x, vec_ref):
    xn, r = _rms(x)
    n = xn * vec_ref[3:4, :]
    h = n * (1.0 + vec_ref[1:2, :]) + vec_ref[0:1, :]
    return h, xn, r, n


def _post_fwd(x, f, vec_ref, res_w):
    fn, _ = _rms(f)
    return x + fn * ((res_w * (1.0 + vec_ref[2:3, :])) * vec_ref[4:5, :])


def _post_bwd(dout, f, vec_ref, acc_ref, res_w):
    fn, r2 = _rms(f)
    acc_ref[2:3, :] += _colsum(dout * fn)
    dfn = dout * ((res_w * (1.0 + vec_ref[2:3, :])) * vec_ref[4:5, :])
    return r2 * (dfn - fn * jnp.mean(dfn * fn, axis=-1, keepdims=True))


def _pre_bwd(dout, dh, xn, r, vec_ref, acc_ref):
    acc_ref[0:1, :] += _colsum(dh)
    acc_ref[1:2, :] += _colsum(dh * xn)
    dxn = dh * ((1.0 + vec_ref[1:2, :]) * vec_ref[3:4, :])
    return dout + r * (dxn - xn * jnp.mean(dxn * xn, axis=-1, keepdims=True))


def _finish_acc(acc_ref, vec_ref, res_w):
    s_pre, s_post = acc_ref[1:2, :], acc_ref[2:3, :]
    acc_ref[1:2, :] = vec_ref[3:4, :] * s_pre
    acc_ref[3:4, :] = (1.0 + vec_ref[1:2, :]) * s_pre
    acc_ref[2:3, :] = (res_w * vec_ref[4:5, :]) * s_post
    acc_ref[4:5, :] = (res_w * (1.0 + vec_ref[2:3, :])) * s_post


def _tile(tm, ncol):
    return pl.BlockSpec((tm, ncol), lambda i: (i, 0))


def _full(shape):
    return pl.BlockSpec(shape, lambda i: (0,) * len(shape))


def _any():
    return pl.BlockSpec(memory_space=pl.ANY)


def _load_ffn_weights(w13_hbm, w2_hbm, w13_v, w2_v, tails, sems):
    copies = []
    for j in range(N_DEV):
        half, k = divmod(j, 4)
        copies.append((w13_hbm.at[j, :, pl.ds(0, FF_MAIN)], w13_v.at[:, pl.ds(D_FF * half + FF_MAIN * k, FF_MAIN)]))
        copies.append((w13_hbm.at[j, :, pl.ds(FF_MAIN, 128)], tails.at[j]))
    for k in range(4):
        copies.append((w2_hbm.at[2 * k], w2_v.at[pl.ds(FF_MAIN * k, W2_SHARD), :]))
        copies.append((w2_hbm.at[2 * k + 1, pl.ds(0, FF_MAIN - W2_SHARD), :],
                       w2_v.at[pl.ds(FF_MAIN * k + W2_SHARD, FF_MAIN - W2_SHARD), :]))
        copies.append((w2_hbm.at[2 * k + 1, pl.ds(FF_MAIN - W2_SHARD, FF_TAIL), :],
                       w2_v.at[pl.ds(4 * FF_MAIN + FF_TAIL * k, FF_TAIL), :]))
    copies = [pltpu.make_async_copy(src, dst, sems.at[n]) for n, (src, dst) in enumerate(copies)]
    for cp in copies:
        cp.start()
    for cp in copies:
        cp.wait()
    for pair in range(4):
        half, kk = divmod(pair, 2)
        base = D_FF * half + 4 * FF_MAIN + 128 * kk
        w13_v[:, base:base + 128] = tails[2 * pair] + tails[2 * pair + 1]


_FFN_SCRATCH = [pltpu.VMEM((D, 2 * D_FF), BF), pltpu.VMEM((D_FF, D), BF), pltpu.VMEM((N_DEV, D, 128), BF),
                pltpu.SemaphoreType.DMA((2 * N_DEV + 12,))]
HID_CHUNKS = ((0, 768), (768, 768), (1536, 768), (2304, 512))


def _hosted(body, n_in, n_out, n_scratch, n_comm, mode, grid):
    if not n_comm:
        return body

    def at(corner):
        hit = pl.program_id(0) == corner[0]
        for d in range(1, len(grid)):
            hit = hit & (pl.program_id(d) == corner[d])
        return hit

    def hosted(*refs):
        ins, cin = refs[:n_in], refs[n_in:n_in + n_comm]
        outs, cout = refs[n_in + n_comm:n_in + n_comm + n_out], refs[n_in + n_comm + n_out:n_in + 2 * n_comm + n_out]
        scratch = refs[n_in + 2 * n_comm + n_out:]
        own, sems = scratch[:n_scratch], scratch[n_scratch:]

        @pl.when(at([0] * len(grid)))
        def _():
            _exchange_ops(cin, cout, sems, mode, "start")

        body(*ins, *outs, *own)

        @pl.when(at([n - 1 for n in grid]))
        def _():
            _exchange_ops(cin, cout, sems, mode, "wait")

    return hosted


def _ffn_fwd(x, vec, w13g, w2g, res_w, name, comm=(), tgt=None):
    t_len = x.shape[0]
    tm = min(TM_FFN, t_len)
    nc = len(comm)
    head = tgt is not None

    def body(*refs):
        if head:
            x_ref, vec_ref, w13_hbm, w2_hbm, t_ref, xo_ref, f_ref, jac_ref, s_ref, loss_ref, w13_v, w2_v, tails, sems = refs
        else:
            x_ref, vec_ref, w13_hbm, w2_hbm, xo_ref, f_ref, jac_ref, s_ref, w13_v, w2_v, tails, sems = refs

        @pl.when(pl.program_id(0) == 0)
        def _():
            _load_ffn_weights(w13_hbm, w2_hbm, w13_v, w2_v, tails, sems)
            if head:
                loss_ref[...] = jnp.zeros((8, 128), F32)

        x_t = x_ref[...]
        h, _, _, _ = _pre_fwd(x_t, vec_ref)
        hb = h.astype(BF)
        for c0, cw in HID_CHUNKS:
            g = _dot(hb, w13_v[:, c0:c0 + cw])
            u = _dot(hb, w13_v[:, D_FF + c0:D_FF + c0 + cw])
            sig = _sigmoid(g)
            sl = g * sig
            jac_ref[0, :, c0:c0 + cw] = (u * (sig + sl * (1.0 - sig))).astype(BF)
            jac_ref[1, :, c0:c0 + cw] = sl.astype(BF)
            s_ref[:, c0:c0 + cw] = (sl * u).astype(BF)
        acc = _dot(s_ref[...], w2_v[...])
        f_ref[...] = acc
        xo = _post_fwd(x_t, acc, vec_ref, res_w)
        if head:
            err = xo - t_ref[...]
            xo_ref[...] = err * (1.0 / D)
            loss_ref[...] += jnp.sum(err * err) * (0.5 / D)
        else:
            xo_ref[...] = xo

    nt = t_len // tm
    n_in, n_out = (5, 5) if head else (4, 4)
    outs = pl.pallas_call(
        _hosted(body, n_in, n_out, 4, nc, "gather", (nt,)), grid=(nt,),
        in_specs=[_tile(tm, D), _full((8, D)), _any(), _any()] + ([_tile(tm, D)] if head else []) + [_any()] * nc,
        out_specs=[_tile(tm, D), _tile(tm, D), pl.BlockSpec((2, tm, D_FF), lambda i: (0, i, 0)), _tile(tm, D_FF)]
        + ([_full((8, 128))] if head else []) + [_any()] * nc,
        out_shape=[jax.ShapeDtypeStruct((t_len, D), F32), jax.ShapeDtypeStruct((t_len, D), F32),
                   jax.ShapeDtypeStruct((2, t_len, D_FF), BF), jax.ShapeDtypeStruct((t_len, D_FF), BF)]
        + ([jax.ShapeDtypeStruct((8, 128), F32)] if head else []) + _exchange_shapes(comm, "gather"),
        scratch_shapes=_FFN_SCRATCH + (_exchange_scratch(nc) if nc else []), compiler_params=_params(), name=name,
    )(x, vec, w13g, w2g, *([tgt] if head else []), *comm)
    return outs[:n_out], outs[n_out:]


def _ffn_bwd(dout, x, fpre, jac, vec, w13g, w2g, res_w, name, comm=()):
    t_len = x.shape[0]
    tm = min(TM_FFN_BWD, t_len)
    nt = t_len // tm
    nc = len(comm)

    def body(dout_ref, x_ref, f_ref, jac_ref, vec_ref, w13_hbm, w2_hbm,
             dx_ref, dgu_ref, hb_ref, dfb_ref, acc_ref, w13_v, w2_v, tails, sems):
        @pl.when(pl.program_id(0) == 0)
        def _():
            _load_ffn_weights(w13_hbm, w2_hbm, w13_v, w2_v, tails, sems)
            acc_ref[...] = jnp.zeros((8, D), F32)

        dout_t = dout_ref[...]
        df = _post_bwd(dout_t, f_ref[...], vec_ref, acc_ref, res_w)
        dfb = df.astype(BF)
        dfb_ref[...] = dfb
        h, xn, r, _ = _pre_fwd(x_ref[...], vec_ref)
        hb_ref[...] = h.astype(BF)
        for c0, cw in HID_CHUNKS:
            ds = _dot_nt(dfb, w2_v[c0:c0 + cw, :])
            dgu_ref[:, c0:c0 + cw] = (ds * jac_ref[0, :, c0:c0 + cw].astype(F32)).astype(BF)
            dgu_ref[:, D_FF + c0:D_FF + c0 + cw] = (ds * jac_ref[1, :, c0:c0 + cw].astype(F32)).astype(BF)
        dh = _dot_nt(dgu_ref[...], w13_v[...])
        dx_ref[...] = _pre_bwd(dout_t, dh, xn, r, vec_ref, acc_ref)

        @pl.when(pl.program_id(0) == nt - 1)
        def _():
            _finish_acc(acc_ref, vec_ref, res_w)

    jac_spec = pl.BlockSpec((2, tm, D_FF), lambda i: (0, i, 0))
    outs = pl.pallas_call(
        _hosted(body, 7, 5, 4, nc, "scatter", (nt,)), grid=(nt,),
        in_specs=[_tile(tm, D), _tile(tm, D), _tile(tm, D), jac_spec, _full((8, D)), _any(), _any()] + [_any()] * nc,
        out_specs=[_tile(tm, D), _tile(tm, 2 * D_FF), _tile(tm, D), _tile(tm, D), _full((8, D))] + [_any()] * nc,
        out_shape=[jax.ShapeDtypeStruct((t_len, D), F32), jax.ShapeDtypeStruct((t_len, 2 * D_FF), BF),
                   jax.ShapeDtypeStruct((t_len, D), BF),
                   jax.ShapeDtypeStruct((t_len, D), BF), jax.ShapeDtypeStruct((8, D), F32)] + _exchange_shapes(comm, "scatter"),
        scratch_shapes=_FFN_SCRATCH + (_exchange_scratch(nc) if nc else []), compiler_params=_params(), name=name,
    )(dout, x, fpre, jac, vec, w13g, w2g, *comm)
    return outs[:5], outs[5:]


def _w13_slots(acc, o_ref):
    for k in range(4):
        o_ref[k, :, 0:FF_MAIN] = acc[:, FF_MAIN * k:FF_MAIN * (k + 1)].astype(BF)
        pair_tile = acc[:, 4 * FF_MAIN + 128 * (k // 2):4 * FF_MAIN + 128 * (k // 2 + 1)]
        o_ref[k, :, FF_MAIN:FF_PAD] = (pair_tile if k % 2 == 0 else pltpu.roll(pair_tile, FF_TAIL, 1)).astype(BF)


def _w2_slots(acc, o_ref):
    rest = FF_MAIN - W2_SHARD
    for k in range(4):
        o_ref[2 * k] = acc[FF_MAIN * k:FF_MAIN * k + W2_SHARD, :].astype(BF)
        o_ref[2 * k + 1, 0:rest, :] = acc[FF_MAIN * k + W2_SHARD:FF_MAIN * (k + 1), :].astype(BF)
        o_ref[2 * k + 1, rest:W2_SHARD, :] = acc[4 * FF_MAIN + FF_TAIL * k:4 * FF_MAIN + FF_TAIL * (k + 1), :].astype(BF)


def _wgrad(a, b, j_count, m, n, a_mode, b_mode, out_rows, out_dtype, name, tk=TK_WGRAD, col_slots=1, comm=(),
           slots=None):
    t_len = a.shape[-2]
    tk = min(tk, t_len)
    nk = t_len // tk
    wn = n // col_slots
    nc = len(comm)

    def spec(mode, width):
        if mode == "stack":
            return pl.BlockSpec((None, tk, width), lambda j, t: (j, t, 0))
        if mode == "cols":
            return pl.BlockSpec((tk, width), lambda j, t: (t, j))
        return pl.BlockSpec((tk, width), lambda j, t: (t, 0))

    def body(a_ref, b_ref, o_ref, acc):
        t = pl.program_id(1)

        @pl.when(t == 0)
        def _():
            acc[...] = jnp.zeros((m, n), F32)

        acc[...] += _dot_tn(a_ref[...], b_ref[...])

        @pl.when(t == nk - 1)
        def _():
            if slots is not None:
                slots[0](acc, o_ref)
            elif col_slots == 1:
                o_ref[...] = acc[0:out_rows, :].astype(out_dtype)
            else:
                for s in range(col_slots):
                    o_ref[s] = acc[0:out_rows, wn * s:wn * (s + 1)].astype(out_dtype)

    if slots is not None:
        blk = slots[1]
        out_spec = pl.BlockSpec(blk, lambda j, t: (j,) + (0,) * (len(blk) - 1))
        out_shape = jax.ShapeDtypeStruct((j_count * blk[0],) + blk[1:], BF)
    elif col_slots == 1:
        out_spec = pl.BlockSpec((None, out_rows, n), lambda j, t: (j, 0, 0))
        out_shape = jax.ShapeDtypeStruct((j_count, out_rows, n), out_dtype)
    else:
        out_spec = pl.BlockSpec((col_slots, out_rows, wn), lambda j, t: (0, 0, 0))
        out_shape = jax.ShapeDtypeStruct((col_slots, out_rows, wn), out_dtype)
    outs = pl.pallas_call(
        _hosted(body, 2, 1, 1, nc, "scatter", (j_count, nk)), grid=(j_count, nk),
        in_specs=[spec(a_mode, m), spec(b_mode, n)] + [_any()] * nc,
        out_specs=[out_spec] + [_any()] * nc, out_shape=[out_shape] + _exchange_shapes(comm, "scatter"),
        scratch_shapes=[pltpu.VMEM((m, n), F32)] + (_exchange_scratch(nc) if nc else []),
        compiler_params=pltpu.CompilerParams(dimension_semantics=("arbitrary", "arbitrary"), vmem_limit_bytes=VMEM_LIMIT),
        name=name,
    )(a, b, *comm)
    return (outs[0], list(outs[1:])) if nc else outs[0]


def _c_mask_weights(ws_ref, wsm, wsmt):
    row = lax.broadcasted_iota(jnp.int32, (CHUNK, CHUNK), 0)
    col = lax.broadcasted_iota(jnp.int32, (CHUNK, CHUNK), 1)
    for hh in range(H_C):
        w = jnp.where(row >= col, ws_ref[hh], 0.0)
        wsm[hh] = w.astype(BF)
        if wsmt is not None:
            wsmt[hh] = w.T.astype(BF)


def _c_inner(pre, cvec_ref, wsm, bst_ref, mix_sc, tm):
    z, t = _gelu(pre)
    u = z[:, 0:D]
    v = z[:, D:2 * D]
    mu = jnp.mean(v, axis=-1, keepdims=True)
    vc = v - mu
    rstd = lax.rsqrt(jnp.mean(vc * vc, axis=-1, keepdims=True) + EPS)
    vhat = vc * rstd
    vnb = (vhat * cvec_ref[0:1, :] + cvec_ref[1:2, :]).astype(BF)
    for nn in range(tm // CHUNK):
        for hh in range(H_C):
            rows = slice(CHUNK * nn, CHUNK * (nn + 1))
            cols = slice(CHUNK * hh, CHUNK * (hh + 1))
            mix_sc[rows, cols] = _dot(wsm[hh], vnb[rows, cols]) + bst_ref[:, hh:hh + 1]
    return u, t, rstd, vhat, vnb


def _mixc_fwd(x, vec, w_in, b_in, cvec, ws, bst, w_out, name):
    t_len = x.shape[0]
    tm = min(TM_MIXC_FWD, t_len)

    def body(x_ref, vec_ref, win_ref, bin_ref, cvec_ref, ws_ref, bst_ref, wout_ref,
             xo_ref, f_ref, pre_ref, wsm, mix_sc):
        @pl.when(pl.program_id(0) == 0)
        def _():
            _c_mask_weights(ws_ref, wsm, None)

        x_t = x_ref[...]
        h, _, _, _ = _pre_fwd(x_t, vec_ref)
        hb = h.astype(BF)
        for j in range(N_DEV):
            cols = slice(256 * j, 256 * (j + 1))
            pre_ref[:, cols] = _dot(hb, win_ref[j]) + bin_ref[:, cols]
        u, _, _, _, _ = _c_inner(pre_ref[...], cvec_ref, wsm, bst_ref, mix_sc, tm)
        fpre = _dot((u * mix_sc[...]).astype(BF), wout_ref[...])
        f_ref[...] = fpre
        xo_ref[...] = _post_fwd(x_t, fpre, vec_ref, 1.0)

    return pl.pallas_call(
        body, grid=(t_len // tm,),
        in_specs=[_tile(tm, D), _full((8, D)), _full((N_DEV, D, 256)), _full((1, 2 * D)), _full((8, D)),
                  _full((H_C, CHUNK, CHUNK)), _full((CHUNK, H_C)), _full((D, D))],
        out_specs=[_tile(tm, D), _tile(tm, D), _tile(tm, 2 * D)],
        out_shape=[jax.ShapeDtypeStruct((t_len, D), F32), jax.ShapeDtypeStruct((t_len, D), F32),
                   jax.ShapeDtypeStruct((t_len, 2 * D), F32)],
        scratch_shapes=[pltpu.VMEM((H_C, CHUNK, CHUNK), BF), pltpu.VMEM((tm, D), F32)],
        compiler_params=_params(), name=name,
    )(x, vec, w_in, b_in, cvec, ws, bst, w_out)


def _mixc_bwd(dout, x, fpre, pre, vec, w_in, cvec, ws, bst, w_out, name):
    t_len = x.shape[0]
    tm = min(TM_MIX, t_len)
    nt = t_len // tm

    def body(dout_ref, x_ref, f_ref, pre_ref, vec_ref, win_ref, cvec_ref, ws_ref, bst_ref, wout_ref,
             dx_ref, dpre_ref, p_ref, hb_ref, dfb_ref, acc_ref, dbin_ref, dws_ref, dbst_ref,
             wsm, wsmt, mix_sc, dvn_sc, dmsum, win_v):
        i = pl.program_id(0)

        @pl.when(i == 0)
        def _():
            _c_mask_weights(ws_ref, wsm, wsmt)
            for j in range(N_DEV):
                win_v[:, 256 * j:256 * (j + 1)] = win_ref[j]
            acc_ref[...] = jnp.zeros((8, D), F32)
            dbin_ref[...] = jnp.zeros((8, 2 * D), F32)
            dws_ref[...] = jnp.zeros((H_C, CHUNK, CHUNK), F32)
            dmsum[...] = jnp.zeros((CHUNK, D), F32)

        dout_t = dout_ref[...]
        df = _post_bwd(dout_t, f_ref[...], vec_ref, acc_ref, 1.0)
        dfb = df.astype(BF)
        dfb_ref[...] = dfb
        h, xn, r, n = _pre_fwd(x_ref[...], vec_ref)
        hb_ref[...] = h.astype(BF)
        pre_t = pre_ref[...]
        u, t, rstd, vhat, vnb = _c_inner(pre_t, cvec_ref, wsm, bst_ref, mix_sc, tm)
        mix = mix_sc[...]
        p_ref[...] = (u * mix).astype(BF)
        dp = _dot_nt(dfb, wout_ref[...])
        du = dp * mix
        dmix = dp * u
        dmb = dmix.astype(BF)
        for nn in range(tm // CHUNK):
            rows = slice(CHUNK * nn, CHUNK * (nn + 1))
            dmsum[...] += dmix[rows, :]
            for hh in range(H_C):
                cols = slice(CHUNK * hh, CHUNK * (hh + 1))
                dvn_sc[rows, cols] = _dot(wsmt[hh], dmb[rows, cols])
                dws_ref[hh] += _dot_nt(dmb[rows, cols], vnb[rows, cols])
        dvn = dvn_sc[...]
        acc_ref[5:6, :] += _colsum(dvn * vhat)
        acc_ref[6:7, :] += _colsum(dvn)
        dvhat = dvn * cvec_ref[0:1, :]
        dv = rstd * (dvhat - jnp.mean(dvhat, axis=-1, keepdims=True)
                     - vhat * jnp.mean(dvhat * vhat, axis=-1, keepdims=True))
        gg = _gelu_grad(pre_t, t)
        dpre_u = du * gg[:, 0:D]
        dpre_v = dv * gg[:, D:2 * D]
        dbin_ref[0:1, 0:D] += _colsum(dpre_u)
        dbin_ref[0:1, D:2 * D] += _colsum(dpre_v)
        dpre_ref[:, 0:D] = dpre_u.astype(BF)
        dpre_ref[:, D:2 * D] = dpre_v.astype(BF)
        dh = _dot_nt(dpre_ref[...], win_v[...])
        dx_ref[...] = _pre_bwd(dout_t, dh, xn, r, vec_ref, acc_ref)

        @pl.when(i == nt - 1)
        def _():
            _finish_acc(acc_ref, vec_ref, 1.0)
            row = lax.broadcasted_iota(jnp.int32, (CHUNK, CHUNK), 0)
            col = lax.broadcasted_iota(jnp.int32, (CHUNK, CHUNK), 1)
            for hh in range(H_C):
                dws_ref[hh] = jnp.where(row >= col, dws_ref[hh], 0.0)
                dbst_ref[:, hh:hh + 1] = jnp.sum(dmsum[:, CHUNK * hh:CHUNK * (hh + 1)], axis=1, keepdims=True)

    return pl.pallas_call(
        body, grid=(nt,),
        in_specs=[_tile(tm, D), _tile(tm, D), _tile(tm, D), _tile(tm, 2 * D), _full((8, D)), _full((N_DEV, D, 256)),
                  _full((8, D)), _full((H_C, CHUNK, CHUNK)), _full((CHUNK, H_C)), _full((D, D))],
        out_specs=[_tile(tm, D), _tile(tm, 2 * D), _tile(tm, D), _tile(tm, D), _tile(tm, D), _full((8, D)),
                   _full((8, 2 * D)), _full((H_C, CHUNK, CHUNK)), _full((CHUNK, H_C))],
        out_shape=[jax.ShapeDtypeStruct((t_len, D), F32), jax.ShapeDtypeStruct((t_len, 2 * D), BF),
                   jax.ShapeDtypeStruct((t_len, D), BF), jax.ShapeDtypeStruct((t_len, D), BF),
                   jax.ShapeDtypeStruct((t_len, D), BF), jax.ShapeDtypeStruct((8, D), F32),
                   jax.ShapeDtypeStruct((8, 2 * D), F32), jax.ShapeDtypeStruct((H_C, CHUNK, CHUNK), F32),
                   jax.ShapeDtypeStruct((CHUNK, H_C), F32)],
        scratch_shapes=[pltpu.VMEM((H_C, CHUNK, CHUNK), BF), pltpu.VMEM((H_C, CHUNK, CHUNK), BF),
                        pltpu.VMEM((tm, D), F32), pltpu.VMEM((tm, D), F32), pltpu.VMEM((CHUNK, D), F32),
                        pltpu.VMEM((D, 2 * D), BF)],
        compiler_params=_params(), name=name,
    )(dout, x, fpre, pre, vec, w_in, cvec, ws, bst, w_out)


def _gmean(x, g_ref):
    hi = x.astype(BF)
    lo = (x - hi.astype(F32)).astype(BF)
    return _dot(hi, g_ref[...]) + _dot(lo, g_ref[...])


def _log_sigmoid(lam):
    e = jnp.exp(-jnp.abs(lam))
    log1p = jnp.where(e < 1e-2, e * (1.0 - e * (0.5 - e * (1.0 / 3.0 - 0.25 * e))), jnp.log(1.0 + e))
    return jnp.minimum(lam, 0.0) - log1p


def _neg_expm1(y):
    series = -(y * (1.0 + y * (0.5 + y * (1.0 / 6.0 + y * (1.0 / 24.0 + y * (1.0 / 120.0))))))
    return jnp.where(y > -0.1, series, 1.0 - jnp.exp(y))


def _conv_causal(ext, taps_ref, bias, k_taps, halo, tm):
    acc = bias + taps_ref[0:1, :] * ext[halo - k_taps + 1:halo - k_taps + 1 + tm, :]
    for k in range(1, k_taps):
        off = halo - k_taps + 1 + k
        acc = acc + taps_ref[k:k + 1, :] * ext[off:off + tm, :]
    return acc


def _build_shifted(sh_ref, e, n_rows):
    sh_ref[0] = e
    for r in range(1, 8):
        sh_ref[r] = pltpu.roll(e, n_rows - r, 0)


def _shifted_rows(sh_ref, off, tm):
    base = off - off % 8
    return sh_ref[off % 8, base:base + tm, :]


def _scan(a, u, tm, reverse):
    row = lax.broadcasted_iota(jnp.int32, (tm, W_A), 0)
    d = 1
    while d < tm:
        if reverse:
            keep = row < tm - d
            shift = tm - d
        else:
            keep = row >= d
            shift = d
        a_sh = jnp.where(keep, pltpu.roll(a, shift, 0), 1.0)
        u_sh = jnp.where(keep, pltpu.roll(u, shift, 0), 0.0)
        u = a * u_sh + u
        a = a * a_sh
        d *= 2
    return a, u


def _a_gates(xc, cv_ref, wr_ref, wi_ref):
    xcb = xc.astype(BF)
    r = _sigmoid(_dot(xcb, wr_ref[...]) + cv_ref[5:6, :])
    ig = _sigmoid(_dot(xcb, wi_ref[...]) + cv_ref[6:7, :])
    ls = _log_sigmoid(cv_ref[7:8, :])
    la = LRU_C * r * ls
    a = jnp.exp(la)
    m = jnp.sqrt(_neg_expm1(2.0 * la))
    return xcb, r, ig, ls, a, m


def _b_norm(vc, cv_ref, g_ref):
    mu = _gmean(vc, g_ref)
    dv = vc - mu
    rstd = lax.rsqrt(_gmean(dv * dv, g_ref) + EPS)
    vhat = dv * rstd
    vln = vhat * cv_ref[9:10, :] + cv_ref[10:11, :]
    return rstd, vhat, vln


def _mixab_fwd(x, vec, w_in, cv, w31, wr, wi, gmat, w_out, name):
    t_len = x.shape[0]
    tm = min(TM_MIX, t_len)

    def body(x_ref, vec_ref, win_ref, cv_ref, w31_ref, wr_ref, wi_ref, g_ref, wout_ref,
             xo_ref, f_ref, z_ref, hs_ref, cvs_ref, ext_a, ext_b, hc, shifted):
        @pl.when(pl.program_id(0) == 0)
        def _():
            ext_a[0:HALO_A, :] = jnp.zeros((HALO_A, W_A), F32)
            ext_b[0:HALO_B, :] = jnp.zeros((HALO_B, W_B), F32)
            hc[...] = jnp.zeros((8, W_A), F32)

        x_t = x_ref[...]
        h, _, _, _ = _pre_fwd(x_t, vec_ref)
        hb = h.astype(BF)
        for j in range(N_DEV):
            z_ref[:, 256 * j:256 * (j + 1)] = _dot(hb, win_ref[j])
        ext_a[HALO_A:HALO_A + tm, :] = z_ref[:, W_A:2 * W_A]
        xc = _conv_causal(ext_a, cv_ref, cv_ref[4:5, :], CONV_A, HALO_A, tm)
        ext_a[0:HALO_A, :] = ext_a[tm:tm + HALO_A, :]
        cvs_ref[:, 0:W_A] = xc
        _, _, ig, _, a, m = _a_gates(xc, cv_ref, wr_ref, wi_ref)
        a_cum, hloc = _scan(a, m * ig * xc, tm, False)
        hs = hloc + a_cum * hc[0:1, :]
        hs_ref[...] = hs
        hc[0:1, :] = hs[tm - 1:tm, :]
        gel, _ = _gelu(z_ref[:, 0:W_A])
        ya = hs * gel
        ext_b[HALO_B:HALO_B + tm, :] = z_ref[:, 2 * W_A:2 * W_A + W_B] * _sigmoid(z_ref[:, 2 * W_A + W_B:2 * W_A + 2 * W_B])
        _build_shifted(shifted, ext_b[...], tm + HALO_B)
        vc = cv_ref[8:9, :] + w31_ref[0:1, :] * _shifted_rows(shifted, HALO_B - CONV_B + 1, tm)
        for k in range(1, CONV_B):
            vc = vc + w31_ref[k:k + 1, :] * _shifted_rows(shifted, HALO_B - CONV_B + 1 + k, tm)
        ext_b[0:HALO_B, :] = ext_b[tm:tm + HALO_B, :]
        cvs_ref[:, W_A:W_A + W_B] = vc
        _, _, vln = _b_norm(vc, cv_ref, g_ref)
        yb = vln * _sigmoid(vln)
        fpre = _dot(ya.astype(BF), wout_ref[0:W_A, :]) + _dot(yb.astype(BF), wout_ref[W_A:W_A + W_B, :])
        f_ref[...] = fpre
        xo_ref[...] = _post_fwd(x_t, fpre, vec_ref, 1.0)

    return pl.pallas_call(
        body, grid=(t_len // tm,),
        in_specs=[_tile(tm, D), _full((8, D)), _full((N_DEV, D, 256)), _full((16, W_A)), _full((32, W_B)),
                  _full((W_A, W_A)), _full((W_A, W_A)), _full((W_B, W_B)), _full((D, D))],
        out_specs=[_tile(tm, D), _tile(tm, D), _tile(tm, 2 * D), _tile(tm, W_A), _tile(tm, W_A + W_B)],
        out_shape=[jax.ShapeDtypeStruct((t_len, D), F32), jax.ShapeDtypeStruct((t_len, D), F32),
                   jax.ShapeDtypeStruct((t_len, 2 * D), F32), jax.ShapeDtypeStruct((t_len, W_A), F32),
                   jax.ShapeDtypeStruct((t_len, W_A + W_B), F32)],
        scratch_shapes=[pltpu.VMEM((tm + HALO_A, W_A), F32), pltpu.VMEM((tm + HALO_B, W_B), F32), pltpu.VMEM((8, W_A), F32),
                        pltpu.VMEM((8, tm + HALO_B, W_B), F32)],
        compiler_params=_params(), name=name,
    )(x, vec, w_in, cv, w31, wr, wi, gmat, w_out)


def _mixab_bwd(dout, x, fpre, z, cvs, hs, vec, w_in, cv, w31, wr, wi, gmat, w_out, name, comm=()):
    t_len = x.shape[0]
    tm = min(TM_MIX, t_len)
    nt = t_len // tm

    def rev(i):
        return nt - 1 - i

    def rtile(ncol):
        return pl.BlockSpec((tm, ncol), lambda i: (rev(i), 0))

    def body(dout_ref, x_ref, f_ref, z_ref, cvs_ref, hs_ref, hsp_ref, vec_ref, win_ref, cv_ref, w31_ref, wr_ref, wi_ref,
             g_ref, wout_ref,
             dx_ref, dz_ref, yab_ref, hb_ref, dfb_ref, xcb_ref, dri_ref, acc_ref, accs_ref, dw31_ref,
             ext_h, ext_dx, ext_dv, carry, shifted, win_v):
        i = pl.program_id(0)
        has_prev = (rev(i) > 0).astype(F32)

        @pl.when(i == 0)
        def _():
            for j in range(N_DEV):
                win_v[:, 256 * j:256 * (j + 1)] = win_ref[j]
            acc_ref[...] = jnp.zeros((8, D), F32)
            accs_ref[...] = jnp.zeros((16, W_A), F32)
            dw31_ref[...] = jnp.zeros((32, W_B), F32)
            ext_dx[tm:tm + HALO_A, :] = jnp.zeros((HALO_A, W_A), F32)
            ext_dv[tm:tm + HALO_B, :] = jnp.zeros((HALO_B, W_B), F32)
            carry[...] = jnp.zeros((8, W_A), F32)

        dout_t = dout_ref[...]
        df = _post_bwd(dout_t, f_ref[...], vec_ref, acc_ref, 1.0)
        dfb = df.astype(BF)
        dfb_ref[...] = dfb
        h, xn, r_x, n = _pre_fwd(x_ref[...], vec_ref)
        hb_ref[...] = h.astype(BF)

        ag = z_ref[:, 0:W_A]
        ax = z_ref[:, W_A:2 * W_A]
        bv = z_ref[:, 2 * W_A:2 * W_A + W_B]
        sg = _sigmoid(z_ref[:, 2 * W_A + W_B:2 * W_A + 2 * W_B])
        vv = bv * sg
        xc = cvs_ref[:, 0:W_A]
        xcb, r, ig, ls, a, m = _a_gates(xc, cv_ref, wr_ref, wi_ref)
        xcb_ref[...] = xcb
        hs_t = hs_ref[...]
        ext_h[0:8, :] = hsp_ref[...] * has_prev
        ext_h[8:8 + tm, :] = hs_t
        hprev = ext_h[7:7 + tm, :]
        gel, tg = _gelu(ag)
        rstd, vhat, vln = _b_norm(cvs_ref[:, W_A:W_A + W_B], cv_ref, g_ref)
        sv = _sigmoid(vln)
        yab_ref[:, 0:W_A] = (hs_t * gel).astype(BF)
        yab_ref[:, W_A:W_A + W_B] = (vln * sv).astype(BF)

        dya = _dot_nt(dfb, wout_ref[0:W_A, :])
        dyb = _dot_nt(dfb, wout_ref[W_A:W_A + W_B, :])

        dag = dya * hs_t * _gelu_grad(ag, tg)
        row = lax.broadcasted_iota(jnp.int32, (tm, W_A), 0)
        last = row == tm - 1
        a_next = jnp.where(last, 1.0, pltpu.roll(a, tm - 1, 0))
        u0 = dya * gel + jnp.where(last, carry[0:1, :], 0.0)
        _, dhs = _scan(a_next, u0, tm, True)
        carry[0:1, :] = a[0:1, :] * dhs[0:1, :]
        da = dhs * hprev
        dm = dhs * ig * xc
        di = dhs * m * xc
        dxc = dhs * m * ig
        dla = da * a - dm * (a * a) / m
        accs_ref[7:8, :] += _colsum(dla * r) * (LRU_C * _sigmoid(-cv_ref[7:8, :]))
        drp = (dla * (LRU_C * ls)) * r * (1.0 - r)
        dip = di * ig * (1.0 - ig)
        accs_ref[5:6, :] += _colsum(drp)
        accs_ref[6:7, :] += _colsum(dip)
        drpb = drp.astype(BF)
        dipb = dip.astype(BF)
        dri_ref[:, 0:W_A] = drpb
        dri_ref[:, W_A:2 * W_A] = dipb
        dxc = dxc + _dot_nt(drpb, wr_ref[...]) + _dot_nt(dipb, wi_ref[...])
        accs_ref[4:5, :] += _colsum(dxc)
        ext_dx[0:tm, :] = dxc
        dax = jnp.zeros((tm, W_A), F32)
        for k in range(CONV_A):
            ahead = ext_dx[CONV_A - 1 - k:CONV_A - 1 - k + tm, :]
            accs_ref[k:k + 1, :] += _colsum(ax * ahead)
            dax = dax + cv_ref[k:k + 1, :] * ahead
        ext_dx[tm:tm + HALO_A, :] = dxc[0:HALO_A, :]

        dvln = dyb * (sv * (1.0 + vln * (1.0 - sv)))
        accs_ref[9:10, :] += _colsum(dvln * vhat)
        accs_ref[10:11, :] += _colsum(dvln)
        dvhat = dvln * cv_ref[9:10, :]
        dvc = rstd * (dvhat - _gmean(dvhat, g_ref) - vhat * _gmean(dvhat * vhat, g_ref))
        accs_ref[8:9, :] += _colsum(dvc)
        ext_dv[0:tm, :] = dvc
        _build_shifted(shifted, ext_dv[...], tm + HALO_B)
        dvv = jnp.zeros((tm, W_B), F32)
        for k in range(CONV_B):
            ahead = _shifted_rows(shifted, CONV_B - 1 - k, tm)
            dw31_ref[k:k + 1, :] += _colsum(vv * ahead)
            dvv = dvv + w31_ref[k:k + 1, :] * ahead
        ext_dv[tm:tm + HALO_B, :] = dvc[0:HALO_B, :]

        dz_ref[:, 0:W_A] = dag.astype(BF)
        dz_ref[:, W_A:2 * W_A] = dax.astype(BF)
        dz_ref[:, 2 * W_A:2 * W_A + W_B] = (dvv * sg).astype(BF)
        dz_ref[:, 2 * W_A + W_B:2 * W_A + 2 * W_B] = (dvv * vv * (1.0 - sg)).astype(BF)
        dh = _dot_nt(dz_ref[...], win_v[...])
        dx_ref[...] = _pre_bwd(dout_t, dh, xn, r_x, vec_ref, acc_ref)

        @pl.when(i == nt - 1)
        def _():
            _finish_acc(acc_ref, vec_ref, 1.0)

    hsp_spec = pl.BlockSpec((8, W_A), lambda i: (jnp.maximum(rev(i) * (tm // 8) - 1, 0), 0))
    nc = len(comm)
    outs = pl.pallas_call(
        _hosted(body, 15, 10, 6, nc, "scatter", (nt,)), grid=(nt,),
        in_specs=[rtile(D), rtile(D), rtile(D), rtile(2 * D), rtile(W_A + W_B), rtile(W_A), hsp_spec, _full((8, D)),
                  _full((N_DEV, D, 256)), _full((16, W_A)), _full((32, W_B)), _full((W_A, W_A)), _full((W_A, W_A)),
                  _full((W_B, W_B)), _full((D, D))] + [_any()] * nc,
        out_specs=[rtile(D), rtile(2 * D), rtile(D), rtile(D), rtile(D), rtile(W_A), rtile(2 * W_A), _full((8, D)),
                   _full((16, W_A)), _full((32, W_B))] + [_any()] * nc,
        out_shape=[jax.ShapeDtypeStruct((t_len, D), F32), jax.ShapeDtypeStruct((t_len, 2 * D), BF),
                   jax.ShapeDtypeStruct((t_len, D), BF), jax.ShapeDtypeStruct((t_len, D), BF),
                   jax.ShapeDtypeStruct((t_len, D), BF), jax.ShapeDtypeStruct((t_len, W_A), BF),
                   jax.ShapeDtypeStruct((t_len, 2 * W_A), BF), jax.ShapeDtypeStruct((8, D), F32),
                   jax.ShapeDtypeStruct((16, W_A), F32), jax.ShapeDtypeStruct((32, W_B), F32)] + _exchange_shapes(comm, "scatter"),
        scratch_shapes=[pltpu.VMEM((tm + 8, W_A), F32), pltpu.VMEM((tm + HALO_A, W_A), F32),
                        pltpu.VMEM((tm + HALO_B, W_B), F32), pltpu.VMEM((8, W_A), F32),
                        pltpu.VMEM((8, tm + HALO_B, W_B), F32), pltpu.VMEM((D, 2 * D), BF)]
        + (_exchange_scratch(nc) if nc else []),
        compiler_params=_params(), name=name,
    )(dout, x, fpre, z, cvs, hs, hs, vec, w_in, cv, w31, wr, wi, gmat, w_out, *comm)
    return outs[:10], list(outs[10:])


def _vec(p, l, j):
    return jnp.concatenate([p["mod"][l, j], p["norm_pre"][l, j][None], p["norm_post"][l, j][None], jnp.zeros((3, D), F32)], 0)


def _ab_consts(p):
    gw = p["a_gate_w"]
    gb = p["a_gate_b"]
    half = W_A // 8
    eye = jnp.eye(8, dtype=F32)[:, None, :, None]

    def block_diag(blocks):
        return (blocks[:, :, None, :] * eye).reshape(W_A, W_A).astype(BF)

    wr = block_diag(gw[:, :, 0:half])
    wi = block_diag(gw[:, :, half:2 * half])
    rows = [p["a_conv_w"], p["a_conv_b"][None], gb[:, 0:half].reshape(1, W_A), gb[:, half:2 * half].reshape(1, W_A),
            p["a_lam"][None], p["b_conv_b"][None], p["b_norm_g"][None], p["b_norm_b"][None], jnp.zeros((5, W_A), F32)]
    cv = jnp.concatenate(rows, 0)
    w31 = jnp.concatenate([p["b_conv_w"], jnp.zeros((1, W_B), F32)], 0)
    grp = jnp.arange(W_B) // (W_B // 8)
    gmat = ((grp[:, None] == grp[None, :]).astype(F32) / (W_B // 8)).astype(BF)
    return cv, w31, wr, wi, gmat


SUBLAYERS = ("f0", "ab", "f1", "f2", "c", "f3")


def _local_step(x, tgt, p, plan=None):
    g = {}
    saved = []
    cur = x
    wsets = dict(p["wsets"])
    ab_c = _ab_consts(p)
    c_cvec = jnp.concatenate([p["c_norm_g"][None], p["c_norm_b"][None], jnp.zeros((6, D), F32)], 0)
    c_bst = jnp.transpose(p["c_b_s"])
    c_bin = p["c_b_in"][None]
    for s_idx, wname in enumerate(SUBLAYERS):
        l, j = divmod(s_idx, 3)
        vec = _vec(p, l, j)
        tag = f"l{l}s{j}"
        if j != 1:
            names = plan["gather"].get(s_idx, []) if plan else []
            comm = [a for nm in names for a in plan["shards"][nm]]
            res, got = _ffn_fwd(cur, vec, *wsets[wname], 0.5, "ffn_fwd_" + tag, comm=comm, tgt=tgt if s_idx == 5 else None)
            nxt, fpre, jac, s_act = res[:4]
            for k, nm in enumerate(names):
                wsets[nm] = list(got[2 * k:2 * k + 2])
            saved.append((cur, fpre, jac, s_act, vec))
            if s_idx == 5:
                loss_blk = res[4]
        elif l == 0:
            w_in, w_out = wsets[wname]
            nxt, fpre, z, hs, cvs = _mixab_fwd(cur, vec, w_in, *ab_c, w_out.reshape(D, D), "mixab_fwd_" + tag)
            saved.append((cur, fpre, z, hs, cvs, vec))
        else:
            w_in, w_out = wsets[wname]
            nxt, fpre, pre = _mixc_fwd(cur, vec, w_in, c_bin, c_cvec, p["c_w_s"], c_bst, w_out.reshape(D, D), "mixc_fwd_" + tag)
            saved.append((cur, fpre, pre, vec))
        cur = nxt
    dcur = cur
    accs, pending, recv = {}, {}, {}

    def take(host):
        keys = plan["scatter"].get(host, []) if plan else []
        return keys, [pending.pop(k) for k in keys]

    def put(keys, got):
        recv.update(zip(keys, got))

    for s_idx in reversed(range(6)):
        wname = SUBLAYERS[s_idx]
        l, j = divmod(s_idx, 3)
        tag = f"l{l}s{j}"
        sv = saved[s_idx]
        if j != 1:
            keys, comm = take("ffn_bwd_" + tag)
            xin, fpre, jac, s, vec = sv
            (dcur, dgu, hb, dfb, acc), got = _ffn_bwd(dcur, xin, fpre, jac, vec, *wsets[wname], 0.5, "ffn_bwd_" + tag, comm=comm)
            put(keys, got)
            keys, comm = take("wgrad_w13_" + tag)
            dw13 = _wgrad(hb, dgu, 2, D, D_FF, "share", "cols", D, BF, "wgrad_w13_" + tag, tk=TK_WGRAD, comm=comm,
                          slots=(_w13_slots, (4, D, FF_PAD)))
            if keys:
                dw13, got = dw13
                put(keys, got)
            pending[wname + ".0"] = dw13
            keys, comm = take("wgrad_w2_" + tag)
            dw2 = _wgrad(s, dfb, 1, D_FF, D, "share", "share", D_FF, BF, "wgrad_w2_" + tag, tk=TK_WGRAD, comm=comm,
                         slots=(_w2_slots, (N_DEV, W2_SHARD, D)))
            if keys:
                dw2, got = dw2
                put(keys, got)
            pending[wname + ".1"] = dw2
        elif l == 0:
            xin, fpre, z, hs, cvs, vec = sv
            w_in, w_out = wsets[wname]
            keys, comm = take("mixab_bwd_" + tag)
            (dcur, dz, yab, hb, dfb, xcb, dri, acc, accs_ab, dw31), got = _mixab_bwd(
                dcur, xin, fpre, z, cvs, hs, vec, w_in, *ab_c, w_out.reshape(D, D), "mixab_bwd_" + tag, comm=comm)
            put(keys, got)
            d_in = _wgrad(hb, dz, 1, D, 2 * D, "share", "share", D, BF, "wgrad_ab_in", tk=TK_WGRAD, col_slots=N_DEV)
            d_out = _wgrad(yab, dfb, 1, D, D, "share", "share", D, BF, "wgrad_ab_out")
            pending[wname + ".0"], pending[wname + ".1"] = d_in, d_out.reshape(N_DEV, D // N_DEV, D)
            g["gate"] = _wgrad(xcb, dri, 1, W_A, 2 * W_A, "share", "share", W_A, F32, "wgrad_gate")
            g["accs_ab"] = accs_ab
            g["dw31"] = dw31
        else:
            xin, fpre, pre, vec = sv
            w_in, w_out = wsets[wname]
            dcur, dpre, pb, hb, dfb, acc, dbin, dws, dbst = _mixc_bwd(
                dcur, xin, fpre, pre, vec, w_in, c_cvec, p["c_w_s"], c_bst, w_out.reshape(D, D), "mixc_bwd_" + tag)
            d_in = _wgrad(hb, dpre, 1, D, 2 * D, "share", "share", D, BF, "wgrad_c_in", tk=TK_WGRAD, col_slots=N_DEV)
            d_out = _wgrad(pb, dfb, 1, D, D, "share", "share", D, BF, "wgrad_c_out")
            pending[wname + ".0"], pending[wname + ".1"] = d_in, d_out.reshape(N_DEV, D // N_DEV, D)
            g["c_small"] = (acc, dbin, dws, dbst)
        accs[f"{l}{j}"] = acc
    g["accs"] = accs
    g["pending"] = pending
    g["recv"] = recv
    return loss_blk, dcur, g


def _exchange_ops(ins, outs, sems, mode, action):
    send_sems, recv_sems, loc_sems = sems
    n = len(ins)
    x, y, c = lax.axis_index("x"), lax.axis_index("y"), lax.axis_index("c")
    me = 4 * x + 2 * y + c

    def src(i, dev):
        return ins[i] if mode == "gather" else ins[i].at[dev]

    for i in range(n):
        cp = pltpu.make_async_copy(src(i, me), outs[i].at[me], loc_sems.at[i])
        if action == "start":
            cp.start()
        else:
            cp.wait()
    for mask in range(1, N_DEV):
        px = 1 - x if mask & 4 else x
        py = 1 - y if mask & 2 else y
        pc = 1 - c if mask & 1 else c
        peer = 4 * px + 2 * py + pc
        for i in range(n):
            k = i * (N_DEV - 1) + mask - 1
            cp = pltpu.make_async_remote_copy(
                src_ref=src(i, peer), dst_ref=outs[i].at[me if action == "start" else peer],
                send_sem=send_sems.at[k], recv_sem=recv_sems.at[k],
                device_id=(px, py, pc), device_id_type=pl.DeviceIdType.MESH)
            if action == "start":
                cp.start()
            else:
                cp.wait()


def _exchange_scratch(n):
    return [pltpu.SemaphoreType.DMA((n * (N_DEV - 1),)), pltpu.SemaphoreType.DMA((n * (N_DEV - 1),)),
            pltpu.SemaphoreType.DMA((n,))]


def _exchange_shapes(arrays, mode):
    return [jax.ShapeDtypeStruct(((N_DEV,) + a.shape) if mode == "gather" else a.shape, a.dtype) for a in arrays]


def _exchange(arrays, mode, name):
    n = len(arrays)

    def body(*refs):
        ins, outs, sems = refs[:n], refs[n:2 * n], refs[2 * n:]
        _exchange_ops(ins, outs, sems, mode, "start")
        _exchange_ops(ins, outs, sems, mode, "wait")

    return pl.pallas_call(
        body, in_specs=[pl.BlockSpec(memory_space=pl.ANY)] * n, out_specs=[pl.BlockSpec(memory_space=pl.ANY)] * n,
        out_shape=_exchange_shapes(arrays, mode), scratch_shapes=_exchange_scratch(n), name=name,
    )(*arrays)


def _gather_two_level(arrays, name):
    n = len(arrays)
    per = N_DEV - 1

    def body(*refs):
        ins, outs = refs[:n], refs[n:2 * n]
        send_sems, recv_sems, loc_sems = refs[2 * n:]
        x, y, c = lax.axis_index("x"), lax.axis_index("y"), lax.axis_index("c")
        me, sibling = (x, y, c), (x, y, 1 - c)
        chips = [(1 - x, y), (x, 1 - y), (1 - x, 1 - y)]

        def rows(i, dev):
            return outs[i].at[4 * dev[0] + 2 * dev[1] + dev[2]]

        def copy(i, k, block, to, src=None):
            return pltpu.make_async_remote_copy(
                src_ref=rows(i, block) if src is None else src, dst_ref=rows(i, block),
                send_sem=send_sems.at[i * per + k], recv_sem=recv_sems.at[i * per + k],
                device_id=to, device_id_type=pl.DeviceIdType.MESH)

        mine = [pltpu.make_async_copy(ins[i], rows(i, me), loc_sems.at[i]) for i in range(n)]
        for cp in mine:
            cp.start()
        first = []
        for i in range(n):
            first.append(copy(i, 0, me, sibling, src=ins[i]))
            first += [copy(i, 1 + j, me, (*chip, c), src=ins[i]) for j, chip in enumerate(chips)]
        for cp in first:
            cp.start()
        passed = []
        for j, chip in enumerate(chips):
            for i in range(n):
                copy(i, 1 + j, (*chip, c), me).wait_recv()
                fwd = copy(i, 4 + j, (*chip, c), sibling)
                fwd.start()
                passed.append(fwd)
        for i in range(n):
            copy(i, 0, sibling, me).wait_recv()
            for j, chip in enumerate(chips):
                copy(i, 4 + j, (*chip, 1 - c), me).wait_recv()
        for cp in first + passed:
            cp.wait_send()
        for cp in mine:
            cp.wait()

    return pl.pallas_call(
        body, in_specs=[pl.BlockSpec(memory_space=pl.ANY)] * n, out_specs=[pl.BlockSpec(memory_space=pl.ANY)] * n,
        out_shape=_exchange_shapes(arrays, "gather"), scratch_shapes=_exchange_scratch(n), name=name,
    )(*arrays)


def _sum_slots(a, name):
    def body(a_ref, o_ref):
        acc = a_ref[0]
        for s in range(1, N_DEV):
            acc = acc + a_ref[s]
        o_ref[...] = acc

    return pl.pallas_call(body, out_shape=jax.ShapeDtypeStruct(a.shape[1:], F32), name=name,
                          compiler_params=pltpu.CompilerParams(vmem_limit_bytes=VMEM_LIMIT))(a)


def _pack(pieces, mult):
    flat = jnp.concatenate([q.reshape(-1).astype(F32) for q in pieces])
    size = -(-flat.shape[0] // mult) * mult
    return jnp.pad(flat, (0, size - flat.shape[0])).reshape(size // 128, 128)


def _unpack(flat, shapes):
    out, off = [], 0
    for shp in shapes:
        size = math.prod(shp)
        out.append(flat[..., off:off + size].reshape(flat.shape[:-1] + tuple(shp)))
        off += size
    return out


def _mod_part(c_all, ada_w, ada_b_mine, name):
    cols = ada_w.shape[-1]

    def body(c_ref, w_ref, b_ref, o_ref):
        cv = c_ref[...]
        ca = cv * _sigmoid(cv)
        for l in range(2):
            o_ref[l] = jnp.dot(ca, w_ref[l], preferred_element_type=F32, precision=lax.Precision.HIGHEST) + b_ref[l:l + 1, :]

    return pl.pallas_call(body, out_shape=jax.ShapeDtypeStruct((2, N_DEV, cols), F32), name=name,
                          compiler_params=pltpu.CompilerParams(vmem_limit_bytes=VMEM_LIMIT))(c_all, ada_w, ada_b_mine)


def _ada_w_grad(c_all_t, dmod_mine, name):
    cols = dmod_mine.shape[-1]

    def body(ct_ref, d_ref, o_ref):
        cv = ct_ref[...]
        ca = cv * _sigmoid(cv)
        for l in range(2):
            acc = ca[:, 0:1] * d_ref[l, 0:1, :]
            for b in range(1, N_DEV):
                acc = acc + ca[:, b:b + 1] * d_ref[l, b:b + 1, :]
            o_ref[l] = acc

    return pl.pallas_call(body, out_shape=jax.ShapeDtypeStruct((2, D, cols), F32), name=name,
                          compiler_params=pltpu.CompilerParams(vmem_limit_bytes=VMEM_LIMIT))(c_all_t, dmod_mine)


def _adamw_math(w, g, m, v):
    m2 = ADAM_B1 * m + (1.0 - ADAM_B1) * g
    v2 = ADAM_B2 * v + (1.0 - ADAM_B2) * (g * g)
    m_hat = m2 / (1.0 - ADAM_B1 ** ADAM_STEP)
    v_hat = v2 / (1.0 - ADAM_B2 ** ADAM_STEP)
    delta = -ADAM_LR * (m_hat / (jnp.sqrt(v_hat) + ADAM_EPS) + ADAM_WD * w)
    return delta, m2, v2


def _adamw_big(w, g, m, v, name):
    n_l, rows, cols = w.shape
    parts = list(g) if isinstance(g, (list, tuple)) else None
    sizes = (512, 352, 256, 128, 64, 32, 16, 8) if parts is None or len(parts) == 1 else (128, 64, 32, 16, 8)
    br = next(b for b in sizes if rows % b == 0)
    nr = rows // br

    def body(*refs):
        w_ref, g_refs, (m_ref, v_ref, go_ref, d_ref, mo_ref, vo_ref) = refs[0], refs[1:-6], refs[-6:]

        def update(gsum):
            go_ref[...] = gsum
            d_ref[...], mo_ref[...], vo_ref[...] = _adamw_math(w_ref[...], gsum, m_ref[...], v_ref[...])

        if parts is None:
            update(g_refs[0][...])
        else:
            for l, g_ref in enumerate(g_refs):
                @pl.when(pl.program_id(0) == l)
                def _(g_ref=g_ref):
                    gsum = g_ref[0, :, 0:cols].astype(F32)
                    for s in range(1, N_DEV):
                        gsum = gsum + g_ref[s, :, 0:cols].astype(F32)
                    update(gsum)

    blk = pl.BlockSpec((None, br, cols), lambda l, i: (l, i, 0))
    if parts is None:
        g_specs, g_args = [blk], [g]
    else:
        def part_spec(l_mine, width):
            return pl.BlockSpec((N_DEV, br, width),
                                lambda l, i: (0, jnp.where(l < l_mine, 0, jnp.where(l == l_mine, i, nr - 1)), 0))
        g_specs, g_args = [part_spec(l, p.shape[-1]) for l, p in enumerate(parts)], parts
    shp = jax.ShapeDtypeStruct((n_l, rows, cols), F32)
    return pl.pallas_call(
        body, grid=(n_l, nr), in_specs=[blk] + g_specs + [blk, blk], out_specs=[blk] * 4, out_shape=[shp] * 4,
        compiler_params=pltpu.CompilerParams(dimension_semantics=("arbitrary", "arbitrary"), vmem_limit_bytes=VMEM_LIMIT),
        name=name)(w, *g_args, m, v)


def _adamw_small(ws, gs, ms, vs, name):
    n = len(ws)

    def body(*refs):
        for i in range(n):
            w_ref, g_ref, m_ref, v_ref = (refs[k * n + i] for k in range(4))
            d_ref, mo_ref, vo_ref = (refs[(4 + k) * n + i] for k in range(3))
            d_ref[...], mo_ref[...], vo_ref[...] = _adamw_math(w_ref[...], g_ref[...], m_ref[...], v_ref[...])

    shapes = [jax.ShapeDtypeStruct(w.shape, F32) for w in ws]
    outs = pl.pallas_call(body, out_shape=shapes * 3, name=name,
                          compiler_params=pltpu.CompilerParams(vmem_limit_bytes=VMEM_LIMIT))(*ws, *gs, *ms, *vs)
    return outs[:n], outs[n:2 * n], outs[2 * n:]


def _as2d(a):
    return a.reshape(-1, a.shape[-1])


def kernel(x, c, ada_w, ada_b, norm_pre, norm_post, ffn_w13, ffn_w2, ab_w_in, a_conv_w, a_conv_b, a_gate_w, a_gate_b, a_lam, b_conv_w, b_conv_b, b_norm_g, b_norm_b, ab_w_out, c_w_in, c_b_in, c_norm_g, c_norm_b, c_w_s, c_b_s, c_w_out, loss_target, m_ada_w, m_ada_b, m_norm_pre, m_norm_post, m_ffn_w13, m_ffn_w2, m_ab_w_in, m_a_conv_w, m_a_conv_b, m_a_gate_w, m_a_gate_b, m_a_lam, m_b_conv_w, m_b_conv_b, m_b_norm_g, m_b_norm_b, m_ab_w_out, m_c_w_in, m_c_b_in, m_c_norm_g, m_c_norm_b, m_c_w_s, m_c_b_s, m_c_w_out, v_ada_w, v_ada_b, v_norm_pre, v_norm_post, v_ffn_w13, v_ffn_w2, v_ab_w_in, v_a_conv_w, v_a_conv_b, v_a_gate_w, v_a_gate_b, v_a_lam, v_b_conv_w, v_b_conv_b, v_b_norm_g, v_b_norm_b, v_ab_w_out, v_c_w_in, v_c_b_in, v_c_norm_g, v_c_norm_b, v_c_w_s, v_c_b_s, v_c_w_out):
    me = 4 * lax.axis_index("x") + 2 * lax.axis_index("y") + lax.axis_index("c")
    weights = dict(ada_w=ada_w, ada_b=ada_b, norm_pre=norm_pre, norm_post=norm_post, ffn_w13=ffn_w13, ffn_w2=ffn_w2,
                   ab_w_in=ab_w_in, a_conv_w=a_conv_w, a_conv_b=a_conv_b, a_gate_w=a_gate_w, a_gate_b=a_gate_b, a_lam=a_lam,
                   b_conv_w=b_conv_w, b_conv_b=b_conv_b, b_norm_g=b_norm_g, b_norm_b=b_norm_b, ab_w_out=ab_w_out,
                   c_w_in=c_w_in, c_b_in=c_b_in, c_norm_g=c_norm_g, c_norm_b=c_norm_b, c_w_s=c_w_s, c_b_s=c_b_s, c_w_out=c_w_out)
    moms = dict(ada_w=m_ada_w, ada_b=m_ada_b, norm_pre=m_norm_pre, norm_post=m_norm_post, ffn_w13=m_ffn_w13, ffn_w2=m_ffn_w2,
                ab_w_in=m_ab_w_in, a_conv_w=m_a_conv_w, a_conv_b=m_a_conv_b, a_gate_w=m_a_gate_w, a_gate_b=m_a_gate_b,
                a_lam=m_a_lam, b_conv_w=m_b_conv_w, b_conv_b=m_b_conv_b, b_norm_g=m_b_norm_g, b_norm_b=m_b_norm_b,
                ab_w_out=m_ab_w_out, c_w_in=m_c_w_in, c_b_in=m_c_b_in, c_norm_g=m_c_norm_g, c_norm_b=m_c_norm_b,
                c_w_s=m_c_w_s, c_b_s=m_c_b_s, c_w_out=m_c_w_out)
    vars_ = dict(ada_w=v_ada_w, ada_b=v_ada_b, norm_pre=v_norm_pre, norm_post=v_norm_post, ffn_w13=v_ffn_w13, ffn_w2=v_ffn_w2,
                 ab_w_in=v_ab_w_in, a_conv_w=v_a_conv_w, a_conv_b=v_a_conv_b, a_gate_w=v_a_gate_w, a_gate_b=v_a_gate_b,
                 a_lam=v_a_lam, b_conv_w=v_b_conv_w, b_conv_b=v_b_conv_b, b_norm_g=v_b_norm_g, b_norm_b=v_b_norm_b,
                 ab_w_out=v_ab_w_out, c_w_in=v_c_w_in, c_b_in=v_c_b_in, c_norm_g=v_c_norm_g, c_norm_b=v_c_norm_b,
                 c_w_s=v_c_w_s, c_b_s=v_c_b_s, c_w_out=v_c_w_out)
    names = list(weights)

    w13s = ffn_w13.astype(BF).reshape(4, D, FF_SHARD)
    tail, blank = w13s[..., FF_MAIN:], jnp.zeros((4, D, FF_TAIL), BF)
    tail_tile = jnp.where(me % 2 == 1, jnp.concatenate([blank, tail], -1), jnp.concatenate([tail, blank], -1))
    w13b = jnp.concatenate([w13s[..., :FF_MAIN], tail_tile], -1)
    small_shapes = [(D,), (2, 3, 128), (2, 3, 128), (CONV_A, 64), (CONV_B, 64), (256,), (128,), (128,)]
    small = _pack([c, norm_pre, norm_post, a_conv_w, b_conv_w, c_b_in, c_norm_g, c_norm_b], 1024)
    w2b = ffn_w2.astype(BF).reshape(4, W2_SHARD, D)
    shards = {f"f{f}": [w13b[f], w2b[f]] for f in range(4)}
    shards["ab"] = [ab_w_in[0].astype(BF), ab_w_out[0].astype(BF)]
    shards["c"] = [c_w_in[0].astype(BF), c_w_out[0].astype(BF)]
    w13g0, w2g0, small_g = _gather_two_level(shards["f0"] + [small], "gather_first")
    plan = dict(shards=shards, gather={0: ["ab", "f1"], 2: ["f2"], 3: ["c", "f3"]},
                scatter={"ffn_bwd_l1s0": ["f3.0", "f3.1", "c.0", "c.1"], "ffn_bwd_l0s2": ["f2.0", "f2.1"],
                         "mixab_bwd_l0s1": ["f1.0", "f1.1"], "wgrad_w13_l0s0": ["ab.0", "ab.1"], "wgrad_w2_l0s0": ["f0.0"]})
    c_all, npre_g, npost_g, acw_g, bcw_g, cbin_g, cng_g, cnb_g = _unpack(small_g.reshape(N_DEV, -1), small_shapes)

    def cat_last(a):
        return jnp.moveaxis(a, 0, -2).reshape(a.shape[1:-1] + (N_DEV * a.shape[-1],))

    ada_b_mine = lax.dynamic_slice_in_dim(ada_b, me * ada_w.shape[-1], ada_w.shape[-1], axis=1)
    (mod_g,) = _exchange([_mod_part(c_all, ada_w, ada_b_mine, "mod_part")], "gather", "gather_mod")
    mod = cat_last(lax.dynamic_index_in_dim(mod_g, me, axis=2, keepdims=False)).reshape(2, 3, 3, D)

    p = dict(mod=mod, norm_pre=cat_last(npre_g), norm_post=cat_last(npost_g), wsets={"f0": [w13g0, w2g0]},
             a_conv_w=cat_last(acw_g), a_conv_b=a_conv_b[0], a_gate_w=a_gate_w[0], a_gate_b=a_gate_b[0], a_lam=a_lam[0],
             b_conv_w=cat_last(bcw_g), b_conv_b=b_conv_b[0], b_norm_g=b_norm_g[0], b_norm_b=b_norm_b[0],
             c_b_in=cat_last(cbin_g), c_norm_g=cat_last(cng_g), c_norm_b=cat_last(cnb_g), c_w_s=c_w_s[0], c_b_s=c_b_s[0])

    loss_blk, grad_x, g = _local_step(x[0], loss_target[0], p, plan)
    loss = lax.psum(loss_blk[0, 0], ("x", "y", "c"))

    accs = g["accs"]
    dmod = jnp.stack([jnp.stack([accs[f"{l}{j}"][0:3] for j in range(3)]) for l in range(2)])
    dnpre = jnp.stack([jnp.stack([accs[f"{l}{j}"][3] for j in range(3)]) for l in range(2)])
    dnpost = jnp.stack([jnp.stack([accs[f"{l}{j}"][4] for j in range(3)]) for l in range(2)])
    sab = g["accs_ab"]
    half = W_A // 8
    dgate = g["gate"][0]
    dgw = jnp.stack([jnp.concatenate([dgate[half * hh:half * (hh + 1), half * hh:half * (hh + 1)],
                                      dgate[half * hh:half * (hh + 1), W_A + half * hh:W_A + half * (hh + 1)]], axis=1)
                     for hh in range(8)])
    dgb = jnp.concatenate([sab[5].reshape(8, half), sab[6].reshape(8, half)], axis=1)
    c_acc, c_dbin, c_dws, c_dbst = g["c_small"]
    red_shapes = [(2, 9216), (2, 3, D), (2, 3, D), (CONV_A, W_A), (W_A,), (8, half, 2 * half), (8, 2 * half), (W_A,),
                  (CONV_B, W_B), (W_B,), (W_B,), (W_B,), (2 * D,), (D,), (D,), (H_C, CHUNK, CHUNK), (H_C, CHUNK)]
    red = _pack([dmod.reshape(2, 9216), dnpre, dnpost, sab[0:4], sab[4], dgw, dgb, sab[7], g["dw31"][0:CONV_B], sab[8],
                 sab[9], sab[10], c_dbin[0], c_acc[5], c_acc[6], c_dws, jnp.transpose(c_dbst)], N_DEV * 1024)
    (red_r,) = _exchange([red.reshape(N_DEV, -1, 128)], "scatter", "scatter_small_grads")
    red_all, dmod_all = _exchange([_sum_slots(red_r, "sum_small_grads"), dmod.reshape(-1, 128)], "gather", "gather_small_grads")
    red_sum = red_all.reshape(-1)
    (g_ada_b, g_npre, g_npost, g_acw, g_acb, g_agw, g_agb, g_alam, g_bcw, g_bcb, g_bng, g_bnb, g_cbin, g_cng, g_cnb,
     g_cws, g_cbs) = _unpack(red_sum, red_shapes)
    dmod_all = dmod_all.reshape(N_DEV, 2, 9216)
    ncol = ada_w.shape[-1]
    dmod_mine = jnp.moveaxis(lax.dynamic_slice_in_dim(dmod_all, me * ncol, ncol, axis=2), 0, 1)
    g_ada_w = _ada_w_grad(jnp.transpose(c_all), dmod_mine, "ada_w_grad")

    def mine(a, width):
        return lax.dynamic_slice_in_dim(a, me * width, width, axis=a.ndim - 1)

    small_grads = dict(
        ada_b=g_ada_b, norm_pre=mine(g_npre, 128), norm_post=mine(g_npost, 128), a_conv_w=mine(g_acw, 64)[None],
        a_conv_b=g_acb[None], a_gate_w=g_agw[None], a_gate_b=g_agb[None], a_lam=g_alam[None], b_conv_w=mine(g_bcw, 64)[None],
        b_conv_b=g_bcb[None], b_norm_g=g_bng[None], b_norm_b=g_bnb[None], c_b_in=mine(g_cbin, 256)[None],
        c_norm_g=mine(g_cng, 128)[None], c_norm_b=mine(g_cnb, 128)[None], c_w_s=g_cws[None], c_b_s=g_cbs[None])

    recv = dict(g["recv"])
    left = sorted(g["pending"])
    recv.update(zip(left, _exchange([g["pending"][k] for k in left], "scatter", "scatter_last")))
    big_partials = dict(ffn_w13=[recv[f"f{f}.0"] for f in range(4)], ffn_w2=[recv[f"f{f}.1"] for f in range(4)],
                        ab_w_in=[recv["ab.0"]], ab_w_out=[recv["ab.1"]], c_w_in=[recv["c.0"]], c_w_out=[recv["c.1"]],
                        ada_w=g_ada_w)

    grads, deltas, new_m, new_v = {}, {}, {}, {}

    def as3d(a):
        return a.reshape((-1,) + a.shape[-2:])

    for nm, gp in big_partials.items():
        shp = weights[nm].shape
        go, dl, mo, vo = _adamw_big(as3d(weights[nm]), gp, as3d(moms[nm]), as3d(vars_[nm]), "adamw_" + nm)
        grads[nm], deltas[nm], new_m[nm], new_v[nm] = (a.reshape(shp) for a in (go, dl, mo, vo))
    snames = list(small_grads)
    dls, mos, vos = _adamw_small([_as2d(weights[nm]) for nm in snames], [_as2d(small_grads[nm]) for nm in snames],
                                 [_as2d(moms[nm]) for nm in snames], [_as2d(vars_[nm]) for nm in snames], "adamw_small")
    for k, nm in enumerate(snames):
        shp = weights[nm].shape
        grads[nm] = small_grads[nm].reshape(shp)
        deltas[nm], new_m[nm], new_v[nm] = dls[k].reshape(shp), mos[k].reshape(shp), vos[k].reshape(shp)

    return (loss, grad_x[None], *[grads[nm] for nm in names], *[deltas[nm] for nm in names],
            *[new_m[nm] for nm in names], *[new_v[nm] for nm in names])
```

```python
import functools
import math

import jax
import jax.numpy as jnp
from jax import lax
from jax.experimental import pallas as pl
from jax.experimental.pallas import tpu as pltpu

F32 = jnp.float32
BF = jnp.bfloat16

N_DEV = 8
D = 1024
EPS = 1e-6
D_FF = 2816
FF_SHARD = 704
FF_PAD = 768
FF_MAIN = 640
FF_TAIL = FF_SHARD - FF_MAIN
W2_SHARD = 352
W_A = 512
W_B = 512
CONV_A = 4
CONV_B = 31
HALO_A = 8
HALO_B = 32
LRU_C = 8.0
CHUNK = 128
H_C = 8
ADAM_LR = 0.001
ADAM_B1 = 0.9
ADAM_B2 = 0.999
ADAM_EPS = 1e-08
ADAM_WD = 0.01
ADAM_STEP = 10
VMEM_LIMIT = 62 * 1024 * 1024
GELU_C = math.sqrt(2.0 / math.pi)

TM_FFN = 512
TM_FFN_BWD = 256
TM_MIX = 256
TM_MIXC_FWD = 512
TK_WGRAD = 2048


def _params(limit=VMEM_LIMIT):
    return pltpu.CompilerParams(dimension_semantics=("arbitrary",), vmem_limit_bytes=limit)


def _dot(a, b):
    return jnp.dot(a, b, preferred_element_type=F32)


def _dot_nt(a, b):
    return lax.dot_general(a, b, (((1,), (1,)), ((), ())), preferred_element_type=F32)


def _dot_tn(a, b):
    return lax.dot_general(a, b, (((0,), (0,)), ((), ())), preferred_element_type=F32)


def _sigmoid(x):
    return 0.5 + 0.5 * jnp.tanh(0.5 * x)


def _gelu(x):
    t = jnp.tanh(GELU_C * (x + 0.044715 * x * x * x))
    return 0.5 * x * (1.0 + t), t


def _gelu_grad(x, t):
    return 0.5 * (1.0 + t) + 0.5 * x * (1.0 - t * t) * GELU_C * (1.0 + 3.0 * 0.044715 * x * x)


def _rms(x):
    r = lax.rsqrt(jnp.mean(x * x, axis=-1, keepdims=True) + EPS)
    return x * r, r


def _colsum(x):
    return jnp.sum(x, axis=0, keepdims=True)


def _pre_fwd(x, vec_ref):
    xn, r = _rms(x)
    n = xn * vec_ref[3:4, :]
    h = n * (1.0 + vec_ref[1:2, :]) + vec_ref[0:1, :]
    return h, xn, r, n


def _post_fwd(x, f, vec_ref, res_w):
    fn, _ = _rms(f)
    return x + fn * ((res_w * (1.0 + vec_ref[2:3, :])) * vec_ref[4:5, :])


def _post_bwd(dout, f, vec_ref, acc_ref, res_w):
    fn, r2 = _rms(f)
    acc_ref[2:3, :] += _colsum(dout * fn)
    dfn = dout * ((res_w * (1.0 + vec_ref[2:3, :])) * vec_ref[4:5, :])
    return r2 * (dfn - fn * jnp.mean(dfn * fn, axis=-1, keepdims=True))


def _pre_bwd(dout, dh, xn, r, vec_ref, acc_ref):
    acc_ref[0:1, :] += _colsum(dh)
    acc_ref[1:2, :] += _colsum(dh * xn)
    dxn = dh * ((1.0 + vec_ref[1:2, :]) * vec_ref[3:4, :])
    return dout + r * (dxn - xn * jnp.mean(dxn * xn, axis=-1, keepdims=True))


def _finish_acc(acc_ref, vec_ref, res_w):
    s_pre, s_post = acc_ref[1:2, :], acc_ref[2:3, :]
    acc_ref[1:2, :] = vec_ref[3:4, :] * s_pre
    acc_ref[3:4, :] = (1.0 + vec_ref[1:2, :]) * s_pre
    acc_ref[2:3, :] = (res_w * vec_ref[4:5, :]) * s_post
    acc_ref[4:5, :] = (res_w * (1.0 + vec_ref[2:3, :])) * s_post


def _tile(tm, ncol):
    return pl.BlockSpec((tm, ncol), lambda i: (i, 0))


def _full(shape):
    return pl.BlockSpec(shape, lambda i: (0,) * len(shape))


def _any():
    return pl.BlockSpec(memory_space=pl.ANY)


def _load_ffn_weights(w13_hbm, w2_hbm, w13_v, w2_v, tails, sems):
    copies = []
    for j in range(N_DEV):
        half, k = divmod(j, 4)
        copies.append((w13_hbm.at[j, :, pl.ds(0, FF_MAIN)], w13_v.at[:, pl.ds(D_FF * half + FF_MAIN * k, FF_MAIN)]))
        copies.append((w13_hbm.at[j, :, pl.ds(FF_MAIN, 128)], tails.at[j]))
    for k in range(4):
        copies.append((w2_hbm.at[2 * k], w2_v.at[pl.ds(FF_MAIN * k, W2_SHARD), :]))
        copies.append((w2_hbm.at[2 * k + 1, pl.ds(0, FF_MAIN - W2_SHARD), :],
                       w2_v.at[pl.ds(FF_MAIN * k + W2_SHARD, FF_MAIN - W2_SHARD), :]))
        copies.append((w2_hbm.at[2 * k + 1, pl.ds(FF_MAIN - W2_SHARD, FF_TAIL), :],
                       w2_v.at[pl.ds(4 * FF_MAIN + FF_TAIL * k, FF_TAIL), :]))
    copies = [pltpu.make_async_copy(src, dst, sems.at[n]) for n, (src, dst) in enumerate(copies)]
    for cp in copies:
        cp.start()
    for cp in copies:
        cp.wait()
    for pair in range(4):
        half, kk = divmod(pair, 2)
        base = D_FF * half + 4 * FF_MAIN + 128 * kk
        w13_v[:, base:base + 128] = tails[2 * pair] + tails[2 * pair + 1]


_FFN_SCRATCH = [pltpu.VMEM((D, 2 * D_FF), BF), pltpu.VMEM((D_FF, D), BF), pltpu.VMEM((N_DEV, D, 128), BF),
                pltpu.SemaphoreType.DMA((2 * N_DEV + 12,))]
HID_CHUNKS = ((0, 768), (768, 768), (1536, 768), (2304, 512))


def _hosted(body, n_in, n_out, n_scratch, n_comm, mode, grid):
    if not n_comm:
        return body

    def at(corner):
        hit = pl.program_id(0) == corner[0]
        for d in range(1, len(grid)):
            hit = hit & (pl.program_id(d) == corner[d])
        return hit

    def hosted(*refs):
        ins, cin = refs[:n_in], refs[n_in:n_in + n_comm]
        outs, cout = refs[n_in + n_comm:n_in + n_comm + n_out], refs[n_in + n_comm + n_out:n_in + 2 * n_comm + n_out]
        scratch = refs[n_in + 2 * n_comm + n_out:]
        own, sems = scratch[:n_scratch], scratch[n_scratch:]

        @pl.when(at([0] * len(grid)))
        def _():
            _exchange_ops(cin, cout, sems, mode, "start")

        body(*ins, *outs, *own)

        @pl.when(at([n - 1 for n in grid]))
        def _():
            _exchange_ops(cin, cout, sems, mode, "wait")

    return hosted


def _ffn_fwd(x, vec, w13g, w2g, res_w, name, comm=(), tgt=None):
    t_len = x.shape[0]
    tm = min(TM_FFN, t_len)
    nc = len(comm)
    head = tgt is not None

    def body(*refs):
        if head:
            x_ref, vec_ref, w13_hbm, w2_hbm, t_ref, xo_ref, f_ref, jac_ref, s_ref, loss_ref, w13_v, w2_v, tails, sems = refs
        else:
            x_ref, vec_ref, w13_hbm, w2_hbm, xo_ref, f_ref, jac_ref, s_ref, w13_v, w2_v, tails, sems = refs

        @pl.when(pl.program_id(0) == 0)
        def _():
            _load_ffn_weights(w13_hbm, w2_hbm, w13_v, w2_v, tails, sems)
            if head:
                loss_ref[...] = jnp.zeros((8, 128), F32)

        x_t = x_ref[...]
        h, _, _, _ = _pre_fwd(x_t, vec_ref)
        hb = h.astype(BF)
        for c0, cw in HID_CHUNKS:
            g = _dot(hb, w13_v[:, c0:c0 + cw])
            u = _dot(hb, w13_v[:, D_FF + c0:D_FF + c0 + cw])
            sig = _sigmoid(g)
            sl = g * sig
            jac_ref[0, :, c0:c0 + cw] = (u * (sig + sl * (1.0 - sig))).astype(BF)
            jac_ref[1, :, c0:c0 + cw] = sl.astype(BF)
            s_ref[:, c0:c0 + cw] = (sl * u).astype(BF)
        acc = _dot(s_ref[...], w2_v[...])
        f_ref[...] = acc
        xo = _post_fwd(x_t, acc, vec_ref, res_w)
        if head:
            err = xo - t_ref[...]
            xo_ref[...] = err * (1.0 / D)
            loss_ref[...] += jnp.sum(err * err) * (0.5 / D)
        else:
            xo_ref[...] = xo

    nt = t_len // tm
    n_in, n_out = (5, 5) if head else (4, 4)
    outs = pl.pallas_call(
        _hosted(body, n_in, n_out, 4, nc, "gather", (nt,)), grid=(nt,),
        in_specs=[_tile(tm, D), _full((8, D)), _any(), _any()] + ([_tile(tm, D)] if head else []) + [_any()] * nc,
        out_specs=[_tile(tm, D), _tile(tm, D), pl.BlockSpec((2, tm, D_FF), lambda i: (0, i, 0)), _tile(tm, D_FF)]
        + ([_full((8, 128))] if head else []) + [_any()] * nc,
        out_shape=[jax.ShapeDtypeStruct((t_len, D), F32), jax.ShapeDtypeStruct((t_len, D), F32),
                   jax.ShapeDtypeStruct((2, t_len, D_FF), BF), jax.ShapeDtypeStruct((t_len, D_FF), BF)]
        + ([jax.ShapeDtypeStruct((8, 128), F32)] if head else []) + _exchange_shapes(comm, "gather"),
        scratch_shapes=_FFN_SCRATCH + (_exchange_scratch(nc) if nc else []), compiler_params=_params(), name=name,
    )(x, vec, w13g, w2g, *([tgt] if head else []), *comm)
    return outs[:n_out], outs[n_out:]


def _ffn_bwd(dout, x, fpre, jac, vec, w13g, w2g, res_w, name, comm=()):
    t_len = x.shape[0]
    tm = min(TM_FFN_BWD, t_len)
    nt = t_len // tm
    nc = len(comm)

    def body(dout_ref, x_ref, f_ref, jac_ref, vec_ref, w13_hbm, w2_hbm,
             dx_ref, dgu_ref, hb_ref, dfb_ref, acc_ref, w13_v, w2_v, tails, sems):
        @pl.when(pl.program_id(0) == 0)
        def _():
            _load_ffn_weights(w13_hbm, w2_hbm, w13_v, w2_v, tails, sems)
            acc_ref[...] = jnp.zeros((8, D), F32)

        dout_t = dout_ref[...]
        df = _post_bwd(dout_t, f_ref[...], vec_ref, acc_ref, res_w)
        dfb = df.astype(BF)
        dfb_ref[...] = dfb
        h, xn, r, _ = _pre_fwd(x_ref[...], vec_ref)
        hb_ref[...] = h.astype(BF)
        for c0, cw in HID_CHUNKS:
            ds = _dot_nt(dfb, w2_v[c0:c0 + cw, :]).astype(BF)
            dgu_ref[:, c0:c0 + cw] = ds * jac_ref[0, :, c0:c0 + cw]
            dgu_ref[:, D_FF + c0:D_FF + c0 + cw] = ds * jac_ref[1, :, c0:c0 + cw]
        dh = _dot_nt(dgu_ref[...], w13_v[...])
        dx_ref[...] = _pre_bwd(dout_t, dh, xn, r, vec_ref, acc_ref)

        @pl.when(pl.program_id(0) == nt - 1)
        def _():
            _finish_acc(acc_ref, vec_ref, res_w)

    jac_spec = pl.BlockSpec((2, tm, D_FF), lambda i: (0, i, 0))
    outs = pl.pallas_call(
        _hosted(body, 7, 5, 4, nc, "scatter", (nt,)), grid=(nt,),
        in_specs=[_tile(tm, D), _tile(tm, D), _tile(tm, D), jac_spec, _full((8, D)), _any(), _any()] + [_any()] * nc,
        out_specs=[_tile(tm, D), _tile(tm, 2 * D_FF), _tile(tm, D), _tile(tm, D), _full((8, D))] + [_any()] * nc,
        out_shape=[jax.ShapeDtypeStruct((t_len, D), F32), jax.ShapeDtypeStruct((t_len, 2 * D_FF), BF),
                   jax.ShapeDtypeStruct((t_len, D), BF),
                   jax.ShapeDtypeStruct((t_len, D), BF), jax.ShapeDtypeStruct((8, D), F32)] + _exchange_shapes(comm, "scatter"),
        scratch_shapes=_FFN_SCRATCH + (_exchange_scratch(nc) if nc else []), compiler_params=_params(), name=name,
    )(dout, x, fpre, jac, vec, w13g, w2g, *comm)
    return outs[:5], outs[5:]


def _w13_slots(acc, o_ref):
    for k in range(4):
        o_ref[k, :, 0:FF_MAIN] = acc[:, FF_MAIN * k:FF_MAIN * (k + 1)].astype(BF)
        pair_tile = acc[:, 4 * FF_MAIN + 128 * (k // 2):4 * FF_MAIN + 128 * (k // 2 + 1)]
        o_ref[k, :, FF_MAIN:FF_PAD] = (pair_tile if k % 2 == 0 else pltpu.roll(pair_tile, FF_TAIL, 1)).astype(BF)


def _w2_slots(acc, o_ref):
    rest = FF_MAIN - W2_SHARD
    for k in range(4):
        o_ref[2 * k] = acc[FF_MAIN * k:FF_MAIN * k + W2_SHARD, :].astype(BF)
        o_ref[2 * k + 1, 0:rest, :] = acc[FF_MAIN * k + W2_SHARD:FF_MAIN * (k + 1), :].astype(BF)
        o_ref[2 * k + 1, rest:W2_SHARD, :] = acc[4 * FF_MAIN + FF_TAIL * k:4 * FF_MAIN + FF_TAIL * (k + 1), :].astype(BF)


def _wgrad(a, b, j_count, m, n, a_mode, b_mode, out_rows, out_dtype, name, tk=TK_WGRAD, col_slots=1, comm=(),
           slots=None):
    t_len = a.shape[-2]
    tk = min(tk, t_len)
    nk = t_len // tk
    wn = n // col_slots
    nc = len(comm)

    def spec(mode, width):
        if mode == "stack":
            return pl.BlockSpec((None, tk, width), lambda j, t: (j, t, 0))
        if mode == "cols":
            return pl.BlockSpec((tk, width), lambda j, t: (t, j))
        return pl.BlockSpec((tk, width), lambda j, t: (t, 0))

    def body(a_ref, b_ref, o_ref, acc):
        t = pl.program_id(1)

        @pl.when(t == 0)
        def _():
            acc[...] = jnp.zeros((m, n), F32)

        acc[...] += _dot_tn(a_ref[...], b_ref[...])

        @pl.when(t == nk - 1)
        def _():
            if slots is not None:
                slots[0](acc, o_ref)
            elif col_slots == 1:
                o_ref[...] = acc[0:out_rows, :].astype(out_dtype)
            else:
                for s in range(col_slots):
                    o_ref[s] = acc[0:out_rows, wn * s:wn * (s + 1)].astype(out_dtype)

    if slots is not None:
        blk = slots[1]
        out_spec = pl.BlockSpec(blk, lambda j, t: (j,) + (0,) * (len(blk) - 1))
        out_shape = jax.ShapeDtypeStruct((j_count * blk[0],) + blk[1:], BF)
    elif col_slots == 1:
        out_spec = pl.BlockSpec((None, out_rows, n), lambda j, t: (j, 0, 0))
        out_shape = jax.ShapeDtypeStruct((j_count, out_rows, n), out_dtype)
    else:
        out_spec = pl.BlockSpec((col_slots, out_rows, wn), lambda j, t: (0, 0, 0))
        out_shape = jax.ShapeDtypeStruct((col_slots, out_rows, wn), out_dtype)
    outs = pl.pallas_call(
        _hosted(body, 2, 1, 1, nc, "scatter", (j_count, nk)), grid=(j_count, nk),
        in_specs=[spec(a_mode, m), spec(b_mode, n)] + [_any()] * nc,
        out_specs=[out_spec] + [_any()] * nc, out_shape=[out_shape] + _exchange_shapes(comm, "scatter"),
        scratch_shapes=[pltpu.VMEM((m, n), F32)] + (_exchange_scratch(nc) if nc else []),
        compiler_params=pltpu.CompilerParams(dimension_semantics=("arbitrary", "arbitrary"), vmem_limit_bytes=VMEM_LIMIT),
        name=name,
    )(a, b, *comm)
    return (outs[0], list(outs[1:])) if nc else outs[0]


def _c_mask_weights(ws_ref, wsm, wsmt):
    row = lax.broadcasted_iota(jnp.int32, (CHUNK, CHUNK), 0)
    col = lax.broadcasted_iota(jnp.int32, (CHUNK, CHUNK), 1)
    for hh in range(H_C):
        w = jnp.where(row >= col, ws_ref[hh], 0.0)
        wsm[hh] = w.astype(BF)
        if wsmt is not None:
            wsmt[hh] = w.T.astype(BF)


def _c_inner(pre, cvec_ref, wsm, bst_ref, mix_sc, tm):
    z, t = _gelu(pre)
    u = z[:, 0:D]
    v = z[:, D:2 * D]
    mu = jnp.mean(v, axis=-1, keepdims=True)
    vc = v - mu
    rstd = lax.rsqrt(jnp.mean(vc * vc, axis=-1, keepdims=True) + EPS)
    vhat = vc * rstd
    vnb = (vhat * cvec_ref[0:1, :] + cvec_ref[1:2, :]).astype(BF)
    for nn in range(tm // CHUNK):
        for hh in range(H_C):
            rows = slice(CHUNK * nn, CHUNK * (nn + 1))
            cols = slice(CHUNK * hh, CHUNK * (hh + 1))
            mix_sc[rows, cols] = _dot(wsm[hh], vnb[rows, cols]) + bst_ref[:, hh:hh + 1]
    return u, t, rstd, vhat, vnb


def _mixc_fwd(x, vec, w_in, b_in, cvec, ws, bst, w_out, name):
    t_len = x.shape[0]
    tm = min(TM_MIXC_FWD, t_len)

    def body(x_ref, vec_ref, win_ref, bin_ref, cvec_ref, ws_ref, bst_ref, wout_ref,
             xo_ref, f_ref, pre_ref, wsm, mix_sc):
        @pl.when(pl.program_id(0) == 0)
        def _():
            _c_mask_weights(ws_ref, wsm, None)

        x_t = x_ref[...]
        h, _, _, _ = _pre_fwd(x_t, vec_ref)
        hb = h.astype(BF)
        for j in range(N_DEV):
            cols = slice(256 * j, 256 * (j + 1))
            pre_ref[:, cols] = _dot(hb, win_ref[j]) + bin_ref[:, cols]
        u, _, _, _, _ = _c_inner(pre_ref[...], cvec_ref, wsm, bst_ref, mix_sc, tm)
        fpre = _dot((u * mix_sc[...]).astype(BF), wout_ref[...])
        f_ref[...] = fpre
        xo_ref[...] = _post_fwd(x_t, fpre, vec_ref, 1.0)

    return pl.pallas_call(
        body, grid=(t_len // tm,),
        in_specs=[_tile(tm, D), _full((8, D)), _full((N_DEV, D, 256)), _full((1, 2 * D)), _full((8, D)),
                  _full((H_C, CHUNK, CHUNK)), _full((CHUNK, H_C)), _full((D, D))],
        out_specs=[_tile(tm, D), _tile(tm, D), _tile(tm, 2 * D)],
        out_shape=[jax.ShapeDtypeStruct((t_len, D), F32), jax.ShapeDtypeStruct((t_len, D), F32),
                   jax.ShapeDtypeStruct((t_len, 2 * D), F32)],
        scratch_shapes=[pltpu.VMEM((H_C, CHUNK, CHUNK), BF), pltpu.VMEM((tm, D), F32)],
        compiler_params=_params(), name=name,
    )(x, vec, w_in, b_in, cvec, ws, bst, w_out)


def _mixc_bwd(dout, x, fpre, pre, vec, w_in, cvec, ws, bst, w_out, name):
    t_len = x.shape[0]
    tm = min(TM_MIX, t_len)
    nt = t_len // tm

    def body(dout_ref, x_ref, f_ref, pre_ref, vec_ref, win_ref, cvec_ref, ws_ref, bst_ref, wout_ref,
             dx_ref, dpre_ref, p_ref, hb_ref, dfb_ref, acc_ref, dbin_ref, dws_ref, dbst_ref,
             wsm, wsmt, mix_sc, dvn_sc, dmsum, win_v):
        i = pl.program_id(0)

        @pl.when(i == 0)
        def _():
            _c_mask_weights(ws_ref, wsm, wsmt)
            for j in range(N_DEV):
                win_v[:, 256 * j:256 * (j + 1)] = win_ref[j]
            acc_ref[...] = jnp.zeros((8, D), F32)
            dbin_ref[...] = jnp.zeros((8, 2 * D), F32)
            dws_ref[...] = jnp.zeros((H_C, CHUNK, CHUNK), F32)
            dmsum[...] = jnp.zeros((CHUNK, D), F32)

        dout_t = dout_ref[...]
        df = _post_bwd(dout_t, f_ref[...], vec_ref, acc_ref, 1.0)
        dfb = df.astype(BF)
        dfb_ref[...] = dfb
        h, xn, r, n = _pre_fwd(x_ref[...], vec_ref)
        hb_ref[...] = h.astype(BF)
        pre_t = pre_ref[...]
        u, t, rstd, vhat, vnb = _c_inner(pre_t, cvec_ref, wsm, bst_ref, mix_sc, tm)
        mix = mix_sc[...]
        p_ref[...] = (u * mix).astype(BF)
        dp = _dot_nt(dfb, wout_ref[...])
        du = dp * mix
        dmix = dp * u
        dmb = dmix.astype(BF)
        for nn in range(tm // CHUNK):
            rows = slice(CHUNK * nn, CHUNK * (nn + 1))
            dmsum[...] += dmix[rows, :]
            for hh in range(H_C):
                cols = slice(CHUNK * hh, CHUNK * (hh + 1))
                dvn_sc[rows, cols] = _dot(wsmt[hh], dmb[rows, cols])
                dws_ref[hh] += _dot_nt(dmb[rows, cols], vnb[rows, cols])
        dvn = dvn_sc[...]
        acc_ref[5:6, :] += _colsum(dvn * vhat)
        acc_ref[6:7, :] += _colsum(dvn)
        dvhat = dvn * cvec_ref[0:1, :]
        dv = rstd * (dvhat - jnp.mean(dvhat, axis=-1, keepdims=True)
                     - vhat * jnp.mean(dvhat * vhat, axis=-1, keepdims=True))
        gg = _gelu_grad(pre_t, t)
        dpre_u = du * gg[:, 0:D]
        dpre_v = dv * gg[:, D:2 * D]
        dbin_ref[0:1, 0:D] += _colsum(dpre_u)
        dbin_ref[0:1, D:2 * D] += _colsum(dpre_v)
        dpre_ref[:, 0:D] = dpre_u.astype(BF)
        dpre_ref[:, D:2 * D] = dpre_v.astype(BF)
        dh = _dot_nt(dpre_ref[...], win_v[...])
        dx_ref[...] = _pre_bwd(dout_t, dh, xn, r, vec_ref, acc_ref)

        @pl.when(i == nt - 1)
        def _():
            _finish_acc(acc_ref, vec_ref, 1.0)
            row = lax.broadcasted_iota(jnp.int32, (CHUNK, CHUNK), 0)
            col = lax.broadcasted_iota(jnp.int32, (CHUNK, CHUNK), 1)
            for hh in range(H_C):
                dws_ref[hh] = jnp.where(row >= col, dws_ref[hh], 0.0)
                dbst_ref[:, hh:hh + 1] = jnp.sum(dmsum[:, CHUNK * hh:CHUNK * (hh + 1)], axis=1, keepdims=True)

    return pl.pallas_call(
        body, grid=(nt,),
        in_specs=[_tile(tm, D), _tile(tm, D), _tile(tm, D), _tile(tm, 2 * D), _full((8, D)), _full((N_DEV, D, 256)),
                  _full((8, D)), _full((H_C, CHUNK, CHUNK)), _full((CHUNK, H_C)), _full((D, D))],
        out_specs=[_tile(tm, D), _tile(tm, 2 * D), _tile(tm, D), _tile(tm, D), _tile(tm, D), _full((8, D)),
                   _full((8, 2 * D)), _full((H_C, CHUNK, CHUNK)), _full((CHUNK, H_C))],
        out_shape=[jax.ShapeDtypeStruct((t_len, D), F32), jax.ShapeDtypeStruct((t_len, 2 * D), BF),
                   jax.ShapeDtypeStruct((t_len, D), BF), jax.ShapeDtypeStruct((t_len, D), BF),
                   jax.ShapeDtypeStruct((t_len, D), BF), jax.ShapeDtypeStruct((8, D), F32),
                   jax.ShapeDtypeStruct((8, 2 * D), F32), jax.ShapeDtypeStruct((H_C, CHUNK, CHUNK), F32),
                   jax.ShapeDtypeStruct((CHUNK, H_C), F32)],
        scratch_shapes=[pltpu.VMEM((H_C, CHUNK, CHUNK), BF), pltpu.VMEM((H_C, CHUNK, CHUNK), BF),
                        pltpu.VMEM((tm, D), F32), pltpu.VMEM((tm, D), F32), pltpu.VMEM((CHUNK, D), F32),
                        pltpu.VMEM((D, 2 * D), BF)],
        compiler_params=_params(), name=name,
    )(dout, x, fpre, pre, vec, w_in, cvec, ws, bst, w_out)


def _gmean(x, g_ref):
    hi = x.astype(BF)
    lo = (x - hi.astype(F32)).astype(BF)
    return _dot(hi, g_ref[...]) + _dot(lo, g_ref[...])


def _log_sigmoid(lam):
    e = jnp.exp(-jnp.abs(lam))
    log1p = jnp.where(e < 1e-2, e * (1.0 - e * (0.5 - e * (1.0 / 3.0 - 0.25 * e))), jnp.log(1.0 + e))
    return jnp.minimum(lam, 0.0) - log1p


def _neg_expm1(y):
    series = -(y * (1.0 + y * (0.5 + y * (1.0 / 6.0 + y * (1.0 / 24.0 + y * (1.0 / 120.0))))))
    return jnp.where(y > -0.1, series, 1.0 - jnp.exp(y))


def _conv_causal(ext, taps_ref, bias, k_taps, halo, tm):
    acc = bias + taps_ref[0:1, :] * ext[halo - k_taps + 1:halo - k_taps + 1 + tm, :]
    for k in range(1, k_taps):
        off = halo - k_taps + 1 + k
        acc = acc + taps_ref[k:k + 1, :] * ext[off:off + tm, :]
    return acc


def _build_shifted(sh_ref, e, n_rows):
    sh_ref[0] = e
    for r in range(1, 8):
        sh_ref[r] = pltpu.roll(e, n_rows - r, 0)


def _shifted_rows(sh_ref, off, tm):
    base = off - off % 8
    return sh_ref[off % 8, base:base + tm, :]


def _scan(a, u, tm, reverse):
    row = lax.broadcasted_iota(jnp.int32, (tm, W_A), 0)
    d = 1
    while d < tm:
        if reverse:
            keep = row < tm - d
            shift = tm - d
        else:
            keep = row >= d
            shift = d
        a_sh = jnp.where(keep, pltpu.roll(a, shift, 0), 1.0)
        u_sh = jnp.where(keep, pltpu.roll(u, shift, 0), 0.0)
        u = a * u_sh + u
        a = a * a_sh
        d *= 2
    return a, u


def _a_gates(xc, cv_ref, wr_ref, wi_ref):
    xcb = xc.astype(BF)
    r = _sigmoid(_dot(xcb, wr_ref[...]) + cv_ref[5:6, :])
    ig = _sigmoid(_dot(xcb, wi_ref[...]) + cv_ref[6:7, :])
    ls = _log_sigmoid(cv_ref[7:8, :])
    la = LRU_C * r * ls
    a = jnp.exp(la)
    m = jnp.sqrt(_neg_expm1(2.0 * la))
    return xcb, r, ig, ls, a, m


def _b_norm(vc, cv_ref, g_ref):
    mu = _gmean(vc, g_ref)
    dv = vc - mu
    rstd = lax.rsqrt(_gmean(dv * dv, g_ref) + EPS)
    vhat = dv * rstd
    vln = vhat * cv_ref[9:10, :] + cv_ref[10:11, :]
    return rstd, vhat, vln


def _mixab_fwd(x, vec, w_in, cv, w31, wr, wi, gmat, w_out, name):
    t_len = x.shape[0]
    tm = min(TM_MIX, t_len)

    def body(x_ref, vec_ref, win_ref, cv_ref, w31_ref, wr_ref, wi_ref, g_ref, wout_ref,
             xo_ref, f_ref, z_ref, hs_ref, cvs_ref, ext_a, ext_b, hc, shifted):
        @pl.when(pl.program_id(0) == 0)
        def _():
            ext_a[0:HALO_A, :] = jnp.zeros((HALO_A, W_A), F32)
            ext_b[0:HALO_B, :] = jnp.zeros((HALO_B, W_B), F32)
            hc[...] = jnp.zeros((8, W_A), F32)

        x_t = x_ref[...]
        h, _, _, _ = _pre_fwd(x_t, vec_ref)
        hb = h.astype(BF)
        for j in range(N_DEV):
            z_ref[:, 256 * j:256 * (j + 1)] = _dot(hb, win_ref[j])
        ext_a[HALO_A:HALO_A + tm, :] = z_ref[:, W_A:2 * W_A]
        xc = _conv_causal(ext_a, cv_ref, cv_ref[4:5, :], CONV_A, HALO_A, tm)
        ext_a[0:HALO_A, :] = ext_a[tm:tm + HALO_A, :]
        cvs_ref[:, 0:W_A] = xc
        _, _, ig, _, a, m = _a_gates(xc, cv_ref, wr_ref, wi_ref)
        a_cum, hloc = _scan(a, m * ig * xc, tm, False)
        hs = hloc + a_cum * hc[0:1, :]
        hs_ref[...] = hs
        hc[0:1, :] = hs[tm - 1:tm, :]
        gel, _ = _gelu(z_ref[:, 0:W_A])
        ya = hs * gel
        ext_b[HALO_B:HALO_B + tm, :] = z_ref[:, 2 * W_A:2 * W_A + W_B] * _sigmoid(z_ref[:, 2 * W_A + W_B:2 * W_A + 2 * W_B])
        _build_shifted(shifted, ext_b[...], tm + HALO_B)
        vc = cv_ref[8:9, :] + w31_ref[0:1, :] * _shifted_rows(shifted, HALO_B - CONV_B + 1, tm)
        for k in range(1, CONV_B):
            vc = vc + w31_ref[k:k + 1, :] * _shifted_rows(shifted, HALO_B - CONV_B + 1 + k, tm)
        ext_b[0:HALO_B, :] = ext_b[tm:tm + HALO_B, :]
        cvs_ref[:, W_A:W_A + W_B] = vc
        _, _, vln = _b_norm(vc, cv_ref, g_ref)
        yb = vln * _sigmoid(vln)
        fpre = _dot(ya.astype(BF), wout_ref[0:W_A, :]) + _dot(yb.astype(BF), wout_ref[W_A:W_A + W_B, :])
        f_ref[...] = fpre
        xo_ref[...] = _post_fwd(x_t, fpre, vec_ref, 1.0)

    return pl.pallas_call(
        body, grid=(t_len // tm,),
        in_specs=[_tile(tm, D), _full((8, D)), _full((N_DEV, D, 256)), _full((16, W_A)), _full((32, W_B)),
                  _full((W_A, W_A)), _full((W_A, W_A)), _full((W_B, W_B)), _full((D, D))],
        out_specs=[_tile(tm, D), _tile(tm, D), _tile(tm, 2 * D), _tile(tm, W_A), _tile(tm, W_A + W_B)],
        out_shape=[jax.ShapeDtypeStruct((t_len, D), F32), jax.ShapeDtypeStruct((t_len, D), F32),
                   jax.ShapeDtypeStruct((t_len, 2 * D), F32), jax.ShapeDtypeStruct((t_len, W_A), F32),
                   jax.ShapeDtypeStruct((t_len, W_A + W_B), F32)],
        scratch_shapes=[pltpu.VMEM((tm + HALO_A, W_A), F32), pltpu.VMEM((tm + HALO_B, W_B), F32), pltpu.VMEM((8, W_A), F32),
                        pltpu.VMEM((8, tm + HALO_B, W_B), F32)],
        compiler_params=_params(), name=name,
    )(x, vec, w_in, cv, w31, wr, wi, gmat, w_out)


def _mixab_bwd(dout, x, fpre, z, cvs, hs, vec, w_in, cv, w31, wr, wi, gmat, w_out, name, comm=()):
    t_len = x.shape[0]
    tm = min(TM_MIX, t_len)
    nt = t_len // tm

    def rev(i):
        return nt - 1 - i

    def rtile(ncol):
        return pl.BlockSpec((tm, ncol), lambda i: (rev(i), 0))

    def body(dout_ref, x_ref, f_ref, z_ref, cvs_ref, hs_ref, hsp_ref, vec_ref, win_ref, cv_ref, w31_ref, wr_ref, wi_ref,
             g_ref, wout_ref,
             dx_ref, dz_ref, yab_ref, hb_ref, dfb_ref, xcb_ref, dri_ref, acc_ref, accs_ref, dw31_ref,
             ext_h, ext_dx, ext_dv, carry, shifted, win_v):
        i = pl.program_id(0)
        has_prev = (rev(i) > 0).astype(F32)

        @pl.when(i == 0)
        def _():
            for j in range(N_DEV):
                win_v[:, 256 * j:256 * (j + 1)] = win_ref[j]
            acc_ref[...] = jnp.zeros((8, D), F32)
            accs_ref[...] = jnp.zeros((16, W_A), F32)
            dw31_ref[...] = jnp.zeros((32, W_B), F32)
            ext_dx[tm:tm + HALO_A, :] = jnp.zeros((HALO_A, W_A), F32)
            ext_dv[tm:tm + HALO_B, :] = jnp.zeros((HALO_B, W_B), F32)
            carry[...] = jnp.zeros((8, W_A), F32)

        dout_t = dout_ref[...]
        df = _post_bwd(dout_t, f_ref[...], vec_ref, acc_ref, 1.0)
        dfb = df.astype(BF)
        dfb_ref[...] = dfb
        h, xn, r_x, n = _pre_fwd(x_ref[...], vec_ref)
        hb_ref[...] = h.astype(BF)

        ag = z_ref[:, 0:W_A]
        ax = z_ref[:, W_A:2 * W_A]
        bv = z_ref[:, 2 * W_A:2 * W_A + W_B]
        sg = _sigmoid(z_ref[:, 2 * W_A + W_B:2 * W_A + 2 * W_B])
        vv = bv * sg
        xc = cvs_ref[:, 0:W_A]
        xcb, r, ig, ls, a, m = _a_gates(xc, cv_ref, wr_ref, wi_ref)
        xcb_ref[...] = xcb
        hs_t = hs_ref[...]
        ext_h[0:8, :] = hsp_ref[...] * has_prev
        ext_h[8:8 + tm, :] = hs_t
        hprev = ext_h[7:7 + tm, :]
        gel, tg = _gelu(ag)
        rstd, vhat, vln = _b_norm(cvs_ref[:, W_A:W_A + W_B], cv_ref, g_ref)
        sv = _sigmoid(vln)
        yab_ref[:, 0:W_A] = (hs_t * gel).astype(BF)
        yab_ref[:, W_A:W_A + W_B] = (vln * sv).astype(BF)

        dya = _dot_nt(dfb, wout_ref[0:W_A, :])
        dyb = _dot_nt(dfb, wout_ref[W_A:W_A + W_B, :])

        dag = dya * hs_t * _gelu_grad(ag, tg)
        row = lax.broadcasted_iota(jnp.int32, (tm, W_A), 0)
        last = row == tm - 1
        a_next = jnp.where(last, 1.0, pltpu.roll(a, tm - 1, 0))
        u0 = dya * gel + jnp.where(last, carry[0:1, :], 0.0)
        _, dhs = _scan(a_next, u0, tm, True)
        carry[0:1, :] = a[0:1, :] * dhs[0:1, :]
        da = dhs * hprev
        dm = dhs * ig * xc
        di = dhs * m * xc
        dxc = dhs * m * ig
        dla = da * a - dm * (a * a) / m
        accs_ref[7:8, :] += _colsum(dla * r) * (LRU_C * _sigmoid(-cv_ref[7:8, :]))
        drp = (dla * (LRU_C * ls)) * r * (1.0 - r)
        dip = di * ig * (1.0 - ig)
        accs_ref[5:6, :] += _colsum(drp)
        accs_ref[6:7, :] += _colsum(dip)
        drpb = drp.astype(BF)
        dipb = dip.astype(BF)
        dri_ref[:, 0:W_A] = drpb
        dri_ref[:, W_A:2 * W_A] = dipb
        dxc = dxc + _dot_nt(drpb, wr_ref[...]) + _dot_nt(dipb, wi_ref[...])
        accs_ref[4:5, :] += _colsum(dxc)
        ext_dx[0:tm, :] = dxc
        dax = jnp.zeros((tm, W_A), F32)
        for k in range(CONV_A):
            ahead = ext_dx[CONV_A - 1 - k:CONV_A - 1 - k + tm, :]
            accs_ref[k:k + 1, :] += _colsum(ax * ahead)
            dax = dax + cv_ref[k:k + 1, :] * ahead
        ext_dx[tm:tm + HALO_A, :] = dxc[0:HALO_A, :]

        dvln = dyb * (sv * (1.0 + vln * (1.0 - sv)))
        accs_ref[9:10, :] += _colsum(dvln * vhat)
        accs_ref[10:11, :] += _colsum(dvln)
        dvhat = dvln * cv_ref[9:10, :]
        dvc = rstd * (dvhat - _gmean(dvhat, g_ref) - vhat * _gmean(dvhat * vhat, g_ref))
        accs_ref[8:9, :] += _colsum(dvc)
        ext_dv[0:tm, :] = dvc
        _build_shifted(shifted, ext_dv[...], tm + HALO_B)
        dvv = jnp.zeros((tm, W_B), F32)
        for k in range(CONV_B):
            ahead = _shifted_rows(shifted, CONV_B - 1 - k, tm)
            dw31_ref[k:k + 1, :] += _colsum(vv * ahead)
            dvv = dvv + w31_ref[k:k + 1, :] * ahead
        ext_dv[tm:tm + HALO_B, :] = dvc[0:HALO_B, :]

        dz_ref[:, 0:W_A] = dag.astype(BF)
        dz_ref[:, W_A:2 * W_A] = dax.astype(BF)
        dz_ref[:, 2 * W_A:2 * W_A + W_B] = (dvv * sg).astype(BF)
        dz_ref[:, 2 * W_A + W_B:2 * W_A + 2 * W_B] = (dvv * vv * (1.0 - sg)).astype(BF)
        dh = _dot_nt(dz_ref[...], win_v[...])
        dx_ref[...] = _pre_bwd(dout_t, dh, xn, r_x, vec_ref, acc_ref)

        @pl.when(i == nt - 1)
        def _():
            _finish_acc(acc_ref, vec_ref, 1.0)

    hsp_spec = pl.BlockSpec((8, W_A), lambda i: (jnp.maximum(rev(i) * (tm // 8) - 1, 0), 0))
    nc = len(comm)
    outs = pl.pallas_call(
        _hosted(body, 15, 10, 6, nc, "scatter", (nt,)), grid=(nt,),
        in_specs=[rtile(D), rtile(D), rtile(D), rtile(2 * D), rtile(W_A + W_B), rtile(W_A), hsp_spec, _full((8, D)),
                  _full((N_DEV, D, 256)), _full((16, W_A)), _full((32, W_B)), _full((W_A, W_A)), _full((W_A, W_A)),
                  _full((W_B, W_B)), _full((D, D))] + [_any()] * nc,
        out_specs=[rtile(D), rtile(2 * D), rtile(D), rtile(D), rtile(D), rtile(W_A), rtile(2 * W_A), _full((8, D)),
                   _full((16, W_A)), _full((32, W_B))] + [_any()] * nc,
        out_shape=[jax.ShapeDtypeStruct((t_len, D), F32), jax.ShapeDtypeStruct((t_len, 2 * D), BF),
                   jax.ShapeDtypeStruct((t_len, D), BF), jax.ShapeDtypeStruct((t_len, D), BF),
                   jax.ShapeDtypeStruct((t_len, D), BF), jax.ShapeDtypeStruct((t_len, W_A), BF),
                   jax.ShapeDtypeStruct((t_len, 2 * W_A), BF), jax.ShapeDtypeStruct((8, D), F32),
                   jax.ShapeDtypeStruct((16, W_A), F32), jax.ShapeDtypeStruct((32, W_B), F32)] + _exchange_shapes(comm, "scatter"),
        scratch_shapes=[pltpu.VMEM((tm + 8, W_A), F32), pltpu.VMEM((tm + HALO_A, W_A), F32),
                        pltpu.VMEM((tm + HALO_B, W_B), F32), pltpu.VMEM((8, W_A), F32),
                        pltpu.VMEM((8, tm + HALO_B, W_B), F32), pltpu.VMEM((D, 2 * D), BF)]
        + (_exchange_scratch(nc) if nc else []),
        compiler_params=_params(), name=name,
    )(dout, x, fpre, z, cvs, hs, hs, vec, w_in, cv, w31, wr, wi, gmat, w_out, *comm)
    return outs[:10], list(outs[10:])


def _vec(p, l, j):
    return jnp.concatenate([p["mod"][l, j], p["norm_pre"][l, j][None], p["norm_post"][l, j][None], jnp.zeros((3, D), F32)], 0)


def _ab_consts(p):
    gw = p["a_gate_w"]
    gb = p["a_gate_b"]
    half = W_A // 8
    eye = jnp.eye(8, dtype=F32)[:, None, :, None]

    def block_diag(blocks):
        return (blocks[:, :, None, :] * eye).reshape(W_A, W_A).astype(BF)

    wr = block_diag(gw[:, :, 0:half])
    wi = block_diag(gw[:, :, half:2 * half])
    rows = [p["a_conv_w"], p["a_conv_b"][None], gb[:, 0:half].reshape(1, W_A), gb[:, half:2 * half].reshape(1, W_A),
            p["a_lam"][None], p["b_conv_b"][None], p["b_norm_g"][None], p["b_norm_b"][None], jnp.zeros((5, W_A), F32)]
    cv = jnp.concatenate(rows, 0)
    w31 = jnp.concatenate([p["b_conv_w"], jnp.zeros((1, W_B), F32)], 0)
    grp = jnp.arange(W_B) // (W_B // 8)
    gmat = ((grp[:, None] == grp[None, :]).astype(F32) / (W_B // 8)).astype(BF)
    return cv, w31, wr, wi, gmat


SUBLAYERS = ("f0", "ab", "f1", "f2", "c", "f3")


def _local_step(x, tgt, p, plan=None):
    g = {}
    saved = []
    cur = x
    wsets = dict(p["wsets"])
    ab_c = _ab_consts(p)
    c_cvec = jnp.concatenate([p["c_norm_g"][None], p["c_norm_b"][None], jnp.zeros((6, D), F32)], 0)
    c_bst = jnp.transpose(p["c_b_s"])
    c_bin = p["c_b_in"][None]
    for s_idx, wname in enumerate(SUBLAYERS):
        l, j = divmod(s_idx, 3)
        vec = _vec(p, l, j)
        tag = f"l{l}s{j}"
        if j != 1:
            names = plan["gather"].get(s_idx, []) if plan else []
            comm = [a for nm in names for a in plan["shards"][nm]]
            res, got = _ffn_fwd(cur, vec, *wsets[wname], 0.5, "ffn_fwd_" + tag, comm=comm, tgt=tgt if s_idx == 5 else None)
            nxt, fpre, jac, s_act = res[:4]
            for k, nm in enumerate(names):
                wsets[nm] = list(got[2 * k:2 * k + 2])
            saved.append((cur, fpre, jac, s_act, vec))
            if s_idx == 5:
                loss_blk = res[4]
        elif l == 0:
            w_in, w_out = wsets[wname]
            nxt, fpre, z, hs, cvs = _mixab_fwd(cur, vec, w_in, *ab_c, w_out.reshape(D, D), "mixab_fwd_" + tag)
            saved.append((cur, fpre, z, hs, cvs, vec))
        else:
            w_in, w_out = wsets[wname]
            nxt, fpre, pre = _mixc_fwd(cur, vec, w_in, c_bin, c_cvec, p["c_w_s"], c_bst, w_out.reshape(D, D), "mixc_fwd_" + tag)
            saved.append((cur, fpre, pre, vec))
        cur = nxt
    dcur = cur
    accs, pending, recv = {}, {}, {}

    def take(host):
        keys = plan["scatter"].get(host, []) if plan else []
        return keys, [pending.pop(k) for k in keys]

    def put(keys, got):
        recv.update(zip(keys, got))

    for s_idx in reversed(range(6)):
        wname = SUBLAYERS[s_idx]
        l, j = divmod(s_idx, 3)
        tag = f"l{l}s{j}"
        sv = saved[s_idx]
        if j != 1:
            keys, comm = take("ffn_bwd_" + tag)
            xin, fpre, jac, s, vec = sv
            (dcur, dgu, hb, dfb, acc), got = _ffn_bwd(dcur, xin, fpre, jac, vec, *wsets[wname], 0.5, "ffn_bwd_" + tag, comm=comm)
            put(keys, got)
            keys, comm = take("wgrad_w13_" + tag)
            dw13 = _wgrad(hb, dgu, 2, D, D_FF, "share", "cols", D, BF, "wgrad_w13_" + tag, tk=TK_WGRAD, comm=comm,
                          slots=(_w13_slots, (4, D, FF_PAD)))
            if keys:
                dw13, got = dw13
                put(keys, got)
            pending[wname + ".0"] = dw13
            keys, comm = take("wgrad_w2_" + tag)
            dw2 = _wgrad(s, dfb, 1, D_FF, D, "share", "share", D_FF, BF, "wgrad_w2_" + tag, tk=TK_WGRAD, comm=comm,
                         slots=(_w2_slots, (N_DEV, W2_SHARD, D)))
            if keys:
                dw2, got = dw2
                put(keys, got)
            pending[wname + ".1"] = dw2
        elif l == 0:
            xin, fpre, z, hs, cvs, vec = sv
            w_in, w_out = wsets[wname]
            keys, comm = take("mixab_bwd_" + tag)
            (dcur, dz, yab, hb, dfb, xcb, dri, acc, accs_ab, dw31), got = _mixab_bwd(
                dcur, xin, fpre, z, cvs, hs, vec, w_in, *ab_c, w_out.reshape(D, D), "mixab_bwd_" + tag, comm=comm)
            put(keys, got)
            d_in = _wgrad(hb, dz, 1, D, 2 * D, "share", "share", D, BF, "wgrad_ab_in", tk=TK_WGRAD, col_slots=N_DEV)
            d_out = _wgrad(yab, dfb, 1, D, D, "share", "share", D, BF, "wgrad_ab_out")
            pending[wname + ".0"], pending[wname + ".1"] = d_in, d_out.reshape(N_DEV, D // N_DEV, D)
            g["gate"] = _wgrad(xcb, dri, 1, W_A, 2 * W_A, "share", "share", W_A, F32, "wgrad_gate")
            g["accs_ab"] = accs_ab
            g["dw31"] = dw31
        else:
            xin, fpre, pre, vec = sv
            w_in, w_out = wsets[wname]
            dcur, dpre, pb, hb, dfb, acc, dbin, dws, dbst = _mixc_bwd(
                dcur, xin, fpre, pre, vec, w_in, c_cvec, p["c_w_s"], c_bst, w_out.reshape(D, D), "mixc_bwd_" + tag)
            d_in = _wgrad(hb, dpre, 1, D, 2 * D, "share", "share", D, BF, "wgrad_c_in", tk=TK_WGRAD, col_slots=N_DEV)
            d_out = _wgrad(pb, dfb, 1, D, D, "share", "share", D, BF, "wgrad_c_out")
            pending[wname + ".0"], pending[wname + ".1"] = d_in, d_out.reshape(N_DEV, D // N_DEV, D)
            g["c_small"] = (acc, dbin, dws, dbst)
        accs[f"{l}{j}"] = acc
    g["accs"] = accs
    g["pending"] = pending
    g["recv"] = recv
    return loss_blk, dcur, g


def _exchange_ops(ins, outs, sems, mode, action):
    send_sems, recv_sems, loc_sems = sems
    n = len(ins)
    x, y, c = lax.axis_index("x"), lax.axis_index("y"), lax.axis_index("c")
    me = 4 * x + 2 * y + c

    def src(i, dev):
        return ins[i] if mode == "gather" else ins[i].at[dev]

    for i in range(n):
        cp = pltpu.make_async_copy(src(i, me), outs[i].at[me], loc_sems.at[i])
        if action == "start":
            cp.start()
        else:
            cp.wait()
    for mask in range(1, N_DEV):
        px = 1 - x if mask & 4 else x
        py = 1 - y if mask & 2 else y
        pc = 1 - c if mask & 1 else c
        peer = 4 * px + 2 * py + pc
        for i in range(n):
            k = i * (N_DEV - 1) + mask - 1
            cp = pltpu.make_async_remote_copy(
                src_ref=src(i, peer), dst_ref=outs[i].at[me if action == "start" else peer],
                send_sem=send_sems.at[k], recv_sem=recv_sems.at[k],
                device_id=(px, py, pc), device_id_type=pl.DeviceIdType.MESH)
            if action == "start":
                cp.start()
            else:
                cp.wait()


def _exchange_scratch(n):
    return [pltpu.SemaphoreType.DMA((n * (N_DEV - 1),)), pltpu.SemaphoreType.DMA((n * (N_DEV - 1),)),
            pltpu.SemaphoreType.DMA((n,))]


def _exchange_shapes(arrays, mode):
    return [jax.ShapeDtypeStruct(((N_DEV,) + a.shape) if mode == "gather" else a.shape, a.dtype) for a in arrays]


def _exchange(arrays, mode, name):
    n = len(arrays)

    def body(*refs):
        ins, outs, sems = refs[:n], refs[n:2 * n], refs[2 * n:]
        _exchange_ops(ins, outs, sems, mode, "start")
        _exchange_ops(ins, outs, sems, mode, "wait")

    return pl.pallas_call(
        body, in_specs=[pl.BlockSpec(memory_space=pl.ANY)] * n, out_specs=[pl.BlockSpec(memory_space=pl.ANY)] * n,
        out_shape=_exchange_shapes(arrays, mode), scratch_shapes=_exchange_scratch(n), name=name,
    )(*arrays)


def _gather_two_level(arrays, name):
    n = len(arrays)
    per = N_DEV - 1

    def body(*refs):
        ins, outs = refs[:n], refs[n:2 * n]
        send_sems, recv_sems, loc_sems = refs[2 * n:]
        x, y, c = lax.axis_index("x"), lax.axis_index("y"), lax.axis_index("c")
        me, sibling = (x, y, c), (x, y, 1 - c)
        chips = [(1 - x, y), (x, 1 - y), (1 - x, 1 - y)]

        def rows(i, dev):
            return outs[i].at[4 * dev[0] + 2 * dev[1] + dev[2]]

        def copy(i, k, block, to, src=None):
            return pltpu.make_async_remote_copy(
                src_ref=rows(i, block) if src is None else src, dst_ref=rows(i, block),
                send_sem=send_sems.at[i * per + k], recv_sem=recv_sems.at[i * per + k],
                device_id=to, device_id_type=pl.DeviceIdType.MESH)

        mine = [pltpu.make_async_copy(ins[i], rows(i, me), loc_sems.at[i]) for i in range(n)]
        for cp in mine:
            cp.start()
        first = []
        for i in range(n):
            first.append(copy(i, 0, me, sibling, src=ins[i]))
            first += [copy(i, 1 + j, me, (*chip, c), src=ins[i]) for j, chip in enumerate(chips)]
        for cp in first:
            cp.start()
        passed = []
        for j, chip in enumerate(chips):
            for i in range(n):
                copy(i, 1 + j, (*chip, c), me).wait_recv()
                fwd = copy(i, 4 + j, (*chip, c), sibling)
                fwd.start()
                passed.append(fwd)
        for i in range(n):
            copy(i, 0, sibling, me).wait_recv()
            for j, chip in enumerate(chips):
                copy(i, 4 + j, (*chip, 1 - c), me).wait_recv()
        for cp in first + passed:
            cp.wait_send()
        for cp in mine:
            cp.wait()

    return pl.pallas_call(
        body, in_specs=[pl.BlockSpec(memory_space=pl.ANY)] * n, out_specs=[pl.BlockSpec(memory_space=pl.ANY)] * n,
        out_shape=_exchange_shapes(arrays, "gather"), scratch_shapes=_exchange_scratch(n), name=name,
    )(*arrays)


def _sum_slots(a, name):
    def body(a_ref, o_ref):
        acc = a_ref[0]
        for s in range(1, N_DEV):
            acc = acc + a_ref[s]
        o_ref[...] = acc

    return pl.pallas_call(body, out_shape=jax.ShapeDtypeStruct(a.shape[1:], F32), name=name,
                          compiler_params=pltpu.CompilerParams(vmem_limit_bytes=VMEM_LIMIT))(a)


def _pack(pieces, mult):
    flat = jnp.concatenate([q.reshape(-1).astype(F32) for q in pieces])
    size = -(-flat.shape[0] // mult) * mult
    return jnp.pad(flat, (0, size - flat.shape[0])).reshape(size // 128, 128)


def _unpack(flat, shapes):
    out, off = [], 0
    for shp in shapes:
        size = math.prod(shp)
        out.append(flat[..., off:off + size].reshape(flat.shape[:-1] + tuple(shp)))
        off += size
    return out


def _mod_part(c_all, ada_w, ada_b_mine, name):
    cols = ada_w.shape[-1]

    def body(c_ref, w_ref, b_ref, o_ref):
        cv = c_ref[...]
        ca = cv * _sigmoid(cv)
        for l in range(2):
            o_ref[l] = jnp.dot(ca, w_ref[l], preferred_element_type=F32, precision=lax.Precision.HIGHEST) + b_ref[l:l + 1, :]

    return pl.pallas_call(body, out_shape=jax.ShapeDtypeStruct((2, N_DEV, cols), F32), name=name,
                          compiler_params=pltpu.CompilerParams(vmem_limit_bytes=VMEM_LIMIT))(c_all, ada_w, ada_b_mine)


def _ada_w_grad(c_all_t, dmod_mine, name):
    cols = dmod_mine.shape[-1]

    def body(ct_ref, d_ref, o_ref):
        cv = ct_ref[...]
        ca = cv * _sigmoid(cv)
        for l in range(2):
            acc = ca[:, 0:1] * d_ref[l, 0:1, :]
            for b in range(1, N_DEV):
                acc = acc + ca[:, b:b + 1] * d_ref[l, b:b + 1, :]
            o_ref[l] = acc

    return pl.pallas_call(body, out_shape=jax.ShapeDtypeStruct((2, D, cols), F32), name=name,
                          compiler_params=pltpu.CompilerParams(vmem_limit_bytes=VMEM_LIMIT))(c_all_t, dmod_mine)


def _adamw_math(w, g, m, v):
    m2 = ADAM_B1 * m + (1.0 - ADAM_B1) * g
    v2 = ADAM_B2 * v + (1.0 - ADAM_B2) * (g * g)
    m_hat = m2 / (1.0 - ADAM_B1 ** ADAM_STEP)
    v_hat = v2 / (1.0 - ADAM_B2 ** ADAM_STEP)
    delta = -ADAM_LR * (m_hat / (jnp.sqrt(v_hat) + ADAM_EPS) + ADAM_WD * w)
    return delta, m2, v2


def _adamw_big(w, g, m, v, name):
    n_l, rows, cols = w.shape
    parts = list(g) if isinstance(g, (list, tuple)) else None
    sizes = (512, 352, 256, 128, 64, 32, 16, 8) if parts is None or len(parts) == 1 else (176, 128, 64, 32, 16, 8)
    br = next(b for b in sizes if rows % b == 0)
    nr = rows // br

    def body(*refs):
        w_ref, g_refs, (m_ref, v_ref, go_ref, d_ref, mo_ref, vo_ref) = refs[0], refs[1:-6], refs[-6:]

        def update(gsum):
            go_ref[...] = gsum
            d_ref[...], mo_ref[...], vo_ref[...] = _adamw_math(w_ref[...], gsum, m_ref[...], v_ref[...])

        if parts is None:
            update(g_refs[0][...])
        else:
            for l, g_ref in enumerate(g_refs):
                @pl.when(pl.program_id(0) == l)
                def _(g_ref=g_ref):
                    gsum = g_ref[0, :, 0:cols].astype(F32)
                    for s in range(1, N_DEV):
                        gsum = gsum + g_ref[s, :, 0:cols].astype(F32)
                    update(gsum)

    blk = pl.BlockSpec((None, br, cols), lambda l, i: (l, i, 0))
    if parts is None:
        g_specs, g_args = [blk], [g]
    else:
        def part_spec(l_mine, width):
            return pl.BlockSpec((N_DEV, br, width),
                                lambda l, i: (0, jnp.where(l < l_mine, 0, jnp.where(l == l_mine, i, nr - 1)), 0))
        g_specs, g_args = [part_spec(l, p.shape[-1]) for l, p in enumerate(parts)], parts
    shp = jax.ShapeDtypeStruct((n_l, rows, cols), F32)
    return pl.pallas_call(
        body, grid=(n_l, nr), in_specs=[blk] + g_specs + [blk, blk], out_specs=[blk] * 4, out_shape=[shp] * 4,
        compiler_params=pltpu.CompilerParams(dimension_semantics=("arbitrary", "arbitrary"), vmem_limit_bytes=VMEM_LIMIT),
        name=name)(w, *g_args, m, v)


def _adamw_small(ws, gs, ms, vs, name):
    n = len(ws)

    def body(*refs):
        for i in range(n):
            w_ref, g_ref, m_ref, v_ref = (refs[k * n + i] for k in range(4))
            d_ref, mo_ref, vo_ref = (refs[(4 + k) * n + i] for k in range(3))
            d_ref[...], mo_ref[...], vo_ref[...] = _adamw_math(w_ref[...], g_ref[...], m_ref[...], v_ref[...])

    shapes = [jax.ShapeDtypeStruct(w.shape, F32) for w in ws]
    outs = pl.pallas_call(body, out_shape=shapes * 3, name=name,
                          compiler_params=pltpu.CompilerParams(vmem_limit_bytes=VMEM_LIMIT))(*ws, *gs, *ms, *vs)
    return outs[:n], outs[n:2 * n], outs[2 * n:]


def _as2d(a):
    return a.reshape(-1, a.shape[-1])


def kernel(x, c, ada_w, ada_b, norm_pre, norm_post, ffn_w13, ffn_w2, ab_w_in, a_conv_w, a_conv_b, a_gate_w, a_gate_b, a_lam, b_conv_w, b_conv_b, b_norm_g, b_norm_b, ab_w_out, c_w_in, c_b_in, c_norm_g, c_norm_b, c_w_s, c_b_s, c_w_out, loss_target, m_ada_w, m_ada_b, m_norm_pre, m_norm_post, m_ffn_w13, m_ffn_w2, m_ab_w_in, m_a_conv_w, m_a_conv_b, m_a_gate_w, m_a_gate_b, m_a_lam, m_b_conv_w, m_b_conv_b, m_b_norm_g, m_b_norm_b, m_ab_w_out, m_c_w_in, m_c_b_in, m_c_norm_g, m_c_norm_b, m_c_w_s, m_c_b_s, m_c_w_out, v_ada_w, v_ada_b, v_norm_pre, v_norm_post, v_ffn_w13, v_ffn_w2, v_ab_w_in, v_a_conv_w, v_a_conv_b, v_a_gate_w, v_a_gate_b, v_a_lam, v_b_conv_w, v_b_conv_b, v_b_norm_g, v_b_norm_b, v_ab_w_out, v_c_w_in, v_c_b_in, v_c_norm_g, v_c_norm_b, v_c_w_s, v_c_b_s, v_c_w_out):
    me = 4 * lax.axis_index("x") + 2 * lax.axis_index("y") + lax.axis_index("c")
    weights = dict(ada_w=ada_w, ada_b=ada_b, norm_pre=norm_pre, norm_post=norm_post, ffn_w13=ffn_w13, ffn_w2=ffn_w2,
                   ab_w_in=ab_w_in, a_conv_w=a_conv_w, a_conv_b=a_conv_b, a_gate_w=a_gate_w, a_gate_b=a_gate_b, a_lam=a_lam,
                   b_conv_w=b_conv_w, b_conv_b=b_conv_b, b_norm_g=b_norm_g, b_norm_b=b_norm_b, ab_w_out=ab_w_out,
                   c_w_in=c_w_in, c_b_in=c_b_in, c_norm_g=c_norm_g, c_norm_b=c_norm_b, c_w_s=c_w_s, c_b_s=c_b_s, c_w_out=c_w_out)
    moms = dict(ada_w=m_ada_w, ada_b=m_ada_b, norm_pre=m_norm_pre, norm_post=m_norm_post, ffn_w13=m_ffn_w13, ffn_w2=m_ffn_w2,
                ab_w_in=m_ab_w_in, a_conv_w=m_a_conv_w, a_conv_b=m_a_conv_b, a_gate_w=m_a_gate_w, a_gate_b=m_a_gate_b,
                a_lam=m_a_lam, b_conv_w=m_b_conv_w, b_conv_b=m_b_conv_b, b_norm_g=m_b_norm_g, b_norm_b=m_b_norm_b,
                ab_w_out=m_ab_w_out, c_w_in=m_c_w_in, c_b_in=m_c_b_in, c_norm_g=m_c_norm_g, c_norm_b=m_c_norm_b,
                c_w_s=m_c_w_s, c_b_s=m_c_b_s, c_w_out=m_c_w_out)
    vars_ = dict(ada_w=v_ada_w, ada_b=v_ada_b, norm_pre=v_norm_pre, norm_post=v_norm_post, ffn_w13=v_ffn_w13, ffn_w2=v_ffn_w2,
                 ab_w_in=v_ab_w_in, a_conv_w=v_a_conv_w, a_conv_b=v_a_conv_b, a_gate_w=v_a_gate_w, a_gate_b=v_a_gate_b,
                 a_lam=v_a_lam, b_conv_w=v_b_conv_w, b_conv_b=v_b_conv_b, b_norm_g=v_b_norm_g, b_norm_b=v_b_norm_b,
                 ab_w_out=v_ab_w_out, c_w_in=v_c_w_in, c_b_in=v_c_b_in, c_norm_g=v_c_norm_g, c_norm_b=v_c_norm_b,
                 c_w_s=v_c_w_s, c_b_s=v_c_b_s, c_w_out=v_c_w_out)
    names = list(weights)

    w13s = ffn_w13.astype(BF).reshape(4, D, FF_SHARD)
    tail, blank = w13s[..., FF_MAIN:], jnp.zeros((4, D, FF_TAIL), BF)
    tail_tile = jnp.where(me % 2 == 1, jnp.concatenate([blank, tail], -1), jnp.concatenate([tail, blank], -1))
    w13b = jnp.concatenate([w13s[..., :FF_MAIN], tail_tile], -1)
    small_shapes = [(D,), (2, 3, 128), (2, 3, 128), (CONV_A, 64), (CONV_B, 64), (256,), (128,), (128,)]
    small = _pack([c, norm_pre, norm_post, a_conv_w, b_conv_w, c_b_in, c_norm_g, c_norm_b], 1024)
    w2b = ffn_w2.astype(BF).reshape(4, W2_SHARD, D)
    shards = {f"f{f}": [w13b[f], w2b[f]] for f in range(4)}
    shards["ab"] = [ab_w_in[0].astype(BF), ab_w_out[0].astype(BF)]
    shards["c"] = [c_w_in[0].astype(BF), c_w_out[0].astype(BF)]
    w13g0, w2g0, small_g = _gather_two_level(shards["f0"] + [small], "gather_first")
    plan = dict(shards=shards, gather={0: ["ab", "f1"], 2: ["f2"], 3: ["c", "f3"]},
                scatter={"ffn_bwd_l1s0": ["f3.0", "f3.1", "c.0", "c.1"], "ffn_bwd_l0s2": ["f2.0", "f2.1"],
                         "mixab_bwd_l0s1": ["f1.0", "f1.1"], "wgrad_w13_l0s0": ["ab.0", "ab.1"], "wgrad_w2_l0s0": ["f0.0"]})
    c_all, npre_g, npost_g, acw_g, bcw_g, cbin_g, cng_g, cnb_g = _unpack(small_g.reshape(N_DEV, -1), small_shapes)

    def cat_last(a):
        return jnp.moveaxis(a, 0, -2).reshape(a.shape[1:-1] + (N_DEV * a.shape[-1],))

    ada_b_mine = lax.dynamic_slice_in_dim(ada_b, me * ada_w.shape[-1], ada_w.shape[-1], axis=1)
    (mod_g,) = _exchange([_mod_part(c_all, ada_w, ada_b_mine, "mod_part")], "gather", "gather_mod")
    mod = cat_last(lax.dynamic_index_in_dim(mod_g, me, axis=2, keepdims=False)).reshape(2, 3, 3, D)

    p = dict(mod=mod, norm_pre=cat_last(npre_g), norm_post=cat_last(npost_g), wsets={"f0": [w13g0, w2g0]},
             a_conv_w=cat_last(acw_g), a_conv_b=a_conv_b[0], a_gate_w=a_gate_w[0], a_gate_b=a_gate_b[0], a_lam=a_lam[0],
             b_conv_w=cat_last(bcw_g), b_conv_b=b_conv_b[0], b_norm_g=b_norm_g[0], b_norm_b=b_norm_b[0],
             c_b_in=cat_last(cbin_g), c_norm_g=cat_last(cng_g), c_norm_b=cat_last(cnb_g), c_w_s=c_w_s[0], c_b_s=c_b_s[0])

    loss_blk, grad_x, g = _local_step(x[0], loss_target[0], p, plan)
    loss = lax.psum(loss_blk[0, 0], ("x", "y", "c"))

    accs = g["accs"]
    dmod = jnp.stack([jnp.stack([accs[f"{l}{j}"][0:3] for j in range(3)]) for l in range(2)])
    dnpre = jnp.stack([jnp.stack([accs[f"{l}{j}"][3] for j in range(3)]) for l in range(2)])
    dnpost = jnp.stack([jnp.stack([accs[f"{l}{j}"][4] for j in range(3)]) for l in range(2)])
    sab = g["accs_ab"]
    half = W_A // 8
    dgate = g["gate"][0]
    dgw = jnp.stack([jnp.concatenate([dgate[half * hh:half * (hh + 1), half * hh:half * (hh + 1)],
                                      dgate[half * hh:half * (hh + 1), W_A + half * hh:W_A + half * (hh + 1)]], axis=1)
                     for hh in range(8)])
    dgb = jnp.concatenate([sab[5].reshape(8, half), sab[6].reshape(8, half)], axis=1)
    c_acc, c_dbin, c_dws, c_dbst = g["c_small"]
    red_shapes = [(2, 9216), (2, 3, D), (2, 3, D), (CONV_A, W_A), (W_A,), (8, half, 2 * half), (8, 2 * half), (W_A,),
                  (CONV_B, W_B), (W_B,), (W_B,), (W_B,), (2 * D,), (D,), (D,), (H_C, CHUNK, CHUNK), (H_C, CHUNK)]
    red = _pack([dmod.reshape(2, 9216), dnpre, dnpost, sab[0:4], sab[4], dgw, dgb, sab[7], g["dw31"][0:CONV_B], sab[8],
                 sab[9], sab[10], c_dbin[0], c_acc[5], c_acc[6], c_dws, jnp.transpose(c_dbst)], N_DEV * 1024)
    left = sorted(g["pending"])
    red_r, *last_recv = _exchange([red.reshape(N_DEV, -1, 128)] + [g["pending"][k] for k in left], "scatter",
                                  "scatter_small_grads")
    red_all, dmod_all = _exchange([_sum_slots(red_r, "sum_small_grads"), dmod.reshape(-1, 128)], "gather", "gather_small_grads")
    red_sum = red_all.reshape(-1)
    (g_ada_b, g_npre, g_npost, g_acw, g_acb, g_agw, g_agb, g_alam, g_bcw, g_bcb, g_bng, g_bnb, g_cbin, g_cng, g_cnb,
     g_cws, g_cbs) = _unpack(red_sum, red_shapes)
    dmod_all = dmod_all.reshape(N_DEV, 2, 9216)
    ncol = ada_w.shape[-1]
    dmod_mine = jnp.moveaxis(lax.dynamic_slice_in_dim(dmod_all, me * ncol, ncol, axis=2), 0, 1)
    g_ada_w = _ada_w_grad(jnp.transpose(c_all), dmod_mine, "ada_w_grad")

    def mine(a, width):
        return lax.dynamic_slice_in_dim(a, me * width, width, axis=a.ndim - 1)

    small_grads = dict(
        ada_b=g_ada_b, norm_pre=mine(g_npre, 128), norm_post=mine(g_npost, 128), a_conv_w=mine(g_acw, 64)[None],
        a_conv_b=g_acb[None], a_gate_w=g_agw[None], a_gate_b=g_agb[None], a_lam=g_alam[None], b_conv_w=mine(g_bcw, 64)[None],
        b_conv_b=g_bcb[None], b_norm_g=g_bng[None], b_norm_b=g_bnb[None], c_b_in=mine(g_cbin, 256)[None],
        c_norm_g=mine(g_cng, 128)[None], c_norm_b=mine(g_cnb, 128)[None], c_w_s=g_cws[None], c_b_s=g_cbs[None])

    recv = dict(g["recv"])
    recv.update(zip(left, last_recv))
    big_partials = dict(ffn_w13=[recv[f"f{f}.0"] for f in range(4)], ffn_w2=[recv[f"f{f}.1"] for f in range(4)],
                        ab_w_in=[recv["ab.0"]], ab_w_out=[recv["ab.1"]], c_w_in=[recv["c.0"]], c_w_out=[recv["c.1"]],
                        ada_w=g_ada_w)

    grads, deltas, new_m, new_v = {}, {}, {}, {}

    def as3d(a):
        return a.reshape((-1,) + a.shape[-2:])

    for nm, gp in big_partials.items():
        shp = weights[nm].shape
        go, dl, mo, vo = _adamw_big(as3d(weights[nm]), gp, as3d(moms[nm]), as3d(vars_[nm]), "adamw_" + nm)
        grads[nm], deltas[nm], new_m[nm], new_v[nm] = (a.reshape(shp) for a in (go, dl, mo, vo))
    snames = list(small_grads)
    dls, mos, vos = _adamw_small([_as2d(weights[nm]) for nm in snames], [_as2d(small_grads[nm]) for nm in snames],
                                 [_as2d(moms[nm]) for nm in snames], [_as2d(vars_[nm]) for nm in snames], "adamw_small")
    for k, nm in enumerate(snames):
        shp = weights[nm].shape
        grads[nm] = small_grads[nm].reshape(shp)
        deltas[nm], new_m[nm], new_v[nm] = dls[k].reshape(shp), mos[k].reshape(shp), vos[k].reshape(shp)

    return (loss, grad_x[None], *[grads[nm] for nm in names], *[deltas[nm] for nm in names],
            *[new_m[nm] for nm in names], *[new_v[nm] for nm in names])
```

```python
import math

import jax
import jax.numpy as jnp
from jax import lax
from jax.experimental import pallas as pl
from jax.experimental.pallas import tpu as pltpu

F32 = jnp.float32
BF = jnp.bfloat16

N_DEV = 8
D = 1024
EPS = 1e-6
D_FF = 2816
FF_SHARD = 704
FF_PAD = 768
FF_MAIN = 640
FF_TAIL = FF_SHARD - FF_MAIN
W2_SHARD = 352
W_A = 512
W_B = 512
AB_SAVED = 6 * W_A
CONV_A = 4
CONV_B = 31
HALO_A = 8
HALO_B = 32
LRU_C = 8.0
CHUNK = 128
H_C = 8
ADAM_LR = 0.001
ADAM_B1 = 0.9
ADAM_B2 = 0.999
ADAM_EPS = 1e-08
ADAM_WD = 0.01
ADAM_STEP = 10
VMEM_LIMIT = 62 * 1024 * 1024
GELU_C = math.sqrt(2.0 / math.pi)

TM_FFN = 512
TM_FFN_BWD = 256
TM_MIX = 256
TM_MIXC_FWD = 512
TK_WGRAD = 2048


def _params(limit=VMEM_LIMIT):
    return pltpu.CompilerParams(dimension_semantics=("arbitrary",), vmem_limit_bytes=limit)


def _dot(a, b):
    return jnp.dot(a, b, preferred_element_type=F32)


def _dot_nt(a, b):
    return lax.dot_general(a, b, (((1,), (1,)), ((), ())), preferred_element_type=F32)


def _dot_tn(a, b):
    return lax.dot_general(a, b, (((0,), (0,)), ((), ())), preferred_element_type=F32)


def _sigmoid(x):
    return 0.5 + 0.5 * jnp.tanh(0.5 * x)


def _gelu(x):
    t = jnp.tanh(GELU_C * (x + 0.044715 * x * x * x))
    return 0.5 * x * (1.0 + t), t


def _gelu_grad(x, t):
    return 0.5 * (1.0 + t) + 0.5 * x * (1.0 - t * t) * GELU_C * (1.0 + 3.0 * 0.044715 * x * x)


def _rms(x):
    r = lax.rsqrt(jnp.mean(x * x, axis=-1, keepdims=True) + EPS)
    return x * r, r


def _colsum(x):
    return jnp.sum(x, axis=0, keepdims=True)


def _pre_fwd(x, vec_ref):
    xn, r = _rms(x)
    n = xn * vec_ref[3:4, :]
    h = n * (1.0 + vec_ref[1:2, :]) + vec_ref[0:1, :]
    return h, xn, r, n


def _post_fwd(x, f, vec_ref, res_w):
    fn, _ = _rms(f)
    return x + fn * ((res_w * (1.0 + vec_ref[2:3, :])) * vec_ref[4:5, :])


def _post_bwd(dout, f, vec_ref, acc_ref, res_w):
    fn, r2 = _rms(f)
    acc_ref[2:3, :] += _colsum(dout * fn)
    dfn = dout * ((res_w * (1.0 + vec_ref[2:3, :])) * vec_ref[4:5, :])
    return r2 * (dfn - fn * jnp.mean(dfn * fn, axis=-1, keepdims=True))


def _pre_bwd(dout, dh, xn, r, vec_ref, acc_ref):
    acc_ref[0:1, :] += _colsum(dh)
    acc_ref[1:2, :] += _colsum(dh * xn)
    dxn = dh * ((1.0 + vec_ref[1:2, :]) * vec_ref[3:4, :])
    return dout + r * (dxn - xn * jnp.mean(dxn * xn, axis=-1, keepdims=True))


def _finish_acc(acc_ref, vec_ref, res_w):
    s_pre, s_post = acc_ref[1:2, :], acc_ref[2:3, :]
    acc_ref[1:2, :] = vec_ref[3:4, :] * s_pre
    acc_ref[3:4, :] = (1.0 + vec_ref[1:2, :]) * s_pre
    acc_ref[2:3, :] = (res_w * vec_ref[4:5, :]) * s_post
    acc_ref[4:5, :] = (res_w * (1.0 + vec_ref[2:3, :])) * s_post


def _tile(tm, ncol):
    return pl.BlockSpec((tm, ncol), lambda i: (i, 0))


def _full(shape):
    return pl.BlockSpec(shape, lambda i: (0,) * len(shape))


def _any():
    return pl.BlockSpec(memory_space=pl.ANY)


def _load_ffn_weights(w13_hbm, w2_hbm, w13_v, w2_v, tails, sems):
    copies = []
    for j in range(N_DEV):
        half, k = divmod(j, 4)
        copies.append((w13_hbm.at[j, :, pl.ds(0, FF_MAIN)], w13_v.at[:, pl.ds(D_FF * half + FF_MAIN * k, FF_MAIN)]))
        copies.append((w13_hbm.at[j, :, pl.ds(FF_MAIN, 128)], tails.at[j]))
    for k in range(4):
        copies.append((w2_hbm.at[2 * k], w2_v.at[pl.ds(FF_MAIN * k, W2_SHARD), :]))
        copies.append((w2_hbm.at[2 * k + 1, pl.ds(0, FF_MAIN - W2_SHARD), :],
                       w2_v.at[pl.ds(FF_MAIN * k + W2_SHARD, FF_MAIN - W2_SHARD), :]))
        copies.append((w2_hbm.at[2 * k + 1, pl.ds(FF_MAIN - W2_SHARD, FF_TAIL), :],
                       w2_v.at[pl.ds(4 * FF_MAIN + FF_TAIL * k, FF_TAIL), :]))
    copies = [pltpu.make_async_copy(src, dst, sems.at[n]) for n, (src, dst) in enumerate(copies)]
    for cp in copies:
        cp.start()
    for cp in copies:
        cp.wait()
    for pair in range(4):
        half, kk = divmod(pair, 2)
        base = D_FF * half + 4 * FF_MAIN + 128 * kk
        w13_v[:, base:base + 128] = tails[2 * pair] + tails[2 * pair + 1]


_FFN_SCRATCH = [pltpu.VMEM((D, 2 * D_FF), BF), pltpu.VMEM((D_FF, D), BF), pltpu.VMEM((N_DEV, D, 128), BF),
                pltpu.SemaphoreType.DMA((2 * N_DEV + 12,))]
HID_CHUNKS = ((0, 768), (768, 768), (1536, 768), (2304, 512))


def _hosted(body, n_in, n_out, n_scratch, n_comm, mode, grid):
    if not n_comm:
        return body

    def at(corner):
        hit = pl.program_id(0) == corner[0]
        for d in range(1, len(grid)):
            hit = hit & (pl.program_id(d) == corner[d])
        return hit

    def hosted(*refs):
        ins, cin = refs[:n_in], refs[n_in:n_in + n_comm]
        outs, cout = refs[n_in + n_comm:n_in + n_comm + n_out], refs[n_in + n_comm + n_out:n_in + 2 * n_comm + n_out]
        scratch = refs[n_in + 2 * n_comm + n_out:]
        own, sems = scratch[:n_scratch], scratch[n_scratch:]

        @pl.when(at([0] * len(grid)))
        def _():
            _exchange_ops(cin, cout, sems, mode, "start")

        body(*ins, *outs, *own)

        @pl.when(at([n - 1 for n in grid]))
        def _():
            _exchange_ops(cin, cout, sems, mode, "wait")

    return hosted


def _ffn_fwd(x, vec, w13g, w2g, res_w, name, comm=(), tgt=None):
    t_len = x.shape[0]
    tm = min(TM_FFN, t_len)
    nc = len(comm)
    head = tgt is not None

    def body(*refs):
        if head:
            x_ref, vec_ref, w13_hbm, w2_hbm, t_ref, xo_ref, f_ref, jac_ref, s_ref, loss_ref, w13_v, w2_v, tails, sems = refs
        else:
            x_ref, vec_ref, w13_hbm, w2_hbm, xo_ref, f_ref, jac_ref, s_ref, w13_v, w2_v, tails, sems = refs

        @pl.when(pl.program_id(0) == 0)
        def _():
            _load_ffn_weights(w13_hbm, w2_hbm, w13_v, w2_v, tails, sems)
            if head:
                loss_ref[...] = jnp.zeros((8, 128), F32)

        x_t = x_ref[...]
        h, _, _, _ = _pre_fwd(x_t, vec_ref)
        hb = h.astype(BF)
        for c0, cw in HID_CHUNKS:
            g = _dot(hb, w13_v[:, c0:c0 + cw])
            u = _dot(hb, w13_v[:, D_FF + c0:D_FF + c0 + cw])
            sig = _sigmoid(g)
            sl = g * sig
            jac_ref[0, :, c0:c0 + cw] = (u * (sig + sl * (1.0 - sig))).astype(BF)
            jac_ref[1, :, c0:c0 + cw] = sl.astype(BF)
            s_ref[:, c0:c0 + cw] = (sl * u).astype(BF)
        acc = _dot(s_ref[...], w2_v[...])
        f_ref[...] = acc
        xo = _post_fwd(x_t, acc, vec_ref, res_w)
        if head:
            err = xo - t_ref[...]
            xo_ref[...] = err * (1.0 / D)
            loss_ref[...] += jnp.sum(err * err) * (0.5 / D)
        else:
            xo_ref[...] = xo

    nt = t_len // tm
    n_in, n_out = (5, 5) if head else (4, 4)
    outs = pl.pallas_call(
        _hosted(body, n_in, n_out, 4, nc, "gather", (nt,)), grid=(nt,),
        in_specs=[_tile(tm, D), _full((8, D)), _any(), _any()] + ([_tile(tm, D)] if head else []) + [_any()] * nc,
        out_specs=[_tile(tm, D), _tile(tm, D), pl.BlockSpec((2, tm, D_FF), lambda i: (0, i, 0)), _tile(tm, D_FF)]
        + ([_full((8, 128))] if head else []) + [_any()] * nc,
        out_shape=[jax.ShapeDtypeStruct((t_len, D), F32), jax.ShapeDtypeStruct((t_len, D), F32),
                   jax.ShapeDtypeStruct((2, t_len, D_FF), BF), jax.ShapeDtypeStruct((t_len, D_FF), BF)]
        + ([jax.ShapeDtypeStruct((8, 128), F32)] if head else []) + _exchange_shapes(comm, "gather"),
        scratch_shapes=_FFN_SCRATCH + (_exchange_scratch(nc) if nc else []), compiler_params=_params(), name=name,
    )(x, vec, w13g, w2g, *([tgt] if head else []), *comm)
    return outs[:n_out], outs[n_out:]


def _ffn_bwd(dout, x, fpre, jac, vec, w13g, w2g, res_w, name, comm=()):
    t_len = x.shape[0]
    tm = min(TM_FFN_BWD, t_len)
    nt = t_len // tm
    nc = len(comm)

    def body(dout_ref, x_ref, f_ref, jac_ref, vec_ref, w13_hbm, w2_hbm,
             dx_ref, dgu_ref, hb_ref, dfb_ref, acc_ref, w13_v, w2_v, tails, sems):
        @pl.when(pl.program_id(0) == 0)
        def _():
            _load_ffn_weights(w13_hbm, w2_hbm, w13_v, w2_v, tails, sems)
            acc_ref[...] = jnp.zeros((8, D), F32)

        dout_t = dout_ref[...]
        df = _post_bwd(dout_t, f_ref[...], vec_ref, acc_ref, res_w)
        dfb = df.astype(BF)
        dfb_ref[...] = dfb
        h, xn, r, _ = _pre_fwd(x_ref[...], vec_ref)
        hb_ref[...] = h.astype(BF)
        for c0, cw in HID_CHUNKS:
            ds = _dot_nt(dfb, w2_v[c0:c0 + cw, :]).astype(BF)
            dgu_ref[:, c0:c0 + cw] = ds * jac_ref[0, :, c0:c0 + cw]
            dgu_ref[:, D_FF + c0:D_FF + c0 + cw] = ds * jac_ref[1, :, c0:c0 + cw]
        dh = _dot_nt(dgu_ref[...], w13_v[...])
        dx_ref[...] = _pre_bwd(dout_t, dh, xn, r, vec_ref, acc_ref)

        @pl.when(pl.program_id(0) == nt - 1)
        def _():
            _finish_acc(acc_ref, vec_ref, res_w)

    jac_spec = pl.BlockSpec((2, tm, D_FF), lambda i: (0, i, 0))
    outs = pl.pallas_call(
        _hosted(body, 7, 5, 4, nc, "scatter", (nt,)), grid=(nt,),
        in_specs=[_tile(tm, D), _tile(tm, D), _tile(tm, D), jac_spec, _full((8, D)), _any(), _any()] + [_any()] * nc,
        out_specs=[_tile(tm, D), _tile(tm, 2 * D_FF), _tile(tm, D), _tile(tm, D), _full((8, D))] + [_any()] * nc,
        out_shape=[jax.ShapeDtypeStruct((t_len, D), F32), jax.ShapeDtypeStruct((t_len, 2 * D_FF), BF),
                   jax.ShapeDtypeStruct((t_len, D), BF),
                   jax.ShapeDtypeStruct((t_len, D), BF), jax.ShapeDtypeStruct((8, D), F32)] + _exchange_shapes(comm, "scatter"),
        scratch_shapes=_FFN_SCRATCH + (_exchange_scratch(nc) if nc else []), compiler_params=_params(), name=name,
    )(dout, x, fpre, jac, vec, w13g, w2g, *comm)
    return outs[:5], outs[5:]


def _w13_slots(acc, o_ref):
    for k in range(4):
        o_ref[k, :, 0:FF_MAIN] = acc[:, FF_MAIN * k:FF_MAIN * (k + 1)].astype(BF)
        pair_tile = acc[:, 4 * FF_MAIN + 128 * (k // 2):4 * FF_MAIN + 128 * (k // 2 + 1)]
        o_ref[k, :, FF_MAIN:FF_PAD] = (pair_tile if k % 2 == 0 else pltpu.roll(pair_tile, FF_TAIL, 1)).astype(BF)


def _w2_slots(acc, o_ref):
    rest = FF_MAIN - W2_SHARD
    for k in range(4):
        o_ref[2 * k] = acc[FF_MAIN * k:FF_MAIN * k + W2_SHARD, :].astype(BF)
        o_ref[2 * k + 1, 0:rest, :] = acc[FF_MAIN * k + W2_SHARD:FF_MAIN * (k + 1), :].astype(BF)
        o_ref[2 * k + 1, rest:W2_SHARD, :] = acc[4 * FF_MAIN + FF_TAIL * k:4 * FF_MAIN + FF_TAIL * (k + 1), :].astype(BF)


def _wgrad(a, b, j_count, m, n, a_mode, b_mode, out_rows, out_dtype, name, tk=TK_WGRAD, col_slots=1, comm=(),
           slots=None):
    t_len = a.shape[-2]
    tk = min(tk, t_len)
    nk = t_len // tk
    wn = n // col_slots
    nc = len(comm)

    def spec(mode, width):
        if mode == "stack":
            return pl.BlockSpec((None, tk, width), lambda j, t: (j, t, 0))
        if mode == "cols":
            return pl.BlockSpec((tk, width), lambda j, t: (t, j))
        return pl.BlockSpec((tk, width), lambda j, t: (t, 0))

    def body(a_ref, b_ref, o_ref, acc):
        t = pl.program_id(1)

        @pl.when(t == 0)
        def _():
            acc[...] = jnp.zeros((m, n), F32)

        acc[...] += _dot_tn(a_ref[...], b_ref[...])

        @pl.when(t == nk - 1)
        def _():
            if slots is not None:
                slots[0](acc, o_ref)
            elif col_slots == 1:
                o_ref[...] = acc[0:out_rows, :].astype(out_dtype)
            else:
                for s in range(col_slots):
                    o_ref[s] = acc[0:out_rows, wn * s:wn * (s + 1)].astype(out_dtype)

    if slots is not None:
        blk = slots[1]
        out_spec = pl.BlockSpec(blk, lambda j, t: (j,) + (0,) * (len(blk) - 1))
        out_shape = jax.ShapeDtypeStruct((j_count * blk[0],) + blk[1:], BF)
    elif col_slots == 1:
        out_spec = pl.BlockSpec((None, out_rows, n), lambda j, t: (j, 0, 0))
        out_shape = jax.ShapeDtypeStruct((j_count, out_rows, n), out_dtype)
    else:
        out_spec = pl.BlockSpec((col_slots, out_rows, wn), lambda j, t: (0, 0, 0))
        out_shape = jax.ShapeDtypeStruct((col_slots, out_rows, wn), out_dtype)
    outs = pl.pallas_call(
        _hosted(body, 2, 1, 1, nc, "scatter", (j_count, nk)), grid=(j_count, nk),
        in_specs=[spec(a_mode, m), spec(b_mode, n)] + [_any()] * nc,
        out_specs=[out_spec] + [_any()] * nc, out_shape=[out_shape] + _exchange_shapes(comm, "scatter"),
        scratch_shapes=[pltpu.VMEM((m, n), F32)] + (_exchange_scratch(nc) if nc else []),
        compiler_params=pltpu.CompilerParams(dimension_semantics=("arbitrary", "arbitrary"), vmem_limit_bytes=VMEM_LIMIT),
        name=name,
    )(a, b, *comm)
    return (outs[0], list(outs[1:])) if nc else outs[0]


def _c_mask_weights(ws_ref, wsm, wsmt):
    row = lax.broadcasted_iota(jnp.int32, (CHUNK, CHUNK), 0)
    col = lax.broadcasted_iota(jnp.int32, (CHUNK, CHUNK), 1)
    for hh in range(H_C):
        w = jnp.where(row >= col, ws_ref[hh], 0.0)
        wsm[hh] = w.astype(BF)
        if wsmt is not None:
            wsmt[hh] = w.T.astype(BF)


def _c_inner(pre, cvec_ref, wsm, bst_ref, mix_sc, tm):
    z, t = _gelu(pre)
    u = z[:, 0:D]
    v = z[:, D:2 * D]
    mu = jnp.mean(v, axis=-1, keepdims=True)
    vc = v - mu
    rstd = lax.rsqrt(jnp.mean(vc * vc, axis=-1, keepdims=True) + EPS)
    vhat = vc * rstd
    vnb = (vhat * cvec_ref[0:1, :] + cvec_ref[1:2, :]).astype(BF)
    for nn in range(tm // CHUNK):
        for hh in range(H_C):
            rows = slice(CHUNK * nn, CHUNK * (nn + 1))
            cols = slice(CHUNK * hh, CHUNK * (hh + 1))
            mix_sc[rows, cols] = _dot(wsm[hh], vnb[rows, cols]) + bst_ref[:, hh:hh + 1]
    return u, t, rstd, vhat, vnb


def _mixc_fwd(x, vec, w_in, b_in, cvec, ws, bst, w_out, name):
    t_len = x.shape[0]
    tm = min(TM_MIXC_FWD, t_len)

    def body(x_ref, vec_ref, win_ref, bin_ref, cvec_ref, ws_ref, bst_ref, wout_ref,
             xo_ref, f_ref, pre_ref, wsm, mix_sc):
        @pl.when(pl.program_id(0) == 0)
        def _():
            _c_mask_weights(ws_ref, wsm, None)

        x_t = x_ref[...]
        h, _, _, _ = _pre_fwd(x_t, vec_ref)
        hb = h.astype(BF)
        for j in range(N_DEV):
            cols = slice(256 * j, 256 * (j + 1))
            pre_ref[:, cols] = _dot(hb, win_ref[j]) + bin_ref[:, cols]
        u, _, _, _, _ = _c_inner(pre_ref[...], cvec_ref, wsm, bst_ref, mix_sc, tm)
        fpre = _dot((u * mix_sc[...]).astype(BF), wout_ref[...])
        f_ref[...] = fpre
        xo_ref[...] = _post_fwd(x_t, fpre, vec_ref, 1.0)

    return pl.pallas_call(
        body, grid=(t_len // tm,),
        in_specs=[_tile(tm, D), _full((8, D)), _full((N_DEV, D, 256)), _full((1, 2 * D)), _full((8, D)),
                  _full((H_C, CHUNK, CHUNK)), _full((CHUNK, H_C)), _full((D, D))],
        out_specs=[_tile(tm, D), _tile(tm, D), _tile(tm, 2 * D)],
        out_shape=[jax.ShapeDtypeStruct((t_len, D), F32), jax.ShapeDtypeStruct((t_len, D), F32),
                   jax.ShapeDtypeStruct((t_len, 2 * D), F32)],
        scratch_shapes=[pltpu.VMEM((H_C, CHUNK, CHUNK), BF), pltpu.VMEM((tm, D), F32)],
        compiler_params=_params(), name=name,
    )(x, vec, w_in, b_in, cvec, ws, bst, w_out)


def _mixc_bwd(dout, x, fpre, pre, vec, w_in, cvec, ws, bst, w_out, name):
    t_len = x.shape[0]
    tm = min(TM_MIX, t_len)
    nt = t_len // tm

    def body(dout_ref, x_ref, f_ref, pre_ref, vec_ref, win_ref, cvec_ref, ws_ref, bst_ref, wout_ref,
             dx_ref, dpre_ref, p_ref, hb_ref, dfb_ref, acc_ref, dbin_ref, dws_ref, dbst_ref,
             wsm, wsmt, mix_sc, dvn_sc, dmsum, win_v):
        i = pl.program_id(0)

        @pl.when(i == 0)
        def _():
            _c_mask_weights(ws_ref, wsm, wsmt)
            for j in range(N_DEV):
                win_v[:, 256 * j:256 * (j + 1)] = win_ref[j]
            acc_ref[...] = jnp.zeros((8, D), F32)
            dbin_ref[...] = jnp.zeros((8, 2 * D), F32)
            dws_ref[...] = jnp.zeros((H_C, CHUNK, CHUNK), F32)
            dmsum[...] = jnp.zeros((CHUNK, D), F32)

        dout_t = dout_ref[...]
        df = _post_bwd(dout_t, f_ref[...], vec_ref, acc_ref, 1.0)
        dfb = df.astype(BF)
        dfb_ref[...] = dfb
        h, xn, r, n = _pre_fwd(x_ref[...], vec_ref)
        hb_ref[...] = h.astype(BF)
        pre_t = pre_ref[...]
        u, t, rstd, vhat, vnb = _c_inner(pre_t, cvec_ref, wsm, bst_ref, mix_sc, tm)
        mix = mix_sc[...]
        p_ref[...] = (u * mix).astype(BF)
        dp = _dot_nt(dfb, wout_ref[...])
        du = dp * mix
        dmix = dp * u
        dmb = dmix.astype(BF)
        for nn in range(tm // CHUNK):
            rows = slice(CHUNK * nn, CHUNK * (nn + 1))
            dmsum[...] += dmix[rows, :]
            for hh in range(H_C):
                cols = slice(CHUNK * hh, CHUNK * (hh + 1))
                dvn_sc[rows, cols] = _dot(wsmt[hh], dmb[rows, cols])
                dws_ref[hh] += _dot_nt(dmb[rows, cols], vnb[rows, cols])
        dvn = dvn_sc[...]
        acc_ref[5:6, :] += _colsum(dvn * vhat)
        acc_ref[6:7, :] += _colsum(dvn)
        dvhat = dvn * cvec_ref[0:1, :]
        dv = rstd * (dvhat - jnp.mean(dvhat, axis=-1, keepdims=True)
                     - vhat * jnp.mean(dvhat * vhat, axis=-1, keepdims=True))
        gg = _gelu_grad(pre_t, t)
        dpre_u = du * gg[:, 0:D]
        dpre_v = dv * gg[:, D:2 * D]
        dbin_ref[0:1, 0:D] += _colsum(dpre_u)
        dbin_ref[0:1, D:2 * D] += _colsum(dpre_v)
        dpre_ref[:, 0:D] = dpre_u.astype(BF)
        dpre_ref[:, D:2 * D] = dpre_v.astype(BF)
        dh = _dot_nt(dpre_ref[...], win_v[...])
        dx_ref[...] = _pre_bwd(dout_t, dh, xn, r, vec_ref, acc_ref)

        @pl.when(i == nt - 1)
        def _():
            _finish_acc(acc_ref, vec_ref, 1.0)
            row = lax.broadcasted_iota(jnp.int32, (CHUNK, CHUNK), 0)
            col = lax.broadcasted_iota(jnp.int32, (CHUNK, CHUNK), 1)
            for hh in range(H_C):
                dws_ref[hh] = jnp.where(row >= col, dws_ref[hh], 0.0)
                dbst_ref[:, hh:hh + 1] = jnp.sum(dmsum[:, CHUNK * hh:CHUNK * (hh + 1)], axis=1, keepdims=True)

    return pl.pallas_call(
        body, grid=(nt,),
        in_specs=[_tile(tm, D), _tile(tm, D), _tile(tm, D), _tile(tm, 2 * D), _full((8, D)), _full((N_DEV, D, 256)),
                  _full((8, D)), _full((H_C, CHUNK, CHUNK)), _full((CHUNK, H_C)), _full((D, D))],
        out_specs=[_tile(tm, D), _tile(tm, 2 * D), _tile(tm, D), _tile(tm, D), _tile(tm, D), _full((8, D)),
                   _full((8, 2 * D)), _full((H_C, CHUNK, CHUNK)), _full((CHUNK, H_C))],
        out_shape=[jax.ShapeDtypeStruct((t_len, D), F32), jax.ShapeDtypeStruct((t_len, 2 * D), BF),
                   jax.ShapeDtypeStruct((t_len, D), BF), jax.ShapeDtypeStruct((t_len, D), BF),
                   jax.ShapeDtypeStruct((t_len, D), BF), jax.ShapeDtypeStruct((8, D), F32),
                   jax.ShapeDtypeStruct((8, 2 * D), F32), jax.ShapeDtypeStruct((H_C, CHUNK, CHUNK), F32),
                   jax.ShapeDtypeStruct((CHUNK, H_C), F32)],
        scratch_shapes=[pltpu.VMEM((H_C, CHUNK, CHUNK), BF), pltpu.VMEM((H_C, CHUNK, CHUNK), BF),
                        pltpu.VMEM((tm, D), F32), pltpu.VMEM((tm, D), F32), pltpu.VMEM((CHUNK, D), F32),
                        pltpu.VMEM((D, 2 * D), BF)],
        compiler_params=_params(), name=name,
    )(dout, x, fpre, pre, vec, w_in, cvec, ws, bst, w_out)


def _gmean(x, g_ref):
    hi = x.astype(BF)
    lo = (x - hi.astype(F32)).astype(BF)
    return _dot(hi, g_ref[...]) + _dot(lo, g_ref[...])


def _log_sigmoid(lam):
    e = jnp.exp(-jnp.abs(lam))
    log1p = jnp.where(e < 1e-2, e * (1.0 - e * (0.5 - e * (1.0 / 3.0 - 0.25 * e))), jnp.log(1.0 + e))
    return jnp.minimum(lam, 0.0) - log1p


def _neg_expm1(y):
    series = -(y * (1.0 + y * (0.5 + y * (1.0 / 6.0 + y * (1.0 / 24.0 + y * (1.0 / 120.0))))))
    return jnp.where(y > -0.1, series, 1.0 - jnp.exp(y))


def _rows_from(e, off, tm):
    return e[off:off + tm, :] if off % 8 == 0 else pltpu.roll(e, e.shape[0] - off, 0)[0:tm, :]


def _conv_causal(ext, taps_ref, bias, k_taps, halo, tm):
    e = ext[...]
    acc = bias
    for k in range(k_taps):
        acc = acc + taps_ref[k:k + 1, :] * _rows_from(e, halo - k_taps + 1 + k, tm)
    return acc


def _build_shifted(sh_ref, e, n_rows):
    sh_ref[0] = e
    for r in range(1, 8):
        sh_ref[r] = pltpu.roll(e, n_rows - r, 0)


def _shifted_rows(sh_ref, off, tm):
    base = off - off % 8
    return sh_ref[off % 8, base:base + tm, :]


def _scan(a, u, tm, reverse):
    row = lax.broadcasted_iota(jnp.int32, (tm, W_A), 0)
    d = 1
    while d < tm:
        if reverse:
            keep = row < tm - d
            shift = tm - d
        else:
            keep = row >= d
            shift = d
        a_sh = jnp.where(keep, pltpu.roll(a, shift, 0), 1.0)
        u_sh = jnp.where(keep, pltpu.roll(u, shift, 0), 0.0)
        u = a * u_sh + u
        a = a * a_sh
        d *= 2
    return a, u


def _a_gates(xc, cv_ref, wr_ref, wi_ref):
    xcb = xc.astype(BF)
    r = _sigmoid(_dot(xcb, wr_ref[...]) + cv_ref[5:6, :])
    ig = _sigmoid(_dot(xcb, wi_ref[...]) + cv_ref[6:7, :])
    ls = _log_sigmoid(cv_ref[7:8, :])
    la = LRU_C * r * ls
    a = jnp.exp(la)
    m = jnp.sqrt(_neg_expm1(2.0 * la))
    return xcb, r, ig, ls, a, m


def _b_norm(vc, cv_ref, g_ref):
    mu = _gmean(vc, g_ref)
    dv = vc - mu
    rstd = lax.rsqrt(_gmean(dv * dv, g_ref) + EPS)
    vhat = dv * rstd
    vln = vhat * cv_ref[9:10, :] + cv_ref[10:11, :]
    return rstd, vhat, vln


def _mixab_fwd(x, vec, w_in, cv, w31, wr, wi, gmat, w_out, name):
    t_len = x.shape[0]
    tm = min(TM_MIX, t_len)

    def body(x_ref, vec_ref, win_ref, cv_ref, w31_ref, wr_ref, wi_ref, g_ref, wout_ref,
             xo_ref, f_ref, z_ref, hs_ref, cvs_ref, ext_a, ext_b, hc, shifted):
        @pl.when(pl.program_id(0) == 0)
        def _():
            ext_a[0:HALO_A, :] = jnp.zeros((HALO_A, W_A), F32)
            ext_b[0:HALO_B, :] = jnp.zeros((HALO_B, W_B), F32)
            hc[...] = jnp.zeros((8, W_A), F32)

        x_t = x_ref[...]
        h, _, _, _ = _pre_fwd(x_t, vec_ref)
        hb = h.astype(BF)
        for j in range(N_DEV):
            z_ref[:, 256 * j:256 * (j + 1)] = _dot(hb, win_ref[j])
        ext_a[HALO_A:HALO_A + tm, :] = z_ref[:, W_A:2 * W_A]
        xc = _conv_causal(ext_a, cv_ref, cv_ref[4:5, :], CONV_A, HALO_A, tm)
        ext_a[0:HALO_A, :] = ext_a[tm:tm + HALO_A, :]
        cvs_ref[:, 0:W_A] = xc
        _, r, ig, _, a, m = _a_gates(xc, cv_ref, wr_ref, wi_ref)
        for col, val in enumerate((r, ig, a, m)):
            cvs_ref[:, W_A * (2 + col):W_A * (3 + col)] = val
        a_cum, hloc = _scan(a, m * ig * xc, tm, False)
        hs = hloc + a_cum * hc[0:1, :]
        hs_ref[...] = hs
        hc[0:1, :] = hs[tm - 1:tm, :]
        gel, _ = _gelu(z_ref[:, 0:W_A])
        ya = hs * gel
        ext_b[HALO_B:HALO_B + tm, :] = z_ref[:, 2 * W_A:2 * W_A + W_B] * _sigmoid(z_ref[:, 2 * W_A + W_B:2 * W_A + 2 * W_B])
        _build_shifted(shifted, ext_b[...], tm + HALO_B)
        vc = cv_ref[8:9, :] + w31_ref[0:1, :] * _shifted_rows(shifted, HALO_B - CONV_B + 1, tm)
        for k in range(1, CONV_B):
            vc = vc + w31_ref[k:k + 1, :] * _shifted_rows(shifted, HALO_B - CONV_B + 1 + k, tm)
        ext_b[0:HALO_B, :] = ext_b[tm:tm + HALO_B, :]
        cvs_ref[:, W_A:W_A + W_B] = vc
        _, _, vln = _b_norm(vc, cv_ref, g_ref)
        yb = vln * _sigmoid(vln)
        fpre = _dot(ya.astype(BF), wout_ref[0:W_A, :]) + _dot(yb.astype(BF), wout_ref[W_A:W_A + W_B, :])
        f_ref[...] = fpre
        xo_ref[...] = _post_fwd(x_t, fpre, vec_ref, 1.0)

    return pl.pallas_call(
        body, grid=(t_len // tm,),
        in_specs=[_tile(tm, D), _full((8, D)), _full((N_DEV, D, 256)), _full((16, W_A)), _full((32, W_B)),
                  _full((W_A, W_A)), _full((W_A, W_A)), _full((W_B, W_B)), _full((D, D))],
        out_specs=[_tile(tm, D), _tile(tm, D), _tile(tm, 2 * D), _tile(tm, W_A), _tile(tm, AB_SAVED)],
        out_shape=[jax.ShapeDtypeStruct((t_len, D), F32), jax.ShapeDtypeStruct((t_len, D), F32),
                   jax.ShapeDtypeStruct((t_len, 2 * D), F32), jax.ShapeDtypeStruct((t_len, W_A), F32),
                   jax.ShapeDtypeStruct((t_len, AB_SAVED), F32)],
        scratch_shapes=[pltpu.VMEM((tm + HALO_A, W_A), F32), pltpu.VMEM((tm + HALO_B, W_B), F32), pltpu.VMEM((8, W_A), F32),
                        pltpu.VMEM((8, tm + HALO_B, W_B), F32)],
        compiler_params=_params(), name=name,
    )(x, vec, w_in, cv, w31, wr, wi, gmat, w_out)


def _mixab_bwd(dout, x, fpre, z, cvs, hs, vec, w_in, cv, w31, wr, wi, gmat, w_out, name, comm=()):
    t_len = x.shape[0]
    tm = min(TM_MIX, t_len)
    nt = t_len // tm

    def rev(i):
        return nt - 1 - i

    def rtile(ncol):
        return pl.BlockSpec((tm, ncol), lambda i: (rev(i), 0))

    def body(dout_ref, x_ref, f_ref, z_ref, cvs_ref, hs_ref, hsp_ref, vec_ref, win_ref, cv_ref, w31_ref, wr_ref, wi_ref,
             g_ref, wout_ref,
             dx_ref, dz_ref, yab_ref, hb_ref, dfb_ref, xcb_ref, dri_ref, acc_ref, accs_ref, dw31_ref,
             ext_h, ext_dx, ext_dv, carry, shifted, win_v):
        i = pl.program_id(0)
        has_prev = (rev(i) > 0).astype(F32)

        @pl.when(i == 0)
        def _():
            for j in range(N_DEV):
                win_v[:, 256 * j:256 * (j + 1)] = win_ref[j]
            acc_ref[...] = jnp.zeros((8, D), F32)
            accs_ref[...] = jnp.zeros((16, W_A), F32)
            dw31_ref[...] = jnp.zeros((32, W_B), F32)
            ext_dx[tm:tm + HALO_A, :] = jnp.zeros((HALO_A, W_A), F32)
            ext_dv[tm:tm + HALO_B, :] = jnp.zeros((HALO_B, W_B), F32)
            carry[...] = jnp.zeros((8, W_A), F32)

        dout_t = dout_ref[...]
        df = _post_bwd(dout_t, f_ref[...], vec_ref, acc_ref, 1.0)
        dfb = df.astype(BF)
        dfb_ref[...] = dfb
        h, xn, r_x, n = _pre_fwd(x_ref[...], vec_ref)
        hb_ref[...] = h.astype(BF)

        ag = z_ref[:, 0:W_A]
        ax = z_ref[:, W_A:2 * W_A]
        bv = z_ref[:, 2 * W_A:2 * W_A + W_B]
        sg = _sigmoid(z_ref[:, 2 * W_A + W_B:2 * W_A + 2 * W_B])
        vv = bv * sg
        xc = cvs_ref[:, 0:W_A]
        r, ig, a, m = (cvs_ref[:, W_A * (2 + col):W_A * (3 + col)] for col in range(4))
        ls = _log_sigmoid(cv_ref[7:8, :])
        xcb_ref[...] = xc.astype(BF)
        hs_t = hs_ref[...]
        ext_h[0:8, :] = hsp_ref[...] * has_prev
        ext_h[8:8 + tm, :] = hs_t
        hprev = ext_h[7:7 + tm, :]
        gel, tg = _gelu(ag)
        rstd, vhat, vln = _b_norm(cvs_ref[:, W_A:W_A + W_B], cv_ref, g_ref)
        sv = _sigmoid(vln)
        yab_ref[:, 0:W_A] = (hs_t * gel).astype(BF)
        yab_ref[:, W_A:W_A + W_B] = (vln * sv).astype(BF)

        dya = _dot_nt(dfb, wout_ref[0:W_A, :])
        dyb = _dot_nt(dfb, wout_ref[W_A:W_A + W_B, :])

        dag = dya * hs_t * _gelu_grad(ag, tg)
        row = lax.broadcasted_iota(jnp.int32, (tm, W_A), 0)
        last = row == tm - 1
        a_next = jnp.where(last, 1.0, pltpu.roll(a, tm - 1, 0))
        u0 = dya * gel + jnp.where(last, carry[0:1, :], 0.0)
        _, dhs = _scan(a_next, u0, tm, True)
        carry[0:1, :] = a[0:1, :] * dhs[0:1, :]
        da = dhs * hprev
        dm = dhs * ig * xc
        di = dhs * m * xc
        dxc = dhs * m * ig
        dla = da * a - dm * (a * a) / m
        accs_ref[7:8, :] += _colsum(dla * r) * (LRU_C * _sigmoid(-cv_ref[7:8, :]))
        drp = (dla * (LRU_C * ls)) * r * (1.0 - r)
        dip = di * ig * (1.0 - ig)
        accs_ref[5:6, :] += _colsum(drp)
        accs_ref[6:7, :] += _colsum(dip)
        drpb = drp.astype(BF)
        dipb = dip.astype(BF)
        dri_ref[:, 0:W_A] = drpb
        dri_ref[:, W_A:2 * W_A] = dipb
        dxc = dxc + _dot_nt(drpb, wr_ref[...]) + _dot_nt(dipb, wi_ref[...])
        accs_ref[4:5, :] += _colsum(dxc)
        ext_dx[0:tm, :] = dxc
        e_dx = ext_dx[...]
        dax = jnp.zeros((tm, W_A), F32)
        for k in range(CONV_A):
            ahead = _rows_from(e_dx, CONV_A - 1 - k, tm)
            accs_ref[k:k + 1, :] += _colsum(ax * ahead)
            dax = dax + cv_ref[k:k + 1, :] * ahead
        ext_dx[tm:tm + HALO_A, :] = dxc[0:HALO_A, :]

        dvln = dyb * (sv * (1.0 + vln * (1.0 - sv)))
        accs_ref[9:10, :] += _colsum(dvln * vhat)
        accs_ref[10:11, :] += _colsum(dvln)
        dvhat = dvln * cv_ref[9:10, :]
        dvc = rstd * (dvhat - _gmean(dvhat, g_ref) - vhat * _gmean(dvhat * vhat, g_ref))
        accs_ref[8:9, :] += _colsum(dvc)
        ext_dv[0:tm, :] = dvc
        _build_shifted(shifted, ext_dv[...], tm + HALO_B)
        dvv = jnp.zeros((tm, W_B), F32)
        for k in range(CONV_B):
            ahead = _shifted_rows(shifted, CONV_B - 1 - k, tm)
            dw31_ref[k:k + 1, :] += _colsum(vv * ahead)
            dvv = dvv + w31_ref[k:k + 1, :] * ahead
        ext_dv[tm:tm + HALO_B, :] = dvc[0:HALO_B, :]

        dz_ref[:, 0:W_A] = dag.astype(BF)
        dz_ref[:, W_A:2 * W_A] = dax.astype(BF)
        dz_ref[:, 2 * W_A:2 * W_A + W_B] = (dvv * sg).astype(BF)
        dz_ref[:, 2 * W_A + W_B:2 * W_A + 2 * W_B] = (dvv * vv * (1.0 - sg)).astype(BF)
        dh = _dot_nt(dz_ref[...], win_v[...])
        dx_ref[...] = _pre_bwd(dout_t, dh, xn, r_x, vec_ref, acc_ref)

        @pl.when(i == nt - 1)
        def _():
            _finish_acc(acc_ref, vec_ref, 1.0)

    hsp_spec = pl.BlockSpec((8, W_A), lambda i: (jnp.maximum(rev(i) * (tm // 8) - 1, 0), 0))
    nc = len(comm)
    outs = pl.pallas_call(
        _hosted(body, 15, 10, 6, nc, "scatter", (nt,)), grid=(nt,),
        in_specs=[rtile(D), rtile(D), rtile(D), rtile(2 * D), rtile(AB_SAVED), rtile(W_A), hsp_spec, _full((8, D)),
                  _full((N_DEV, D, 256)), _full((16, W_A)), _full((32, W_B)), _full((W_A, W_A)), _full((W_A, W_A)),
                  _full((W_B, W_B)), _full((D, D))] + [_any()] * nc,
        out_specs=[rtile(D), rtile(2 * D), rtile(D), rtile(D), rtile(D), rtile(W_A), rtile(2 * W_A), _full((8, D)),
                   _full((16, W_A)), _full((32, W_B))] + [_any()] * nc,
        out_shape=[jax.ShapeDtypeStruct((t_len, D), F32), jax.ShapeDtypeStruct((t_len, 2 * D), BF),
                   jax.ShapeDtypeStruct((t_len, D), BF), jax.ShapeDtypeStruct((t_len, D), BF),
                   jax.ShapeDtypeStruct((t_len, D), BF), jax.ShapeDtypeStruct((t_len, W_A), BF),
                   jax.ShapeDtypeStruct((t_len, 2 * W_A), BF), jax.ShapeDtypeStruct((8, D), F32),
                   jax.ShapeDtypeStruct((16, W_A), F32), jax.ShapeDtypeStruct((32, W_B), F32)] + _exchange_shapes(comm, "scatter"),
        scratch_shapes=[pltpu.VMEM((tm + 8, W_A), F32), pltpu.VMEM((tm + HALO_A, W_A), F32),
                        pltpu.VMEM((tm + HALO_B, W_B), F32), pltpu.VMEM((8, W_A), F32),
                        pltpu.VMEM((8, tm + HALO_B, W_B), F32), pltpu.VMEM((D, 2 * D), BF)]
        + (_exchange_scratch(nc) if nc else []),
        compiler_params=_params(), name=name,
    )(dout, x, fpre, z, cvs, hs, hs, vec, w_in, cv, w31, wr, wi, gmat, w_out, *comm)
    return outs[:10], list(outs[10:])


def _vec(p, l, j):
    return jnp.concatenate([p["mod"][l, j], p["norm_pre"][l, j][None], p["norm_post"][l, j][None], jnp.zeros((3, D), F32)], 0)


def _ab_consts(p):
    gw = p["a_gate_w"]
    gb = p["a_gate_b"]
    half = W_A // 8
    eye = jnp.eye(8, dtype=F32)[:, None, :, None]

    def block_diag(blocks):
        return (blocks[:, :, None, :] * eye).reshape(W_A, W_A).astype(BF)

    wr = block_diag(gw[:, :, 0:half])
    wi = block_diag(gw[:, :, half:2 * half])
    rows = [p["a_conv_w"], p["a_conv_b"][None], gb[:, 0:half].reshape(1, W_A), gb[:, half:2 * half].reshape(1, W_A),
            p["a_lam"][None], p["b_conv_b"][None], p["b_norm_g"][None], p["b_norm_b"][None], jnp.zeros((5, W_A), F32)]
    cv = jnp.concatenate(rows, 0)
    w31 = jnp.concatenate([p["b_conv_w"], jnp.zeros((1, W_B), F32)], 0)
    grp = jnp.arange(W_B) // (W_B // 8)
    gmat = ((grp[:, None] == grp[None, :]).astype(F32) / (W_B // 8)).astype(BF)
    return cv, w31, wr, wi, gmat


SUBLAYERS = ("f0", "ab", "f1", "f2", "c", "f3")


def _local_step(x, tgt, p, plan=None):
    g = {}
    saved = []
    cur = x
    wsets = dict(p["wsets"])
    ab_c = _ab_consts(p)
    c_cvec = jnp.concatenate([p["c_norm_g"][None], p["c_norm_b"][None], jnp.zeros((6, D), F32)], 0)
    c_bst = jnp.transpose(p["c_b_s"])
    c_bin = p["c_b_in"][None]
    for s_idx, wname in enumerate(SUBLAYERS):
        l, j = divmod(s_idx, 3)
        vec = _vec(p, l, j)
        tag = f"l{l}s{j}"
        if j != 1:
            names = plan["gather"].get(s_idx, []) if plan else []
            comm = [a for nm in names for a in plan["shards"][nm]]
            res, got = _ffn_fwd(cur, vec, *wsets[wname], 0.5, "ffn_fwd_" + tag, comm=comm, tgt=tgt if s_idx == 5 else None)
            nxt, fpre, jac, s_act = res[:4]
            for k, nm in enumerate(names):
                wsets[nm] = list(got[2 * k:2 * k + 2])
            saved.append((cur, fpre, jac, s_act, vec))
            if s_idx == 5:
                loss_blk = res[4]
        elif l == 0:
            w_in, w_out = wsets[wname]
            nxt, fpre, z, hs, cvs = _mixab_fwd(cur, vec, w_in, *ab_c, w_out.reshape(D, D), "mixab_fwd_" + tag)
            saved.append((cur, fpre, z, hs, cvs, vec))
        else:
            w_in, w_out = wsets[wname]
            nxt, fpre, pre = _mixc_fwd(cur, vec, w_in, c_bin, c_cvec, p["c_w_s"], c_bst, w_out.reshape(D, D), "mixc_fwd_" + tag)
            saved.append((cur, fpre, pre, vec))
        cur = nxt
    dcur = cur
    accs, pending, recv = {}, {}, {}

    def take(host):
        keys = plan["scatter"].get(host, []) if plan else []
        return keys, [pending.pop(k) for k in keys]

    def put(keys, got):
        recv.update(zip(keys, got))

    for s_idx in reversed(range(6)):
        wname = SUBLAYERS[s_idx]
        l, j = divmod(s_idx, 3)
        tag = f"l{l}s{j}"
        sv = saved[s_idx]
        if j != 1:
            keys, comm = take("ffn_bwd_" + tag)
            xin, fpre, jac, s, vec = sv
            (dcur, dgu, hb, dfb, acc), got = _ffn_bwd(dcur, xin, fpre, jac, vec, *wsets[wname], 0.5, "ffn_bwd_" + tag, comm=comm)
            put(keys, got)
            keys, comm = take("wgrad_w13_" + tag)
            dw13 = _wgrad(hb, dgu, 2, D, D_FF, "share", "cols", D, BF, "wgrad_w13_" + tag, tk=TK_WGRAD, comm=comm,
                          slots=(_w13_slots, (4, D, FF_PAD)))
            if keys:
                dw13, got = dw13
                put(keys, got)
            pending[wname + ".0"] = dw13
            keys, comm = take("wgrad_w2_" + tag)
            dw2 = _wgrad(s, dfb, 1, D_FF, D, "share", "share", D_FF, BF, "wgrad_w2_" + tag, tk=TK_WGRAD, comm=comm,
                         slots=(_w2_slots, (N_DEV, W2_SHARD, D)))
            if keys:
                dw2, got = dw2
                put(keys, got)
            pending[wname + ".1"] = dw2
        elif l == 0:
            xin, fpre, z, hs, cvs, vec = sv
            w_in, w_out = wsets[wname]
            keys, comm = take("mixab_bwd_" + tag)
            (dcur, dz, yab, hb, dfb, xcb, dri, acc, accs_ab, dw31), got = _mixab_bwd(
                dcur, xin, fpre, z, cvs, hs, vec, w_in, *ab_c, w_out.reshape(D, D), "mixab_bwd_" + tag, comm=comm)
            put(keys, got)
            d_in = _wgrad(hb, dz, 1, D, 2 * D, "share", "share", D, BF, "wgrad_ab_in", tk=TK_WGRAD, col_slots=N_DEV)
            d_out = _wgrad(yab, dfb, 1, D, D, "share", "share", D, BF, "wgrad_ab_out")
            pending[wname + ".0"], pending[wname + ".1"] = d_in, d_out.reshape(N_DEV, D // N_DEV, D)
            g["gate"] = _wgrad(xcb, dri, 1, W_A, 2 * W_A, "share", "share", W_A, F32, "wgrad_gate")
            g["accs_ab"] = accs_ab
            g["dw31"] = dw31
        else:
            xin, fpre, pre, vec = sv
            w_in, w_out = wsets[wname]
            dcur, dpre, pb, hb, dfb, acc, dbin, dws, dbst = _mixc_bwd(
                dcur, xin, fpre, pre, vec, w_in, c_cvec, p["c_w_s"], c_bst, w_out.reshape(D, D), "mixc_bwd_" + tag)
            d_in = _wgrad(hb, dpre, 1, D, 2 * D, "share", "share", D, BF, "wgrad_c_in", tk=TK_WGRAD, col_slots=N_DEV)
            d_out = _wgrad(pb, dfb, 1, D, D, "share", "share", D, BF, "wgrad_c_out")
            pending[wname + ".0"], pending[wname + ".1"] = d_in, d_out.reshape(N_DEV, D // N_DEV, D)
            g["c_small"] = (acc, dbin, dws, dbst)
        accs[f"{l}{j}"] = acc
    g["accs"] = accs
    g["pending"] = pending
    g["recv"] = recv
    return loss_blk, dcur, g


def _exchange_ops(ins, outs, sems, mode, action):
    send_sems, recv_sems, loc_sems = sems
    n = len(ins)
    x, y, c = lax.axis_index("x"), lax.axis_index("y"), lax.axis_index("c")
    me = 4 * x + 2 * y + c

    def src(i, dev):
        return ins[i] if mode == "gather" else ins[i].at[dev]

    for i in range(n):
        cp = pltpu.make_async_copy(src(i, me), outs[i].at[me], loc_sems.at[i])
        if action == "start":
            cp.start()
        else:
            cp.wait()
    for mask in range(1, N_DEV):
        px = 1 - x if mask & 4 else x
        py = 1 - y if mask & 2 else y
        pc = 1 - c if mask & 1 else c
        peer = 4 * px + 2 * py + pc
        for i in range(n):
            k = i * (N_DEV - 1) + mask - 1
            cp = pltpu.make_async_remote_copy(
                src_ref=src(i, peer), dst_ref=outs[i].at[me if action == "start" else peer],
                send_sem=send_sems.at[k], recv_sem=recv_sems.at[k],
                device_id=(px, py, pc), device_id_type=pl.DeviceIdType.MESH)
            if action == "start":
                cp.start()
            else:
                cp.wait()


def _exchange_scratch(n):
    return [pltpu.SemaphoreType.DMA((n * (N_DEV - 1),)), pltpu.SemaphoreType.DMA((n * (N_DEV - 1),)),
            pltpu.SemaphoreType.DMA((n,))]


def _exchange_shapes(arrays, mode):
    return [jax.ShapeDtypeStruct(((N_DEV,) + a.shape) if mode == "gather" else a.shape, a.dtype) for a in arrays]


def _exchange(arrays, mode, name):
    n = len(arrays)

    def body(*refs):
        ins, outs, sems = refs[:n], refs[n:2 * n], refs[2 * n:]
        _exchange_ops(ins, outs, sems, mode, "start")
        _exchange_ops(ins, outs, sems, mode, "wait")

    return pl.pallas_call(
        body, in_specs=[pl.BlockSpec(memory_space=pl.ANY)] * n, out_specs=[pl.BlockSpec(memory_space=pl.ANY)] * n,
        out_shape=_exchange_shapes(arrays, mode), scratch_shapes=_exchange_scratch(n), name=name,
    )(*arrays)


def _gather_two_level(arrays, name):
    n = len(arrays)
    per = N_DEV - 1

    def body(*refs):
        ins, outs = refs[:n], refs[n:2 * n]
        send_sems, recv_sems, loc_sems = refs[2 * n:]
        x, y, c = lax.axis_index("x"), lax.axis_index("y"), lax.axis_index("c")
        me, sibling = (x, y, c), (x, y, 1 - c)
        chips = [(1 - x, y), (x, 1 - y), (1 - x, 1 - y)]

        def rows(i, dev):
            return outs[i].at[4 * dev[0] + 2 * dev[1] + dev[2]]

        def copy(i, k, block, to, src=None):
            return pltpu.make_async_remote_copy(
                src_ref=rows(i, block) if src is None else src, dst_ref=rows(i, block),
                send_sem=send_sems.at[i * per + k], recv_sem=recv_sems.at[i * per + k],
                device_id=to, device_id_type=pl.DeviceIdType.MESH)

        mine = [pltpu.make_async_copy(ins[i], rows(i, me), loc_sems.at[i]) for i in range(n)]
        for cp in mine:
            cp.start()
        first = []
        for i in range(n):
            first.append(copy(i, 0, me, sibling, src=ins[i]))
            first += [copy(i, 1 + j, me, (*chip, c), src=ins[i]) for j, chip in enumerate(chips)]
        for cp in first:
            cp.start()
        passed = []
        for j, chip in enumerate(chips):
            for i in range(n):
                copy(i, 1 + j, (*chip, c), me).wait_recv()
                fwd = copy(i, 4 + j, (*chip, c), sibling)
                fwd.start()
                passed.append(fwd)
        for i in range(n):
            copy(i, 0, sibling, me).wait_recv()
            for j, chip in enumerate(chips):
                copy(i, 4 + j, (*chip, 1 - c), me).wait_recv()
        for cp in first + passed:
            cp.wait_send()
        for cp in mine:
            cp.wait()

    return pl.pallas_call(
        body, in_specs=[pl.BlockSpec(memory_space=pl.ANY)] * n, out_specs=[pl.BlockSpec(memory_space=pl.ANY)] * n,
        out_shape=_exchange_shapes(arrays, "gather"), scratch_shapes=_exchange_scratch(n), name=name,
    )(*arrays)


def _sum_slots(a, name):
    def body(a_ref, o_ref):
        acc = a_ref[0]
        for s in range(1, N_DEV):
            acc = acc + a_ref[s]
        o_ref[...] = acc

    return pl.pallas_call(body, out_shape=jax.ShapeDtypeStruct(a.shape[1:], F32), name=name,
                          compiler_params=pltpu.CompilerParams(vmem_limit_bytes=VMEM_LIMIT))(a)


def _pack(pieces, mult):
    flat = jnp.concatenate([q.reshape(-1).astype(F32) for q in pieces])
    size = -(-flat.shape[0] // mult) * mult
    return jnp.pad(flat, (0, size - flat.shape[0])).reshape(size // 128, 128)


def _unpack(flat, shapes):
    out, off = [], 0
    for shp in shapes:
        size = math.prod(shp)
        out.append(flat[..., off:off + size].reshape(flat.shape[:-1] + tuple(shp)))
        off += size
    return out


def _mod_part(c_all, ada_w, ada_b_mine, name):
    cols = ada_w.shape[-1]

    def body(c_ref, w_ref, b_ref, o_ref):
        cv = c_ref[...]
        ca = cv * _sigmoid(cv)
        for l in range(2):
            o_ref[l] = jnp.dot(ca, w_ref[l], preferred_element_type=F32, precision=lax.Precision.HIGHEST) + b_ref[l:l + 1, :]

    return pl.pallas_call(body, out_shape=jax.ShapeDtypeStruct((2, N_DEV, cols), F32), name=name,
                          compiler_params=pltpu.CompilerParams(vmem_limit_bytes=VMEM_LIMIT))(c_all, ada_w, ada_b_mine)


def _ada_w_grad(c_all_t, dmod_mine, name):
    cols = dmod_mine.shape[-1]

    def body(ct_ref, d_ref, o_ref):
        cv = ct_ref[...]
        ca = cv * _sigmoid(cv)
        for l in range(2):
            acc = ca[:, 0:1] * d_ref[l, 0:1, :]
            for b in range(1, N_DEV):
                acc = acc + ca[:, b:b + 1] * d_ref[l, b:b + 1, :]
            o_ref[l] = acc

    return pl.pallas_call(body, out_shape=jax.ShapeDtypeStruct((2, D, cols), F32), name=name,
                          compiler_params=pltpu.CompilerParams(vmem_limit_bytes=VMEM_LIMIT))(c_all_t, dmod_mine)


def _adamw_math(w, g, m, v):
    m2 = ADAM_B1 * m + (1.0 - ADAM_B1) * g
    v2 = ADAM_B2 * v + (1.0 - ADAM_B2) * (g * g)
    m_hat = m2 / (1.0 - ADAM_B1 ** ADAM_STEP)
    v_hat = v2 / (1.0 - ADAM_B2 ** ADAM_STEP)
    delta = -ADAM_LR * (m_hat / (jnp.sqrt(v_hat) + ADAM_EPS) + ADAM_WD * w)
    return delta, m2, v2


def _adamw_big(w, g, m, v, name):
    n_l, rows, cols = w.shape
    parts = list(g) if isinstance(g, (list, tuple)) else None
    sizes = (512, 352, 256, 128, 64, 32, 16, 8) if parts is None or len(parts) == 1 else (176, 128, 64, 32, 16, 8)
    br = next(b for b in sizes if rows % b == 0)
    nr = rows // br

    def body(*refs):
        w_ref, g_refs, (m_ref, v_ref, go_ref, d_ref, mo_ref, vo_ref) = refs[0], refs[1:-6], refs[-6:]

        def update(gsum):
            go_ref[...] = gsum
            d_ref[...], mo_ref[...], vo_ref[...] = _adamw_math(w_ref[...], gsum, m_ref[...], v_ref[...])

        if parts is None:
            update(g_refs[0][...])
        else:
            for l, g_ref in enumerate(g_refs):
                @pl.when(pl.program_id(0) == l)
                def _(g_ref=g_ref):
                    gsum = g_ref[0, :, 0:cols].astype(F32)
                    for s in range(1, N_DEV):
                        gsum = gsum + g_ref[s, :, 0:cols].astype(F32)
                    update(gsum)

    blk = pl.BlockSpec((None, br, cols), lambda l, i: (l, i, 0))
    if parts is None:
        g_specs, g_args = [blk], [g]
    else:
        def part_spec(l_mine, width):
            return pl.BlockSpec((N_DEV, br, width),
                                lambda l, i: (0, jnp.where(l < l_mine, 0, jnp.where(l == l_mine, i, nr - 1)), 0))
        g_specs, g_args = [part_spec(l, p.shape[-1]) for l, p in enumerate(parts)], parts
    shp = jax.ShapeDtypeStruct((n_l, rows, cols), F32)
    return pl.pallas_call(
        body, grid=(n_l, nr), in_specs=[blk] + g_specs + [blk, blk], out_specs=[blk] * 4, out_shape=[shp] * 4,
        compiler_params=pltpu.CompilerParams(dimension_semantics=("arbitrary", "arbitrary"), vmem_limit_bytes=VMEM_LIMIT),
        name=name)(w, *g_args, m, v)


def _adamw_small(ws, gs, ms, vs, name):
    n = len(ws)

    def body(*refs):
        for i in range(n):
            w_ref, g_ref, m_ref, v_ref = (refs[k * n + i] for k in range(4))
            d_ref, mo_ref, vo_ref = (refs[(4 + k) * n + i] for k in range(3))
            d_ref[...], mo_ref[...], vo_ref[...] = _adamw_math(w_ref[...], g_ref[...], m_ref[...], v_ref[...])

    shapes = [jax.ShapeDtypeStruct(w.shape, F32) for w in ws]
    outs = pl.pallas_call(body, out_shape=shapes * 3, name=name,
                          compiler_params=pltpu.CompilerParams(vmem_limit_bytes=VMEM_LIMIT))(*ws, *gs, *ms, *vs)
    return outs[:n], outs[n:2 * n], outs[2 * n:]


def _as2d(a):
    return a.reshape(-1, a.shape[-1])


def kernel(x, c, ada_w, ada_b, norm_pre, norm_post, ffn_w13, ffn_w2, ab_w_in, a_conv_w, a_conv_b, a_gate_w, a_gate_b, a_lam, b_conv_w, b_conv_b, b_norm_g, b_norm_b, ab_w_out, c_w_in, c_b_in, c_norm_g, c_norm_b, c_w_s, c_b_s, c_w_out, loss_target, m_ada_w, m_ada_b, m_norm_pre, m_norm_post, m_ffn_w13, m_ffn_w2, m_ab_w_in, m_a_conv_w, m_a_conv_b, m_a_gate_w, m_a_gate_b, m_a_lam, m_b_conv_w, m_b_conv_b, m_b_norm_g, m_b_norm_b, m_ab_w_out, m_c_w_in, m_c_b_in, m_c_norm_g, m_c_norm_b, m_c_w_s, m_c_b_s, m_c_w_out, v_ada_w, v_ada_b, v_norm_pre, v_norm_post, v_ffn_w13, v_ffn_w2, v_ab_w_in, v_a_conv_w, v_a_conv_b, v_a_gate_w, v_a_gate_b, v_a_lam, v_b_conv_w, v_b_conv_b, v_b_norm_g, v_b_norm_b, v_ab_w_out, v_c_w_in, v_c_b_in, v_c_norm_g, v_c_norm_b, v_c_w_s, v_c_b_s, v_c_w_out):
    me = 4 * lax.axis_index("x") + 2 * lax.axis_index("y") + lax.axis_index("c")
    weights = dict(ada_w=ada_w, ada_b=ada_b, norm_pre=norm_pre, norm_post=norm_post, ffn_w13=ffn_w13, ffn_w2=ffn_w2,
                   ab_w_in=ab_w_in, a_conv_w=a_conv_w, a_conv_b=a_conv_b, a_gate_w=a_gate_w, a_gate_b=a_gate_b, a_lam=a_lam,
                   b_conv_w=b_conv_w, b_conv_b=b_conv_b, b_norm_g=b_norm_g, b_norm_b=b_norm_b, ab_w_out=ab_w_out,
                   c_w_in=c_w_in, c_b_in=c_b_in, c_norm_g=c_norm_g, c_norm_b=c_norm_b, c_w_s=c_w_s, c_b_s=c_b_s, c_w_out=c_w_out)
    moms = dict(ada_w=m_ada_w, ada_b=m_ada_b, norm_pre=m_norm_pre, norm_post=m_norm_post, ffn_w13=m_ffn_w13, ffn_w2=m_ffn_w2,
                ab_w_in=m_ab_w_in, a_conv_w=m_a_conv_w, a_conv_b=m_a_conv_b, a_gate_w=m_a_gate_w, a_gate_b=m_a_gate_b,
                a_lam=m_a_lam, b_conv_w=m_b_conv_w, b_conv_b=m_b_conv_b, b_norm_g=m_b_norm_g, b_norm_b=m_b_norm_b,
                ab_w_out=m_ab_w_out, c_w_in=m_c_w_in, c_b_in=m_c_b_in, c_norm_g=m_c_norm_g, c_norm_b=m_c_norm_b,
                c_w_s=m_c_w_s, c_b_s=m_c_b_s, c_w_out=m_c_w_out)
    vars_ = dict(ada_w=v_ada_w, ada_b=v_ada_b, norm_pre=v_norm_pre, norm_post=v_norm_post, ffn_w13=v_ffn_w13, ffn_w2=v_ffn_w2,
                 ab_w_in=v_ab_w_in, a_conv_w=v_a_conv_w, a_conv_b=v_a_conv_b, a_gate_w=v_a_gate_w, a_gate_b=v_a_gate_b,
                 a_lam=v_a_lam, b_conv_w=v_b_conv_w, b_conv_b=v_b_conv_b, b_norm_g=v_b_norm_g, b_norm_b=v_b_norm_b,
                 ab_w_out=v_ab_w_out, c_w_in=v_c_w_in, c_b_in=v_c_b_in, c_norm_g=v_c_norm_g, c_norm_b=v_c_norm_b,
                 c_w_s=v_c_w_s, c_b_s=v_c_b_s, c_w_out=v_c_w_out)
    names = list(weights)

    w13s = ffn_w13.astype(BF).reshape(4, D, FF_SHARD)
    tail, blank = w13s[..., FF_MAIN:], jnp.zeros((4, D, FF_TAIL), BF)
    tail_tile = jnp.where(me % 2 == 1, jnp.concatenate([blank, tail], -1), jnp.concatenate([tail, blank], -1))
    w13b = jnp.concatenate([w13s[..., :FF_MAIN], tail_tile], -1)
    small_shapes = [(D,), (2, 3, 128), (2, 3, 128), (CONV_A, 64), (CONV_B, 64), (256,), (128,), (128,)]
    small = _pack([c, norm_pre, norm_post, a_conv_w, b_conv_w, c_b_in, c_norm_g, c_norm_b], 1024)
    w2b = ffn_w2.astype(BF).reshape(4, W2_SHARD, D)
    shards = {f"f{f}": [w13b[f], w2b[f]] for f in range(4)}
    shards["ab"] = [ab_w_in[0].astype(BF), ab_w_out[0].astype(BF)]
    shards["c"] = [c_w_in[0].astype(BF), c_w_out[0].astype(BF)]
    w13g0, w2g0, small_g = _gather_two_level(shards["f0"] + [small], "gather_first")
    plan = dict(shards=shards, gather={0: ["ab", "f1"], 2: ["f2"], 3: ["c", "f3"]},
                scatter={"ffn_bwd_l1s0": ["f3.0", "f3.1", "c.0", "c.1"], "ffn_bwd_l0s2": ["f2.0", "f2.1"],
                         "mixab_bwd_l0s1": ["f1.0", "f1.1"], "wgrad_w13_l0s0": ["ab.0", "ab.1"], "wgrad_w2_l0s0": ["f0.0"]})
    c_all, npre_g, npost_g, acw_g, bcw_g, cbin_g, cng_g, cnb_g = _unpack(small_g.reshape(N_DEV, -1), small_shapes)

    def cat_last(a):
        return jnp.moveaxis(a, 0, -2).reshape(a.shape[1:-1] + (N_DEV * a.shape[-1],))

    ada_b_mine = lax.dynamic_slice_in_dim(ada_b, me * ada_w.shape[-1], ada_w.shape[-1], axis=1)
    (mod_g,) = _exchange([_mod_part(c_all, ada_w, ada_b_mine, "mod_part")], "gather", "gather_mod")
    mod = cat_last(lax.dynamic_index_in_dim(mod_g, me, axis=2, keepdims=False)).reshape(2, 3, 3, D)

    p = dict(mod=mod, norm_pre=cat_last(npre_g), norm_post=cat_last(npost_g), wsets={"f0": [w13g0, w2g0]},
             a_conv_w=cat_last(acw_g), a_conv_b=a_conv_b[0], a_gate_w=a_gate_w[0], a_gate_b=a_gate_b[0], a_lam=a_lam[0],
             b_conv_w=cat_last(bcw_g), b_conv_b=b_conv_b[0], b_norm_g=b_norm_g[0], b_norm_b=b_norm_b[0],
             c_b_in=cat_last(cbin_g), c_norm_g=cat_last(cng_g), c_norm_b=cat_last(cnb_g), c_w_s=c_w_s[0], c_b_s=c_b_s[0])

    loss_blk, grad_x, g = _local_step(x[0], loss_target[0], p, plan)
    loss = lax.psum(loss_blk[0, 0], ("x", "y", "c"))

    accs = g["accs"]
    dmod = jnp.stack([jnp.stack([accs[f"{l}{j}"][0:3] for j in range(3)]) for l in range(2)])
    dnpre = jnp.stack([jnp.stack([accs[f"{l}{j}"][3] for j in range(3)]) for l in range(2)])
    dnpost = jnp.stack([jnp.stack([accs[f"{l}{j}"][4] for j in range(3)]) for l in range(2)])
    sab = g["accs_ab"]
    half = W_A // 8
    dgate = g["gate"][0]
    dgw = jnp.stack([jnp.concatenate([dgate[half * hh:half * (hh + 1), half * hh:half * (hh + 1)],
                                      dgate[half * hh:half * (hh + 1), W_A + half * hh:W_A + half * (hh + 1)]], axis=1)
                     for hh in range(8)])
    dgb = jnp.concatenate([sab[5].reshape(8, half), sab[6].reshape(8, half)], axis=1)
    c_acc, c_dbin, c_dws, c_dbst = g["c_small"]
    red_shapes = [(2, 9216), (2, 3, D), (2, 3, D), (CONV_A, W_A), (W_A,), (8, half, 2 * half), (8, 2 * half), (W_A,),
                  (CONV_B, W_B), (W_B,), (W_B,), (W_B,), (2 * D,), (D,), (D,), (H_C, CHUNK, CHUNK), (H_C, CHUNK)]
    red = _pack([dmod.reshape(2, 9216), dnpre, dnpost, sab[0:4], sab[4], dgw, dgb, sab[7], g["dw31"][0:CONV_B], sab[8],
                 sab[9], sab[10], c_dbin[0], c_acc[5], c_acc[6], c_dws, jnp.transpose(c_dbst)], N_DEV * 1024)
    left = sorted(g["pending"])
    red_r, *last_recv = _exchange([red.reshape(N_DEV, -1, 128)] + [g["pending"][k] for k in left], "scatter",
                                  "scatter_small_grads")
    red_all, dmod_all = _exchange([_sum_slots(red_r, "sum_small_grads"), dmod.reshape(-1, 128)], "gather", "gather_small_grads")
    red_sum = red_all.reshape(-1)
    (g_ada_b, g_npre, g_npost, g_acw, g_acb, g_agw, g_agb, g_alam, g_bcw, g_bcb, g_bng, g_bnb, g_cbin, g_cng, g_cnb,
     g_cws, g_cbs) = _unpack(red_sum, red_shapes)
    dmod_all = dmod_all.reshape(N_DEV, 2, 9216)
    ncol = ada_w.shape[-1]
    dmod_mine = jnp.moveaxis(lax.dynamic_slice_in_dim(dmod_all, me * ncol, ncol, axis=2), 0, 1)
    g_ada_w = _ada_w_grad(jnp.transpose(c_all), dmod_mine, "ada_w_grad")

    def mine(a, width):
        return lax.dynamic_slice_in_dim(a, me * width, width, axis=a.ndim - 1)

    small_grads = dict(
        ada_b=g_ada_b, norm_pre=mine(g_npre, 128), norm_post=mine(g_npost, 128), a_conv_w=mine(g_acw, 64)[None],
        a_conv_b=g_acb[None], a_gate_w=g_agw[None], a_gate_b=g_agb[None], a_lam=g_alam[None], b_conv_w=mine(g_bcw, 64)[None],
        b_conv_b=g_bcb[None], b_norm_g=g_bng[None], b_norm_b=g_bnb[None], c_b_in=mine(g_cbin, 256)[None],
        c_norm_g=mine(g_cng, 128)[None], c_norm_b=mine(g_cnb, 128)[None], c_w_s=g_cws[None], c_b_s=g_cbs[None])

    recv = dict(g["recv"])
    recv.update(zip(left, last_recv))
    big_partials = dict(ffn_w13=[recv[f"f{f}.0"] for f in range(4)], ffn_w2=[recv[f"f{f}.1"] for f in range(4)],
                        ab_w_in=[recv["ab.0"]], ab_w_out=[recv["ab.1"]], c_w_in=[recv["c.0"]], c_w_out=[recv["c.1"]],
                        ada_w=g_ada_w)

    grads, deltas, new_m, new_v = {}, {}, {}, {}

    def as3d(a):
        return a.reshape((-1,) + a.shape[-2:])

    for nm, gp in big_partials.items():
        shp = weights[nm].shape
        go, dl, mo, vo = _adamw_big(as3d(weights[nm]), gp, as3d(moms[nm]), as3d(vars_[nm]), "adamw_" + nm)
        grads[nm], deltas[nm], new_m[nm], new_v[nm] = (a.reshape(shp) for a in (go, dl, mo, vo))
    snames = list(small_grads)
    dls, mos, vos = _adamw_small([_as2d(weights[nm]) for nm in snames], [_as2d(small_grads[nm]) for nm in snames],
                                 [_as2d(moms[nm]) for nm in snames], [_as2d(vars_[nm]) for nm in snames], "adamw_small")
    for k, nm in enumerate(snames):
        shp = weights[nm].shape
        grads[nm] = small_grads[nm].reshape(shp)
        deltas[nm], new_m[nm], new_v[nm] = dls[k].reshape(shp), mos[k].reshape(shp), vos[k].reshape(shp)

    return (loss, grad_x[None], *[grads[nm] for nm in names], *[deltas[nm] for nm in names],
            *[new_m[nm] for nm in names], *[new_v[nm] for nm in names])
```

```python
import math

import jax
import jax.numpy as jnp
from jax import lax
from jax.experimental import pallas as pl
from jax.experimental.pallas import tpu as pltpu

F32 = jnp.float32
BF = jnp.bfloat16

N_DEV = 8
D = 1024
EPS = 1e-6
D_FF = 2816
FF_SHARD = 704
FF_PAD = 768
FF_MAIN = 640
FF_TAIL = FF_SHARD - FF_MAIN
W2_SHARD = 352
W_A = 512
W_B = 512
AB_SAVED = 9 * W_A
CONV_A = 4
CONV_B = 31
HALO_A = 8
HALO_B = 32
LRU_C = 8.0
CHUNK = 128
H_C = 8
ADAM_LR = 0.001
ADAM_B1 = 0.9
ADAM_B2 = 0.999
ADAM_EPS = 1e-08
ADAM_WD = 0.01
ADAM_STEP = 10
VMEM_LIMIT = 62 * 1024 * 1024
GELU_C = math.sqrt(2.0 / math.pi)

TM_FFN = 512
TM_FFN_BWD = 256
TM_MIX = 256
TM_MIXC_FWD = 512
TK_WGRAD = 2048


def _params(limit=VMEM_LIMIT):
    return pltpu.CompilerParams(dimension_semantics=("arbitrary",), vmem_limit_bytes=limit)


def _dot(a, b):
    return jnp.dot(a, b, preferred_element_type=F32)


def _dot_nt(a, b):
    return lax.dot_general(a, b, (((1,), (1,)), ((), ())), preferred_element_type=F32)


def _dot_tn(a, b):
    return lax.dot_general(a, b, (((0,), (0,)), ((), ())), preferred_element_type=F32)


def _sigmoid(x):
    return 0.5 + 0.5 * jnp.tanh(0.5 * x)


def _gelu(x):
    t = jnp.tanh(GELU_C * (x + 0.044715 * x * x * x))
    return 0.5 * x * (1.0 + t), t


def _gelu_grad(x, t):
    return 0.5 * (1.0 + t) + 0.5 * x * (1.0 - t * t) * GELU_C * (1.0 + 3.0 * 0.044715 * x * x)


def _rms(x):
    r = lax.rsqrt(jnp.mean(x * x, axis=-1, keepdims=True) + EPS)
    return x * r, r


def _colsum(x):
    return jnp.sum(x, axis=0, keepdims=True)


def _pre_fwd(x, vec_ref):
    xn, r = _rms(x)
    n = xn * vec_ref[3:4, :]
    h = n * (1.0 + vec_ref[1:2, :]) + vec_ref[0:1, :]
    return h, xn, r, n


def _post_fwd(x, f, vec_ref, res_w):
    fn, _ = _rms(f)
    return x + fn * ((res_w * (1.0 + vec_ref[2:3, :])) * vec_ref[4:5, :])


def _post_bwd(dout, f, vec_ref, acc_ref, res_w):
    fn, r2 = _rms(f)
    acc_ref[2:3, :] += _colsum(dout * fn)
    dfn = dout * ((res_w * (1.0 + vec_ref[2:3, :])) * vec_ref[4:5, :])
    return r2 * (dfn - fn * jnp.mean(dfn * fn, axis=-1, keepdims=True))


def _pre_bwd(dout, dh, xn, r, vec_ref, acc_ref):
    acc_ref[0:1, :] += _colsum(dh)
    acc_ref[1:2, :] += _colsum(dh * xn)
    dxn = dh * ((1.0 + vec_ref[1:2, :]) * vec_ref[3:4, :])
    return dout + r * (dxn - xn * jnp.mean(dxn * xn, axis=-1, keepdims=True))


def _finish_acc(acc_ref, vec_ref, res_w):
    s_pre, s_post = acc_ref[1:2, :], acc_ref[2:3, :]
    acc_ref[1:2, :] = vec_ref[3:4, :] * s_pre
    acc_ref[3:4, :] = (1.0 + vec_ref[1:2, :]) * s_pre
    acc_ref[2:3, :] = (res_w * vec_ref[4:5, :]) * s_post
    acc_ref[4:5, :] = (res_w * (1.0 + vec_ref[2:3, :])) * s_post


def _tile(tm, ncol):
    return pl.BlockSpec((tm, ncol), lambda i: (i, 0))


def _full(shape):
    return pl.BlockSpec(shape, lambda i: (0,) * len(shape))


def _any():
    return pl.BlockSpec(memory_space=pl.ANY)


def _load_ffn_weights(w13_hbm, w2_hbm, w13_v, w2_v, tails, sems):
    copies = []
    for j in range(N_DEV):
        half, k = divmod(j, 4)
        copies.append((w13_hbm.at[j, :, pl.ds(0, FF_MAIN)], w13_v.at[:, pl.ds(D_FF * half + FF_MAIN * k, FF_MAIN)]))
        copies.append((w13_hbm.at[j, :, pl.ds(FF_MAIN, 128)], tails.at[j]))
    for k in range(4):
        copies.append((w2_hbm.at[2 * k], w2_v.at[pl.ds(FF_MAIN * k, W2_SHARD), :]))
        copies.append((w2_hbm.at[2 * k + 1, pl.ds(0, FF_MAIN - W2_SHARD), :],
                       w2_v.at[pl.ds(FF_MAIN * k + W2_SHARD, FF_MAIN - W2_SHARD), :]))
        copies.append((w2_hbm.at[2 * k + 1, pl.ds(FF_MAIN - W2_SHARD, FF_TAIL), :],
                       w2_v.at[pl.ds(4 * FF_MAIN + FF_TAIL * k, FF_TAIL), :]))
    copies = [pltpu.make_async_copy(src, dst, sems.at[n]) for n, (src, dst) in enumerate(copies)]
    for cp in copies:
        cp.start()
    for cp in copies:
        cp.wait()
    for pair in range(4):
        half, kk = divmod(pair, 2)
        base = D_FF * half + 4 * FF_MAIN + 128 * kk
        w13_v[:, base:base + 128] = tails[2 * pair] + tails[2 * pair + 1]


_FFN_SCRATCH = [pltpu.VMEM((D, 2 * D_FF), BF), pltpu.VMEM((D_FF, D), BF), pltpu.VMEM((N_DEV, D, 128), BF),
                pltpu.SemaphoreType.DMA((2 * N_DEV + 12,))]
HID_CHUNKS = ((0, 768), (768, 768), (1536, 768), (2304, 512))


def _hosted(body, n_in, n_out, n_scratch, n_comm, mode, grid):
    if not n_comm:
        return body

    def at(corner):
        hit = pl.program_id(0) == corner[0]
        for d in range(1, len(grid)):
            hit = hit & (pl.program_id(d) == corner[d])
        return hit

    def hosted(*refs):
        ins, cin = refs[:n_in], refs[n_in:n_in + n_comm]
        outs, cout = refs[n_in + n_comm:n_in + n_comm + n_out], refs[n_in + n_comm + n_out:n_in + 2 * n_comm + n_out]
        scratch = refs[n_in + 2 * n_comm + n_out:]
        own, sems = scratch[:n_scratch], scratch[n_scratch:]

        @pl.when(at([0] * len(grid)))
        def _():
            _exchange_ops(cin, cout, sems, mode, "start")

        body(*ins, *outs, *own)

        @pl.when(at([n - 1 for n in grid]))
        def _():
            _exchange_ops(cin, cout, sems, mode, "wait")

    return hosted


def _ffn_fwd(x, vec, w13g, w2g, res_w, name, comm=(), tgt=None):
    t_len = x.shape[0]
    tm = min(TM_FFN, t_len)
    nc = len(comm)
    head = tgt is not None

    def body(*refs):
        if head:
            x_ref, vec_ref, w13_hbm, w2_hbm, t_ref, xo_ref, f_ref, jac_ref, s_ref, loss_ref, w13_v, w2_v, tails, sems = refs
        else:
            x_ref, vec_ref, w13_hbm, w2_hbm, xo_ref, f_ref, jac_ref, s_ref, w13_v, w2_v, tails, sems = refs

        @pl.when(pl.program_id(0) == 0)
        def _():
            _load_ffn_weights(w13_hbm, w2_hbm, w13_v, w2_v, tails, sems)
            if head:
                loss_ref[...] = jnp.zeros((8, 128), F32)

        x_t = x_ref[...]
        h, _, _, _ = _pre_fwd(x_t, vec_ref)
        hb = h.astype(BF)
        for c0, cw in HID_CHUNKS:
            g = _dot(hb, w13_v[:, c0:c0 + cw])
            u = _dot(hb, w13_v[:, D_FF + c0:D_FF + c0 + cw])
            sig = _sigmoid(g)
            sl = g * sig
            jac_ref[0, :, c0:c0 + cw] = (u * (sig + sl * (1.0 - sig))).astype(BF)
            jac_ref[1, :, c0:c0 + cw] = sl.astype(BF)
            s_ref[:, c0:c0 + cw] = (sl * u).astype(BF)
        acc = _dot(s_ref[...], w2_v[...])
        f_ref[...] = acc
        xo = _post_fwd(x_t, acc, vec_ref, res_w)
        if head:
            err = xo - t_ref[...]
            xo_ref[...] = err * (1.0 / D)
            loss_ref[...] += jnp.sum(err * err) * (0.5 / D)
        else:
            xo_ref[...] = xo

    nt = t_len // tm
    n_in, n_out = (5, 5) if head else (4, 4)
    outs = pl.pallas_call(
        _hosted(body, n_in, n_out, 4, nc, "gather", (nt,)), grid=(nt,),
        in_specs=[_tile(tm, D), _full((8, D)), _any(), _any()] + ([_tile(tm, D)] if head else []) + [_any()] * nc,
        out_specs=[_tile(tm, D), _tile(tm, D), pl.BlockSpec((2, tm, D_FF), lambda i: (0, i, 0)), _tile(tm, D_FF)]
        + ([_full((8, 128))] if head else []) + [_any()] * nc,
        out_shape=[jax.ShapeDtypeStruct((t_len, D), F32), jax.ShapeDtypeStruct((t_len, D), F32),
                   jax.ShapeDtypeStruct((2, t_len, D_FF), BF), jax.ShapeDtypeStruct((t_len, D_FF), BF)]
        + ([jax.ShapeDtypeStruct((8, 128), F32)] if head else []) + _exchange_shapes(comm, "gather"),
        scratch_shapes=_FFN_SCRATCH + (_exchange_scratch(nc) if nc else []), compiler_params=_params(), name=name,
    )(x, vec, w13g, w2g, *([tgt] if head else []), *comm)
    return outs[:n_out], outs[n_out:]


def _ffn_bwd(dout, x, fpre, jac, vec, w13g, w2g, res_w, name, comm=()):
    t_len = x.shape[0]
    tm = min(TM_FFN_BWD, t_len)
    nt = t_len // tm
    nc = len(comm)

    def body(dout_ref, x_ref, f_ref, jac_ref, vec_ref, w13_hbm, w2_hbm,
             dx_ref, dgu_ref, hb_ref, dfb_ref, acc_ref, w13_v, w2_v, tails, sems):
        @pl.when(pl.program_id(0) == 0)
        def _():
            _load_ffn_weights(w13_hbm, w2_hbm, w13_v, w2_v, tails, sems)
            acc_ref[...] = jnp.zeros((8, D), F32)

        dout_t = dout_ref[...]
        df = _post_bwd(dout_t, f_ref[...], vec_ref, acc_ref, res_w)
        dfb = df.astype(BF)
        dfb_ref[...] = dfb
        h, xn, r, _ = _pre_fwd(x_ref[...], vec_ref)
        hb_ref[...] = h.astype(BF)
        for c0, cw in HID_CHUNKS:
            ds = _dot_nt(dfb, w2_v[c0:c0 + cw, :]).astype(BF)
            dgu_ref[:, c0:c0 + cw] = ds * jac_ref[0, :, c0:c0 + cw]
            dgu_ref[:, D_FF + c0:D_FF + c0 + cw] = ds * jac_ref[1, :, c0:c0 + cw]
        dh = _dot_nt(dgu_ref[...], w13_v[...])
        dx_ref[...] = _pre_bwd(dout_t, dh, xn, r, vec_ref, acc_ref)

        @pl.when(pl.program_id(0) == nt - 1)
        def _():
            _finish_acc(acc_ref, vec_ref, res_w)

    jac_spec = pl.BlockSpec((2, tm, D_FF), lambda i: (0, i, 0))
    outs = pl.pallas_call(
        _hosted(body, 7, 5, 4, nc, "scatter", (nt,)), grid=(nt,),
        in_specs=[_tile(tm, D), _tile(tm, D), _tile(tm, D), jac_spec, _full((8, D)), _any(), _any()] + [_any()] * nc,
        out_specs=[_tile(tm, D), _tile(tm, 2 * D_FF), _tile(tm, D), _tile(tm, D), _full((8, D))] + [_any()] * nc,
        out_shape=[jax.ShapeDtypeStruct((t_len, D), F32), jax.ShapeDtypeStruct((t_len, 2 * D_FF), BF),
                   jax.ShapeDtypeStruct((t_len, D), BF),
                   jax.ShapeDtypeStruct((t_len, D), BF), jax.ShapeDtypeStruct((8, D), F32)] + _exchange_shapes(comm, "scatter"),
        scratch_shapes=_FFN_SCRATCH + (_exchange_scratch(nc) if nc else []), compiler_params=_params(), name=name,
    )(dout, x, fpre, jac, vec, w13g, w2g, *comm)
    return outs[:5], outs[5:]


def _w13_slots(acc, o_ref):
    for k in range(4):
        o_ref[k, :, 0:FF_MAIN] = acc[:, FF_MAIN * k:FF_MAIN * (k + 1)].astype(BF)
        pair_tile = acc[:, 4 * FF_MAIN + 128 * (k // 2):4 * FF_MAIN + 128 * (k // 2 + 1)]
        o_ref[k, :, FF_MAIN:FF_PAD] = (pair_tile if k % 2 == 0 else pltpu.roll(pair_tile, FF_TAIL, 1)).astype(BF)


def _w2_slots(acc, o_ref):
    rest = FF_MAIN - W2_SHARD
    for k in range(4):
        o_ref[2 * k] = acc[FF_MAIN * k:FF_MAIN * k + W2_SHARD, :].astype(BF)
        o_ref[2 * k + 1, 0:rest, :] = acc[FF_MAIN * k + W2_SHARD:FF_MAIN * (k + 1), :].astype(BF)
        o_ref[2 * k + 1, rest:W2_SHARD, :] = acc[4 * FF_MAIN + FF_TAIL * k:4 * FF_MAIN + FF_TAIL * (k + 1), :].astype(BF)


def _wgrad(a, b, j_count, m, n, a_mode, b_mode, out_rows, out_dtype, name, tk=TK_WGRAD, col_slots=1, comm=(),
           slots=None):
    t_len = a.shape[-2]
    tk = min(tk, t_len)
    nk = t_len // tk
    wn = n // col_slots
    nc = len(comm)

    def spec(mode, width):
        if mode == "stack":
            return pl.BlockSpec((None, tk, width), lambda j, t: (j, t, 0))
        if mode == "cols":
            return pl.BlockSpec((tk, width), lambda j, t: (t, j))
        return pl.BlockSpec((tk, width), lambda j, t: (t, 0))

    def body(a_ref, b_ref, o_ref, acc):
        t = pl.program_id(1)

        @pl.when(t == 0)
        def _():
            acc[...] = jnp.zeros((m, n), F32)

        acc[...] += _dot_tn(a_ref[...], b_ref[...])

        @pl.when(t == nk - 1)
        def _():
            if slots is not None:
                slots[0](acc, o_ref)
            elif col_slots == 1:
                o_ref[...] = acc[0:out_rows, :].astype(out_dtype)
            else:
                for s in range(col_slots):
                    o_ref[s] = acc[0:out_rows, wn * s:wn * (s + 1)].astype(out_dtype)

    if slots is not None:
        blk = slots[1]
        out_spec = pl.BlockSpec(blk, lambda j, t: (j,) + (0,) * (len(blk) - 1))
        out_shape = jax.ShapeDtypeStruct((j_count * blk[0],) + blk[1:], BF)
    elif col_slots == 1:
        out_spec = pl.BlockSpec((None, out_rows, n), lambda j, t: (j, 0, 0))
        out_shape = jax.ShapeDtypeStruct((j_count, out_rows, n), out_dtype)
    else:
        out_spec = pl.BlockSpec((col_slots, out_rows, wn), lambda j, t: (0, 0, 0))
        out_shape = jax.ShapeDtypeStruct((col_slots, out_rows, wn), out_dtype)
    outs = pl.pallas_call(
        _hosted(body, 2, 1, 1, nc, "scatter", (j_count, nk)), grid=(j_count, nk),
        in_specs=[spec(a_mode, m), spec(b_mode, n)] + [_any()] * nc,
        out_specs=[out_spec] + [_any()] * nc, out_shape=[out_shape] + _exchange_shapes(comm, "scatter"),
        scratch_shapes=[pltpu.VMEM((m, n), F32)] + (_exchange_scratch(nc) if nc else []),
        compiler_params=pltpu.CompilerParams(dimension_semantics=("arbitrary", "arbitrary"), vmem_limit_bytes=VMEM_LIMIT),
        name=name,
    )(a, b, *comm)
    return (outs[0], list(outs[1:])) if nc else outs[0]


def _c_mask_weights(ws_ref, wsm, wsmt):
    row = lax.broadcasted_iota(jnp.int32, (CHUNK, CHUNK), 0)
    col = lax.broadcasted_iota(jnp.int32, (CHUNK, CHUNK), 1)
    for hh in range(H_C):
        w = jnp.where(row >= col, ws_ref[hh], 0.0)
        wsm[hh] = w.astype(BF)
        if wsmt is not None:
            wsmt[hh] = w.T.astype(BF)


def _c_inner(pre, cvec_ref, wsm, bst_ref, mix_sc, tm):
    z, t = _gelu(pre)
    u = z[:, 0:D]
    v = z[:, D:2 * D]
    mu = jnp.mean(v, axis=-1, keepdims=True)
    vc = v - mu
    rstd = lax.rsqrt(jnp.mean(vc * vc, axis=-1, keepdims=True) + EPS)
    vhat = vc * rstd
    vnb = (vhat * cvec_ref[0:1, :] + cvec_ref[1:2, :]).astype(BF)
    for nn in range(tm // CHUNK):
        for hh in range(H_C):
            rows = slice(CHUNK * nn, CHUNK * (nn + 1))
            cols = slice(CHUNK * hh, CHUNK * (hh + 1))
            mix_sc[rows, cols] = _dot(wsm[hh], vnb[rows, cols]) + bst_ref[:, hh:hh + 1]
    return u, t, rstd, vhat, vnb


def _mixc_fwd(x, vec, w_in, b_in, cvec, ws, bst, w_out, name):
    t_len = x.shape[0]
    tm = min(TM_MIXC_FWD, t_len)

    def body(x_ref, vec_ref, win_ref, bin_ref, cvec_ref, ws_ref, bst_ref, wout_ref,
             xo_ref, f_ref, pre_ref, wsm, mix_sc):
        @pl.when(pl.program_id(0) == 0)
        def _():
            _c_mask_weights(ws_ref, wsm, None)

        x_t = x_ref[...]
        h, _, _, _ = _pre_fwd(x_t, vec_ref)
        hb = h.astype(BF)
        for j in range(N_DEV):
            cols = slice(256 * j, 256 * (j + 1))
            pre_ref[:, cols] = _dot(hb, win_ref[j]) + bin_ref[:, cols]
        u, _, _, _, _ = _c_inner(pre_ref[...], cvec_ref, wsm, bst_ref, mix_sc, tm)
        fpre = _dot((u * mix_sc[...]).astype(BF), wout_ref[...])
        f_ref[...] = fpre
        xo_ref[...] = _post_fwd(x_t, fpre, vec_ref, 1.0)

    return pl.pallas_call(
        body, grid=(t_len // tm,),
        in_specs=[_tile(tm, D), _full((8, D)), _full((N_DEV, D, 256)), _full((1, 2 * D)), _full((8, D)),
                  _full((H_C, CHUNK, CHUNK)), _full((CHUNK, H_C)), _full((D, D))],
        out_specs=[_tile(tm, D), _tile(tm, D), _tile(tm, 2 * D)],
        out_shape=[jax.ShapeDtypeStruct((t_len, D), F32), jax.ShapeDtypeStruct((t_len, D), F32),
                   jax.ShapeDtypeStruct((t_len, 2 * D), F32)],
        scratch_shapes=[pltpu.VMEM((H_C, CHUNK, CHUNK), BF), pltpu.VMEM((tm, D), F32)],
        compiler_params=_params(), name=name,
    )(x, vec, w_in, b_in, cvec, ws, bst, w_out)


def _mixc_bwd(dout, x, fpre, pre, vec, w_in, cvec, ws, bst, w_out, name):
    t_len = x.shape[0]
    tm = min(TM_MIX, t_len)
    nt = t_len // tm

    def body(dout_ref, x_ref, f_ref, pre_ref, vec_ref, win_ref, cvec_ref, ws_ref, bst_ref, wout_ref,
             dx_ref, dpre_ref, p_ref, hb_ref, dfb_ref, acc_ref, dbin_ref, dws_ref, dbst_ref,
             wsm, wsmt, mix_sc, dvn_sc, dmsum, win_v):
        i = pl.program_id(0)

        @pl.when(i == 0)
        def _():
            _c_mask_weights(ws_ref, wsm, wsmt)
            for j in range(N_DEV):
                win_v[:, 256 * j:256 * (j + 1)] = win_ref[j]
            acc_ref[...] = jnp.zeros((8, D), F32)
            dbin_ref[...] = jnp.zeros((8, 2 * D), F32)
            dws_ref[...] = jnp.zeros((H_C, CHUNK, CHUNK), F32)
            dmsum[...] = jnp.zeros((CHUNK, D), F32)

        dout_t = dout_ref[...]
        df = _post_bwd(dout_t, f_ref[...], vec_ref, acc_ref, 1.0)
        dfb = df.astype(BF)
        dfb_ref[...] = dfb
        h, xn, r, n = _pre_fwd(x_ref[...], vec_ref)
        hb_ref[...] = h.astype(BF)
        pre_t = pre_ref[...]
        u, t, rstd, vhat, vnb = _c_inner(pre_t, cvec_ref, wsm, bst_ref, mix_sc, tm)
        mix = mix_sc[...]
        p_ref[...] = (u * mix).astype(BF)
        dp = _dot_nt(dfb, wout_ref[...])
        du = dp * mix
        dmix = dp * u
        dmb = dmix.astype(BF)
        for nn in range(tm // CHUNK):
            rows = slice(CHUNK * nn, CHUNK * (nn + 1))
            dmsum[...] += dmix[rows, :]
            for hh in range(H_C):
                cols = slice(CHUNK * hh, CHUNK * (hh + 1))
                dvn_sc[rows, cols] = _dot(wsmt[hh], dmb[rows, cols])
                dws_ref[hh] += _dot_nt(dmb[rows, cols], vnb[rows, cols])
        dvn = dvn_sc[...]
        acc_ref[5:6, :] += _colsum(dvn * vhat)
        acc_ref[6:7, :] += _colsum(dvn)
        dvhat = dvn * cvec_ref[0:1, :]
        dv = rstd * (dvhat - jnp.mean(dvhat, axis=-1, keepdims=True)
                     - vhat * jnp.mean(dvhat * vhat, axis=-1, keepdims=True))
        gg = _gelu_grad(pre_t, t)
        dpre_u = du * gg[:, 0:D]
        dpre_v = dv * gg[:, D:2 * D]
        dbin_ref[0:1, 0:D] += _colsum(dpre_u)
        dbin_ref[0:1, D:2 * D] += _colsum(dpre_v)
        dpre_ref[:, 0:D] = dpre_u.astype(BF)
        dpre_ref[:, D:2 * D] = dpre_v.astype(BF)
        dh = _dot_nt(dpre_ref[...], win_v[...])
        dx_ref[...] = _pre_bwd(dout_t, dh, xn, r, vec_ref, acc_ref)

        @pl.when(i == nt - 1)
        def _():
            _finish_acc(acc_ref, vec_ref, 1.0)
            row = lax.broadcasted_iota(jnp.int32, (CHUNK, CHUNK), 0)
            col = lax.broadcasted_iota(jnp.int32, (CHUNK, CHUNK), 1)
            for hh in range(H_C):
                dws_ref[hh] = jnp.where(row >= col, dws_ref[hh], 0.0)
                dbst_ref[:, hh:hh + 1] = jnp.sum(dmsum[:, CHUNK * hh:CHUNK * (hh + 1)], axis=1, keepdims=True)

    return pl.pallas_call(
        body, grid=(nt,),
        in_specs=[_tile(tm, D), _tile(tm, D), _tile(tm, D), _tile(tm, 2 * D), _full((8, D)), _full((N_DEV, D, 256)),
                  _full((8, D)), _full((H_C, CHUNK, CHUNK)), _full((CHUNK, H_C)), _full((D, D))],
        out_specs=[_tile(tm, D), _tile(tm, 2 * D), _tile(tm, D), _tile(tm, D), _tile(tm, D), _full((8, D)),
                   _full((8, 2 * D)), _full((H_C, CHUNK, CHUNK)), _full((CHUNK, H_C))],
        out_shape=[jax.ShapeDtypeStruct((t_len, D), F32), jax.ShapeDtypeStruct((t_len, 2 * D), BF),
                   jax.ShapeDtypeStruct((t_len, D), BF), jax.ShapeDtypeStruct((t_len, D), BF),
                   jax.ShapeDtypeStruct((t_len, D), BF), jax.ShapeDtypeStruct((8, D), F32),
                   jax.ShapeDtypeStruct((8, 2 * D), F32), jax.ShapeDtypeStruct((H_C, CHUNK, CHUNK), F32),
                   jax.ShapeDtypeStruct((CHUNK, H_C), F32)],
        scratch_shapes=[pltpu.VMEM((H_C, CHUNK, CHUNK), BF), pltpu.VMEM((H_C, CHUNK, CHUNK), BF),
                        pltpu.VMEM((tm, D), F32), pltpu.VMEM((tm, D), F32), pltpu.VMEM((CHUNK, D), F32),
                        pltpu.VMEM((D, 2 * D), BF)],
        compiler_params=_params(), name=name,
    )(dout, x, fpre, pre, vec, w_in, cvec, ws, bst, w_out)


def _gmean(x, g_ref):
    hi = x.astype(BF)
    lo = (x - hi.astype(F32)).astype(BF)
    return _dot(hi, g_ref[...]) + _dot(lo, g_ref[...])


def _log_sigmoid(lam):
    e = jnp.exp(-jnp.abs(lam))
    log1p = jnp.where(e < 1e-2, e * (1.0 - e * (0.5 - e * (1.0 / 3.0 - 0.25 * e))), jnp.log(1.0 + e))
    return jnp.minimum(lam, 0.0) - log1p


def _neg_expm1(y):
    series = -(y * (1.0 + y * (0.5 + y * (1.0 / 6.0 + y * (1.0 / 24.0 + y * (1.0 / 120.0))))))
    return jnp.where(y > -0.1, series, 1.0 - jnp.exp(y))


def _rows_from(e, off, tm):
    return e[off:off + tm, :] if off % 8 == 0 else pltpu.roll(e, e.shape[0] - off, 0)[0:tm, :]


def _conv_causal(ext, taps_ref, bias, k_taps, halo, tm):
    e = ext[...]
    acc = bias
    for k in range(k_taps):
        acc = acc + taps_ref[k:k + 1, :] * _rows_from(e, halo - k_taps + 1 + k, tm)
    return acc


def _build_shifted(sh_ref, e, n_rows):
    sh_ref[0] = e
    for r in range(1, 8):
        sh_ref[r] = pltpu.roll(e, n_rows - r, 0)


def _shifted_rows(sh_ref, off, tm):
    base = off - off % 8
    return sh_ref[off % 8, base:base + tm, :]


def _scan(a, u, tm, reverse):
    row = lax.broadcasted_iota(jnp.int32, (tm, W_A), 0)
    d = 1
    while d < tm:
        if reverse:
            keep = row < tm - d
            shift = tm - d
        else:
            keep = row >= d
            shift = d
        a_sh = jnp.where(keep, pltpu.roll(a, shift, 0), 1.0)
        u_sh = jnp.where(keep, pltpu.roll(u, shift, 0), 0.0)
        u = a * u_sh + u
        a = a * a_sh
        d *= 2
    return a, u


def _a_gates(xc, cv_ref, wr_ref, wi_ref):
    xcb = xc.astype(BF)
    r = _sigmoid(_dot(xcb, wr_ref[...]) + cv_ref[5:6, :])
    ig = _sigmoid(_dot(xcb, wi_ref[...]) + cv_ref[6:7, :])
    ls = _log_sigmoid(cv_ref[7:8, :])
    la = LRU_C * r * ls
    a = jnp.exp(la)
    m = jnp.sqrt(_neg_expm1(2.0 * la))
    return xcb, r, ig, ls, a, m


def _b_norm(vc, cv_ref, g_ref):
    mu = _gmean(vc, g_ref)
    dv = vc - mu
    rstd = lax.rsqrt(_gmean(dv * dv, g_ref) + EPS)
    vhat = dv * rstd
    vln = vhat * cv_ref[9:10, :] + cv_ref[10:11, :]
    return rstd, vhat, vln


def _mixab_fwd(x, vec, w_in, cv, w31, wr, wi, gmat, w_out, name):
    t_len = x.shape[0]
    tm = min(TM_MIX, t_len)

    def body(x_ref, vec_ref, win_ref, cv_ref, w31_ref, wr_ref, wi_ref, g_ref, wout_ref,
             xo_ref, f_ref, z_ref, hs_ref, cvs_ref, ext_a, ext_b, hc, shifted):
        @pl.when(pl.program_id(0) == 0)
        def _():
            ext_a[0:HALO_A, :] = jnp.zeros((HALO_A, W_A), F32)
            ext_b[0:HALO_B, :] = jnp.zeros((HALO_B, W_B), F32)
            hc[...] = jnp.zeros((8, W_A), F32)

        x_t = x_ref[...]
        h, _, _, _ = _pre_fwd(x_t, vec_ref)
        hb = h.astype(BF)
        for j in range(N_DEV):
            z_ref[:, 256 * j:256 * (j + 1)] = _dot(hb, win_ref[j])
        ext_a[HALO_A:HALO_A + tm, :] = z_ref[:, W_A:2 * W_A]
        xc = _conv_causal(ext_a, cv_ref, cv_ref[4:5, :], CONV_A, HALO_A, tm)
        ext_a[0:HALO_A, :] = ext_a[tm:tm + HALO_A, :]
        cvs_ref[:, 0:W_A] = xc
        _, r, ig, _, a, m = _a_gates(xc, cv_ref, wr_ref, wi_ref)
        for col, val in enumerate((r, ig, a, m)):
            cvs_ref[:, W_A * (2 + col):W_A * (3 + col)] = val
        a_cum, hloc = _scan(a, m * ig * xc, tm, False)
        hs = hloc + a_cum * hc[0:1, :]
        hs_ref[...] = hs
        hc[0:1, :] = hs[tm - 1:tm, :]
        ag = z_ref[:, 0:W_A]
        gel, tg = _gelu(ag)
        cvs_ref[:, 7 * W_A:8 * W_A] = gel
        cvs_ref[:, 8 * W_A:9 * W_A] = _gelu_grad(ag, tg)
        ya = hs * gel
        ext_b[HALO_B:HALO_B + tm, :] = z_ref[:, 2 * W_A:2 * W_A + W_B] * _sigmoid(z_ref[:, 2 * W_A + W_B:2 * W_A + 2 * W_B])
        _build_shifted(shifted, ext_b[...], tm + HALO_B)
        vc = cv_ref[8:9, :] + w31_ref[0:1, :] * _shifted_rows(shifted, HALO_B - CONV_B + 1, tm)
        for k in range(1, CONV_B):
            vc = vc + w31_ref[k:k + 1, :] * _shifted_rows(shifted, HALO_B - CONV_B + 1 + k, tm)
        ext_b[0:HALO_B, :] = ext_b[tm:tm + HALO_B, :]
        rstd, vhat, vln = _b_norm(vc, cv_ref, g_ref)
        cvs_ref[:, W_A:W_A + W_B] = vhat
        cvs_ref[:, 6 * W_A:7 * W_A] = rstd
        yb = vln * _sigmoid(vln)
        fpre = _dot(ya.astype(BF), wout_ref[0:W_A, :]) + _dot(yb.astype(BF), wout_ref[W_A:W_A + W_B, :])
        f_ref[...] = fpre
        xo_ref[...] = _post_fwd(x_t, fpre, vec_ref, 1.0)

    return pl.pallas_call(
        body, grid=(t_len // tm,),
        in_specs=[_tile(tm, D), _full((8, D)), _full((N_DEV, D, 256)), _full((16, W_A)), _full((32, W_B)),
                  _full((W_A, W_A)), _full((W_A, W_A)), _full((W_B, W_B)), _full((D, D))],
        out_specs=[_tile(tm, D), _tile(tm, D), _tile(tm, 2 * D), _tile(tm, W_A), _tile(tm, AB_SAVED)],
        out_shape=[jax.ShapeDtypeStruct((t_len, D), F32), jax.ShapeDtypeStruct((t_len, D), F32),
                   jax.ShapeDtypeStruct((t_len, 2 * D), F32), jax.ShapeDtypeStruct((t_len, W_A), F32),
                   jax.ShapeDtypeStruct((t_len, AB_SAVED), F32)],
        scratch_shapes=[pltpu.VMEM((tm + HALO_A, W_A), F32), pltpu.VMEM((tm + HALO_B, W_B), F32), pltpu.VMEM((8, W_A), F32),
                        pltpu.VMEM((8, tm + HALO_B, W_B), F32)],
        compiler_params=_params(), name=name,
    )(x, vec, w_in, cv, w31, wr, wi, gmat, w_out)


def _mixab_bwd(dout, x, fpre, z, cvs, hs, vec, w_in, cv, w31, wr, wi, gmat, w_out, name, comm=()):
    t_len = x.shape[0]
    tm = min(TM_MIX, t_len)
    nt = t_len // tm

    def rev(i):
        return nt - 1 - i

    def rtile(ncol):
        return pl.BlockSpec((tm, ncol), lambda i: (rev(i), 0))

    def body(dout_ref, x_ref, f_ref, z_ref, cvs_ref, hs_ref, hsp_ref, vec_ref, win_ref, cv_ref, w31_ref, wr_ref, wi_ref,
             g_ref, wout_ref,
             dx_ref, dz_ref, yab_ref, hb_ref, dfb_ref, xcb_ref, dri_ref, acc_ref, accs_ref, dw31_ref,
             ext_h, ext_dx, ext_dv, carry, shifted, win_v):
        i = pl.program_id(0)
        has_prev = (rev(i) > 0).astype(F32)

        @pl.when(i == 0)
        def _():
            for j in range(N_DEV):
                win_v[:, 256 * j:256 * (j + 1)] = win_ref[j]
            acc_ref[...] = jnp.zeros((8, D), F32)
            accs_ref[...] = jnp.zeros((16, W_A), F32)
            dw31_ref[...] = jnp.zeros((32, W_B), F32)
            ext_dx[tm:tm + HALO_A, :] = jnp.zeros((HALO_A, W_A), F32)
            ext_dv[tm:tm + HALO_B, :] = jnp.zeros((HALO_B, W_B), F32)
            carry[...] = jnp.zeros((8, W_A), F32)

        dout_t = dout_ref[...]
        df = _post_bwd(dout_t, f_ref[...], vec_ref, acc_ref, 1.0)
        dfb = df.astype(BF)
        dfb_ref[...] = dfb
        h, xn, r_x, n = _pre_fwd(x_ref[...], vec_ref)
        hb_ref[...] = h.astype(BF)

        ag = z_ref[:, 0:W_A]
        ax = z_ref[:, W_A:2 * W_A]
        bv = z_ref[:, 2 * W_A:2 * W_A + W_B]
        sg = _sigmoid(z_ref[:, 2 * W_A + W_B:2 * W_A + 2 * W_B])
        vv = bv * sg
        xc = cvs_ref[:, 0:W_A]
        r, ig, a, m = (cvs_ref[:, W_A * (2 + col):W_A * (3 + col)] for col in range(4))
        ls = _log_sigmoid(cv_ref[7:8, :])
        xcb_ref[...] = xc.astype(BF)
        hs_t = hs_ref[...]
        ext_h[0:8, :] = hsp_ref[...] * has_prev
        ext_h[8:8 + tm, :] = hs_t
        hprev = ext_h[7:7 + tm, :]
        gel = cvs_ref[:, 7 * W_A:8 * W_A]
        vhat = cvs_ref[:, W_A:W_A + W_B]
        rstd = cvs_ref[:, 6 * W_A:7 * W_A]
        vln = vhat * cv_ref[9:10, :] + cv_ref[10:11, :]
        sv = _sigmoid(vln)
        yab_ref[:, 0:W_A] = (hs_t * gel).astype(BF)
        yab_ref[:, W_A:W_A + W_B] = (vln * sv).astype(BF)

        dya = _dot_nt(dfb, wout_ref[0:W_A, :])
        dyb = _dot_nt(dfb, wout_ref[W_A:W_A + W_B, :])

        dag = dya * hs_t * cvs_ref[:, 8 * W_A:9 * W_A]
        row = lax.broadcasted_iota(jnp.int32, (tm, W_A), 0)
        last = row == tm - 1
        a_next = jnp.where(last, 1.0, pltpu.roll(a, tm - 1, 0))
        u0 = dya * gel + jnp.where(last, carry[0:1, :], 0.0)
        _, dhs = _scan(a_next, u0, tm, True)
        carry[0:1, :] = a[0:1, :] * dhs[0:1, :]
        da = dhs * hprev
        dm = dhs * ig * xc
        di = dhs * m * xc
        dxc = dhs * m * ig
        dla = da * a - dm * (a * a) / m
        accs_ref[7:8, :] += _colsum(dla * r) * (LRU_C * _sigmoid(-cv_ref[7:8, :]))
        drp = (dla * (LRU_C * ls)) * r * (1.0 - r)
        dip = di * ig * (1.0 - ig)
        accs_ref[5:6, :] += _colsum(drp)
        accs_ref[6:7, :] += _colsum(dip)
        drpb = drp.astype(BF)
        dipb = dip.astype(BF)
        dri_ref[:, 0:W_A] = drpb
        dri_ref[:, W_A:2 * W_A] = dipb
        dxc = dxc + _dot_nt(drpb, wr_ref[...]) + _dot_nt(dipb, wi_ref[...])
        accs_ref[4:5, :] += _colsum(dxc)
        ext_dx[0:tm, :] = dxc
        e_dx = ext_dx[...]
        dax = jnp.zeros((tm, W_A), F32)
        for k in range(CONV_A):
            ahead = _rows_from(e_dx, CONV_A - 1 - k, tm)
            accs_ref[k:k + 1, :] += _colsum(ax * ahead)
            dax = dax + cv_ref[k:k + 1, :] * ahead
        ext_dx[tm:tm + HALO_A, :] = dxc[0:HALO_A, :]

        dvln = dyb * (sv * (1.0 + vln * (1.0 - sv)))
        accs_ref[9:10, :] += _colsum(dvln * vhat)
        accs_ref[10:11, :] += _colsum(dvln)
        dvhat = dvln * cv_ref[9:10, :]
        dvc = rstd * (dvhat - _gmean(dvhat, g_ref) - vhat * _gmean(dvhat * vhat, g_ref))
        accs_ref[8:9, :] += _colsum(dvc)
        ext_dv[0:tm, :] = dvc
        _build_shifted(shifted, ext_dv[...], tm + HALO_B)
        dvv = jnp.zeros((tm, W_B), F32)
        for k in range(CONV_B):
            ahead = _shifted_rows(shifted, CONV_B - 1 - k, tm)
            dw31_ref[k:k + 1, :] += _colsum(vv * ahead)
            dvv = dvv + w31_ref[k:k + 1, :] * ahead
        ext_dv[tm:tm + HALO_B, :] = dvc[0:HALO_B, :]

        dz_ref[:, 0:W_A] = dag.astype(BF)
        dz_ref[:, W_A:2 * W_A] = dax.astype(BF)
        dz_ref[:, 2 * W_A:2 * W_A + W_B] = (dvv * sg).astype(BF)
        dz_ref[:, 2 * W_A + W_B:2 * W_A + 2 * W_B] = (dvv * vv * (1.0 - sg)).astype(BF)
        dh = _dot_nt(dz_ref[...], win_v[...])
        dx_ref[...] = _pre_bwd(dout_t, dh, xn, r_x, vec_ref, acc_ref)

        @pl.when(i == nt - 1)
        def _():
            _finish_acc(acc_ref, vec_ref, 1.0)

    hsp_spec = pl.BlockSpec((8, W_A), lambda i: (jnp.maximum(rev(i) * (tm // 8) - 1, 0), 0))
    nc = len(comm)
    outs = pl.pallas_call(
        _hosted(body, 15, 10, 6, nc, "scatter", (nt,)), grid=(nt,),
        in_specs=[rtile(D), rtile(D), rtile(D), rtile(2 * D), rtile(AB_SAVED), rtile(W_A), hsp_spec, _full((8, D)),
                  _full((N_DEV, D, 256)), _full((16, W_A)), _full((32, W_B)), _full((W_A, W_A)), _full((W_A, W_A)),
                  _full((W_B, W_B)), _full((D, D))] + [_any()] * nc,
        out_specs=[rtile(D), rtile(2 * D), rtile(D), rtile(D), rtile(D), rtile(W_A), rtile(2 * W_A), _full((8, D)),
                   _full((16, W_A)), _full((32, W_B))] + [_any()] * nc,
        out_shape=[jax.ShapeDtypeStruct((t_len, D), F32), jax.ShapeDtypeStruct((t_len, 2 * D), BF),
                   jax.ShapeDtypeStruct((t_len, D), BF), jax.ShapeDtypeStruct((t_len, D), BF),
                   jax.ShapeDtypeStruct((t_len, D), BF), jax.ShapeDtypeStruct((t_len, W_A), BF),
                   jax.ShapeDtypeStruct((t_len, 2 * W_A), BF), jax.ShapeDtypeStruct((8, D), F32),
                   jax.ShapeDtypeStruct((16, W_A), F32), jax.ShapeDtypeStruct((32, W_B), F32)] + _exchange_shapes(comm, "scatter"),
        scratch_shapes=[pltpu.VMEM((tm + 8, W_A), F32), pltpu.VMEM((tm + HALO_A, W_A), F32),
                        pltpu.VMEM((tm + HALO_B, W_B), F32), pltpu.VMEM((8, W_A), F32),
                        pltpu.VMEM((8, tm + HALO_B, W_B), F32), pltpu.VMEM((D, 2 * D), BF)]
        + (_exchange_scratch(nc) if nc else []),
        compiler_params=_params(), name=name,
    )(dout, x, fpre, z, cvs, hs, hs, vec, w_in, cv, w31, wr, wi, gmat, w_out, *comm)
    return outs[:10], list(outs[10:])


def _vec(p, l, j):
    return jnp.concatenate([p["mod"][l, j], p["norm_pre"][l, j][None], p["norm_post"][l, j][None], jnp.zeros((3, D), F32)], 0)


def _ab_consts(p):
    gw = p["a_gate_w"]
    gb = p["a_gate_b"]
    half = W_A // 8
    eye = jnp.eye(8, dtype=F32)[:, None, :, None]

    def block_diag(blocks):
        return (blocks[:, :, None, :] * eye).reshape(W_A, W_A).astype(BF)

    wr = block_diag(gw[:, :, 0:half])
    wi = block_diag(gw[:, :, half:2 * half])
    rows = [p["a_conv_w"], p["a_conv_b"][None], gb[:, 0:half].reshape(1, W_A), gb[:, half:2 * half].reshape(1, W_A),
            p["a_lam"][None], p["b_conv_b"][None], p["b_norm_g"][None], p["b_norm_b"][None], jnp.zeros((5, W_A), F32)]
    cv = jnp.concatenate(rows, 0)
    w31 = jnp.concatenate([p["b_conv_w"], jnp.zeros((1, W_B), F32)], 0)
    grp = jnp.arange(W_B) // (W_B // 8)
    gmat = ((grp[:, None] == grp[None, :]).astype(F32) / (W_B // 8)).astype(BF)
    return cv, w31, wr, wi, gmat


SUBLAYERS = ("f0", "ab", "f1", "f2", "c", "f3")


def _local_step(x, tgt, p, plan=None):
    g = {}
    saved = []
    cur = x
    wsets = dict(p["wsets"])
    ab_c = _ab_consts(p)
    c_cvec = jnp.concatenate([p["c_norm_g"][None], p["c_norm_b"][None], jnp.zeros((6, D), F32)], 0)
    c_bst = jnp.transpose(p["c_b_s"])
    c_bin = p["c_b_in"][None]
    for s_idx, wname in enumerate(SUBLAYERS):
        l, j = divmod(s_idx, 3)
        vec = _vec(p, l, j)
        tag = f"l{l}s{j}"
        if j != 1:
            names = plan["gather"].get(s_idx, []) if plan else []
            comm = [a for nm in names for a in plan["shards"][nm]]
            res, got = _ffn_fwd(cur, vec, *wsets[wname], 0.5, "ffn_fwd_" + tag, comm=comm, tgt=tgt if s_idx == 5 else None)
            nxt, fpre, jac, s_act = res[:4]
            for k, nm in enumerate(names):
                wsets[nm] = list(got[2 * k:2 * k + 2])
            saved.append((cur, fpre, jac, s_act, vec))
            if s_idx == 5:
                loss_blk = res[4]
        elif l == 0:
            w_in, w_out = wsets[wname]
            nxt, fpre, z, hs, cvs = _mixab_fwd(cur, vec, w_in, *ab_c, w_out.reshape(D, D), "mixab_fwd_" + tag)
            saved.append((cur, fpre, z, hs, cvs, vec))
        else:
            w_in, w_out = wsets[wname]
            nxt, fpre, pre = _mixc_fwd(cur, vec, w_in, c_bin, c_cvec, p["c_w_s"], c_bst, w_out.reshape(D, D), "mixc_fwd_" + tag)
            saved.append((cur, fpre, pre, vec))
        cur = nxt
    dcur = cur
    accs, pending, recv = {}, {}, {}

    def take(host):
        keys = plan["scatter"].get(host, []) if plan else []
        return keys, [pending.pop(k) for k in keys]

    def put(keys, got):
        recv.update(zip(keys, got))

    for s_idx in reversed(range(6)):
        wname = SUBLAYERS[s_idx]
        l, j = divmod(s_idx, 3)
        tag = f"l{l}s{j}"
        sv = saved[s_idx]
        if j != 1:
            keys, comm = take("ffn_bwd_" + tag)
            xin, fpre, jac, s, vec = sv
            (dcur, dgu, hb, dfb, acc), got = _ffn_bwd(dcur, xin, fpre, jac, vec, *wsets[wname], 0.5, "ffn_bwd_" + tag, comm=comm)
            put(keys, got)
            keys, comm = take("wgrad_w13_" + tag)
            dw13 = _wgrad(hb, dgu, 2, D, D_FF, "share", "cols", D, BF, "wgrad_w13_" + tag, tk=TK_WGRAD, comm=comm,
                          slots=(_w13_slots, (4, D, FF_PAD)))
            if keys:
                dw13, got = dw13
                put(keys, got)
            pending[wname + ".0"] = dw13
            keys, comm = take("wgrad_w2_" + tag)
            dw2 = _wgrad(s, dfb, 1, D_FF, D, "share", "share", D_FF, BF, "wgrad_w2_" + tag, tk=TK_WGRAD, comm=comm,
                         slots=(_w2_slots, (N_DEV, W2_SHARD, D)))
            if keys:
                dw2, got = dw2
                put(keys, got)
            pending[wname + ".1"] = dw2
        elif l == 0:
            xin, fpre, z, hs, cvs, vec = sv
            w_in, w_out = wsets[wname]
            keys, comm = take("mixab_bwd_" + tag)
            (dcur, dz, yab, hb, dfb, xcb, dri, acc, accs_ab, dw31), got = _mixab_bwd(
                dcur, xin, fpre, z, cvs, hs, vec, w_in, *ab_c, w_out.reshape(D, D), "mixab_bwd_" + tag, comm=comm)
            put(keys, got)
            d_in = _wgrad(hb, dz, 1, D, 2 * D, "share", "share", D, BF, "wgrad_ab_in", tk=TK_WGRAD, col_slots=N_DEV)
            d_out = _wgrad(yab, dfb, 1, D, D, "share", "share", D, BF, "wgrad_ab_out")
            pending[wname + ".0"], pending[wname + ".1"] = d_in, d_out.reshape(N_DEV, D // N_DEV, D)
            g["gate"] = _wgrad(xcb, dri, 1, W_A, 2 * W_A, "share", "share", W_A, F32, "wgrad_gate")
            g["accs_ab"] = accs_ab
            g["dw31"] = dw31
        else:
            xin, fpre, pre, vec = sv
            w_in, w_out = wsets[wname]
            dcur, dpre, pb, hb, dfb, acc, dbin, dws, dbst = _mixc_bwd(
                dcur, xin, fpre, pre, vec, w_in, c_cvec, p["c_w_s"], c_bst, w_out.reshape(D, D), "mixc_bwd_" + tag)
            d_in = _wgrad(hb, dpre, 1, D, 2 * D, "share", "share", D, BF, "wgrad_c_in", tk=TK_WGRAD, col_slots=N_DEV)
            d_out = _wgrad(pb, dfb, 1, D, D, "share", "share", D, BF, "wgrad_c_out")
            pending[wname + ".0"], pending[wname + ".1"] = d_in, d_out.reshape(N_DEV, D // N_DEV, D)
            g["c_small"] = (acc, dbin, dws, dbst)
        accs[f"{l}{j}"] = acc
    g["accs"] = accs
    g["pending"] = pending
    g["recv"] = recv
    return loss_blk, dcur, g


def _exchange_ops(ins, outs, sems, mode, action):
    send_sems, recv_sems, loc_sems = sems
    n = len(ins)
    x, y, c = lax.axis_index("x"), lax.axis_index("y"), lax.axis_index("c")
    me = 4 * x + 2 * y + c

    def src(i, dev):
        return ins[i] if mode == "gather" else ins[i].at[dev]

    for i in range(n):
        cp = pltpu.make_async_copy(src(i, me), outs[i].at[me], loc_sems.at[i])
        if action == "start":
            cp.start()
        else:
            cp.wait()
    for mask in range(1, N_DEV):
        px = 1 - x if mask & 4 else x
        py = 1 - y if mask & 2 else y
        pc = 1 - c if mask & 1 else c
        peer = 4 * px + 2 * py + pc
        for i in range(n):
            k = i * (N_DEV - 1) + mask - 1
            cp = pltpu.make_async_remote_copy(
                src_ref=src(i, peer), dst_ref=outs[i].at[me if action == "start" else peer],
                send_sem=send_sems.at[k], recv_sem=recv_sems.at[k],
                device_id=(px, py, pc), device_id_type=pl.DeviceIdType.MESH)
            if action == "start":
                cp.start()
            else:
                cp.wait()


def _exchange_scratch(n):
    return [pltpu.SemaphoreType.DMA((n * (N_DEV - 1),)), pltpu.SemaphoreType.DMA((n * (N_DEV - 1),)),
            pltpu.SemaphoreType.DMA((n,))]


def _exchange_shapes(arrays, mode):
    return [jax.ShapeDtypeStruct(((N_DEV,) + a.shape) if mode == "gather" else a.shape, a.dtype) for a in arrays]


def _exchange(arrays, mode, name):
    n = len(arrays)

    def body(*refs):
        ins, outs, sems = refs[:n], refs[n:2 * n], refs[2 * n:]
        _exchange_ops(ins, outs, sems, mode, "start")
        _exchange_ops(ins, outs, sems, mode, "wait")

    return pl.pallas_call(
        body, in_specs=[pl.BlockSpec(memory_space=pl.ANY)] * n, out_specs=[pl.BlockSpec(memory_space=pl.ANY)] * n,
        out_shape=_exchange_shapes(arrays, mode), scratch_shapes=_exchange_scratch(n), name=name,
    )(*arrays)


def _gather_two_level(arrays, name):
    n = len(arrays)
    per = N_DEV - 1

    def body(*refs):
        ins, outs = refs[:n], refs[n:2 * n]
        send_sems, recv_sems, loc_sems = refs[2 * n:]
        x, y, c = lax.axis_index("x"), lax.axis_index("y"), lax.axis_index("c")
        me, sibling = (x, y, c), (x, y, 1 - c)
        chips = [(1 - x, y), (x, 1 - y), (1 - x, 1 - y)]

        def rows(i, dev):
            return outs[i].at[4 * dev[0] + 2 * dev[1] + dev[2]]

        def copy(i, k, block, to, src=None):
            return pltpu.make_async_remote_copy(
                src_ref=rows(i, block) if src is None else src, dst_ref=rows(i, block),
                send_sem=send_sems.at[i * per + k], recv_sem=recv_sems.at[i * per + k],
                device_id=to, device_id_type=pl.DeviceIdType.MESH)

        mine = [pltpu.make_async_copy(ins[i], rows(i, me), loc_sems.at[i]) for i in range(n)]
        for cp in mine:
            cp.start()
        first = []
        for i in range(n):
            first.append(copy(i, 0, me, sibling, src=ins[i]))
            first += [copy(i, 1 + j, me, (*chip, c), src=ins[i]) for j, chip in enumerate(chips)]
        for cp in first:
            cp.start()
        passed = []
        for j, chip in enumerate(chips):
            for i in range(n):
                copy(i, 1 + j, (*chip, c), me).wait_recv()
                fwd = copy(i, 4 + j, (*chip, c), sibling)
                fwd.start()
                passed.append(fwd)
        for i in range(n):
            copy(i, 0, sibling, me).wait_recv()
            for j, chip in enumerate(chips):
                copy(i, 4 + j, (*chip, 1 - c), me).wait_recv()
        for cp in first + passed:
            cp.wait_send()
        for cp in mine:
            cp.wait()

    return pl.pallas_call(
        body, in_specs=[pl.BlockSpec(memory_space=pl.ANY)] * n, out_specs=[pl.BlockSpec(memory_space=pl.ANY)] * n,
        out_shape=_exchange_shapes(arrays, "gather"), scratch_shapes=_exchange_scratch(n), name=name,
    )(*arrays)


def _sum_slots(a, name):
    def body(a_ref, o_ref):
        acc = a_ref[0]
        for s in range(1, N_DEV):
            acc = acc + a_ref[s]
        o_ref[...] = acc

    return pl.pallas_call(body, out_shape=jax.ShapeDtypeStruct(a.shape[1:], F32), name=name,
                          compiler_params=pltpu.CompilerParams(vmem_limit_bytes=VMEM_LIMIT))(a)


def _pack(pieces, mult):
    flat = jnp.concatenate([q.reshape(-1).astype(F32) for q in pieces])
    size = -(-flat.shape[0] // mult) * mult
    return jnp.pad(flat, (0, size - flat.shape[0])).reshape(size // 128, 128)


def _unpack(flat, shapes):
    out, off = [], 0
    for shp in shapes:
        size = math.prod(shp)
        out.append(flat[..., off:off + size].reshape(flat.shape[:-1] + tuple(shp)))
        off += size
    return out


def _mod_part(c_all, ada_w, ada_b_mine, name):
    cols = ada_w.shape[-1]

    def body(c_ref, w_ref, b_ref, o_ref):
        cv = c_ref[...]
        ca = cv * _sigmoid(cv)
        for l in range(2):
            o_ref[l] = jnp.dot(ca, w_ref[l], preferred_element_type=F32, precision=lax.Precision.HIGHEST) + b_ref[l:l + 1, :]

    return pl.pallas_call(body, out_shape=jax.ShapeDtypeStruct((2, N_DEV, cols), F32), name=name,
                          compiler_params=pltpu.CompilerParams(vmem_limit_bytes=VMEM_LIMIT))(c_all, ada_w, ada_b_mine)


def _ada_w_grad(c_all_t, dmod_mine, name):
    cols = dmod_mine.shape[-1]

    def body(ct_ref, d_ref, o_ref):
        cv = ct_ref[...]
        ca = cv * _sigmoid(cv)
        for l in range(2):
            acc = ca[:, 0:1] * d_ref[l, 0:1, :]
            for b in range(1, N_DEV):
                acc = acc + ca[:, b:b + 1] * d_ref[l, b:b + 1, :]
            o_ref[l] = acc

    return pl.pallas_call(body, out_shape=jax.ShapeDtypeStruct((2, D, cols), F32), name=name,
                          compiler_params=pltpu.CompilerParams(vmem_limit_bytes=VMEM_LIMIT))(c_all_t, dmod_mine)


def _adamw_math(w, g, m, v):
    m2 = ADAM_B1 * m + (1.0 - ADAM_B1) * g
    v2 = ADAM_B2 * v + (1.0 - ADAM_B2) * (g * g)
    m_hat = m2 / (1.0 - ADAM_B1 ** ADAM_STEP)
    v_hat = v2 / (1.0 - ADAM_B2 ** ADAM_STEP)
    delta = -ADAM_LR * (m_hat / (jnp.sqrt(v_hat) + ADAM_EPS) + ADAM_WD * w)
    return delta, m2, v2


def _adamw_big(w, g, m, v, name):
    n_l, rows, cols = w.shape
    parts = list(g) if isinstance(g, (list, tuple)) else None
    sizes = (512, 352, 256, 128, 64, 32, 16, 8) if parts is None or len(parts) == 1 else (176, 128, 64, 32, 16, 8)
    br = next(b for b in sizes if rows % b == 0)
    nr = rows // br

    def body(*refs):
        w_ref, g_refs, (m_ref, v_ref, go_ref, d_ref, mo_ref, vo_ref) = refs[0], refs[1:-6], refs[-6:]

        def update(gsum):
            go_ref[...] = gsum
            d_ref[...], mo_ref[...], vo_ref[...] = _adamw_math(w_ref[...], gsum, m_ref[...], v_ref[...])

        if parts is None:
            update(g_refs[0][...])
        else:
            for l, g_ref in enumerate(g_refs):
                @pl.when(pl.program_id(0) == l)
                def _(g_ref=g_ref):
                    gsum = g_ref[0, :, 0:cols].astype(F32)
                    for s in range(1, N_DEV):
                        gsum = gsum + g_ref[s, :, 0:cols].astype(F32)
                    update(gsum)

    blk = pl.BlockSpec((None, br, cols), lambda l, i: (l, i, 0))
    if parts is None:
        g_specs, g_args = [blk], [g]
    else:
        def part_spec(l_mine, width):
            return pl.BlockSpec((N_DEV, br, width),
                                lambda l, i: (0, jnp.where(l < l_mine, 0, jnp.where(l == l_mine, i, nr - 1)), 0))
        g_specs, g_args = [part_spec(l, p.shape[-1]) for l, p in enumerate(parts)], parts
    shp = jax.ShapeDtypeStruct((n_l, rows, cols), F32)
    return pl.pallas_call(
        body, grid=(n_l, nr), in_specs=[blk] + g_specs + [blk, blk], out_specs=[blk] * 4, out_shape=[shp] * 4,
        compiler_params=pltpu.CompilerParams(dimension_semantics=("arbitrary", "arbitrary"), vmem_limit_bytes=VMEM_LIMIT),
        name=name)(w, *g_args, m, v)


def _adamw_small(ws, gs, ms, vs, name):
    n = len(ws)

    def body(*refs):
        for i in range(n):
            w_ref, g_ref, m_ref, v_ref = (refs[k * n + i] for k in range(4))
            d_ref, mo_ref, vo_ref = (refs[(4 + k) * n + i] for k in range(3))
            d_ref[...], mo_ref[...], vo_ref[...] = _adamw_math(w_ref[...], g_ref[...], m_ref[...], v_ref[...])

    shapes = [jax.ShapeDtypeStruct(w.shape, F32) for w in ws]
    outs = pl.pallas_call(body, out_shape=shapes * 3, name=name,
                          compiler_params=pltpu.CompilerParams(vmem_limit_bytes=VMEM_LIMIT))(*ws, *gs, *ms, *vs)
    return outs[:n], outs[n:2 * n], outs[2 * n:]


def _as2d(a):
    return a.reshape(-1, a.shape[-1])


def kernel(x, c, ada_w, ada_b, norm_pre, norm_post, ffn_w13, ffn_w2, ab_w_in, a_conv_w, a_conv_b, a_gate_w, a_gate_b, a_lam, b_conv_w, b_conv_b, b_norm_g, b_norm_b, ab_w_out, c_w_in, c_b_in, c_norm_g, c_norm_b, c_w_s, c_b_s, c_w_out, loss_target, m_ada_w, m_ada_b, m_norm_pre, m_norm_post, m_ffn_w13, m_ffn_w2, m_ab_w_in, m_a_conv_w, m_a_conv_b, m_a_gate_w, m_a_gate_b, m_a_lam, m_b_conv_w, m_b_conv_b, m_b_norm_g, m_b_norm_b, m_ab_w_out, m_c_w_in, m_c_b_in, m_c_norm_g, m_c_norm_b, m_c_w_s, m_c_b_s, m_c_w_out, v_ada_w, v_ada_b, v_norm_pre, v_norm_post, v_ffn_w13, v_ffn_w2, v_ab_w_in, v_a_conv_w, v_a_conv_b, v_a_gate_w, v_a_gate_b, v_a_lam, v_b_conv_w, v_b_conv_b, v_b_norm_g, v_b_norm_b, v_ab_w_out, v_c_w_in, v_c_b_in, v_c_norm_g, v_c_norm_b, v_c_w_s, v_c_b_s, v_c_w_out):
    me = 4 * lax.axis_index("x") + 2 * lax.axis_index("y") + lax.axis_index("c")
    weights = dict(ada_w=ada_w, ada_b=ada_b, norm_pre=norm_pre, norm_post=norm_post, ffn_w13=ffn_w13, ffn_w2=ffn_w2,
                   ab_w_in=ab_w_in, a_conv_w=a_conv_w, a_conv_b=a_conv_b, a_gate_w=a_gate_w, a_gate_b=a_gate_b, a_lam=a_lam,
                   b_conv_w=b_conv_w, b_conv_b=b_conv_b, b_norm_g=b_norm_g, b_norm_b=b_norm_b, ab_w_out=ab_w_out,
                   c_w_in=c_w_in, c_b_in=c_b_in, c_norm_g=c_norm_g, c_norm_b=c_norm_b, c_w_s=c_w_s, c_b_s=c_b_s, c_w_out=c_w_out)
    moms = dict(ada_w=m_ada_w, ada_b=m_ada_b, norm_pre=m_norm_pre, norm_post=m_norm_post, ffn_w13=m_ffn_w13, ffn_w2=m_ffn_w2,
                ab_w_in=m_ab_w_in, a_conv_w=m_a_conv_w, a_conv_b=m_a_conv_b, a_gate_w=m_a_gate_w, a_gate_b=m_a_gate_b,
                a_lam=m_a_lam, b_conv_w=m_b_conv_w, b_conv_b=m_b_conv_b, b_norm_g=m_b_norm_g, b_norm_b=m_b_norm_b,
                ab_w_out=m_ab_w_out, c_w_in=m_c_w_in, c_b_in=m_c_b_in, c_norm_g=m_c_norm_g, c_norm_b=m_c_norm_b,
                c_w_s=m_c_w_s, c_b_s=m_c_b_s, c_w_out=m_c_w_out)
    vars_ = dict(ada_w=v_ada_w, ada_b=v_ada_b, norm_pre=v_norm_pre, norm_post=v_norm_post, ffn_w13=v_ffn_w13, ffn_w2=v_ffn_w2,
                 ab_w_in=v_ab_w_in, a_conv_w=v_a_conv_w, a_conv_b=v_a_conv_b, a_gate_w=v_a_gate_w, a_gate_b=v_a_gate_b,
                 a_lam=v_a_lam, b_conv_w=v_b_conv_w, b_conv_b=v_b_conv_b, b_norm_g=v_b_norm_g, b_norm_b=v_b_norm_b,
                 ab_w_out=v_ab_w_out, c_w_in=v_c_w_in, c_b_in=v_c_b_in, c_norm_g=v_c_norm_g, c_norm_b=v_c_norm_b,
                 c_w_s=v_c_w_s, c_b_s=v_c_b_s, c_w_out=v_c_w_out)
    names = list(weights)

    w13s = ffn_w13.astype(BF).reshape(4, D, FF_SHARD)
    tail, blank = w13s[..., FF_MAIN:], jnp.zeros((4, D, FF_TAIL), BF)
    tail_tile = jnp.where(me % 2 == 1, jnp.concatenate([blank, tail], -1), jnp.concatenate([tail, blank], -1))
    w13b = jnp.concatenate([w13s[..., :FF_MAIN], tail_tile], -1)
    small_shapes = [(D,), (2, 3, 128), (2, 3, 128), (CONV_A, 64), (CONV_B, 64), (256,), (128,), (128,)]
    small = _pack([c, norm_pre, norm_post, a_conv_w, b_conv_w, c_b_in, c_norm_g, c_norm_b], 1024)
    w2b = ffn_w2.astype(BF).reshape(4, W2_SHARD, D)
    shards = {f"f{f}": [w13b[f], w2b[f]] for f in range(4)}
    shards["ab"] = [ab_w_in[0].astype(BF), ab_w_out[0].astype(BF)]
    shards["c"] = [c_w_in[0].astype(BF), c_w_out[0].astype(BF)]
    w13g0, w2g0, small_g = _gather_two_level(shards["f0"] + [small], "gather_first")
    plan = dict(shards=shards, gather={0: ["ab", "f1"], 2: ["f2"], 3: ["c", "f3"]},
                scatter={"ffn_bwd_l1s0": ["f3.0", "f3.1", "c.0", "c.1"], "ffn_bwd_l0s2": ["f2.0", "f2.1"],
                         "mixab_bwd_l0s1": ["f1.0", "f1.1"], "wgrad_w13_l0s0": ["ab.0", "ab.1"], "wgrad_w2_l0s0": ["f0.0"]})
    c_all, npre_g, npost_g, acw_g, bcw_g, cbin_g, cng_g, cnb_g = _unpack(small_g.reshape(N_DEV, -1), small_shapes)

    def cat_last(a):
        return jnp.moveaxis(a, 0, -2).reshape(a.shape[1:-1] + (N_DEV * a.shape[-1],))

    ada_b_mine = lax.dynamic_slice_in_dim(ada_b, me * ada_w.shape[-1], ada_w.shape[-1], axis=1)
    (mod_g,) = _exchange([_mod_part(c_all, ada_w, ada_b_mine, "mod_part")], "gather", "gather_mod")
    mod = cat_last(lax.dynamic_index_in_dim(mod_g, me, axis=2, keepdims=False)).reshape(2, 3, 3, D)

    p = dict(mod=mod, norm_pre=cat_last(npre_g), norm_post=cat_last(npost_g), wsets={"f0": [w13g0, w2g0]},
             a_conv_w=cat_last(acw_g), a_conv_b=a_conv_b[0], a_gate_w=a_gate_w[0], a_gate_b=a_gate_b[0], a_lam=a_lam[0],
             b_conv_w=cat_last(bcw_g), b_conv_b=b_conv_b[0], b_norm_g=b_norm_g[0], b_norm_b=b_norm_b[0],
             c_b_in=cat_last(cbin_g), c_norm_g=cat_last(cng_g), c_norm_b=cat_last(cnb_g), c_w_s=c_w_s[0], c_b_s=c_b_s[0])

    loss_blk, grad_x, g = _local_step(x[0], loss_target[0], p, plan)
    loss = lax.psum(loss_blk[0, 0], ("x", "y", "c"))

    accs = g["accs"]
    dmod = jnp.stack([jnp.stack([accs[f"{l}{j}"][0:3] for j in range(3)]) for l in range(2)])
    dnpre = jnp.stack([jnp.stack([accs[f"{l}{j}"][3] for j in range(3)]) for l in range(2)])
    dnpost = jnp.stack([jnp.stack([accs[f"{l}{j}"][4] for j in range(3)]) for l in range(2)])
    sab = g["accs_ab"]
    half = W_A // 8
    dgate = g["gate"][0]
    dgw = jnp.stack([jnp.concatenate([dgate[half * hh:half * (hh + 1), half * hh:half * (hh + 1)],
                                      dgate[half * hh:half * (hh + 1), W_A + half * hh:W_A + half * (hh + 1)]], axis=1)
                     for hh in range(8)])
    dgb = jnp.concatenate([sab[5].reshape(8, half), sab[6].reshape(8, half)], axis=1)
    c_acc, c_dbin, c_dws, c_dbst = g["c_small"]
    red_shapes = [(2, 9216), (2, 3, D), (2, 3, D), (CONV_A, W_A), (W_A,), (8, half, 2 * half), (8, 2 * half), (W_A,),
                  (CONV_B, W_B), (W_B,), (W_B,), (W_B,), (2 * D,), (D,), (D,), (H_C, CHUNK, CHUNK), (H_C, CHUNK)]
    red = _pack([dmod.reshape(2, 9216), dnpre, dnpost, sab[0:4], sab[4], dgw, dgb, sab[7], g["dw31"][0:CONV_B], sab[8],
                 sab[9], sab[10], c_dbin[0], c_acc[5], c_acc[6], c_dws, jnp.transpose(c_dbst)], N_DEV * 1024)
    left = sorted(g["pending"])
    red_r, *last_recv = _exchange([red.reshape(N_DEV, -1, 128)] + [g["pending"][k] for k in left], "scatter",
                                  "scatter_small_grads")
    red_all, dmod_all = _exchange([_sum_slots(red_r, "sum_small_grads"), dmod.reshape(-1, 128)], "gather", "gather_small_grads")
    red_sum = red_all.reshape(-1)
    (g_ada_b, g_npre, g_npost, g_acw, g_acb, g_agw, g_agb, g_alam, g_bcw, g_bcb, g_bng, g_bnb, g_cbin, g_cng, g_cnb,
     g_cws, g_cbs) = _unpack(red_sum, red_shapes)
    dmod_all = dmod_all.reshape(N_DEV, 2, 9216)
    ncol = ada_w.shape[-1]
    dmod_mine = jnp.moveaxis(lax.dynamic_slice_in_dim(dmod_all, me * ncol, ncol, axis=2), 0, 1)
    g_ada_w = _ada_w_grad(jnp.transpose(c_all), dmod_mine, "ada_w_grad")

    def mine(a, width):
        return lax.dynamic_slice_in_dim(a, me * width, width, axis=a.ndim - 1)

    small_grads = dict(
        ada_b=g_ada_b, norm_pre=mine(g_npre, 128), norm_post=mine(g_npost, 128), a_conv_w=mine(g_acw, 64)[None],
        a_conv_b=g_acb[None], a_gate_w=g_agw[None], a_gate_b=g_agb[None], a_lam=g_alam[None], b_conv_w=mine(g_bcw, 64)[None],
        b_conv_b=g_bcb[None], b_norm_g=g_bng[None], b_norm_b=g_bnb[None], c_b_in=mine(g_cbin, 256)[None],
        c_norm_g=mine(g_cng, 128)[None], c_norm_b=mine(g_cnb, 128)[None], c_w_s=g_cws[None], c_b_s=g_cbs[None])

    recv = dict(g["recv"])
    recv.update(zip(left, last_recv))
    big_partials = dict(ffn_w13=[recv[f"f{f}.0"] for f in range(4)], ffn_w2=[recv[f"f{f}.1"] for f in range(4)],
                        ab_w_in=[recv["ab.0"]], ab_w_out=[recv["ab.1"]], c_w_in=[recv["c.0"]], c_w_out=[recv["c.1"]],
                        ada_w=g_ada_w)

    grads, deltas, new_m, new_v = {}, {}, {}, {}

    def as3d(a):
        return a.reshape((-1,) + a.shape[-2:])

    for nm, gp in big_partials.items():
        shp = weights[nm].shape
        go, dl, mo, vo = _adamw_big(as3d(weights[nm]), gp, as3d(moms[nm]), as3d(vars_[nm]), "adamw_" + nm)
        grads[nm], deltas[nm], new_m[nm], new_v[nm] = (a.reshape(shp) for a in (go, dl, mo, vo))
    snames = list(small_grads)
    dls, mos, vos = _adamw_small([_as2d(weights[nm]) for nm in snames], [_as2d(small_grads[nm]) for nm in snames],
                                 [_as2d(moms[nm]) for nm in snames], [_as2d(vars_[nm]) for nm in snames], "adamw_small")
    for k, nm in enumerate(snames):
        shp = weights[nm].shape
        grads[nm] = small_grads[nm].reshape(shp)
        deltas[nm], new_m[nm], new_v[nm] = dls[k].reshape(shp), mos[k].reshape(shp), vos[k].reshape(shp)

    return (loss, grad_x[None], *[grads[nm] for nm in names], *[deltas[nm] for nm in names],
            *[new_m[nm] for nm in names], *[new_v[nm] for nm in names])
```

```python
import math

import jax
import jax.numpy as jnp
from jax import lax
from jax.experimental import pallas as pl
from jax.experimental.pallas import tpu as pltpu

F32 = jnp.float32
BF = jnp.bfloat16

N_DEV = 8
D = 1024
EPS = 1e-6
D_FF = 2816
FF_SHARD = 704
FF_PAD = 768
FF_MAIN = 640
FF_TAIL = FF_SHARD - FF_MAIN
W2_SHARD = 352
W_A = 512
W_B = 512
AB_SAVED = 9 * W_A
CONV_A = 4
CONV_B = 31
HALO_A = 8
HALO_B = 32
LRU_C = 8.0
CHUNK = 128
H_C = 8
ADAM_LR = 0.001
ADAM_B1 = 0.9
ADAM_B2 = 0.999
ADAM_EPS = 1e-08
ADAM_WD = 0.01
ADAM_STEP = 10
VMEM_LIMIT = 62 * 1024 * 1024
GELU_C = math.sqrt(2.0 / math.pi)

TM_FFN = 512
TM_FFN_BWD = 256
TM_MIX = 256
TM_MIXC_FWD = 512
TK_WGRAD = 2048


def _params(limit=VMEM_LIMIT):
    return pltpu.CompilerParams(dimension_semantics=("arbitrary",), vmem_limit_bytes=limit)


def _dot(a, b):
    return jnp.dot(a, b, preferred_element_type=F32)


def _dot_nt(a, b):
    return lax.dot_general(a, b, (((1,), (1,)), ((), ())), preferred_element_type=F32)


def _dot_tn(a, b):
    return lax.dot_general(a, b, (((0,), (0,)), ((), ())), preferred_element_type=F32)


def _sigmoid(x):
    return 0.5 + 0.5 * jnp.tanh(0.5 * x)


def _gelu(x):
    t = jnp.tanh(GELU_C * (x + 0.044715 * x * x * x))
    return 0.5 * x * (1.0 + t), t


def _gelu_grad(x, t):
    return 0.5 * (1.0 + t) + 0.5 * x * (1.0 - t * t) * GELU_C * (1.0 + 3.0 * 0.044715 * x * x)


def _rms(x):
    r = lax.rsqrt(jnp.mean(x * x, axis=-1, keepdims=True) + EPS)
    return x * r, r


def _colsum(x):
    return jnp.sum(x, axis=0, keepdims=True)


def _pre_fwd(x, vec_ref):
    xn, r = _rms(x)
    n = xn * vec_ref[3:4, :]
    h = n * (1.0 + vec_ref[1:2, :]) + vec_ref[0:1, :]
    return h, xn, r, n


def _post_fwd(x, f, vec_ref, res_w):
    fn, _ = _rms(f)
    return x + fn * ((res_w * (1.0 + vec_ref[2:3, :])) * vec_ref[4:5, :])


def _post_bwd(dout, f, vec_ref, acc_ref, res_w):
    fn, r2 = _rms(f)
    acc_ref[2:3, :] += _colsum(dout * fn)
    dfn = dout * ((res_w * (1.0 + vec_ref[2:3, :])) * vec_ref[4:5, :])
    return r2 * (dfn - fn * jnp.mean(dfn * fn, axis=-1, keepdims=True))


def _pre_bwd(dout, dh, xn, r, vec_ref, acc_ref):
    acc_ref[0:1, :] += _colsum(dh)
    acc_ref[1:2, :] += _colsum(dh * xn)
    dxn = dh * ((1.0 + vec_ref[1:2, :]) * vec_ref[3:4, :])
    return dout + r * (dxn - xn * jnp.mean(dxn * xn, axis=-1, keepdims=True))


def _finish_acc(acc_ref, vec_ref, res_w):
    s_pre, s_post = acc_ref[1:2, :], acc_ref[2:3, :]
    acc_ref[1:2, :] = vec_ref[3:4, :] * s_pre
    acc_ref[3:4, :] = (1.0 + vec_ref[1:2, :]) * s_pre
    acc_ref[2:3, :] = (res_w * vec_ref[4:5, :]) * s_post
    acc_ref[4:5, :] = (res_w * (1.0 + vec_ref[2:3, :])) * s_post


def _tile(tm, ncol):
    return pl.BlockSpec((tm, ncol), lambda i: (i, 0))


def _full(shape):
    return pl.BlockSpec(shape, lambda i: (0,) * len(shape))


def _any():
    return pl.BlockSpec(memory_space=pl.ANY)


def _load_ffn_weights(w13_hbm, w2_hbm, w13_v, w2_v, tails, sems):
    copies = []
    for j in range(N_DEV):
        half, k = divmod(j, 4)
        copies.append((w13_hbm.at[j, :, pl.ds(0, FF_MAIN)], w13_v.at[:, pl.ds(D_FF * half + FF_MAIN * k, FF_MAIN)]))
        copies.append((w13_hbm.at[j, :, pl.ds(FF_MAIN, 128)], tails.at[j]))
    for k in range(4):
        copies.append((w2_hbm.at[2 * k], w2_v.at[pl.ds(FF_MAIN * k, W2_SHARD), :]))
        copies.append((w2_hbm.at[2 * k + 1, pl.ds(0, FF_MAIN - W2_SHARD), :],
                       w2_v.at[pl.ds(FF_MAIN * k + W2_SHARD, FF_MAIN - W2_SHARD), :]))
        copies.append((w2_hbm.at[2 * k + 1, pl.ds(FF_MAIN - W2_SHARD, FF_TAIL), :],
                       w2_v.at[pl.ds(4 * FF_MAIN + FF_TAIL * k, FF_TAIL), :]))
    copies = [pltpu.make_async_copy(src, dst, sems.at[n]) for n, (src, dst) in enumerate(copies)]
    for cp in copies:
        cp.start()
    for cp in copies:
        cp.wait()
    for pair in range(4):
        half, kk = divmod(pair, 2)
        base = D_FF * half + 4 * FF_MAIN + 128 * kk
        w13_v[:, base:base + 128] = tails[2 * pair] + tails[2 * pair + 1]


_FFN_SCRATCH = [pltpu.VMEM((D, 2 * D_FF), BF), pltpu.VMEM((D_FF, D), BF), pltpu.VMEM((N_DEV, D, 128), BF),
                pltpu.SemaphoreType.DMA((2 * N_DEV + 12,))]
HID_CHUNKS = ((0, 768), (768, 768), (1536, 768), (2304, 512))


def _hosted(body, n_in, n_out, n_scratch, n_comm, mode, grid):
    if not n_comm:
        return body

    def at(corner):
        hit = pl.program_id(0) == corner[0]
        for d in range(1, len(grid)):
            hit = hit & (pl.program_id(d) == corner[d])
        return hit

    def hosted(*refs):
        ins, cin = refs[:n_in], refs[n_in:n_in + n_comm]
        outs, cout = refs[n_in + n_comm:n_in + n_comm + n_out], refs[n_in + n_comm + n_out:n_in + 2 * n_comm + n_out]
        scratch = refs[n_in + 2 * n_comm + n_out:]
        own, sems = scratch[:n_scratch], scratch[n_scratch:]

        @pl.when(at([0] * len(grid)))
        def _():
            _exchange_ops(cin, cout, sems, mode, "start")

        body(*ins, *outs, *own)

        @pl.when(at([n - 1 for n in grid]))
        def _():
            _exchange_ops(cin, cout, sems, mode, "wait")

    return hosted


def _ffn_fwd(x, vec, w13g, w2g, res_w, name, comm=(), tgt=None):
    t_len = x.shape[0]
    tm = min(TM_FFN, t_len)
    nc = len(comm)
    head = tgt is not None

    def body(*refs):
        if head:
            x_ref, vec_ref, w13_hbm, w2_hbm, t_ref, xo_ref, f_ref, jac_ref, s_ref, loss_ref, w13_v, w2_v, tails, sems = refs
        else:
            x_ref, vec_ref, w13_hbm, w2_hbm, xo_ref, f_ref, jac_ref, s_ref, w13_v, w2_v, tails, sems = refs

        @pl.when(pl.program_id(0) == 0)
        def _():
            _load_ffn_weights(w13_hbm, w2_hbm, w13_v, w2_v, tails, sems)
            if head:
                loss_ref[...] = jnp.zeros((8, 128), F32)

        x_t = x_ref[...]
        h, _, _, _ = _pre_fwd(x_t, vec_ref)
        hb = h.astype(BF)
        for c0, cw in HID_CHUNKS:
            g = _dot(hb, w13_v[:, c0:c0 + cw])
            u = _dot(hb, w13_v[:, D_FF + c0:D_FF + c0 + cw])
            sig = _sigmoid(g)
            sl = g * sig
            jac_ref[0, :, c0:c0 + cw] = (u * (sig + sl * (1.0 - sig))).astype(BF)
            jac_ref[1, :, c0:c0 + cw] = sl.astype(BF)
            s_ref[:, c0:c0 + cw] = (sl * u).astype(BF)
        acc = _dot(s_ref[...], w2_v[...])
        f_ref[...] = acc
        xo = _post_fwd(x_t, acc, vec_ref, res_w)
        if head:
            err = xo - t_ref[...]
            xo_ref[...] = err * (1.0 / D)
            loss_ref[...] += jnp.sum(err * err) * (0.5 / D)
        else:
            xo_ref[...] = xo

    nt = t_len // tm
    n_in, n_out = (5, 5) if head else (4, 4)
    outs = pl.pallas_call(
        _hosted(body, n_in, n_out, 4, nc, "gather", (nt,)), grid=(nt,),
        in_specs=[_tile(tm, D), _full((8, D)), _any(), _any()] + ([_tile(tm, D)] if head else []) + [_any()] * nc,
        out_specs=[_tile(tm, D), _tile(tm, D), pl.BlockSpec((2, tm, D_FF), lambda i: (0, i, 0)), _tile(tm, D_FF)]
        + ([_full((8, 128))] if head else []) + [_any()] * nc,
        out_shape=[jax.ShapeDtypeStruct((t_len, D), F32), jax.ShapeDtypeStruct((t_len, D), F32),
                   jax.ShapeDtypeStruct((2, t_len, D_FF), BF), jax.ShapeDtypeStruct((t_len, D_FF), BF)]
        + ([jax.ShapeDtypeStruct((8, 128), F32)] if head else []) + _exchange_shapes(comm, "gather"),
        scratch_shapes=_FFN_SCRATCH + (_exchange_scratch(nc) if nc else []), compiler_params=_params(), name=name,
    )(x, vec, w13g, w2g, *([tgt] if head else []), *comm)
    return outs[:n_out], outs[n_out:]


def _ffn_bwd(dout, x, fpre, jac, vec, w13g, w2g, res_w, name, comm=()):
    t_len = x.shape[0]
    tm = min(TM_FFN_BWD, t_len)
    nt = t_len // tm
    nc = len(comm)

    def body(dout_ref, x_ref, f_ref, jac_ref, vec_ref, w13_hbm, w2_hbm,
             dx_ref, dgu_ref, hb_ref, dfb_ref, acc_ref, w13_v, w2_v, tails, sems):
        @pl.when(pl.program_id(0) == 0)
        def _():
            _load_ffn_weights(w13_hbm, w2_hbm, w13_v, w2_v, tails, sems)
            acc_ref[...] = jnp.zeros((8, D), F32)

        dout_t = dout_ref[...]
        df = _post_bwd(dout_t, f_ref[...], vec_ref, acc_ref, res_w)
        dfb = df.astype(BF)
        dfb_ref[...] = dfb
        h, xn, r, _ = _pre_fwd(x_ref[...], vec_ref)
        hb_ref[...] = h.astype(BF)
        for c0, cw in HID_CHUNKS:
            ds = _dot_nt(dfb, w2_v[c0:c0 + cw, :]).astype(BF)
            dgu_ref[:, c0:c0 + cw] = ds * jac_ref[0, :, c0:c0 + cw]
            dgu_ref[:, D_FF + c0:D_FF + c0 + cw] = ds * jac_ref[1, :, c0:c0 + cw]
        dh = _dot_nt(dgu_ref[...], w13_v[...])
        dx_ref[...] = _pre_bwd(dout_t, dh, xn, r, vec_ref, acc_ref)

        @pl.when(pl.program_id(0) == nt - 1)
        def _():
            _finish_acc(acc_ref, vec_ref, res_w)

    jac_spec = pl.BlockSpec((2, tm, D_FF), lambda i: (0, i, 0))
    outs = pl.pallas_call(
        _hosted(body, 7, 5, 4, nc, "scatter", (nt,)), grid=(nt,),
        in_specs=[_tile(tm, D), _tile(tm, D), _tile(tm, D), jac_spec, _full((8, D)), _any(), _any()] + [_any()] * nc,
        out_specs=[_tile(tm, D), _tile(tm, 2 * D_FF), _tile(tm, D), _tile(tm, D), _full((8, D))] + [_any()] * nc,
        out_shape=[jax.ShapeDtypeStruct((t_len, D), F32), jax.ShapeDtypeStruct((t_len, 2 * D_FF), BF),
                   jax.ShapeDtypeStruct((t_len, D), BF),
                   jax.ShapeDtypeStruct((t_len, D), BF), jax.ShapeDtypeStruct((8, D), F32)] + _exchange_shapes(comm, "scatter"),
        scratch_shapes=_FFN_SCRATCH + (_exchange_scratch(nc) if nc else []), compiler_params=_params(), name=name,
    )(dout, x, fpre, jac, vec, w13g, w2g, *comm)
    return outs[:5], outs[5:]


def _w13_slots(acc, o_ref):
    for k in range(4):
        o_ref[k, :, 0:FF_MAIN] = acc[:, FF_MAIN * k:FF_MAIN * (k + 1)].astype(BF)
        pair_tile = acc[:, 4 * FF_MAIN + 128 * (k // 2):4 * FF_MAIN + 128 * (k // 2 + 1)]
        o_ref[k, :, FF_MAIN:FF_PAD] = (pair_tile if k % 2 == 0 else pltpu.roll(pair_tile, FF_TAIL, 1)).astype(BF)


def _w2_slots(acc, o_ref):
    rest = FF_MAIN - W2_SHARD
    for k in range(4):
        o_ref[2 * k] = acc[FF_MAIN * k:FF_MAIN * k + W2_SHARD, :].astype(BF)
        o_ref[2 * k + 1, 0:rest, :] = acc[FF_MAIN * k + W2_SHARD:FF_MAIN * (k + 1), :].astype(BF)
        o_ref[2 * k + 1, rest:W2_SHARD, :] = acc[4 * FF_MAIN + FF_TAIL * k:4 * FF_MAIN + FF_TAIL * (k + 1), :].astype(BF)


def _wgrad(a, b, j_count, m, n, a_mode, b_mode, out_rows, out_dtype, name, tk=TK_WGRAD, col_slots=1, comm=(),
           slots=None):
    t_len = a.shape[-2]
    tk = min(tk, t_len)
    nk = t_len // tk
    wn = n // col_slots
    nc = len(comm)

    def spec(mode, width):
        if mode == "stack":
            return pl.BlockSpec((None, tk, width), lambda j, t: (j, t, 0))
        if mode == "cols":
            return pl.BlockSpec((tk, width), lambda j, t: (t, j))
        return pl.BlockSpec((tk, width), lambda j, t: (t, 0))

    def body(a_ref, b_ref, o_ref, acc):
        t = pl.program_id(1)

        @pl.when(t == 0)
        def _():
            acc[...] = jnp.zeros((m, n), F32)

        acc[...] += _dot_tn(a_ref[...], b_ref[...])

        @pl.when(t == nk - 1)
        def _():
            if slots is not None:
                slots[0](acc, o_ref)
            elif col_slots == 1:
                o_ref[...] = acc[0:out_rows, :].astype(out_dtype)
            else:
                for s in range(col_slots):
                    o_ref[s] = acc[0:out_rows, wn * s:wn * (s + 1)].astype(out_dtype)

    if slots is not None:
        blk = slots[1]
        out_spec = pl.BlockSpec(blk, lambda j, t: (j,) + (0,) * (len(blk) - 1))
        out_shape = jax.ShapeDtypeStruct((j_count * blk[0],) + blk[1:], BF)
    elif col_slots == 1:
        out_spec = pl.BlockSpec((None, out_rows, n), lambda j, t: (j, 0, 0))
        out_shape = jax.ShapeDtypeStruct((j_count, out_rows, n), out_dtype)
    else:
        out_spec = pl.BlockSpec((col_slots, out_rows, wn), lambda j, t: (0, 0, 0))
        out_shape = jax.ShapeDtypeStruct((col_slots, out_rows, wn), out_dtype)
    outs = pl.pallas_call(
        _hosted(body, 2, 1, 1, nc, "scatter", (j_count, nk)), grid=(j_count, nk),
        in_specs=[spec(a_mode, m), spec(b_mode, n)] + [_any()] * nc,
        out_specs=[out_spec] + [_any()] * nc, out_shape=[out_shape] + _exchange_shapes(comm, "scatter"),
        scratch_shapes=[pltpu.VMEM((m, n), F32)] + (_exchange_scratch(nc) if nc else []),
        compiler_params=pltpu.CompilerParams(dimension_semantics=("arbitrary", "arbitrary"), vmem_limit_bytes=VMEM_LIMIT),
        name=name,
    )(a, b, *comm)
    return (outs[0], list(outs[1:])) if nc else outs[0]


def _c_mask_weights(ws_ref, wsm, wsmt):
    row = lax.broadcasted_iota(jnp.int32, (CHUNK, CHUNK), 0)
    col = lax.broadcasted_iota(jnp.int32, (CHUNK, CHUNK), 1)
    for hh in range(H_C):
        w = jnp.where(row >= col, ws_ref[hh], 0.0)
        wsm[hh] = w.astype(BF)
        if wsmt is not None:
            wsmt[hh] = w.T.astype(BF)


def _c_inner(pre, cvec_ref, wsm, bst_ref, mix_sc, tm):
    z, t = _gelu(pre)
    u = z[:, 0:D]
    v = z[:, D:2 * D]
    mu = jnp.mean(v, axis=-1, keepdims=True)
    vc = v - mu
    rstd = lax.rsqrt(jnp.mean(vc * vc, axis=-1, keepdims=True) + EPS)
    vhat = vc * rstd
    vnb = (vhat * cvec_ref[0:1, :] + cvec_ref[1:2, :]).astype(BF)
    for nn in range(tm // CHUNK):
        for hh in range(H_C):
            rows = slice(CHUNK * nn, CHUNK * (nn + 1))
            cols = slice(CHUNK * hh, CHUNK * (hh + 1))
            mix_sc[rows, cols] = _dot(wsm[hh], vnb[rows, cols]) + bst_ref[:, hh:hh + 1]
    return u, t, rstd, vhat, vnb


C_SAVED = 6 * D


def _mixc_fwd(x, vec, w_in, b_in, cvec, ws, bst, w_out, name):
    t_len = x.shape[0]
    tm = min(TM_MIXC_FWD, t_len)

    def body(x_ref, vec_ref, win_ref, bin_ref, cvec_ref, ws_ref, bst_ref, wout_ref,
             xo_ref, f_ref, sv_ref, rstd_ref, p_ref, wsm, mix_sc, pre_sc):
        @pl.when(pl.program_id(0) == 0)
        def _():
            _c_mask_weights(ws_ref, wsm, None)

        x_t = x_ref[...]
        h, _, _, _ = _pre_fwd(x_t, vec_ref)
        hb = h.astype(BF)
        for j in range(N_DEV):
            cols = slice(256 * j, 256 * (j + 1))
            pre_sc[:, cols] = _dot(hb, win_ref[j]) + bin_ref[:, cols]
        pre = pre_sc[...]
        u, t, rstd, vhat, vnb = _c_inner(pre, cvec_ref, wsm, bst_ref, mix_sc, tm)
        mix = mix_sc[...]
        pb = (u * mix).astype(BF)
        gg = _gelu_grad(pre, t)
        sv_ref[:, 0:D] = u.astype(BF)
        sv_ref[:, D:2 * D] = mix.astype(BF)
        sv_ref[:, 2 * D:4 * D] = gg.astype(BF)
        sv_ref[:, 4 * D:5 * D] = vhat.astype(BF)
        sv_ref[:, 5 * D:6 * D] = vnb
        rstd_ref[...] = jnp.broadcast_to(rstd, (tm, 128))
        p_ref[...] = pb
        fpre = _dot(pb, wout_ref[...])
        f_ref[...] = fpre
        xo_ref[...] = _post_fwd(x_t, fpre, vec_ref, 1.0)

    return pl.pallas_call(
        body, grid=(t_len // tm,),
        in_specs=[_tile(tm, D), _full((8, D)), _full((N_DEV, D, 256)), _full((1, 2 * D)), _full((8, D)),
                  _full((H_C, CHUNK, CHUNK)), _full((CHUNK, H_C)), _full((D, D))],
        out_specs=[_tile(tm, D), _tile(tm, D), _tile(tm, C_SAVED), _tile(tm, 128), _tile(tm, D)],
        out_shape=[jax.ShapeDtypeStruct((t_len, D), F32), jax.ShapeDtypeStruct((t_len, D), F32),
                   jax.ShapeDtypeStruct((t_len, C_SAVED), BF), jax.ShapeDtypeStruct((t_len, 128), F32),
                   jax.ShapeDtypeStruct((t_len, D), BF)],
        scratch_shapes=[pltpu.VMEM((H_C, CHUNK, CHUNK), BF), pltpu.VMEM((tm, D), F32), pltpu.VMEM((tm, 2 * D), F32)],
        compiler_params=_params(), name=name,
    )(x, vec, w_in, b_in, cvec, ws, bst, w_out)


def _mixc_bwd(dout, x, fpre, saved, rstd_b, vec, w_in, cvec, ws, bst, w_out, name):
    t_len = x.shape[0]
    tm = min(TM_MIX, t_len)
    nt = t_len // tm

    def body(dout_ref, x_ref, f_ref, sv_ref, rstd_ref, vec_ref, win_ref, cvec_ref, ws_ref, bst_ref, wout_ref,
             dx_ref, dpre_ref, hb_ref, dfb_ref, acc_ref, dbin_ref, dws_ref, dbst_ref,
             wsm, wsmt, dvn_sc, dmsum, win_v):
        i = pl.program_id(0)

        @pl.when(i == 0)
        def _():
            _c_mask_weights(ws_ref, wsm, wsmt)
            for j in range(N_DEV):
                win_v[:, 256 * j:256 * (j + 1)] = win_ref[j]
            acc_ref[...] = jnp.zeros((8, D), F32)
            dbin_ref[...] = jnp.zeros((8, 2 * D), F32)
            dws_ref[...] = jnp.zeros((H_C, CHUNK, CHUNK), F32)
            dmsum[...] = jnp.zeros((CHUNK, D), F32)

        dout_t = dout_ref[...]
        df = _post_bwd(dout_t, f_ref[...], vec_ref, acc_ref, 1.0)
        dfb = df.astype(BF)
        dfb_ref[...] = dfb
        h, xn, r, _ = _pre_fwd(x_ref[...], vec_ref)
        hb_ref[...] = h.astype(BF)
        u = sv_ref[:, 0:D].astype(F32)
        mix = sv_ref[:, D:2 * D].astype(F32)
        vhat = sv_ref[:, 4 * D:5 * D].astype(F32)
        rstd = rstd_ref[:, 0:1]
        dp = _dot_nt(dfb, wout_ref[...])
        du = dp * mix
        dmix = dp * u
        dmb = dmix.astype(BF)
        for nn in range(tm // CHUNK):
            rows = slice(CHUNK * nn, CHUNK * (nn + 1))
            dmsum[...] += dmix[rows, :]
            for hh in range(H_C):
                cols = slice(CHUNK * hh, CHUNK * (hh + 1))
                dvn_sc[rows, cols] = _dot(wsmt[hh], dmb[rows, cols])
                dws_ref[hh] += _dot_nt(dmb[rows, cols], sv_ref[rows, 5 * D + CHUNK * hh:5 * D + CHUNK * (hh + 1)])
        dvn = dvn_sc[...]
        acc_ref[5:6, :] += _colsum(dvn * vhat)
        acc_ref[6:7, :] += _colsum(dvn)
        dvhat = dvn * cvec_ref[0:1, :]
        dv = rstd * (dvhat - jnp.mean(dvhat, axis=-1, keepdims=True)
                     - vhat * jnp.mean(dvhat * vhat, axis=-1, keepdims=True))
        dpre_u = du * sv_ref[:, 2 * D:3 * D].astype(F32)
        dpre_v = dv * sv_ref[:, 3 * D:4 * D].astype(F32)
        dbin_ref[0:1, 0:D] += _colsum(dpre_u)
        dbin_ref[0:1, D:2 * D] += _colsum(dpre_v)
        dpre_ref[:, 0:D] = dpre_u.astype(BF)
        dpre_ref[:, D:2 * D] = dpre_v.astype(BF)
        dh = _dot_nt(dpre_ref[...], win_v[...])
        dx_ref[...] = _pre_bwd(dout_t, dh, xn, r, vec_ref, acc_ref)

        @pl.when(i == nt - 1)
        def _():
            _finish_acc(acc_ref, vec_ref, 1.0)
            row = lax.broadcasted_iota(jnp.int32, (CHUNK, CHUNK), 0)
            col = lax.broadcasted_iota(jnp.int32, (CHUNK, CHUNK), 1)
            for hh in range(H_C):
                dws_ref[hh] = jnp.where(row >= col, dws_ref[hh], 0.0)
                dbst_ref[:, hh:hh + 1] = jnp.sum(dmsum[:, CHUNK * hh:CHUNK * (hh + 1)], axis=1, keepdims=True)

    return pl.pallas_call(
        body, grid=(nt,),
        in_specs=[_tile(tm, D), _tile(tm, D), _tile(tm, D), _tile(tm, C_SAVED), _tile(tm, 128), _full((8, D)),
                  _full((N_DEV, D, 256)), _full((8, D)), _full((H_C, CHUNK, CHUNK)), _full((CHUNK, H_C)), _full((D, D))],
        out_specs=[_tile(tm, D), _tile(tm, 2 * D), _tile(tm, D), _tile(tm, D), _full((8, D)),
                   _full((8, 2 * D)), _full((H_C, CHUNK, CHUNK)), _full((CHUNK, H_C))],
        out_shape=[jax.ShapeDtypeStruct((t_len, D), F32), jax.ShapeDtypeStruct((t_len, 2 * D), BF),
                   jax.ShapeDtypeStruct((t_len, D), BF),
                   jax.ShapeDtypeStruct((t_len, D), BF), jax.ShapeDtypeStruct((8, D), F32),
                   jax.ShapeDtypeStruct((8, 2 * D), F32), jax.ShapeDtypeStruct((H_C, CHUNK, CHUNK), F32),
                   jax.ShapeDtypeStruct((CHUNK, H_C), F32)],
        scratch_shapes=[pltpu.VMEM((H_C, CHUNK, CHUNK), BF), pltpu.VMEM((H_C, CHUNK, CHUNK), BF),
                        pltpu.VMEM((tm, D), F32), pltpu.VMEM((CHUNK, D), F32), pltpu.VMEM((D, 2 * D), BF)],
        compiler_params=_params(), name=name,
    )(dout, x, fpre, saved, rstd_b, vec, w_in, cvec, ws, bst, w_out)


def _gmean(x, g_ref):
    hi = x.astype(BF)
    lo = (x - hi.astype(F32)).astype(BF)
    return _dot(hi, g_ref[...]) + _dot(lo, g_ref[...])


def _log_sigmoid(lam):
    e = jnp.exp(-jnp.abs(lam))
    log1p = jnp.where(e < 1e-2, e * (1.0 - e * (0.5 - e * (1.0 / 3.0 - 0.25 * e))), jnp.log(1.0 + e))
    return jnp.minimum(lam, 0.0) - log1p


def _neg_expm1(y):
    series = -(y * (1.0 + y * (0.5 + y * (1.0 / 6.0 + y * (1.0 / 24.0 + y * (1.0 / 120.0))))))
    return jnp.where(y > -0.1, series, 1.0 - jnp.exp(y))


def _rows_from(e, off, tm):
    return e[off:off + tm, :] if off % 8 == 0 else pltpu.roll(e, e.shape[0] - off, 0)[0:tm, :]


def _conv_causal(ext, taps_ref, bias, k_taps, halo, tm):
    e = ext[...]
    acc = bias
    for k in range(k_taps):
        acc = acc + taps_ref[k:k + 1, :] * _rows_from(e, halo - k_taps + 1 + k, tm)
    return acc


def _build_shifted(sh_ref, e, n_rows):
    sh_ref[0] = e
    for r in range(1, 8):
        sh_ref[r] = pltpu.roll(e, n_rows - r, 0)


def _shifted_rows(sh_ref, off, tm):
    base = off - off % 8
    return sh_ref[off % 8, base:base + tm, :]


def _scan(a, u, tm, reverse):
    row = lax.broadcasted_iota(jnp.int32, (tm, W_A), 0)
    d = 1
    while d < tm:
        if reverse:
            keep = row < tm - d
            shift = tm - d
        else:
            keep = row >= d
            shift = d
        a_sh = jnp.where(keep, pltpu.roll(a, shift, 0), 1.0)
        u_sh = jnp.where(keep, pltpu.roll(u, shift, 0), 0.0)
        u = a * u_sh + u
        a = a * a_sh
        d *= 2
    return a, u


def _a_gates(xc, cv_ref, wr_ref, wi_ref):
    xcb = xc.astype(BF)
    r = _sigmoid(_dot(xcb, wr_ref[...]) + cv_ref[5:6, :])
    ig = _sigmoid(_dot(xcb, wi_ref[...]) + cv_ref[6:7, :])
    ls = _log_sigmoid(cv_ref[7:8, :])
    la = LRU_C * r * ls
    a = jnp.exp(la)
    m = jnp.sqrt(_neg_expm1(2.0 * la))
    return xcb, r, ig, ls, a, m


def _b_norm(vc, cv_ref, g_ref):
    mu = _gmean(vc, g_ref)
    dv = vc - mu
    rstd = lax.rsqrt(_gmean(dv * dv, g_ref) + EPS)
    vhat = dv * rstd
    vln = vhat * cv_ref[9:10, :] + cv_ref[10:11, :]
    return rstd, vhat, vln


def _mixab_fwd(x, vec, w_in, cv, w31, wr, wi, gmat, w_out, name):
    t_len = x.shape[0]
    tm = min(TM_MIX, t_len)

    def body(x_ref, vec_ref, win_ref, cv_ref, w31_ref, wr_ref, wi_ref, g_ref, wout_ref,
             xo_ref, f_ref, z_ref, hs_ref, cvs_ref, ext_a, ext_b, hc, shifted):
        @pl.when(pl.program_id(0) == 0)
        def _():
            ext_a[0:HALO_A, :] = jnp.zeros((HALO_A, W_A), F32)
            ext_b[0:HALO_B, :] = jnp.zeros((HALO_B, W_B), F32)
            hc[...] = jnp.zeros((8, W_A), F32)

        x_t = x_ref[...]
        h, _, _, _ = _pre_fwd(x_t, vec_ref)
        hb = h.astype(BF)
        for j in range(N_DEV):
            z_ref[:, 256 * j:256 * (j + 1)] = _dot(hb, win_ref[j])
        ext_a[HALO_A:HALO_A + tm, :] = z_ref[:, W_A:2 * W_A]
        xc = _conv_causal(ext_a, cv_ref, cv_ref[4:5, :], CONV_A, HALO_A, tm)
        ext_a[0:HALO_A, :] = ext_a[tm:tm + HALO_A, :]
        cvs_ref[:, 0:W_A] = xc
        _, r, ig, _, a, m = _a_gates(xc, cv_ref, wr_ref, wi_ref)
        for col, val in enumerate((r, ig, a, m)):
            cvs_ref[:, W_A * (2 + col):W_A * (3 + col)] = val
        a_cum, hloc = _scan(a, m * ig * xc, tm, False)
        hs = hloc + a_cum * hc[0:1, :]
        hs_ref[...] = hs
        hc[0:1, :] = hs[tm - 1:tm, :]
        ag = z_ref[:, 0:W_A]
        gel, tg = _gelu(ag)
        cvs_ref[:, 7 * W_A:8 * W_A] = gel
        cvs_ref[:, 8 * W_A:9 * W_A] = _gelu_grad(ag, tg)
        ya = hs * gel
        ext_b[HALO_B:HALO_B + tm, :] = z_ref[:, 2 * W_A:2 * W_A + W_B] * _sigmoid(z_ref[:, 2 * W_A + W_B:2 * W_A + 2 * W_B])
        _build_shifted(shifted, ext_b[...], tm + HALO_B)
        vc = cv_ref[8:9, :] + w31_ref[0:1, :] * _shifted_rows(shifted, HALO_B - CONV_B + 1, tm)
        for k in range(1, CONV_B):
            vc = vc + w31_ref[k:k + 1, :] * _shifted_rows(shifted, HALO_B - CONV_B + 1 + k, tm)
        ext_b[0:HALO_B, :] = ext_b[tm:tm + HALO_B, :]
        rstd, vhat, vln = _b_norm(vc, cv_ref, g_ref)
        cvs_ref[:, W_A:W_A + W_B] = vhat
        cvs_ref[:, 6 * W_A:7 * W_A] = rstd
        yb = vln * _sigmoid(vln)
        fpre = _dot(ya.astype(BF), wout_ref[0:W_A, :]) + _dot(yb.astype(BF), wout_ref[W_A:W_A + W_B, :])
        f_ref[...] = fpre
        xo_ref[...] = _post_fwd(x_t, fpre, vec_ref, 1.0)

    return pl.pallas_call(
        body, grid=(t_len // tm,),
        in_specs=[_tile(tm, D), _full((8, D)), _full((N_DEV, D, 256)), _full((16, W_A)), _full((32, W_B)),
                  _full((W_A, W_A)), _full((W_A, W_A)), _full((W_B, W_B)), _full((D, D))],
        out_specs=[_tile(tm, D), _tile(tm, D), _tile(tm, 2 * D), _tile(tm, W_A), _tile(tm, AB_SAVED)],
        out_shape=[jax.ShapeDtypeStruct((t_len, D), F32), jax.ShapeDtypeStruct((t_len, D), F32),
                   jax.ShapeDtypeStruct((t_len, 2 * D), F32), jax.ShapeDtypeStruct((t_len, W_A), F32),
                   jax.ShapeDtypeStruct((t_len, AB_SAVED), F32)],
        scratch_shapes=[pltpu.VMEM((tm + HALO_A, W_A), F32), pltpu.VMEM((tm + HALO_B, W_B), F32), pltpu.VMEM((8, W_A), F32),
                        pltpu.VMEM((8, tm + HALO_B, W_B), F32)],
        compiler_params=_params(), name=name,
    )(x, vec, w_in, cv, w31, wr, wi, gmat, w_out)


def _mixab_bwd(dout, x, fpre, z, cvs, hs, vec, w_in, cv, w31, wr, wi, gmat, w_out, name, comm=()):
    t_len = x.shape[0]
    tm = min(TM_MIX, t_len)
    nt = t_len // tm

    def rev(i):
        return nt - 1 - i

    def rtile(ncol):
        return pl.BlockSpec((tm, ncol), lambda i: (rev(i), 0))

    def body(dout_ref, x_ref, f_ref, z_ref, cvs_ref, hs_ref, hsp_ref, vec_ref, win_ref, cv_ref, w31_ref, wr_ref, wi_ref,
             g_ref, wout_ref,
             dx_ref, dz_ref, yab_ref, hb_ref, dfb_ref, xcb_ref, dri_ref, acc_ref, accs_ref, dw31_ref,
             ext_h, ext_dx, ext_dv, carry, shifted, win_v):
        i = pl.program_id(0)
        has_prev = (rev(i) > 0).astype(F32)

        @pl.when(i == 0)
        def _():
            for j in range(N_DEV):
                win_v[:, 256 * j:256 * (j + 1)] = win_ref[j]
            acc_ref[...] = jnp.zeros((8, D), F32)
            accs_ref[...] = jnp.zeros((16, W_A), F32)
            dw31_ref[...] = jnp.zeros((32, W_B), F32)
            ext_dx[tm:tm + HALO_A, :] = jnp.zeros((HALO_A, W_A), F32)
            ext_dv[tm:tm + HALO_B, :] = jnp.zeros((HALO_B, W_B), F32)
            carry[...] = jnp.zeros((8, W_A), F32)

        dout_t = dout_ref[...]
        df = _post_bwd(dout_t, f_ref[...], vec_ref, acc_ref, 1.0)
        dfb = df.astype(BF)
        dfb_ref[...] = dfb
        h, xn, r_x, n = _pre_fwd(x_ref[...], vec_ref)
        hb_ref[...] = h.astype(BF)

        ag = z_ref[:, 0:W_A]
        ax = z_ref[:, W_A:2 * W_A]
        bv = z_ref[:, 2 * W_A:2 * W_A + W_B]
        sg = _sigmoid(z_ref[:, 2 * W_A + W_B:2 * W_A + 2 * W_B])
        vv = bv * sg
        xc = cvs_ref[:, 0:W_A]
        r, ig, a, m = (cvs_ref[:, W_A * (2 + col):W_A * (3 + col)] for col in range(4))
        ls = _log_sigmoid(cv_ref[7:8, :])
        xcb_ref[...] = xc.astype(BF)
        hs_t = hs_ref[...]
        ext_h[0:8, :] = hsp_ref[...] * has_prev
        ext_h[8:8 + tm, :] = hs_t
        hprev = ext_h[7:7 + tm, :]
        gel = cvs_ref[:, 7 * W_A:8 * W_A]
        vhat = cvs_ref[:, W_A:W_A + W_B]
        rstd = cvs_ref[:, 6 * W_A:7 * W_A]
        vln = vhat * cv_ref[9:10, :] + cv_ref[10:11, :]
        sv = _sigmoid(vln)
        yab_ref[:, 0:W_A] = (hs_t * gel).astype(BF)
        yab_ref[:, W_A:W_A + W_B] = (vln * sv).astype(BF)

        dya = _dot_nt(dfb, wout_ref[0:W_A, :])
        dyb = _dot_nt(dfb, wout_ref[W_A:W_A + W_B, :])

        dag = dya * hs_t * cvs_ref[:, 8 * W_A:9 * W_A]
        row = lax.broadcasted_iota(jnp.int32, (tm, W_A), 0)
        last = row == tm - 1
        a_next = jnp.where(last, 1.0, pltpu.roll(a, tm - 1, 0))
        u0 = dya * gel + jnp.where(last, carry[0:1, :], 0.0)
        _, dhs = _scan(a_next, u0, tm, True)
        carry[0:1, :] = a[0:1, :] * dhs[0:1, :]
        da = dhs * hprev
        dm = dhs * ig * xc
        di = dhs * m * xc
        dxc = dhs * m * ig
        dla = da * a - dm * (a * a) / m
        accs_ref[7:8, :] += _colsum(dla * r) * (LRU_C * _sigmoid(-cv_ref[7:8, :]))
        drp = (dla * (LRU_C * ls)) * r * (1.0 - r)
        dip = di * ig * (1.0 - ig)
        accs_ref[5:6, :] += _colsum(drp)
        accs_ref[6:7, :] += _colsum(dip)
        drpb = drp.astype(BF)
        dipb = dip.astype(BF)
        dri_ref[:, 0:W_A] = drpb
        dri_ref[:, W_A:2 * W_A] = dipb
        dxc = dxc + _dot_nt(drpb, wr_ref[...]) + _dot_nt(dipb, wi_ref[...])
        accs_ref[4:5, :] += _colsum(dxc)
        ext_dx[0:tm, :] = dxc
        e_dx = ext_dx[...]
        dax = jnp.zeros((tm, W_A), F32)
        for k in range(CONV_A):
            ahead = _rows_from(e_dx, CONV_A - 1 - k, tm)
            accs_ref[k:k + 1, :] += _colsum(ax * ahead)
            dax = dax + cv_ref[k:k + 1, :] * ahead
        ext_dx[tm:tm + HALO_A, :] = dxc[0:HALO_A, :]

        dvln = dyb * (sv * (1.0 + vln * (1.0 - sv)))
        accs_ref[9:10, :] += _colsum(dvln * vhat)
        accs_ref[10:11, :] += _colsum(dvln)
        dvhat = dvln * cv_ref[9:10, :]
        dvc = rstd * (dvhat - _gmean(dvhat, g_ref) - vhat * _gmean(dvhat * vhat, g_ref))
        accs_ref[8:9, :] += _colsum(dvc)
        ext_dv[0:tm, :] = dvc
        _build_shifted(shifted, ext_dv[...], tm + HALO_B)
        dvv = jnp.zeros((tm, W_B), F32)
        for k in range(CONV_B):
            ahead = _shifted_rows(shifted, CONV_B - 1 - k, tm)
            dw31_ref[k:k + 1, :] += _colsum(vv * ahead)
            dvv = dvv + w31_ref[k:k + 1, :] * ahead
        ext_dv[tm:tm + HALO_B, :] = dvc[0:HALO_B, :]

        dz_ref[:, 0:W_A] = dag.astype(BF)
        dz_ref[:, W_A:2 * W_A] = dax.astype(BF)
        dz_ref[:, 2 * W_A:2 * W_A + W_B] = (dvv * sg).astype(BF)
        dz_ref[:, 2 * W_A + W_B:2 * W_A + 2 * W_B] = (dvv * vv * (1.0 - sg)).astype(BF)
        dh = _dot_nt(dz_ref[...], win_v[...])
        dx_ref[...] = _pre_bwd(dout_t, dh, xn, r_x, vec_ref, acc_ref)

        @pl.when(i == nt - 1)
        def _():
            _finish_acc(acc_ref, vec_ref, 1.0)

    hsp_spec = pl.BlockSpec((8, W_A), lambda i: (jnp.maximum(rev(i) * (tm // 8) - 1, 0), 0))
    nc = len(comm)
    outs = pl.pallas_call(
        _hosted(body, 15, 10, 6, nc, "scatter", (nt,)), grid=(nt,),
        in_specs=[rtile(D), rtile(D), rtile(D), rtile(2 * D), rtile(AB_SAVED), rtile(W_A), hsp_spec, _full((8, D)),
                  _full((N_DEV, D, 256)), _full((16, W_A)), _full((32, W_B)), _full((W_A, W_A)), _full((W_A, W_A)),
                  _full((W_B, W_B)), _full((D, D))] + [_any()] * nc,
        out_specs=[rtile(D), rtile(2 * D), rtile(D), rtile(D), rtile(D), rtile(W_A), rtile(2 * W_A), _full((8, D)),
                   _full((16, W_A)), _full((32, W_B))] + [_any()] * nc,
        out_shape=[jax.ShapeDtypeStruct((t_len, D), F32), jax.ShapeDtypeStruct((t_len, 2 * D), BF),
                   jax.ShapeDtypeStruct((t_len, D), BF), jax.ShapeDtypeStruct((t_len, D), BF),
                   jax.ShapeDtypeStruct((t_len, D), BF), jax.ShapeDtypeStruct((t_len, W_A), BF),
                   jax.ShapeDtypeStruct((t_len, 2 * W_A), BF), jax.ShapeDtypeStruct((8, D), F32),
                   jax.ShapeDtypeStruct((16, W_A), F32), jax.ShapeDtypeStruct((32, W_B), F32)] + _exchange_shapes(comm, "scatter"),
        scratch_shapes=[pltpu.VMEM((tm + 8, W_A), F32), pltpu.VMEM((tm + HALO_A, W_A), F32),
                        pltpu.VMEM((tm + HALO_B, W_B), F32), pltpu.VMEM((8, W_A), F32),
                        pltpu.VMEM((8, tm + HALO_B, W_B), F32), pltpu.VMEM((D, 2 * D), BF)]
        + (_exchange_scratch(nc) if nc else []),
        compiler_params=_params(), name=name,
    )(dout, x, fpre, z, cvs, hs, hs, vec, w_in, cv, w31, wr, wi, gmat, w_out, *comm)
    return outs[:10], list(outs[10:])


def _vec(p, l, j):
    return jnp.concatenate([p["mod"][l, j], p["norm_pre"][l, j][None], p["norm_post"][l, j][None], jnp.zeros((3, D), F32)], 0)


def _ab_consts(p):
    gw = p["a_gate_w"]
    gb = p["a_gate_b"]
    half = W_A // 8
    eye = jnp.eye(8, dtype=F32)[:, None, :, None]

    def block_diag(blocks):
        return (blocks[:, :, None, :] * eye).reshape(W_A, W_A).astype(BF)

    wr = block_diag(gw[:, :, 0:half])
    wi = block_diag(gw[:, :, half:2 * half])
    rows = [p["a_conv_w"], p["a_conv_b"][None], gb[:, 0:half].reshape(1, W_A), gb[:, half:2 * half].reshape(1, W_A),
            p["a_lam"][None], p["b_conv_b"][None], p["b_norm_g"][None], p["b_norm_b"][None], jnp.zeros((5, W_A), F32)]
    cv = jnp.concatenate(rows, 0)
    w31 = jnp.concatenate([p["b_conv_w"], jnp.zeros((1, W_B), F32)], 0)
    grp = jnp.arange(W_B) // (W_B // 8)
    gmat = ((grp[:, None] == grp[None, :]).astype(F32) / (W_B // 8)).astype(BF)
    return cv, w31, wr, wi, gmat


SUBLAYERS = ("f0", "ab", "f1", "f2", "c", "f3")


def _local_step(x, tgt, p, plan=None):
    g = {}
    saved = []
    cur = x
    wsets = dict(p["wsets"])
    ab_c = _ab_consts(p)
    c_cvec = jnp.concatenate([p["c_norm_g"][None], p["c_norm_b"][None], jnp.zeros((6, D), F32)], 0)
    c_bst = jnp.transpose(p["c_b_s"])
    c_bin = p["c_b_in"][None]
    for s_idx, wname in enumerate(SUBLAYERS):
        l, j = divmod(s_idx, 3)
        vec = _vec(p, l, j)
        tag = f"l{l}s{j}"
        if j != 1:
            names = plan["gather"].get(s_idx, []) if plan else []
            comm = [a for nm in names for a in plan["shards"][nm]]
            res, got = _ffn_fwd(cur, vec, *wsets[wname], 0.5, "ffn_fwd_" + tag, comm=comm, tgt=tgt if s_idx == 5 else None)
            nxt, fpre, jac, s_act = res[:4]
            for k, nm in enumerate(names):
                wsets[nm] = list(got[2 * k:2 * k + 2])
            saved.append((cur, fpre, jac, s_act, vec))
            if s_idx == 5:
                loss_blk = res[4]
        elif l == 0:
            w_in, w_out = wsets[wname]
            nxt, fpre, z, hs, cvs = _mixab_fwd(cur, vec, w_in, *ab_c, w_out.reshape(D, D), "mixab_fwd_" + tag)
            saved.append((cur, fpre, z, hs, cvs, vec))
        else:
            w_in, w_out = wsets[wname]
            nxt, fpre, c_saved, c_rstd, pb = _mixc_fwd(cur, vec, w_in, c_bin, c_cvec, p["c_w_s"], c_bst, w_out.reshape(D, D),
                                                       "mixc_fwd_" + tag)
            saved.append((cur, fpre, c_saved, c_rstd, pb, vec))
        cur = nxt
    dcur = cur
    accs, pending, recv = {}, {}, {}

    def take(host):
        keys = plan["scatter"].get(host, []) if plan else []
        return keys, [pending.pop(k) for k in keys]

    def put(keys, got):
        recv.update(zip(keys, got))

    for s_idx in reversed(range(6)):
        wname = SUBLAYERS[s_idx]
        l, j = divmod(s_idx, 3)
        tag = f"l{l}s{j}"
        sv = saved[s_idx]
        if j != 1:
            keys, comm = take("ffn_bwd_" + tag)
            xin, fpre, jac, s, vec = sv
            (dcur, dgu, hb, dfb, acc), got = _ffn_bwd(dcur, xin, fpre, jac, vec, *wsets[wname], 0.5, "ffn_bwd_" + tag, comm=comm)
            put(keys, got)
            keys, comm = take("wgrad_w13_" + tag)
            dw13 = _wgrad(hb, dgu, 2, D, D_FF, "share", "cols", D, BF, "wgrad_w13_" + tag, tk=TK_WGRAD, comm=comm,
                          slots=(_w13_slots, (4, D, FF_PAD)))
            if keys:
                dw13, got = dw13
                put(keys, got)
            pending[wname + ".0"] = dw13
            keys, comm = take("wgrad_w2_" + tag)
            dw2 = _wgrad(s, dfb, 1, D_FF, D, "share", "share", D_FF, BF, "wgrad_w2_" + tag, tk=TK_WGRAD, comm=comm,
                         slots=(_w2_slots, (N_DEV, W2_SHARD, D)))
            if keys:
                dw2, got = dw2
                put(keys, got)
            pending[wname + ".1"] = dw2
        elif l == 0:
            xin, fpre, z, hs, cvs, vec = sv
            w_in, w_out = wsets[wname]
            keys, comm = take("mixab_bwd_" + tag)
            (dcur, dz, yab, hb, dfb, xcb, dri, acc, accs_ab, dw31), got = _mixab_bwd(
                dcur, xin, fpre, z, cvs, hs, vec, w_in, *ab_c, w_out.reshape(D, D), "mixab_bwd_" + tag, comm=comm)
            put(keys, got)
            d_in = _wgrad(hb, dz, 1, D, 2 * D, "share", "share", D, BF, "wgrad_ab_in", tk=TK_WGRAD, col_slots=N_DEV)
            d_out = _wgrad(yab, dfb, 1, D, D, "share", "share", D, BF, "wgrad_ab_out")
            pending[wname + ".0"], pending[wname + ".1"] = d_in, d_out.reshape(N_DEV, D // N_DEV, D)
            g["gate"] = _wgrad(xcb, dri, 1, W_A, 2 * W_A, "share", "share", W_A, F32, "wgrad_gate")
            g["accs_ab"] = accs_ab
            g["dw31"] = dw31
        else:
            xin, fpre, c_saved, c_rstd, pb, vec = sv
            w_in, w_out = wsets[wname]
            dcur, dpre, hb, dfb, acc, dbin, dws, dbst = _mixc_bwd(
                dcur, xin, fpre, c_saved, c_rstd, vec, w_in, c_cvec, p["c_w_s"], c_bst, w_out.reshape(D, D), "mixc_bwd_" + tag)
            d_in = _wgrad(hb, dpre, 1, D, 2 * D, "share", "share", D, BF, "wgrad_c_in", tk=TK_WGRAD, col_slots=N_DEV)
            d_out = _wgrad(pb, dfb, 1, D, D, "share", "share", D, BF, "wgrad_c_out")
            pending[wname + ".0"], pending[wname + ".1"] = d_in, d_out.reshape(N_DEV, D // N_DEV, D)
            g["c_small"] = (acc, dbin, dws, dbst)
        accs[f"{l}{j}"] = acc
    g["accs"] = accs
    g["pending"] = pending
    g["recv"] = recv
    return loss_blk, dcur, g


def _exchange_ops(ins, outs, sems, mode, action):
    send_sems, recv_sems, loc_sems = sems
    n = len(ins)
    x, y, c = lax.axis_index("x"), lax.axis_index("y"), lax.axis_index("c")
    me = 4 * x + 2 * y + c

    def src(i, dev):
        return ins[i] if mode == "gather" else ins[i].at[dev]

    for i in range(n):
        cp = pltpu.make_async_copy(src(i, me), outs[i].at[me], loc_sems.at[i])
        if action == "start":
            cp.start()
        else:
            cp.wait()
    for mask in range(1, N_DEV):
        px = 1 - x if mask & 4 else x
        py = 1 - y if mask & 2 else y
        pc = 1 - c if mask & 1 else c
        peer = 4 * px + 2 * py + pc
        for i in range(n):
            k = i * (N_DEV - 1) + mask - 1
            cp = pltpu.make_async_remote_copy(
                src_ref=src(i, peer), dst_ref=outs[i].at[me if action == "start" else peer],
                send_sem=send_sems.at[k], recv_sem=recv_sems.at[k],
                device_id=(px, py, pc), device_id_type=pl.DeviceIdType.MESH)
            if action == "start":
                cp.start()
            else:
                cp.wait()


def _exchange_scratch(n):
    return [pltpu.SemaphoreType.DMA((n * (N_DEV - 1),)), pltpu.SemaphoreType.DMA((n * (N_DEV - 1),)),
            pltpu.SemaphoreType.DMA((n,))]


def _exchange_shapes(arrays, mode):
    return [jax.ShapeDtypeStruct(((N_DEV,) + a.shape) if mode == "gather" else a.shape, a.dtype) for a in arrays]


def _exchange(arrays, mode, name):
    n = len(arrays)

    def body(*refs):
        ins, outs, sems = refs[:n], refs[n:2 * n], refs[2 * n:]
        _exchange_ops(ins, outs, sems, mode, "start")
        _exchange_ops(ins, outs, sems, mode, "wait")

    return pl.pallas_call(
        body, in_specs=[pl.BlockSpec(memory_space=pl.ANY)] * n, out_specs=[pl.BlockSpec(memory_space=pl.ANY)] * n,
        out_shape=_exchange_shapes(arrays, mode), scratch_shapes=_exchange_scratch(n), name=name,
    )(*arrays)


def _gather_two_level(arrays, name):
    n = len(arrays)
    per = N_DEV - 1

    def body(*refs):
        ins, outs = refs[:n], refs[n:2 * n]
        send_sems, recv_sems, loc_sems = refs[2 * n:]
        x, y, c = lax.axis_index("x"), lax.axis_index("y"), lax.axis_index("c")
        me, sibling = (x, y, c), (x, y, 1 - c)
        chips = [(1 - x, y), (x, 1 - y), (1 - x, 1 - y)]

        def rows(i, dev):
            return outs[i].at[4 * dev[0] + 2 * dev[1] + dev[2]]

        def copy(i, k, block, to, src=None):
            return pltpu.make_async_remote_copy(
                src_ref=rows(i, block) if src is None else src, dst_ref=rows(i, block),
                send_sem=send_sems.at[i * per + k], recv_sem=recv_sems.at[i * per + k],
                device_id=to, device_id_type=pl.DeviceIdType.MESH)

        mine = [pltpu.make_async_copy(ins[i], rows(i, me), loc_sems.at[i]) for i in range(n)]
        for cp in mine:
            cp.start()
        first = []
        for i in range(n):
            first.append(copy(i, 0, me, sibling, src=ins[i]))
            first += [copy(i, 1 + j, me, (*chip, c), src=ins[i]) for j, chip in enumerate(chips)]
        for cp in first:
            cp.start()
        passed = []
        for j, chip in enumerate(chips):
            for i in range(n):
                copy(i, 1 + j, (*chip, c), me).wait_recv()
                fwd = copy(i, 4 + j, (*chip, c), sibling)
                fwd.start()
                passed.append(fwd)
        for i in range(n):
            copy(i, 0, sibling, me).wait_recv()
            for j, chip in enumerate(chips):
                copy(i, 4 + j, (*chip, 1 - c), me).wait_recv()
        for cp in first + passed:
            cp.wait_send()
        for cp in mine:
            cp.wait()

    return pl.pallas_call(
        body, in_specs=[pl.BlockSpec(memory_space=pl.ANY)] * n, out_specs=[pl.BlockSpec(memory_space=pl.ANY)] * n,
        out_shape=_exchange_shapes(arrays, "gather"), scratch_shapes=_exchange_scratch(n), name=name,
    )(*arrays)


def _sum_slots(a, name):
    def body(a_ref, o_ref):
        acc = a_ref[0]
        for s in range(1, N_DEV):
            acc = acc + a_ref[s]
        o_ref[...] = acc

    return pl.pallas_call(body, out_shape=jax.ShapeDtypeStruct(a.shape[1:], F32), name=name,
                          compiler_params=pltpu.CompilerParams(vmem_limit_bytes=VMEM_LIMIT))(a)


def _pack(pieces, mult):
    flat = jnp.concatenate([q.reshape(-1).astype(F32) for q in pieces])
    size = -(-flat.shape[0] // mult) * mult
    return jnp.pad(flat, (0, size - flat.shape[0])).reshape(size // 128, 128)


def _unpack(flat, shapes):
    out, off = [], 0
    for shp in shapes:
        size = math.prod(shp)
        out.append(flat[..., off:off + size].reshape(flat.shape[:-1] + tuple(shp)))
        off += size
    return out


def _mod_part(c_all, ada_w, ada_b_mine, name):
    cols = ada_w.shape[-1]

    def body(c_ref, w_ref, b_ref, o_ref):
        cv = c_ref[...]
        ca = cv * _sigmoid(cv)
        for l in range(2):
            o_ref[l] = jnp.dot(ca, w_ref[l], preferred_element_type=F32, precision=lax.Precision.HIGHEST) + b_ref[l:l + 1, :]

    return pl.pallas_call(body, out_shape=jax.ShapeDtypeStruct((2, N_DEV, cols), F32), name=name,
                          compiler_params=pltpu.CompilerParams(vmem_limit_bytes=VMEM_LIMIT))(c_all, ada_w, ada_b_mine)


def _ada_w_grad(c_all_t, dmod_mine, name):
    cols = dmod_mine.shape[-1]

    def body(ct_ref, d_ref, o_ref):
        cv = ct_ref[...]
        ca = cv * _sigmoid(cv)
        for l in range(2):
            acc = ca[:, 0:1] * d_ref[l, 0:1, :]
            for b in range(1, N_DEV):
                acc = acc + ca[:, b:b + 1] * d_ref[l, b:b + 1, :]
            o_ref[l] = acc

    return pl.pallas_call(body, out_shape=jax.ShapeDtypeStruct((2, D, cols), F32), name=name,
                          compiler_params=pltpu.CompilerParams(vmem_limit_bytes=VMEM_LIMIT))(c_all_t, dmod_mine)


def _adamw_math(w, g, m, v):
    m2 = ADAM_B1 * m + (1.0 - ADAM_B1) * g
    v2 = ADAM_B2 * v + (1.0 - ADAM_B2) * (g * g)
    m_hat = m2 / (1.0 - ADAM_B1 ** ADAM_STEP)
    v_hat = v2 / (1.0 - ADAM_B2 ** ADAM_STEP)
    delta = -ADAM_LR * (m_hat / (jnp.sqrt(v_hat) + ADAM_EPS) + ADAM_WD * w)
    return delta, m2, v2


def _adamw_big(w, g, m, v, name):
    n_l, rows, cols = w.shape
    parts = list(g) if isinstance(g, (list, tuple)) else None
    sizes = (512, 352, 256, 128, 64, 32, 16, 8) if parts is None or len(parts) == 1 else (176, 128, 64, 32, 16, 8)
    br = next(b for b in sizes if rows % b == 0)
    nr = rows // br

    def body(*refs):
        w_ref, g_refs, (m_ref, v_ref, go_ref, d_ref, mo_ref, vo_ref) = refs[0], refs[1:-6], refs[-6:]

        def update(gsum):
            go_ref[...] = gsum
            d_ref[...], mo_ref[...], vo_ref[...] = _adamw_math(w_ref[...], gsum, m_ref[...], v_ref[...])

        if parts is None:
            update(g_refs[0][...])
        else:
            for l, g_ref in enumerate(g_refs):
                @pl.when(pl.program_id(0) == l)
                def _(g_ref=g_ref):
                    gsum = g_ref[0, :, 0:cols].astype(F32)
                    for s in range(1, N_DEV):
                        gsum = gsum + g_ref[s, :, 0:cols].astype(F32)
                    update(gsum)

    blk = pl.BlockSpec((None, br, cols), lambda l, i: (l, i, 0))
    if parts is None:
        g_specs, g_args = [blk], [g]
    else:
        def part_spec(l_mine, width):
            return pl.BlockSpec((N_DEV, br, width),
                                lambda l, i: (0, jnp.where(l < l_mine, 0, jnp.where(l == l_mine, i, nr - 1)), 0))
        g_specs, g_args = [part_spec(l, p.shape[-1]) for l, p in enumerate(parts)], parts
    shp = jax.ShapeDtypeStruct((n_l, rows, cols), F32)
    return pl.pallas_call(
        body, grid=(n_l, nr), in_specs=[blk] + g_specs + [blk, blk], out_specs=[blk] * 4, out_shape=[shp] * 4,
        compiler_params=pltpu.CompilerParams(dimension_semantics=("arbitrary", "arbitrary"), vmem_limit_bytes=VMEM_LIMIT),
        name=name)(w, *g_args, m, v)


def _adamw_small(ws, gs, ms, vs, name):
    n = len(ws)

    def body(*refs):
        for i in range(n):
            w_ref, g_ref, m_ref, v_ref = (refs[k * n + i] for k in range(4))
            d_ref, mo_ref, vo_ref = (refs[(4 + k) * n + i] for k in range(3))
            d_ref[...], mo_ref[...], vo_ref[...] = _adamw_math(w_ref[...], g_ref[...], m_ref[...], v_ref[...])

    shapes = [jax.ShapeDtypeStruct(w.shape, F32) for w in ws]
    outs = pl.pallas_call(body, out_shape=shapes * 3, name=name,
                          compiler_params=pltpu.CompilerParams(vmem_limit_bytes=VMEM_LIMIT))(*ws, *gs, *ms, *vs)
    return outs[:n], outs[n:2 * n], outs[2 * n:]


def _as2d(a):
    return a.reshape(-1, a.shape[-1])


def kernel(x, c, ada_w, ada_b, norm_pre, norm_post, ffn_w13, ffn_w2, ab_w_in, a_conv_w, a_conv_b, a_gate_w, a_gate_b, a_lam, b_conv_w, b_conv_b, b_norm_g, b_norm_b, ab_w_out, c_w_in, c_b_in, c_norm_g, c_norm_b, c_w_s, c_b_s, c_w_out, loss_target, m_ada_w, m_ada_b, m_norm_pre, m_norm_post, m_ffn_w13, m_ffn_w2, m_ab_w_in, m_a_conv_w, m_a_conv_b, m_a_gate_w, m_a_gate_b, m_a_lam, m_b_conv_w, m_b_conv_b, m_b_norm_g, m_b_norm_b, m_ab_w_out, m_c_w_in, m_c_b_in, m_c_norm_g, m_c_norm_b, m_c_w_s, m_c_b_s, m_c_w_out, v_ada_w, v_ada_b, v_norm_pre, v_norm_post, v_ffn_w13, v_ffn_w2, v_ab_w_in, v_a_conv_w, v_a_conv_b, v_a_gate_w, v_a_gate_b, v_a_lam, v_b_conv_w, v_b_conv_b, v_b_norm_g, v_b_norm_b, v_ab_w_out, v_c_w_in, v_c_b_in, v_c_norm_g, v_c_norm_b, v_c_w_s, v_c_b_s, v_c_w_out):
    me = 4 * lax.axis_index("x") + 2 * lax.axis_index("y") + lax.axis_index("c")
    weights = dict(ada_w=ada_w, ada_b=ada_b, norm_pre=norm_pre, norm_post=norm_post, ffn_w13=ffn_w13, ffn_w2=ffn_w2,
                   ab_w_in=ab_w_in, a_conv_w=a_conv_w, a_conv_b=a_conv_b, a_gate_w=a_gate_w, a_gate_b=a_gate_b, a_lam=a_lam,
                   b_conv_w=b_conv_w, b_conv_b=b_conv_b, b_norm_g=b_norm_g, b_norm_b=b_norm_b, ab_w_out=ab_w_out,
                   c_w_in=c_w_in, c_b_in=c_b_in, c_norm_g=c_norm_g, c_norm_b=c_norm_b, c_w_s=c_w_s, c_b_s=c_b_s, c_w_out=c_w_out)
    moms = dict(ada_w=m_ada_w, ada_b=m_ada_b, norm_pre=m_norm_pre, norm_post=m_norm_post, ffn_w13=m_ffn_w13, ffn_w2=m_ffn_w2,
                ab_w_in=m_ab_w_in, a_conv_w=m_a_conv_w, a_conv_b=m_a_conv_b, a_gate_w=m_a_gate_w, a_gate_b=m_a_gate_b,
                a_lam=m_a_lam, b_conv_w=m_b_conv_w, b_conv_b=m_b_conv_b, b_norm_g=m_b_norm_g, b_norm_b=m_b_norm_b,
                ab_w_out=m_ab_w_out, c_w_in=m_c_w_in, c_b_in=m_c_b_in, c_norm_g=m_c_norm_g, c_norm_b=m_c_norm_b,
                c_w_s=m_c_w_s, c_b_s=m_c_b_s, c_w_out=m_c_w_out)
    vars_ = dict(ada_w=v_ada_w, ada_b=v_ada_b, norm_pre=v_norm_pre, norm_post=v_norm_post, ffn_w13=v_ffn_w13, ffn_w2=v_ffn_w2,
                 ab_w_in=v_ab_w_in, a_conv_w=v_a_conv_w, a_conv_b=v_a_conv_b, a_gate_w=v_a_gate_w, a_gate_b=v_a_gate_b,
                 a_lam=v_a_lam, b_conv_w=v_b_conv_w, b_conv_b=v_b_conv_b, b_norm_g=v_b_norm_g, b_norm_b=v_b_norm_b,
                 ab_w_out=v_ab_w_out, c_w_in=v_c_w_in, c_b_in=v_c_b_in, c_norm_g=v_c_norm_g, c_norm_b=v_c_norm_b,
                 c_w_s=v_c_w_s, c_b_s=v_c_b_s, c_w_out=v_c_w_out)
    names = list(weights)

    w13s = ffn_w13.astype(BF).reshape(4, D, FF_SHARD)
    tail, blank = w13s[..., FF_MAIN:], jnp.zeros((4, D, FF_TAIL), BF)
    tail_tile = jnp.where(me % 2 == 1, jnp.concatenate([blank, tail], -1), jnp.concatenate([tail, blank], -1))
    w13b = jnp.concatenate([w13s[..., :FF_MAIN], tail_tile], -1)
    small_shapes = [(D,), (2, 3, 128), (2, 3, 128), (CONV_A, 64), (CONV_B, 64), (256,), (128,), (128,)]
    small = _pack([c, norm_pre, norm_post, a_conv_w, b_conv_w, c_b_in, c_norm_g, c_norm_b], 1024)
    w2b = ffn_w2.astype(BF).reshape(4, W2_SHARD, D)
    shards = {f"f{f}": [w13b[f], w2b[f]] for f in range(4)}
    shards["ab"] = [ab_w_in[0].astype(BF), ab_w_out[0].astype(BF)]
    shards["c"] = [c_w_in[0].astype(BF), c_w_out[0].astype(BF)]
    w13g0, w2g0, small_g = _gather_two_level(shards["f0"] + [small], "gather_first")
    plan = dict(shards=shards, gather={0: ["ab", "f1"], 2: ["f2"], 3: ["c", "f3"]},
                scatter={"ffn_bwd_l1s0": ["f3.0", "f3.1", "c.0", "c.1"], "ffn_bwd_l0s2": ["f2.0", "f2.1"],
                         "mixab_bwd_l0s1": ["f1.0", "f1.1"], "wgrad_w13_l0s0": ["ab.0", "ab.1"], "wgrad_w2_l0s0": ["f0.0"]})
    c_all, npre_g, npost_g, acw_g, bcw_g, cbin_g, cng_g, cnb_g = _unpack(small_g.reshape(N_DEV, -1), small_shapes)

    def cat_last(a):
        return jnp.moveaxis(a, 0, -2).reshape(a.shape[1:-1] + (N_DEV * a.shape[-1],))

    ada_b_mine = lax.dynamic_slice_in_dim(ada_b, me * ada_w.shape[-1], ada_w.shape[-1], axis=1)
    (mod_g,) = _exchange([_mod_part(c_all, ada_w, ada_b_mine, "mod_part")], "gather", "gather_mod")
    mod = cat_last(lax.dynamic_index_in_dim(mod_g, me, axis=2, keepdims=False)).reshape(2, 3, 3, D)

    p = dict(mod=mod, norm_pre=cat_last(npre_g), norm_post=cat_last(npost_g), wsets={"f0": [w13g0, w2g0]},
             a_conv_w=cat_last(acw_g), a_conv_b=a_conv_b[0], a_gate_w=a_gate_w[0], a_gate_b=a_gate_b[0], a_lam=a_lam[0],
             b_conv_w=cat_last(bcw_g), b_conv_b=b_conv_b[0], b_norm_g=b_norm_g[0], b_norm_b=b_norm_b[0],
             c_b_in=cat_last(cbin_g), c_norm_g=cat_last(cng_g), c_norm_b=cat_last(cnb_g), c_w_s=c_w_s[0], c_b_s=c_b_s[0])

    loss_blk, grad_x, g = _local_step(x[0], loss_target[0], p, plan)
    loss = lax.psum(loss_blk[0, 0], ("x", "y", "c"))

    accs = g["accs"]
    dmod = jnp.stack([jnp.stack([accs[f"{l}{j}"][0:3] for j in range(3)]) for l in range(2)])
    dnpre = jnp.stack([jnp.stack([accs[f"{l}{j}"][3] for j in range(3)]) for l in range(2)])
    dnpost = jnp.stack([jnp.stack([accs[f"{l}{j}"][4] for j in range(3)]) for l in range(2)])
    sab = g["accs_ab"]
    half = W_A // 8
    dgate = g["gate"][0]
    dgw = jnp.stack([jnp.concatenate([dgate[half * hh:half * (hh + 1), half * hh:half * (hh + 1)],
                                      dgate[half * hh:half * (hh + 1), W_A + half * hh:W_A + half * (hh + 1)]], axis=1)
                     for hh in range(8)])
    dgb = jnp.concatenate([sab[5].reshape(8, half), sab[6].reshape(8, half)], axis=1)
    c_acc, c_dbin, c_dws, c_dbst = g["c_small"]
    red_shapes = [(2, 9216), (2, 3, D), (2, 3, D), (CONV_A, W_A), (W_A,), (8, half, 2 * half), (8, 2 * half), (W_A,),
                  (CONV_B, W_B), (W_B,), (W_B,), (W_B,), (2 * D,), (D,), (D,), (H_C, CHUNK, CHUNK), (H_C, CHUNK)]
    red = _pack([dmod.reshape(2, 9216), dnpre, dnpost, sab[0:4], sab[4], dgw, dgb, sab[7], g["dw31"][0:CONV_B], sab[8],
                 sab[9], sab[10], c_dbin[0], c_acc[5], c_acc[6], c_dws, jnp.transpose(c_dbst)], N_DEV * 1024)
    left = sorted(g["pending"])
    red_r, *last_recv = _exchange([red.reshape(N_DEV, -1, 128)] + [g["pending"][k] for k in left], "scatter",
                                  "scatter_small_grads")
    red_all, dmod_all = _exchange([_sum_slots(red_r, "sum_small_grads"), dmod.reshape(-1, 128)], "gather", "gather_small_grads")
    red_sum = red_all.reshape(-1)
    (g_ada_b, g_npre, g_npost, g_acw, g_acb, g_agw, g_agb, g_alam, g_bcw, g_bcb, g_bng, g_bnb, g_cbin, g_cng, g_cnb,
     g_cws, g_cbs) = _unpack(red_sum, red_shapes)
    dmod_all = dmod_all.reshape(N_DEV, 2, 9216)
    ncol = ada_w.shape[-1]
    dmod_mine = jnp.moveaxis(lax.dynamic_slice_in_dim(dmod_all, me * ncol, ncol, axis=2), 0, 1)
    g_ada_w = _ada_w_grad(jnp.transpose(c_all), dmod_mine, "ada_w_grad")

    def mine(a, width):
        return lax.dynamic_slice_in_dim(a, me * width, width, axis=a.ndim - 1)

    small_grads = dict(
        ada_b=g_ada_b, norm_pre=mine(g_npre, 128), norm_post=mine(g_npost, 128), a_conv_w=mine(g_acw, 64)[None],
        a_conv_b=g_acb[None], a_gate_w=g_agw[None], a_gate_b=g_agb[None], a_lam=g_alam[None], b_conv_w=mine(g_bcw, 64)[None],
        b_conv_b=g_bcb[None], b_norm_g=g_bng[None], b_norm_b=g_bnb[None], c_b_in=mine(g_cbin, 256)[None],
        c_norm_g=mine(g_cng, 128)[None], c_norm_b=mine(g_cnb, 128)[None], c_w_s=g_cws[None], c_b_s=g_cbs[None])

    recv = dict(g["recv"])
    recv.update(zip(left, last_recv))
    big_partials = dict(ffn_w13=[recv[f"f{f}.0"] for f in range(4)], ffn_w2=[recv[f"f{f}.1"] for f in range(4)],
                        ab_w_in=[recv["ab.0"]], ab_w_out=[recv["ab.1"]], c_w_in=[recv["c.0"]], c_w_out=[recv["c.1"]],
                        ada_w=g_ada_w)

    grads, deltas, new_m, new_v = {}, {}, {}, {}

    def as3d(a):
        return a.reshape((-1,) + a.shape[-2:])

    for nm, gp in big_partials.items():
        shp = weights[nm].shape
        go, dl, mo, vo = _adamw_big(as3d(weights[nm]), gp, as3d(moms[nm]), as3d(vars_[nm]), "adamw_" + nm)
        grads[nm], deltas[nm], new_m[nm], new_v[nm] = (a.reshape(shp) for a in (go, dl, mo, vo))
    snames = list(small_grads)
    dls, mos, vos = _adamw_small([_as2d(weights[nm]) for nm in snames], [_as2d(small_grads[nm]) for nm in snames],
                                 [_as2d(moms[nm]) for nm in snames], [_as2d(vars_[nm]) for nm in snames], "adamw_small")
    for k, nm in enumerate(snames):
        shp = weights[nm].shape
        grads[nm] = small_grads[nm].reshape(shp)
        deltas[nm], new_m[nm], new_v[nm] = dls[k].reshape(shp), mos[k].reshape(shp), vos[k].reshape(shp)

    return (loss, grad_x[None], *[grads[nm] for nm in names], *[deltas[nm] for nm in names],
            *[new_m[nm] for nm in names], *[new_v[nm] for nm in names])
```

```python
import math

import jax
import jax.numpy as jnp
from jax import lax
from jax.experimental import pallas as pl
from jax.experimental.pallas import tpu as pltpu

F32 = jnp.float32
BF = jnp.bfloat16

N_DEV = 8
D = 1024
EPS = 1e-6
D_FF = 2816
FF_SHARD = 704
FF_PAD = 768
FF_MAIN = 640
FF_TAIL = FF_SHARD - FF_MAIN
W2_SHARD = 352
W_A = 512
W_B = 512
AB_SAVED = 9 * W_A
CONV_A = 4
CONV_B = 31
HALO_A = 8
HALO_B = 32
LRU_C = 8.0
CHUNK = 128
H_C = 8
ADAM_LR = 0.001
ADAM_B1 = 0.9
ADAM_B2 = 0.999
ADAM_EPS = 1e-08
ADAM_WD = 0.01
ADAM_STEP = 10
VMEM_LIMIT = 62 * 1024 * 1024
GELU_C = math.sqrt(2.0 / math.pi)

TM_FFN = 512
TM_FFN_BWD = 256
TM_MIX = 256
TM_MIXC_FWD = 512
TK_WGRAD = 2048


def _params(limit=VMEM_LIMIT):
    return pltpu.CompilerParams(dimension_semantics=("arbitrary",), vmem_limit_bytes=limit)


def _dot(a, b):
    return jnp.dot(a, b, preferred_element_type=F32)


def _dot_nt(a, b):
    return lax.dot_general(a, b, (((1,), (1,)), ((), ())), preferred_element_type=F32)


def _dot_tn(a, b):
    return lax.dot_general(a, b, (((0,), (0,)), ((), ())), preferred_element_type=F32)


def _sigmoid(x):
    return 0.5 + 0.5 * jnp.tanh(0.5 * x)


def _gelu(x):
    t = jnp.tanh(GELU_C * (x + 0.044715 * x * x * x))
    return 0.5 * x * (1.0 + t), t


def _gelu_grad(x, t):
    return 0.5 * (1.0 + t) + 0.5 * x * (1.0 - t * t) * GELU_C * (1.0 + 3.0 * 0.044715 * x * x)


def _rms(x):
    r = lax.rsqrt(jnp.mean(x * x, axis=-1, keepdims=True) + EPS)
    return x * r, r


def _colsum(x):
    return jnp.sum(x, axis=0, keepdims=True)


def _pre_fwd(x, vec_ref):
    xn, r = _rms(x)
    n = xn * vec_ref[3:4, :]
    h = n * (1.0 + vec_ref[1:2, :]) + vec_ref[0:1, :]
    return h, xn, r, n


def _post_fwd(x, f, vec_ref, res_w):
    fn, _ = _rms(f)
    return x + fn * ((res_w * (1.0 + vec_ref[2:3, :])) * vec_ref[4:5, :])


def _post_bwd(dout, f, vec_ref, acc_ref, res_w):
    fn, r2 = _rms(f)
    acc_ref[2:3, :] += _colsum(dout * fn)
    dfn = dout * ((res_w * (1.0 + vec_ref[2:3, :])) * vec_ref[4:5, :])
    return r2 * (dfn - fn * jnp.mean(dfn * fn, axis=-1, keepdims=True))


def _pre_bwd(dout, dh, xn, r, vec_ref, acc_ref):
    acc_ref[0:1, :] += _colsum(dh)
    acc_ref[1:2, :] += _colsum(dh * xn)
    dxn = dh * ((1.0 + vec_ref[1:2, :]) * vec_ref[3:4, :])
    return dout + r * (dxn - xn * jnp.mean(dxn * xn, axis=-1, keepdims=True))


def _finish_acc(acc_ref, vec_ref, res_w):
    s_pre, s_post = acc_ref[1:2, :], acc_ref[2:3, :]
    acc_ref[1:2, :] = vec_ref[3:4, :] * s_pre
    acc_ref[3:4, :] = (1.0 + vec_ref[1:2, :]) * s_pre
    acc_ref[2:3, :] = (res_w * vec_ref[4:5, :]) * s_post
    acc_ref[4:5, :] = (res_w * (1.0 + vec_ref[2:3, :])) * s_post


def _tile(tm, ncol):
    return pl.BlockSpec((tm, ncol), lambda i: (i, 0))


def _full(shape):
    return pl.BlockSpec(shape, lambda i: (0,) * len(shape))


def _any():
    return pl.BlockSpec(memory_space=pl.ANY)


def _load_ffn_weights(w13_hbm, w2_hbm, w13_v, w2_v, tails, sems):
    copies = []
    for j in range(N_DEV):
        half, k = divmod(j, 4)
        copies.append((w13_hbm.at[j, :, pl.ds(0, FF_MAIN)], w13_v.at[:, pl.ds(D_FF * half + FF_MAIN * k, FF_MAIN)]))
        copies.append((w13_hbm.at[j, :, pl.ds(FF_MAIN, 128)], tails.at[j]))
    for k in range(4):
        copies.append((w2_hbm.at[2 * k], w2_v.at[pl.ds(FF_MAIN * k, W2_SHARD), :]))
        copies.append((w2_hbm.at[2 * k + 1, pl.ds(0, FF_MAIN - W2_SHARD), :],
                       w2_v.at[pl.ds(FF_MAIN * k + W2_SHARD, FF_MAIN - W2_SHARD), :]))
        copies.append((w2_hbm.at[2 * k + 1, pl.ds(FF_MAIN - W2_SHARD, FF_TAIL), :],
                       w2_v.at[pl.ds(4 * FF_MAIN + FF_TAIL * k, FF_TAIL), :]))
    copies = [pltpu.make_async_copy(src, dst, sems.at[n]) for n, (src, dst) in enumerate(copies)]
    for cp in copies:
        cp.start()
    for cp in copies:
        cp.wait()
    for pair in range(4):
        half, kk = divmod(pair, 2)
        base = D_FF * half + 4 * FF_MAIN + 128 * kk
        w13_v[:, base:base + 128] = tails[2 * pair] + tails[2 * pair + 1]


_FFN_SCRATCH = [pltpu.VMEM((D, 2 * D_FF), BF), pltpu.VMEM((D_FF, D), BF), pltpu.VMEM((N_DEV, D, 128), BF),
                pltpu.SemaphoreType.DMA((2 * N_DEV + 12,))]
HID_CHUNKS = ((0, 768), (768, 768), (1536, 768), (2304, 512))


def _hosted(body, n_in, n_out, n_scratch, n_comm, mode, grid):
    if not n_comm:
        return body

    def at(corner):
        hit = pl.program_id(0) == corner[0]
        for d in range(1, len(grid)):
            hit = hit & (pl.program_id(d) == corner[d])
        return hit

    def hosted(*refs):
        ins, cin = refs[:n_in], refs[n_in:n_in + n_comm]
        outs, cout = refs[n_in + n_comm:n_in + n_comm + n_out], refs[n_in + n_comm + n_out:n_in + 2 * n_comm + n_out]
        scratch = refs[n_in + 2 * n_comm + n_out:]
        own, sems = scratch[:n_scratch], scratch[n_scratch:]

        @pl.when(at([0] * len(grid)))
        def _():
            _exchange_ops(cin, cout, sems, mode, "start")

        body(*ins, *outs, *own)

        @pl.when(at([n - 1 for n in grid]))
        def _():
            _exchange_ops(cin, cout, sems, mode, "wait")

    return hosted


def _ffn_fwd(x, vec, w13g, w2g, res_w, name, comm=(), tgt=None):
    t_len = x.shape[0]
    tm = min(TM_FFN, t_len)
    nc = len(comm)
    head = tgt is not None

    def body(*refs):
        if head:
            x_ref, vec_ref, w13_hbm, w2_hbm, t_ref, xo_ref, f_ref, jac_ref, s_ref, loss_ref, w13_v, w2_v, tails, sems = refs
        else:
            x_ref, vec_ref, w13_hbm, w2_hbm, xo_ref, f_ref, jac_ref, s_ref, w13_v, w2_v, tails, sems = refs

        @pl.when(pl.program_id(0) == 0)
        def _():
            _load_ffn_weights(w13_hbm, w2_hbm, w13_v, w2_v, tails, sems)
            if head:
                loss_ref[...] = jnp.zeros((8, 128), F32)

        x_t = x_ref[...]
        h, _, _, _ = _pre_fwd(x_t, vec_ref)
        hb = h.astype(BF)
        for c0, cw in HID_CHUNKS:
            g = _dot(hb, w13_v[:, c0:c0 + cw])
            u = _dot(hb, w13_v[:, D_FF + c0:D_FF + c0 + cw])
            sig = _sigmoid(g)
            sl = g * sig
            jac_ref[0, :, c0:c0 + cw] = (u * (sig + sl * (1.0 - sig))).astype(BF)
            jac_ref[1, :, c0:c0 + cw] = sl.astype(BF)
            s_ref[:, c0:c0 + cw] = (sl * u).astype(BF)
        acc = _dot(s_ref[...], w2_v[...])
        f_ref[...] = acc
        xo = _post_fwd(x_t, acc, vec_ref, res_w)
        if head:
            err = xo - t_ref[...]
            xo_ref[...] = err * (1.0 / D)
            loss_ref[...] += jnp.sum(err * err) * (0.5 / D)
        else:
            xo_ref[...] = xo

    nt = t_len // tm
    n_in, n_out = (5, 5) if head else (4, 4)
    outs = pl.pallas_call(
        _hosted(body, n_in, n_out, 4, nc, "gather", (nt,)), grid=(nt,),
        in_specs=[_tile(tm, D), _full((8, D)), _any(), _any()] + ([_tile(tm, D)] if head else []) + [_any()] * nc,
        out_specs=[_tile(tm, D), _tile(tm, D), pl.BlockSpec((2, tm, D_FF), lambda i: (0, i, 0)), _tile(tm, D_FF)]
        + ([_full((8, 128))] if head else []) + [_any()] * nc,
        out_shape=[jax.ShapeDtypeStruct((t_len, D), F32), jax.ShapeDtypeStruct((t_len, D), F32),
                   jax.ShapeDtypeStruct((2, t_len, D_FF), BF), jax.ShapeDtypeStruct((t_len, D_FF), BF)]
        + ([jax.ShapeDtypeStruct((8, 128), F32)] if head else []) + _exchange_shapes(comm, "gather"),
        scratch_shapes=_FFN_SCRATCH + (_exchange_scratch(nc) if nc else []), compiler_params=_params(), name=name,
    )(x, vec, w13g, w2g, *([tgt] if head else []), *comm)
    return outs[:n_out], outs[n_out:]


def _ffn_bwd(dout, x, fpre, jac, vec, w13g, w2g, res_w, name, comm=()):
    t_len = x.shape[0]
    tm = min(TM_FFN_BWD, t_len)
    nt = t_len // tm
    nc = len(comm)

    def body(dout_ref, x_ref, f_ref, jac_ref, vec_ref, w13_hbm, w2_hbm,
             dx_ref, dgu_ref, hb_ref, dfb_ref, acc_ref, w13_v, w2_v, tails, sems):
        @pl.when(pl.program_id(0) == 0)
        def _():
            _load_ffn_weights(w13_hbm, w2_hbm, w13_v, w2_v, tails, sems)
            acc_ref[...] = jnp.zeros((8, D), F32)

        dout_t = dout_ref[...]
        df = _post_bwd(dout_t, f_ref[...], vec_ref, acc_ref, res_w)
        dfb = df.astype(BF)
        dfb_ref[...] = dfb
        h, xn, r, _ = _pre_fwd(x_ref[...], vec_ref)
        hb_ref[...] = h.astype(BF)
        for c0, cw in HID_CHUNKS:
            ds = _dot_nt(dfb, w2_v[c0:c0 + cw, :]).astype(BF)
            dgu_ref[:, c0:c0 + cw] = ds * jac_ref[0, :, c0:c0 + cw]
            dgu_ref[:, D_FF + c0:D_FF + c0 + cw] = ds * jac_ref[1, :, c0:c0 + cw]
        dh = _dot_nt(dgu_ref[...], w13_v[...])
        dx_ref[...] = _pre_bwd(dout_t, dh, xn, r, vec_ref, acc_ref)

        @pl.when(pl.program_id(0) == nt - 1)
        def _():
            _finish_acc(acc_ref, vec_ref, res_w)

    jac_spec = pl.BlockSpec((2, tm, D_FF), lambda i: (0, i, 0))
    outs = pl.pallas_call(
        _hosted(body, 7, 5, 4, nc, "scatter", (nt,)), grid=(nt,),
        in_specs=[_tile(tm, D), _tile(tm, D), _tile(tm, D), jac_spec, _full((8, D)), _any(), _any()] + [_any()] * nc,
        out_specs=[_tile(tm, D), _tile(tm, 2 * D_FF), _tile(tm, D), _tile(tm, D), _full((8, D))] + [_any()] * nc,
        out_shape=[jax.ShapeDtypeStruct((t_len, D), F32), jax.ShapeDtypeStruct((t_len, 2 * D_FF), BF),
                   jax.ShapeDtypeStruct((t_len, D), BF),
                   jax.ShapeDtypeStruct((t_len, D), BF), jax.ShapeDtypeStruct((8, D), F32)] + _exchange_shapes(comm, "scatter"),
        scratch_shapes=_FFN_SCRATCH + (_exchange_scratch(nc) if nc else []), compiler_params=_params(), name=name,
    )(dout, x, fpre, jac, vec, w13g, w2g, *comm)
    return outs[:5], outs[5:]


def _w13_slots(acc, o_ref):
    for k in range(4):
        o_ref[k, :, 0:FF_MAIN] = acc[:, FF_MAIN * k:FF_MAIN * (k + 1)].astype(BF)
        pair_tile = acc[:, 4 * FF_MAIN + 128 * (k // 2):4 * FF_MAIN + 128 * (k // 2 + 1)]
        o_ref[k, :, FF_MAIN:FF_PAD] = (pair_tile if k % 2 == 0 else pltpu.roll(pair_tile, FF_TAIL, 1)).astype(BF)


def _w2_slots(acc, o_ref):
    rest = FF_MAIN - W2_SHARD
    for k in range(4):
        o_ref[2 * k] = acc[FF_MAIN * k:FF_MAIN * k + W2_SHARD, :].astype(BF)
        o_ref[2 * k + 1, 0:rest, :] = acc[FF_MAIN * k + W2_SHARD:FF_MAIN * (k + 1), :].astype(BF)
        o_ref[2 * k + 1, rest:W2_SHARD, :] = acc[4 * FF_MAIN + FF_TAIL * k:4 * FF_MAIN + FF_TAIL * (k + 1), :].astype(BF)


def _wgrad(a, b, j_count, m, n, a_mode, b_mode, out_rows, out_dtype, name, tk=TK_WGRAD, col_slots=1, comm=(),
           slots=None):
    t_len = a.shape[-2]
    tk = min(tk, t_len)
    nk = t_len // tk
    wn = n // col_slots
    nc = len(comm)

    def spec(mode, width):
        if mode == "stack":
            return pl.BlockSpec((None, tk, width), lambda j, t: (j, t, 0))
        if mode == "cols":
            return pl.BlockSpec((tk, width), lambda j, t: (t, j))
        return pl.BlockSpec((tk, width), lambda j, t: (t, 0))

    def body(a_ref, b_ref, o_ref, acc):
        t = pl.program_id(1)

        @pl.when(t == 0)
        def _():
            acc[...] = jnp.zeros((m, n), F32)

        acc[...] += _dot_tn(a_ref[...], b_ref[...])

        @pl.when(t == nk - 1)
        def _():
            if slots is not None:
                slots[0](acc, o_ref)
            elif col_slots == 1:
                o_ref[...] = acc[0:out_rows, :].astype(out_dtype)
            else:
                for s in range(col_slots):
                    o_ref[s] = acc[0:out_rows, wn * s:wn * (s + 1)].astype(out_dtype)

    if slots is not None:
        blk = slots[1]
        out_spec = pl.BlockSpec(blk, lambda j, t: (j,) + (0,) * (len(blk) - 1))
        out_shape = jax.ShapeDtypeStruct((j_count * blk[0],) + blk[1:], BF)
    elif col_slots == 1:
        out_spec = pl.BlockSpec((None, out_rows, n), lambda j, t: (j, 0, 0))
        out_shape = jax.ShapeDtypeStruct((j_count, out_rows, n), out_dtype)
    else:
        out_spec = pl.BlockSpec((col_slots, out_rows, wn), lambda j, t: (0, 0, 0))
        out_shape = jax.ShapeDtypeStruct((col_slots, out_rows, wn), out_dtype)
    outs = pl.pallas_call(
        _hosted(body, 2, 1, 1, nc, "scatter", (j_count, nk)), grid=(j_count, nk),
        in_specs=[spec(a_mode, m), spec(b_mode, n)] + [_any()] * nc,
        out_specs=[out_spec] + [_any()] * nc, out_shape=[out_shape] + _exchange_shapes(comm, "scatter"),
        scratch_shapes=[pltpu.VMEM((m, n), F32)] + (_exchange_scratch(nc) if nc else []),
        compiler_params=pltpu.CompilerParams(dimension_semantics=("arbitrary", "arbitrary"), vmem_limit_bytes=VMEM_LIMIT),
        name=name,
    )(a, b, *comm)
    return (outs[0], list(outs[1:])) if nc else outs[0]


def _c_mask_weights(ws_ref, wsm, wsmt):
    row = lax.broadcasted_iota(jnp.int32, (CHUNK, CHUNK), 0)
    col = lax.broadcasted_iota(jnp.int32, (CHUNK, CHUNK), 1)
    for hh in range(H_C):
        w = jnp.where(row >= col, ws_ref[hh], 0.0)
        wsm[hh] = w.astype(BF)
        if wsmt is not None:
            wsmt[hh] = w.T.astype(BF)


def _c_inner(pre, cvec_ref, wsm, bst_ref, mix_sc, tm):
    z, t = _gelu(pre)
    u = z[:, 0:D]
    v = z[:, D:2 * D]
    mu = jnp.mean(v, axis=-1, keepdims=True)
    vc = v - mu
    rstd = lax.rsqrt(jnp.mean(vc * vc, axis=-1, keepdims=True) + EPS)
    vhat = vc * rstd
    vnb = (vhat * cvec_ref[0:1, :] + cvec_ref[1:2, :]).astype(BF)
    for nn in range(tm // CHUNK):
        for hh in range(H_C):
            rows = slice(CHUNK * nn, CHUNK * (nn + 1))
            cols = slice(CHUNK * hh, CHUNK * (hh + 1))
            mix_sc[rows, cols] = _dot(wsm[hh], vnb[rows, cols]) + bst_ref[:, hh:hh + 1]
    return u, t, rstd, vhat, vnb


C_SAVED = 6 * D


def _mixc_fwd(x, vec, w_in, b_in, cvec, ws, bst, w_out, name):
    t_len = x.shape[0]
    tm = min(TM_MIXC_FWD, t_len)

    def body(x_ref, vec_ref, win_ref, bin_ref, cvec_ref, ws_ref, bst_ref, wout_ref,
             xo_ref, f_ref, sv_ref, rstd_ref, p_ref, wsm, mix_sc, pre_sc):
        @pl.when(pl.program_id(0) == 0)
        def _():
            _c_mask_weights(ws_ref, wsm, None)

        x_t = x_ref[...]
        h, _, _, _ = _pre_fwd(x_t, vec_ref)
        hb = h.astype(BF)
        for j in range(N_DEV):
            cols = slice(256 * j, 256 * (j + 1))
            pre_sc[:, cols] = _dot(hb, win_ref[j]) + bin_ref[:, cols]
        pre = pre_sc[...]
        u, t, rstd, vhat, vnb = _c_inner(pre, cvec_ref, wsm, bst_ref, mix_sc, tm)
        mix = mix_sc[...]
        pb = (u * mix).astype(BF)
        gg = _gelu_grad(pre, t)
        sv_ref[:, 0:D] = u.astype(BF)
        sv_ref[:, D:2 * D] = mix.astype(BF)
        sv_ref[:, 2 * D:4 * D] = gg.astype(BF)
        sv_ref[:, 4 * D:5 * D] = vhat.astype(BF)
        sv_ref[:, 5 * D:6 * D] = vnb
        rstd_ref[...] = jnp.broadcast_to(rstd, (tm, 128))
        p_ref[...] = pb
        fpre = _dot(pb, wout_ref[...])
        f_ref[...] = fpre
        xo_ref[...] = _post_fwd(x_t, fpre, vec_ref, 1.0)

    return pl.pallas_call(
        body, grid=(t_len // tm,),
        in_specs=[_tile(tm, D), _full((8, D)), _full((N_DEV, D, 256)), _full((1, 2 * D)), _full((8, D)),
                  _full((H_C, CHUNK, CHUNK)), _full((CHUNK, H_C)), _full((D, D))],
        out_specs=[_tile(tm, D), _tile(tm, D), _tile(tm, C_SAVED), _tile(tm, 128), _tile(tm, D)],
        out_shape=[jax.ShapeDtypeStruct((t_len, D), F32), jax.ShapeDtypeStruct((t_len, D), F32),
                   jax.ShapeDtypeStruct((t_len, C_SAVED), BF), jax.ShapeDtypeStruct((t_len, 128), F32),
                   jax.ShapeDtypeStruct((t_len, D), BF)],
        scratch_shapes=[pltpu.VMEM((H_C, CHUNK, CHUNK), BF), pltpu.VMEM((tm, D), F32), pltpu.VMEM((tm, 2 * D), F32)],
        compiler_params=_params(), name=name,
    )(x, vec, w_in, b_in, cvec, ws, bst, w_out)


def _mixc_bwd(dout, x, fpre, saved, rstd_b, vec, w_in, cvec, ws, bst, w_out, name):
    t_len = x.shape[0]
    tm = min(TM_MIX, t_len)
    nt = t_len // tm

    def body(dout_ref, x_ref, f_ref, sv_ref, rstd_ref, vec_ref, win_ref, cvec_ref, ws_ref, bst_ref, wout_ref,
             dx_ref, dpre_ref, hb_ref, dfb_ref, acc_ref, dbin_ref, dws_ref, dbst_ref,
             wsm, wsmt, dvn_sc, dmsum, win_v):
        i = pl.program_id(0)

        @pl.when(i == 0)
        def _():
            _c_mask_weights(ws_ref, wsm, wsmt)
            for j in range(N_DEV):
                win_v[:, 256 * j:256 * (j + 1)] = win_ref[j]
            acc_ref[...] = jnp.zeros((8, D), F32)
            dbin_ref[...] = jnp.zeros((8, 2 * D), F32)
            dws_ref[...] = jnp.zeros((H_C, CHUNK, CHUNK), F32)
            dmsum[...] = jnp.zeros((CHUNK, D), F32)

        dout_t = dout_ref[...]
        df = _post_bwd(dout_t, f_ref[...], vec_ref, acc_ref, 1.0)
        dfb = df.astype(BF)
        dfb_ref[...] = dfb
        h, xn, r, _ = _pre_fwd(x_ref[...], vec_ref)
        hb_ref[...] = h.astype(BF)
        u = sv_ref[:, 0:D].astype(F32)
        mix = sv_ref[:, D:2 * D].astype(F32)
        vhat = sv_ref[:, 4 * D:5 * D].astype(F32)
        rstd = rstd_ref[:, 0:1]
        dp = _dot_nt(dfb, wout_ref[...])
        du = dp * mix
        dmix = dp * u
        dmb = dmix.astype(BF)
        for nn in range(tm // CHUNK):
            rows = slice(CHUNK * nn, CHUNK * (nn + 1))
            dmsum[...] += dmix[rows, :]
            for hh in range(H_C):
                cols = slice(CHUNK * hh, CHUNK * (hh + 1))
                dvn_sc[rows, cols] = _dot(wsmt[hh], dmb[rows, cols])
                dws_ref[hh] += _dot_nt(dmb[rows, cols], sv_ref[rows, 5 * D + CHUNK * hh:5 * D + CHUNK * (hh + 1)])
        dvn = dvn_sc[...]
        acc_ref[5:6, :] += _colsum(dvn * vhat)
        acc_ref[6:7, :] += _colsum(dvn)
        dvhat = dvn * cvec_ref[0:1, :]
        dv = rstd * (dvhat - jnp.mean(dvhat, axis=-1, keepdims=True)
                     - vhat * jnp.mean(dvhat * vhat, axis=-1, keepdims=True))
        dpre_u = du * sv_ref[:, 2 * D:3 * D].astype(F32)
        dpre_v = dv * sv_ref[:, 3 * D:4 * D].astype(F32)
        dbin_ref[0:1, 0:D] += _colsum(dpre_u)
        dbin_ref[0:1, D:2 * D] += _colsum(dpre_v)
        dpre_ref[:, 0:D] = dpre_u.astype(BF)
        dpre_ref[:, D:2 * D] = dpre_v.astype(BF)
        dh = _dot_nt(dpre_ref[...], win_v[...])
        dx_ref[...] = _pre_bwd(dout_t, dh, xn, r, vec_ref, acc_ref)

        @pl.when(i == nt - 1)
        def _():
            _finish_acc(acc_ref, vec_ref, 1.0)
            row = lax.broadcasted_iota(jnp.int32, (CHUNK, CHUNK), 0)
            col = lax.broadcasted_iota(jnp.int32, (CHUNK, CHUNK), 1)
            for hh in range(H_C):
                dws_ref[hh] = jnp.where(row >= col, dws_ref[hh], 0.0)
                dbst_ref[:, hh:hh + 1] = jnp.sum(dmsum[:, CHUNK * hh:CHUNK * (hh + 1)], axis=1, keepdims=True)

    return pl.pallas_call(
        body, grid=(nt,),
        in_specs=[_tile(tm, D), _tile(tm, D), _tile(tm, D), _tile(tm, C_SAVED), _tile(tm, 128), _full((8, D)),
                  _full((N_DEV, D, 256)), _full((8, D)), _full((H_C, CHUNK, CHUNK)), _full((CHUNK, H_C)), _full((D, D))],
        out_specs=[_tile(tm, D), _tile(tm, 2 * D), _tile(tm, D), _tile(tm, D), _full((8, D)),
                   _full((8, 2 * D)), _full((H_C, CHUNK, CHUNK)), _full((CHUNK, H_C))],
        out_shape=[jax.ShapeDtypeStruct((t_len, D), F32), jax.ShapeDtypeStruct((t_len, 2 * D), BF),
                   jax.ShapeDtypeStruct((t_len, D), BF),
                   jax.ShapeDtypeStruct((t_len, D), BF), jax.ShapeDtypeStruct((8, D), F32),
                   jax.ShapeDtypeStruct((8, 2 * D), F32), jax.ShapeDtypeStruct((H_C, CHUNK, CHUNK), F32),
                   jax.ShapeDtypeStruct((CHUNK, H_C), F32)],
        scratch_shapes=[pltpu.VMEM((H_C, CHUNK, CHUNK), BF), pltpu.VMEM((H_C, CHUNK, CHUNK), BF),
                        pltpu.VMEM((tm, D), F32), pltpu.VMEM((CHUNK, D), F32), pltpu.VMEM((D, 2 * D), BF)],
        compiler_params=_params(), name=name,
    )(dout, x, fpre, saved, rstd_b, vec, w_in, cvec, ws, bst, w_out)


def _gmean(x, g_ref):
    hi = x.astype(BF)
    lo = (x - hi.astype(F32)).astype(BF)
    return _dot(hi, g_ref[...]) + _dot(lo, g_ref[...])


def _log_sigmoid(lam):
    e = jnp.exp(-jnp.abs(lam))
    log1p = jnp.where(e < 1e-2, e * (1.0 - e * (0.5 - e * (1.0 / 3.0 - 0.25 * e))), jnp.log(1.0 + e))
    return jnp.minimum(lam, 0.0) - log1p


def _neg_expm1(y):
    series = -(y * (1.0 + y * (0.5 + y * (1.0 / 6.0 + y * (1.0 / 24.0 + y * (1.0 / 120.0))))))
    return jnp.where(y > -0.1, series, 1.0 - jnp.exp(y))


def _rows_from(e, off, tm):
    return e[off:off + tm, :] if off % 8 == 0 else pltpu.roll(e, e.shape[0] - off, 0)[0:tm, :]


def _conv_causal(ext, taps_ref, bias, k_taps, halo, tm):
    e = ext[...]
    acc = bias
    for k in range(k_taps):
        acc = acc + taps_ref[k:k + 1, :] * _rows_from(e, halo - k_taps + 1 + k, tm)
    return acc


def _build_shifted(sh_ref, e, n_rows):
    sh_ref[0] = e
    for r in range(1, 8):
        sh_ref[r] = pltpu.roll(e, n_rows - r, 0)


def _shifted_rows(sh_ref, off, tm):
    base = off - off % 8
    return sh_ref[off % 8, base:base + tm, :]


def _scan(a, u, tm, reverse):
    row = lax.broadcasted_iota(jnp.int32, (tm, W_A), 0)
    d = 1
    while d < tm:
        if reverse:
            keep = row < tm - d
            shift = tm - d
        else:
            keep = row >= d
            shift = d
        a_sh = jnp.where(keep, pltpu.roll(a, shift, 0), 1.0)
        u_sh = jnp.where(keep, pltpu.roll(u, shift, 0), 0.0)
        u = a * u_sh + u
        a = a * a_sh
        d *= 2
    return a, u


def _a_gates(xc, cv_ref, wr_ref, wi_ref):
    xcb = xc.astype(BF)
    r = _sigmoid(_dot(xcb, wr_ref[...]) + cv_ref[5:6, :])
    ig = _sigmoid(_dot(xcb, wi_ref[...]) + cv_ref[6:7, :])
    ls = _log_sigmoid(cv_ref[7:8, :])
    la = LRU_C * r * ls
    a = jnp.exp(la)
    m = jnp.sqrt(_neg_expm1(2.0 * la))
    return xcb, r, ig, ls, a, m


def _b_norm(vc, cv_ref, g_ref):
    mu = _gmean(vc, g_ref)
    dv = vc - mu
    rstd = lax.rsqrt(_gmean(dv * dv, g_ref) + EPS)
    vhat = dv * rstd
    vln = vhat * cv_ref[9:10, :] + cv_ref[10:11, :]
    return rstd, vhat, vln


def _mixab_fwd(x, vec, w_in, cv, w31, wr, wi, gmat, w_out, name):
    t_len = x.shape[0]
    tm = min(TM_MIX, t_len)

    def body(x_ref, vec_ref, win_ref, cv_ref, w31_ref, wr_ref, wi_ref, g_ref, wout_ref,
             xo_ref, f_ref, z_ref, hs_ref, cvs_ref, ext_a, ext_b, hc, shifted):
        @pl.when(pl.program_id(0) == 0)
        def _():
            ext_a[0:HALO_A, :] = jnp.zeros((HALO_A, W_A), F32)
            ext_b[0:HALO_B, :] = jnp.zeros((HALO_B, W_B), F32)
            hc[...] = jnp.zeros((8, W_A), F32)

        x_t = x_ref[...]
        h, _, _, _ = _pre_fwd(x_t, vec_ref)
        hb = h.astype(BF)
        for j in range(N_DEV):
            z_ref[:, 256 * j:256 * (j + 1)] = _dot(hb, win_ref[j])
        ext_a[HALO_A:HALO_A + tm, :] = z_ref[:, W_A:2 * W_A]
        xc = _conv_causal(ext_a, cv_ref, cv_ref[4:5, :], CONV_A, HALO_A, tm)
        ext_a[0:HALO_A, :] = ext_a[tm:tm + HALO_A, :]
        cvs_ref[:, 0:W_A] = xc
        _, r, ig, _, a, m = _a_gates(xc, cv_ref, wr_ref, wi_ref)
        for col, val in enumerate((r, ig, a, m)):
            cvs_ref[:, W_A * (2 + col):W_A * (3 + col)] = val
        a_cum, hloc = _scan(a, m * ig * xc, tm, False)
        hs = hloc + a_cum * hc[0:1, :]
        hs_ref[...] = hs
        hc[0:1, :] = hs[tm - 1:tm, :]
        ag = z_ref[:, 0:W_A]
        gel, tg = _gelu(ag)
        cvs_ref[:, 7 * W_A:8 * W_A] = gel
        cvs_ref[:, 8 * W_A:9 * W_A] = _gelu_grad(ag, tg)
        ya = hs * gel
        ext_b[HALO_B:HALO_B + tm, :] = z_ref[:, 2 * W_A:2 * W_A + W_B] * _sigmoid(z_ref[:, 2 * W_A + W_B:2 * W_A + 2 * W_B])
        _build_shifted(shifted, ext_b[...], tm + HALO_B)
        vc = cv_ref[8:9, :] + w31_ref[0:1, :] * _shifted_rows(shifted, HALO_B - CONV_B + 1, tm)
        for k in range(1, CONV_B):
            vc = vc + w31_ref[k:k + 1, :] * _shifted_rows(shifted, HALO_B - CONV_B + 1 + k, tm)
        ext_b[0:HALO_B, :] = ext_b[tm:tm + HALO_B, :]
        rstd, vhat, vln = _b_norm(vc, cv_ref, g_ref)
        cvs_ref[:, W_A:W_A + W_B] = vhat
        cvs_ref[:, 6 * W_A:7 * W_A] = rstd
        yb = vln * _sigmoid(vln)
        fpre = _dot(ya.astype(BF), wout_ref[0:W_A, :]) + _dot(yb.astype(BF), wout_ref[W_A:W_A + W_B, :])
        f_ref[...] = fpre
        xo_ref[...] = _post_fwd(x_t, fpre, vec_ref, 1.0)

    return pl.pallas_call(
        body, grid=(t_len // tm,),
        in_specs=[_tile(tm, D), _full((8, D)), _full((N_DEV, D, 256)), _full((16, W_A)), _full((32, W_B)),
                  _full((W_A, W_A)), _full((W_A, W_A)), _full((W_B, W_B)), _full((D, D))],
        out_specs=[_tile(tm, D), _tile(tm, D), _tile(tm, 2 * D), _tile(tm, W_A), _tile(tm, AB_SAVED)],
        out_shape=[jax.ShapeDtypeStruct((t_len, D), F32), jax.ShapeDtypeStruct((t_len, D), F32),
                   jax.ShapeDtypeStruct((t_len, 2 * D), F32), jax.ShapeDtypeStruct((t_len, W_A), F32),
                   jax.ShapeDtypeStruct((t_len, AB_SAVED), F32)],
        scratch_shapes=[pltpu.VMEM((tm + HALO_A, W_A), F32), pltpu.VMEM((tm + HALO_B, W_B), F32), pltpu.VMEM((8, W_A), F32),
                        pltpu.VMEM((8, tm + HALO_B, W_B), F32)],
        compiler_params=_params(), name=name,
    )(x, vec, w_in, cv, w31, wr, wi, gmat, w_out)


def _mixab_bwd(dout, x, fpre, z, cvs, hs, vec, w_in, cv, w31, wr, wi, gmat, w_out, name, comm=()):
    t_len = x.shape[0]
    tm = min(TM_MIX, t_len)
    nt = t_len // tm

    def rev(i):
        return nt - 1 - i

    def rtile(ncol):
        return pl.BlockSpec((tm, ncol), lambda i: (rev(i), 0))

    def body(dout_ref, x_ref, f_ref, z_ref, cvs_ref, hs_ref, hsp_ref, vec_ref, win_ref, cv_ref, w31_ref, wr_ref, wi_ref,
             g_ref, wout_ref,
             dx_ref, dz_ref, yab_ref, hb_ref, dfb_ref, xcb_ref, dri_ref, acc_ref, accs_ref, dw31_ref,
             ext_h, ext_dx, ext_dv, carry, shifted, win_v):
        i = pl.program_id(0)
        has_prev = (rev(i) > 0).astype(F32)

        @pl.when(i == 0)
        def _():
            for j in range(N_DEV):
                win_v[:, 256 * j:256 * (j + 1)] = win_ref[j]
            acc_ref[...] = jnp.zeros((8, D), F32)
            accs_ref[...] = jnp.zeros((16, W_A), F32)
            dw31_ref[...] = jnp.zeros((32, W_B), F32)
            ext_dx[tm:tm + HALO_A, :] = jnp.zeros((HALO_A, W_A), F32)
            ext_dv[tm:tm + HALO_B, :] = jnp.zeros((HALO_B, W_B), F32)
            carry[...] = jnp.zeros((8, W_A), F32)

        dout_t = dout_ref[...]
        df = _post_bwd(dout_t, f_ref[...], vec_ref, acc_ref, 1.0)
        dfb = df.astype(BF)
        dfb_ref[...] = dfb
        h, xn, r_x, n = _pre_fwd(x_ref[...], vec_ref)
        hb_ref[...] = h.astype(BF)

        ag = z_ref[:, 0:W_A]
        ax = z_ref[:, W_A:2 * W_A]
        bv = z_ref[:, 2 * W_A:2 * W_A + W_B]
        sg = _sigmoid(z_ref[:, 2 * W_A + W_B:2 * W_A + 2 * W_B])
        vv = bv * sg
        xc = cvs_ref[:, 0:W_A]
        r, ig, a, m = (cvs_ref[:, W_A * (2 + col):W_A * (3 + col)] for col in range(4))
        ls = _log_sigmoid(cv_ref[7:8, :])
        xcb_ref[...] = xc.astype(BF)
        hs_t = hs_ref[...]
        ext_h[0:8, :] = hsp_ref[...] * has_prev
        ext_h[8:8 + tm, :] = hs_t
        hprev = ext_h[7:7 + tm, :]
        gel = cvs_ref[:, 7 * W_A:8 * W_A]
        vhat = cvs_ref[:, W_A:W_A + W_B]
        rstd = cvs_ref[:, 6 * W_A:7 * W_A]
        vln = vhat * cv_ref[9:10, :] + cv_ref[10:11, :]
        sv = _sigmoid(vln)
        yab_ref[:, 0:W_A] = (hs_t * gel).astype(BF)
        yab_ref[:, W_A:W_A + W_B] = (vln * sv).astype(BF)

        dya = _dot_nt(dfb, wout_ref[0:W_A, :])
        dyb = _dot_nt(dfb, wout_ref[W_A:W_A + W_B, :])

        dag = dya * hs_t * cvs_ref[:, 8 * W_A:9 * W_A]
        row = lax.broadcasted_iota(jnp.int32, (tm, W_A), 0)
        last = row == tm - 1
        a_next = jnp.where(last, 1.0, pltpu.roll(a, tm - 1, 0))
        u0 = dya * gel + jnp.where(last, carry[0:1, :], 0.0)
        _, dhs = _scan(a_next, u0, tm, True)
        carry[0:1, :] = a[0:1, :] * dhs[0:1, :]
        da = dhs * hprev
        dm = dhs * ig * xc
        di = dhs * m * xc
        dxc = dhs * m * ig
        dla = da * a - dm * (a * a) / m
        accs_ref[7:8, :] += _colsum(dla * r) * (LRU_C * _sigmoid(-cv_ref[7:8, :]))
        drp = (dla * (LRU_C * ls)) * r * (1.0 - r)
        dip = di * ig * (1.0 - ig)
        accs_ref[5:6, :] += _colsum(drp)
        accs_ref[6:7, :] += _colsum(dip)
        drpb = drp.astype(BF)
        dipb = dip.astype(BF)
        dri_ref[:, 0:W_A] = drpb
        dri_ref[:, W_A:2 * W_A] = dipb
        dxc = dxc + _dot_nt(drpb, wr_ref[...]) + _dot_nt(dipb, wi_ref[...])
        accs_ref[4:5, :] += _colsum(dxc)
        ext_dx[0:tm, :] = dxc
        e_dx = ext_dx[...]
        dax = jnp.zeros((tm, W_A), F32)
        for k in range(CONV_A):
            ahead = _rows_from(e_dx, CONV_A - 1 - k, tm)
            accs_ref[k:k + 1, :] += _colsum(ax * ahead)
            dax = dax + cv_ref[k:k + 1, :] * ahead
        ext_dx[tm:tm + HALO_A, :] = dxc[0:HALO_A, :]

        dvln = dyb * (sv * (1.0 + vln * (1.0 - sv)))
        accs_ref[9:10, :] += _colsum(dvln * vhat)
        accs_ref[10:11, :] += _colsum(dvln)
        dvhat = dvln * cv_ref[9:10, :]
        dvc = rstd * (dvhat - _gmean(dvhat, g_ref) - vhat * _gmean(dvhat * vhat, g_ref))
        accs_ref[8:9, :] += _colsum(dvc)
        ext_dv[0:tm, :] = dvc
        _build_shifted(shifted, ext_dv[...], tm + HALO_B)
        dvv = jnp.zeros((tm, W_B), F32)
        for k in range(CONV_B):
            ahead = _shifted_rows(shifted, CONV_B - 1 - k, tm)
            dw31_ref[k:k + 1, :] += _colsum(vv * ahead)
            dvv = dvv + w31_ref[k:k + 1, :] * ahead
        ext_dv[tm:tm + HALO_B, :] = dvc[0:HALO_B, :]

        dz_ref[:, 0:W_A] = dag.astype(BF)
        dz_ref[:, W_A:2 * W_A] = dax.astype(BF)
        dz_ref[:, 2 * W_A:2 * W_A + W_B] = (dvv * sg).astype(BF)
        dz_ref[:, 2 * W_A + W_B:2 * W_A + 2 * W_B] = (dvv * vv * (1.0 - sg)).astype(BF)
        dh = _dot_nt(dz_ref[...], win_v[...])
        dx_ref[...] = _pre_bwd(dout_t, dh, xn, r_x, vec_ref, acc_ref)

        @pl.when(i == nt - 1)
        def _():
            _finish_acc(acc_ref, vec_ref, 1.0)

    hsp_spec = pl.BlockSpec((8, W_A), lambda i: (jnp.maximum(rev(i) * (tm // 8) - 1, 0), 0))
    nc = len(comm)
    outs = pl.pallas_call(
        _hosted(body, 15, 10, 6, nc, "scatter", (nt,)), grid=(nt,),
        in_specs=[rtile(D), rtile(D), rtile(D), rtile(2 * D), rtile(AB_SAVED), rtile(W_A), hsp_spec, _full((8, D)),
                  _full((N_DEV, D, 256)), _full((16, W_A)), _full((32, W_B)), _full((W_A, W_A)), _full((W_A, W_A)),
                  _full((W_B, W_B)), _full((D, D))] + [_any()] * nc,
        out_specs=[rtile(D), rtile(2 * D), rtile(D), rtile(D), rtile(D), rtile(W_A), rtile(2 * W_A), _full((8, D)),
                   _full((16, W_A)), _full((32, W_B))] + [_any()] * nc,
        out_shape=[jax.ShapeDtypeStruct((t_len, D), F32), jax.ShapeDtypeStruct((t_len, 2 * D), BF),
                   jax.ShapeDtypeStruct((t_len, D), BF), jax.ShapeDtypeStruct((t_len, D), BF),
                   jax.ShapeDtypeStruct((t_len, D), BF), jax.ShapeDtypeStruct((t_len, W_A), BF),
                   jax.ShapeDtypeStruct((t_len, 2 * W_A), BF), jax.ShapeDtypeStruct((8, D), F32),
                   jax.ShapeDtypeStruct((16, W_A), F32), jax.ShapeDtypeStruct((32, W_B), F32)] + _exchange_shapes(comm, "scatter"),
        scratch_shapes=[pltpu.VMEM((tm + 8, W_A), F32), pltpu.VMEM((tm + HALO_A, W_A), F32),
                        pltpu.VMEM((tm + HALO_B, W_B), F32), pltpu.VMEM((8, W_A), F32),
                        pltpu.VMEM((8, tm + HALO_B, W_B), F32), pltpu.VMEM((D, 2 * D), BF)]
        + (_exchange_scratch(nc) if nc else []),
        compiler_params=_params(), name=name,
    )(dout, x, fpre, z, cvs, hs, hs, vec, w_in, cv, w31, wr, wi, gmat, w_out, *comm)
    return outs[:10], list(outs[10:])


def _vec(p, l, j):
    return jnp.concatenate([p["mod"][l, j], p["norm_pre"][l, j][None], p["norm_post"][l, j][None], jnp.zeros((3, D), F32)], 0)


def _ab_consts(p):
    gw = p["a_gate_w"]
    gb = p["a_gate_b"]
    half = W_A // 8
    eye = jnp.eye(8, dtype=F32)[:, None, :, None]

    def block_diag(blocks):
        return (blocks[:, :, None, :] * eye).reshape(W_A, W_A).astype(BF)

    wr = block_diag(gw[:, :, 0:half])
    wi = block_diag(gw[:, :, half:2 * half])
    rows = [p["a_conv_w"], p["a_conv_b"][None], gb[:, 0:half].reshape(1, W_A), gb[:, half:2 * half].reshape(1, W_A),
            p["a_lam"][None], p["b_conv_b"][None], p["b_norm_g"][None], p["b_norm_b"][None], jnp.zeros((5, W_A), F32)]
    cv = jnp.concatenate(rows, 0)
    w31 = jnp.concatenate([p["b_conv_w"], jnp.zeros((1, W_B), F32)], 0)
    grp = jnp.arange(W_B) // (W_B // 8)
    gmat = ((grp[:, None] == grp[None, :]).astype(F32) / (W_B // 8)).astype(BF)
    return cv, w31, wr, wi, gmat


SUBLAYERS = ("f0", "ab", "f1", "f2", "c", "f3")


def _local_step(x, tgt, p, plan=None):
    g = {}
    saved = []
    cur = x
    wsets = dict(p["wsets"])
    ab_c = _ab_consts(p)
    c_cvec = jnp.concatenate([p["c_norm_g"][None], p["c_norm_b"][None], jnp.zeros((6, D), F32)], 0)
    c_bst = jnp.transpose(p["c_b_s"])
    c_bin = p["c_b_in"][None]
    for s_idx, wname in enumerate(SUBLAYERS):
        l, j = divmod(s_idx, 3)
        vec = _vec(p, l, j)
        tag = f"l{l}s{j}"
        if j != 1:
            names = plan["gather"].get(s_idx, []) if plan else []
            comm = [a for nm in names for a in plan["shards"][nm]]
            res, got = _ffn_fwd(cur, vec, *wsets[wname], 0.5, "ffn_fwd_" + tag, comm=comm, tgt=tgt if s_idx == 5 else None)
            nxt, fpre, jac, s_act = res[:4]
            for k, nm in enumerate(names):
                wsets[nm] = list(got[2 * k:2 * k + 2])
            saved.append((cur, fpre, jac, s_act, vec))
            if s_idx == 5:
                loss_blk = res[4]
        elif l == 0:
            w_in, w_out = wsets[wname]
            nxt, fpre, z, hs, cvs = _mixab_fwd(cur, vec, w_in, *ab_c, w_out.reshape(D, D), "mixab_fwd_" + tag)
            saved.append((cur, fpre, z, hs, cvs, vec))
        else:
            w_in, w_out = wsets[wname]
            nxt, fpre, c_saved, c_rstd, pb = _mixc_fwd(cur, vec, w_in, c_bin, c_cvec, p["c_w_s"], c_bst, w_out.reshape(D, D),
                                                       "mixc_fwd_" + tag)
            saved.append((cur, fpre, c_saved, c_rstd, pb, vec))
        cur = nxt
    dcur = cur
    accs, pending, recv = {}, {}, {}

    def take(host):
        keys = plan["scatter"].get(host, []) if plan else []
        return keys, [pending.pop(k) for k in keys]

    def put(keys, got):
        recv.update(zip(keys, got))

    for s_idx in reversed(range(6)):
        wname = SUBLAYERS[s_idx]
        l, j = divmod(s_idx, 3)
        tag = f"l{l}s{j}"
        sv = saved[s_idx]
        if j != 1:
            keys, comm = take("ffn_bwd_" + tag)
            xin, fpre, jac, s, vec = sv
            (dcur, dgu, hb, dfb, acc), got = _ffn_bwd(dcur, xin, fpre, jac, vec, *wsets[wname], 0.5, "ffn_bwd_" + tag, comm=comm)
            put(keys, got)
            keys, comm = take("wgrad_w13_" + tag)
            dw13 = _wgrad(hb, dgu, 2, D, D_FF, "share", "cols", D, BF, "wgrad_w13_" + tag, tk=TK_WGRAD, comm=comm,
                          slots=(_w13_slots, (4, D, FF_PAD)))
            if keys:
                dw13, got = dw13
                put(keys, got)
            pending[wname + ".0"] = dw13
            keys, comm = take("wgrad_w2_" + tag)
            dw2 = _wgrad(s, dfb, 1, D_FF, D, "share", "share", D_FF, BF, "wgrad_w2_" + tag, tk=TK_WGRAD, comm=comm,
                         slots=(_w2_slots, (N_DEV, W2_SHARD, D)))
            if keys:
                dw2, got = dw2
                put(keys, got)
            pending[wname + ".1"] = dw2
        elif l == 0:
            xin, fpre, z, hs, cvs, vec = sv
            w_in, w_out = wsets[wname]
            keys, comm = take("mixab_bwd_" + tag)
            (dcur, dz, yab, hb, dfb, xcb, dri, acc, accs_ab, dw31), got = _mixab_bwd(
                dcur, xin, fpre, z, cvs, hs, vec, w_in, *ab_c, w_out.reshape(D, D), "mixab_bwd_" + tag, comm=comm)
            put(keys, got)
            d_in = _wgrad(hb, dz, 1, D, 2 * D, "share", "share", D, BF, "wgrad_ab_in", tk=TK_WGRAD, col_slots=N_DEV)
            d_out = _wgrad(yab, dfb, 1, D, D, "share", "share", D, BF, "wgrad_ab_out")
            pending[wname + ".0"], pending[wname + ".1"] = d_in, d_out.reshape(N_DEV, D // N_DEV, D)
            g["gate"] = _wgrad(xcb, dri, 1, W_A, 2 * W_A, "share", "share", W_A, F32, "wgrad_gate")
            g["accs_ab"] = accs_ab
            g["dw31"] = dw31
        else:
            xin, fpre, c_saved, c_rstd, pb, vec = sv
            w_in, w_out = wsets[wname]
            dcur, dpre, hb, dfb, acc, dbin, dws, dbst = _mixc_bwd(
                dcur, xin, fpre, c_saved, c_rstd, vec, w_in, c_cvec, p["c_w_s"], c_bst, w_out.reshape(D, D), "mixc_bwd_" + tag)
            d_in = _wgrad(hb, dpre, 1, D, 2 * D, "share", "share", D, BF, "wgrad_c_in", tk=TK_WGRAD, col_slots=N_DEV)
            d_out = _wgrad(pb, dfb, 1, D, D, "share", "share", D, BF, "wgrad_c_out")
            pending[wname + ".0"], pending[wname + ".1"] = d_in, d_out.reshape(N_DEV, D // N_DEV, D)
            g["c_small"] = (acc, dbin, dws, dbst)
        accs[f"{l}{j}"] = acc
    g["accs"] = accs
    g["pending"] = pending
    g["recv"] = recv
    return loss_blk, dcur, g


def _exchange_ops(ins, outs, sems, mode, action):
    send_sems, recv_sems, loc_sems = sems
    n = len(ins)
    modes = [mode] * n if isinstance(mode, str) else list(mode)
    x, y, c = lax.axis_index("x"), lax.axis_index("y"), lax.axis_index("c")
    me = 4 * x + 2 * y + c

    def src(i, dev):
        return ins[i] if modes[i] == "gather" else ins[i].at[dev]

    for i in range(n):
        cp = pltpu.make_async_copy(src(i, me), outs[i].at[me], loc_sems.at[i])
        if action == "start":
            cp.start()
        else:
            cp.wait()
    for mask in range(1, N_DEV):
        px = 1 - x if mask & 4 else x
        py = 1 - y if mask & 2 else y
        pc = 1 - c if mask & 1 else c
        peer = 4 * px + 2 * py + pc
        for i in range(n):
            k = i * (N_DEV - 1) + mask - 1
            cp = pltpu.make_async_remote_copy(
                src_ref=src(i, peer), dst_ref=outs[i].at[me if action == "start" else peer],
                send_sem=send_sems.at[k], recv_sem=recv_sems.at[k],
                device_id=(px, py, pc), device_id_type=pl.DeviceIdType.MESH)
            if action == "start":
                cp.start()
            else:
                cp.wait()


def _exchange_scratch(n):
    return [pltpu.SemaphoreType.DMA((n * (N_DEV - 1),)), pltpu.SemaphoreType.DMA((n * (N_DEV - 1),)),
            pltpu.SemaphoreType.DMA((n,))]


def _exchange_shapes(arrays, mode):
    modes = [mode] * len(arrays) if isinstance(mode, str) else list(mode)
    return [jax.ShapeDtypeStruct(((N_DEV,) + a.shape) if m == "gather" else a.shape, a.dtype)
            for a, m in zip(arrays, modes)]


def _exchange(arrays, mode, name):
    n = len(arrays)

    def body(*refs):
        ins, outs, sems = refs[:n], refs[n:2 * n], refs[2 * n:]
        _exchange_ops(ins, outs, sems, mode, "start")
        _exchange_ops(ins, outs, sems, mode, "wait")

    return pl.pallas_call(
        body, in_specs=[pl.BlockSpec(memory_space=pl.ANY)] * n, out_specs=[pl.BlockSpec(memory_space=pl.ANY)] * n,
        out_shape=_exchange_shapes(arrays, mode), scratch_shapes=_exchange_scratch(n), name=name,
    )(*arrays)


def _gather_two_level(arrays, name):
    n = len(arrays)
    per = N_DEV - 1

    def body(*refs):
        ins, outs = refs[:n], refs[n:2 * n]
        send_sems, recv_sems, loc_sems = refs[2 * n:]
        x, y, c = lax.axis_index("x"), lax.axis_index("y"), lax.axis_index("c")
        me, sibling = (x, y, c), (x, y, 1 - c)
        chips = [(1 - x, y), (x, 1 - y), (1 - x, 1 - y)]

        def rows(i, dev):
            return outs[i].at[4 * dev[0] + 2 * dev[1] + dev[2]]

        def copy(i, k, block, to, src=None):
            return pltpu.make_async_remote_copy(
                src_ref=rows(i, block) if src is None else src, dst_ref=rows(i, block),
                send_sem=send_sems.at[i * per + k], recv_sem=recv_sems.at[i * per + k],
                device_id=to, device_id_type=pl.DeviceIdType.MESH)

        mine = [pltpu.make_async_copy(ins[i], rows(i, me), loc_sems.at[i]) for i in range(n)]
        for cp in mine:
            cp.start()
        first = []
        for i in range(n):
            first.append(copy(i, 0, me, sibling, src=ins[i]))
            first += [copy(i, 1 + j, me, (*chip, c), src=ins[i]) for j, chip in enumerate(chips)]
        for cp in first:
            cp.start()
        passed = []
        for j, chip in enumerate(chips):
            for i in range(n):
                copy(i, 1 + j, (*chip, c), me).wait_recv()
                fwd = copy(i, 4 + j, (*chip, c), sibling)
                fwd.start()
                passed.append(fwd)
        for i in range(n):
            copy(i, 0, sibling, me).wait_recv()
            for j, chip in enumerate(chips):
                copy(i, 4 + j, (*chip, 1 - c), me).wait_recv()
        for cp in first + passed:
            cp.wait_send()
        for cp in mine:
            cp.wait()

    return pl.pallas_call(
        body, in_specs=[pl.BlockSpec(memory_space=pl.ANY)] * n, out_specs=[pl.BlockSpec(memory_space=pl.ANY)] * n,
        out_shape=_exchange_shapes(arrays, "gather"), scratch_shapes=_exchange_scratch(n), name=name,
    )(*arrays)


def _sum_slots(a, name):
    def body(a_ref, o_ref):
        acc = a_ref[0]
        for s in range(1, N_DEV):
            acc = acc + a_ref[s]
        o_ref[...] = acc

    return pl.pallas_call(body, out_shape=jax.ShapeDtypeStruct(a.shape[1:], F32), name=name,
                          compiler_params=pltpu.CompilerParams(vmem_limit_bytes=VMEM_LIMIT))(a)


def _pack(pieces, mult):
    flat = jnp.concatenate([q.reshape(-1).astype(F32) for q in pieces])
    size = -(-flat.shape[0] // mult) * mult
    return jnp.pad(flat, (0, size - flat.shape[0])).reshape(size // 128, 128)


def _unpack(flat, shapes):
    out, off = [], 0
    for shp in shapes:
        size = math.prod(shp)
        out.append(flat[..., off:off + size].reshape(flat.shape[:-1] + tuple(shp)))
        off += size
    return out


def _mod_part(c_all, ada_w, ada_b_mine, name):
    cols = ada_w.shape[-1]

    def body(c_ref, w_ref, b_ref, o_ref):
        cv = c_ref[...]
        ca = cv * _sigmoid(cv)
        for l in range(2):
            o_ref[l] = jnp.dot(ca, w_ref[l], preferred_element_type=F32, precision=lax.Precision.HIGHEST) + b_ref[l:l + 1, :]

    return pl.pallas_call(body, out_shape=jax.ShapeDtypeStruct((2, N_DEV, cols), F32), name=name,
                          compiler_params=pltpu.CompilerParams(vmem_limit_bytes=VMEM_LIMIT))(c_all, ada_w, ada_b_mine)


def _ada_w_grad(c_all_t, dmod_mine, name):
    cols = dmod_mine.shape[-1]

    def body(ct_ref, d_ref, o_ref):
        cv = ct_ref[...]
        ca = cv * _sigmoid(cv)
        for l in range(2):
            acc = ca[:, 0:1] * d_ref[l, 0:1, :]
            for b in range(1, N_DEV):
                acc = acc + ca[:, b:b + 1] * d_ref[l, b:b + 1, :]
            o_ref[l] = acc

    return pl.pallas_call(body, out_shape=jax.ShapeDtypeStruct((2, D, cols), F32), name=name,
                          compiler_params=pltpu.CompilerParams(vmem_limit_bytes=VMEM_LIMIT))(c_all_t, dmod_mine)


def _adamw_math(w, g, m, v):
    m2 = ADAM_B1 * m + (1.0 - ADAM_B1) * g
    v2 = ADAM_B2 * v + (1.0 - ADAM_B2) * (g * g)
    m_hat = m2 / (1.0 - ADAM_B1 ** ADAM_STEP)
    v_hat = v2 / (1.0 - ADAM_B2 ** ADAM_STEP)
    delta = -ADAM_LR * (m_hat / (jnp.sqrt(v_hat) + ADAM_EPS) + ADAM_WD * w)
    return delta, m2, v2


def _adamw_big(w, g, m, v, name):
    n_l, rows, cols = w.shape
    parts = list(g) if isinstance(g, (list, tuple)) else None
    sizes = (512, 352, 256, 128, 64, 32, 16, 8) if parts is None or len(parts) == 1 else (176, 128, 64, 32, 16, 8)
    br = next(b for b in sizes if rows % b == 0)
    nr = rows // br

    def body(*refs):
        w_ref, g_refs, (m_ref, v_ref, go_ref, d_ref, mo_ref, vo_ref) = refs[0], refs[1:-6], refs[-6:]

        def update(gsum):
            go_ref[...] = gsum
            d_ref[...], mo_ref[...], vo_ref[...] = _adamw_math(w_ref[...], gsum, m_ref[...], v_ref[...])

        if parts is None:
            update(g_refs[0][...])
        else:
            for l, g_ref in enumerate(g_refs):
                @pl.when(pl.program_id(0) == l)
                def _(g_ref=g_ref):
                    gsum = g_ref[0, :, 0:cols].astype(F32)
                    for s in range(1, N_DEV):
                        gsum = gsum + g_ref[s, :, 0:cols].astype(F32)
                    update(gsum)

    blk = pl.BlockSpec((None, br, cols), lambda l, i: (l, i, 0))
    if parts is None:
        g_specs, g_args = [blk], [g]
    else:
        def part_spec(l_mine, width):
            return pl.BlockSpec((N_DEV, br, width),
                                lambda l, i: (0, jnp.where(l < l_mine, 0, jnp.where(l == l_mine, i, nr - 1)), 0))
        g_specs, g_args = [part_spec(l, p.shape[-1]) for l, p in enumerate(parts)], parts
    shp = jax.ShapeDtypeStruct((n_l, rows, cols), F32)
    return pl.pallas_call(
        body, grid=(n_l, nr), in_specs=[blk] + g_specs + [blk, blk], out_specs=[blk] * 4, out_shape=[shp] * 4,
        compiler_params=pltpu.CompilerParams(dimension_semantics=("arbitrary", "arbitrary"), vmem_limit_bytes=VMEM_LIMIT),
        name=name)(w, *g_args, m, v)


def _adamw_small(ws, gs, ms, vs, name):
    n = len(ws)

    def body(*refs):
        for i in range(n):
            w_ref, g_ref, m_ref, v_ref = (refs[k * n + i] for k in range(4))
            d_ref, mo_ref, vo_ref = (refs[(4 + k) * n + i] for k in range(3))
            d_ref[...], mo_ref[...], vo_ref[...] = _adamw_math(w_ref[...], g_ref[...], m_ref[...], v_ref[...])

    shapes = [jax.ShapeDtypeStruct(w.shape, F32) for w in ws]
    outs = pl.pallas_call(body, out_shape=shapes * 3, name=name,
                          compiler_params=pltpu.CompilerParams(vmem_limit_bytes=VMEM_LIMIT))(*ws, *gs, *ms, *vs)
    return outs[:n], outs[n:2 * n], outs[2 * n:]


def _as2d(a):
    return a.reshape(-1, a.shape[-1])


def kernel(x, c, ada_w, ada_b, norm_pre, norm_post, ffn_w13, ffn_w2, ab_w_in, a_conv_w, a_conv_b, a_gate_w, a_gate_b, a_lam, b_conv_w, b_conv_b, b_norm_g, b_norm_b, ab_w_out, c_w_in, c_b_in, c_norm_g, c_norm_b, c_w_s, c_b_s, c_w_out, loss_target, m_ada_w, m_ada_b, m_norm_pre, m_norm_post, m_ffn_w13, m_ffn_w2, m_ab_w_in, m_a_conv_w, m_a_conv_b, m_a_gate_w, m_a_gate_b, m_a_lam, m_b_conv_w, m_b_conv_b, m_b_norm_g, m_b_norm_b, m_ab_w_out, m_c_w_in, m_c_b_in, m_c_norm_g, m_c_norm_b, m_c_w_s, m_c_b_s, m_c_w_out, v_ada_w, v_ada_b, v_norm_pre, v_norm_post, v_ffn_w13, v_ffn_w2, v_ab_w_in, v_a_conv_w, v_a_conv_b, v_a_gate_w, v_a_gate_b, v_a_lam, v_b_conv_w, v_b_conv_b, v_b_norm_g, v_b_norm_b, v_ab_w_out, v_c_w_in, v_c_b_in, v_c_norm_g, v_c_norm_b, v_c_w_s, v_c_b_s, v_c_w_out):
    me = 4 * lax.axis_index("x") + 2 * lax.axis_index("y") + lax.axis_index("c")
    weights = dict(ada_w=ada_w, ada_b=ada_b, norm_pre=norm_pre, norm_post=norm_post, ffn_w13=ffn_w13, ffn_w2=ffn_w2,
                   ab_w_in=ab_w_in, a_conv_w=a_conv_w, a_conv_b=a_conv_b, a_gate_w=a_gate_w, a_gate_b=a_gate_b, a_lam=a_lam,
                   b_conv_w=b_conv_w, b_conv_b=b_conv_b, b_norm_g=b_norm_g, b_norm_b=b_norm_b, ab_w_out=ab_w_out,
                   c_w_in=c_w_in, c_b_in=c_b_in, c_norm_g=c_norm_g, c_norm_b=c_norm_b, c_w_s=c_w_s, c_b_s=c_b_s, c_w_out=c_w_out)
    moms = dict(ada_w=m_ada_w, ada_b=m_ada_b, norm_pre=m_norm_pre, norm_post=m_norm_post, ffn_w13=m_ffn_w13, ffn_w2=m_ffn_w2,
                ab_w_in=m_ab_w_in, a_conv_w=m_a_conv_w, a_conv_b=m_a_conv_b, a_gate_w=m_a_gate_w, a_gate_b=m_a_gate_b,
                a_lam=m_a_lam, b_conv_w=m_b_conv_w, b_conv_b=m_b_conv_b, b_norm_g=m_b_norm_g, b_norm_b=m_b_norm_b,
                ab_w_out=m_ab_w_out, c_w_in=m_c_w_in, c_b_in=m_c_b_in, c_norm_g=m_c_norm_g, c_norm_b=m_c_norm_b,
                c_w_s=m_c_w_s, c_b_s=m_c_b_s, c_w_out=m_c_w_out)
    vars_ = dict(ada_w=v_ada_w, ada_b=v_ada_b, norm_pre=v_norm_pre, norm_post=v_norm_post, ffn_w13=v_ffn_w13, ffn_w2=v_ffn_w2,
                 ab_w_in=v_ab_w_in, a_conv_w=v_a_conv_w, a_conv_b=v_a_conv_b, a_gate_w=v_a_gate_w, a_gate_b=v_a_gate_b,
                 a_lam=v_a_lam, b_conv_w=v_b_conv_w, b_conv_b=v_b_conv_b, b_norm_g=v_b_norm_g, b_norm_b=v_b_norm_b,
                 ab_w_out=v_ab_w_out, c_w_in=v_c_w_in, c_b_in=v_c_b_in, c_norm_g=v_c_norm_g, c_norm_b=v_c_norm_b,
                 c_w_s=v_c_w_s, c_b_s=v_c_b_s, c_w_out=v_c_w_out)
    names = list(weights)

    w13s = ffn_w13.astype(BF).reshape(4, D, FF_SHARD)
    tail, blank = w13s[..., FF_MAIN:], jnp.zeros((4, D, FF_TAIL), BF)
    tail_tile = jnp.where(me % 2 == 1, jnp.concatenate([blank, tail], -1), jnp.concatenate([tail, blank], -1))
    w13b = jnp.concatenate([w13s[..., :FF_MAIN], tail_tile], -1)
    small_shapes = [(D,), (2, 3, 128), (2, 3, 128), (CONV_A, 64), (CONV_B, 64), (256,), (128,), (128,)]
    small = _pack([c, norm_pre, norm_post, a_conv_w, b_conv_w, c_b_in, c_norm_g, c_norm_b], 1024)
    w2b = ffn_w2.astype(BF).reshape(4, W2_SHARD, D)
    shards = {f"f{f}": [w13b[f], w2b[f]] for f in range(4)}
    shards["ab"] = [ab_w_in[0].astype(BF), ab_w_out[0].astype(BF)]
    shards["c"] = [c_w_in[0].astype(BF), c_w_out[0].astype(BF)]
    w13g0, w2g0, small_g = _gather_two_level(shards["f0"] + [small], "gather_first")
    plan = dict(shards=shards, gather={0: ["ab", "f1"], 2: ["f2"], 3: ["c", "f3"]},
                scatter={"ffn_bwd_l1s0": ["f3.0", "f3.1", "c.0", "c.1"], "ffn_bwd_l0s2": ["f2.0", "f2.1"],
                         "mixab_bwd_l0s1": ["f1.0", "f1.1"], "wgrad_w13_l0s0": ["ab.0", "ab.1"], "wgrad_w2_l0s0": ["f0.0"]})
    c_all, npre_g, npost_g, acw_g, bcw_g, cbin_g, cng_g, cnb_g = _unpack(small_g.reshape(N_DEV, -1), small_shapes)

    def cat_last(a):
        return jnp.moveaxis(a, 0, -2).reshape(a.shape[1:-1] + (N_DEV * a.shape[-1],))

    ada_b_mine = lax.dynamic_slice_in_dim(ada_b, me * ada_w.shape[-1], ada_w.shape[-1], axis=1)
    (mod_g,) = _exchange([_mod_part(c_all, ada_w, ada_b_mine, "mod_part")], "gather", "gather_mod")
    mod = cat_last(lax.dynamic_index_in_dim(mod_g, me, axis=2, keepdims=False)).reshape(2, 3, 3, D)

    p = dict(mod=mod, norm_pre=cat_last(npre_g), norm_post=cat_last(npost_g), wsets={"f0": [w13g0, w2g0]},
             a_conv_w=cat_last(acw_g), a_conv_b=a_conv_b[0], a_gate_w=a_gate_w[0], a_gate_b=a_gate_b[0], a_lam=a_lam[0],
             b_conv_w=cat_last(bcw_g), b_conv_b=b_conv_b[0], b_norm_g=b_norm_g[0], b_norm_b=b_norm_b[0],
             c_b_in=cat_last(cbin_g), c_norm_g=cat_last(cng_g), c_norm_b=cat_last(cnb_g), c_w_s=c_w_s[0], c_b_s=c_b_s[0])

    loss_blk, grad_x, g = _local_step(x[0], loss_target[0], p, plan)
    loss = lax.psum(loss_blk[0, 0], ("x", "y", "c"))

    accs = g["accs"]
    dmod = jnp.stack([jnp.stack([accs[f"{l}{j}"][0:3] for j in range(3)]) for l in range(2)])
    dnpre = jnp.stack([jnp.stack([accs[f"{l}{j}"][3] for j in range(3)]) for l in range(2)])
    dnpost = jnp.stack([jnp.stack([accs[f"{l}{j}"][4] for j in range(3)]) for l in range(2)])
    sab = g["accs_ab"]
    half = W_A // 8
    dgate = g["gate"][0]
    dgw = jnp.stack([jnp.concatenate([dgate[half * hh:half * (hh + 1), half * hh:half * (hh + 1)],
                                      dgate[half * hh:half * (hh + 1), W_A + half * hh:W_A + half * (hh + 1)]], axis=1)
                     for hh in range(8)])
    dgb = jnp.concatenate([sab[5].reshape(8, half), sab[6].reshape(8, half)], axis=1)
    c_acc, c_dbin, c_dws, c_dbst = g["c_small"]
    red_shapes = [(2, 9216), (2, 3, D), (2, 3, D), (CONV_A, W_A), (W_A,), (8, half, 2 * half), (8, 2 * half), (W_A,),
                  (CONV_B, W_B), (W_B,), (W_B,), (W_B,), (2 * D,), (D,), (D,), (H_C, CHUNK, CHUNK), (H_C, CHUNK)]
    red = _pack([dmod.reshape(2, 9216), dnpre, dnpost, sab[0:4], sab[4], dgw, dgb, sab[7], g["dw31"][0:CONV_B], sab[8],
                 sab[9], sab[10], c_dbin[0], c_acc[5], c_acc[6], c_dws, jnp.transpose(c_dbst)], N_DEV * 1024)
    left = sorted(g["pending"])
    (red_r,) = _exchange([red.reshape(N_DEV, -1, 128)], "scatter", "scatter_small_grads")
    red_all, dmod_all, *last_recv = _exchange(
        [_sum_slots(red_r, "sum_small_grads"), dmod.reshape(-1, 128)] + [g["pending"][k] for k in left],
        ["gather", "gather"] + ["scatter"] * len(left), "gather_small_grads")
    red_sum = red_all.reshape(-1)
    (g_ada_b, g_npre, g_npost, g_acw, g_acb, g_agw, g_agb, g_alam, g_bcw, g_bcb, g_bng, g_bnb, g_cbin, g_cng, g_cnb,
     g_cws, g_cbs) = _unpack(red_sum, red_shapes)
    dmod_all = dmod_all.reshape(N_DEV, 2, 9216)
    ncol = ada_w.shape[-1]
    dmod_mine = jnp.moveaxis(lax.dynamic_slice_in_dim(dmod_all, me * ncol, ncol, axis=2), 0, 1)
    g_ada_w = _ada_w_grad(jnp.transpose(c_all), dmod_mine, "ada_w_grad")

    def mine(a, width):
        return lax.dynamic_slice_in_dim(a, me * width, width, axis=a.ndim - 1)

    small_grads = dict(
        ada_b=g_ada_b, norm_pre=mine(g_npre, 128), norm_post=mine(g_npost, 128), a_conv_w=mine(g_acw, 64)[None],
        a_conv_b=g_acb[None], a_gate_w=g_agw[None], a_gate_b=g_agb[None], a_lam=g_alam[None], b_conv_w=mine(g_bcw, 64)[None],
        b_conv_b=g_bcb[None], b_norm_g=g_bng[None], b_norm_b=g_bnb[None], c_b_in=mine(g_cbin, 256)[None],
        c_norm_g=mine(g_cng, 128)[None], c_norm_b=mine(g_cnb, 128)[None], c_w_s=g_cws[None], c_b_s=g_cbs[None])

    recv = dict(g["recv"])
    recv.update(zip(left, last_recv))
    big_partials = dict(ffn_w13=[recv[f"f{f}.0"] for f in range(4)], ffn_w2=[recv[f"f{f}.1"] for f in range(4)],
                        ab_w_in=[recv["ab.0"]], ab_w_out=[recv["ab.1"]], c_w_in=[recv["c.0"]], c_w_out=[recv["c.1"]],
                        ada_w=g_ada_w)

    grads, deltas, new_m, new_v = {}, {}, {}, {}

    def as3d(a):
        return a.reshape((-1,) + a.shape[-2:])

    for nm, gp in big_partials.items():
        shp = weights[nm].shape
        go, dl, mo, vo = _adamw_big(as3d(weights[nm]), gp, as3d(moms[nm]), as3d(vars_[nm]), "adamw_" + nm)
        grads[nm], deltas[nm], new_m[nm], new_v[nm] = (a.reshape(shp) for a in (go, dl, mo, vo))
    snames = list(small_grads)
    dls, mos, vos = _adamw_small([_as2d(weights[nm]) for nm in snames], [_as2d(small_grads[nm]) for nm in snames],
                                 [_as2d(moms[nm]) for nm in snames], [_as2d(vars_[nm]) for nm in snames], "adamw_small")
    for k, nm in enumerate(snames):
        shp = weights[nm].shape
        grads[nm] = small_grads[nm].reshape(shp)
        deltas[nm], new_m[nm], new_v[nm] = dls[k].reshape(shp), mos[k].reshape(shp), vos[k].reshape(shp)

    return (loss, grad_x[None], *[grads[nm] for nm in names], *[deltas[nm] for nm in names],
            *[new_m[nm] for nm in names], *[new_v[nm] for nm in names])
```

```python
import math

import jax
import jax.numpy as jnp
from jax import lax
from jax.experimental import pallas as pl
from jax.experimental.pallas import tpu as pltpu

F32 = jnp.float32
BF = jnp.bfloat16

N_DEV = 8
D = 1024
EPS = 1e-6
D_FF = 2816
FF_SHARD = 704
FF_PAD = 768
FF_MAIN = 640
FF_TAIL = FF_SHARD - FF_MAIN
W2_SHARD = 352
W_A = 512
W_B = 512
AB_SAVED = 9 * W_A
CONV_A = 4
CONV_B = 31
HALO_A = 8
HALO_B = 32
LRU_C = 8.0
CHUNK = 128
H_C = 8
ADAM_LR = 0.001
ADAM_B1 = 0.9
ADAM_B2 = 0.999
ADAM_EPS = 1e-08
ADAM_WD = 0.01
ADAM_STEP = 10
VMEM_LIMIT = 62 * 1024 * 1024
GELU_C = math.sqrt(2.0 / math.pi)

TM_FFN = 512
TM_FFN_BWD = 256
TM_MIX = 256
TM_MIXC_FWD = 512
TK_WGRAD = 2048


def _params(limit=VMEM_LIMIT):
    return pltpu.CompilerParams(dimension_semantics=("arbitrary",), vmem_limit_bytes=limit)


def _dot(a, b):
    return jnp.dot(a, b, preferred_element_type=F32)


def _dot_nt(a, b):
    return lax.dot_general(a, b, (((1,), (1,)), ((), ())), preferred_element_type=F32)


def _dot_tn(a, b):
    return lax.dot_general(a, b, (((0,), (0,)), ((), ())), preferred_element_type=F32)


def _sigmoid(x):
    return 0.5 + 0.5 * jnp.tanh(0.5 * x)


def _gelu(x):
    t = jnp.tanh(GELU_C * (x + 0.044715 * x * x * x))
    return 0.5 * x * (1.0 + t), t


def _gelu_grad(x, t):
    return 0.5 * (1.0 + t) + 0.5 * x * (1.0 - t * t) * GELU_C * (1.0 + 3.0 * 0.044715 * x * x)


def _rms(x):
    r = lax.rsqrt(jnp.mean(x * x, axis=-1, keepdims=True) + EPS)
    return x * r, r


def _colsum(x):
    return jnp.sum(x, axis=0, keepdims=True)


def _pre_fwd(x, vec_ref):
    xn, r = _rms(x)
    n = xn * vec_ref[3:4, :]
    h = n * (1.0 + vec_ref[1:2, :]) + vec_ref[0:1, :]
    return h, xn, r, n


def _post_fwd(x, f, vec_ref, res_w):
    fn, _ = _rms(f)
    return x + fn * ((res_w * (1.0 + vec_ref[2:3, :])) * vec_ref[4:5, :])


def _post_bwd(dout, f, vec_ref, acc_ref, res_w):
    fn, r2 = _rms(f)
    acc_ref[2:3, :] += _colsum(dout * fn)
    dfn = dout * ((res_w * (1.0 + vec_ref[2:3, :])) * vec_ref[4:5, :])
    return r2 * (dfn - fn * jnp.mean(dfn * fn, axis=-1, keepdims=True))


def _pre_bwd(dout, dh, xn, r, vec_ref, acc_ref):
    acc_ref[0:1, :] += _colsum(dh)
    acc_ref[1:2, :] += _colsum(dh * xn)
    dxn = dh * ((1.0 + vec_ref[1:2, :]) * vec_ref[3:4, :])
    return dout + r * (dxn - xn * jnp.mean(dxn * xn, axis=-1, keepdims=True))


def _finish_acc(acc_ref, vec_ref, res_w):
    s_pre, s_post = acc_ref[1:2, :], acc_ref[2:3, :]
    acc_ref[1:2, :] = vec_ref[3:4, :] * s_pre
    acc_ref[3:4, :] = (1.0 + vec_ref[1:2, :]) * s_pre
    acc_ref[2:3, :] = (res_w * vec_ref[4:5, :]) * s_post
    acc_ref[4:5, :] = (res_w * (1.0 + vec_ref[2:3, :])) * s_post


def _tile(tm, ncol):
    return pl.BlockSpec((tm, ncol), lambda i: (i, 0))


def _full(shape):
    return pl.BlockSpec(shape, lambda i: (0,) * len(shape))


def _any():
    return pl.BlockSpec(memory_space=pl.ANY)


def _load_ffn_weights(w13_hbm, w2_hbm, w13_v, w2_v, tails, sems):
    copies = []
    for j in range(N_DEV):
        half, k = divmod(j, 4)
        copies.append((w13_hbm.at[j, :, pl.ds(0, FF_MAIN)], w13_v.at[:, pl.ds(D_FF * half + FF_MAIN * k, FF_MAIN)]))
        copies.append((w13_hbm.at[j, :, pl.ds(FF_MAIN, 128)], tails.at[j]))
    for k in range(4):
        copies.append((w2_hbm.at[2 * k], w2_v.at[pl.ds(FF_MAIN * k, W2_SHARD), :]))
        copies.append((w2_hbm.at[2 * k + 1, pl.ds(0, FF_MAIN - W2_SHARD), :],
                       w2_v.at[pl.ds(FF_MAIN * k + W2_SHARD, FF_MAIN - W2_SHARD), :]))
        copies.append((w2_hbm.at[2 * k + 1, pl.ds(FF_MAIN - W2_SHARD, FF_TAIL), :],
                       w2_v.at[pl.ds(4 * FF_MAIN + FF_TAIL * k, FF_TAIL), :]))
    copies = [pltpu.make_async_copy(src, dst, sems.at[n]) for n, (src, dst) in enumerate(copies)]
    for cp in copies:
        cp.start()
    for cp in copies:
        cp.wait()
    for pair in range(4):
        half, kk = divmod(pair, 2)
        base = D_FF * half + 4 * FF_MAIN + 128 * kk
        w13_v[:, base:base + 128] = tails[2 * pair] + tails[2 * pair + 1]


_FFN_SCRATCH = [pltpu.VMEM((D, 2 * D_FF), BF), pltpu.VMEM((D_FF, D), BF), pltpu.VMEM((N_DEV, D, 128), BF),
                pltpu.SemaphoreType.DMA((2 * N_DEV + 12,))]
HID_CHUNKS = ((0, 768), (768, 768), (1536, 768), (2304, 512))


def _hosted(body, n_in, n_out, n_scratch, n_comm, mode, grid):
    if not n_comm:
        return body

    def at(corner):
        hit = pl.program_id(0) == corner[0]
        for d in range(1, len(grid)):
            hit = hit & (pl.program_id(d) == corner[d])
        return hit

    def hosted(*refs):
        ins, cin = refs[:n_in], refs[n_in:n_in + n_comm]
        outs, cout = refs[n_in + n_comm:n_in + n_comm + n_out], refs[n_in + n_comm + n_out:n_in + 2 * n_comm + n_out]
        scratch = refs[n_in + 2 * n_comm + n_out:]
        own, sems = scratch[:n_scratch], scratch[n_scratch:]

        @pl.when(at([0] * len(grid)))
        def _():
            _exchange_ops(cin, cout, sems, mode, "start")

        body(*ins, *outs, *own)

        @pl.when(at([n - 1 for n in grid]))
        def _():
            _exchange_ops(cin, cout, sems, mode, "wait")

    return hosted


def _ffn_fwd(x, vec, w13g, w2g, res_w, name, comm=(), tgt=None):
    t_len = x.shape[0]
    tm = min(TM_FFN, t_len)
    nc = len(comm)
    head = tgt is not None

    def body(*refs):
        if head:
            x_ref, vec_ref, w13_hbm, w2_hbm, t_ref, xo_ref, f_ref, jac_ref, s_ref, loss_ref, w13_v, w2_v, tails, sems = refs
        else:
            x_ref, vec_ref, w13_hbm, w2_hbm, xo_ref, f_ref, jac_ref, s_ref, w13_v, w2_v, tails, sems = refs

        @pl.when(pl.program_id(0) == 0)
        def _():
            _load_ffn_weights(w13_hbm, w2_hbm, w13_v, w2_v, tails, sems)
            if head:
                loss_ref[...] = jnp.zeros((8, 128), F32)

        x_t = x_ref[...]
        h, _, _, _ = _pre_fwd(x_t, vec_ref)
        hb = h.astype(BF)
        for c0, cw in HID_CHUNKS:
            g = _dot(hb, w13_v[:, c0:c0 + cw])
            u = _dot(hb, w13_v[:, D_FF + c0:D_FF + c0 + cw])
            sig = _sigmoid(g)
            sl = g * sig
            jac_ref[0, :, c0:c0 + cw] = (u * (sig + sl * (1.0 - sig))).astype(BF)
            jac_ref[1, :, c0:c0 + cw] = sl.astype(BF)
            s_ref[:, c0:c0 + cw] = (sl * u).astype(BF)
        acc = _dot(s_ref[...], w2_v[...])
        f_ref[...] = acc
        xo = _post_fwd(x_t, acc, vec_ref, res_w)
        if head:
            err = xo - t_ref[...]
            xo_ref[...] = err * (1.0 / D)
            loss_ref[...] += jnp.sum(err * err) * (0.5 / D)
        else:
            xo_ref[...] = xo

    nt = t_len // tm
    n_in, n_out = (5, 5) if head else (4, 4)
    outs = pl.pallas_call(
        _hosted(body, n_in, n_out, 4, nc, "gather", (nt,)), grid=(nt,),
        in_specs=[_tile(tm, D), _full((8, D)), _any(), _any()] + ([_tile(tm, D)] if head else []) + [_any()] * nc,
        out_specs=[_tile(tm, D), _tile(tm, D), pl.BlockSpec((2, tm, D_FF), lambda i: (0, i, 0)), _tile(tm, D_FF)]
        + ([_full((8, 128))] if head else []) + [_any()] * nc,
        out_shape=[jax.ShapeDtypeStruct((t_len, D), F32), jax.ShapeDtypeStruct((t_len, D), F32),
                   jax.ShapeDtypeStruct((2, t_len, D_FF), BF), jax.ShapeDtypeStruct((t_len, D_FF), BF)]
        + ([jax.ShapeDtypeStruct((8, 128), F32)] if head else []) + _exchange_shapes(comm, "gather"),
        scratch_shapes=_FFN_SCRATCH + (_exchange_scratch(nc) if nc else []), compiler_params=_params(), name=name,
    )(x, vec, w13g, w2g, *([tgt] if head else []), *comm)
    return outs[:n_out], outs[n_out:]


def _ffn_bwd(dout, x, fpre, jac, vec, w13g, w2g, res_w, name, comm=()):
    t_len = x.shape[0]
    tm = min(TM_FFN_BWD, t_len)
    nt = t_len // tm
    nc = len(comm)

    def body(dout_ref, x_ref, f_ref, jac_ref, vec_ref, w13_hbm, w2_hbm,
             dx_ref, dgu_ref, hb_ref, dfb_ref, acc_ref, w13_v, w2_v, tails, sems):
        @pl.when(pl.program_id(0) == 0)
        def _():
            _load_ffn_weights(w13_hbm, w2_hbm, w13_v, w2_v, tails, sems)
            acc_ref[...] = jnp.zeros((8, D), F32)

        dout_t = dout_ref[...]
        df = _post_bwd(dout_t, f_ref[...], vec_ref, acc_ref, res_w)
        dfb = df.astype(BF)
        dfb_ref[...] = dfb
        h, xn, r, _ = _pre_fwd(x_ref[...], vec_ref)
        hb_ref[...] = h.astype(BF)
        for c0, cw in HID_CHUNKS:
            ds = _dot_nt(dfb, w2_v[c0:c0 + cw, :]).astype(BF)
            dgu_ref[:, c0:c0 + cw] = ds * jac_ref[0, :, c0:c0 + cw]
            dgu_ref[:, D_FF + c0:D_FF + c0 + cw] = ds * jac_ref[1, :, c0:c0 + cw]
        dh = _dot_nt(dgu_ref[...], w13_v[...])
        dx_ref[...] = _pre_bwd(dout_t, dh, xn, r, vec_ref, acc_ref)

        @pl.when(pl.program_id(0) == nt - 1)
        def _():
            _finish_acc(acc_ref, vec_ref, res_w)

    jac_spec = pl.BlockSpec((2, tm, D_FF), lambda i: (0, i, 0))
    outs = pl.pallas_call(
        _hosted(body, 7, 5, 4, nc, "scatter", (nt,)), grid=(nt,),
        in_specs=[_tile(tm, D), _tile(tm, D), _tile(tm, D), jac_spec, _full((8, D)), _any(), _any()] + [_any()] * nc,
        out_specs=[_tile(tm, D), _tile(tm, 2 * D_FF), _tile(tm, D), _tile(tm, D), _full((8, D))] + [_any()] * nc,
        out_shape=[jax.ShapeDtypeStruct((t_len, D), F32), jax.ShapeDtypeStruct((t_len, 2 * D_FF), BF),
                   jax.ShapeDtypeStruct((t_len, D), BF),
                   jax.ShapeDtypeStruct((t_len, D), BF), jax.ShapeDtypeStruct((8, D), F32)] + _exchange_shapes(comm, "scatter"),
        scratch_shapes=_FFN_SCRATCH + (_exchange_scratch(nc) if nc else []), compiler_params=_params(), name=name,
    )(dout, x, fpre, jac, vec, w13g, w2g, *comm)
    return outs[:5], outs[5:]


def _w13_slots(acc, o_ref):
    for k in range(4):
        o_ref[k, :, 0:FF_MAIN] = acc[:, FF_MAIN * k:FF_MAIN * (k + 1)].astype(BF)
        pair_tile = acc[:, 4 * FF_MAIN + 128 * (k // 2):4 * FF_MAIN + 128 * (k // 2 + 1)]
        o_ref[k, :, FF_MAIN:FF_PAD] = (pair_tile if k % 2 == 0 else pltpu.roll(pair_tile, FF_TAIL, 1)).astype(BF)


def _w2_slots(acc, o_ref):
    rest = FF_MAIN - W2_SHARD
    for k in range(4):
        o_ref[2 * k] = acc[FF_MAIN * k:FF_MAIN * k + W2_SHARD, :].astype(BF)
        o_ref[2 * k + 1, 0:rest, :] = acc[FF_MAIN * k + W2_SHARD:FF_MAIN * (k + 1), :].astype(BF)
        o_ref[2 * k + 1, rest:W2_SHARD, :] = acc[4 * FF_MAIN + FF_TAIL * k:4 * FF_MAIN + FF_TAIL * (k + 1), :].astype(BF)


def _wgrad(a, b, j_count, m, n, a_mode, b_mode, out_rows, out_dtype, name, tk=TK_WGRAD, col_slots=1, comm=(),
           slots=None):
    t_len = a.shape[-2]
    tk = min(tk, t_len)
    nk = t_len // tk
    wn = n // col_slots
    nc = len(comm)

    def spec(mode, width):
        if mode == "stack":
            return pl.BlockSpec((None, tk, width), lambda j, t: (j, t, 0))
        if mode == "cols":
            return pl.BlockSpec((tk, width), lambda j, t: (t, j))
        return pl.BlockSpec((tk, width), lambda j, t: (t, 0))

    def body(a_ref, b_ref, o_ref, acc):
        t = pl.program_id(1)

        @pl.when(t == 0)
        def _():
            acc[...] = jnp.zeros((m, n), F32)

        acc[...] += _dot_tn(a_ref[...], b_ref[...])

        @pl.when(t == nk - 1)
        def _():
            if slots is not None:
                slots[0](acc, o_ref)
            elif col_slots == 1:
                o_ref[...] = acc[0:out_rows, :].astype(out_dtype)
            else:
                for s in range(col_slots):
                    o_ref[s] = acc[0:out_rows, wn * s:wn * (s + 1)].astype(out_dtype)

    if slots is not None:
        blk = slots[1]
        out_spec = pl.BlockSpec(blk, lambda j, t: (j,) + (0,) * (len(blk) - 1))
        out_shape = jax.ShapeDtypeStruct((j_count * blk[0],) + blk[1:], BF)
    elif col_slots == 1:
        out_spec = pl.BlockSpec((None, out_rows, n), lambda j, t: (j, 0, 0))
        out_shape = jax.ShapeDtypeStruct((j_count, out_rows, n), out_dtype)
    else:
        out_spec = pl.BlockSpec((col_slots, out_rows, wn), lambda j, t: (0, 0, 0))
        out_shape = jax.ShapeDtypeStruct((col_slots, out_rows, wn), out_dtype)
    outs = pl.pallas_call(
        _hosted(body, 2, 1, 1, nc, "scatter", (j_count, nk)), grid=(j_count, nk),
        in_specs=[spec(a_mode, m), spec(b_mode, n)] + [_any()] * nc,
        out_specs=[out_spec] + [_any()] * nc, out_shape=[out_shape] + _exchange_shapes(comm, "scatter"),
        scratch_shapes=[pltpu.VMEM((m, n), F32)] + (_exchange_scratch(nc) if nc else []),
        compiler_params=pltpu.CompilerParams(dimension_semantics=("arbitrary", "arbitrary"), vmem_limit_bytes=VMEM_LIMIT),
        name=name,
    )(a, b, *comm)
    return (outs[0], list(outs[1:])) if nc else outs[0]


def _c_mask_weights(ws_ref, wsm, wsmt):
    row = lax.broadcasted_iota(jnp.int32, (CHUNK, CHUNK), 0)
    col = lax.broadcasted_iota(jnp.int32, (CHUNK, CHUNK), 1)
    for hh in range(H_C):
        w = jnp.where(row >= col, ws_ref[hh], 0.0)
        wsm[hh] = w.astype(BF)
        if wsmt is not None:
            wsmt[hh] = w.T.astype(BF)


def _c_inner(pre, cvec_ref, wsm, bst_ref, mix_sc, tm):
    z, t = _gelu(pre)
    u = z[:, 0:D]
    v = z[:, D:2 * D]
    mu = jnp.mean(v, axis=-1, keepdims=True)
    vc = v - mu
    rstd = lax.rsqrt(jnp.mean(vc * vc, axis=-1, keepdims=True) + EPS)
    vhat = vc * rstd
    vnb = (vhat * cvec_ref[0:1, :] + cvec_ref[1:2, :]).astype(BF)
    for nn in range(tm // CHUNK):
        for hh in range(H_C):
            rows = slice(CHUNK * nn, CHUNK * (nn + 1))
            cols = slice(CHUNK * hh, CHUNK * (hh + 1))
            mix_sc[rows, cols] = _dot(wsm[hh], vnb[rows, cols]) + bst_ref[:, hh:hh + 1]
    return u, t, rstd, vhat, vnb


C_SAVED = 6 * D


def _mixc_fwd(x, vec, w_in, b_in, cvec, ws, bst, w_out, name):
    t_len = x.shape[0]
    tm = min(TM_MIXC_FWD, t_len)

    def body(x_ref, vec_ref, win_ref, bin_ref, cvec_ref, ws_ref, bst_ref, wout_ref,
             xo_ref, f_ref, sv_ref, rstd_ref, p_ref, wsm, mix_sc, win_v):
        @pl.when(pl.program_id(0) == 0)
        def _():
            _c_mask_weights(ws_ref, wsm, None)
            for j in range(N_DEV):
                win_v[:, 256 * j:256 * (j + 1)] = win_ref[j]

        x_t = x_ref[...]
        h, _, _, _ = _pre_fwd(x_t, vec_ref)
        hb = h.astype(BF)
        pre = _dot(hb, win_v[...]) + bin_ref[...]
        u, t, rstd, vhat, vnb = _c_inner(pre, cvec_ref, wsm, bst_ref, mix_sc, tm)
        mix = mix_sc[...]
        pb = (u * mix).astype(BF)
        gg = _gelu_grad(pre, t)
        sv_ref[:, 0:D] = u.astype(BF)
        sv_ref[:, D:2 * D] = mix.astype(BF)
        sv_ref[:, 2 * D:4 * D] = gg.astype(BF)
        sv_ref[:, 4 * D:5 * D] = vhat.astype(BF)
        sv_ref[:, 5 * D:6 * D] = vnb
        rstd_ref[...] = jnp.broadcast_to(rstd, (tm, 128))
        p_ref[...] = pb
        fpre = _dot(pb, wout_ref[...])
        f_ref[...] = fpre
        xo_ref[...] = _post_fwd(x_t, fpre, vec_ref, 1.0)

    return pl.pallas_call(
        body, grid=(t_len // tm,),
        in_specs=[_tile(tm, D), _full((8, D)), _full((N_DEV, D, 256)), _full((1, 2 * D)), _full((8, D)),
                  _full((H_C, CHUNK, CHUNK)), _full((CHUNK, H_C)), _full((D, D))],
        out_specs=[_tile(tm, D), _tile(tm, D), _tile(tm, C_SAVED), _tile(tm, 128), _tile(tm, D)],
        out_shape=[jax.ShapeDtypeStruct((t_len, D), F32), jax.ShapeDtypeStruct((t_len, D), F32),
                   jax.ShapeDtypeStruct((t_len, C_SAVED), BF), jax.ShapeDtypeStruct((t_len, 128), F32),
                   jax.ShapeDtypeStruct((t_len, D), BF)],
        scratch_shapes=[pltpu.VMEM((H_C, CHUNK, CHUNK), BF), pltpu.VMEM((tm, D), F32), pltpu.VMEM((D, 2 * D), BF)],
        compiler_params=_params(), name=name,
    )(x, vec, w_in, b_in, cvec, ws, bst, w_out)


def _mixc_bwd(dout, x, fpre, saved, rstd_b, vec, w_in, cvec, ws, bst, w_out, name):
    t_len = x.shape[0]
    tm = min(TM_MIX, t_len)
    nt = t_len // tm

    def body(dout_ref, x_ref, f_ref, sv_ref, rstd_ref, vec_ref, win_ref, cvec_ref, ws_ref, bst_ref, wout_ref,
             dx_ref, dpre_ref, hb_ref, dfb_ref, acc_ref, dbin_ref, dws_ref, dbst_ref,
             wsm, wsmt, dvn_sc, dmsum, win_v):
        i = pl.program_id(0)

        @pl.when(i == 0)
        def _():
            _c_mask_weights(ws_ref, wsm, wsmt)
            for j in range(N_DEV):
                win_v[:, 256 * j:256 * (j + 1)] = win_ref[j]
            acc_ref[...] = jnp.zeros((8, D), F32)
            dbin_ref[...] = jnp.zeros((8, 2 * D), F32)
            dws_ref[...] = jnp.zeros((H_C, CHUNK, CHUNK), F32)
            dmsum[...] = jnp.zeros((CHUNK, D), F32)

        dout_t = dout_ref[...]
        df = _post_bwd(dout_t, f_ref[...], vec_ref, acc_ref, 1.0)
        dfb = df.astype(BF)
        dfb_ref[...] = dfb
        h, xn, r, _ = _pre_fwd(x_ref[...], vec_ref)
        hb_ref[...] = h.astype(BF)
        u = sv_ref[:, 0:D].astype(F32)
        mix = sv_ref[:, D:2 * D].astype(F32)
        vhat = sv_ref[:, 4 * D:5 * D].astype(F32)
        rstd = rstd_ref[:, 0:1]
        dp = _dot_nt(dfb, wout_ref[...])
        du = dp * mix
        dmix = dp * u
        dmb = dmix.astype(BF)
        for nn in range(tm // CHUNK):
            rows = slice(CHUNK * nn, CHUNK * (nn + 1))
            dmsum[...] += dmix[rows, :]
            for hh in range(H_C):
                cols = slice(CHUNK * hh, CHUNK * (hh + 1))
                dvn_sc[rows, cols] = _dot(wsmt[hh], dmb[rows, cols])
                dws_ref[hh] += _dot_nt(dmb[rows, cols], sv_ref[rows, 5 * D + CHUNK * hh:5 * D + CHUNK * (hh + 1)])
        dvn = dvn_sc[...]
        acc_ref[5:6, :] += _colsum(dvn * vhat)
        acc_ref[6:7, :] += _colsum(dvn)
        dvhat = dvn * cvec_ref[0:1, :]
        dv = rstd * (dvhat - jnp.mean(dvhat, axis=-1, keepdims=True)
                     - vhat * jnp.mean(dvhat * vhat, axis=-1, keepdims=True))
        dpre_u = du * sv_ref[:, 2 * D:3 * D].astype(F32)
        dpre_v = dv * sv_ref[:, 3 * D:4 * D].astype(F32)
        dbin_ref[0:1, 0:D] += _colsum(dpre_u)
        dbin_ref[0:1, D:2 * D] += _colsum(dpre_v)
        dpre_ref[:, 0:D] = dpre_u.astype(BF)
        dpre_ref[:, D:2 * D] = dpre_v.astype(BF)
        dh = _dot_nt(dpre_ref[...], win_v[...])
        dx_ref[...] = _pre_bwd(dout_t, dh, xn, r, vec_ref, acc_ref)

        @pl.when(i == nt - 1)
        def _():
            _finish_acc(acc_ref, vec_ref, 1.0)
            row = lax.broadcasted_iota(jnp.int32, (CHUNK, CHUNK), 0)
            col = lax.broadcasted_iota(jnp.int32, (CHUNK, CHUNK), 1)
            for hh in range(H_C):
                dws_ref[hh] = jnp.where(row >= col, dws_ref[hh], 0.0)
                dbst_ref[:, hh:hh + 1] = jnp.sum(dmsum[:, CHUNK * hh:CHUNK * (hh + 1)], axis=1, keepdims=True)

    return pl.pallas_call(
        body, grid=(nt,),
        in_specs=[_tile(tm, D), _tile(tm, D), _tile(tm, D), _tile(tm, C_SAVED), _tile(tm, 128), _full((8, D)),
                  _full((N_DEV, D, 256)), _full((8, D)), _full((H_C, CHUNK, CHUNK)), _full((CHUNK, H_C)), _full((D, D))],
        out_specs=[_tile(tm, D), _tile(tm, 2 * D), _tile(tm, D), _tile(tm, D), _full((8, D)),
                   _full((8, 2 * D)), _full((H_C, CHUNK, CHUNK)), _full((CHUNK, H_C))],
        out_shape=[jax.ShapeDtypeStruct((t_len, D), F32), jax.ShapeDtypeStruct((t_len, 2 * D), BF),
                   jax.ShapeDtypeStruct((t_len, D), BF),
                   jax.ShapeDtypeStruct((t_len, D), BF), jax.ShapeDtypeStruct((8, D), F32),
                   jax.ShapeDtypeStruct((8, 2 * D), F32), jax.ShapeDtypeStruct((H_C, CHUNK, CHUNK), F32),
                   jax.ShapeDtypeStruct((CHUNK, H_C), F32)],
        scratch_shapes=[pltpu.VMEM((H_C, CHUNK, CHUNK), BF), pltpu.VMEM((H_C, CHUNK, CHUNK), BF),
                        pltpu.VMEM((tm, D), F32), pltpu.VMEM((CHUNK, D), F32), pltpu.VMEM((D, 2 * D), BF)],
        compiler_params=_params(), name=name,
    )(dout, x, fpre, saved, rstd_b, vec, w_in, cvec, ws, bst, w_out)


def _gmean(x, g_ref):
    hi = x.astype(BF)
    lo = (x - hi.astype(F32)).astype(BF)
    return _dot(hi, g_ref[...]) + _dot(lo, g_ref[...])


def _log_sigmoid(lam):
    e = jnp.exp(-jnp.abs(lam))
    log1p = jnp.where(e < 1e-2, e * (1.0 - e * (0.5 - e * (1.0 / 3.0 - 0.25 * e))), jnp.log(1.0 + e))
    return jnp.minimum(lam, 0.0) - log1p


def _neg_expm1(y):
    series = -(y * (1.0 + y * (0.5 + y * (1.0 / 6.0 + y * (1.0 / 24.0 + y * (1.0 / 120.0))))))
    return jnp.where(y > -0.1, series, 1.0 - jnp.exp(y))


def _rows_from(e, off, tm):
    return e[off:off + tm, :] if off % 8 == 0 else pltpu.roll(e, e.shape[0] - off, 0)[0:tm, :]


def _conv_causal(ext, taps_ref, bias, k_taps, halo, tm):
    e = ext[...]
    acc = bias
    for k in range(k_taps):
        acc = acc + taps_ref[k:k + 1, :] * _rows_from(e, halo - k_taps + 1 + k, tm)
    return acc


def _build_shifted(sh_ref, e, n_rows):
    sh_ref[0] = e
    for r in range(1, 8):
        sh_ref[r] = pltpu.roll(e, n_rows - r, 0)


def _shifted_rows(sh_ref, off, tm):
    base = off - off % 8
    return sh_ref[off % 8, base:base + tm, :]


def _scan(a, u, tm, reverse):
    row = lax.broadcasted_iota(jnp.int32, (tm, W_A), 0)
    d = 1
    while d < tm:
        if reverse:
            keep = row < tm - d
            shift = tm - d
        else:
            keep = row >= d
            shift = d
        a_sh = jnp.where(keep, pltpu.roll(a, shift, 0), 1.0)
        u_sh = jnp.where(keep, pltpu.roll(u, shift, 0), 0.0)
        u = a * u_sh + u
        a = a * a_sh
        d *= 2
    return a, u


def _a_gates(xc, cv_ref, wr_ref, wi_ref):
    xcb = xc.astype(BF)
    r = _sigmoid(_dot(xcb, wr_ref[...]) + cv_ref[5:6, :])
    ig = _sigmoid(_dot(xcb, wi_ref[...]) + cv_ref[6:7, :])
    ls = _log_sigmoid(cv_ref[7:8, :])
    la = LRU_C * r * ls
    a = jnp.exp(la)
    m = jnp.sqrt(_neg_expm1(2.0 * la))
    return xcb, r, ig, ls, a, m


def _b_norm(vc, cv_ref, g_ref):
    mu = _gmean(vc, g_ref)
    dv = vc - mu
    rstd = lax.rsqrt(_gmean(dv * dv, g_ref) + EPS)
    vhat = dv * rstd
    vln = vhat * cv_ref[9:10, :] + cv_ref[10:11, :]
    return rstd, vhat, vln


def _mixab_fwd(x, vec, w_in, cv, w31, wr, wi, gmat, w_out, name):
    t_len = x.shape[0]
    tm = min(TM_MIX, t_len)

    def body(x_ref, vec_ref, win_ref, cv_ref, w31_ref, wr_ref, wi_ref, g_ref, wout_ref,
             xo_ref, f_ref, z_ref, hs_ref, cvs_ref, ext_a, ext_b, hc, shifted, win_v):
        @pl.when(pl.program_id(0) == 0)
        def _():
            ext_a[0:HALO_A, :] = jnp.zeros((HALO_A, W_A), F32)
            ext_b[0:HALO_B, :] = jnp.zeros((HALO_B, W_B), F32)
            hc[...] = jnp.zeros((8, W_A), F32)
            for j in range(N_DEV):
                win_v[:, 256 * j:256 * (j + 1)] = win_ref[j]

        x_t = x_ref[...]
        h, _, _, _ = _pre_fwd(x_t, vec_ref)
        hb = h.astype(BF)
        z_ref[...] = _dot(hb, win_v[...])
        ext_a[HALO_A:HALO_A + tm, :] = z_ref[:, W_A:2 * W_A]
        xc = _conv_causal(ext_a, cv_ref, cv_ref[4:5, :], CONV_A, HALO_A, tm)
        ext_a[0:HALO_A, :] = ext_a[tm:tm + HALO_A, :]
        cvs_ref[:, 0:W_A] = xc
        _, r, ig, _, a, m = _a_gates(xc, cv_ref, wr_ref, wi_ref)
        for col, val in enumerate((r, ig, a, m)):
            cvs_ref[:, W_A * (2 + col):W_A * (3 + col)] = val
        a_cum, hloc = _scan(a, m * ig * xc, tm, False)
        hs = hloc + a_cum * hc[0:1, :]
        hs_ref[...] = hs
        hc[0:1, :] = hs[tm - 1:tm, :]
        ag = z_ref[:, 0:W_A]
        gel, tg = _gelu(ag)
        cvs_ref[:, 7 * W_A:8 * W_A] = gel
        cvs_ref[:, 8 * W_A:9 * W_A] = _gelu_grad(ag, tg)
        ya = hs * gel
        ext_b[HALO_B:HALO_B + tm, :] = z_ref[:, 2 * W_A:2 * W_A + W_B] * _sigmoid(z_ref[:, 2 * W_A + W_B:2 * W_A + 2 * W_B])
        _build_shifted(shifted, ext_b[...], tm + HALO_B)
        vc = cv_ref[8:9, :] + w31_ref[0:1, :] * _shifted_rows(shifted, HALO_B - CONV_B + 1, tm)
        for k in range(1, CONV_B):
            vc = vc + w31_ref[k:k + 1, :] * _shifted_rows(shifted, HALO_B - CONV_B + 1 + k, tm)
        ext_b[0:HALO_B, :] = ext_b[tm:tm + HALO_B, :]
        rstd, vhat, vln = _b_norm(vc, cv_ref, g_ref)
        cvs_ref[:, W_A:W_A + W_B] = vhat
        cvs_ref[:, 6 * W_A:7 * W_A] = rstd
        yb = vln * _sigmoid(vln)
        fpre = _dot(ya.astype(BF), wout_ref[0:W_A, :]) + _dot(yb.astype(BF), wout_ref[W_A:W_A + W_B, :])
        f_ref[...] = fpre
        xo_ref[...] = _post_fwd(x_t, fpre, vec_ref, 1.0)

    return pl.pallas_call(
        body, grid=(t_len // tm,),
        in_specs=[_tile(tm, D), _full((8, D)), _full((N_DEV, D, 256)), _full((16, W_A)), _full((32, W_B)),
                  _full((W_A, W_A)), _full((W_A, W_A)), _full((W_B, W_B)), _full((D, D))],
        out_specs=[_tile(tm, D), _tile(tm, D), _tile(tm, 2 * D), _tile(tm, W_A), _tile(tm, AB_SAVED)],
        out_shape=[jax.ShapeDtypeStruct((t_len, D), F32), jax.ShapeDtypeStruct((t_len, D), F32),
                   jax.ShapeDtypeStruct((t_len, 2 * D), F32), jax.ShapeDtypeStruct((t_len, W_A), F32),
                   jax.ShapeDtypeStruct((t_len, AB_SAVED), F32)],
        scratch_shapes=[pltpu.VMEM((tm + HALO_A, W_A), F32), pltpu.VMEM((tm + HALO_B, W_B), F32), pltpu.VMEM((8, W_A), F32),
                        pltpu.VMEM((8, tm + HALO_B, W_B), F32), pltpu.VMEM((D, 2 * D), BF)],
        compiler_params=_params(), name=name,
    )(x, vec, w_in, cv, w31, wr, wi, gmat, w_out)


def _mixab_bwd(dout, x, fpre, z, cvs, hs, vec, w_in, cv, w31, wr, wi, gmat, w_out, name, comm=()):
    t_len = x.shape[0]
    tm = min(TM_MIX, t_len)
    nt = t_len // tm

    def rev(i):
        return nt - 1 - i

    def rtile(ncol):
        return pl.BlockSpec((tm, ncol), lambda i: (rev(i), 0))

    def body(dout_ref, x_ref, f_ref, z_ref, cvs_ref, hs_ref, hsp_ref, vec_ref, win_ref, cv_ref, w31_ref, wr_ref, wi_ref,
             g_ref, wout_ref,
             dx_ref, dz_ref, yab_ref, hb_ref, dfb_ref, xcb_ref, dri_ref, acc_ref, accs_ref, dw31_ref,
             ext_h, ext_dx, ext_dv, carry, shifted, win_v):
        i = pl.program_id(0)
        has_prev = (rev(i) > 0).astype(F32)

        @pl.when(i == 0)
        def _():
            for j in range(N_DEV):
                win_v[:, 256 * j:256 * (j + 1)] = win_ref[j]
            acc_ref[...] = jnp.zeros((8, D), F32)
            accs_ref[...] = jnp.zeros((16, W_A), F32)
            dw31_ref[...] = jnp.zeros((32, W_B), F32)
            ext_dx[tm:tm + HALO_A, :] = jnp.zeros((HALO_A, W_A), F32)
            ext_dv[tm:tm + HALO_B, :] = jnp.zeros((HALO_B, W_B), F32)
            carry[...] = jnp.zeros((8, W_A), F32)

        dout_t = dout_ref[...]
        df = _post_bwd(dout_t, f_ref[...], vec_ref, acc_ref, 1.0)
        dfb = df.astype(BF)
        dfb_ref[...] = dfb
        h, xn, r_x, n = _pre_fwd(x_ref[...], vec_ref)
        hb_ref[...] = h.astype(BF)

        ag = z_ref[:, 0:W_A]
        ax = z_ref[:, W_A:2 * W_A]
        bv = z_ref[:, 2 * W_A:2 * W_A + W_B]
        sg = _sigmoid(z_ref[:, 2 * W_A + W_B:2 * W_A + 2 * W_B])
        vv = bv * sg
        xc = cvs_ref[:, 0:W_A]
        r, ig, a, m = (cvs_ref[:, W_A * (2 + col):W_A * (3 + col)] for col in range(4))
        ls = _log_sigmoid(cv_ref[7:8, :])
        xcb_ref[...] = xc.astype(BF)
        hs_t = hs_ref[...]
        ext_h[0:8, :] = hsp_ref[...] * has_prev
        ext_h[8:8 + tm, :] = hs_t
        hprev = ext_h[7:7 + tm, :]
        gel = cvs_ref[:, 7 * W_A:8 * W_A]
        vhat = cvs_ref[:, W_A:W_A + W_B]
        rstd = cvs_ref[:, 6 * W_A:7 * W_A]
        vln = vhat * cv_ref[9:10, :] + cv_ref[10:11, :]
        sv = _sigmoid(vln)
        yab_ref[:, 0:W_A] = (hs_t * gel).astype(BF)
        yab_ref[:, W_A:W_A + W_B] = (vln * sv).astype(BF)

        dya = _dot_nt(dfb, wout_ref[0:W_A, :])
        dyb = _dot_nt(dfb, wout_ref[W_A:W_A + W_B, :])

        dag = dya * hs_t * cvs_ref[:, 8 * W_A:9 * W_A]
        row = lax.broadcasted_iota(jnp.int32, (tm, W_A), 0)
        last = row == tm - 1
        a_next = jnp.where(last, 1.0, pltpu.roll(a, tm - 1, 0))
        u0 = dya * gel + jnp.where(last, carry[0:1, :], 0.0)
        _, dhs = _scan(a_next, u0, tm, True)
        carry[0:1, :] = a[0:1, :] * dhs[0:1, :]
        da = dhs * hprev
        dm = dhs * ig * xc
        di = dhs * m * xc
        dxc = dhs * m * ig
        dla = da * a - dm * (a * a) / m
        accs_ref[7:8, :] += _colsum(dla * r) * (LRU_C * _sigmoid(-cv_ref[7:8, :]))
        drp = (dla * (LRU_C * ls)) * r * (1.0 - r)
        dip = di * ig * (1.0 - ig)
        accs_ref[5:6, :] += _colsum(drp)
        accs_ref[6:7, :] += _colsum(dip)
        drpb = drp.astype(BF)
        dipb = dip.astype(BF)
        dri_ref[:, 0:W_A] = drpb
        dri_ref[:, W_A:2 * W_A] = dipb
        dxc = dxc + _dot_nt(drpb, wr_ref[...]) + _dot_nt(dipb, wi_ref[...])
        accs_ref[4:5, :] += _colsum(dxc)
        ext_dx[0:tm, :] = dxc
        e_dx = ext_dx[...]
        dax = jnp.zeros((tm, W_A), F32)
        for k in range(CONV_A):
            ahead = _rows_from(e_dx, CONV_A - 1 - k, tm)
            accs_ref[k:k + 1, :] += _colsum(ax * ahead)
            dax = dax + cv_ref[k:k + 1, :] * ahead
        ext_dx[tm:tm + HALO_A, :] = dxc[0:HALO_A, :]

        dvln = dyb * (sv * (1.0 + vln * (1.0 - sv)))
        accs_ref[9:10, :] += _colsum(dvln * vhat)
        accs_ref[10:11, :] += _colsum(dvln)
        dvhat = dvln * cv_ref[9:10, :]
        dvc = rstd * (dvhat - _gmean(dvhat, g_ref) - vhat * _gmean(dvhat * vhat, g_ref))
        accs_ref[8:9, :] += _colsum(dvc)
        ext_dv[0:tm, :] = dvc
        _build_shifted(shifted, ext_dv[...], tm + HALO_B)
        dvv = jnp.zeros((tm, W_B), F32)
        for k in range(CONV_B):
            ahead = _shifted_rows(shifted, CONV_B - 1 - k, tm)
            dw31_ref[k:k + 1, :] += _colsum(vv * ahead)
            dvv = dvv + w31_ref[k:k + 1, :] * ahead
        ext_dv[tm:tm + HALO_B, :] = dvc[0:HALO_B, :]

        dz_ref[:, 0:W_A] = dag.astype(BF)
        dz_ref[:, W_A:2 * W_A] = dax.astype(BF)
        dz_ref[:, 2 * W_A:2 * W_A + W_B] = (dvv * sg).astype(BF)
        dz_ref[:, 2 * W_A + W_B:2 * W_A + 2 * W_B] = (dvv * vv * (1.0 - sg)).astype(BF)
        dh = _dot_nt(dz_ref[...], win_v[...])
        dx_ref[...] = _pre_bwd(dout_t, dh, xn, r_x, vec_ref, acc_ref)

        @pl.when(i == nt - 1)
        def _():
            _finish_acc(acc_ref, vec_ref, 1.0)

    hsp_spec = pl.BlockSpec((8, W_A), lambda i: (jnp.maximum(rev(i) * (tm // 8) - 1, 0), 0))
    nc = len(comm)
    outs = pl.pallas_call(
        _hosted(body, 15, 10, 6, nc, "scatter", (nt,)), grid=(nt,),
        in_specs=[rtile(D), rtile(D), rtile(D), rtile(2 * D), rtile(AB_SAVED), rtile(W_A), hsp_spec, _full((8, D)),
                  _full((N_DEV, D, 256)), _full((16, W_A)), _full((32, W_B)), _full((W_A, W_A)), _full((W_A, W_A)),
                  _full((W_B, W_B)), _full((D, D))] + [_any()] * nc,
        out_specs=[rtile(D), rtile(2 * D), rtile(D), rtile(D), rtile(D), rtile(W_A), rtile(2 * W_A), _full((8, D)),
                   _full((16, W_A)), _full((32, W_B))] + [_any()] * nc,
        out_shape=[jax.ShapeDtypeStruct((t_len, D), F32), jax.ShapeDtypeStruct((t_len, 2 * D), BF),
                   jax.ShapeDtypeStruct((t_len, D), BF), jax.ShapeDtypeStruct((t_len, D), BF),
                   jax.ShapeDtypeStruct((t_len, D), BF), jax.ShapeDtypeStruct((t_len, W_A), BF),
                   jax.ShapeDtypeStruct((t_len, 2 * W_A), BF), jax.ShapeDtypeStruct((8, D), F32),
                   jax.ShapeDtypeStruct((16, W_A), F32), jax.ShapeDtypeStruct((32, W_B), F32)] + _exchange_shapes(comm, "scatter"),
        scratch_shapes=[pltpu.VMEM((tm + 8, W_A), F32), pltpu.VMEM((tm + HALO_A, W_A), F32),
                        pltpu.VMEM((tm + HALO_B, W_B), F32), pltpu.VMEM((8, W_A), F32),
                        pltpu.VMEM((8, tm + HALO_B, W_B), F32), pltpu.VMEM((D, 2 * D), BF)]
        + (_exchange_scratch(nc) if nc else []),
        compiler_params=_params(), name=name,
    )(dout, x, fpre, z, cvs, hs, hs, vec, w_in, cv, w31, wr, wi, gmat, w_out, *comm)
    return outs[:10], list(outs[10:])


def _vec(p, l, j):
    return jnp.concatenate([p["mod"][l, j], p["norm_pre"][l, j][None], p["norm_post"][l, j][None], jnp.zeros((3, D), F32)], 0)


def _ab_consts(p):
    gw = p["a_gate_w"]
    gb = p["a_gate_b"]
    half = W_A // 8
    eye = jnp.eye(8, dtype=F32)[:, None, :, None]

    def block_diag(blocks):
        return (blocks[:, :, None, :] * eye).reshape(W_A, W_A).astype(BF)

    wr = block_diag(gw[:, :, 0:half])
    wi = block_diag(gw[:, :, half:2 * half])
    rows = [p["a_conv_w"], p["a_conv_b"][None], gb[:, 0:half].reshape(1, W_A), gb[:, half:2 * half].reshape(1, W_A),
            p["a_lam"][None], p["b_conv_b"][None], p["b_norm_g"][None], p["b_norm_b"][None], jnp.zeros((5, W_A), F32)]
    cv = jnp.concatenate(rows, 0)
    w31 = jnp.concatenate([p["b_conv_w"], jnp.zeros((1, W_B), F32)], 0)
    grp = jnp.arange(W_B) // (W_B // 8)
    gmat = ((grp[:, None] == grp[None, :]).astype(F32) / (W_B // 8)).astype(BF)
    return cv, w31, wr, wi, gmat


SUBLAYERS = ("f0", "ab", "f1", "f2", "c", "f3")


def _local_step(x, tgt, p, plan=None):
    g = {}
    saved = []
    cur = x
    wsets = dict(p["wsets"])
    ab_c = _ab_consts(p)
    c_cvec = jnp.concatenate([p["c_norm_g"][None], p["c_norm_b"][None], jnp.zeros((6, D), F32)], 0)
    c_bst = jnp.transpose(p["c_b_s"])
    c_bin = p["c_b_in"][None]
    for s_idx, wname in enumerate(SUBLAYERS):
        l, j = divmod(s_idx, 3)
        vec = _vec(p, l, j)
        tag = f"l{l}s{j}"
        if j != 1:
            names = plan["gather"].get(s_idx, []) if plan else []
            comm = [a for nm in names for a in plan["shards"][nm]]
            res, got = _ffn_fwd(cur, vec, *wsets[wname], 0.5, "ffn_fwd_" + tag, comm=comm, tgt=tgt if s_idx == 5 else None)
            nxt, fpre, jac, s_act = res[:4]
            for k, nm in enumerate(names):
                wsets[nm] = list(got[2 * k:2 * k + 2])
            saved.append((cur, fpre, jac, s_act, vec))
            if s_idx == 5:
                loss_blk = res[4]
        elif l == 0:
            w_in, w_out = wsets[wname]
            nxt, fpre, z, hs, cvs = _mixab_fwd(cur, vec, w_in, *ab_c, w_out.reshape(D, D), "mixab_fwd_" + tag)
            saved.append((cur, fpre, z, hs, cvs, vec))
        else:
            w_in, w_out = wsets[wname]
            nxt, fpre, c_saved, c_rstd, pb = _mixc_fwd(cur, vec, w_in, c_bin, c_cvec, p["c_w_s"], c_bst, w_out.reshape(D, D),
                                                       "mixc_fwd_" + tag)
            saved.append((cur, fpre, c_saved, c_rstd, pb, vec))
        cur = nxt
    dcur = cur
    accs, pending, recv = {}, {}, {}

    def take(host):
        keys = plan["scatter"].get(host, []) if plan else []
        return keys, [pending.pop(k) for k in keys]

    def put(keys, got):
        recv.update(zip(keys, got))

    for s_idx in reversed(range(6)):
        wname = SUBLAYERS[s_idx]
        l, j = divmod(s_idx, 3)
        tag = f"l{l}s{j}"
        sv = saved[s_idx]
        if j != 1:
            keys, comm = take("ffn_bwd_" + tag)
            xin, fpre, jac, s, vec = sv
            (dcur, dgu, hb, dfb, acc), got = _ffn_bwd(dcur, xin, fpre, jac, vec, *wsets[wname], 0.5, "ffn_bwd_" + tag, comm=comm)
            put(keys, got)
            keys, comm = take("wgrad_w13_" + tag)
            dw13 = _wgrad(hb, dgu, 2, D, D_FF, "share", "cols", D, BF, "wgrad_w13_" + tag, tk=TK_WGRAD, comm=comm,
                          slots=(_w13_slots, (4, D, FF_PAD)))
            if keys:
                dw13, got = dw13
                put(keys, got)
            pending[wname + ".0"] = dw13
            keys, comm = take("wgrad_w2_" + tag)
            dw2 = _wgrad(s, dfb, 1, D_FF, D, "share", "share", D_FF, BF, "wgrad_w2_" + tag, tk=TK_WGRAD, comm=comm,
                         slots=(_w2_slots, (N_DEV, W2_SHARD, D)))
            if keys:
                dw2, got = dw2
                put(keys, got)
            pending[wname + ".1"] = dw2
        elif l == 0:
            xin, fpre, z, hs, cvs, vec = sv
            w_in, w_out = wsets[wname]
            keys, comm = take("mixab_bwd_" + tag)
            (dcur, dz, yab, hb, dfb, xcb, dri, acc, accs_ab, dw31), got = _mixab_bwd(
                dcur, xin, fpre, z, cvs, hs, vec, w_in, *ab_c, w_out.reshape(D, D), "mixab_bwd_" + tag, comm=comm)
            put(keys, got)
            d_in = _wgrad(hb, dz, 1, D, 2 * D, "share", "share", D, BF, "wgrad_ab_in", tk=TK_WGRAD, col_slots=N_DEV)
            d_out = _wgrad(yab, dfb, 1, D, D, "share", "share", D, BF, "wgrad_ab_out")
            pending[wname + ".0"], pending[wname + ".1"] = d_in, d_out.reshape(N_DEV, D // N_DEV, D)
            g["gate"] = _wgrad(xcb, dri, 1, W_A, 2 * W_A, "share", "share", W_A, F32, "wgrad_gate")
            g["accs_ab"] = accs_ab
            g["dw31"] = dw31
        else:
            xin, fpre, c_saved, c_rstd, pb, vec = sv
            w_in, w_out = wsets[wname]
            dcur, dpre, hb, dfb, acc, dbin, dws, dbst = _mixc_bwd(
                dcur, xin, fpre, c_saved, c_rstd, vec, w_in, c_cvec, p["c_w_s"], c_bst, w_out.reshape(D, D), "mixc_bwd_" + tag)
            d_in = _wgrad(hb, dpre, 1, D, 2 * D, "share", "share", D, BF, "wgrad_c_in", tk=TK_WGRAD, col_slots=N_DEV)
            d_out = _wgrad(pb, dfb, 1, D, D, "share", "share", D, BF, "wgrad_c_out")
            pending[wname + ".0"], pending[wname + ".1"] = d_in, d_out.reshape(N_DEV, D // N_DEV, D)
            g["c_small"] = (acc, dbin, dws, dbst)
        accs[f"{l}{j}"] = acc
    g["accs"] = accs
    g["pending"] = pending
    g["recv"] = recv
    return loss_blk, dcur, g


def _exchange_ops(ins, outs, sems, mode, action):
    send_sems, recv_sems, loc_sems = sems
    n = len(ins)
    x, y, c = lax.axis_index("x"), lax.axis_index("y"), lax.axis_index("c")
    me = 4 * x + 2 * y + c

    def src(i, dev):
        return ins[i] if mode == "gather" else ins[i].at[dev]

    for i in range(n):
        cp = pltpu.make_async_copy(src(i, me), outs[i].at[me], loc_sems.at[i])
        if action == "start":
            cp.start()
        else:
            cp.wait()
    for mask in range(1, N_DEV):
        px = 1 - x if mask & 4 else x
        py = 1 - y if mask & 2 else y
        pc = 1 - c if mask & 1 else c
        peer = 4 * px + 2 * py + pc
        for i in range(n):
            k = i * (N_DEV - 1) + mask - 1
            cp = pltpu.make_async_remote_copy(
                src_ref=src(i, peer), dst_ref=outs[i].at[me if action == "start" else peer],
                send_sem=send_sems.at[k], recv_sem=recv_sems.at[k],
                device_id=(px, py, pc), device_id_type=pl.DeviceIdType.MESH)
            if action == "start":
                cp.start()
            else:
                cp.wait()


def _exchange_scratch(n):
    return [pltpu.SemaphoreType.DMA((n * (N_DEV - 1),)), pltpu.SemaphoreType.DMA((n * (N_DEV - 1),)),
            pltpu.SemaphoreType.DMA((n,))]


def _exchange_shapes(arrays, mode):
    return [jax.ShapeDtypeStruct(((N_DEV,) + a.shape) if mode == "gather" else a.shape, a.dtype) for a in arrays]


def _exchange(arrays, mode, name):
    n = len(arrays)

    def body(*refs):
        ins, outs, sems = refs[:n], refs[n:2 * n], refs[2 * n:]
        _exchange_ops(ins, outs, sems, mode, "start")
        _exchange_ops(ins, outs, sems, mode, "wait")

    return pl.pallas_call(
        body, in_specs=[pl.BlockSpec(memory_space=pl.ANY)] * n, out_specs=[pl.BlockSpec(memory_space=pl.ANY)] * n,
        out_shape=_exchange_shapes(arrays, mode), scratch_shapes=_exchange_scratch(n), name=name,
    )(*arrays)


def _gather_two_level(arrays, name):
    n = len(arrays)
    per = N_DEV - 1

    def body(*refs):
        ins, outs = refs[:n], refs[n:2 * n]
        send_sems, recv_sems, loc_sems = refs[2 * n:]
        x, y, c = lax.axis_index("x"), lax.axis_index("y"), lax.axis_index("c")
        me, sibling = (x, y, c), (x, y, 1 - c)
        chips = [(1 - x, y), (x, 1 - y), (1 - x, 1 - y)]

        def rows(i, dev):
            return outs[i].at[4 * dev[0] + 2 * dev[1] + dev[2]]

        def copy(i, k, block, to, src=None):
            return pltpu.make_async_remote_copy(
                src_ref=rows(i, block) if src is None else src, dst_ref=rows(i, block),
                send_sem=send_sems.at[i * per + k], recv_sem=recv_sems.at[i * per + k],
                device_id=to, device_id_type=pl.DeviceIdType.MESH)

        mine = [pltpu.make_async_copy(ins[i], rows(i, me), loc_sems.at[i]) for i in range(n)]
        for cp in mine:
            cp.start()
        first = []
        for i in range(n):
            first.append(copy(i, 0, me, sibling, src=ins[i]))
            first += [copy(i, 1 + j, me, (*chip, c), src=ins[i]) for j, chip in enumerate(chips)]
        for cp in first:
            cp.start()
        passed = []
        for j, chip in enumerate(chips):
            for i in range(n):
                copy(i, 1 + j, (*chip, c), me).wait_recv()
                fwd = copy(i, 4 + j, (*chip, c), sibling)
                fwd.start()
                passed.append(fwd)
        for i in range(n):
            copy(i, 0, sibling, me).wait_recv()
            for j, chip in enumerate(chips):
                copy(i, 4 + j, (*chip, 1 - c), me).wait_recv()
        for cp in first + passed:
            cp.wait_send()
        for cp in mine:
            cp.wait()

    return pl.pallas_call(
        body, in_specs=[pl.BlockSpec(memory_space=pl.ANY)] * n, out_specs=[pl.BlockSpec(memory_space=pl.ANY)] * n,
        out_shape=_exchange_shapes(arrays, "gather"), scratch_shapes=_exchange_scratch(n), name=name,
    )(*arrays)


def _sum_slots(a, name):
    def body(a_ref, o_ref):
        acc = a_ref[0]
        for s in range(1, N_DEV):
            acc = acc + a_ref[s]
        o_ref[...] = acc

    return pl.pallas_call(body, out_shape=jax.ShapeDtypeStruct(a.shape[1:], F32), name=name,
                          compiler_params=pltpu.CompilerParams(vmem_limit_bytes=VMEM_LIMIT))(a)


def _pack(pieces, mult):
    flat = jnp.concatenate([q.reshape(-1).astype(F32) for q in pieces])
    size = -(-flat.shape[0] // mult) * mult
    return jnp.pad(flat, (0, size - flat.shape[0])).reshape(size // 128, 128)


def _unpack(flat, shapes):
    out, off = [], 0
    for shp in shapes:
        size = math.prod(shp)
        out.append(flat[..., off:off + size].reshape(flat.shape[:-1] + tuple(shp)))
        off += size
    return out


def _mod_part(c_all, ada_w, ada_b_mine, name):
    cols = ada_w.shape[-1]

    def body(c_ref, w_ref, b_ref, o_ref):
        cv = c_ref[...]
        ca = cv * _sigmoid(cv)
        for l in range(2):
            o_ref[l] = jnp.dot(ca, w_ref[l], preferred_element_type=F32, precision=lax.Precision.HIGHEST) + b_ref[l:l + 1, :]

    return pl.pallas_call(body, out_shape=jax.ShapeDtypeStruct((2, N_DEV, cols), F32), name=name,
                          compiler_params=pltpu.CompilerParams(vmem_limit_bytes=VMEM_LIMIT))(c_all, ada_w, ada_b_mine)


def _ada_w_grad(c_all_t, dmod_mine, name):
    cols = dmod_mine.shape[-1]

    def body(ct_ref, d_ref, o_ref):
        cv = ct_ref[...]
        ca = cv * _sigmoid(cv)
        for l in range(2):
            acc = ca[:, 0:1] * d_ref[l, 0:1, :]
            for b in range(1, N_DEV):
                acc = acc + ca[:, b:b + 1] * d_ref[l, b:b + 1, :]
            o_ref[l] = acc

    return pl.pallas_call(body, out_shape=jax.ShapeDtypeStruct((2, D, cols), F32), name=name,
                          compiler_params=pltpu.CompilerParams(vmem_limit_bytes=VMEM_LIMIT))(c_all_t, dmod_mine)


def _adamw_math(w, g, m, v):
    m2 = ADAM_B1 * m + (1.0 - ADAM_B1) * g
    v2 = ADAM_B2 * v + (1.0 - ADAM_B2) * (g * g)
    m_hat = m2 / (1.0 - ADAM_B1 ** ADAM_STEP)
    v_hat = v2 / (1.0 - ADAM_B2 ** ADAM_STEP)
    delta = -ADAM_LR * (m_hat / (jnp.sqrt(v_hat) + ADAM_EPS) + ADAM_WD * w)
    return delta, m2, v2


def _adamw_big(w, g, m, v, name):
    n_l, rows, cols = w.shape
    parts = list(g) if isinstance(g, (list, tuple)) else None
    sizes = (512, 352, 256, 128, 64, 32, 16, 8) if parts is None or len(parts) == 1 else (176, 128, 64, 32, 16, 8)
    br = next(b for b in sizes if rows % b == 0)
    nr = rows // br

    def body(*refs):
        w_ref, g_refs, (m_ref, v_ref, go_ref, d_ref, mo_ref, vo_ref) = refs[0], refs[1:-6], refs[-6:]

        def update(gsum):
            go_ref[...] = gsum
            d_ref[...], mo_ref[...], vo_ref[...] = _adamw_math(w_ref[...], gsum, m_ref[...], v_ref[...])

        if parts is None:
            update(g_refs[0][...])
        else:
            for l, g_ref in enumerate(g_refs):
                @pl.when(pl.program_id(0) == l)
                def _(g_ref=g_ref):
                    gsum = g_ref[0, :, 0:cols].astype(F32)
                    for s in range(1, N_DEV):
                        gsum = gsum + g_ref[s, :, 0:cols].astype(F32)
                    update(gsum)

    blk = pl.BlockSpec((None, br, cols), lambda l, i: (l, i, 0))
    if parts is None:
        g_specs, g_args = [blk], [g]
    else:
        def part_spec(l_mine, width):
            return pl.BlockSpec((N_DEV, br, width),
                                lambda l, i: (0, jnp.where(l < l_mine, 0, jnp.where(l == l_mine, i, nr - 1)), 0))
        g_specs, g_args = [part_spec(l, p.shape[-1]) for l, p in enumerate(parts)], parts
    shp = jax.ShapeDtypeStruct((n_l, rows, cols), F32)
    return pl.pallas_call(
        body, grid=(n_l, nr), in_specs=[blk] + g_specs + [blk, blk], out_specs=[blk] * 4, out_shape=[shp] * 4,
        compiler_params=pltpu.CompilerParams(dimension_semantics=("arbitrary", "arbitrary"), vmem_limit_bytes=VMEM_LIMIT),
        name=name)(w, *g_args, m, v)


def _adamw_small(ws, gs, ms, vs, name):
    n = len(ws)

    def body(*refs):
        for i in range(n):
            w_ref, g_ref, m_ref, v_ref = (refs[k * n + i] for k in range(4))
            d_ref, mo_ref, vo_ref = (refs[(4 + k) * n + i] for k in range(3))
            d_ref[...], mo_ref[...], vo_ref[...] = _adamw_math(w_ref[...], g_ref[...], m_ref[...], v_ref[...])

    shapes = [jax.ShapeDtypeStruct(w.shape, F32) for w in ws]
    outs = pl.pallas_call(body, out_shape=shapes * 3, name=name,
                          compiler_params=pltpu.CompilerParams(vmem_limit_bytes=VMEM_LIMIT))(*ws, *gs, *ms, *vs)
    return outs[:n], outs[n:2 * n], outs[2 * n:]


def _as2d(a):
    return a.reshape(-1, a.shape[-1])


def kernel(x, c, ada_w, ada_b, norm_pre, norm_post, ffn_w13, ffn_w2, ab_w_in, a_conv_w, a_conv_b, a_gate_w, a_gate_b, a_lam, b_conv_w, b_conv_b, b_norm_g, b_norm_b, ab_w_out, c_w_in, c_b_in, c_norm_g, c_norm_b, c_w_s, c_b_s, c_w_out, loss_target, m_ada_w, m_ada_b, m_norm_pre, m_norm_post, m_ffn_w13, m_ffn_w2, m_ab_w_in, m_a_conv_w, m_a_conv_b, m_a_gate_w, m_a_gate_b, m_a_lam, m_b_conv_w, m_b_conv_b, m_b_norm_g, m_b_norm_b, m_ab_w_out, m_c_w_in, m_c_b_in, m_c_norm_g, m_c_norm_b, m_c_w_s, m_c_b_s, m_c_w_out, v_ada_w, v_ada_b, v_norm_pre, v_norm_post, v_ffn_w13, v_ffn_w2, v_ab_w_in, v_a_conv_w, v_a_conv_b, v_a_gate_w, v_a_gate_b, v_a_lam, v_b_conv_w, v_b_conv_b, v_b_norm_g, v_b_norm_b, v_ab_w_out, v_c_w_in, v_c_b_in, v_c_norm_g, v_c_norm_b, v_c_w_s, v_c_b_s, v_c_w_out):
    me = 4 * lax.axis_index("x") + 2 * lax.axis_index("y") + lax.axis_index("c")
    weights = dict(ada_w=ada_w, ada_b=ada_b, norm_pre=norm_pre, norm_post=norm_post, ffn_w13=ffn_w13, ffn_w2=ffn_w2,
                   ab_w_in=ab_w_in, a_conv_w=a_conv_w, a_conv_b=a_conv_b, a_gate_w=a_gate_w, a_gate_b=a_gate_b, a_lam=a_lam,
                   b_conv_w=b_conv_w, b_conv_b=b_conv_b, b_norm_g=b_norm_g, b_norm_b=b_norm_b, ab_w_out=ab_w_out,
                   c_w_in=c_w_in, c_b_in=c_b_in, c_norm_g=c_norm_g, c_norm_b=c_norm_b, c_w_s=c_w_s, c_b_s=c_b_s, c_w_out=c_w_out)
    moms = dict(ada_w=m_ada_w, ada_b=m_ada_b, norm_pre=m_norm_pre, norm_post=m_norm_post, ffn_w13=m_ffn_w13, ffn_w2=m_ffn_w2,
                ab_w_in=m_ab_w_in, a_conv_w=m_a_conv_w, a_conv_b=m_a_conv_b, a_gate_w=m_a_gate_w, a_gate_b=m_a_gate_b,
                a_lam=m_a_lam, b_conv_w=m_b_conv_w, b_conv_b=m_b_conv_b, b_norm_g=m_b_norm_g, b_norm_b=m_b_norm_b,
                ab_w_out=m_ab_w_out, c_w_in=m_c_w_in, c_b_in=m_c_b_in, c_norm_g=m_c_norm_g, c_norm_b=m_c_norm_b,
                c_w_s=m_c_w_s, c_b_s=m_c_b_s, c_w_out=m_c_w_out)
    vars_ = dict(ada_w=v_ada_w, ada_b=v_ada_b, norm_pre=v_norm_pre, norm_post=v_norm_post, ffn_w13=v_ffn_w13, ffn_w2=v_ffn_w2,
                 ab_w_in=v_ab_w_in, a_conv_w=v_a_conv_w, a_conv_b=v_a_conv_b, a_gate_w=v_a_gate_w, a_gate_b=v_a_gate_b,
                 a_lam=v_a_lam, b_conv_w=v_b_conv_w, b_conv_b=v_b_conv_b, b_norm_g=v_b_norm_g, b_norm_b=v_b_norm_b,
                 ab_w_out=v_ab_w_out, c_w_in=v_c_w_in, c_b_in=v_c_b_in, c_norm_g=v_c_norm_g, c_norm_b=v_c_norm_b,
                 c_w_s=v_c_w_s, c_b_s=v_c_b_s, c_w_out=v_c_w_out)
    names = list(weights)

    w13s = ffn_w13.astype(BF).reshape(4, D, FF_SHARD)
    tail, blank = w13s[..., FF_MAIN:], jnp.zeros((4, D, FF_TAIL), BF)
    tail_tile = jnp.where(me % 2 == 1, jnp.concatenate([blank, tail], -1), jnp.concatenate([tail, blank], -1))
    w13b = jnp.concatenate([w13s[..., :FF_MAIN], tail_tile], -1)
    small_shapes = [(D,), (2, 3, 128), (2, 3, 128), (CONV_A, 64), (CONV_B, 64), (256,), (128,), (128,)]
    small = _pack([c, norm_pre, norm_post, a_conv_w, b_conv_w, c_b_in, c_norm_g, c_norm_b], 1024)
    w2b = ffn_w2.astype(BF).reshape(4, W2_SHARD, D)
    shards = {f"f{f}": [w13b[f], w2b[f]] for f in range(4)}
    shards["ab"] = [ab_w_in[0].astype(BF), ab_w_out[0].astype(BF)]
    shards["c"] = [c_w_in[0].astype(BF), c_w_out[0].astype(BF)]
    w13g0, w2g0, small_g = _gather_two_level(shards["f0"] + [small], "gather_first")
    plan = dict(shards=shards, gather={0: ["ab", "f1"], 2: ["f2"], 3: ["c", "f3"]},
                scatter={"ffn_bwd_l1s0": ["f3.0", "f3.1", "c.0", "c.1"], "ffn_bwd_l0s2": ["f2.0", "f2.1"],
                         "mixab_bwd_l0s1": ["f1.0", "f1.1"], "wgrad_w13_l0s0": ["ab.0", "ab.1"], "wgrad_w2_l0s0": ["f0.0"]})
    c_all, npre_g, npost_g, acw_g, bcw_g, cbin_g, cng_g, cnb_g = _unpack(small_g.reshape(N_DEV, -1), small_shapes)

    def cat_last(a):
        return jnp.moveaxis(a, 0, -2).reshape(a.shape[1:-1] + (N_DEV * a.shape[-1],))

    ada_b_mine = lax.dynamic_slice_in_dim(ada_b, me * ada_w.shape[-1], ada_w.shape[-1], axis=1)
    (mod_g,) = _exchange([_mod_part(c_all, ada_w, ada_b_mine, "mod_part")], "gather", "gather_mod")
    mod = cat_last(lax.dynamic_index_in_dim(mod_g, me, axis=2, keepdims=False)).reshape(2, 3, 3, D)

    p = dict(mod=mod, norm_pre=cat_last(npre_g), norm_post=cat_last(npost_g), wsets={"f0": [w13g0, w2g0]},
             a_conv_w=cat_last(acw_g), a_conv_b=a_conv_b[0], a_gate_w=a_gate_w[0], a_gate_b=a_gate_b[0], a_lam=a_lam[0],
             b_conv_w=cat_last(bcw_g), b_conv_b=b_conv_b[0], b_norm_g=b_norm_g[0], b_norm_b=b_norm_b[0],
             c_b_in=cat_last(cbin_g), c_norm_g=cat_last(cng_g), c_norm_b=cat_last(cnb_g), c_w_s=c_w_s[0], c_b_s=c_b_s[0])

    loss_blk, grad_x, g = _local_step(x[0], loss_target[0], p, plan)
    loss = lax.psum(loss_blk[0, 0], ("x", "y", "c"))

    accs = g["accs"]
    dmod = jnp.stack([jnp.stack([accs[f"{l}{j}"][0:3] for j in range(3)]) for l in range(2)])
    dnpre = jnp.stack([jnp.stack([accs[f"{l}{j}"][3] for j in range(3)]) for l in range(2)])
    dnpost = jnp.stack([jnp.stack([accs[f"{l}{j}"][4] for j in range(3)]) for l in range(2)])
    sab = g["accs_ab"]
    half = W_A // 8
    dgate = g["gate"][0]
    dgw = jnp.stack([jnp.concatenate([dgate[half * hh:half * (hh + 1), half * hh:half * (hh + 1)],
                                      dgate[half * hh:half * (hh + 1), W_A + half * hh:W_A + half * (hh + 1)]], axis=1)
                     for hh in range(8)])
    dgb = jnp.concatenate([sab[5].reshape(8, half), sab[6].reshape(8, half)], axis=1)
    c_acc, c_dbin, c_dws, c_dbst = g["c_small"]
    red_shapes = [(2, 9216), (2, 3, D), (2, 3, D), (CONV_A, W_A), (W_A,), (8, half, 2 * half), (8, 2 * half), (W_A,),
                  (CONV_B, W_B), (W_B,), (W_B,), (W_B,), (2 * D,), (D,), (D,), (H_C, CHUNK, CHUNK), (H_C, CHUNK)]
    red = _pack([dmod.reshape(2, 9216), dnpre, dnpost, sab[0:4], sab[4], dgw, dgb, sab[7], g["dw31"][0:CONV_B], sab[8],
                 sab[9], sab[10], c_dbin[0], c_acc[5], c_acc[6], c_dws, jnp.transpose(c_dbst)], N_DEV * 1024)
    left = sorted(g["pending"])
    red_r, *last_recv = _exchange([red.reshape(N_DEV, -1, 128)] + [g["pending"][k] for k in left], "scatter",
                                  "scatter_small_grads")
    red_all, dmod_all = _exchange([_sum_slots(red_r, "sum_small_grads"), dmod.reshape(-1, 128)], "gather", "gather_small_grads")
    red_sum = red_all.reshape(-1)
    (g_ada_b, g_npre, g_npost, g_acw, g_acb, g_agw, g_agb, g_alam, g_bcw, g_bcb, g_bng, g_bnb, g_cbin, g_cng, g_cnb,
     g_cws, g_cbs) = _unpack(red_sum, red_shapes)
    dmod_all = dmod_all.reshape(N_DEV, 2, 9216)
    ncol = ada_w.shape[-1]
    dmod_mine = jnp.moveaxis(lax.dynamic_slice_in_dim(dmod_all, me * ncol, ncol, axis=2), 0, 1)
    g_ada_w = _ada_w_grad(jnp.transpose(c_all), dmod_mine, "ada_w_grad")

    def mine(a, width):
        return lax.dynamic_slice_in_dim(a, me * width, width, axis=a.ndim - 1)

    small_grads = dict(
        ada_b=g_ada_b, norm_pre=mine(g_npre, 128), norm_post=mine(g_npost, 128), a_conv_w=mine(g_acw, 64)[None],
        a_conv_b=g_acb[None], a_gate_w=g_agw[None], a_gate_b=g_agb[None], a_lam=g_alam[None], b_conv_w=mine(g_bcw, 64)[None],
        b_conv_b=g_bcb[None], b_norm_g=g_bng[None], b_norm_b=g_bnb[None], c_b_in=mine(g_cbin, 256)[None],
        c_norm_g=mine(g_cng, 128)[None], c_norm_b=mine(g_cnb, 128)[None], c_w_s=g_cws[None], c_b_s=g_cbs[None])

    recv = dict(g["recv"])
    recv.update(zip(left, last_recv))
    big_partials = dict(ffn_w13=[recv[f"f{f}.0"] for f in range(4)], ffn_w2=[recv[f"f{f}.1"] for f in range(4)],
                        ab_w_in=[recv["ab.0"]], ab_w_out=[recv["ab.1"]], c_w_in=[recv["c.0"]], c_w_out=[recv["c.1"]],
                        ada_w=g_ada_w)

    grads, deltas, new_m, new_v = {}, {}, {}, {}

    def as3d(a):
        return a.reshape((-1,) + a.shape[-2:])

    for nm, gp in big_partials.items():
        shp = weights[nm].shape
        go, dl, mo, vo = _adamw_big(as3d(weights[nm]), gp, as3d(moms[nm]), as3d(vars_[nm]), "adamw_" + nm)
        grads[nm], deltas[nm], new_m[nm], new_v[nm] = (a.reshape(shp) for a in (go, dl, mo, vo))
    snames = list(small_grads)
    dls, mos, vos = _adamw_small([_as2d(weights[nm]) for nm in snames], [_as2d(small_grads[nm]) for nm in snames],
                                 [_as2d(moms[nm]) for nm in snames], [_as2d(vars_[nm]) for nm in snames], "adamw_small")
    for k, nm in enumerate(snames):
        shp = weights[nm].shape
        grads[nm] = small_grads[nm].reshape(shp)
        deltas[nm], new_m[nm], new_v[nm] = dls[k].reshape(shp), mos[k].reshape(shp), vos[k].reshape(shp)

    return (loss, grad_x[None], *[grads[nm] for nm in names], *[deltas[nm] for nm in names],
            *[new_m[nm] for nm in names], *[new_v[nm] for nm in names])
```

```python
import math

import jax
import jax.numpy as jnp
from jax import lax
from jax.experimental import pallas as pl
from jax.experimental.pallas import tpu as pltpu

F32 = jnp.float32
BF = jnp.bfloat16

N_DEV = 8
D = 1024
EPS = 1e-6
D_FF = 2816
FF_SHARD = 704
FF_PAD = 768
FF_MAIN = 640
FF_TAIL = FF_SHARD - FF_MAIN
W2_SHARD = 352
W_A = 512
W_B = 512
AB_SAVED = 9 * W_A
CONV_A = 4
CONV_B = 31
HALO_A = 8
HALO_B = 32
LRU_C = 8.0
CHUNK = 128
H_C = 8
ADAM_LR = 0.001
ADAM_B1 = 0.9
ADAM_B2 = 0.999
ADAM_EPS = 1e-08
ADAM_WD = 0.01
ADAM_STEP = 10
VMEM_LIMIT = 62 * 1024 * 1024
GELU_C = math.sqrt(2.0 / math.pi)

TM_FFN = 512
TM_FFN_BWD = 256
TM_MIX = 256
TM_MIXC_FWD = 512
TK_WGRAD = 2048


def _params(limit=VMEM_LIMIT):
    return pltpu.CompilerParams(dimension_semantics=("arbitrary",), vmem_limit_bytes=limit)


def _dot(a, b):
    return jnp.dot(a, b, preferred_element_type=F32)


def _dot_nt(a, b):
    return lax.dot_general(a, b, (((1,), (1,)), ((), ())), preferred_element_type=F32)


def _dot_tn(a, b):
    return lax.dot_general(a, b, (((0,), (0,)), ((), ())), preferred_element_type=F32)


def _sigmoid(x):
    return 0.5 + 0.5 * jnp.tanh(0.5 * x)


def _gelu(x):
    t = jnp.tanh(GELU_C * (x + 0.044715 * x * x * x))
    return 0.5 * x * (1.0 + t), t


def _gelu_grad(x, t):
    return 0.5 * (1.0 + t) + 0.5 * x * (1.0 - t * t) * GELU_C * (1.0 + 3.0 * 0.044715 * x * x)


def _rms(x):
    r = lax.rsqrt(jnp.mean(x * x, axis=-1, keepdims=True) + EPS)
    return x * r, r


def _colsum(x):
    return jnp.sum(x, axis=0, keepdims=True)


def _pre_fwd(x, vec_ref):
    xn, r = _rms(x)
    n = xn * vec_ref[3:4, :]
    h = n * (1.0 + vec_ref[1:2, :]) + vec_ref[0:1, :]
    return h, xn, r, n


def _post_fwd(x, f, vec_ref, res_w):
    fn, _ = _rms(f)
    return x + fn * ((res_w * (1.0 + vec_ref[2:3, :])) * vec_ref[4:5, :])


def _post_bwd(dout, f, vec_ref, acc_ref, res_w):
    fn, r2 = _rms(f)
    acc_ref[2:3, :] += _colsum(dout * fn)
    dfn = dout * ((res_w * (1.0 + vec_ref[2:3, :])) * vec_ref[4:5, :])
    return r2 * (dfn - fn * jnp.mean(dfn * fn, axis=-1, keepdims=True))


def _pre_bwd(dout, dh, xn, r, vec_ref, acc_ref):
    acc_ref[0:1, :] += _colsum(dh)
    acc_ref[1:2, :] += _colsum(dh * xn)
    dxn = dh * ((1.0 + vec_ref[1:2, :]) * vec_ref[3:4, :])
    return dout + r * (dxn - xn * jnp.mean(dxn * xn, axis=-1, keepdims=True))


def _finish_acc(acc_ref, vec_ref, res_w):
    s_pre, s_post = acc_ref[1:2, :], acc_ref[2:3, :]
    acc_ref[1:2, :] = vec_ref[3:4, :] * s_pre
    acc_ref[3:4, :] = (1.0 + vec_ref[1:2, :]) * s_pre
    acc_ref[2:3, :] = (res_w * vec_ref[4:5, :]) * s_post
    acc_ref[4:5, :] = (res_w * (1.0 + vec_ref[2:3, :])) * s_post


def _tile(tm, ncol):
    return pl.BlockSpec((tm, ncol), lambda i: (i, 0))


def _full(shape):
    return pl.BlockSpec(shape, lambda i: (0,) * len(shape))


def _any():
    return pl.BlockSpec(memory_space=pl.ANY)


def _load_ffn_weights(w13_hbm, w2_hbm, w13_v, w2_v, tails, sems):
    copies = []
    for j in range(N_DEV):
        half, k = divmod(j, 4)
        copies.append((w13_hbm.at[j, :, pl.ds(0, FF_MAIN)], w13_v.at[:, pl.ds(D_FF * half + FF_MAIN * k, FF_MAIN)]))
        copies.append((w13_hbm.at[j, :, pl.ds(FF_MAIN, 128)], tails.at[j]))
    for k in range(4):
        copies.append((w2_hbm.at[2 * k], w2_v.at[pl.ds(FF_MAIN * k, W2_SHARD), :]))
        copies.append((w2_hbm.at[2 * k + 1, pl.ds(0, FF_MAIN - W2_SHARD), :],
                       w2_v.at[pl.ds(FF_MAIN * k + W2_SHARD, FF_MAIN - W2_SHARD), :]))
        copies.append((w2_hbm.at[2 * k + 1, pl.ds(FF_MAIN - W2_SHARD, FF_TAIL), :],
                       w2_v.at[pl.ds(4 * FF_MAIN + FF_TAIL * k, FF_TAIL), :]))
    copies = [pltpu.make_async_copy(src, dst, sems.at[n]) for n, (src, dst) in enumerate(copies)]
    for cp in copies:
        cp.start()
    for cp in copies:
        cp.wait()
    for pair in range(4):
        half, kk = divmod(pair, 2)
        base = D_FF * half + 4 * FF_MAIN + 128 * kk
        w13_v[:, base:base + 128] = tails[2 * pair] + tails[2 * pair + 1]


_FFN_SCRATCH = [pltpu.VMEM((D, 2 * D_FF), BF), pltpu.VMEM((D_FF, D), BF), pltpu.VMEM((N_DEV, D, 128), BF),
                pltpu.SemaphoreType.DMA((2 * N_DEV + 12,))]
HID_CHUNKS = ((0, 768), (768, 768), (1536, 768), (2304, 512))


def _hosted(body, n_in, n_out, n_scratch, n_comm, mode, grid):
    if not n_comm:
        return body

    def at(corner):
        hit = pl.program_id(0) == corner[0]
        for d in range(1, len(grid)):
            hit = hit & (pl.program_id(d) == corner[d])
        return hit

    def hosted(*refs):
        ins, cin = refs[:n_in], refs[n_in:n_in + n_comm]
        outs, cout = refs[n_in + n_comm:n_in + n_comm + n_out], refs[n_in + n_comm + n_out:n_in + 2 * n_comm + n_out]
        scratch = refs[n_in + 2 * n_comm + n_out:]
        own, sems = scratch[:n_scratch], scratch[n_scratch:]

        @pl.when(at([0] * len(grid)))
        def _():
            _exchange_ops(cin, cout, sems, mode, "start")

        body(*ins, *outs, *own)

        @pl.when(at([n - 1 for n in grid]))
        def _():
            _exchange_ops(cin, cout, sems, mode, "wait")

    return hosted


def _ffn_fwd(x, vec, w13g, w2g, res_w, name, comm=(), tgt=None):
    t_len = x.shape[0]
    tm = min(TM_FFN, t_len)
    nc = len(comm)
    head = tgt is not None

    def body(*refs):
        if head:
            x_ref, vec_ref, w13_hbm, w2_hbm, t_ref, xo_ref, f_ref, jac_ref, s_ref, loss_ref, w13_v, w2_v, tails, sems = refs
        else:
            x_ref, vec_ref, w13_hbm, w2_hbm, xo_ref, f_ref, jac_ref, s_ref, w13_v, w2_v, tails, sems = refs

        @pl.when(pl.program_id(0) == 0)
        def _():
            _load_ffn_weights(w13_hbm, w2_hbm, w13_v, w2_v, tails, sems)
            if head:
                loss_ref[...] = jnp.zeros((8, 128), F32)

        x_t = x_ref[...]
        h, _, _, _ = _pre_fwd(x_t, vec_ref)
        hb = h.astype(BF)
        for c0, cw in HID_CHUNKS:
            g = _dot(hb, w13_v[:, c0:c0 + cw])
            u = _dot(hb, w13_v[:, D_FF + c0:D_FF + c0 + cw])
            sig = _sigmoid(g)
            sl = g * sig
            jac_ref[0, :, c0:c0 + cw] = (u * (sig + sl * (1.0 - sig))).astype(BF)
            jac_ref[1, :, c0:c0 + cw] = sl.astype(BF)
            s_ref[:, c0:c0 + cw] = (sl * u).astype(BF)
        acc = _dot(s_ref[...], w2_v[...])
        f_ref[...] = acc
        xo = _post_fwd(x_t, acc, vec_ref, res_w)
        if head:
            err = xo - t_ref[...]
            xo_ref[...] = err * (1.0 / D)
            loss_ref[...] += jnp.sum(err * err) * (0.5 / D)
        else:
            xo_ref[...] = xo

    nt = t_len // tm
    n_in, n_out = (5, 5) if head else (4, 4)
    outs = pl.pallas_call(
        _hosted(body, n_in, n_out, 4, nc, "gather", (nt,)), grid=(nt,),
        in_specs=[_tile(tm, D), _full((8, D)), _any(), _any()] + ([_tile(tm, D)] if head else []) + [_any()] * nc,
        out_specs=[_tile(tm, D), _tile(tm, D), pl.BlockSpec((2, tm, D_FF), lambda i: (0, i, 0)), _tile(tm, D_FF)]
        + ([_full((8, 128))] if head else []) + [_any()] * nc,
        out_shape=[jax.ShapeDtypeStruct((t_len, D), F32), jax.ShapeDtypeStruct((t_len, D), F32),
                   jax.ShapeDtypeStruct((2, t_len, D_FF), BF), jax.ShapeDtypeStruct((t_len, D_FF), BF)]
        + ([jax.ShapeDtypeStruct((8, 128), F32)] if head else []) + _exchange_shapes(comm, "gather"),
        scratch_shapes=_FFN_SCRATCH + (_exchange_scratch(nc) if nc else []), compiler_params=_params(), name=name,
    )(x, vec, w13g, w2g, *([tgt] if head else []), *comm)
    return outs[:n_out], outs[n_out:]


def _ffn_bwd(dout, x, fpre, jac, vec, w13g, w2g, res_w, name, comm=()):
    t_len = x.shape[0]
    tm = min(TM_FFN_BWD, t_len)
    nt = t_len // tm
    nc = len(comm)

    def body(dout_ref, x_ref, f_ref, jac_ref, vec_ref, w13_hbm, w2_hbm,
             dx_ref, dgu_ref, hb_ref, dfb_ref, acc_ref, w13_v, w2_v, tails, sems):
        @pl.when(pl.program_id(0) == 0)
        def _():
            _load_ffn_weights(w13_hbm, w2_hbm, w13_v, w2_v, tails, sems)
            acc_ref[...] = jnp.zeros((8, D), F32)

        dout_t = dout_ref[...]
        df = _post_bwd(dout_t, f_ref[...], vec_ref, acc_ref, res_w)
        dfb = df.astype(BF)
        dfb_ref[...] = dfb
        h, xn, r, _ = _pre_fwd(x_ref[...], vec_ref)
        hb_ref[...] = h.astype(BF)
        for c0, cw in HID_CHUNKS:
            ds = _dot_nt(dfb, w2_v[c0:c0 + cw, :]).astype(BF)
            dgu_ref[:, c0:c0 + cw] = ds * jac_ref[0, :, c0:c0 + cw]
            dgu_ref[:, D_FF + c0:D_FF + c0 + cw] = ds * jac_ref[1, :, c0:c0 + cw]
        dh = _dot_nt(dgu_ref[...], w13_v[...])
        dx_ref[...] = _pre_bwd(dout_t, dh, xn, r, vec_ref, acc_ref)

        @pl.when(pl.program_id(0) == nt - 1)
        def _():
            _finish_acc(acc_ref, vec_ref, res_w)

    jac_spec = pl.BlockSpec((2, tm, D_FF), lambda i: (0, i, 0))
    outs = pl.pallas_call(
        _hosted(body, 7, 5, 4, nc, "scatter", (nt,)), grid=(nt,),
        in_specs=[_tile(tm, D), _tile(tm, D), _tile(tm, D), jac_spec, _full((8, D)), _any(), _any()] + [_any()] * nc,
        out_specs=[_tile(tm, D), _tile(tm, 2 * D_FF), _tile(tm, D), _tile(tm, D), _full((8, D))] + [_any()] * nc,
        out_shape=[jax.ShapeDtypeStruct((t_len, D), F32), jax.ShapeDtypeStruct((t_len, 2 * D_FF), BF),
                   jax.ShapeDtypeStruct((t_len, D), BF),
                   jax.ShapeDtypeStruct((t_len, D), BF), jax.ShapeDtypeStruct((8, D), F32)] + _exchange_shapes(comm, "scatter"),
        scratch_shapes=_FFN_SCRATCH + (_exchange_scratch(nc) if nc else []), compiler_params=_params(), name=name,
    )(dout, x, fpre, jac, vec, w13g, w2g, *comm)
    return outs[:5], outs[5:]


def _w13_slots(acc, o_ref):
    for k in range(4):
        o_ref[k, :, 0:FF_MAIN] = acc[:, FF_MAIN * k:FF_MAIN * (k + 1)].astype(BF)
        pair_tile = acc[:, 4 * FF_MAIN + 128 * (k // 2):4 * FF_MAIN + 128 * (k // 2 + 1)]
        o_ref[k, :, FF_MAIN:FF_PAD] = (pair_tile if k % 2 == 0 else pltpu.roll(pair_tile, FF_TAIL, 1)).astype(BF)


def _w2_slots(acc, o_ref):
    rest = FF_MAIN - W2_SHARD
    for k in range(4):
        o_ref[2 * k] = acc[FF_MAIN * k:FF_MAIN * k + W2_SHARD, :].astype(BF)
        o_ref[2 * k + 1, 0:rest, :] = acc[FF_MAIN * k + W2_SHARD:FF_MAIN * (k + 1), :].astype(BF)
        o_ref[2 * k + 1, rest:W2_SHARD, :] = acc[4 * FF_MAIN + FF_TAIL * k:4 * FF_MAIN + FF_TAIL * (k + 1), :].astype(BF)


def _wgrad(a, b, j_count, m, n, a_mode, b_mode, out_rows, out_dtype, name, tk=TK_WGRAD, col_slots=1, comm=(),
           slots=None):
    t_len = a.shape[-2]
    tk = min(tk, t_len)
    nk = t_len // tk
    wn = n // col_slots
    nc = len(comm)

    def spec(mode, width):
        if mode == "stack":
            return pl.BlockSpec((None, tk, width), lambda j, t: (j, t, 0))
        if mode == "cols":
            return pl.BlockSpec((tk, width), lambda j, t: (t, j))
        return pl.BlockSpec((tk, width), lambda j, t: (t, 0))

    def body(a_ref, b_ref, o_ref, acc):
        t = pl.program_id(1)

        @pl.when(t == 0)
        def _():
            acc[...] = jnp.zeros((m, n), F32)

        acc[...] += _dot_tn(a_ref[...], b_ref[...])

        @pl.when(t == nk - 1)
        def _():
            if slots is not None:
                slots[0](acc, o_ref)
            elif col_slots == 1:
                o_ref[...] = acc[0:out_rows, :].astype(out_dtype)
            else:
                for s in range(col_slots):
                    o_ref[s] = acc[0:out_rows, wn * s:wn * (s + 1)].astype(out_dtype)

    if slots is not None:
        blk = slots[1]
        out_spec = pl.BlockSpec(blk, lambda j, t: (j,) + (0,) * (len(blk) - 1))
        out_shape = jax.ShapeDtypeStruct((j_count * blk[0],) + blk[1:], BF)
    elif col_slots == 1:
        out_spec = pl.BlockSpec((None, out_rows, n), lambda j, t: (j, 0, 0))
        out_shape = jax.ShapeDtypeStruct((j_count, out_rows, n), out_dtype)
    else:
        out_spec = pl.BlockSpec((col_slots, out_rows, wn), lambda j, t: (0, 0, 0))
        out_shape = jax.ShapeDtypeStruct((col_slots, out_rows, wn), out_dtype)
    outs = pl.pallas_call(
        _hosted(body, 2, 1, 1, nc, "scatter", (j_count, nk)), grid=(j_count, nk),
        in_specs=[spec(a_mode, m), spec(b_mode, n)] + [_any()] * nc,
        out_specs=[out_spec] + [_any()] * nc, out_shape=[out_shape] + _exchange_shapes(comm, "scatter"),
        scratch_shapes=[pltpu.VMEM((m, n), F32)] + (_exchange_scratch(nc) if nc else []),
        compiler_params=pltpu.CompilerParams(dimension_semantics=("arbitrary", "arbitrary"), vmem_limit_bytes=VMEM_LIMIT),
        name=name,
    )(a, b, *comm)
    return (outs[0], list(outs[1:])) if nc else outs[0]


def _c_mask_weights(ws_ref, wsm, wsmt):
    row = lax.broadcasted_iota(jnp.int32, (CHUNK, CHUNK), 0)
    col = lax.broadcasted_iota(jnp.int32, (CHUNK, CHUNK), 1)
    for hh in range(H_C):
        w = jnp.where(row >= col, ws_ref[hh], 0.0)
        wsm[hh] = w.astype(BF)
        if wsmt is not None:
            wsmt[hh] = w.T.astype(BF)


def _c_inner(pre, cvec_ref, wsm, bst_ref, mix_sc, tm):
    z, t = _gelu(pre)
    u = z[:, 0:D]
    v = z[:, D:2 * D]
    mu = jnp.mean(v, axis=-1, keepdims=True)
    vc = v - mu
    rstd = lax.rsqrt(jnp.mean(vc * vc, axis=-1, keepdims=True) + EPS)
    vhat = vc * rstd
    vnb = (vhat * cvec_ref[0:1, :] + cvec_ref[1:2, :]).astype(BF)
    for nn in range(tm // CHUNK):
        for hh in range(H_C):
            rows = slice(CHUNK * nn, CHUNK * (nn + 1))
            cols = slice(CHUNK * hh, CHUNK * (hh + 1))
            mix_sc[rows, cols] = _dot(wsm[hh], vnb[rows, cols]) + bst_ref[:, hh:hh + 1]
    return u, t, rstd, vhat, vnb


C_SAVED = 6 * D


def _mixc_fwd(x, vec, w_in, b_in, cvec, ws, bst, w_out, name):
    t_len = x.shape[0]
    tm = min(TM_MIXC_FWD, t_len)

    def body(x_ref, vec_ref, win_ref, bin_ref, cvec_ref, ws_ref, bst_ref, wout_ref,
             xo_ref, f_ref, sv_ref, rstd_ref, p_ref, wsm, mix_sc, win_v):
        @pl.when(pl.program_id(0) == 0)
        def _():
            _c_mask_weights(ws_ref, wsm, None)
            for j in range(N_DEV):
                win_v[:, 256 * j:256 * (j + 1)] = win_ref[j]

        x_t = x_ref[...]
        h, _, _, _ = _pre_fwd(x_t, vec_ref)
        hb = h.astype(BF)
        pre = _dot(hb, win_v[...]) + bin_ref[...]
        u, t, rstd, vhat, vnb = _c_inner(pre, cvec_ref, wsm, bst_ref, mix_sc, tm)
        mix = mix_sc[...]
        pb = (u * mix).astype(BF)
        gg = _gelu_grad(pre, t)
        sv_ref[:, 0:D] = u.astype(BF)
        sv_ref[:, D:2 * D] = mix.astype(BF)
        sv_ref[:, 2 * D:4 * D] = gg.astype(BF)
        sv_ref[:, 4 * D:5 * D] = vhat.astype(BF)
        sv_ref[:, 5 * D:6 * D] = vnb
        rstd_ref[...] = jnp.broadcast_to(rstd, (tm, 128))
        p_ref[...] = pb
        fpre = _dot(pb, wout_ref[...])
        f_ref[...] = fpre
        xo_ref[...] = _post_fwd(x_t, fpre, vec_ref, 1.0)

    return pl.pallas_call(
        body, grid=(t_len // tm,),
        in_specs=[_tile(tm, D), _full((8, D)), _full((N_DEV, D, 256)), _full((1, 2 * D)), _full((8, D)),
                  _full((H_C, CHUNK, CHUNK)), _full((CHUNK, H_C)), _full((D, D))],
        out_specs=[_tile(tm, D), _tile(tm, D), _tile(tm, C_SAVED), _tile(tm, 128), _tile(tm, D)],
        out_shape=[jax.ShapeDtypeStruct((t_len, D), F32), jax.ShapeDtypeStruct((t_len, D), F32),
                   jax.ShapeDtypeStruct((t_len, C_SAVED), BF), jax.ShapeDtypeStruct((t_len, 128), F32),
                   jax.ShapeDtypeStruct((t_len, D), BF)],
        scratch_shapes=[pltpu.VMEM((H_C, CHUNK, CHUNK), BF), pltpu.VMEM((tm, D), F32), pltpu.VMEM((D, 2 * D), BF)],
        compiler_params=_params(), name=name,
    )(x, vec, w_in, b_in, cvec, ws, bst, w_out)


def _mixc_bwd(dout, x, fpre, saved, rstd_b, vec, w_in, cvec, ws, bst, w_out, name):
    t_len = x.shape[0]
    tm = min(TM_MIX, t_len)
    nt = t_len // tm

    def body(dout_ref, x_ref, f_ref, sv_ref, rstd_ref, vec_ref, win_ref, cvec_ref, ws_ref, bst_ref, wout_ref,
             dx_ref, dpre_ref, hb_ref, dfb_ref, acc_ref, dbin_ref, dws_ref, dbst_ref,
             wsm, wsmt, dvn_sc, dmsum, win_v):
        i = pl.program_id(0)

        @pl.when(i == 0)
        def _():
            _c_mask_weights(ws_ref, wsm, wsmt)
            for j in range(N_DEV):
                win_v[:, 256 * j:256 * (j + 1)] = win_ref[j]
            acc_ref[...] = jnp.zeros((8, D), F32)
            dbin_ref[...] = jnp.zeros((8, 2 * D), F32)
            dws_ref[...] = jnp.zeros((H_C, CHUNK, CHUNK), F32)
            dmsum[...] = jnp.zeros((CHUNK, D), F32)

        dout_t = dout_ref[...]
        df = _post_bwd(dout_t, f_ref[...], vec_ref, acc_ref, 1.0)
        dfb = df.astype(BF)
        dfb_ref[...] = dfb
        h, xn, r, _ = _pre_fwd(x_ref[...], vec_ref)
        hb_ref[...] = h.astype(BF)
        u = sv_ref[:, 0:D].astype(F32)
        mix = sv_ref[:, D:2 * D].astype(F32)
        vhat = sv_ref[:, 4 * D:5 * D].astype(F32)
        rstd = rstd_ref[:, 0:1]
        dp = _dot_nt(dfb, wout_ref[...])
        du = dp * mix
        dmix = dp * u
        dmb = dmix.astype(BF)
        for nn in range(tm // CHUNK):
            rows = slice(CHUNK * nn, CHUNK * (nn + 1))
            dmsum[...] += dmix[rows, :]
            for hh in range(H_C):
                cols = slice(CHUNK * hh, CHUNK * (hh + 1))
                dvn_sc[rows, cols] = _dot(wsmt[hh], dmb[rows, cols])
                dws_ref[hh] += _dot_nt(dmb[rows, cols], sv_ref[rows, 5 * D + CHUNK * hh:5 * D + CHUNK * (hh + 1)])
        dvn = dvn_sc[...]
        acc_ref[5:6, :] += _colsum(dvn * vhat)
        acc_ref[6:7, :] += _colsum(dvn)
        dvhat = dvn * cvec_ref[0:1, :]
        dv = rstd * (dvhat - jnp.mean(dvhat, axis=-1, keepdims=True)
                     - vhat * jnp.mean(dvhat * vhat, axis=-1, keepdims=True))
        dpre_u = du * sv_ref[:, 2 * D:3 * D].astype(F32)
        dpre_v = dv * sv_ref[:, 3 * D:4 * D].astype(F32)
        dbin_ref[0:1, 0:D] += _colsum(dpre_u)
        dbin_ref[0:1, D:2 * D] += _colsum(dpre_v)
        dpre_ref[:, 0:D] = dpre_u.astype(BF)
        dpre_ref[:, D:2 * D] = dpre_v.astype(BF)
        dh = _dot_nt(dpre_ref[...], win_v[...])
        dx_ref[...] = _pre_bwd(dout_t, dh, xn, r, vec_ref, acc_ref)

        @pl.when(i == nt - 1)
        def _():
            _finish_acc(acc_ref, vec_ref, 1.0)
            row = lax.broadcasted_iota(jnp.int32, (CHUNK, CHUNK), 0)
            col = lax.broadcasted_iota(jnp.int32, (CHUNK, CHUNK), 1)
            for hh in range(H_C):
                dws_ref[hh] = jnp.where(row >= col, dws_ref[hh], 0.0)
                dbst_ref[:, hh:hh + 1] = jnp.sum(dmsum[:, CHUNK * hh:CHUNK * (hh + 1)], axis=1, keepdims=True)

    return pl.pallas_call(
        body, grid=(nt,),
        in_specs=[_tile(tm, D), _tile(tm, D), _tile(tm, D), _tile(tm, C_SAVED), _tile(tm, 128), _full((8, D)),
                  _full((N_DEV, D, 256)), _full((8, D)), _full((H_C, CHUNK, CHUNK)), _full((CHUNK, H_C)), _full((D, D))],
        out_specs=[_tile(tm, D), _tile(tm, 2 * D), _tile(tm, D), _tile(tm, D), _full((8, D)),
                   _full((8, 2 * D)), _full((H_C, CHUNK, CHUNK)), _full((CHUNK, H_C))],
        out_shape=[jax.ShapeDtypeStruct((t_len, D), F32), jax.ShapeDtypeStruct((t_len, 2 * D), BF),
                   jax.ShapeDtypeStruct((t_len, D), BF),
                   jax.ShapeDtypeStruct((t_len, D), BF), jax.ShapeDtypeStruct((8, D), F32),
                   jax.ShapeDtypeStruct((8, 2 * D), F32), jax.ShapeDtypeStruct((H_C, CHUNK, CHUNK), F32),
                   jax.ShapeDtypeStruct((CHUNK, H_C), F32)],
        scratch_shapes=[pltpu.VMEM((H_C, CHUNK, CHUNK), BF), pltpu.VMEM((H_C, CHUNK, CHUNK), BF),
                        pltpu.VMEM((tm, D), F32), pltpu.VMEM((CHUNK, D), F32), pltpu.VMEM((D, 2 * D), BF)],
        compiler_params=_params(), name=name,
    )(dout, x, fpre, saved, rstd_b, vec, w_in, cvec, ws, bst, w_out)


def _gmean(x, g_ref):
    hi = x.astype(BF)
    lo = (x - hi.astype(F32)).astype(BF)
    return _dot(hi, g_ref[...]) + _dot(lo, g_ref[...])


def _log_sigmoid(lam):
    e = jnp.exp(-jnp.abs(lam))
    log1p = jnp.where(e < 1e-2, e * (1.0 - e * (0.5 - e * (1.0 / 3.0 - 0.25 * e))), jnp.log(1.0 + e))
    return jnp.minimum(lam, 0.0) - log1p


def _neg_expm1(y):
    series = -(y * (1.0 + y * (0.5 + y * (1.0 / 6.0 + y * (1.0 / 24.0 + y * (1.0 / 120.0))))))
    return jnp.where(y > -0.1, series, 1.0 - jnp.exp(y))


def _rows_from(e, off, tm):
    return e[off:off + tm, :] if off % 8 == 0 else pltpu.roll(e, e.shape[0] - off, 0)[0:tm, :]


def _conv_causal(ext, taps_ref, bias, k_taps, halo, tm):
    e = ext[...]
    acc = bias
    for k in range(k_taps):
        acc = acc + taps_ref[k:k + 1, :] * _rows_from(e, halo - k_taps + 1 + k, tm)
    return acc


def _build_shifted(sh_ref, e, n_rows):
    sh_ref[0] = e
    for r in range(1, 8):
        sh_ref[r] = pltpu.roll(e, n_rows - r, 0)


def _shifted_rows(sh_ref, off, tm):
    base = off - off % 8
    return sh_ref[off % 8, base:base + tm, :]


def _scan(a, u, tm, reverse):
    row = lax.broadcasted_iota(jnp.int32, (tm, W_A), 0)
    d = 1
    while d < tm:
        if reverse:
            keep = row < tm - d
            shift = tm - d
        else:
            keep = row >= d
            shift = d
        a_sh = jnp.where(keep, pltpu.roll(a, shift, 0), 1.0)
        u_sh = jnp.where(keep, pltpu.roll(u, shift, 0), 0.0)
        u = a * u_sh + u
        a = a * a_sh
        d *= 2
    return a, u


def _a_gates(xc, cv_ref, wr_ref, wi_ref):
    xcb = xc.astype(BF)
    r = _sigmoid(_dot(xcb, wr_ref[...]) + cv_ref[5:6, :])
    ig = _sigmoid(_dot(xcb, wi_ref[...]) + cv_ref[6:7, :])
    ls = _log_sigmoid(cv_ref[7:8, :])
    la = LRU_C * r * ls
    a = jnp.exp(la)
    m = jnp.sqrt(_neg_expm1(2.0 * la))
    return xcb, r, ig, ls, a, m


def _b_norm(vc, cv_ref, g_ref):
    mu = _gmean(vc, g_ref)
    dv = vc - mu
    rstd = lax.rsqrt(_gmean(dv * dv, g_ref) + EPS)
    vhat = dv * rstd
    vln = vhat * cv_ref[9:10, :] + cv_ref[10:11, :]
    return rstd, vhat, vln


def _mixab_fwd(x, vec, w_in, cv, w31, wr, wi, gmat, w_out, name):
    t_len = x.shape[0]
    tm = min(TM_MIX, t_len)

    def body(x_ref, vec_ref, win_ref, cv_ref, w31_ref, wr_ref, wi_ref, g_ref, wout_ref,
             xo_ref, f_ref, z_ref, hs_ref, cvs_ref, ext_a, ext_b, hc, shifted, win_v, yab):
        @pl.when(pl.program_id(0) == 0)
        def _():
            ext_a[0:HALO_A, :] = jnp.zeros((HALO_A, W_A), F32)
            ext_b[0:HALO_B, :] = jnp.zeros((HALO_B, W_B), F32)
            hc[...] = jnp.zeros((8, W_A), F32)
            for j in range(N_DEV):
                win_v[:, 256 * j:256 * (j + 1)] = win_ref[j]

        x_t = x_ref[...]
        h, _, _, _ = _pre_fwd(x_t, vec_ref)
        hb = h.astype(BF)
        z_ref[...] = _dot(hb, win_v[...])
        ext_a[HALO_A:HALO_A + tm, :] = z_ref[:, W_A:2 * W_A]
        xc = _conv_causal(ext_a, cv_ref, cv_ref[4:5, :], CONV_A, HALO_A, tm)
        ext_a[0:HALO_A, :] = ext_a[tm:tm + HALO_A, :]
        cvs_ref[:, 0:W_A] = xc
        _, r, ig, _, a, m = _a_gates(xc, cv_ref, wr_ref, wi_ref)
        for col, val in enumerate((r, ig, a, m)):
            cvs_ref[:, W_A * (2 + col):W_A * (3 + col)] = val
        a_cum, hloc = _scan(a, m * ig * xc, tm, False)
        hs = hloc + a_cum * hc[0:1, :]
        hs_ref[...] = hs
        hc[0:1, :] = hs[tm - 1:tm, :]
        ag = z_ref[:, 0:W_A]
        gel, tg = _gelu(ag)
        cvs_ref[:, 7 * W_A:8 * W_A] = gel
        cvs_ref[:, 8 * W_A:9 * W_A] = _gelu_grad(ag, tg)
        ya = hs * gel
        ext_b[HALO_B:HALO_B + tm, :] = z_ref[:, 2 * W_A:2 * W_A + W_B] * _sigmoid(z_ref[:, 2 * W_A + W_B:2 * W_A + 2 * W_B])
        _build_shifted(shifted, ext_b[...], tm + HALO_B)
        vc = cv_ref[8:9, :] + w31_ref[0:1, :] * _shifted_rows(shifted, HALO_B - CONV_B + 1, tm)
        for k in range(1, CONV_B):
            vc = vc + w31_ref[k:k + 1, :] * _shifted_rows(shifted, HALO_B - CONV_B + 1 + k, tm)
        ext_b[0:HALO_B, :] = ext_b[tm:tm + HALO_B, :]
        rstd, vhat, vln = _b_norm(vc, cv_ref, g_ref)
        cvs_ref[:, W_A:W_A + W_B] = vhat
        cvs_ref[:, 6 * W_A:7 * W_A] = rstd
        yb = vln * _sigmoid(vln)
        yab[:, 0:W_A] = ya.astype(BF)
        yab[:, W_A:W_A + W_B] = yb.astype(BF)
        fpre = _dot(yab[...], wout_ref[...])
        f_ref[...] = fpre
        xo_ref[...] = _post_fwd(x_t, fpre, vec_ref, 1.0)

    return pl.pallas_call(
        body, grid=(t_len // tm,),
        in_specs=[_tile(tm, D), _full((8, D)), _full((N_DEV, D, 256)), _full((16, W_A)), _full((32, W_B)),
                  _full((W_A, W_A)), _full((W_A, W_A)), _full((W_B, W_B)), _full((D, D))],
        out_specs=[_tile(tm, D), _tile(tm, D), _tile(tm, 2 * D), _tile(tm, W_A), _tile(tm, AB_SAVED)],
        out_shape=[jax.ShapeDtypeStruct((t_len, D), F32), jax.ShapeDtypeStruct((t_len, D), F32),
                   jax.ShapeDtypeStruct((t_len, 2 * D), F32), jax.ShapeDtypeStruct((t_len, W_A), F32),
                   jax.ShapeDtypeStruct((t_len, AB_SAVED), F32)],
        scratch_shapes=[pltpu.VMEM((tm + HALO_A, W_A), F32), pltpu.VMEM((tm + HALO_B, W_B), F32), pltpu.VMEM((8, W_A), F32),
                        pltpu.VMEM((8, tm + HALO_B, W_B), F32), pltpu.VMEM((D, 2 * D), BF), pltpu.VMEM((tm, D), BF)],
        compiler_params=_params(), name=name,
    )(x, vec, w_in, cv, w31, wr, wi, gmat, w_out)


def _mixab_bwd(dout, x, fpre, z, cvs, hs, vec, w_in, cv, w31, wr, wi, gmat, w_out, name, comm=()):
    t_len = x.shape[0]
    tm = min(TM_MIX, t_len)
    nt = t_len // tm

    def rev(i):
        return nt - 1 - i

    def rtile(ncol):
        return pl.BlockSpec((tm, ncol), lambda i: (rev(i), 0))

    def body(dout_ref, x_ref, f_ref, z_ref, cvs_ref, hs_ref, hsp_ref, vec_ref, win_ref, cv_ref, w31_ref, wr_ref, wi_ref,
             g_ref, wout_ref,
             dx_ref, dz_ref, yab_ref, hb_ref, dfb_ref, xcb_ref, dri_ref, acc_ref, accs_ref, dw31_ref,
             ext_h, ext_dx, ext_dv, carry, shifted, win_v):
        i = pl.program_id(0)
        has_prev = (rev(i) > 0).astype(F32)

        @pl.when(i == 0)
        def _():
            for j in range(N_DEV):
                win_v[:, 256 * j:256 * (j + 1)] = win_ref[j]
            acc_ref[...] = jnp.zeros((8, D), F32)
            accs_ref[...] = jnp.zeros((16, W_A), F32)
            dw31_ref[...] = jnp.zeros((32, W_B), F32)
            ext_dx[tm:tm + HALO_A, :] = jnp.zeros((HALO_A, W_A), F32)
            ext_dv[tm:tm + HALO_B, :] = jnp.zeros((HALO_B, W_B), F32)
            carry[...] = jnp.zeros((8, W_A), F32)

        dout_t = dout_ref[...]
        df = _post_bwd(dout_t, f_ref[...], vec_ref, acc_ref, 1.0)
        dfb = df.astype(BF)
        dfb_ref[...] = dfb
        h, xn, r_x, n = _pre_fwd(x_ref[...], vec_ref)
        hb_ref[...] = h.astype(BF)

        ag = z_ref[:, 0:W_A]
        ax = z_ref[:, W_A:2 * W_A]
        bv = z_ref[:, 2 * W_A:2 * W_A + W_B]
        sg = _sigmoid(z_ref[:, 2 * W_A + W_B:2 * W_A + 2 * W_B])
        vv = bv * sg
        xc = cvs_ref[:, 0:W_A]
        r, ig, a, m = (cvs_ref[:, W_A * (2 + col):W_A * (3 + col)] for col in range(4))
        ls = _log_sigmoid(cv_ref[7:8, :])
        xcb_ref[...] = xc.astype(BF)
        hs_t = hs_ref[...]
        ext_h[0:8, :] = hsp_ref[...] * has_prev
        ext_h[8:8 + tm, :] = hs_t
        hprev = ext_h[7:7 + tm, :]
        gel = cvs_ref[:, 7 * W_A:8 * W_A]
        vhat = cvs_ref[:, W_A:W_A + W_B]
        rstd = cvs_ref[:, 6 * W_A:7 * W_A]
        vln = vhat * cv_ref[9:10, :] + cv_ref[10:11, :]
        sv = _sigmoid(vln)
        yab_ref[:, 0:W_A] = (hs_t * gel).astype(BF)
        yab_ref[:, W_A:W_A + W_B] = (vln * sv).astype(BF)

        dya = _dot_nt(dfb, wout_ref[0:W_A, :])
        dyb = _dot_nt(dfb, wout_ref[W_A:W_A + W_B, :])

        dag = dya * hs_t * cvs_ref[:, 8 * W_A:9 * W_A]
        row = lax.broadcasted_iota(jnp.int32, (tm, W_A), 0)
        last = row == tm - 1
        a_next = jnp.where(last, 1.0, pltpu.roll(a, tm - 1, 0))
        u0 = dya * gel + jnp.where(last, carry[0:1, :], 0.0)
        _, dhs = _scan(a_next, u0, tm, True)
        carry[0:1, :] = a[0:1, :] * dhs[0:1, :]
        da = dhs * hprev
        dm = dhs * ig * xc
        di = dhs * m * xc
        dxc = dhs * m * ig
        dla = da * a - dm * (a * a) / m
        accs_ref[7:8, :] += _colsum(dla * r) * (LRU_C * _sigmoid(-cv_ref[7:8, :]))
        drp = (dla * (LRU_C * ls)) * r * (1.0 - r)
        dip = di * ig * (1.0 - ig)
        accs_ref[5:6, :] += _colsum(drp)
        accs_ref[6:7, :] += _colsum(dip)
        drpb = drp.astype(BF)
        dipb = dip.astype(BF)
        dri_ref[:, 0:W_A] = drpb
        dri_ref[:, W_A:2 * W_A] = dipb
        dxc = dxc + _dot_nt(drpb, wr_ref[...]) + _dot_nt(dipb, wi_ref[...])
        accs_ref[4:5, :] += _colsum(dxc)
        ext_dx[0:tm, :] = dxc
        e_dx = ext_dx[...]
        dax = jnp.zeros((tm, W_A), F32)
        for k in range(CONV_A):
            ahead = _rows_from(e_dx, CONV_A - 1 - k, tm)
            accs_ref[k:k + 1, :] += _colsum(ax * ahead)
            dax = dax + cv_ref[k:k + 1, :] * ahead
        ext_dx[tm:tm + HALO_A, :] = dxc[0:HALO_A, :]

        dvln = dyb * (sv * (1.0 + vln * (1.0 - sv)))
        accs_ref[9:10, :] += _colsum(dvln * vhat)
        accs_ref[10:11, :] += _colsum(dvln)
        dvhat = dvln * cv_ref[9:10, :]
        dvc = rstd * (dvhat - _gmean(dvhat, g_ref) - vhat * _gmean(dvhat * vhat, g_ref))
        accs_ref[8:9, :] += _colsum(dvc)
        ext_dv[0:tm, :] = dvc
        _build_shifted(shifted, ext_dv[...], tm + HALO_B)
        dvv = jnp.zeros((tm, W_B), F32)
        for k in range(CONV_B):
            ahead = _shifted_rows(shifted, CONV_B - 1 - k, tm)
            dw31_ref[k:k + 1, :] += _colsum(vv * ahead)
            dvv = dvv + w31_ref[k:k + 1, :] * ahead
        ext_dv[tm:tm + HALO_B, :] = dvc[0:HALO_B, :]

        dz_ref[:, 0:W_A] = dag.astype(BF)
        dz_ref[:, W_A:2 * W_A] = dax.astype(BF)
        dz_ref[:, 2 * W_A:2 * W_A + W_B] = (dvv * sg).astype(BF)
        dz_ref[:, 2 * W_A + W_B:2 * W_A + 2 * W_B] = (dvv * vv * (1.0 - sg)).astype(BF)
        dh = _dot_nt(dz_ref[...], win_v[...])
        dx_ref[...] = _pre_bwd(dout_t, dh, xn, r_x, vec_ref, acc_ref)

        @pl.when(i == nt - 1)
        def _():
            _finish_acc(acc_ref, vec_ref, 1.0)

    hsp_spec = pl.BlockSpec((8, W_A), lambda i: (jnp.maximum(rev(i) * (tm // 8) - 1, 0), 0))
    nc = len(comm)
    outs = pl.pallas_call(
        _hosted(body, 15, 10, 6, nc, "scatter", (nt,)), grid=(nt,),
        in_specs=[rtile(D), rtile(D), rtile(D), rtile(2 * D), rtile(AB_SAVED), rtile(W_A), hsp_spec, _full((8, D)),
                  _full((N_DEV, D, 256)), _full((16, W_A)), _full((32, W_B)), _full((W_A, W_A)), _full((W_A, W_A)),
                  _full((W_B, W_B)), _full((D, D))] + [_any()] * nc,
        out_specs=[rtile(D), rtile(2 * D), rtile(D), rtile(D), rtile(D), rtile(W_A), rtile(2 * W_A), _full((8, D)),
                   _full((16, W_A)), _full((32, W_B))] + [_any()] * nc,
        out_shape=[jax.ShapeDtypeStruct((t_len, D), F32), jax.ShapeDtypeStruct((t_len, 2 * D), BF),
                   jax.ShapeDtypeStruct((t_len, D), BF), jax.ShapeDtypeStruct((t_len, D), BF),
                   jax.ShapeDtypeStruct((t_len, D), BF), jax.ShapeDtypeStruct((t_len, W_A), BF),
                   jax.ShapeDtypeStruct((t_len, 2 * W_A), BF), jax.ShapeDtypeStruct((8, D), F32),
                   jax.ShapeDtypeStruct((16, W_A), F32), jax.ShapeDtypeStruct((32, W_B), F32)] + _exchange_shapes(comm, "scatter"),
        scratch_shapes=[pltpu.VMEM((tm + 8, W_A), F32), pltpu.VMEM((tm + HALO_A, W_A), F32),
                        pltpu.VMEM((tm + HALO_B, W_B), F32), pltpu.VMEM((8, W_A), F32),
                        pltpu.VMEM((8, tm + HALO_B, W_B), F32), pltpu.VMEM((D, 2 * D), BF)]
        + (_exchange_scratch(nc) if nc else []),
        compiler_params=_params(), name=name,
    )(dout, x, fpre, z, cvs, hs, hs, vec, w_in, cv, w31, wr, wi, gmat, w_out, *comm)
    return outs[:10], list(outs[10:])


def _vec(p, l, j):
    return jnp.concatenate([p["mod"][l, j], p["norm_pre"][l, j][None], p["norm_post"][l, j][None], jnp.zeros((3, D), F32)], 0)


def _ab_consts(p):
    gw = p["a_gate_w"]
    gb = p["a_gate_b"]
    half = W_A // 8
    eye = jnp.eye(8, dtype=F32)[:, None, :, None]

    def block_diag(blocks):
        return (blocks[:, :, None, :] * eye).reshape(W_A, W_A).astype(BF)

    wr = block_diag(gw[:, :, 0:half])
    wi = block_diag(gw[:, :, half:2 * half])
    rows = [p["a_conv_w"], p["a_conv_b"][None], gb[:, 0:half].reshape(1, W_A), gb[:, half:2 * half].reshape(1, W_A),
            p["a_lam"][None], p["b_conv_b"][None], p["b_norm_g"][None], p["b_norm_b"][None], jnp.zeros((5, W_A), F32)]
    cv = jnp.concatenate(rows, 0)
    w31 = jnp.concatenate([p["b_conv_w"], jnp.zeros((1, W_B), F32)], 0)
    grp = jnp.arange(W_B) // (W_B // 8)
    gmat = ((grp[:, None] == grp[None, :]).astype(F32) / (W_B // 8)).astype(BF)
    return cv, w31, wr, wi, gmat


SUBLAYERS = ("f0", "ab", "f1", "f2", "c", "f3")


def _local_step(x, tgt, p, plan=None):
    g = {}
    saved = []
    cur = x
    wsets = dict(p["wsets"])
    ab_c = _ab_consts(p)
    c_cvec = jnp.concatenate([p["c_norm_g"][None], p["c_norm_b"][None], jnp.zeros((6, D), F32)], 0)
    c_bst = jnp.transpose(p["c_b_s"])
    c_bin = p["c_b_in"][None]
    for s_idx, wname in enumerate(SUBLAYERS):
        l, j = divmod(s_idx, 3)
        vec = _vec(p, l, j)
        tag = f"l{l}s{j}"
        if j != 1:
            names = plan["gather"].get(s_idx, []) if plan else []
            comm = [a for nm in names for a in plan["shards"][nm]]
            res, got = _ffn_fwd(cur, vec, *wsets[wname], 0.5, "ffn_fwd_" + tag, comm=comm, tgt=tgt if s_idx == 5 else None)
            nxt, fpre, jac, s_act = res[:4]
            for k, nm in enumerate(names):
                wsets[nm] = list(got[2 * k:2 * k + 2])
            saved.append((cur, fpre, jac, s_act, vec))
            if s_idx == 5:
                loss_blk = res[4]
        elif l == 0:
            w_in, w_out = wsets[wname]
            nxt, fpre, z, hs, cvs = _mixab_fwd(cur, vec, w_in, *ab_c, w_out.reshape(D, D), "mixab_fwd_" + tag)
            saved.append((cur, fpre, z, hs, cvs, vec))
        else:
            w_in, w_out = wsets[wname]
            nxt, fpre, c_saved, c_rstd, pb = _mixc_fwd(cur, vec, w_in, c_bin, c_cvec, p["c_w_s"], c_bst, w_out.reshape(D, D),
                                                       "mixc_fwd_" + tag)
            saved.append((cur, fpre, c_saved, c_rstd, pb, vec))
        cur = nxt
    dcur = cur
    accs, pending, recv = {}, {}, {}

    def take(host):
        keys = plan["scatter"].get(host, []) if plan else []
        return keys, [pending.pop(k) for k in keys]

    def put(keys, got):
        recv.update(zip(keys, got))

    for s_idx in reversed(range(6)):
        wname = SUBLAYERS[s_idx]
        l, j = divmod(s_idx, 3)
        tag = f"l{l}s{j}"
        sv = saved[s_idx]
        if j != 1:
            keys, comm = take("ffn_bwd_" + tag)
            xin, fpre, jac, s, vec = sv
            (dcur, dgu, hb, dfb, acc), got = _ffn_bwd(dcur, xin, fpre, jac, vec, *wsets[wname], 0.5, "ffn_bwd_" + tag, comm=comm)
            put(keys, got)
            keys, comm = take("wgrad_w13_" + tag)
            dw13 = _wgrad(hb, dgu, 2, D, D_FF, "share", "cols", D, BF, "wgrad_w13_" + tag, tk=TK_WGRAD, comm=comm,
                          slots=(_w13_slots, (4, D, FF_PAD)))
            if keys:
                dw13, got = dw13
                put(keys, got)
            pending[wname + ".0"] = dw13
            keys, comm = take("wgrad_w2_" + tag)
            dw2 = _wgrad(s, dfb, 1, D_FF, D, "share", "share", D_FF, BF, "wgrad_w2_" + tag, tk=TK_WGRAD, comm=comm,
                         slots=(_w2_slots, (N_DEV, W2_SHARD, D)))
            if keys:
                dw2, got = dw2
                put(keys, got)
            pending[wname + ".1"] = dw2
        elif l == 0:
            xin, fpre, z, hs, cvs, vec = sv
            w_in, w_out = wsets[wname]
            keys, comm = take("mixab_bwd_" + tag)
            (dcur, dz, yab, hb, dfb, xcb, dri, acc, accs_ab, dw31), got = _mixab_bwd(
                dcur, xin, fpre, z, cvs, hs, vec, w_in, *ab_c, w_out.reshape(D, D), "mixab_bwd_" + tag, comm=comm)
            put(keys, got)
            d_in = _wgrad(hb, dz, 1, D, 2 * D, "share", "share", D, BF, "wgrad_ab_in", tk=TK_WGRAD, col_slots=N_DEV)
            d_out = _wgrad(yab, dfb, 1, D, D, "share", "share", D, BF, "wgrad_ab_out")
            pending[wname + ".0"], pending[wname + ".1"] = d_in, d_out.reshape(N_DEV, D // N_DEV, D)
            g["gate"] = _wgrad(xcb, dri, 1, W_A, 2 * W_A, "share", "share", W_A, F32, "wgrad_gate")
            g["accs_ab"] = accs_ab
            g["dw31"] = dw31
        else:
            xin, fpre, c_saved, c_rstd, pb, vec = sv
            w_in, w_out = wsets[wname]
            dcur, dpre, hb, dfb, acc, dbin, dws, dbst = _mixc_bwd(
                dcur, xin, fpre, c_saved, c_rstd, vec, w_in, c_cvec, p["c_w_s"], c_bst, w_out.reshape(D, D), "mixc_bwd_" + tag)
            d_in = _wgrad(hb, dpre, 1, D, 2 * D, "share", "share", D, BF, "wgrad_c_in", tk=TK_WGRAD, col_slots=N_DEV)
            d_out = _wgrad(pb, dfb, 1, D, D, "share", "share", D, BF, "wgrad_c_out")
            pending[wname + ".0"], pending[wname + ".1"] = d_in, d_out.reshape(N_DEV, D // N_DEV, D)
            g["c_small"] = (acc, dbin, dws, dbst)
        accs[f"{l}{j}"] = acc
    g["accs"] = accs
    g["pending"] = pending
    g["recv"] = recv
    return loss_blk, dcur, g


def _exchange_ops(ins, outs, sems, mode, action):
    send_sems, recv_sems, loc_sems = sems
    n = len(ins)
    x, y, c = lax.axis_index("x"), lax.axis_index("y"), lax.axis_index("c")
    me = 4 * x + 2 * y + c

    def src(i, dev):
        return ins[i] if mode == "gather" else ins[i].at[dev]

    for i in range(n):
        cp = pltpu.make_async_copy(src(i, me), outs[i].at[me], loc_sems.at[i])
        if action == "start":
            cp.start()
        else:
            cp.wait()
    for mask in range(1, N_DEV):
        px = 1 - x if mask & 4 else x
        py = 1 - y if mask & 2 else y
        pc = 1 - c if mask & 1 else c
        peer = 4 * px + 2 * py + pc
        for i in range(n):
            k = i * (N_DEV - 1) + mask - 1
            cp = pltpu.make_async_remote_copy(
                src_ref=src(i, peer), dst_ref=outs[i].at[me if action == "start" else peer],
                send_sem=send_sems.at[k], recv_sem=recv_sems.at[k],
                device_id=(px, py, pc), device_id_type=pl.DeviceIdType.MESH)
            if action == "start":
                cp.start()
            else:
                cp.wait()


def _exchange_scratch(n):
    return [pltpu.SemaphoreType.DMA((n * (N_DEV - 1),)), pltpu.SemaphoreType.DMA((n * (N_DEV - 1),)),
            pltpu.SemaphoreType.DMA((n,))]


def _exchange_shapes(arrays, mode):
    return [jax.ShapeDtypeStruct(((N_DEV,) + a.shape) if mode == "gather" else a.shape, a.dtype) for a in arrays]


def _exchange(arrays, mode, name):
    n = len(arrays)

    def body(*refs):
        ins, outs, sems = refs[:n], refs[n:2 * n], refs[2 * n:]
        _exchange_ops(ins, outs, sems, mode, "start")
        _exchange_ops(ins, outs, sems, mode, "wait")

    return pl.pallas_call(
        body, in_specs=[pl.BlockSpec(memory_space=pl.ANY)] * n, out_specs=[pl.BlockSpec(memory_space=pl.ANY)] * n,
        out_shape=_exchange_shapes(arrays, mode), scratch_shapes=_exchange_scratch(n), name=name,
    )(*arrays)


def _gather_two_level(arrays, name):
    n = len(arrays)
    per = N_DEV - 1

    def body(*refs):
        ins, outs = refs[:n], refs[n:2 * n]
        send_sems, recv_sems, loc_sems = refs[2 * n:]
        x, y, c = lax.axis_index("x"), lax.axis_index("y"), lax.axis_index("c")
        me, sibling = (x, y, c), (x, y, 1 - c)
        chips = [(1 - x, y), (x, 1 - y), (1 - x, 1 - y)]

        def rows(i, dev):
            return outs[i].at[4 * dev[0] + 2 * dev[1] + dev[2]]

        def copy(i, k, block, to, src=None):
            return pltpu.make_async_remote_copy(
                src_ref=rows(i, block) if src is None else src, dst_ref=rows(i, block),
                send_sem=send_sems.at[i * per + k], recv_sem=recv_sems.at[i * per + k],
                device_id=to, device_id_type=pl.DeviceIdType.MESH)

        mine = [pltpu.make_async_copy(ins[i], rows(i, me), loc_sems.at[i]) for i in range(n)]
        for cp in mine:
            cp.start()
        first = []
        for i in range(n):
            first.append(copy(i, 0, me, sibling, src=ins[i]))
            first += [copy(i, 1 + j, me, (*chip, c), src=ins[i]) for j, chip in enumerate(chips)]
        for cp in first:
            cp.start()
        passed = []
        for j, chip in enumerate(chips):
            for i in range(n):
                copy(i, 1 + j, (*chip, c), me).wait_recv()
                fwd = copy(i, 4 + j, (*chip, c), sibling)
                fwd.start()
                passed.append(fwd)
        for i in range(n):
            copy(i, 0, sibling, me).wait_recv()
            for j, chip in enumerate(chips):
                copy(i, 4 + j, (*chip, 1 - c), me).wait_recv()
        for cp in first + passed:
            cp.wait_send()
        for cp in mine:
            cp.wait()

    return pl.pallas_call(
        body, in_specs=[pl.BlockSpec(memory_space=pl.ANY)] * n, out_specs=[pl.BlockSpec(memory_space=pl.ANY)] * n,
        out_shape=_exchange_shapes(arrays, "gather"), scratch_shapes=_exchange_scratch(n), name=name,
    )(*arrays)


def _sum_slots(a, name):
    def body(a_ref, o_ref):
        acc = a_ref[0]
        for s in range(1, N_DEV):
            acc = acc + a_ref[s]
        o_ref[...] = acc

    return pl.pallas_call(body, out_shape=jax.ShapeDtypeStruct(a.shape[1:], F32), name=name,
                          compiler_params=pltpu.CompilerParams(vmem_limit_bytes=VMEM_LIMIT))(a)


def _pack(pieces, mult):
    flat = jnp.concatenate([q.reshape(-1).astype(F32) for q in pieces])
    size = -(-flat.shape[0] // mult) * mult
    return jnp.pad(flat, (0, size - flat.shape[0])).reshape(size // 128, 128)


def _unpack(flat, shapes):
    out, off = [], 0
    for shp in shapes:
        size = math.prod(shp)
        out.append(flat[..., off:off + size].reshape(flat.shape[:-1] + tuple(shp)))
        off += size
    return out


def _mod_part(c_all, ada_w, ada_b_mine, name):
    cols = ada_w.shape[-1]

    def body(c_ref, w_ref, b_ref, o_ref):
        cv = c_ref[...]
        ca = cv * _sigmoid(cv)
        for l in range(2):
            o_ref[l] = jnp.dot(ca, w_ref[l], preferred_element_type=F32, precision=lax.Precision.HIGHEST) + b_ref[l:l + 1, :]

    return pl.pallas_call(body, out_shape=jax.ShapeDtypeStruct((2, N_DEV, cols), F32), name=name,
                          compiler_params=pltpu.CompilerParams(vmem_limit_bytes=VMEM_LIMIT))(c_all, ada_w, ada_b_mine)


def _ada_w_grad(c_all_t, dmod_mine, name):
    cols = dmod_mine.shape[-1]

    def body(ct_ref, d_ref, o_ref):
        cv = ct_ref[...]
        ca = cv * _sigmoid(cv)
        for l in range(2):
            acc = ca[:, 0:1] * d_ref[l, 0:1, :]
            for b in range(1, N_DEV):
                acc = acc + ca[:, b:b + 1] * d_ref[l, b:b + 1, :]
            o_ref[l] = acc

    return pl.pallas_call(body, out_shape=jax.ShapeDtypeStruct((2, D, cols), F32), name=name,
                          compiler_params=pltpu.CompilerParams(vmem_limit_bytes=VMEM_LIMIT))(c_all_t, dmod_mine)


def _adamw_math(w, g, m, v):
    m2 = ADAM_B1 * m + (1.0 - ADAM_B1) * g
    v2 = ADAM_B2 * v + (1.0 - ADAM_B2) * (g * g)
    m_hat = m2 / (1.0 - ADAM_B1 ** ADAM_STEP)
    v_hat = v2 / (1.0 - ADAM_B2 ** ADAM_STEP)
    delta = -ADAM_LR * (m_hat / (jnp.sqrt(v_hat) + ADAM_EPS) + ADAM_WD * w)
    return delta, m2, v2


def _adamw_big(w, g, m, v, name):
    n_l, rows, cols = w.shape
    parts = list(g) if isinstance(g, (list, tuple)) else None
    sizes = (512, 352, 256, 128, 64, 32, 16, 8) if parts is None or len(parts) == 1 else (176, 128, 64, 32, 16, 8)
    br = next(b for b in sizes if rows % b == 0)
    nr = rows // br

    def body(*refs):
        w_ref, g_refs, (m_ref, v_ref, go_ref, d_ref, mo_ref, vo_ref) = refs[0], refs[1:-6], refs[-6:]

        def update(gsum):
            go_ref[...] = gsum
            d_ref[...], mo_ref[...], vo_ref[...] = _adamw_math(w_ref[...], gsum, m_ref[...], v_ref[...])

        if parts is None:
            update(g_refs[0][...])
        else:
            for l, g_ref in enumerate(g_refs):
                @pl.when(pl.program_id(0) == l)
                def _(g_ref=g_ref):
                    gsum = g_ref[0, :, 0:cols].astype(F32)
                    for s in range(1, N_DEV):
                        gsum = gsum + g_ref[s, :, 0:cols].astype(F32)
                    update(gsum)

    blk = pl.BlockSpec((None, br, cols), lambda l, i: (l, i, 0))
    if parts is None:
        g_specs, g_args = [blk], [g]
    else:
        def part_spec(l_mine, width):
            return pl.BlockSpec((N_DEV, br, width),
                                lambda l, i: (0, jnp.where(l < l_mine, 0, jnp.where(l == l_mine, i, nr - 1)), 0))
        g_specs, g_args = [part_spec(l, p.shape[-1]) for l, p in enumerate(parts)], parts
    shp = jax.ShapeDtypeStruct((n_l, rows, cols), F32)
    return pl.pallas_call(
        body, grid=(n_l, nr), in_specs=[blk] + g_specs + [blk, blk], out_specs=[blk] * 4, out_shape=[shp] * 4,
        compiler_params=pltpu.CompilerParams(dimension_semantics=("arbitrary", "arbitrary"), vmem_limit_bytes=VMEM_LIMIT),
        name=name)(w, *g_args, m, v)


def _adamw_small(ws, gs, ms, vs, name):
    n = len(ws)

    def body(*refs):
        for i in range(n):
            w_ref, g_ref, m_ref, v_ref = (refs[k * n + i] for k in range(4))
            d_ref, mo_ref, vo_ref = (refs[(4 + k) * n + i] for k in range(3))
            d_ref[...], mo_ref[...], vo_ref[...] = _adamw_math(w_ref[...], g_ref[...], m_ref[...], v_ref[...])

    shapes = [jax.ShapeDtypeStruct(w.shape, F32) for w in ws]
    outs = pl.pallas_call(body, out_shape=shapes * 3, name=name,
                          compiler_params=pltpu.CompilerParams(vmem_limit_bytes=VMEM_LIMIT))(*ws, *gs, *ms, *vs)
    return outs[:n], outs[n:2 * n], outs[2 * n:]


def _as2d(a):
    return a.reshape(-1, a.shape[-1])


def kernel(x, c, ada_w, ada_b, norm_pre, norm_post, ffn_w13, ffn_w2, ab_w_in, a_conv_w, a_conv_b, a_gate_w, a_gate_b, a_lam, b_conv_w, b_conv_b, b_norm_g, b_norm_b, ab_w_out, c_w_in, c_b_in, c_norm_g, c_norm_b, c_w_s, c_b_s, c_w_out, loss_target, m_ada_w, m_ada_b, m_norm_pre, m_norm_post, m_ffn_w13, m_ffn_w2, m_ab_w_in, m_a_conv_w, m_a_conv_b, m_a_gate_w, m_a_gate_b, m_a_lam, m_b_conv_w, m_b_conv_b, m_b_norm_g, m_b_norm_b, m_ab_w_out, m_c_w_in, m_c_b_in, m_c_norm_g, m_c_norm_b, m_c_w_s, m_c_b_s, m_c_w_out, v_ada_w, v_ada_b, v_norm_pre, v_norm_post, v_ffn_w13, v_ffn_w2, v_ab_w_in, v_a_conv_w, v_a_conv_b, v_a_gate_w, v_a_gate_b, v_a_lam, v_b_conv_w, v_b_conv_b, v_b_norm_g, v_b_norm_b, v_ab_w_out, v_c_w_in, v_c_b_in, v_c_norm_g, v_c_norm_b, v_c_w_s, v_c_b_s, v_c_w_out):
    me = 4 * lax.axis_index("x") + 2 * lax.axis_index("y") + lax.axis_index("c")
    weights = dict(ada_w=ada_w, ada_b=ada_b, norm_pre=norm_pre, norm_post=norm_post, ffn_w13=ffn_w13, ffn_w2=ffn_w2,
                   ab_w_in=ab_w_in, a_conv_w=a_conv_w, a_conv_b=a_conv_b, a_gate_w=a_gate_w, a_gate_b=a_gate_b, a_lam=a_lam,
                   b_conv_w=b_conv_w, b_conv_b=b_conv_b, b_norm_g=b_norm_g, b_norm_b=b_norm_b, ab_w_out=ab_w_out,
                   c_w_in=c_w_in, c_b_in=c_b_in, c_norm_g=c_norm_g, c_norm_b=c_norm_b, c_w_s=c_w_s, c_b_s=c_b_s, c_w_out=c_w_out)
    moms = dict(ada_w=m_ada_w, ada_b=m_ada_b, norm_pre=m_norm_pre, norm_post=m_norm_post, ffn_w13=m_ffn_w13, ffn_w2=m_ffn_w2,
                ab_w_in=m_ab_w_in, a_conv_w=m_a_conv_w, a_conv_b=m_a_conv_b, a_gate_w=m_a_gate_w, a_gate_b=m_a_gate_b,
                a_lam=m_a_lam, b_conv_w=m_b_conv_w, b_conv_b=m_b_conv_b, b_norm_g=m_b_norm_g, b_norm_b=m_b_norm_b,
                ab_w_out=m_ab_w_out, c_w_in=m_c_w_in, c_b_in=m_c_b_in, c_norm_g=m_c_norm_g, c_norm_b=m_c_norm_b,
                c_w_s=m_c_w_s, c_b_s=m_c_b_s, c_w_out=m_c_w_out)
    vars_ = dict(ada_w=v_ada_w, ada_b=v_ada_b, norm_pre=v_norm_pre, norm_post=v_norm_post, ffn_w13=v_ffn_w13, ffn_w2=v_ffn_w2,
                 ab_w_in=v_ab_w_in, a_conv_w=v_a_conv_w, a_conv_b=v_a_conv_b, a_gate_w=v_a_gate_w, a_gate_b=v_a_gate_b,
                 a_lam=v_a_lam, b_conv_w=v_b_conv_w, b_conv_b=v_b_conv_b, b_norm_g=v_b_norm_g, b_norm_b=v_b_norm_b,
                 ab_w_out=v_ab_w_out, c_w_in=v_c_w_in, c_b_in=v_c_b_in, c_norm_g=v_c_norm_g, c_norm_b=v_c_norm_b,
                 c_w_s=v_c_w_s, c_b_s=v_c_b_s, c_w_out=v_c_w_out)
    names = list(weights)

    w13s = ffn_w13.astype(BF).reshape(4, D, FF_SHARD)
    tail, blank = w13s[..., FF_MAIN:], jnp.zeros((4, D, FF_TAIL), BF)
    tail_tile = jnp.where(me % 2 == 1, jnp.concatenate([blank, tail], -1), jnp.concatenate([tail, blank], -1))
    w13b = jnp.concatenate([w13s[..., :FF_MAIN], tail_tile], -1)
    small_shapes = [(D,), (2, 3, 128), (2, 3, 128), (CONV_A, 64), (CONV_B, 64), (256,), (128,), (128,)]
    small = _pack([c, norm_pre, norm_post, a_conv_w, b_conv_w, c_b_in, c_norm_g, c_norm_b], 1024)
    w2b = ffn_w2.astype(BF).reshape(4, W2_SHARD, D)
    shards = {f"f{f}": [w13b[f], w2b[f]] for f in range(4)}
    shards["ab"] = [ab_w_in[0].astype(BF), ab_w_out[0].astype(BF)]
    shards["c"] = [c_w_in[0].astype(BF), c_w_out[0].astype(BF)]
    w13g0, w2g0, small_g = _gather_two_level(shards["f0"] + [small], "gather_first")
    plan = dict(shards=shards, gather={0: ["ab", "f1"], 2: ["f2"], 3: ["c", "f3"]},
                scatter={"ffn_bwd_l1s0": ["f3.0", "f3.1", "c.0", "c.1"], "ffn_bwd_l0s2": ["f2.0", "f2.1"],
                         "mixab_bwd_l0s1": ["f1.0", "f1.1"], "wgrad_w13_l0s0": ["ab.0", "ab.1"], "wgrad_w2_l0s0": ["f0.0"]})
    c_all, npre_g, npost_g, acw_g, bcw_g, cbin_g, cng_g, cnb_g = _unpack(small_g.reshape(N_DEV, -1), small_shapes)

    def cat_last(a):
        return jnp.moveaxis(a, 0, -2).reshape(a.shape[1:-1] + (N_DEV * a.shape[-1],))

    ada_b_mine = lax.dynamic_slice_in_dim(ada_b, me * ada_w.shape[-1], ada_w.shape[-1], axis=1)
    (mod_g,) = _exchange([_mod_part(c_all, ada_w, ada_b_mine, "mod_part")], "gather", "gather_mod")
    mod = cat_last(lax.dynamic_index_in_dim(mod_g, me, axis=2, keepdims=False)).reshape(2, 3, 3, D)

    p = dict(mod=mod, norm_pre=cat_last(npre_g), norm_post=cat_last(npost_g), wsets={"f0": [w13g0, w2g0]},
             a_conv_w=cat_last(acw_g), a_conv_b=a_conv_b[0], a_gate_w=a_gate_w[0], a_gate_b=a_gate_b[0], a_lam=a_lam[0],
             b_conv_w=cat_last(bcw_g), b_conv_b=b_conv_b[0], b_norm_g=b_norm_g[0], b_norm_b=b_norm_b[0],
             c_b_in=cat_last(cbin_g), c_norm_g=cat_last(cng_g), c_norm_b=cat_last(cnb_g), c_w_s=c_w_s[0], c_b_s=c_b_s[0])

    loss_blk, grad_x, g = _local_step(x[0], loss_target[0], p, plan)
    loss = lax.psum(loss_blk[0, 0], ("x", "y", "c"))

    accs = g["accs"]
    dmod = jnp.stack([jnp.stack([accs[f"{l}{j}"][0:3] for j in range(3)]) for l in range(2)])
    dnpre = jnp.stack([jnp.stack([accs[f"{l}{j}"][3] for j in range(3)]) for l in range(2)])
    dnpost = jnp.stack([jnp.stack([accs[f"{l}{j}"][4] for j in range(3)]) for l in range(2)])
    sab = g["accs_ab"]
    half = W_A // 8
    dgate = g["gate"][0]
    dgw = jnp.stack([jnp.concatenate([dgate[half * hh:half * (hh + 1), half * hh:half * (hh + 1)],
                                      dgate[half * hh:half * (hh + 1), W_A + half * hh:W_A + half * (hh + 1)]], axis=1)
                     for hh in range(8)])
    dgb = jnp.concatenate([sab[5].reshape(8, half), sab[6].reshape(8, half)], axis=1)
    c_acc, c_dbin, c_dws, c_dbst = g["c_small"]
    red_shapes = [(2, 9216), (2, 3, D), (2, 3, D), (CONV_A, W_A), (W_A,), (8, half, 2 * half), (8, 2 * half), (W_A,),
                  (CONV_B, W_B), (W_B,), (W_B,), (W_B,), (2 * D,), (D,), (D,), (H_C, CHUNK, CHUNK), (H_C, CHUNK)]
    red = _pack([dmod.reshape(2, 9216), dnpre, dnpost, sab[0:4], sab[4], dgw, dgb, sab[7], g["dw31"][0:CONV_B], sab[8],
                 sab[9], sab[10], c_dbin[0], c_acc[5], c_acc[6], c_dws, jnp.transpose(c_dbst)], N_DEV * 1024)
    left = sorted(g["pending"])
    red_r, *last_recv = _exchange([red.reshape(N_DEV, -1, 128)] + [g["pending"][k] for k in left], "scatter",
                                  "scatter_small_grads")
    red_all, dmod_all = _exchange([_sum_slots(red_r, "sum_small_grads"), dmod.reshape(-1, 128)], "gather", "gather_small_grads")
    red_sum = red_all.reshape(-1)
    (g_ada_b, g_npre, g_npost, g_acw, g_acb, g_agw, g_agb, g_alam, g_bcw, g_bcb, g_bng, g_bnb, g_cbin, g_cng, g_cnb,
     g_cws, g_cbs) = _unpack(red_sum, red_shapes)
    dmod_all = dmod_all.reshape(N_DEV, 2, 9216)
    ncol = ada_w.shape[-1]
    dmod_mine = jnp.moveaxis(lax.dynamic_slice_in_dim(dmod_all, me * ncol, ncol, axis=2), 0, 1)
    g_ada_w = _ada_w_grad(jnp.transpose(c_all), dmod_mine, "ada_w_grad")

    def mine(a, width):
        return lax.dynamic_slice_in_dim(a, me * width, width, axis=a.ndim - 1)

    small_grads = dict(
        ada_b=g_ada_b, norm_pre=mine(g_npre, 128), norm_post=mine(g_npost, 128), a_conv_w=mine(g_acw, 64)[None],
        a_conv_b=g_acb[None], a_gate_w=g_agw[None], a_gate_b=g_agb[None], a_lam=g_alam[None], b_conv_w=mine(g_bcw, 64)[None],
        b_conv_b=g_bcb[None], b_norm_g=g_bng[None], b_norm_b=g_bnb[None], c_b_in=mine(g_cbin, 256)[None],
        c_norm_g=mine(g_cng, 128)[None], c_norm_b=mine(g_cnb, 128)[None], c_w_s=g_cws[None], c_b_s=g_cbs[None])

    recv = dict(g["recv"])
    recv.update(zip(left, last_recv))
    big_partials = dict(ffn_w13=[recv[f"f{f}.0"] for f in range(4)], ffn_w2=[recv[f"f{f}.1"] for f in range(4)],
                        ab_w_in=[recv["ab.0"]], ab_w_out=[recv["ab.1"]], c_w_in=[recv["c.0"]], c_w_out=[recv["c.1"]],
                        ada_w=g_ada_w)

    grads, deltas, new_m, new_v = {}, {}, {}, {}

    def as3d(a):
        return a.reshape((-1,) + a.shape[-2:])

    for nm, gp in big_partials.items():
        shp = weights[nm].shape
        go, dl, mo, vo = _adamw_big(as3d(weights[nm]), gp, as3d(moms[nm]), as3d(vars_[nm]), "adamw_" + nm)
        grads[nm], deltas[nm], new_m[nm], new_v[nm] = (a.reshape(shp) for a in (go, dl, mo, vo))
    snames = list(small_grads)
    dls, mos, vos = _adamw_small([_as2d(weights[nm]) for nm in snames], [_as2d(small_grads[nm]) for nm in snames],
                                 [_as2d(moms[nm]) for nm in snames], [_as2d(vars_[nm]) for nm in snames], "adamw_small")
    for k, nm in enumerate(snames):
        shp = weights[nm].shape
        grads[nm] = small_grads[nm].reshape(shp)
        deltas[nm], new_m[nm], new_v[nm] = dls[k].reshape(shp), mos[k].reshape(shp), vos[k].reshape(shp)

    return (loss, grad_x[None], *[grads[nm] for nm in names], *[deltas[nm] for nm in names],
            *[new_m[nm] for nm in names], *[new_v[nm] for nm in names])
```
